```python
import math
import jax, jax.numpy as jnp
from jax import lax
import numpy as np

D_MODEL = 1024
BATCH = 2
SEQ = 8192
DEPTH = 1

ATTN_HEADS = 8
ATTN_HEAD_DIM = 64
ATTN_WIDTH = ATTN_HEADS * ATTN_HEAD_DIM
DILATED_PATTERNS = ((128, 1), (512, 4), (2048, 16))
REL_BUCKETS = 32
REL_MAX_DISTANCE = 2048
SSM_HEADS = 24
SSM_HEAD_DIM = 64
SSM_WIDTH = SSM_HEADS * SSM_HEAD_DIM
SSM_GROUPS = 4
SSM_HEADS_PER_GROUP = SSM_HEADS // SSM_GROUPS
SSM_STATE = 128
SSM_CONV = 4
SSM_CHUNK = 128
CONV_CH = SSM_WIDTH + 2 * SSM_GROUPS * SSM_STATE
MIX_WIDTH = ATTN_WIDTH + SSM_WIDTH
IN_PROJ_WIDTH = 3 * ATTN_WIDTH + SSM_WIDTH + CONV_CH + SSM_HEADS
N_EXPERTS = 256
TOP_K = 8
N_EXPERT_GROUPS = 8
TOPK_GROUPS = 4
EXPERT_FF = 256
SHARED_FF = 256
ROUTED_SCALE = 2.5
EXPERT_BLOCK = 128
NORM_EPS = 1e-6

kernel_name = "hybrid_dilated_attn_ssd_moe_block"


def rms_norm(x, gain):
    xf = x.astype(jnp.float32)
    y = xf * lax.rsqrt(jnp.mean(xf * xf, axis=-1, keepdims=True) + NORM_EPS)
    return (y * gain.astype(jnp.float32)).astype(x.dtype)


def modulate(h, shift, scale):
    return h * (1 + scale[:, None, :]) + shift[:, None, :]


def t5_causal_buckets(distance):
    n = np.maximum(distance, 0)
    max_exact = REL_BUCKETS // 2
    large = max_exact + (np.log(np.maximum(n, 1) / max_exact) / math.log(REL_MAX_DISTANCE / max_exact)
                         * (REL_BUCKETS - max_exact)).astype(np.int64)
    large = np.minimum(large, REL_BUCKETS - 1)
    return np.where(n < max_exact, n, large).astype(np.int32)


def dilated_window_attention(q, k, v, bias_table, window, dilation):
    b, h, s, hd = q.shape
    steps = window // dilation
    blk = steps
    L = s // dilation
    nb = -(-L // blk)
    Lp = nb * blk

    def to_residue(t):
        t = t.reshape(b, h, L, dilation, hd).transpose(0, 1, 3, 2, 4)
        return jnp.pad(t, ((0, 0), (0, 0), (0, 0), (0, Lp - L), (0, 0)))

    def windows(t):
        t = jnp.pad(t, ((0, 0), (0, 0), (0, 0), (blk, 0), (0, 0))).reshape(b, h, dilation, nb + 1, blk, hd)
        return jnp.concatenate([t[:, :, :, :-1], t[:, :, :, 1:]], axis=4)

    qb = to_residue(q).reshape(b, h, dilation, nb, blk, hd)
    kw = windows(to_residue(k))
    vw = windows(to_residue(v))

    qi = np.arange(blk)[:, None]
    kj = np.arange(2 * blk)[None, :]
    dist = qi + blk - kj
    band = (dist >= 0) & (dist <= steps)
    key_pos = np.arange(nb)[:, None] * blk - blk + kj
    valid = band[None] & (key_pos >= 0)[:, None, :]
    bias = jnp.transpose(bias_table.astype(jnp.float32)[t5_causal_buckets(dist * dilation)], (2, 0, 1))

    logits = jnp.einsum('bhrnqd,bhrnkd->bhrnqk', qb, kw) * (hd ** -0.5) + bias[None, :, None, None]
    logits = jnp.where(valid, logits, -jnp.inf)
    m = jnp.max(logits, axis=-1, keepdims=True)
    p = jnp.exp(logits - m)
    denom = jnp.sum(p, axis=-1, keepdims=True)
    o = jnp.einsum('bhrnqk,bhrnkd->bhrnqd', p, vw) / denom
    lse = (m + jnp.log(denom))[..., 0]
    o = o.reshape(b, h, dilation, Lp, hd)[:, :, :, :L].transpose(0, 1, 3, 2, 4).reshape(b, h, s, hd)
    lse = lse.reshape(b, h, dilation, Lp)[:, :, :, :L].transpose(0, 1, 3, 2).reshape(b, h, s)
    return o, lse


def dilated_attention_mixer(q, k, v, q_gain, k_gain, rel_bias_table):
    b, s, _ = q.shape
    def heads(t):
        return t.astype(jnp.float32).reshape(b, s, ATTN_HEADS, ATTN_HEAD_DIM)
    qh = rms_norm(heads(q), q_gain).transpose(0, 2, 1, 3)
    kh = rms_norm(heads(k), k_gain).transpose(0, 2, 1, 3)
    vh = heads(v).transpose(0, 2, 1, 3)
    outs, lses = [], []
    for window, dilation in DILATED_PATTERNS:
        o, lse = dilated_window_attention(qh, kh, vh, rel_bias_table, window, dilation)
        outs.append(o)
        lses.append(lse)
    weights = jax.nn.softmax(jnp.stack(lses), axis=0)
    o = jnp.sum(weights[..., None] * jnp.stack(outs), axis=0)
    return o.transpose(0, 2, 1, 3).reshape(b, s, ATTN_WIDTH).astype(q.dtype)


def causal_depthwise_conv(x, w, bias):
    y = lax.conv_general_dilated(x, w[:, None, :].astype(x.dtype), window_strides=(1,),
                                 padding=[(SSM_CONV - 1, 0)], dimension_numbers=('NWC', 'WIO', 'NWC'),
                                 feature_group_count=x.shape[-1])
    return y + bias


def ssd_chunked_scan(x, dt, A, Bm, Cm):
    b, s, g, hg, p = x.shape
    n = Bm.shape[-1]
    l = SSM_CHUNK
    nc = s // l
    xc = (x * dt[..., None]).reshape(b, nc, l, g, hg, p)
    bc = Bm.reshape(b, nc, l, g, n)
    cc = Cm.reshape(b, nc, l, g, n)
    a = (dt * A).reshape(b, nc, l, g, hg).transpose(0, 3, 4, 1, 2)
    a_cs = jnp.cumsum(a, axis=-1)
    causal = np.tril(np.ones((l, l), dtype=bool))
    seg = a_cs[..., :, None] - a_cs[..., None, :]
    decay_in = jnp.exp(jnp.where(causal, seg, -jnp.inf))
    cb = jnp.einsum('bclgn,bcsgn->bgcls', cc, bc)
    y_diag = jnp.einsum('bghcls,bcsghp->bclghp', cb[:, :, None] * decay_in, xc)
    decay_to_end = jnp.exp(a_cs[..., -1:] - a_cs)
    states = jnp.einsum('bclgn,bghcl,bclghp->bcghpn', bc, decay_to_end, xc)
    chunk_decay = jnp.exp(a_cs[..., -1])

    def step(h, inp):
        st, dec = inp
        return h * dec[..., None, None] + st, h

    h0 = jnp.zeros((b, g, hg, p, n), jnp.float32)
    _, prev = lax.scan(step, h0, (jnp.moveaxis(states, 1, 0), jnp.moveaxis(chunk_decay, -1, 0)))
    y_off = jnp.einsum('bclgn,cbghpn,bghcl->bclghp', cc, prev, jnp.exp(a_cs))
    return (y_diag + y_off).reshape(b, s, g, hg, p)


def ssd_mixer(z, xbc, dt_raw, conv_w, conv_b, dt_bias, a_log, d_skip, norm_gain):
    b, s, _ = z.shape
    xbc = jax.nn.silu(causal_depthwise_conv(xbc, conv_w, conv_b)).astype(jnp.float32)
    x_s = xbc[..., :SSM_WIDTH].reshape(b, s, SSM_GROUPS, SSM_HEADS_PER_GROUP, SSM_HEAD_DIM)
    Bm = xbc[..., SSM_WIDTH:SSM_WIDTH + SSM_GROUPS * SSM_STATE].reshape(b, s, SSM_GROUPS, SSM_STATE)
    Cm = xbc[..., SSM_WIDTH + SSM_GROUPS * SSM_STATE:].reshape(b, s, SSM_GROUPS, SSM_STATE)
    dt = jax.nn.softplus(dt_raw.astype(jnp.float32) + dt_bias.astype(jnp.float32))
    dt = dt.reshape(b, s, SSM_GROUPS, SSM_HEADS_PER_GROUP)
    A = -jnp.exp(a_log.astype(jnp.float32)).reshape(SSM_GROUPS, SSM_HEADS_PER_GROUP)
    y = ssd_chunked_scan(x_s, dt, A, Bm, Cm)
    y = y + d_skip.astype(jnp.float32).reshape(SSM_GROUPS, SSM_HEADS_PER_GROUP)[:, :, None] * x_s
    y = y.reshape(b, s, SSM_WIDTH) * jax.nn.silu(z.astype(jnp.float32))
    y = rms_norm(y.reshape(b, s, SSM_GROUPS, SSM_WIDTH // SSM_GROUPS),
                 norm_gain.reshape(SSM_GROUPS, SSM_WIDTH // SSM_GROUPS))
    return y.reshape(b, s, SSM_WIDTH).astype(z.dtype)


def swiglu(h, w_gate, w_up, w_down):
    return (jax.nn.silu(h @ w_gate) * (h @ w_up)) @ w_down


def routed_moe(h, w_router, router_bias, w_gate, w_up, w_down):
    t, d = h.shape
    e_per_group = N_EXPERTS // N_EXPERT_GROUPS
    scores = jax.nn.sigmoid(h.astype(jnp.float32) @ w_router.astype(jnp.float32))
    choice = scores + router_bias.astype(jnp.float32)
    group_scores = lax.top_k(choice.reshape(t, N_EXPERT_GROUPS, e_per_group), 2)[0].sum(-1)
    _, top_groups = lax.top_k(group_scores, TOPK_GROUPS)
    group_mask = jax.nn.one_hot(top_groups, N_EXPERT_GROUPS, dtype=jnp.float32).sum(1) > 0
    expert_mask = jnp.repeat(group_mask, e_per_group, axis=1)
    _, top_idx = lax.top_k(jnp.where(expert_mask, choice, -jnp.inf), TOP_K)
    gate = jnp.take_along_axis(scores, top_idx, axis=1)
    gate = gate / jnp.sum(gate, axis=-1, keepdims=True) * ROUTED_SCALE

    n_assign = t * TOP_K
    flat_e = top_idx.reshape(-1)
    flat_tok = jnp.repeat(jnp.arange(t, dtype=jnp.int32), TOP_K)
    flat_gate = gate.reshape(-1).astype(h.dtype)
    order = jnp.argsort(flat_e)
    se, stok, sgate = flat_e[order], flat_tok[order], flat_gate[order]
    counts = jnp.bincount(flat_e, length=N_EXPERTS)
    padded = (counts + EXPERT_BLOCK - 1) // EXPERT_BLOCK * EXPERT_BLOCK
    pad_end = jnp.cumsum(padded)
    pad_start = pad_end - padded
    start = jnp.cumsum(counts) - counts
    dest = pad_start[se] + jnp.arange(n_assign, dtype=jnp.int32) - start[se]
    n_blocks = -(-(n_assign + N_EXPERTS * (EXPERT_BLOCK - 1)) // EXPERT_BLOCK)
    rows = n_blocks * EXPERT_BLOCK
    buf_tok = jnp.full((rows,), t, jnp.int32).at[dest].set(stok)
    buf_gate = jnp.zeros((rows,), h.dtype).at[dest].set(sgate)
    block_expert = jnp.minimum(
        jnp.searchsorted(pad_end, jnp.arange(n_blocks, dtype=jnp.int32) * EXPERT_BLOCK, side='right'),
        N_EXPERTS - 1)
    h_pad = jnp.concatenate([h, jnp.zeros((1, d), h.dtype)], axis=0)
    xbuf = h_pad[buf_tok].reshape(n_blocks, EXPERT_BLOCK, d)

    def expert_block(args):
        xb, gb, e = args
        return swiglu(xb, w_gate[e], w_up[e], w_down[e]) * gb[:, None]

    ybuf = lax.map(expert_block, (xbuf, buf_gate.reshape(n_blocks, EXPERT_BLOCK), block_expert))
    return jax.ops.segment_sum(ybuf.reshape(rows, d), buf_tok, num_segments=t + 1)[:t]


def setup_inputs(seed: int = 0) -> dict:
    key = jax.random.key(seed)
    ks = jax.random.split(key, 32)
    f32 = jnp.float32

    def normal(k, shape, scale):
        return jax.random.normal(k, shape, f32) * scale

    dt_init = jnp.exp(jax.random.uniform(ks[11], (DEPTH, SSM_HEADS), f32,
                                         minval=math.log(1e-3), maxval=math.log(1e-1)))
    return {
        "x": normal(ks[0], (BATCH, SEQ, D_MODEL), 1.0),
        "c": normal(ks[1], (BATCH, D_MODEL), 1.0),
        "w_ada": normal(ks[2], (DEPTH, D_MODEL, 6 * D_MODEL), 0.5 * D_MODEL ** -0.5),
        "b_ada": normal(ks[3], (DEPTH, 6 * D_MODEL), 0.02),
        "norm_mix_gain": 1.0 + normal(ks[4], (DEPTH, D_MODEL), 0.05),
        "w_in": normal(ks[5], (DEPTH, D_MODEL, IN_PROJ_WIDTH), D_MODEL ** -0.5),
        "q_norm_gain": 1.0 + normal(ks[6], (DEPTH, ATTN_HEAD_DIM), 0.05),
        "k_norm_gain": 1.0 + normal(ks[7], (DEPTH, ATTN_HEAD_DIM), 0.05),
        "rel_bias_table": normal(ks[8], (REL_BUCKETS, ATTN_HEADS), 0.5),
        "conv_w": normal(ks[9], (DEPTH, SSM_CONV, CONV_CH), SSM_CONV ** -0.5),
        "conv_b": normal(ks[10], (DEPTH, CONV_CH), 0.02),
        "dt_bias": dt_init + jnp.log(-jnp.expm1(-dt_init)),
        "a_log": jnp.log(jax.random.uniform(ks[12], (DEPTH, SSM_HEADS), f32, minval=1.0, maxval=16.0)),
        "d_skip": 1.0 + normal(ks[13], (DEPTH, SSM_HEADS), 0.1),
        "ssm_norm_gain": 1.0 + normal(ks[14], (DEPTH, SSM_WIDTH), 0.05),
        "w_out": normal(ks[15], (DEPTH, MIX_WIDTH, D_MODEL), MIX_WIDTH ** -0.5),
        "norm_ffn_gain": 1.0 + normal(ks[16], (DEPTH, D_MODEL), 0.05),
        "w_router": normal(ks[17], (DEPTH, D_MODEL, N_EXPERTS), D_MODEL ** -0.5),
        "router_bias": normal(ks[18], (DEPTH, N_EXPERTS), 0.01),
        "w_gate_experts": normal(ks[19], (DEPTH, N_EXPERTS, D_MODEL, EXPERT_FF), D_MODEL ** -0.5),
        "w_up_experts": normal(ks[20], (DEPTH, N_EXPERTS, D_MODEL, EXPERT_FF), D_MODEL ** -0.5),
        "w_down_experts": normal(ks[21], (DEPTH, N_EXPERTS, EXPERT_FF, D_MODEL), EXPERT_FF ** -0.5),
        "w_gate_shared": normal(ks[22], (DEPTH, D_MODEL, SHARED_FF), D_MODEL ** -0.5),
        "w_up_shared": normal(ks[23], (DEPTH, D_MODEL, SHARED_FF), D_MODEL ** -0.5),
        "w_down_shared": normal(ks[24], (DEPTH, SHARED_FF, D_MODEL), SHARED_FF ** -0.5),
    }


def reference(x, c, w_ada, b_ada, norm_mix_gain, w_in, q_norm_gain, k_norm_gain, rel_bias_table,
              conv_w, conv_b, dt_bias, a_log, d_skip, ssm_norm_gain, w_out, norm_ffn_gain,
              w_router, router_bias, w_gate_experts, w_up_experts, w_down_experts,
              w_gate_shared, w_up_shared, w_down_shared):
    b, s, d = x.shape
    split_points = np.cumsum([ATTN_WIDTH, ATTN_WIDTH, ATTN_WIDTH, SSM_WIDTH, CONV_CH])
    for layer in range(DEPTH):
        mod = jax.nn.silu(c) @ w_ada[layer] + b_ada[layer]
        shift_m, scale_m, gate_m, shift_f, scale_f, gate_f = jnp.split(mod, 6, axis=-1)

        h = modulate(rms_norm(x, norm_mix_gain[layer]), shift_m, scale_m)
        proj = h @ w_in[layer]
        q, k, v, z, xbc, dt_raw = jnp.split(proj, split_points, axis=-1)
        attn_out = dilated_attention_mixer(q, k, v, q_norm_gain[layer], k_norm_gain[layer], rel_bias_table)
        ssm_out = ssd_mixer(z, xbc, dt_raw, conv_w[layer], conv_b[layer], dt_bias[layer], a_log[layer],
                            d_skip[layer], ssm_norm_gain[layer])
        mixed = jnp.concatenate([attn_out, ssm_out], axis=-1) @ w_out[layer]
        x = x + gate_m[:, None, :] * mixed

        h = modulate(rms_norm(x, norm_ffn_gain[layer]), shift_f, scale_f).reshape(b * s, d)
        ffn = swiglu(h, w_gate_shared[layer], w_up_shared[layer], w_down_shared[layer]) + routed_moe(
            h, w_router[layer], router_bias[layer], w_gate_experts[layer], w_up_experts[layer],
            w_down_experts[layer])
        x = x + gate_f[:, None, :] * ffn.reshape(b, s, d)
    return x
```

```python
import functools
import math

import numpy as np
import jax
import jax.numpy as jnp
from jax import lax
from jax.experimental import pallas as pl
from jax.experimental.pallas import tpu as pltpu

F32 = jnp.float32
BF16 = jnp.bfloat16
I32 = jnp.int32

D_MODEL = 1024
ATTN_HEADS = 8
HEAD_DIM = 64
ATTN_WIDTH = ATTN_HEADS * HEAD_DIM
PATTERNS = ((128, 1), (512, 4), (2048, 16))
WIN_STEPS = 128
REL_BUCKETS = 32
REL_MAX_DISTANCE = 2048
SSM_HEADS = 24
SSM_HEAD_DIM = 64
SSM_WIDTH = SSM_HEADS * SSM_HEAD_DIM
SSM_GROUPS = 4
HEADS_PER_GROUP = SSM_HEADS // SSM_GROUPS
GROUP_WIDTH = SSM_WIDTH // SSM_GROUPS
SSM_STATE = 128
SSM_CONV = 4
SSM_CHUNK = 128
CONV_CH = SSM_WIDTH + 2 * SSM_GROUPS * SSM_STATE
N_EXPERTS = 256
TOP_K = 8
N_EXPERT_GROUPS = 8
EXPERTS_PER_GROUP = N_EXPERTS // N_EXPERT_GROUPS
TOPK_GROUPS = 4
EXPERT_FF = 256
ROUTED_SCALE = 2.5
NORM_EPS = 1e-6

LANES = 128
SUBLANES = 8
NEG_BIG = -1e30
VMEM_LIMIT = 56 * 1024 * 1024


def _params(sem, vmem=VMEM_LIMIT):
    return pltpu.CompilerParams(dimension_semantics=sem, vmem_limit_bytes=vmem)


def _sigmoid(x):
    return 1.0 / (1.0 + jnp.exp(-x))


def _silu(x):
    return x * _sigmoid(x)


def _split3(x):
    hi = x.astype(BF16)
    r = x - hi.astype(F32)
    mid = r.astype(BF16)
    lo = (r - mid.astype(F32)).astype(BF16)
    return hi, mid, lo


def _dot(a, b):
    return jnp.dot(a, b, preferred_element_type=F32)


def _dot_nt(a, b):
    return lax.dot_general(a, b, (((1,), (1,)), ((), ())), preferred_element_type=F32)


def _dot_exact_rhs(a, b_exact):
    hi, mid, lo = _split3(a)
    return _dot(hi, b_exact) + _dot(mid, b_exact) + _dot(lo, b_exact)


def _dot_exact_lhs(a_exact, b):
    hi, mid, lo = _split3(b)
    return _dot(a_exact, hi) + _dot(a_exact, mid) + _dot(a_exact, lo)


def _adaln_kernel(c_ref, w_ref, b_ref, o_ref):
    s = _silu(c_ref[...]).astype(BF16)
    o_ref[...] = _dot(s, w_ref[...].astype(BF16)) + b_ref[...]


def _adaln(c, w_ada, b_ada):
    b, d = c.shape
    n = w_ada.shape[1]
    rows = SUBLANES
    c_pad = jnp.zeros((rows, d), F32).at[:b].set(c)
    tn = 1024
    out = pl.pallas_call(
        _adaln_kernel,
        grid=(n // tn,),
        in_specs=[pl.BlockSpec((rows, d), lambda j: (0, 0)),
                  pl.BlockSpec((d, tn), lambda j: (0, j)),
                  pl.BlockSpec((1, tn), lambda j: (0, j))],
        out_specs=pl.BlockSpec((rows, tn), lambda j: (0, j)),
        out_shape=jax.ShapeDtypeStruct((rows, n), F32),
        compiler_params=_params(("arbitrary",)),
        name="adaln",
    )(c_pad, w_ada, b_ada.reshape(1, n))
    return out[:b]


def _inproj_kernel(x_ref, shift_ref, scale_ref, g_ref, wqkv_ref, wz_ref, wxbc_ref, wdt_ref,
                   qg_ref, kg_ref, hmean_ref, q_ref, k_ref, v_ref, z_ref, xbc_ref, dt_ref):
    x = x_ref[...]
    ms = jnp.mean(x * x, axis=-1, keepdims=True)
    h = x * lax.rsqrt(ms + NORM_EPS) * g_ref[...]
    h = h * (1.0 + scale_ref[...]) + shift_ref[...]
    hb = h.astype(BF16)

    hmean = hmean_ref[...]

    def head_norm(t, gain):
        ss = _dot_exact_rhs(t * t, hmean)
        return t * lax.rsqrt(ss + NORM_EPS) * gain

    q = _dot(hb, wqkv_ref[:, 0:ATTN_WIDTH])
    q_ref[...] = head_norm(q, qg_ref[...]) * (HEAD_DIM ** -0.5)
    k = _dot(hb, wqkv_ref[:, ATTN_WIDTH:2 * ATTN_WIDTH])
    k_ref[...] = head_norm(k, kg_ref[...])
    v_ref[...] = _dot(hb, wqkv_ref[:, 2 * ATTN_WIDTH:3 * ATTN_WIDTH])
    for c0 in range(0, SSM_WIDTH, 512):
        z_ref[:, c0:c0 + 512] = _dot(hb, wz_ref[:, c0:c0 + 512])
    for c0 in range(0, CONV_CH, 512):
        xbc_ref[:, c0:c0 + 512] = _dot(hb, wxbc_ref[:, c0:c0 + 512])
    dt_ref[...] = _dot(hb, wdt_ref[...])


def _in_proj(x, shift, scale, gain, w_in, q_gain, k_gain, tm=256):
    b, s, d = x.shape
    t = b * s
    tiles_per_seq = s // tm
    w = w_in.astype(BF16)
    o_z = 3 * ATTN_WIDTH
    o_x = o_z + SSM_WIDTH
    o_dt = o_x + CONV_CH
    w_qkv, w_z, w_xbc = w[:, :o_z], w[:, o_z:o_x], w[:, o_x:o_dt]
    w_dt = jnp.zeros((d, LANES), BF16).at[:, :SSM_HEADS].set(w[:, o_dt:])
    head_of = np.arange(ATTN_WIDTH) // HEAD_DIM
    hmean = jnp.asarray((head_of[:, None] == head_of[None, :]).astype(np.float32) / HEAD_DIM, BF16)
    full = lambda shp: pl.BlockSpec(shp, lambda i: (0,) * len(shp))
    row = lambda n: pl.BlockSpec((tm, n), lambda i: (i, 0))
    per_batch = pl.BlockSpec((None, 1, d), lambda i: (i // tiles_per_seq, 0, 0))
    outs = pl.pallas_call(
        _inproj_kernel,
        grid=(t // tm,),
        in_specs=[row(d), per_batch, per_batch, full((1, d)),
                  full((d, o_z)), full((d, SSM_WIDTH)), full((d, CONV_CH)), full((d, LANES)),
                  full((1, ATTN_WIDTH)), full((1, ATTN_WIDTH)), full((ATTN_WIDTH, ATTN_WIDTH))],
        out_specs=[row(ATTN_WIDTH), row(ATTN_WIDTH), row(ATTN_WIDTH), row(SSM_WIDTH), row(CONV_CH), row(LANES)],
        out_shape=[jax.ShapeDtypeStruct((t, n), F32)
                   for n in (ATTN_WIDTH, ATTN_WIDTH, ATTN_WIDTH, SSM_WIDTH, CONV_CH, LANES)],
        compiler_params=_params(("arbitrary",)),
        name="in_proj",
    )(x.reshape(t, d), shift.reshape(b, 1, d), scale.reshape(b, 1, d), gain.reshape(1, d),
      w_qkv, w_z, w_xbc, w_dt,
      jnp.tile(q_gain, ATTN_HEADS).reshape(1, ATTN_WIDTH), jnp.tile(k_gain, ATTN_HEADS).reshape(1, ATTN_WIDTH), hmean)
    return outs


def _t5_causal_buckets(distance):
    n = np.maximum(distance, 0)
    max_exact = REL_BUCKETS // 2
    large = max_exact + (np.log(np.maximum(n, 1) / max_exact) / math.log(REL_MAX_DISTANCE / max_exact)
                         * (REL_BUCKETS - max_exact)).astype(np.int64)
    large = np.minimum(large, REL_BUCKETS - 1)
    return np.where(n < max_exact, n, large).astype(np.int32)


def _window_bias(rel_bias_table, dilation):
    qi = np.arange(WIN_STEPS)[:, None]
    kj = np.arange(2 * WIN_STEPS)[None, :]
    dist = qi + WIN_STEPS - kj
    band = (dist >= 0) & (dist <= WIN_STEPS)
    bias = jnp.transpose(rel_bias_table.astype(F32)[_t5_causal_buckets(dist * dilation)], (2, 0, 1))
    return jnp.where(jnp.asarray(band)[None], bias, NEG_BIG)


def _attn_kernel(q_ref, kp_ref, kc_ref, vp_ref, vc_ref, bias_ref, o_ref, l_ref):
    first = pl.program_id(3) == 0
    q = q_ref[...]
    kp, kc, vp, vc = kp_ref[...], kc_ref[...], vp_ref[...], vc_ref[...]
    col = lax.broadcasted_iota(I32, (WIN_STEPS, 2 * WIN_STEPS), 1)
    before_start = jnp.logical_and(col < WIN_STEPS, first)
    outs, lses = [], []
    for h in range(2):
        sl = slice(h * HEAD_DIM, (h + 1) * HEAD_DIM)
        qh = q[:, sl].astype(BF16)
        kh = jnp.concatenate([kp[:, sl], kc[:, sl]], axis=0).astype(BF16)
        vh = jnp.concatenate([vp[:, sl], vc[:, sl]], axis=0).astype(BF16)
        s = _dot_nt(qh, kh) + bias_ref[h]
        s = jnp.where(before_start, NEG_BIG, s)
        m = jnp.max(s, axis=-1, keepdims=True)
        p = jnp.exp(s - m)
        denom = jnp.sum(p, axis=-1, keepdims=True)
        o = _dot(p.astype(BF16), vh) / denom
        outs.append(o)
        lses.append(jnp.broadcast_to(m + jnp.log(denom), (WIN_STEPS, HEAD_DIM)))
    o_ref[...] = jnp.concatenate(outs, axis=1)
    l_ref[...] = jnp.concatenate(lses, axis=1)


def _attention_pattern(q, k, v, bias, dilation):
    b, s, w = q.shape
    d = dilation
    length = s // d
    nblk = length // WIN_STEPS
    pairs = ATTN_HEADS // 2
    view = lambda t: t.reshape(b, length, d * w)
    cur = pl.BlockSpec((None, WIN_STEPS, LANES), lambda bi, r, hp, i: (bi, i, r * pairs + hp))
    prev = pl.BlockSpec((None, WIN_STEPS, LANES), lambda bi, r, hp, i: (bi, jnp.maximum(i - 1, 0), r * pairs + hp))
    o, lse = pl.pallas_call(
        _attn_kernel,
        grid=(b, d, pairs, nblk),
        in_specs=[cur, prev, cur, prev, cur,
                  pl.BlockSpec((2, WIN_STEPS, 2 * WIN_STEPS), lambda bi, r, hp, i: (hp, 0, 0))],
        out_specs=[cur, cur],
        out_shape=[jax.ShapeDtypeStruct((b, length, d * w), F32)] * 2,
        compiler_params=_params(("arbitrary",) * 4),
        name=f"attn_d{d}",
    )(view(q), view(k), view(k), view(v), view(v), bias)
    return o.reshape(b, s, w), lse.reshape(b, s, w)


def _ssd_kernel(xbc_ref, halo_ref, z_ref, dtraw_ref, convw_ref, convb_ref, dtb_ref, alog_ref, dskip_ref, gain_ref,
                expand_ref, tril_ref, y_ref, state_ref):
    c = pl.program_id(1)

    @pl.when(c == 0)
    def _():
        state_ref[...] = jnp.zeros_like(state_ref)

    x = xbc_ref[...]
    halo = jnp.where(c == 0, 0.0, halo_ref[...])
    w = convw_ref[...]
    acc = x * w[SSM_CONV - 1:SSM_CONV, :] + convb_ref[...]
    row8 = lax.broadcasted_iota(I32, (SUBLANES, CONV_CH), 0)
    for shift in range(1, SSM_CONV):
        xs = pltpu.roll(x, shift, axis=0)
        hs = pltpu.roll(halo, shift, axis=0)
        head = jnp.where(row8 < shift, hs, xs[0:SUBLANES])
        xs = jnp.concatenate([head, xs[SUBLANES:]], axis=0)
        acc = acc + xs * w[SSM_CONV - 1 - shift:SSM_CONV - shift, :]
    act = _silu(acc)
    x_s = act[:, :SSM_WIDTH]
    bc0 = SSM_WIDTH
    cc0 = SSM_WIDTH + SSM_GROUPS * SSM_STATE

    t = dtraw_ref[...] + dtb_ref[...]
    dt = jnp.maximum(t, 0.0) + jnp.log(1.0 + jnp.exp(-jnp.abs(t)))
    a = dt * (-jnp.exp(alog_ref[...]))
    a_cs = _dot_exact_lhs(tril_ref[...], a)
    a_cs_t = a_cs.T
    a_last = a_cs[SSM_CHUNK - 1:SSM_CHUNK, :]
    expand = expand_ref[...]
    dt_e = _dot_exact_rhs(dt, expand)
    ea_e = _dot_exact_rhs(jnp.exp(a_cs), expand)
    dte_e = _dot_exact_rhs(jnp.exp(a_last - a_cs), expand)
    xdt = x_s * dt_e
    xw = (xdt * dte_e).astype(BF16)
    xdt_b = xdt.astype(BF16)

    li = lax.broadcasted_iota(I32, (SSM_CHUNK, SSM_CHUNK), 0)
    si = lax.broadcasted_iota(I32, (SSM_CHUNK, SSM_CHUNK), 1)
    causal = li >= si

    ys = []
    for g in range(SSM_GROUPS):
        gs = slice(g * GROUP_WIDTH, (g + 1) * GROUP_WIDTH)
        b_g = act[:, bc0 + g * SSM_STATE:bc0 + (g + 1) * SSM_STATE]
        c_g = act[:, cc0 + g * SSM_STATE:cc0 + (g + 1) * SSM_STATE].astype(BF16)
        cb = _dot_nt(c_g, b_g.astype(BF16))
        state = state_ref[g]
        y_off = _dot(c_g, state.astype(BF16)) * ea_e[:, gs]
        parts = []
        for j in range(HEADS_PER_GROUP):
            hh = g * HEADS_PER_GROUP + j
            seg = a_cs[:, hh:hh + 1] - a_cs_t[hh:hh + 1, :]
            decay = jnp.exp(jnp.where(causal, seg, NEG_BIG))
            m = (cb * decay).astype(BF16)
            parts.append(_dot(m, xdt_b[:, hh * SSM_HEAD_DIM:(hh + 1) * SSM_HEAD_DIM]))
        ys.append(jnp.concatenate(parts, axis=1) + y_off)
        state_ref[g] = state * ea_e[SSM_CHUNK - 1:SSM_CHUNK, gs] + _dot(b_g.T.astype(BF16), xw[:, gs])
    y = jnp.concatenate(ys, axis=1) + dskip_ref[...] * x_s
    y = y * _silu(z_ref[...])
    gain = gain_ref[...]
    for g in range(SSM_GROUPS):
        gs = slice(g * GROUP_WIDTH, (g + 1) * GROUP_WIDTH)
        yg = y[:, gs]
        ms = jnp.mean(yg * yg, axis=-1, keepdims=True)
        y_ref[:, gs] = yg * lax.rsqrt(ms + NORM_EPS) * gain[:, gs]


def _ssd(xbc, z, dt_raw, conv_w, conv_b, dt_bias, a_log, d_skip, norm_gain, b, s):
    t = b * s
    nc = s // SSM_CHUNK
    pad_heads = lambda v: jnp.zeros((1, LANES), F32).at[0, :SSM_HEADS].set(v)
    head_of_lane = np.arange(SSM_WIDTH) // SSM_HEAD_DIM
    expand = jnp.asarray((np.arange(LANES)[:, None] == head_of_lane[None, :]).astype(np.float32), BF16)
    tril = jnp.asarray(np.tril(np.ones((SSM_CHUNK, SSM_CHUNK), np.float32)), BF16)
    halo_blocks = SSM_CHUNK // SUBLANES
    chunk = lambda n: pl.BlockSpec((SSM_CHUNK, n), lambda bi, c: (bi * nc + c, 0))
    full = lambda shp: pl.BlockSpec(shp, lambda bi, c: (0,) * len(shp))
    halo = pl.BlockSpec((SUBLANES, CONV_CH), lambda bi, c: (jnp.maximum((bi * nc + c) * halo_blocks - 1, 0), 0))
    return pl.pallas_call(
        _ssd_kernel,
        grid=(b, nc),
        in_specs=[chunk(CONV_CH), halo, chunk(SSM_WIDTH), chunk(LANES),
                  full((SSM_CONV, CONV_CH)), full((1, CONV_CH)), full((1, LANES)), full((1, LANES)),
                  full((1, SSM_WIDTH)), full((1, SSM_WIDTH)), full((LANES, SSM_WIDTH)), full((SSM_CHUNK, SSM_CHUNK))],
        out_specs=chunk(SSM_WIDTH),
        out_shape=jax.ShapeDtypeStruct((t, SSM_WIDTH), F32),
        scratch_shapes=[pltpu.VMEM((SSM_GROUPS, SSM_STATE, GROUP_WIDTH), F32)],
        compiler_params=_params(("arbitrary", "arbitrary")),
        name="ssd",
    )(xbc, xbc, z, dt_raw, conv_w, conv_b.reshape(1, CONV_CH), pad_heads(dt_bias), pad_heads(a_log),
      jnp.repeat(d_skip, SSM_HEAD_DIM).reshape(1, SSM_WIDTH), norm_gain.reshape(1, SSM_WIDTH), expand, tril)


def _outproj_kernel(o1_ref, o2_ref, o3_ref, l1_ref, l2_ref, l3_ref, ssm_ref, x_ref, gate_ref, shift_ref, scale_ref,
                    g_ref, wa_ref, ws_ref, x1_ref, h2_ref):
    l1, l2, l3 = l1_ref[...], l2_ref[...], l3_ref[...]
    m = jnp.maximum(jnp.maximum(l1, l2), l3)
    e1, e2, e3 = jnp.exp(l1 - m), jnp.exp(l2 - m), jnp.exp(l3 - m)
    attn = (e1 * o1_ref[...] + e2 * o2_ref[...] + e3 * o3_ref[...]) / (e1 + e2 + e3)
    mixed = _dot(attn.astype(BF16), wa_ref[...]) + _dot(ssm_ref[...].astype(BF16), ws_ref[...])
    x1 = x_ref[...] + gate_ref[...] * mixed
    x1_ref[...] = x1
    ms = jnp.mean(x1 * x1, axis=-1, keepdims=True)
    h = x1 * lax.rsqrt(ms + NORM_EPS) * g_ref[...]
    h2_ref[...] = h * (1.0 + scale_ref[...]) + shift_ref[...]


def _out_proj(os_, ls_, ssm, x, gate, shift, scale, gain, w_out, b, s, tm=256):
    t = b * s
    d = D_MODEL
    tiles_per_seq = s // tm
    w = w_out.astype(BF16)
    row = lambda n: pl.BlockSpec((tm, n), lambda i: (i, 0))
    full = lambda shp: pl.BlockSpec(shp, lambda i: (0,) * len(shp))
    per_batch = pl.BlockSpec((None, 1, d), lambda i: (i // tiles_per_seq, 0, 0))
    return pl.pallas_call(
        _outproj_kernel,
        grid=(t // tm,),
        in_specs=[row(ATTN_WIDTH)] * 6 + [row(SSM_WIDTH), row(d), per_batch, per_batch, per_batch, full((1, d)),
                                          full((ATTN_WIDTH, d)), full((SSM_WIDTH, d))],
        out_specs=[row(d), row(d)],
        out_shape=[jax.ShapeDtypeStruct((t, d), F32)] * 2,
        compiler_params=_params(("arbitrary",)),
        name="out_proj",
    )(*[o.reshape(t, ATTN_WIDTH) for o in os_], *[l.reshape(t, ATTN_WIDTH) for l in ls_], ssm, x.reshape(t, d),
      gate.reshape(b, 1, d), shift.reshape(b, 1, d), scale.reshape(b, 1, d), gain.reshape(1, d),
      w[:ATTN_WIDTH], w[ATTN_WIDTH:])


def _mixer_sublayer(x, mod, norm_mix_gain, w_in, q_norm_gain, k_norm_gain, rel_bias_table, conv_w, conv_b, dt_bias,
                    a_log, d_skip, ssm_norm_gain, w_out, norm_ffn_gain):
    b, s, d = x.shape
    shift_m, scale_m, gate_m, shift_f, scale_f, _ = jnp.split(mod, 6, axis=-1)
    q, k, v, z, xbc, dt_raw = _in_proj(x, shift_m, scale_m, norm_mix_gain, w_in, q_norm_gain, k_norm_gain)
    os_, ls_ = [], []
    for _, dilation in PATTERNS:
        o, lse = _attention_pattern(q.reshape(b, s, ATTN_WIDTH), k.reshape(b, s, ATTN_WIDTH),
                                    v.reshape(b, s, ATTN_WIDTH), _window_bias(rel_bias_table, dilation), dilation)
        os_.append(o)
        ls_.append(lse)
    ssm = _ssd(xbc, z, dt_raw, conv_w, conv_b, dt_bias, a_log, d_skip, ssm_norm_gain, b, s)
    return _out_proj(os_, ls_, ssm, x, gate_m, shift_f, scale_f, norm_ffn_gain, w_out, b, s)


def _first_argmax(v, iota, limit):
    m = jnp.max(v, axis=0, keepdims=True)
    idx = jnp.min(jnp.where(v == m, iota, limit), axis=0, keepdims=True)
    return m, idx


def _router_kernel(h_ref, wt_ref, bias_ref, upper_ref, eidx_ref, rank_ref, gate_ref, counts_ref, carry_ref):
    @pl.when(pl.program_id(0) == 0)
    def _():
        carry_ref[...] = jnp.zeros_like(carry_ref)

    tm = h_ref.shape[0]
    h = h_ref[...]
    wt = wt_ref[...]
    h_hi = h.astype(BF16)
    h_lo = (h - h_hi.astype(F32)).astype(BF16)
    w_hi = wt.astype(BF16)
    w_lo = (wt - w_hi.astype(F32)).astype(BF16)
    logits = _dot_nt(w_hi, h_hi) + _dot_nt(w_hi, h_lo) + _dot_nt(w_lo, h_hi)
    scores = _sigmoid(logits)
    choice = scores + bias_ref[...]
    neg_inf = -jnp.inf

    iota_g = lax.broadcasted_iota(I32, (EXPERTS_PER_GROUP, tm), 0).astype(F32)
    group_rows = []
    for g in range(N_EXPERT_GROUPS):
        v = choice[g * EXPERTS_PER_GROUP:(g + 1) * EXPERTS_PER_GROUP]
        m1, i1 = _first_argmax(v, iota_g, float(EXPERTS_PER_GROUP))
        m2 = jnp.max(jnp.where(iota_g == i1, neg_inf, v), axis=0, keepdims=True)
        group_rows.append(m1 + m2)
    group_scores = jnp.concatenate(group_rows, axis=0)

    iota_n = lax.broadcasted_iota(I32, (N_EXPERT_GROUPS, tm), 0).astype(F32)
    chosen = jnp.zeros((N_EXPERT_GROUPS, tm), F32)
    for _ in range(TOPK_GROUPS):
        _, gi = _first_argmax(group_scores, iota_n, float(N_EXPERT_GROUPS))
        hit = iota_n == gi
        chosen = jnp.where(hit, 1.0, chosen)
        group_scores = jnp.where(hit, neg_inf, group_scores)

    masked = jnp.concatenate(
        [jnp.where(chosen[g:g + 1] > 0.0, choice[g * EXPERTS_PER_GROUP:(g + 1) * EXPERTS_PER_GROUP], neg_inf)
         for g in range(N_EXPERT_GROUPS)], axis=0)

    iota_e = lax.broadcasted_iota(I32, (N_EXPERTS, tm), 0).astype(F32)
    picked, gates = [], []
    onehot = jnp.zeros((N_EXPERTS, tm), F32)
    for _ in range(TOP_K):
        _, ei = _first_argmax(masked, iota_e, float(N_EXPERTS))
        hit = iota_e == ei
        gates.append(jnp.sum(jnp.where(hit, scores, 0.0), axis=0, keepdims=True))
        masked = jnp.where(hit, neg_inf, masked)
        onehot = jnp.where(hit, 1.0, onehot)
        picked.append(ei)
    gate_sum = gates[0]
    for gk in gates[1:]:
        gate_sum = gate_sum + gk

    base = _dot(onehot.astype(BF16), upper_ref[...]) + carry_ref[...]
    ranks = [jnp.sum(jnp.where(iota_e == ei, base, 0.0), axis=0, keepdims=True) for ei in picked]
    carry_ref[...] = carry_ref[...] + jnp.sum(onehot, axis=1, keepdims=True)

    eidx_ref[...] = jnp.concatenate(picked, axis=0).astype(I32)
    rank_ref[...] = jnp.concatenate(ranks, axis=0).astype(I32)
    gate_ref[...] = jnp.concatenate([gk / gate_sum * ROUTED_SCALE for gk in gates], axis=0)
    counts_ref[...] = carry_ref[...].astype(I32)


def _router(h2, w_router, router_bias, tm=256):
    t, d = h2.shape
    upper = jnp.asarray(np.triu(np.ones((tm, tm), np.float32), 1), BF16)
    tok = pl.BlockSpec((TOP_K, tm), lambda i: (0, i))
    full = lambda shp: pl.BlockSpec(shp, lambda i: (0,) * len(shp))
    return pl.pallas_call(
        _router_kernel,
        grid=(t // tm,),
        in_specs=[pl.BlockSpec((tm, d), lambda i: (i, 0)), full((N_EXPERTS, d)), full((N_EXPERTS, 1)), full((tm, tm))],
        out_specs=[tok, tok, tok, full((N_EXPERTS, 1))],
        out_shape=[jax.ShapeDtypeStruct((TOP_K, t), I32), jax.ShapeDtypeStruct((TOP_K, t), I32),
                   jax.ShapeDtypeStruct((TOP_K, t), F32), jax.ShapeDtypeStruct((N_EXPERTS, 1), I32)],
        scratch_shapes=[pltpu.VMEM((N_EXPERTS, 1), F32)],
        compiler_params=_params(("arbitrary",)),
        name="router",
    )(h2, w_router.T, router_bias.reshape(N_EXPERTS, 1), upper)


def _dispatch_kernel(pos_ref, h_ref, xs_ref, sem):
    tm = h_ref.shape[0]

    def row_copy(t, k):
        return pltpu.make_async_copy(h_ref.at[pl.ds(t, 1)], xs_ref.at[pl.ds(pos_ref[0, k, t], 1)], sem)

    def start(t, carry):
        for k in range(TOP_K):
            row_copy(t, k).start()
        return carry

    def wait(t, carry):
        for k in range(TOP_K):
            row_copy(t, k).wait()
        return carry

    lax.fori_loop(0, tm, start, 0)
    lax.fori_loop(0, tm, wait, 0)


def _dispatch(h2, pos_tiles, tm):
    t, d = h2.shape
    return pl.pallas_call(
        _dispatch_kernel,
        grid=(t // tm,),
        in_specs=[pl.BlockSpec((1, TOP_K, tm), lambda i: (i, 0, 0), memory_space=pltpu.SMEM),
                  pl.BlockSpec((tm, d), lambda i: (i, 0))],
        out_specs=pl.BlockSpec(memory_space=pl.ANY),
        out_shape=jax.ShapeDtypeStruct((t * TOP_K, d), h2.dtype),
        scratch_shapes=[pltpu.SemaphoreType.DMA(())],
        compiler_params=_params(("arbitrary",)),
        name="dispatch",
    )(pos_tiles, h2)


def _experts_kernel(tile_ref, exp_ref, lo_ref, hi_ref, xs_ref, wg_ref, wu_ref, wd_ref, ys_ref):
    w = pl.program_id(0)
    tmg = xs_ref.shape[0]
    tile = tile_ref[w]
    lo, hi = lo_ref[w], hi_ref[w]
    new_tile = jnp.logical_or(w == 0, tile_ref[jnp.maximum(w - 1, 0)] != tile)

    @pl.when(new_tile)
    def _():
        ys_ref[...] = jnp.zeros_like(ys_ref)

    @pl.when(hi > lo)
    def _():
        x = xs_ref[...].astype(BF16)
        g = _dot(x, wg_ref[...].astype(BF16))
        u = _dot(x, wu_ref[...].astype(BF16))
        y = _dot((_silu(g) * u).astype(BF16), wd_ref[...].astype(BF16))
        row = tile * tmg + lax.broadcasted_iota(I32, (tmg, 1), 0)
        inside = jnp.logical_and(row >= lo, row < hi)
        ys_ref[...] = jnp.where(inside, y, ys_ref[...])


def _group_metadata(counts, n_rows, tmg):
    n_tiles = n_rows // tmg
    n_work = n_tiles + N_EXPERTS
    ends = jnp.cumsum(counts)
    starts = ends - counts
    first_tile = starts // tmg
    n_items = jnp.where(counts > 0, (ends - 1) // tmg - first_tile + 1, 0)
    item_ends = jnp.cumsum(n_items)
    w = jnp.arange(n_work, dtype=I32)
    valid = w < item_ends[-1]
    e = jnp.minimum(jnp.searchsorted(item_ends, w, side="right"), N_EXPERTS - 1).astype(I32)
    tile = first_tile[e] + (w - (item_ends[e] - n_items[e]))
    lo = jnp.maximum(starts[e], tile * tmg)
    hi = jnp.minimum(ends[e], (tile + 1) * tmg)
    last_e = jnp.max(jnp.where(counts > 0, jnp.arange(N_EXPERTS, dtype=I32), 0))
    tile = jnp.where(valid, tile, n_tiles - 1).astype(I32)
    e = jnp.where(valid, e, last_e).astype(I32)
    lo = jnp.where(valid, lo, 0).astype(I32)
    hi = jnp.where(valid, hi, 0).astype(I32)
    return tile, e, lo, hi


def _experts(xs, counts, w_gate, w_up, w_down, tmg=256):
    n, d = xs.shape
    tile, e, lo, hi = _group_metadata(counts, n, tmg)
    n_work = tile.shape[0]
    grid_spec = pltpu.PrefetchScalarGridSpec(
        num_scalar_prefetch=4,
        grid=(n_work,),
        in_specs=[pl.BlockSpec((tmg, d), lambda w, tile, e, lo, hi: (tile[w], 0)),
                  pl.BlockSpec((None, d, EXPERT_FF), lambda w, tile, e, lo, hi: (e[w], 0, 0)),
                  pl.BlockSpec((None, d, EXPERT_FF), lambda w, tile, e, lo, hi: (e[w], 0, 0)),
                  pl.BlockSpec((None, EXPERT_FF, d), lambda w, tile, e, lo, hi: (e[w], 0, 0))],
        out_specs=pl.BlockSpec((tmg, d), lambda w, tile, e, lo, hi: (tile[w], 0)),
    )
    return pl.pallas_call(
        _experts_kernel,
        grid_spec=grid_spec,
        out_shape=jax.ShapeDtypeStruct((n, d), F32),
        compiler_params=_params(("arbitrary",)),
        name="experts",
    )(tile, e, lo, hi, xs, w_gate, w_up, w_down)


def _combine_kernel(pos_ref, gates_ref, h_ref, x1_ref, gatef_ref, wg_ref, wu_ref, wd_ref, ys_ref, out_ref, buf, sem):
    tm = h_ref.shape[0]

    def row_copy(t, k):
        return pltpu.make_async_copy(ys_ref.at[pl.ds(pos_ref[0, k, t], 1)], buf.at[k, pl.ds(t, 1)], sem)

    def start(t, carry):
        for k in range(TOP_K):
            row_copy(t, k).start()
        return carry

    def wait(t, carry):
        for k in range(TOP_K):
            row_copy(t, k).wait()
        return carry

    lax.fori_loop(0, tm, start, 0)
    hb = h_ref[...].astype(BF16)
    shared = _dot((_silu(_dot(hb, wg_ref[...])) * _dot(hb, wu_ref[...])).astype(BF16), wd_ref[...])
    lax.fori_loop(0, tm, wait, 0)
    gates = gates_ref[...]
    routed = buf[0] * gates[:, 0:1]
    for k in range(1, TOP_K):
        routed = routed + buf[k] * gates[:, k:k + 1]
    out_ref[...] = x1_ref[...] + gatef_ref[...] * (shared + routed)


def _combine(ys, pos_tiles, gates_t, h2, x1, gate_f, w_gate_s, w_up_s, w_down_s, b, s, tm):
    t, d = h2.shape
    tiles_per_seq = s // tm
    row = lambda n: pl.BlockSpec((tm, n), lambda i: (i, 0))
    full = lambda shp: pl.BlockSpec(shp, lambda i: (0,) * len(shp))
    return pl.pallas_call(
        _combine_kernel,
        grid=(t // tm,),
        in_specs=[pl.BlockSpec((1, TOP_K, tm), lambda i: (i, 0, 0), memory_space=pltpu.SMEM),
                  row(TOP_K), row(d), row(d),
                  pl.BlockSpec((None, 1, d), lambda i: (i // tiles_per_seq, 0, 0)),
                  full((d, EXPERT_FF)), full((d, EXPERT_FF)), full((EXPERT_FF, d)),
                  pl.BlockSpec(memory_space=pl.ANY)],
        out_specs=row(d),
        out_shape=jax.ShapeDtypeStruct((t, d), F32),
        scratch_shapes=[pltpu.VMEM((TOP_K, tm, d), F32), pltpu.SemaphoreType.DMA(())],
        compiler_params=_params(("arbitrary",)),
        name="combine",
    )(pos_tiles, gates_t, h2, x1, gate_f.reshape(b, 1, d),
      w_gate_s.astype(BF16), w_up_s.astype(BF16), w_down_s.astype(BF16), ys)


def _moe_sublayer(x1, h2, gate_f, w_router, router_bias, w_gate, w_up, w_down, w_gate_s, w_up_s, w_down_s, b, s,
                  tm=256):
    t = b * s
    eidx, rank, gates, counts = _router(h2, w_router, router_bias)
    counts = counts[:, 0]
    offsets = jnp.cumsum(counts) - counts
    pos = offsets[eidx] + rank
    pos_tiles = pos.reshape(TOP_K, t // tm, tm).transpose(1, 0, 2)
    xs = _dispatch(h2, pos_tiles, tm)
    ys = _experts(xs, counts, w_gate, w_up, w_down)
    return _combine(ys, pos_tiles, gates.T, h2, x1, gate_f, w_gate_s, w_up_s, w_down_s, b, s, tm)


def kernel(x, c, w_ada, b_ada, norm_mix_gain, w_in, q_norm_gain, k_norm_gain, rel_bias_table, conv_w, conv_b, dt_bias,
           a_log, d_skip, ssm_norm_gain, w_out, norm_ffn_gain, w_router, router_bias, w_gate_experts, w_up_experts,
           w_down_experts, w_gate_shared, w_up_shared, w_down_shared):
    b, s, d = x.shape
    for layer in range(w_ada.shape[0]):
        mod = _adaln(c, w_ada[layer], b_ada[layer])
        x1, h2 = _mixer_sublayer(x, mod, norm_mix_gain[layer], w_in[layer], q_norm_gain[layer], k_norm_gain[layer],
                                 rel_bias_table, conv_w[layer], conv_b[layer], dt_bias[layer], a_log[layer],
                                 d_skip[layer], ssm_norm_gain[layer], w_out[layer], norm_ffn_gain[layer])
        gate_f = mod[:, 5 * d:]
        out = _moe_sublayer(x1, h2, gate_f, w_router[layer], router_bias[layer], w_gate_experts[layer],
                            w_up_experts[layer], w_down_experts[layer], w_gate_shared[layer], w_up_shared[layer],
                            w_down_shared[layer], b, s)
        x = out.reshape(b, s, d)
    return x
```

```python
import functools
import math

import numpy as np
import jax
import jax.numpy as jnp
from jax import lax
from jax.experimental import pallas as pl
from jax.experimental.pallas import tpu as pltpu

F32 = jnp.float32
BF16 = jnp.bfloat16
I32 = jnp.int32

D_MODEL = 1024
ATTN_HEADS = 8
HEAD_DIM = 64
ATTN_WIDTH = ATTN_HEADS * HEAD_DIM
PATTERNS = ((128, 1), (512, 4), (2048, 16))
WIN_STEPS = 128
REL_BUCKETS = 32
REL_MAX_DISTANCE = 2048
SSM_HEADS = 24
SSM_HEAD_DIM = 64
SSM_WIDTH = SSM_HEADS * SSM_HEAD_DIM
SSM_GROUPS = 4
HEADS_PER_GROUP = SSM_HEADS // SSM_GROUPS
GROUP_WIDTH = SSM_WIDTH // SSM_GROUPS
SSM_STATE = 128
SSM_CONV = 4
SSM_CHUNK = 128
CONV_CH = SSM_WIDTH + 2 * SSM_GROUPS * SSM_STATE
N_EXPERTS = 256
TOP_K = 8
N_EXPERT_GROUPS = 8
EXPERTS_PER_GROUP = N_EXPERTS // N_EXPERT_GROUPS
TOPK_GROUPS = 4
EXPERT_FF = 256
ROUTED_SCALE = 2.5
NORM_EPS = 1e-6

LANES = 128
SUBLANES = 8
NEG_BIG = -1e30
VMEM_LIMIT = 56 * 1024 * 1024


def _params(sem, vmem=VMEM_LIMIT):
    return pltpu.CompilerParams(dimension_semantics=sem, vmem_limit_bytes=vmem)


def _sigmoid(x):
    return 1.0 / (1.0 + jnp.exp(-x))


def _silu(x):
    return x * _sigmoid(x)


def _split3(x):
    hi = x.astype(BF16)
    r = x - hi.astype(F32)
    mid = r.astype(BF16)
    lo = (r - mid.astype(F32)).astype(BF16)
    return hi, mid, lo


def _dot(a, b):
    return jnp.dot(a, b, preferred_element_type=F32)


def _dot_nt(a, b):
    return lax.dot_general(a, b, (((1,), (1,)), ((), ())), preferred_element_type=F32)


def _dot_exact_rhs(a, b_exact):
    hi, mid, lo = _split3(a)
    return _dot(hi, b_exact) + _dot(mid, b_exact) + _dot(lo, b_exact)


def _dot_exact_lhs(a_exact, b):
    hi, mid, lo = _split3(b)
    return _dot(a_exact, hi) + _dot(a_exact, mid) + _dot(a_exact, lo)


def _adaln_kernel(c_ref, w_ref, b_ref, o_ref):
    s = _silu(c_ref[...]).astype(BF16)
    o_ref[...] = _dot(s, w_ref[...].astype(BF16)) + b_ref[...]


def _adaln(c, w_ada, b_ada):
    b, d = c.shape
    n = w_ada.shape[1]
    rows = SUBLANES
    c_pad = jnp.zeros((rows, d), F32).at[:b].set(c)
    tn = 1024
    out = pl.pallas_call(
        _adaln_kernel,
        grid=(n // tn,),
        in_specs=[pl.BlockSpec((rows, d), lambda j: (0, 0)),
                  pl.BlockSpec((d, tn), lambda j: (0, j)),
                  pl.BlockSpec((1, tn), lambda j: (0, j))],
        out_specs=pl.BlockSpec((rows, tn), lambda j: (0, j)),
        out_shape=jax.ShapeDtypeStruct((rows, n), F32),
        compiler_params=_params(("arbitrary",)),
        name="adaln",
    )(c_pad, w_ada, b_ada.reshape(1, n))
    return out[:b]


def _inproj_kernel(x_ref, shift_ref, scale_ref, g_ref, wqkv_ref, wz_ref, wxbc_ref, wdt_ref,
                   qg_ref, kg_ref, hmean_ref, q_ref, k_ref, v_ref, z_ref, xbc_ref, dt_ref):
    x = x_ref[...]
    ms = jnp.mean(x * x, axis=-1, keepdims=True)
    h = x * lax.rsqrt(ms + NORM_EPS) * g_ref[...]
    h = h * (1.0 + scale_ref[...]) + shift_ref[...]
    hb = h.astype(BF16)

    hmean = hmean_ref[...]

    def head_norm(t, gain):
        ss = _dot_exact_rhs(t * t, hmean)
        return t * lax.rsqrt(ss + NORM_EPS) * gain

    q = _dot(hb, wqkv_ref[:, 0:ATTN_WIDTH])
    q_ref[...] = head_norm(q, qg_ref[...]) * (HEAD_DIM ** -0.5)
    k = _dot(hb, wqkv_ref[:, ATTN_WIDTH:2 * ATTN_WIDTH])
    k_ref[...] = head_norm(k, kg_ref[...])
    v_ref[...] = _dot(hb, wqkv_ref[:, 2 * ATTN_WIDTH:3 * ATTN_WIDTH])
    for c0 in range(0, SSM_WIDTH, 512):
        z_ref[:, c0:c0 + 512] = _dot(hb, wz_ref[:, c0:c0 + 512])
    for c0 in range(0, CONV_CH, 512):
        xbc_ref[:, c0:c0 + 512] = _dot(hb, wxbc_ref[:, c0:c0 + 512])
    dt_ref[...] = _dot(hb, wdt_ref[...])


def _in_proj(x, shift, scale, gain, w_in, q_gain, k_gain, tm=256):
    b, s, d = x.shape
    t = b * s
    tiles_per_seq = s // tm
    w = w_in.astype(BF16)
    o_z = 3 * ATTN_WIDTH
    o_x = o_z + SSM_WIDTH
    o_dt = o_x + CONV_CH
    w_qkv, w_z, w_xbc = w[:, :o_z], w[:, o_z:o_x], w[:, o_x:o_dt]
    w_dt = jnp.zeros((d, LANES), BF16).at[:, :SSM_HEADS].set(w[:, o_dt:])
    head_of = np.arange(ATTN_WIDTH) // HEAD_DIM
    hmean = jnp.asarray((head_of[:, None] == head_of[None, :]).astype(np.float32) / HEAD_DIM, BF16)
    full = lambda shp: pl.BlockSpec(shp, lambda i: (0,) * len(shp))
    row = lambda n: pl.BlockSpec((tm, n), lambda i: (i, 0))
    per_batch = pl.BlockSpec((None, 1, d), lambda i: (i // tiles_per_seq, 0, 0))
    outs = pl.pallas_call(
        _inproj_kernel,
        grid=(t // tm,),
        in_specs=[row(d), per_batch, per_batch, full((1, d)),
                  full((d, o_z)), full((d, SSM_WIDTH)), full((d, CONV_CH)), full((d, LANES)),
                  full((1, ATTN_WIDTH)), full((1, ATTN_WIDTH)), full((ATTN_WIDTH, ATTN_WIDTH))],
        out_specs=[row(ATTN_WIDTH), row(ATTN_WIDTH), row(ATTN_WIDTH), row(SSM_WIDTH), row(CONV_CH), row(LANES)],
        out_shape=[jax.ShapeDtypeStruct((t, n), F32)
                   for n in (ATTN_WIDTH, ATTN_WIDTH, ATTN_WIDTH, SSM_WIDTH, CONV_CH, LANES)],
        compiler_params=_params(("arbitrary",)),
        name="in_proj",
    )(x.reshape(t, d), shift.reshape(b, 1, d), scale.reshape(b, 1, d), gain.reshape(1, d),
      w_qkv, w_z, w_xbc, w_dt,
      jnp.tile(q_gain, ATTN_HEADS).reshape(1, ATTN_WIDTH), jnp.tile(k_gain, ATTN_HEADS).reshape(1, ATTN_WIDTH), hmean)
    return outs


def _t5_causal_buckets(distance):
    n = np.maximum(distance, 0)
    max_exact = REL_BUCKETS // 2
    large = max_exact + (np.log(np.maximum(n, 1) / max_exact) / math.log(REL_MAX_DISTANCE / max_exact)
                         * (REL_BUCKETS - max_exact)).astype(np.int64)
    large = np.minimum(large, REL_BUCKETS - 1)
    return np.where(n < max_exact, n, large).astype(np.int32)


def _window_bias(rel_bias_table, dilation):
    qi = np.arange(WIN_STEPS)[:, None]
    kj = np.arange(2 * WIN_STEPS)[None, :]
    dist = qi + WIN_STEPS - kj
    band = (dist >= 0) & (dist <= WIN_STEPS)
    onehot = (_t5_causal_buckets(dist * dilation).reshape(-1, 1) == np.arange(REL_BUCKETS)[None, :]).astype(np.float32)
    bias = jnp.dot(rel_bias_table.astype(F32).T, jnp.asarray(onehot).T, precision=lax.Precision.HIGHEST)
    bias = bias.reshape(ATTN_HEADS, WIN_STEPS, 2 * WIN_STEPS)
    return jnp.where(jnp.asarray(band)[None], bias, NEG_BIG)


def _attn_kernel(q_ref, kp_ref, kc_ref, vp_ref, vc_ref, bias_ref, o_ref, l_ref):
    first = pl.program_id(3) == 0
    q = q_ref[...]
    kp, kc, vp, vc = kp_ref[...], kc_ref[...], vp_ref[...], vc_ref[...]
    col = lax.broadcasted_iota(I32, (WIN_STEPS, 2 * WIN_STEPS), 1)
    before_start = jnp.logical_and(col < WIN_STEPS, first)
    outs, lses = [], []
    for h in range(2):
        sl = slice(h * HEAD_DIM, (h + 1) * HEAD_DIM)
        qh = q[:, sl].astype(BF16)
        kh = jnp.concatenate([kp[:, sl], kc[:, sl]], axis=0).astype(BF16)
        vh = jnp.concatenate([vp[:, sl], vc[:, sl]], axis=0).astype(BF16)
        s = _dot_nt(qh, kh) + bias_ref[h]
        s = jnp.where(before_start, NEG_BIG, s)
        m = jnp.max(s, axis=-1, keepdims=True)
        p = jnp.exp(s - m)
        denom = jnp.sum(p, axis=-1, keepdims=True)
        o = _dot(p.astype(BF16), vh) / denom
        outs.append(o)
        lses.append(jnp.broadcast_to(m + jnp.log(denom), (WIN_STEPS, HEAD_DIM)))
    o_ref[...] = jnp.concatenate(outs, axis=1)
    l_ref[...] = jnp.concatenate(lses, axis=1)


def _attention_pattern(q, k, v, bias, dilation):
    b, s, w = q.shape
    d = dilation
    length = s // d
    nblk = length // WIN_STEPS
    pairs = ATTN_HEADS // 2
    view = lambda t: t.reshape(b, length, d * w)
    cur = pl.BlockSpec((None, WIN_STEPS, LANES), lambda bi, r, hp, i: (bi, i, r * pairs + hp))
    prev = pl.BlockSpec((None, WIN_STEPS, LANES), lambda bi, r, hp, i: (bi, jnp.maximum(i - 1, 0), r * pairs + hp))
    o, lse = pl.pallas_call(
        _attn_kernel,
        grid=(b, d, pairs, nblk),
        in_specs=[cur, prev, cur, prev, cur,
                  pl.BlockSpec((2, WIN_STEPS, 2 * WIN_STEPS), lambda bi, r, hp, i: (hp, 0, 0))],
        out_specs=[cur, cur],
        out_shape=[jax.ShapeDtypeStruct((b, length, d * w), F32)] * 2,
        compiler_params=_params(("arbitrary",) * 4),
        name=f"attn_d{d}",
    )(view(q), view(k), view(k), view(v), view(v), bias)
    return o.reshape(b, s, w), lse.reshape(b, s, w)


def _ssd_kernel(xbc_ref, halo_ref, z_ref, dtraw_ref, convw_ref, convb_ref, dtb_ref, alog_ref, dskip_ref, gain_ref,
                expand_ref, tril_ref, y_ref, state_ref):
    c = pl.program_id(1)

    @pl.when(c == 0)
    def _():
        state_ref[...] = jnp.zeros_like(state_ref)

    x = xbc_ref[...]
    halo = jnp.where(c == 0, 0.0, halo_ref[...])
    w = convw_ref[...]
    acc = x * w[SSM_CONV - 1:SSM_CONV, :] + convb_ref[...]
    row8 = lax.broadcasted_iota(I32, (SUBLANES, CONV_CH), 0)
    for shift in range(1, SSM_CONV):
        xs = pltpu.roll(x, shift, axis=0)
        hs = pltpu.roll(halo, shift, axis=0)
        head = jnp.where(row8 < shift, hs, xs[0:SUBLANES])
        xs = jnp.concatenate([head, xs[SUBLANES:]], axis=0)
        acc = acc + xs * w[SSM_CONV - 1 - shift:SSM_CONV - shift, :]
    act = _silu(acc)
    x_s = act[:, :SSM_WIDTH]
    bc0 = SSM_WIDTH
    cc0 = SSM_WIDTH + SSM_GROUPS * SSM_STATE

    t = dtraw_ref[...] + dtb_ref[...]
    dt = jnp.maximum(t, 0.0) + jnp.log(1.0 + jnp.exp(-jnp.abs(t)))
    a = dt * (-jnp.exp(alog_ref[...]))
    a_cs = _dot_exact_lhs(tril_ref[...], a)
    a_cs_t = a_cs.T
    a_last = a_cs[SSM_CHUNK - 1:SSM_CHUNK, :]
    expand = expand_ref[...]
    dt_e = _dot_exact_rhs(dt, expand)
    ea_e = _dot_exact_rhs(jnp.exp(a_cs), expand)
    dte_e = _dot_exact_rhs(jnp.exp(a_last - a_cs), expand)
    xdt = x_s * dt_e
    xw = (xdt * dte_e).astype(BF16)
    xdt_b = xdt.astype(BF16)

    li = lax.broadcasted_iota(I32, (SSM_CHUNK, SSM_CHUNK), 0)
    si = lax.broadcasted_iota(I32, (SSM_CHUNK, SSM_CHUNK), 1)
    causal = li >= si

    ys = []
    for g in range(SSM_GROUPS):
        gs = slice(g * GROUP_WIDTH, (g + 1) * GROUP_WIDTH)
        b_g = act[:, bc0 + g * SSM_STATE:bc0 + (g + 1) * SSM_STATE]
        c_g = act[:, cc0 + g * SSM_STATE:cc0 + (g + 1) * SSM_STATE].astype(BF16)
        cb = _dot_nt(c_g, b_g.astype(BF16))
        state = state_ref[g]
        y_off = _dot(c_g, state.astype(BF16)) * ea_e[:, gs]
        parts = []
        for j in range(HEADS_PER_GROUP):
            hh = g * HEADS_PER_GROUP + j
            seg = a_cs[:, hh:hh + 1] - a_cs_t[hh:hh + 1, :]
            decay = jnp.exp(jnp.where(causal, seg, NEG_BIG))
            m = (cb * decay).astype(BF16)
            parts.append(_dot(m, xdt_b[:, hh * SSM_HEAD_DIM:(hh + 1) * SSM_HEAD_DIM]))
        ys.append(jnp.concatenate(parts, axis=1) + y_off)
        state_ref[g] = state * ea_e[SSM_CHUNK - 1:SSM_CHUNK, gs] + _dot(b_g.T.astype(BF16), xw[:, gs])
    y = jnp.concatenate(ys, axis=1) + dskip_ref[...] * x_s
    y = y * _silu(z_ref[...])
    gain = gain_ref[...]
    for g in range(SSM_GROUPS):
        gs = slice(g * GROUP_WIDTH, (g + 1) * GROUP_WIDTH)
        yg = y[:, gs]
        ms = jnp.mean(yg * yg, axis=-1, keepdims=True)
        y_ref[:, gs] = yg * lax.rsqrt(ms + NORM_EPS) * gain[:, gs]


def _ssd(xbc, z, dt_raw, conv_w, conv_b, dt_bias, a_log, d_skip, norm_gain, b, s):
    t = b * s
    nc = s // SSM_CHUNK
    pad_heads = lambda v: jnp.zeros((1, LANES), F32).at[0, :SSM_HEADS].set(v)
    head_of_lane = np.arange(SSM_WIDTH) // SSM_HEAD_DIM
    expand = jnp.asarray((np.arange(LANES)[:, None] == head_of_lane[None, :]).astype(np.float32), BF16)
    tril = jnp.asarray(np.tril(np.ones((SSM_CHUNK, SSM_CHUNK), np.float32)), BF16)
    halo_blocks = SSM_CHUNK // SUBLANES
    chunk = lambda n: pl.BlockSpec((SSM_CHUNK, n), lambda bi, c: (bi * nc + c, 0))
    full = lambda shp: pl.BlockSpec(shp, lambda bi, c: (0,) * len(shp))
    halo = pl.BlockSpec((SUBLANES, CONV_CH), lambda bi, c: (jnp.maximum((bi * nc + c) * halo_blocks - 1, 0), 0))
    return pl.pallas_call(
        _ssd_kernel,
        grid=(b, nc),
        in_specs=[chunk(CONV_CH), halo, chunk(SSM_WIDTH), chunk(LANES),
                  full((SSM_CONV, CONV_CH)), full((1, CONV_CH)), full((1, LANES)), full((1, LANES)),
                  full((1, SSM_WIDTH)), full((1, SSM_WIDTH)), full((LANES, SSM_WIDTH)), full((SSM_CHUNK, SSM_CHUNK))],
        out_specs=chunk(SSM_WIDTH),
        out_shape=jax.ShapeDtypeStruct((t, SSM_WIDTH), F32),
        scratch_shapes=[pltpu.VMEM((SSM_GROUPS, SSM_STATE, GROUP_WIDTH), F32)],
        compiler_params=_params(("arbitrary", "arbitrary")),
        name="ssd",
    )(xbc, xbc, z, dt_raw, conv_w, conv_b.reshape(1, CONV_CH), pad_heads(dt_bias), pad_heads(a_log),
      jnp.repeat(d_skip, SSM_HEAD_DIM).reshape(1, SSM_WIDTH), norm_gain.reshape(1, SSM_WIDTH), expand, tril)


def _outproj_kernel(o1_ref, o2_ref, o3_ref, l1_ref, l2_ref, l3_ref, ssm_ref, x_ref, gate_ref, shift_ref, scale_ref,
                    g_ref, wa_ref, ws_ref, x1_ref, h2_ref):
    l1, l2, l3 = l1_ref[...], l2_ref[...], l3_ref[...]
    m = jnp.maximum(jnp.maximum(l1, l2), l3)
    e1, e2, e3 = jnp.exp(l1 - m), jnp.exp(l2 - m), jnp.exp(l3 - m)
    attn = (e1 * o1_ref[...] + e2 * o2_ref[...] + e3 * o3_ref[...]) / (e1 + e2 + e3)
    mixed = _dot(attn.astype(BF16), wa_ref[...]) + _dot(ssm_ref[...].astype(BF16), ws_ref[...])
    x1 = x_ref[...] + gate_ref[...] * mixed
    x1_ref[...] = x1
    ms = jnp.mean(x1 * x1, axis=-1, keepdims=True)
    h = x1 * lax.rsqrt(ms + NORM_EPS) * g_ref[...]
    h2_ref[...] = h * (1.0 + scale_ref[...]) + shift_ref[...]


def _out_proj(os_, ls_, ssm, x, gate, shift, scale, gain, w_out, b, s, tm=256):
    t = b * s
    d = D_MODEL
    tiles_per_seq = s // tm
    w = w_out.astype(BF16)
    row = lambda n: pl.BlockSpec((tm, n), lambda i: (i, 0))
    full = lambda shp: pl.BlockSpec(shp, lambda i: (0,) * len(shp))
    per_batch = pl.BlockSpec((None, 1, d), lambda i: (i // tiles_per_seq, 0, 0))
    return pl.pallas_call(
        _outproj_kernel,
        grid=(t // tm,),
        in_specs=[row(ATTN_WIDTH)] * 6 + [row(SSM_WIDTH), row(d), per_batch, per_batch, per_batch, full((1, d)),
                                          full((ATTN_WIDTH, d)), full((SSM_WIDTH, d))],
        out_specs=[row(d), row(d)],
        out_shape=[jax.ShapeDtypeStruct((t, d), F32)] * 2,
        compiler_params=_params(("arbitrary",)),
        name="out_proj",
    )(*[o.reshape(t, ATTN_WIDTH) for o in os_], *[l.reshape(t, ATTN_WIDTH) for l in ls_], ssm, x.reshape(t, d),
      gate.reshape(b, 1, d), shift.reshape(b, 1, d), scale.reshape(b, 1, d), gain.reshape(1, d),
      w[:ATTN_WIDTH], w[ATTN_WIDTH:])


def _mixer_sublayer(x, mod, norm_mix_gain, w_in, q_norm_gain, k_norm_gain, rel_bias_table, conv_w, conv_b, dt_bias,
                    a_log, d_skip, ssm_norm_gain, w_out, norm_ffn_gain):
    b, s, d = x.shape
    shift_m, scale_m, gate_m, shift_f, scale_f, _ = jnp.split(mod, 6, axis=-1)
    q, k, v, z, xbc, dt_raw = _in_proj(x, shift_m, scale_m, norm_mix_gain, w_in, q_norm_gain, k_norm_gain)
    os_, ls_ = [], []
    for _, dilation in PATTERNS:
        o, lse = _attention_pattern(q.reshape(b, s, ATTN_WIDTH), k.reshape(b, s, ATTN_WIDTH),
                                    v.reshape(b, s, ATTN_WIDTH), _window_bias(rel_bias_table, dilation), dilation)
        os_.append(o)
        ls_.append(lse)
    ssm = _ssd(xbc, z, dt_raw, conv_w, conv_b, dt_bias, a_log, d_skip, ssm_norm_gain, b, s)
    return _out_proj(os_, ls_, ssm, x, gate_m, shift_f, scale_f, norm_ffn_gain, w_out, b, s)


def _first_argmax(v, iota, limit):
    m = jnp.max(v, axis=0, keepdims=True)
    idx = jnp.min(jnp.where(v == m, iota, limit), axis=0, keepdims=True)
    return m, idx


def _router_kernel(h_ref, wt_ref, bias_ref, upper_ref, eidx_ref, rank_ref, gate_ref, counts_ref, carry_ref):
    @pl.when(pl.program_id(0) == 0)
    def _():
        carry_ref[...] = jnp.zeros_like(carry_ref)

    tm = h_ref.shape[0]
    h = h_ref[...]
    wt = wt_ref[...]
    h_hi = h.astype(BF16)
    h_lo = (h - h_hi.astype(F32)).astype(BF16)
    w_hi = wt.astype(BF16)
    w_lo = (wt - w_hi.astype(F32)).astype(BF16)
    logits = _dot_nt(w_hi, h_hi) + _dot_nt(w_hi, h_lo) + _dot_nt(w_lo, h_hi)
    scores = _sigmoid(logits)
    choice = scores + bias_ref[...]
    neg_inf = -jnp.inf

    iota_g = lax.broadcasted_iota(I32, (EXPERTS_PER_GROUP, tm), 0).astype(F32)
    group_rows = []
    for g in range(N_EXPERT_GROUPS):
        v = choice[g * EXPERTS_PER_GROUP:(g + 1) * EXPERTS_PER_GROUP]
        m1, i1 = _first_argmax(v, iota_g, float(EXPERTS_PER_GROUP))
        m2 = jnp.max(jnp.where(iota_g == i1, neg_inf, v), axis=0, keepdims=True)
        group_rows.append(m1 + m2)
    group_scores = jnp.concatenate(group_rows, axis=0)

    iota_n = lax.broadcasted_iota(I32, (N_EXPERT_GROUPS, tm), 0).astype(F32)
    chosen = jnp.zeros((N_EXPERT_GROUPS, tm), F32)
    for _ in range(TOPK_GROUPS):
        _, gi = _first_argmax(group_scores, iota_n, float(N_EXPERT_GROUPS))
        hit = iota_n == gi
        chosen = jnp.where(hit, 1.0, chosen)
        group_scores = jnp.where(hit, neg_inf, group_scores)

    masked = jnp.concatenate(
        [jnp.where(chosen[g:g + 1] > 0.0, choice[g * EXPERTS_PER_GROUP:(g + 1) * EXPERTS_PER_GROUP], neg_inf)
         for g in range(N_EXPERT_GROUPS)], axis=0)

    iota_e = lax.broadcasted_iota(I32, (N_EXPERTS, tm), 0).astype(F32)
    picked, gates = [], []
    onehot = jnp.zeros((N_EXPERTS, tm), F32)
    for _ in range(TOP_K):
        _, ei = _first_argmax(masked, iota_e, float(N_EXPERTS))
        hit = iota_e == ei
        gates.append(jnp.sum(jnp.where(hit, scores, 0.0), axis=0, keepdims=True))
        masked = jnp.where(hit, neg_inf, masked)
        onehot = jnp.where(hit, 1.0, onehot)
        picked.append(ei)
    gate_sum = gates[0]
    for gk in gates[1:]:
        gate_sum = gate_sum + gk

    base = _dot(onehot.astype(BF16), upper_ref[...]) + carry_ref[...]
    ranks = [jnp.sum(jnp.where(iota_e == ei, base, 0.0), axis=0, keepdims=True) for ei in picked]
    carry_ref[...] = carry_ref[...] + jnp.sum(onehot, axis=1, keepdims=True)

    eidx_ref[...] = jnp.concatenate(picked, axis=0).astype(I32)
    rank_ref[...] = jnp.concatenate(ranks, axis=0).astype(I32)
    gate_ref[...] = jnp.concatenate([gk / gate_sum * ROUTED_SCALE for gk in gates], axis=0)
    counts_ref[...] = carry_ref[...].astype(I32)


def _router(h2, w_router, router_bias, tm=256):
    t, d = h2.shape
    upper = jnp.asarray(np.triu(np.ones((tm, tm), np.float32), 1), BF16)
    tok = pl.BlockSpec((TOP_K, tm), lambda i: (0, i))
    full = lambda shp: pl.BlockSpec(shp, lambda i: (0,) * len(shp))
    return pl.pallas_call(
        _router_kernel,
        grid=(t // tm,),
        in_specs=[pl.BlockSpec((tm, d), lambda i: (i, 0)), full((N_EXPERTS, d)), full((N_EXPERTS, 1)), full((tm, tm))],
        out_specs=[tok, tok, tok, full((N_EXPERTS, 1))],
        out_shape=[jax.ShapeDtypeStruct((TOP_K, t), I32), jax.ShapeDtypeStruct((TOP_K, t), I32),
                   jax.ShapeDtypeStruct((TOP_K, t), F32), jax.ShapeDtypeStruct((N_EXPERTS, 1), I32)],
        scratch_shapes=[pltpu.VMEM((N_EXPERTS, 1), F32)],
        compiler_params=_params(("arbitrary",)),
        name="router",
    )(h2, w_router.T, router_bias.reshape(N_EXPERTS, 1), upper)


def _positions_kernel(counts_ref, lower_ref, eidx_ref, rank_ref, pos_ref):
    tm = eidx_ref.shape[1]
    counts = jnp.broadcast_to(counts_ref[...].astype(F32), (N_EXPERTS, LANES))
    offsets = _dot_exact_lhs(lower_ref[...], counts)[:, 0:1]
    iota_e = lax.broadcasted_iota(I32, (N_EXPERTS, tm), 0).astype(F32)
    e = eidx_ref[...].astype(F32)
    rows = [jnp.sum(jnp.where(iota_e == e[k:k + 1], offsets, 0.0), axis=0, keepdims=True) for k in range(TOP_K)]
    pos_ref[0] = jnp.concatenate(rows, axis=0).astype(I32) + rank_ref[...]


def _positions(counts, eidx, rank, tm):
    t = eidx.shape[1]
    lower = jnp.asarray(np.tril(np.ones((N_EXPERTS, N_EXPERTS), np.float32), -1), BF16)
    tok = pl.BlockSpec((TOP_K, tm), lambda i: (0, i))
    return pl.pallas_call(
        _positions_kernel,
        grid=(t // tm,),
        in_specs=[pl.BlockSpec((N_EXPERTS, 1), lambda i: (0, 0)), pl.BlockSpec((N_EXPERTS, N_EXPERTS), lambda i: (0, 0)),
                  tok, tok],
        out_specs=pl.BlockSpec((1, TOP_K, tm), lambda i: (i, 0, 0)),
        out_shape=jax.ShapeDtypeStruct((t // tm, TOP_K, tm), I32),
        compiler_params=_params(("arbitrary",)),
        name="positions",
    )(counts, lower, eidx, rank)


def _dispatch_kernel(pos_ref, h_ref, xs_ref, sem):
    tm = h_ref.shape[0]

    def row_copy(t, k):
        return pltpu.make_async_copy(h_ref.at[pl.ds(t, 1)], xs_ref.at[pl.ds(pos_ref[0, k, t], 1)], sem)

    def start(t, carry):
        for k in range(TOP_K):
            row_copy(t, k).start(priority=k % 2)
        return carry

    def wait(t, carry):
        for k in range(TOP_K):
            row_copy(t, k).wait()
        return carry

    lax.fori_loop(0, tm, start, 0)
    lax.fori_loop(0, tm, wait, 0)


def _dispatch(h2, pos_tiles, tm):
    t, d = h2.shape
    return pl.pallas_call(
        _dispatch_kernel,
        grid=(t // tm,),
        in_specs=[pl.BlockSpec((1, TOP_K, tm), lambda i: (i, 0, 0), memory_space=pltpu.SMEM),
                  pl.BlockSpec((tm, d), lambda i: (i, 0))],
        out_specs=pl.BlockSpec(memory_space=pl.ANY),
        out_shape=jax.ShapeDtypeStruct((t * TOP_K, d), h2.dtype),
        scratch_shapes=[pltpu.SemaphoreType.DMA(())],
        compiler_params=_params(("arbitrary",)),
        name="dispatch",
    )(pos_tiles, h2)


def _experts_kernel(tile_ref, exp_ref, lo_ref, hi_ref, xs_ref, wg_ref, wu_ref, wd_ref, ys_ref):
    w = pl.program_id(0)
    tmg = xs_ref.shape[0]
    tile = tile_ref[w]
    lo, hi = lo_ref[w], hi_ref[w]
    new_tile = jnp.logical_or(w == 0, tile_ref[jnp.maximum(w - 1, 0)] != tile)

    @pl.when(new_tile)
    def _():
        ys_ref[...] = jnp.zeros_like(ys_ref)

    @pl.when(hi > lo)
    def _():
        x = xs_ref[...].astype(BF16)
        g = _dot(x, wg_ref[...].astype(BF16))
        u = _dot(x, wu_ref[...].astype(BF16))
        y = _dot((_silu(g) * u).astype(BF16), wd_ref[...].astype(BF16))
        row = tile * tmg + lax.broadcasted_iota(I32, (tmg, 1), 0)
        inside = jnp.logical_and(row >= lo, row < hi)
        ys_ref[...] = jnp.where(inside, y, ys_ref[...])


def _group_metadata(counts, n_rows, tmg):
    n_tiles = n_rows // tmg
    n_work = n_tiles + N_EXPERTS
    ends = jnp.cumsum(counts)
    starts = ends - counts
    first_tile = starts // tmg
    n_items = jnp.where(counts > 0, (ends - 1) // tmg - first_tile + 1, 0)
    item_ends = jnp.cumsum(n_items)
    w = jnp.arange(n_work, dtype=I32)
    valid = w < item_ends[-1]
    e = jnp.minimum(jnp.searchsorted(item_ends, w, side="right"), N_EXPERTS - 1).astype(I32)
    tile = first_tile[e] + (w - (item_ends[e] - n_items[e]))
    lo = jnp.maximum(starts[e], tile * tmg)
    hi = jnp.minimum(ends[e], (tile + 1) * tmg)
    last_e = jnp.max(jnp.where(counts > 0, jnp.arange(N_EXPERTS, dtype=I32), 0))
    tile = jnp.where(valid, tile, n_tiles - 1).astype(I32)
    e = jnp.where(valid, e, last_e).astype(I32)
    lo = jnp.where(valid, lo, 0).astype(I32)
    hi = jnp.where(valid, hi, 0).astype(I32)
    return tile, e, lo, hi


def _experts(xs, counts, w_gate, w_up, w_down, tmg=256):
    n, d = xs.shape
    tile, e, lo, hi = _group_metadata(counts, n, tmg)
    n_work = tile.shape[0]
    grid_spec = pltpu.PrefetchScalarGridSpec(
        num_scalar_prefetch=4,
        grid=(n_work,),
        in_specs=[pl.BlockSpec((tmg, d), lambda w, tile, e, lo, hi: (tile[w], 0)),
                  pl.BlockSpec((None, d, EXPERT_FF), lambda w, tile, e, lo, hi: (e[w], 0, 0)),
                  pl.BlockSpec((None, d, EXPERT_FF), lambda w, tile, e, lo, hi: (e[w], 0, 0)),
                  pl.BlockSpec((None, EXPERT_FF, d), lambda w, tile, e, lo, hi: (e[w], 0, 0))],
        out_specs=pl.BlockSpec((tmg, d), lambda w, tile, e, lo, hi: (tile[w], 0)),
    )
    return pl.pallas_call(
        _experts_kernel,
        grid_spec=grid_spec,
        out_shape=jax.ShapeDtypeStruct((n, d), F32),
        compiler_params=_params(("arbitrary",)),
        name="experts",
    )(tile, e, lo, hi, xs, w_gate, w_up, w_down)


def _combine_kernel(pos_ref, gates_ref, h_ref, x1_ref, gatef_ref, wg_ref, wu_ref, wd_ref, ys_ref, out_ref, buf, sem):
    tm = h_ref.shape[0]

    def row_copy(t, k):
        return pltpu.make_async_copy(ys_ref.at[pl.ds(pos_ref[0, k, t], 1)], buf.at[k, pl.ds(t, 1)], sem)

    def start(t, carry):
        for k in range(TOP_K):
            row_copy(t, k).start(priority=k % 2)
        return carry

    def wait(t, carry):
        for k in range(TOP_K):
            row_copy(t, k).wait()
        return carry

    lax.fori_loop(0, tm, start, 0)
    hb = h_ref[...].astype(BF16)
    shared = _dot((_silu(_dot(hb, wg_ref[...])) * _dot(hb, wu_ref[...])).astype(BF16), wd_ref[...])
    lax.fori_loop(0, tm, wait, 0)
    gates = gates_ref[...]
    routed = buf[0] * gates[:, 0:1]
    for k in range(1, TOP_K):
        routed = routed + buf[k] * gates[:, k:k + 1]
    out_ref[...] = x1_ref[...] + gatef_ref[...] * (shared + routed)


def _combine(ys, pos_tiles, gates_t, h2, x1, gate_f, w_gate_s, w_up_s, w_down_s, b, s, tm):
    t, d = h2.shape
    tiles_per_seq = s // tm
    row = lambda n: pl.BlockSpec((tm, n), lambda i: (i, 0))
    full = lambda shp: pl.BlockSpec(shp, lambda i: (0,) * len(shp))
    return pl.pallas_call(
        _combine_kernel,
        grid=(t // tm,),
        in_specs=[pl.BlockSpec((1, TOP_K, tm), lambda i: (i, 0, 0), memory_space=pltpu.SMEM),
                  row(TOP_K), row(d), row(d),
                  pl.BlockSpec((None, 1, d), lambda i: (i // tiles_per_seq, 0, 0)),
                  full((d, EXPERT_FF)), full((d, EXPERT_FF)), full((EXPERT_FF, d)),
                  pl.BlockSpec(memory_space=pl.ANY)],
        out_specs=row(d),
        out_shape=jax.ShapeDtypeStruct((t, d), F32),
        scratch_shapes=[pltpu.VMEM((TOP_K, tm, d), F32), pltpu.SemaphoreType.DMA(())],
        compiler_params=_params(("arbitrary",)),
        name="combine",
    )(pos_tiles, gates_t, h2, x1, gate_f.reshape(b, 1, d),
      w_gate_s.astype(BF16), w_up_s.astype(BF16), w_down_s.astype(BF16), ys)


def _moe_sublayer(x1, h2, gate_f, w_router, router_bias, w_gate, w_up, w_down, w_gate_s, w_up_s, w_down_s, b, s,
                  tm=256):
    t = b * s
    eidx, rank, gates, counts = _router(h2, w_router, router_bias)
    pos_tiles = _positions(counts, eidx, rank, tm)
    xs = _dispatch(h2, pos_tiles, tm)
    ys = _experts(xs, counts[:, 0], w_gate, w_up, w_down)
    return _combine(ys, pos_tiles, gates.T, h2, x1, gate_f, w_gate_s, w_up_s, w_down_s, b, s, tm)


def kernel(x, c, w_ada, b_ada, norm_mix_gain, w_in, q_norm_gain, k_norm_gain, rel_bias_table, conv_w, conv_b, dt_bias,
           a_log, d_skip, ssm_norm_gain, w_out, norm_ffn_gain, w_router, router_bias, w_gate_experts, w_up_experts,
           w_down_experts, w_gate_shared, w_up_shared, w_down_shared):
    b, s, d = x.shape
    for layer in range(w_ada.shape[0]):
        mod = _adaln(c, w_ada[layer], b_ada[layer])
        x1, h2 = _mixer_sublayer(x, mod, norm_mix_gain[layer], w_in[layer], q_norm_gain[layer], k_norm_gain[layer],
                                 rel_bias_table, conv_w[layer], conv_b[layer], dt_bias[layer], a_log[layer],
                                 d_skip[layer], ssm_norm_gain[layer], w_out[layer], norm_ffn_gain[layer])
        gate_f = mod[:, 5 * d:]
        out = _moe_sublayer(x1, h2, gate_f, w_router[layer], router_bias[layer], w_gate_experts[layer],
                            w_up_experts[layer], w_down_experts[layer], w_gate_shared[layer], w_up_shared[layer],
                            w_down_shared[layer], b, s)
        x = out.reshape(b, s, d)
    return x
```

```python
import functools
import math

import numpy as np
import jax
import jax.numpy as jnp
from jax import lax
from jax.experimental import pallas as pl
from jax.experimental.pallas import tpu as pltpu

F32 = jnp.float32
BF16 = jnp.bfloat16
I32 = jnp.int32

D_MODEL = 1024
ATTN_HEADS = 8
HEAD_DIM = 64
ATTN_WIDTH = ATTN_HEADS * HEAD_DIM
PATTERNS = ((128, 1), (512, 4), (2048, 16))
WIN_STEPS = 128
REL_BUCKETS = 32
REL_MAX_DISTANCE = 2048
SSM_HEADS = 24
SSM_HEAD_DIM = 64
SSM_WIDTH = SSM_HEADS * SSM_HEAD_DIM
SSM_GROUPS = 4
HEADS_PER_GROUP = SSM_HEADS // SSM_GROUPS
GROUP_WIDTH = SSM_WIDTH // SSM_GROUPS
SSM_STATE = 128
SSM_CONV = 4
SSM_CHUNK = 128
CONV_CH = SSM_WIDTH + 2 * SSM_GROUPS * SSM_STATE
N_EXPERTS = 256
TOP_K = 8
N_EXPERT_GROUPS = 8
EXPERTS_PER_GROUP = N_EXPERTS // N_EXPERT_GROUPS
TOPK_GROUPS = 4
EXPERT_FF = 256
ROUTED_SCALE = 2.5
NORM_EPS = 1e-6

LANES = 128
SUBLANES = 8
NEG_BIG = -1e30
VMEM_LIMIT = 56 * 1024 * 1024


def _params(sem, vmem=VMEM_LIMIT):
    return pltpu.CompilerParams(dimension_semantics=sem, vmem_limit_bytes=vmem)


def _sigmoid(x):
    return 1.0 / (1.0 + jnp.exp(-x))


def _silu(x):
    return x * _sigmoid(x)


def _split3(x):
    hi = x.astype(BF16)
    r = x - hi.astype(F32)
    mid = r.astype(BF16)
    lo = (r - mid.astype(F32)).astype(BF16)
    return hi, mid, lo


def _dot(a, b):
    return jnp.dot(a, b, preferred_element_type=F32)


def _dot_nt(a, b):
    return lax.dot_general(a, b, (((1,), (1,)), ((), ())), preferred_element_type=F32)


def _dot_exact_rhs(a, b_exact):
    hi, mid, lo = _split3(a)
    return _dot(hi, b_exact) + _dot(mid, b_exact) + _dot(lo, b_exact)


def _dot_exact_lhs(a_exact, b):
    hi, mid, lo = _split3(b)
    return _dot(a_exact, hi) + _dot(a_exact, mid) + _dot(a_exact, lo)


def _adaln_kernel(c_ref, w_ref, b_ref, o_ref):
    s = _silu(c_ref[...]).astype(BF16)
    o_ref[...] = _dot(s, w_ref[...].astype(BF16)) + b_ref[...]


def _adaln(c, w_ada, b_ada):
    b, d = c.shape
    n = w_ada.shape[1]
    rows = SUBLANES
    c_pad = jnp.zeros((rows, d), F32).at[:b].set(c)
    tn = 1024
    out = pl.pallas_call(
        _adaln_kernel,
        grid=(n // tn,),
        in_specs=[pl.BlockSpec((rows, d), lambda j: (0, 0)),
                  pl.BlockSpec((d, tn), lambda j: (0, j)),
                  pl.BlockSpec((1, tn), lambda j: (0, j))],
        out_specs=pl.BlockSpec((rows, tn), lambda j: (0, j)),
        out_shape=jax.ShapeDtypeStruct((rows, n), F32),
        compiler_params=_params(("arbitrary",)),
        name="adaln",
    )(c_pad, w_ada, b_ada.reshape(1, n))
    return out[:b]


def _inproj_kernel(x_ref, shift_ref, scale_ref, g_ref, wqkv_ref, wz_ref, wxbc_ref, wdt_ref,
                   qg_ref, kg_ref, hmean_ref, q_ref, k_ref, v_ref, z_ref, xbc_ref, dt_ref):
    x = x_ref[...]
    ms = jnp.mean(x * x, axis=-1, keepdims=True)
    h = x * lax.rsqrt(ms + NORM_EPS) * g_ref[...]
    h = h * (1.0 + scale_ref[...]) + shift_ref[...]
    hb = h.astype(BF16)

    hmean = hmean_ref[...]

    def head_norm(t, gain):
        ss = _dot_exact_rhs(t * t, hmean)
        return t * lax.rsqrt(ss + NORM_EPS) * gain

    q = _dot(hb, wqkv_ref[:, 0:ATTN_WIDTH])
    q_ref[...] = head_norm(q, qg_ref[...]) * (HEAD_DIM ** -0.5)
    k = _dot(hb, wqkv_ref[:, ATTN_WIDTH:2 * ATTN_WIDTH])
    k_ref[...] = head_norm(k, kg_ref[...])
    v_ref[...] = _dot(hb, wqkv_ref[:, 2 * ATTN_WIDTH:3 * ATTN_WIDTH])
    for c0 in range(0, SSM_WIDTH, 512):
        z_ref[:, c0:c0 + 512] = _dot(hb, wz_ref[:, c0:c0 + 512])
    for c0 in range(0, CONV_CH, 512):
        xbc_ref[:, c0:c0 + 512] = _dot(hb, wxbc_ref[:, c0:c0 + 512])
    dt_ref[...] = _dot(hb, wdt_ref[...])


def _in_proj(x, shift, scale, gain, w_in, q_gain, k_gain, tm=256):
    b, s, d = x.shape
    t = b * s
    tiles_per_seq = s // tm
    w = w_in.astype(BF16)
    o_z = 3 * ATTN_WIDTH
    o_x = o_z + SSM_WIDTH
    o_dt = o_x + CONV_CH
    w_qkv, w_z, w_xbc = w[:, :o_z], w[:, o_z:o_x], w[:, o_x:o_dt]
    w_dt = jnp.zeros((d, LANES), BF16).at[:, :SSM_HEADS].set(w[:, o_dt:])
    head_of = np.arange(ATTN_WIDTH) // HEAD_DIM
    hmean = jnp.asarray((head_of[:, None] == head_of[None, :]).astype(np.float32) / HEAD_DIM, BF16)
    full = lambda shp: pl.BlockSpec(shp, lambda i: (0,) * len(shp))
    row = lambda n: pl.BlockSpec((tm, n), lambda i: (i, 0))
    per_batch = pl.BlockSpec((None, 1, d), lambda i: (i // tiles_per_seq, 0, 0))
    outs = pl.pallas_call(
        _inproj_kernel,
        grid=(t // tm,),
        in_specs=[row(d), per_batch, per_batch, full((1, d)),
                  full((d, o_z)), full((d, SSM_WIDTH)), full((d, CONV_CH)), full((d, LANES)),
                  full((1, ATTN_WIDTH)), full((1, ATTN_WIDTH)), full((ATTN_WIDTH, ATTN_WIDTH))],
        out_specs=[row(ATTN_WIDTH), row(ATTN_WIDTH), row(ATTN_WIDTH), row(SSM_WIDTH), row(CONV_CH), row(LANES)],
        out_shape=[jax.ShapeDtypeStruct((t, n), F32)
                   for n in (ATTN_WIDTH, ATTN_WIDTH, ATTN_WIDTH, SSM_WIDTH, CONV_CH, LANES)],
        compiler_params=_params(("arbitrary",)),
        name="in_proj",
    )(x.reshape(t, d), shift.reshape(b, 1, d), scale.reshape(b, 1, d), gain.reshape(1, d),
      w_qkv, w_z, w_xbc, w_dt,
      jnp.tile(q_gain, ATTN_HEADS).reshape(1, ATTN_WIDTH), jnp.tile(k_gain, ATTN_HEADS).reshape(1, ATTN_WIDTH), hmean)
    return outs


def _t5_causal_buckets(distance):
    n = np.maximum(distance, 0)
    max_exact = REL_BUCKETS // 2
    large = max_exact + (np.log(np.maximum(n, 1) / max_exact) / math.log(REL_MAX_DISTANCE / max_exact)
                         * (REL_BUCKETS - max_exact)).astype(np.int64)
    large = np.minimum(large, REL_BUCKETS - 1)
    return np.where(n < max_exact, n, large).astype(np.int32)


def _window_bias(rel_bias_table, dilation):
    qi = np.arange(WIN_STEPS)[:, None]
    kj = np.arange(2 * WIN_STEPS)[None, :]
    dist = qi + WIN_STEPS - kj
    band = (dist >= 0) & (dist <= WIN_STEPS)
    onehot = (_t5_causal_buckets(dist * dilation).reshape(-1, 1) == np.arange(REL_BUCKETS)[None, :]).astype(np.float32)
    bias = jnp.dot(rel_bias_table.astype(F32).T, jnp.asarray(onehot).T, precision=lax.Precision.HIGHEST)
    bias = bias.reshape(ATTN_HEADS, WIN_STEPS, 2 * WIN_STEPS)
    return jnp.where(jnp.asarray(band)[None], bias, NEG_BIG)


ATTN_TOKENS = max(w for w, _ in PATTERNS)
ATTN_UNROLL = 4


def _attn_kernel(q_ref, kp_ref, kc_ref, vp_ref, vc_ref, bias_ref, out_ref, kw, vw, o_acc, l_acc):
    tb = ATTN_TOKENS
    first = pl.program_id(2) == 0
    kw[0:tb] = kp_ref[...]
    kw[tb:2 * tb] = kc_ref[...]
    vw[0:tb] = vp_ref[...]
    vw[tb:2 * tb] = vc_ref[...]
    lane = lax.broadcasted_iota(I32, (WIN_STEPS, LANES), 1)
    head0 = lane < HEAD_DIM
    col = lax.broadcasted_iota(I32, (WIN_STEPS, 2 * WIN_STEPS), 1)
    in_prev = col < WIN_STEPS

    for p, (_, d) in enumerate(PATTERNS):
        shift = d.bit_length() - 1
        n_blocks = tb // WIN_STEPS

        def rows(start, n, d=d):
            return pl.ds(start, n, stride=d) if d > 1 else pl.ds(start, n)

        def body(it, carry, p=p, d=d, shift=shift, rows=rows):
            for u in range(ATTN_UNROLL):
                idx = it * ATTN_UNROLL + u
                r = jnp.bitwise_and(idx, d - 1)
                j = jnp.right_shift(idx, shift)
                qs = j * (WIN_STEPS * d) + r
                q = q_ref[rows(qs, WIN_STEPS), :]
                k = kw[rows(tb + qs - WIN_STEPS * d, 2 * WIN_STEPS), :].astype(BF16)
                v = vw[rows(tb + qs - WIN_STEPS * d, 2 * WIN_STEPS), :].astype(BF16)
                no_prev = jnp.logical_and(in_prev, jnp.logical_and(first, j == 0))
                o_h, lse_h = [], []
                for h in range(2):
                    qh = jnp.where(head0 if h == 0 else jnp.logical_not(head0), q, 0.0).astype(BF16)
                    s = _dot_nt(qh, k) + bias_ref[p, h]
                    s = jnp.where(no_prev, NEG_BIG, s)
                    m = jnp.max(s, axis=-1, keepdims=True)
                    e = jnp.exp(s - m)
                    denom = jnp.sum(e, axis=-1, keepdims=True)
                    o_h.append(_dot(e.astype(BF16), v) / denom)
                    lse_h.append(m + jnp.log(denom))
                o_acc[p, rows(qs, WIN_STEPS), :] = jnp.where(head0, o_h[0], o_h[1])
                l_acc[p, rows(qs, WIN_STEPS), :] = jnp.where(head0, lse_h[0], lse_h[1])
            return carry

        lax.fori_loop(0, n_blocks // ATTN_UNROLL, body, 0)

    chunk = 256
    for c0 in range(0, tb, chunk):
        l1, l2, l3 = (l_acc[p, c0:c0 + chunk, :] for p in range(3))
        m = jnp.maximum(jnp.maximum(l1, l2), l3)
        e1, e2, e3 = jnp.exp(l1 - m), jnp.exp(l2 - m), jnp.exp(l3 - m)
        num = e1 * o_acc[0, c0:c0 + chunk, :] + e2 * o_acc[1, c0:c0 + chunk, :] + e3 * o_acc[2, c0:c0 + chunk, :]
        out_ref[c0:c0 + chunk, :] = num / (e1 + e2 + e3)


def _attention(q, k, v, bias):
    b, s, w = q.shape
    tb = ATTN_TOKENS
    pairs = ATTN_HEADS // 2
    cur = pl.BlockSpec((None, tb, LANES), lambda bi, hp, i: (bi, i, hp))
    prev = pl.BlockSpec((None, tb, LANES), lambda bi, hp, i: (bi, jnp.maximum(i - 1, 0), hp))
    return pl.pallas_call(
        _attn_kernel,
        grid=(b, pairs, s // tb),
        in_specs=[cur, prev, cur, prev, cur,
                  pl.BlockSpec((len(PATTERNS), 2, WIN_STEPS, 2 * WIN_STEPS), lambda bi, hp, i: (0, hp, 0, 0))],
        out_specs=cur,
        out_shape=jax.ShapeDtypeStruct((b, s, w), F32),
        scratch_shapes=[pltpu.VMEM((2 * tb, LANES), F32), pltpu.VMEM((2 * tb, LANES), F32),
                        pltpu.VMEM((len(PATTERNS), tb, LANES), F32), pltpu.VMEM((len(PATTERNS), tb, LANES), F32)],
        compiler_params=_params(("arbitrary",) * 3),
        name="attention",
    )(q, k, k, v, v, bias)


def _ssd_kernel(xbc_ref, halo_ref, z_ref, dtraw_ref, convw_ref, convb_ref, dtb_ref, alog_ref, dskip_ref, gain_ref,
                expand_ref, tril_ref, y_ref, state_ref):
    c = pl.program_id(1)

    @pl.when(c == 0)
    def _():
        state_ref[...] = jnp.zeros_like(state_ref)

    x = xbc_ref[...]
    halo = jnp.where(c == 0, 0.0, halo_ref[...])
    w = convw_ref[...]
    acc = x * w[SSM_CONV - 1:SSM_CONV, :] + convb_ref[...]
    row8 = lax.broadcasted_iota(I32, (SUBLANES, CONV_CH), 0)
    for shift in range(1, SSM_CONV):
        xs = pltpu.roll(x, shift, axis=0)
        hs = pltpu.roll(halo, shift, axis=0)
        head = jnp.where(row8 < shift, hs, xs[0:SUBLANES])
        xs = jnp.concatenate([head, xs[SUBLANES:]], axis=0)
        acc = acc + xs * w[SSM_CONV - 1 - shift:SSM_CONV - shift, :]
    act = _silu(acc)
    x_s = act[:, :SSM_WIDTH]
    bc0 = SSM_WIDTH
    cc0 = SSM_WIDTH + SSM_GROUPS * SSM_STATE

    t = dtraw_ref[...] + dtb_ref[...]
    dt = jnp.maximum(t, 0.0) + jnp.log(1.0 + jnp.exp(-jnp.abs(t)))
    a = dt * (-jnp.exp(alog_ref[...]))
    a_cs = _dot_exact_lhs(tril_ref[...], a)
    a_cs_t = a_cs.T
    a_last = a_cs[SSM_CHUNK - 1:SSM_CHUNK, :]
    expand = expand_ref[...]
    dt_e = _dot_exact_rhs(dt, expand)
    ea_e = _dot_exact_rhs(jnp.exp(a_cs), expand)
    dte_e = _dot_exact_rhs(jnp.exp(a_last - a_cs), expand)
    xdt = x_s * dt_e
    xw = (xdt * dte_e).astype(BF16)
    xdt_b = xdt.astype(BF16)

    li = lax.broadcasted_iota(I32, (SSM_CHUNK, SSM_CHUNK), 0)
    si = lax.broadcasted_iota(I32, (SSM_CHUNK, SSM_CHUNK), 1)
    causal = li >= si

    ys = []
    for g in range(SSM_GROUPS):
        gs = slice(g * GROUP_WIDTH, (g + 1) * GROUP_WIDTH)
        b_g = act[:, bc0 + g * SSM_STATE:bc0 + (g + 1) * SSM_STATE]
        c_g = act[:, cc0 + g * SSM_STATE:cc0 + (g + 1) * SSM_STATE].astype(BF16)
        cb = _dot_nt(c_g, b_g.astype(BF16))
        state = state_ref[g]
        y_off = _dot(c_g, state.astype(BF16)) * ea_e[:, gs]
        parts = []
        for j in range(HEADS_PER_GROUP):
            hh = g * HEADS_PER_GROUP + j
            seg = a_cs[:, hh:hh + 1] - a_cs_t[hh:hh + 1, :]
            decay = jnp.exp(jnp.where(causal, seg, NEG_BIG))
            m = (cb * decay).astype(BF16)
            parts.append(_dot(m, xdt_b[:, hh * SSM_HEAD_DIM:(hh + 1) * SSM_HEAD_DIM]))
        ys.append(jnp.concatenate(parts, axis=1) + y_off)
        state_ref[g] = state * ea_e[SSM_CHUNK - 1:SSM_CHUNK, gs] + _dot(b_g.T.astype(BF16), xw[:, gs])
    y = jnp.concatenate(ys, axis=1) + dskip_ref[...] * x_s
    y = y * _silu(z_ref[...])
    gain = gain_ref[...]
    for g in range(SSM_GROUPS):
        gs = slice(g * GROUP_WIDTH, (g + 1) * GROUP_WIDTH)
        yg = y[:, gs]
        ms = jnp.mean(yg * yg, axis=-1, keepdims=True)
        y_ref[:, gs] = yg * lax.rsqrt(ms + NORM_EPS) * gain[:, gs]


def _ssd(xbc, z, dt_raw, conv_w, conv_b, dt_bias, a_log, d_skip, norm_gain, b, s):
    t = b * s
    nc = s // SSM_CHUNK
    pad_heads = lambda v: jnp.zeros((1, LANES), F32).at[0, :SSM_HEADS].set(v)
    head_of_lane = np.arange(SSM_WIDTH) // SSM_HEAD_DIM
    expand = jnp.asarray((np.arange(LANES)[:, None] == head_of_lane[None, :]).astype(np.float32), BF16)
    tril = jnp.asarray(np.tril(np.ones((SSM_CHUNK, SSM_CHUNK), np.float32)), BF16)
    halo_blocks = SSM_CHUNK // SUBLANES
    chunk = lambda n: pl.BlockSpec((SSM_CHUNK, n), lambda bi, c: (bi * nc + c, 0))
    full = lambda shp: pl.BlockSpec(shp, lambda bi, c: (0,) * len(shp))
    halo = pl.BlockSpec((SUBLANES, CONV_CH), lambda bi, c: (jnp.maximum((bi * nc + c) * halo_blocks - 1, 0), 0))
    return pl.pallas_call(
        _ssd_kernel,
        grid=(b, nc),
        in_specs=[chunk(CONV_CH), halo, chunk(SSM_WIDTH), chunk(LANES),
                  full((SSM_CONV, CONV_CH)), full((1, CONV_CH)), full((1, LANES)), full((1, LANES)),
                  full((1, SSM_WIDTH)), full((1, SSM_WIDTH)), full((LANES, SSM_WIDTH)), full((SSM_CHUNK, SSM_CHUNK))],
        out_specs=chunk(SSM_WIDTH),
        out_shape=jax.ShapeDtypeStruct((t, SSM_WIDTH), F32),
        scratch_shapes=[pltpu.VMEM((SSM_GROUPS, SSM_STATE, GROUP_WIDTH), F32)],
        compiler_params=_params(("arbitrary", "arbitrary")),
        name="ssd",
    )(xbc, xbc, z, dt_raw, conv_w, conv_b.reshape(1, CONV_CH), pad_heads(dt_bias), pad_heads(a_log),
      jnp.repeat(d_skip, SSM_HEAD_DIM).reshape(1, SSM_WIDTH), norm_gain.reshape(1, SSM_WIDTH), expand, tril)


def _outproj_kernel(attn_ref, ssm_ref, x_ref, gate_ref, shift_ref, scale_ref, g_ref, wa_ref, ws_ref, x1_ref, h2_ref):
    mixed = _dot(attn_ref[...].astype(BF16), wa_ref[...]) + _dot(ssm_ref[...].astype(BF16), ws_ref[...])
    x1 = x_ref[...] + gate_ref[...] * mixed
    x1_ref[...] = x1
    ms = jnp.mean(x1 * x1, axis=-1, keepdims=True)
    h = x1 * lax.rsqrt(ms + NORM_EPS) * g_ref[...]
    h2_ref[...] = h * (1.0 + scale_ref[...]) + shift_ref[...]


def _out_proj(attn, ssm, x, gate, shift, scale, gain, w_out, b, s, tm=256):
    t = b * s
    d = D_MODEL
    tiles_per_seq = s // tm
    w = w_out.astype(BF16)
    row = lambda n: pl.BlockSpec((tm, n), lambda i: (i, 0))
    full = lambda shp: pl.BlockSpec(shp, lambda i: (0,) * len(shp))
    per_batch = pl.BlockSpec((None, 1, d), lambda i: (i // tiles_per_seq, 0, 0))
    return pl.pallas_call(
        _outproj_kernel,
        grid=(t // tm,),
        in_specs=[row(ATTN_WIDTH), row(SSM_WIDTH), row(d), per_batch, per_batch, per_batch, full((1, d)),
                  full((ATTN_WIDTH, d)), full((SSM_WIDTH, d))],
        out_specs=[row(d), row(d)],
        out_shape=[jax.ShapeDtypeStruct((t, d), F32)] * 2,
        compiler_params=_params(("arbitrary",)),
        name="out_proj",
    )(attn.reshape(t, ATTN_WIDTH), ssm, x.reshape(t, d),
      gate.reshape(b, 1, d), shift.reshape(b, 1, d), scale.reshape(b, 1, d), gain.reshape(1, d),
      w[:ATTN_WIDTH], w[ATTN_WIDTH:])


def _mixer_sublayer(x, mod, norm_mix_gain, w_in, q_norm_gain, k_norm_gain, rel_bias_table, conv_w, conv_b, dt_bias,
                    a_log, d_skip, ssm_norm_gain, w_out, norm_ffn_gain):
    b, s, d = x.shape
    shift_m, scale_m, gate_m, shift_f, scale_f, _ = jnp.split(mod, 6, axis=-1)
    q, k, v, z, xbc, dt_raw = _in_proj(x, shift_m, scale_m, norm_mix_gain, w_in, q_norm_gain, k_norm_gain)
    bias = jnp.stack([_window_bias(rel_bias_table, dilation) for _, dilation in PATTERNS])
    attn = _attention(q.reshape(b, s, ATTN_WIDTH), k.reshape(b, s, ATTN_WIDTH), v.reshape(b, s, ATTN_WIDTH), bias)
    ssm = _ssd(xbc, z, dt_raw, conv_w, conv_b, dt_bias, a_log, d_skip, ssm_norm_gain, b, s)
    return _out_proj(attn, ssm, x, gate_m, shift_f, scale_f, norm_ffn_gain, w_out, b, s)


def _first_argmax(v, iota, limit):
    m = jnp.max(v, axis=0, keepdims=True)
    idx = jnp.min(jnp.where(v == m, iota, limit), axis=0, keepdims=True)
    return m, idx


def _router_kernel(h_ref, wt_ref, bias_ref, upper_ref, eidx_ref, rank_ref, gate_ref, counts_ref, carry_ref):
    @pl.when(pl.program_id(0) == 0)
    def _():
        carry_ref[...] = jnp.zeros_like(carry_ref)

    tm = h_ref.shape[0]
    h = h_ref[...]
    wt = wt_ref[...]
    h_hi = h.astype(BF16)
    h_lo = (h - h_hi.astype(F32)).astype(BF16)
    w_hi = wt.astype(BF16)
    w_lo = (wt - w_hi.astype(F32)).astype(BF16)
    logits = _dot_nt(w_hi, h_hi) + _dot_nt(w_hi, h_lo) + _dot_nt(w_lo, h_hi)
    scores = _sigmoid(logits)
    choice = scores + bias_ref[...]
    neg_inf = -jnp.inf

    iota_g = lax.broadcasted_iota(I32, (EXPERTS_PER_GROUP, tm), 0).astype(F32)
    group_rows = []
    for g in range(N_EXPERT_GROUPS):
        v = choice[g * EXPERTS_PER_GROUP:(g + 1) * EXPERTS_PER_GROUP]
        m1, i1 = _first_argmax(v, iota_g, float(EXPERTS_PER_GROUP))
        m2 = jnp.max(jnp.where(iota_g == i1, neg_inf, v), axis=0, keepdims=True)
        group_rows.append(m1 + m2)
    group_scores = jnp.concatenate(group_rows, axis=0)

    iota_n = lax.broadcasted_iota(I32, (N_EXPERT_GROUPS, tm), 0).astype(F32)
    chosen = jnp.zeros((N_EXPERT_GROUPS, tm), F32)
    for _ in range(TOPK_GROUPS):
        _, gi = _first_argmax(group_scores, iota_n, float(N_EXPERT_GROUPS))
        hit = iota_n == gi
        chosen = jnp.where(hit, 1.0, chosen)
        group_scores = jnp.where(hit, neg_inf, group_scores)

    masked = jnp.concatenate(
        [jnp.where(chosen[g:g + 1] > 0.0, choice[g * EXPERTS_PER_GROUP:(g + 1) * EXPERTS_PER_GROUP], neg_inf)
         for g in range(N_EXPERT_GROUPS)], axis=0)

    iota_e = lax.broadcasted_iota(I32, (N_EXPERTS, tm), 0).astype(F32)
    picked, gates = [], []
    onehot = jnp.zeros((N_EXPERTS, tm), F32)
    for _ in range(TOP_K):
        _, ei = _first_argmax(masked, iota_e, float(N_EXPERTS))
        hit = iota_e == ei
        gates.append(jnp.sum(jnp.where(hit, scores, 0.0), axis=0, keepdims=True))
        masked = jnp.where(hit, neg_inf, masked)
        onehot = jnp.where(hit, 1.0, onehot)
        picked.append(ei)
    gate_sum = gates[0]
    for gk in gates[1:]:
        gate_sum = gate_sum + gk

    base = _dot(onehot.astype(BF16), upper_ref[...]) + carry_ref[...]
    ranks = [jnp.sum(jnp.where(iota_e == ei, base, 0.0), axis=0, keepdims=True) for ei in picked]
    carry_ref[...] = carry_ref[...] + jnp.sum(onehot, axis=1, keepdims=True)

    eidx_ref[...] = jnp.concatenate(picked, axis=0).astype(I32)
    rank_ref[...] = jnp.concatenate(ranks, axis=0).astype(I32)
    gate_ref[...] = jnp.concatenate([gk / gate_sum * ROUTED_SCALE for gk in gates], axis=0)
    counts_ref[...] = carry_ref[...].astype(I32)


def _router(h2, w_router, router_bias, tm=256):
    t, d = h2.shape
    upper = jnp.asarray(np.triu(np.ones((tm, tm), np.float32), 1), BF16)
    tok = pl.BlockSpec((TOP_K, tm), lambda i: (0, i))
    full = lambda shp: pl.BlockSpec(shp, lambda i: (0,) * len(shp))
    return pl.pallas_call(
        _router_kernel,
        grid=(t // tm,),
        in_specs=[pl.BlockSpec((tm, d), lambda i: (i, 0)), full((N_EXPERTS, d)), full((N_EXPERTS, 1)), full((tm, tm))],
        out_specs=[tok, tok, tok, full((N_EXPERTS, 1))],
        out_shape=[jax.ShapeDtypeStruct((TOP_K, t), I32), jax.ShapeDtypeStruct((TOP_K, t), I32),
                   jax.ShapeDtypeStruct((TOP_K, t), F32), jax.ShapeDtypeStruct((N_EXPERTS, 1), I32)],
        scratch_shapes=[pltpu.VMEM((N_EXPERTS, 1), F32)],
        compiler_params=_params(("arbitrary",)),
        name="router",
    )(h2, w_router.T, router_bias.reshape(N_EXPERTS, 1), upper)


def _positions_kernel(counts_ref, lower_ref, eidx_ref, rank_ref, pos_ref):
    tm = eidx_ref.shape[1]
    counts = jnp.broadcast_to(counts_ref[...].astype(F32), (N_EXPERTS, LANES))
    offsets = _dot_exact_lhs(lower_ref[...], counts)[:, 0:1]
    iota_e = lax.broadcasted_iota(I32, (N_EXPERTS, tm), 0).astype(F32)
    e = eidx_ref[...].astype(F32)
    rows = [jnp.sum(jnp.where(iota_e == e[k:k + 1], offsets, 0.0), axis=0, keepdims=True) for k in range(TOP_K)]
    pos_ref[0] = jnp.concatenate(rows, axis=0).astype(I32) + rank_ref[...]


def _positions(counts, eidx, rank, tm):
    t = eidx.shape[1]
    lower = jnp.asarray(np.tril(np.ones((N_EXPERTS, N_EXPERTS), np.float32), -1), BF16)
    tok = pl.BlockSpec((TOP_K, tm), lambda i: (0, i))
    return pl.pallas_call(
        _positions_kernel,
        grid=(t // tm,),
        in_specs=[pl.BlockSpec((N_EXPERTS, 1), lambda i: (0, 0)), pl.BlockSpec((N_EXPERTS, N_EXPERTS), lambda i: (0, 0)),
                  tok, tok],
        out_specs=pl.BlockSpec((1, TOP_K, tm), lambda i: (i, 0, 0)),
        out_shape=jax.ShapeDtypeStruct((t // tm, TOP_K, tm), I32),
        compiler_params=_params(("arbitrary",)),
        name="positions",
    )(counts, lower, eidx, rank)


def _dispatch_kernel(pos_ref, h_ref, xs_ref, sem):
    tm = h_ref.shape[0]

    def row_copy(t, k):
        return pltpu.make_async_copy(h_ref.at[pl.ds(t, 1)], xs_ref.at[pl.ds(pos_ref[0, k, t], 1)], sem)

    def start(t, carry):
        for k in range(TOP_K):
            row_copy(t, k).start(priority=k % 2)
        return carry

    def wait(t, carry):
        for k in range(TOP_K):
            row_copy(t, k).wait()
        return carry

    lax.fori_loop(0, tm, start, 0)
    lax.fori_loop(0, tm, wait, 0)


def _dispatch(h2, pos_tiles, tm):
    t, d = h2.shape
    return pl.pallas_call(
        _dispatch_kernel,
        grid=(t // tm,),
        in_specs=[pl.BlockSpec((1, TOP_K, tm), lambda i: (i, 0, 0), memory_space=pltpu.SMEM),
                  pl.BlockSpec((tm, d), lambda i: (i, 0))],
        out_specs=pl.BlockSpec(memory_space=pl.ANY),
        out_shape=jax.ShapeDtypeStruct((t * TOP_K, d), h2.dtype),
        scratch_shapes=[pltpu.SemaphoreType.DMA(())],
        compiler_params=_params(("arbitrary",)),
        name="dispatch",
    )(pos_tiles, h2)


def _experts_kernel(tile_ref, exp_ref, lo_ref, hi_ref, xs_ref, wg_ref, wu_ref, wd_ref, ys_ref):
    w = pl.program_id(0)
    tmg = xs_ref.shape[0]
    tile = tile_ref[w]
    lo, hi = lo_ref[w], hi_ref[w]
    new_tile = jnp.logical_or(w == 0, tile_ref[jnp.maximum(w - 1, 0)] != tile)

    @pl.when(new_tile)
    def _():
        ys_ref[...] = jnp.zeros_like(ys_ref)

    @pl.when(hi > lo)
    def _():
        x = xs_ref[...].astype(BF16)
        g = _dot(x, wg_ref[...].astype(BF16))
        u = _dot(x, wu_ref[...].astype(BF16))
        y = _dot((_silu(g) * u).astype(BF16), wd_ref[...].astype(BF16))
        row = tile * tmg + lax.broadcasted_iota(I32, (tmg, 1), 0)
        inside = jnp.logical_and(row >= lo, row < hi)
        ys_ref[...] = jnp.where(inside, y, ys_ref[...])


def _group_metadata(counts, n_rows, tmg):
    n_tiles = n_rows // tmg
    n_work = n_tiles + N_EXPERTS
    ends = jnp.cumsum(counts)
    starts = ends - counts
    first_tile = starts // tmg
    n_items = jnp.where(counts > 0, (ends - 1) // tmg - first_tile + 1, 0)
    item_ends = jnp.cumsum(n_items)
    w = jnp.arange(n_work, dtype=I32)
    valid = w < item_ends[-1]
    e = jnp.minimum(jnp.searchsorted(item_ends, w, side="right"), N_EXPERTS - 1).astype(I32)
    tile = first_tile[e] + (w - (item_ends[e] - n_items[e]))
    lo = jnp.maximum(starts[e], tile * tmg)
    hi = jnp.minimum(ends[e], (tile + 1) * tmg)
    last_e = jnp.max(jnp.where(counts > 0, jnp.arange(N_EXPERTS, dtype=I32), 0))
    tile = jnp.where(valid, tile, n_tiles - 1).astype(I32)
    e = jnp.where(valid, e, last_e).astype(I32)
    lo = jnp.where(valid, lo, 0).astype(I32)
    hi = jnp.where(valid, hi, 0).astype(I32)
    return tile, e, lo, hi


def _experts(xs, counts, w_gate, w_up, w_down, tmg=256):
    n, d = xs.shape
    tile, e, lo, hi = _group_metadata(counts, n, tmg)
    n_work = tile.shape[0]
    grid_spec = pltpu.PrefetchScalarGridSpec(
        num_scalar_prefetch=4,
        grid=(n_work,),
        in_specs=[pl.BlockSpec((tmg, d), lambda w, tile, e, lo, hi: (tile[w], 0)),
                  pl.BlockSpec((None, d, EXPERT_FF), lambda w, tile, e, lo, hi: (e[w], 0, 0)),
                  pl.BlockSpec((None, d, EXPERT_FF), lambda w, tile, e, lo, hi: (e[w], 0, 0)),
                  pl.BlockSpec((None, EXPERT_FF, d), lambda w, tile, e, lo, hi: (e[w], 0, 0))],
        out_specs=pl.BlockSpec((tmg, d), lambda w, tile, e, lo, hi: (tile[w], 0)),
    )
    return pl.pallas_call(
        _experts_kernel,
        grid_spec=grid_spec,
        out_shape=jax.ShapeDtypeStruct((n, d), F32),
        compiler_params=_params(("arbitrary",)),
        name="experts",
    )(tile, e, lo, hi, xs, w_gate, w_up, w_down)


def _combine_kernel(pos_ref, gates_ref, h_ref, x1_ref, gatef_ref, wg_ref, wu_ref, wd_ref, ys_ref, out_ref, buf, sem):
    tm = h_ref.shape[0]

    def row_copy(t, k):
        return pltpu.make_async_copy(ys_ref.at[pl.ds(pos_ref[0, k, t], 1)], buf.at[k, pl.ds(t, 1)], sem)

    def start(t, carry):
        for k in range(TOP_K):
            row_copy(t, k).start(priority=k % 2)
        return carry

    def wait(t, carry):
        for k in range(TOP_K):
            row_copy(t, k).wait()
        return carry

    lax.fori_loop(0, tm, start, 0)
    hb = h_ref[...].astype(BF16)
    shared = _dot((_silu(_dot(hb, wg_ref[...])) * _dot(hb, wu_ref[...])).astype(BF16), wd_ref[...])
    lax.fori_loop(0, tm, wait, 0)
    gates = gates_ref[...]
    routed = buf[0] * gates[:, 0:1]
    for k in range(1, TOP_K):
        routed = routed + buf[k] * gates[:, k:k + 1]
    out_ref[...] = x1_ref[...] + gatef_ref[...] * (shared + routed)


def _combine(ys, pos_tiles, gates_t, h2, x1, gate_f, w_gate_s, w_up_s, w_down_s, b, s, tm):
    t, d = h2.shape
    tiles_per_seq = s // tm
    row = lambda n: pl.BlockSpec((tm, n), lambda i: (i, 0))
    full = lambda shp: pl.BlockSpec(shp, lambda i: (0,) * len(shp))
    return pl.pallas_call(
        _combine_kernel,
        grid=(t // tm,),
        in_specs=[pl.BlockSpec((1, TOP_K, tm), lambda i: (i, 0, 0), memory_space=pltpu.SMEM),
                  row(TOP_K), row(d), row(d),
                  pl.BlockSpec((None, 1, d), lambda i: (i // tiles_per_seq, 0, 0)),
                  full((d, EXPERT_FF)), full((d, EXPERT_FF)), full((EXPERT_FF, d)),
                  pl.BlockSpec(memory_space=pl.ANY)],
        out_specs=row(d),
        out_shape=jax.ShapeDtypeStruct((t, d), F32),
        scratch_shapes=[pltpu.VMEM((TOP_K, tm, d), F32), pltpu.SemaphoreType.DMA(())],
        compiler_params=_params(("arbitrary",)),
        name="combine",
    )(pos_tiles, gates_t, h2, x1, gate_f.reshape(b, 1, d),
      w_gate_s.astype(BF16), w_up_s.astype(BF16), w_down_s.astype(BF16), ys)


def _moe_sublayer(x1, h2, gate_f, w_router, router_bias, w_gate, w_up, w_down, w_gate_s, w_up_s, w_down_s, b, s,
                  tm=256):
    t = b * s
    eidx, rank, gates, counts = _router(h2, w_router, router_bias)
    pos_tiles = _positions(counts, eidx, rank, tm)
    xs = _dispatch(h2, pos_tiles, tm)
    ys = _experts(xs, counts[:, 0], w_gate, w_up, w_down)
    return _combine(ys, pos_tiles, gates.T, h2, x1, gate_f, w_gate_s, w_up_s, w_down_s, b, s, tm)


def kernel(x, c, w_ada, b_ada, norm_mix_gain, w_in, q_norm_gain, k_norm_gain, rel_bias_table, conv_w, conv_b, dt_bias,
           a_log, d_skip, ssm_norm_gain, w_out, norm_ffn_gain, w_router, router_bias, w_gate_experts, w_up_experts,
           w_down_experts, w_gate_shared, w_up_shared, w_down_shared):
    b, s, d = x.shape
    for layer in range(w_ada.shape[0]):
        mod = _adaln(c, w_ada[layer], b_ada[layer])
        x1, h2 = _mixer_sublayer(x, mod, norm_mix_gain[layer], w_in[layer], q_norm_gain[layer], k_norm_gain[layer],
                                 rel_bias_table, conv_w[layer], conv_b[layer], dt_bias[layer], a_log[layer],
                                 d_skip[layer], ssm_norm_gain[layer], w_out[layer], norm_ffn_gain[layer])
        gate_f = mod[:, 5 * d:]
        out = _moe_sublayer(x1, h2, gate_f, w_router[layer], router_bias[layer], w_gate_experts[layer],
                            w_up_experts[layer], w_down_experts[layer], w_gate_shared[layer], w_up_shared[layer],
                            w_down_shared[layer], b, s)
        x = out.reshape(b, s, d)
    return x
```

```python
import functools
import math

import numpy as np
import jax
import jax.numpy as jnp
from jax import lax
from jax.experimental import pallas as pl
from jax.experimental.pallas import tpu as pltpu

F32 = jnp.float32
BF16 = jnp.bfloat16
I32 = jnp.int32

D_MODEL = 1024
ATTN_HEADS = 8
HEAD_DIM = 64
ATTN_WIDTH = ATTN_HEADS * HEAD_DIM
PATTERNS = ((128, 1), (512, 4), (2048, 16))
WIN_STEPS = 128
REL_BUCKETS = 32
REL_MAX_DISTANCE = 2048
SSM_HEADS = 24
SSM_HEAD_DIM = 64
SSM_WIDTH = SSM_HEADS * SSM_HEAD_DIM
SSM_GROUPS = 4
HEADS_PER_GROUP = SSM_HEADS // SSM_GROUPS
GROUP_WIDTH = SSM_WIDTH // SSM_GROUPS
SSM_STATE = 128
SSM_CONV = 4
SSM_CHUNK = 128
CONV_CH = SSM_WIDTH + 2 * SSM_GROUPS * SSM_STATE
N_EXPERTS = 256
TOP_K = 8
N_EXPERT_GROUPS = 8
EXPERTS_PER_GROUP = N_EXPERTS // N_EXPERT_GROUPS
TOPK_GROUPS = 4
EXPERT_FF = 256
ROUTED_SCALE = 2.5
NORM_EPS = 1e-6

LANES = 128
SUBLANES = 8
NEG_BIG = -1e30
VMEM_LIMIT = 56 * 1024 * 1024


def _params(sem, vmem=VMEM_LIMIT):
    return pltpu.CompilerParams(dimension_semantics=sem, vmem_limit_bytes=vmem)


def _sigmoid(x):
    return 1.0 / (1.0 + jnp.exp(-x))


def _silu(x):
    return x * _sigmoid(x)


def _split3(x):
    hi = x.astype(BF16)
    r = x - hi.astype(F32)
    mid = r.astype(BF16)
    lo = (r - mid.astype(F32)).astype(BF16)
    return hi, mid, lo


def _dot(a, b):
    return jnp.dot(a, b, preferred_element_type=F32)


def _dot_nt(a, b):
    return lax.dot_general(a, b, (((1,), (1,)), ((), ())), preferred_element_type=F32)


def _dot_exact_rhs(a, b_exact):
    hi, mid, lo = _split3(a)
    return _dot(hi, b_exact) + _dot(mid, b_exact) + _dot(lo, b_exact)


def _dot_exact_lhs(a_exact, b):
    hi, mid, lo = _split3(b)
    return _dot(a_exact, hi) + _dot(a_exact, mid) + _dot(a_exact, lo)


def _adaln_kernel(c_ref, w_ref, b_ref, o_ref):
    s = _silu(c_ref[...]).astype(BF16)
    o_ref[...] = _dot(s, w_ref[...].astype(BF16)) + b_ref[...]


def _adaln(c, w_ada, b_ada):
    b, d = c.shape
    n = w_ada.shape[1]
    rows = SUBLANES
    c_pad = jnp.zeros((rows, d), F32).at[:b].set(c)
    tn = 1024
    out = pl.pallas_call(
        _adaln_kernel,
        grid=(n // tn,),
        in_specs=[pl.BlockSpec((rows, d), lambda j: (0, 0)),
                  pl.BlockSpec((d, tn), lambda j: (0, j)),
                  pl.BlockSpec((1, tn), lambda j: (0, j))],
        out_specs=pl.BlockSpec((rows, tn), lambda j: (0, j)),
        out_shape=jax.ShapeDtypeStruct((rows, n), F32),
        compiler_params=_params(("arbitrary",)),
        name="adaln",
    )(c_pad, w_ada, b_ada.reshape(1, n))
    return out[:b]


def _inproj_kernel(x_ref, shift_ref, scale_ref, g_ref, wqkv_ref, wz_ref, wxbc_ref, wdt_ref,
                   qg_ref, kg_ref, hmean_ref, q_ref, k_ref, v_ref, z_ref, xbc_ref, dt_ref):
    x = x_ref[...]
    ms = jnp.mean(x * x, axis=-1, keepdims=True)
    h = x * lax.rsqrt(ms + NORM_EPS) * g_ref[...]
    h = h * (1.0 + scale_ref[...]) + shift_ref[...]
    hb = h.astype(BF16)

    hmean = hmean_ref[...]

    def head_norm(t, gain):
        ss = _dot_exact_rhs(t * t, hmean)
        return t * lax.rsqrt(ss + NORM_EPS) * gain

    q = _dot(hb, wqkv_ref[:, 0:ATTN_WIDTH])
    q_ref[...] = head_norm(q, qg_ref[...]) * (HEAD_DIM ** -0.5)
    k = _dot(hb, wqkv_ref[:, ATTN_WIDTH:2 * ATTN_WIDTH])
    k_ref[...] = head_norm(k, kg_ref[...])
    v_ref[...] = _dot(hb, wqkv_ref[:, 2 * ATTN_WIDTH:3 * ATTN_WIDTH])
    for c0 in range(0, SSM_WIDTH, 512):
        z_ref[:, c0:c0 + 512] = _dot(hb, wz_ref[:, c0:c0 + 512])
    for c0 in range(0, CONV_CH, 512):
        xbc_ref[:, c0:c0 + 512] = _dot(hb, wxbc_ref[:, c0:c0 + 512])
    dt_ref[...] = _dot(hb, wdt_ref[...])


def _in_proj(x, shift, scale, gain, w_in, q_gain, k_gain, tm=256):
    b, s, d = x.shape
    t = b * s
    tiles_per_seq = s // tm
    w = w_in.astype(BF16)
    o_z = 3 * ATTN_WIDTH
    o_x = o_z + SSM_WIDTH
    o_dt = o_x + CONV_CH
    w_qkv, w_z, w_xbc = w[:, :o_z], w[:, o_z:o_x], w[:, o_x:o_dt]
    w_dt = jnp.zeros((d, LANES), BF16).at[:, :SSM_HEADS].set(w[:, o_dt:])
    head_of = np.arange(ATTN_WIDTH) // HEAD_DIM
    hmean = jnp.asarray((head_of[:, None] == head_of[None, :]).astype(np.float32) / HEAD_DIM, BF16)
    full = lambda shp: pl.BlockSpec(shp, lambda i: (0,) * len(shp))
    row = lambda n: pl.BlockSpec((tm, n), lambda i: (i, 0))
    per_batch = pl.BlockSpec((None, 1, d), lambda i: (i // tiles_per_seq, 0, 0))
    outs = pl.pallas_call(
        _inproj_kernel,
        grid=(t // tm,),
        in_specs=[row(d), per_batch, per_batch, full((1, d)),
                  full((d, o_z)), full((d, SSM_WIDTH)), full((d, CONV_CH)), full((d, LANES)),
                  full((1, ATTN_WIDTH)), full((1, ATTN_WIDTH)), full((ATTN_WIDTH, ATTN_WIDTH))],
        out_specs=[row(ATTN_WIDTH), row(ATTN_WIDTH), row(ATTN_WIDTH), row(SSM_WIDTH), row(CONV_CH), row(LANES)],
        out_shape=[jax.ShapeDtypeStruct((t, n), F32)
                   for n in (ATTN_WIDTH, ATTN_WIDTH, ATTN_WIDTH, SSM_WIDTH, CONV_CH, LANES)],
        compiler_params=_params(("arbitrary",)),
        name="in_proj",
    )(x.reshape(t, d), shift.reshape(b, 1, d), scale.reshape(b, 1, d), gain.reshape(1, d),
      w_qkv, w_z, w_xbc, w_dt,
      jnp.tile(q_gain, ATTN_HEADS).reshape(1, ATTN_WIDTH), jnp.tile(k_gain, ATTN_HEADS).reshape(1, ATTN_WIDTH), hmean)
    return outs


def _t5_causal_buckets(distance):
    n = np.maximum(distance, 0)
    max_exact = REL_BUCKETS // 2
    large = max_exact + (np.log(np.maximum(n, 1) / max_exact) / math.log(REL_MAX_DISTANCE / max_exact)
                         * (REL_BUCKETS - max_exact)).astype(np.int64)
    large = np.minimum(large, REL_BUCKETS - 1)
    return np.where(n < max_exact, n, large).astype(np.int32)


def _window_bias(rel_bias_table, dilation):
    qi = np.arange(WIN_STEPS)[:, None]
    kj = np.arange(2 * WIN_STEPS)[None, :]
    dist = qi + WIN_STEPS - kj
    band = (dist >= 0) & (dist <= WIN_STEPS)
    onehot = (_t5_causal_buckets(dist * dilation).reshape(-1, 1) == np.arange(REL_BUCKETS)[None, :]).astype(np.float32)
    bias = jnp.dot(rel_bias_table.astype(F32).T, jnp.asarray(onehot).T, precision=lax.Precision.HIGHEST)
    bias = bias.reshape(ATTN_HEADS, WIN_STEPS, 2 * WIN_STEPS)
    return jnp.where(jnp.asarray(band)[None], bias, NEG_BIG)


ATTN_TOKENS = max(w for w, _ in PATTERNS)
ATTN_UNROLL = 4


def _attn_kernel(q_ref, kp_ref, kc_ref, vp_ref, vc_ref, bias_ref, out_ref, kw, vw, o_acc, l_acc):
    tb = ATTN_TOKENS
    first = pl.program_id(2) == 0
    kw[0:tb] = kp_ref[...]
    kw[tb:2 * tb] = kc_ref[...]
    vw[0:tb] = vp_ref[...]
    vw[tb:2 * tb] = vc_ref[...]
    lane = lax.broadcasted_iota(I32, (WIN_STEPS, LANES), 1)
    head0 = lane < HEAD_DIM
    col = lax.broadcasted_iota(I32, (WIN_STEPS, 2 * WIN_STEPS), 1)
    in_prev = col < WIN_STEPS

    for p, (_, d) in enumerate(PATTERNS):
        shift = d.bit_length() - 1
        n_blocks = tb // WIN_STEPS

        def rows(start, n, d=d):
            return pl.ds(start, n, stride=d) if d > 1 else pl.ds(start, n)

        def body(it, carry, p=p, d=d, shift=shift, rows=rows):
            for u in range(ATTN_UNROLL):
                idx = it * ATTN_UNROLL + u
                r = jnp.bitwise_and(idx, d - 1)
                j = jnp.right_shift(idx, shift)
                qs = j * (WIN_STEPS * d) + r
                q = q_ref[rows(qs, WIN_STEPS), :]
                k = kw[rows(tb + qs - WIN_STEPS * d, 2 * WIN_STEPS), :].astype(BF16)
                v = vw[rows(tb + qs - WIN_STEPS * d, 2 * WIN_STEPS), :].astype(BF16)
                no_prev = jnp.logical_and(in_prev, jnp.logical_and(first, j == 0))
                o_h, lse_h = [], []
                for h in range(2):
                    qh = jnp.where(head0 if h == 0 else jnp.logical_not(head0), q, 0.0).astype(BF16)
                    s = _dot_nt(qh, k) + bias_ref[p, h]
                    s = jnp.where(no_prev, NEG_BIG, s)
                    m = jnp.max(s, axis=-1, keepdims=True)
                    e = jnp.exp(s - m)
                    denom = jnp.sum(e, axis=-1, keepdims=True)
                    o_h.append(_dot(e.astype(BF16), v) / denom)
                    lse_h.append(m + jnp.log(denom))
                o_acc[p, rows(qs, WIN_STEPS), :] = jnp.where(head0, o_h[0], o_h[1])
                l_acc[p, rows(qs, WIN_STEPS), :] = jnp.where(head0, lse_h[0], lse_h[1])
            return carry

        lax.fori_loop(0, n_blocks // ATTN_UNROLL, body, 0)

    chunk = 256
    for c0 in range(0, tb, chunk):
        l1, l2, l3 = (l_acc[p, c0:c0 + chunk, :] for p in range(3))
        m = jnp.maximum(jnp.maximum(l1, l2), l3)
        e1, e2, e3 = jnp.exp(l1 - m), jnp.exp(l2 - m), jnp.exp(l3 - m)
        num = e1 * o_acc[0, c0:c0 + chunk, :] + e2 * o_acc[1, c0:c0 + chunk, :] + e3 * o_acc[2, c0:c0 + chunk, :]
        out_ref[c0:c0 + chunk, :] = num / (e1 + e2 + e3)


def _attention(q, k, v, bias):
    b, s, w = q.shape
    tb = ATTN_TOKENS
    pairs = ATTN_HEADS // 2
    cur = pl.BlockSpec((None, tb, LANES), lambda bi, hp, i: (bi, i, hp))
    prev = pl.BlockSpec((None, tb, LANES), lambda bi, hp, i: (bi, jnp.maximum(i - 1, 0), hp))
    return pl.pallas_call(
        _attn_kernel,
        grid=(b, pairs, s // tb),
        in_specs=[cur, prev, cur, prev, cur,
                  pl.BlockSpec((len(PATTERNS), 2, WIN_STEPS, 2 * WIN_STEPS), lambda bi, hp, i: (0, hp, 0, 0))],
        out_specs=cur,
        out_shape=jax.ShapeDtypeStruct((b, s, w), F32),
        scratch_shapes=[pltpu.VMEM((2 * tb, LANES), F32), pltpu.VMEM((2 * tb, LANES), F32),
                        pltpu.VMEM((len(PATTERNS), tb, LANES), F32), pltpu.VMEM((len(PATTERNS), tb, LANES), F32)],
        compiler_params=_params(("arbitrary",) * 3),
        name="attention",
    )(q, k, k, v, v, bias)


def _ssd_kernel(xbc_ref, halo_ref, z_ref, dtraw_ref, convw_ref, convb_ref, dtb_ref, alog_ref, dskip_ref, gain_ref,
                expand_ref, tril_ref, y_ref, state_ref):
    c = pl.program_id(1)

    @pl.when(c == 0)
    def _():
        state_ref[...] = jnp.zeros_like(state_ref)

    x = xbc_ref[...]
    halo = jnp.where(c == 0, 0.0, halo_ref[...])
    w = convw_ref[...]
    acc = x * w[SSM_CONV - 1:SSM_CONV, :] + convb_ref[...]
    row8 = lax.broadcasted_iota(I32, (SUBLANES, CONV_CH), 0)
    for shift in range(1, SSM_CONV):
        xs = pltpu.roll(x, shift, axis=0)
        hs = pltpu.roll(halo, shift, axis=0)
        head = jnp.where(row8 < shift, hs, xs[0:SUBLANES])
        xs = jnp.concatenate([head, xs[SUBLANES:]], axis=0)
        acc = acc + xs * w[SSM_CONV - 1 - shift:SSM_CONV - shift, :]
    act = _silu(acc)
    x_s = act[:, :SSM_WIDTH]
    bc0 = SSM_WIDTH
    cc0 = SSM_WIDTH + SSM_GROUPS * SSM_STATE

    t = dtraw_ref[...] + dtb_ref[...]
    dt = jnp.maximum(t, 0.0) + jnp.log(1.0 + jnp.exp(-jnp.abs(t)))
    a = dt * (-jnp.exp(alog_ref[...]))
    a_cs = _dot_exact_lhs(tril_ref[...], a)
    a_cs_t = a_cs.T
    a_last = a_cs[SSM_CHUNK - 1:SSM_CHUNK, :]
    expand = expand_ref[...]
    dt_e = _dot_exact_rhs(dt, expand)
    ea_e = _dot_exact_rhs(jnp.exp(a_cs), expand)
    dte_e = _dot_exact_rhs(jnp.exp(a_last - a_cs), expand)
    xdt = x_s * dt_e
    xw = (xdt * dte_e).astype(BF16)
    xdt_b = xdt.astype(BF16)

    li = lax.broadcasted_iota(I32, (SSM_CHUNK, SSM_CHUNK), 0)
    si = lax.broadcasted_iota(I32, (SSM_CHUNK, SSM_CHUNK), 1)
    causal = li >= si

    ys = []
    for g in range(SSM_GROUPS):
        gs = slice(g * GROUP_WIDTH, (g + 1) * GROUP_WIDTH)
        b_g = act[:, bc0 + g * SSM_STATE:bc0 + (g + 1) * SSM_STATE]
        c_g = act[:, cc0 + g * SSM_STATE:cc0 + (g + 1) * SSM_STATE].astype(BF16)
        cb = _dot_nt(c_g, b_g.astype(BF16))
        state = state_ref[g]
        y_off = _dot(c_g, state.astype(BF16)) * ea_e[:, gs]
        parts = []
        for j in range(HEADS_PER_GROUP):
            hh = g * HEADS_PER_GROUP + j
            seg = a_cs[:, hh:hh + 1] - a_cs_t[hh:hh + 1, :]
            decay = jnp.exp(jnp.where(causal, seg, NEG_BIG))
            m = (cb * decay).astype(BF16)
            parts.append(_dot(m, xdt_b[:, hh * SSM_HEAD_DIM:(hh + 1) * SSM_HEAD_DIM]))
        ys.append(jnp.concatenate(parts, axis=1) + y_off)
        state_ref[g] = state * ea_e[SSM_CHUNK - 1:SSM_CHUNK, gs] + _dot(b_g.T.astype(BF16), xw[:, gs])
    y = jnp.concatenate(ys, axis=1) + dskip_ref[...] * x_s
    y = y * _silu(z_ref[...])
    gain = gain_ref[...]
    for g in range(SSM_GROUPS):
        gs = slice(g * GROUP_WIDTH, (g + 1) * GROUP_WIDTH)
        yg = y[:, gs]
        ms = jnp.mean(yg * yg, axis=-1, keepdims=True)
        y_ref[:, gs] = yg * lax.rsqrt(ms + NORM_EPS) * gain[:, gs]


def _ssd(xbc, z, dt_raw, conv_w, conv_b, dt_bias, a_log, d_skip, norm_gain, b, s):
    t = b * s
    nc = s // SSM_CHUNK
    pad_heads = lambda v: jnp.zeros((1, LANES), F32).at[0, :SSM_HEADS].set(v)
    head_of_lane = np.arange(SSM_WIDTH) // SSM_HEAD_DIM
    expand = jnp.asarray((np.arange(LANES)[:, None] == head_of_lane[None, :]).astype(np.float32), BF16)
    tril = jnp.asarray(np.tril(np.ones((SSM_CHUNK, SSM_CHUNK), np.float32)), BF16)
    halo_blocks = SSM_CHUNK // SUBLANES
    chunk = lambda n: pl.BlockSpec((SSM_CHUNK, n), lambda bi, c: (bi * nc + c, 0))
    full = lambda shp: pl.BlockSpec(shp, lambda bi, c: (0,) * len(shp))
    halo = pl.BlockSpec((SUBLANES, CONV_CH), lambda bi, c: (jnp.maximum((bi * nc + c) * halo_blocks - 1, 0), 0))
    return pl.pallas_call(
        _ssd_kernel,
        grid=(b, nc),
        in_specs=[chunk(CONV_CH), halo, chunk(SSM_WIDTH), chunk(LANES),
                  full((SSM_CONV, CONV_CH)), full((1, CONV_CH)), full((1, LANES)), full((1, LANES)),
                  full((1, SSM_WIDTH)), full((1, SSM_WIDTH)), full((LANES, SSM_WIDTH)), full((SSM_CHUNK, SSM_CHUNK))],
        out_specs=chunk(SSM_WIDTH),
        out_shape=jax.ShapeDtypeStruct((t, SSM_WIDTH), F32),
        scratch_shapes=[pltpu.VMEM((SSM_GROUPS, SSM_STATE, GROUP_WIDTH), F32)],
        compiler_params=_params(("arbitrary", "arbitrary")),
        name="ssd",
    )(xbc, xbc, z, dt_raw, conv_w, conv_b.reshape(1, CONV_CH), pad_heads(dt_bias), pad_heads(a_log),
      jnp.repeat(d_skip, SSM_HEAD_DIM).reshape(1, SSM_WIDTH), norm_gain.reshape(1, SSM_WIDTH), expand, tril)


TOKEN_ROWS = D_MODEL // LANES


def _to_token_tiles(ref, x):
    n = x.shape[0]
    for c in range(TOKEN_ROWS):
        ref[pl.ds(c, n, stride=TOKEN_ROWS), :] = x[:, c * LANES:(c + 1) * LANES]


def _from_token_tiles(ref, n):
    return jnp.concatenate([ref[pl.ds(c, n, stride=TOKEN_ROWS), :] for c in range(TOKEN_ROWS)], axis=1)


def _outproj_kernel(attn_ref, ssm_ref, x_ref, gate_ref, shift_ref, scale_ref, g_ref, wa_ref, ws_ref, x1_ref, h2_ref,
                    h2t_ref):
    mixed = _dot(attn_ref[...].astype(BF16), wa_ref[...]) + _dot(ssm_ref[...].astype(BF16), ws_ref[...])
    x1 = x_ref[...] + gate_ref[...] * mixed
    x1_ref[...] = x1
    ms = jnp.mean(x1 * x1, axis=-1, keepdims=True)
    h = x1 * lax.rsqrt(ms + NORM_EPS) * g_ref[...]
    h2 = h * (1.0 + scale_ref[...]) + shift_ref[...]
    h2_ref[...] = h2
    _to_token_tiles(h2t_ref, h2)


def _out_proj(attn, ssm, x, gate, shift, scale, gain, w_out, b, s, tm=256):
    t = b * s
    d = D_MODEL
    tiles_per_seq = s // tm
    w = w_out.astype(BF16)
    row = lambda n: pl.BlockSpec((tm, n), lambda i: (i, 0))
    full = lambda shp: pl.BlockSpec(shp, lambda i: (0,) * len(shp))
    per_batch = pl.BlockSpec((None, 1, d), lambda i: (i // tiles_per_seq, 0, 0))
    return pl.pallas_call(
        _outproj_kernel,
        grid=(t // tm,),
        in_specs=[row(ATTN_WIDTH), row(SSM_WIDTH), row(d), per_batch, per_batch, per_batch, full((1, d)),
                  full((ATTN_WIDTH, d)), full((SSM_WIDTH, d))],
        out_specs=[row(d), row(d), pl.BlockSpec((tm * TOKEN_ROWS, LANES), lambda i: (i, 0))],
        out_shape=[jax.ShapeDtypeStruct((t, d), F32)] * 2 + [jax.ShapeDtypeStruct((t * TOKEN_ROWS, LANES), F32)],
        compiler_params=_params(("arbitrary",)),
        name="out_proj",
    )(attn.reshape(t, ATTN_WIDTH), ssm, x.reshape(t, d),
      gate.reshape(b, 1, d), shift.reshape(b, 1, d), scale.reshape(b, 1, d), gain.reshape(1, d),
      w[:ATTN_WIDTH], w[ATTN_WIDTH:])


def _mixer_sublayer(x, mod, norm_mix_gain, w_in, q_norm_gain, k_norm_gain, rel_bias_table, conv_w, conv_b, dt_bias,
                    a_log, d_skip, ssm_norm_gain, w_out, norm_ffn_gain):
    b, s, d = x.shape
    shift_m, scale_m, gate_m, shift_f, scale_f, _ = jnp.split(mod, 6, axis=-1)
    q, k, v, z, xbc, dt_raw = _in_proj(x, shift_m, scale_m, norm_mix_gain, w_in, q_norm_gain, k_norm_gain)
    bias = jnp.stack([_window_bias(rel_bias_table, dilation) for _, dilation in PATTERNS])
    attn = _attention(q.reshape(b, s, ATTN_WIDTH), k.reshape(b, s, ATTN_WIDTH), v.reshape(b, s, ATTN_WIDTH), bias)
    ssm = _ssd(xbc, z, dt_raw, conv_w, conv_b, dt_bias, a_log, d_skip, ssm_norm_gain, b, s)
    return _out_proj(attn, ssm, x, gate_m, shift_f, scale_f, norm_ffn_gain, w_out, b, s)


def _first_argmax(v, iota, limit):
    m = jnp.max(v, axis=0, keepdims=True)
    idx = jnp.min(jnp.where(v == m, iota, limit), axis=0, keepdims=True)
    return m, idx


def _router_kernel(h_ref, wt_ref, bias_ref, upper_ref, eidx_ref, rank_ref, gate_ref, counts_ref, carry_ref):
    @pl.when(pl.program_id(0) == 0)
    def _():
        carry_ref[...] = jnp.zeros_like(carry_ref)

    tm = h_ref.shape[0]
    h = h_ref[...]
    wt = wt_ref[...]
    h_hi = h.astype(BF16)
    h_lo = (h - h_hi.astype(F32)).astype(BF16)
    w_hi = wt.astype(BF16)
    w_lo = (wt - w_hi.astype(F32)).astype(BF16)
    logits = _dot_nt(w_hi, h_hi) + _dot_nt(w_hi, h_lo) + _dot_nt(w_lo, h_hi)
    scores = _sigmoid(logits)
    choice = scores + bias_ref[...]
    neg_inf = -jnp.inf

    iota_g = lax.broadcasted_iota(I32, (EXPERTS_PER_GROUP, tm), 0).astype(F32)
    group_rows = []
    for g in range(N_EXPERT_GROUPS):
        v = choice[g * EXPERTS_PER_GROUP:(g + 1) * EXPERTS_PER_GROUP]
        m1, i1 = _first_argmax(v, iota_g, float(EXPERTS_PER_GROUP))
        m2 = jnp.max(jnp.where(iota_g == i1, neg_inf, v), axis=0, keepdims=True)
        group_rows.append(m1 + m2)
    group_scores = jnp.concatenate(group_rows, axis=0)

    iota_n = lax.broadcasted_iota(I32, (N_EXPERT_GROUPS, tm), 0).astype(F32)
    chosen = jnp.zeros((N_EXPERT_GROUPS, tm), F32)
    for _ in range(TOPK_GROUPS):
        _, gi = _first_argmax(group_scores, iota_n, float(N_EXPERT_GROUPS))
        hit = iota_n == gi
        chosen = jnp.where(hit, 1.0, chosen)
        group_scores = jnp.where(hit, neg_inf, group_scores)

    masked = jnp.concatenate(
        [jnp.where(chosen[g:g + 1] > 0.0, choice[g * EXPERTS_PER_GROUP:(g + 1) * EXPERTS_PER_GROUP], neg_inf)
         for g in range(N_EXPERT_GROUPS)], axis=0)

    iota_e = lax.broadcasted_iota(I32, (N_EXPERTS, tm), 0).astype(F32)
    picked, gates = [], []
    onehot = jnp.zeros((N_EXPERTS, tm), F32)
    for _ in range(TOP_K):
        _, ei = _first_argmax(masked, iota_e, float(N_EXPERTS))
        hit = iota_e == ei
        gates.append(jnp.sum(jnp.where(hit, scores, 0.0), axis=0, keepdims=True))
        masked = jnp.where(hit, neg_inf, masked)
        onehot = jnp.where(hit, 1.0, onehot)
        picked.append(ei)
    gate_sum = gates[0]
    for gk in gates[1:]:
        gate_sum = gate_sum + gk

    base = _dot(onehot.astype(BF16), upper_ref[...]) + carry_ref[...]
    ranks = [jnp.sum(jnp.where(iota_e == ei, base, 0.0), axis=0, keepdims=True) for ei in picked]
    carry_ref[...] = carry_ref[...] + jnp.sum(onehot, axis=1, keepdims=True)

    eidx_ref[...] = jnp.concatenate(picked, axis=0).astype(I32)
    rank_ref[...] = jnp.concatenate(ranks, axis=0).astype(I32)
    gate_ref[...] = jnp.concatenate([gk / gate_sum * ROUTED_SCALE for gk in gates], axis=0)
    counts_ref[...] = carry_ref[...].astype(I32)


def _router(h2, w_router, router_bias, tm=256):
    t, d = h2.shape
    upper = jnp.asarray(np.triu(np.ones((tm, tm), np.float32), 1), BF16)
    tok = pl.BlockSpec((TOP_K, tm), lambda i: (0, i))
    full = lambda shp: pl.BlockSpec(shp, lambda i: (0,) * len(shp))
    return pl.pallas_call(
        _router_kernel,
        grid=(t // tm,),
        in_specs=[pl.BlockSpec((tm, d), lambda i: (i, 0)), full((N_EXPERTS, d)), full((N_EXPERTS, 1)), full((tm, tm))],
        out_specs=[tok, tok, tok, full((N_EXPERTS, 1))],
        out_shape=[jax.ShapeDtypeStruct((TOP_K, t), I32), jax.ShapeDtypeStruct((TOP_K, t), I32),
                   jax.ShapeDtypeStruct((TOP_K, t), F32), jax.ShapeDtypeStruct((N_EXPERTS, 1), I32)],
        scratch_shapes=[pltpu.VMEM((N_EXPERTS, 1), F32)],
        compiler_params=_params(("arbitrary",)),
        name="router",
    )(h2, w_router.T, router_bias.reshape(N_EXPERTS, 1), upper)


def _positions_kernel(counts_ref, lower_ref, eidx_ref, rank_ref, pos_ref):
    tm = eidx_ref.shape[1]
    counts = jnp.broadcast_to(counts_ref[...].astype(F32), (N_EXPERTS, LANES))
    offsets = _dot_exact_lhs(lower_ref[...], counts)[:, 0:1]
    iota_e = lax.broadcasted_iota(I32, (N_EXPERTS, tm), 0).astype(F32)
    e = eidx_ref[...].astype(F32)
    rows = [jnp.sum(jnp.where(iota_e == e[k:k + 1], offsets, 0.0), axis=0, keepdims=True) for k in range(TOP_K)]
    pos_ref[0] = jnp.concatenate(rows, axis=0).astype(I32) + rank_ref[...]


def _positions(counts, eidx, rank, tm):
    t = eidx.shape[1]
    lower = jnp.asarray(np.tril(np.ones((N_EXPERTS, N_EXPERTS), np.float32), -1), BF16)
    tok = pl.BlockSpec((TOP_K, tm), lambda i: (0, i))
    return pl.pallas_call(
        _positions_kernel,
        grid=(t // tm,),
        in_specs=[pl.BlockSpec((N_EXPERTS, 1), lambda i: (0, 0)), pl.BlockSpec((N_EXPERTS, N_EXPERTS), lambda i: (0, 0)),
                  tok, tok],
        out_specs=pl.BlockSpec((1, TOP_K, tm), lambda i: (i, 0, 0)),
        out_shape=jax.ShapeDtypeStruct((t // tm, TOP_K, tm), I32),
        compiler_params=_params(("arbitrary",)),
        name="positions",
    )(counts, lower, eidx, rank)


def _token_rows(i):
    return pl.ds(pl.multiple_of(i * TOKEN_ROWS, TOKEN_ROWS), TOKEN_ROWS)


def _dispatch_kernel(pos_ref, h_ref, xs_ref, sem):
    tm = h_ref.shape[0] // TOKEN_ROWS

    def row_copy(t, k):
        return pltpu.make_async_copy(h_ref.at[_token_rows(t)], xs_ref.at[_token_rows(pos_ref[0, k, t])], sem)

    def start(t, carry):
        for k in range(TOP_K):
            row_copy(t, k).start(priority=k % 2)
        return carry

    def wait(t, carry):
        for k in range(TOP_K):
            row_copy(t, k).wait()
        return carry

    lax.fori_loop(0, tm, start, 0)
    lax.fori_loop(0, tm, wait, 0)


def _dispatch(h2t, pos_tiles, tm):
    t = h2t.shape[0] // TOKEN_ROWS
    return pl.pallas_call(
        _dispatch_kernel,
        grid=(t // tm,),
        in_specs=[pl.BlockSpec((1, TOP_K, tm), lambda i: (i, 0, 0), memory_space=pltpu.SMEM),
                  pl.BlockSpec((tm * TOKEN_ROWS, LANES), lambda i: (i, 0))],
        out_specs=pl.BlockSpec(memory_space=pl.ANY),
        out_shape=jax.ShapeDtypeStruct((t * TOP_K * TOKEN_ROWS, LANES), h2t.dtype),
        scratch_shapes=[pltpu.SemaphoreType.DMA(())],
        compiler_params=_params(("arbitrary",)),
        name="dispatch",
    )(pos_tiles, h2t)


def _experts_kernel(tile_ref, exp_ref, lo_ref, hi_ref, xs_ref, wg_ref, wu_ref, wd_ref, ys_ref):
    w = pl.program_id(0)
    tmg = xs_ref.shape[0] // TOKEN_ROWS
    tile = tile_ref[w]
    lo, hi = lo_ref[w], hi_ref[w]
    new_tile = jnp.logical_or(w == 0, tile_ref[jnp.maximum(w - 1, 0)] != tile)

    @pl.when(new_tile)
    def _():
        ys_ref[...] = jnp.zeros_like(ys_ref)

    @pl.when(hi > lo)
    def _():
        x = _from_token_tiles(xs_ref, tmg).astype(BF16)
        g = _dot(x, wg_ref[...].astype(BF16))
        u = _dot(x, wu_ref[...].astype(BF16))
        y = _dot((_silu(g) * u).astype(BF16), wd_ref[...].astype(BF16))
        row = tile * tmg + lax.broadcasted_iota(I32, (tmg, 1), 0)
        inside = jnp.logical_and(row >= lo, row < hi)
        _to_token_tiles(ys_ref, jnp.where(inside, y, _from_token_tiles(ys_ref, tmg)))


def _group_metadata(counts, n_rows, tmg):
    n_tiles = n_rows // tmg
    n_work = n_tiles + N_EXPERTS
    ends = jnp.cumsum(counts)
    starts = ends - counts
    first_tile = starts // tmg
    n_items = jnp.where(counts > 0, (ends - 1) // tmg - first_tile + 1, 0)
    item_ends = jnp.cumsum(n_items)
    w = jnp.arange(n_work, dtype=I32)
    valid = w < item_ends[-1]
    e = jnp.minimum(jnp.searchsorted(item_ends, w, side="right"), N_EXPERTS - 1).astype(I32)
    tile = first_tile[e] + (w - (item_ends[e] - n_items[e]))
    lo = jnp.maximum(starts[e], tile * tmg)
    hi = jnp.minimum(ends[e], (tile + 1) * tmg)
    last_e = jnp.max(jnp.where(counts > 0, jnp.arange(N_EXPERTS, dtype=I32), 0))
    tile = jnp.where(valid, tile, n_tiles - 1).astype(I32)
    e = jnp.where(valid, e, last_e).astype(I32)
    lo = jnp.where(valid, lo, 0).astype(I32)
    hi = jnp.where(valid, hi, 0).astype(I32)
    return tile, e, lo, hi


def _experts(xs, counts, w_gate, w_up, w_down, tmg=256):
    d = D_MODEL
    n = xs.shape[0] // TOKEN_ROWS
    tile, e, lo, hi = _group_metadata(counts, n, tmg)
    n_work = tile.shape[0]
    grid_spec = pltpu.PrefetchScalarGridSpec(
        num_scalar_prefetch=4,
        grid=(n_work,),
        in_specs=[pl.BlockSpec((tmg * TOKEN_ROWS, LANES), lambda w, tile, e, lo, hi: (tile[w], 0)),
                  pl.BlockSpec((None, d, EXPERT_FF), lambda w, tile, e, lo, hi: (e[w], 0, 0)),
                  pl.BlockSpec((None, d, EXPERT_FF), lambda w, tile, e, lo, hi: (e[w], 0, 0)),
                  pl.BlockSpec((None, EXPERT_FF, d), lambda w, tile, e, lo, hi: (e[w], 0, 0))],
        out_specs=pl.BlockSpec((tmg * TOKEN_ROWS, LANES), lambda w, tile, e, lo, hi: (tile[w], 0)),
    )
    return pl.pallas_call(
        _experts_kernel,
        grid_spec=grid_spec,
        out_shape=jax.ShapeDtypeStruct((n * TOKEN_ROWS, LANES), F32),
        compiler_params=_params(("arbitrary",)),
        name="experts",
    )(tile, e, lo, hi, xs, w_gate, w_up, w_down)


def _combine_kernel(pos_ref, gates_ref, h_ref, x1_ref, gatef_ref, wg_ref, wu_ref, wd_ref, ys_ref, out_ref, buf, sem):
    tm = h_ref.shape[0]

    def row_copy(t, k):
        return pltpu.make_async_copy(ys_ref.at[_token_rows(pos_ref[0, k, t])], buf.at[k, _token_rows(t)], sem)

    def start(t, carry):
        for k in range(TOP_K):
            row_copy(t, k).start(priority=k % 2)
        return carry

    def wait(t, carry):
        for k in range(TOP_K):
            row_copy(t, k).wait()
        return carry

    lax.fori_loop(0, tm, start, 0)
    hb = h_ref[...].astype(BF16)
    shared = _dot((_silu(_dot(hb, wg_ref[...])) * _dot(hb, wu_ref[...])).astype(BF16), wd_ref[...])
    lax.fori_loop(0, tm, wait, 0)
    gates = gates_ref[...]
    routed = _from_token_tiles(buf.at[0], tm) * gates[:, 0:1]
    for k in range(1, TOP_K):
        routed = routed + _from_token_tiles(buf.at[k], tm) * gates[:, k:k + 1]
    out_ref[...] = x1_ref[...] + gatef_ref[...] * (shared + routed)


def _combine(ys, pos_tiles, gates_t, h2, x1, gate_f, w_gate_s, w_up_s, w_down_s, b, s, tm):
    t, d = h2.shape
    tiles_per_seq = s // tm
    row = lambda n: pl.BlockSpec((tm, n), lambda i: (i, 0))
    full = lambda shp: pl.BlockSpec(shp, lambda i: (0,) * len(shp))
    return pl.pallas_call(
        _combine_kernel,
        grid=(t // tm,),
        in_specs=[pl.BlockSpec((1, TOP_K, tm), lambda i: (i, 0, 0), memory_space=pltpu.SMEM),
                  row(TOP_K), row(d), row(d),
                  pl.BlockSpec((None, 1, d), lambda i: (i // tiles_per_seq, 0, 0)),
                  full((d, EXPERT_FF)), full((d, EXPERT_FF)), full((EXPERT_FF, d)),
                  pl.BlockSpec(memory_space=pl.ANY)],
        out_specs=row(d),
        out_shape=jax.ShapeDtypeStruct((t, d), F32),
        scratch_shapes=[pltpu.VMEM((TOP_K, tm * TOKEN_ROWS, LANES), F32), pltpu.SemaphoreType.DMA(())],
        compiler_params=_params(("arbitrary",)),
        name="combine",
    )(pos_tiles, gates_t, h2, x1, gate_f.reshape(b, 1, d),
      w_gate_s.astype(BF16), w_up_s.astype(BF16), w_down_s.astype(BF16), ys)


def _moe_sublayer(x1, h2, h2t, gate_f, w_router, router_bias, w_gate, w_up, w_down, w_gate_s, w_up_s, w_down_s, b, s,
                  tm=256):
    t = b * s
    eidx, rank, gates, counts = _router(h2, w_router, router_bias)
    pos_tiles = _positions(counts, eidx, rank, tm)
    xs = _dispatch(h2t, pos_tiles, tm)
    ys = _experts(xs, counts[:, 0], w_gate, w_up, w_down)
    return _combine(ys, pos_tiles, gates.T, h2, x1, gate_f, w_gate_s, w_up_s, w_down_s, b, s, tm)


def kernel(x, c, w_ada, b_ada, norm_mix_gain, w_in, q_norm_gain, k_norm_gain, rel_bias_table, conv_w, conv_b, dt_bias,
           a_log, d_skip, ssm_norm_gain, w_out, norm_ffn_gain, w_router, router_bias, w_gate_experts, w_up_experts,
           w_down_experts, w_gate_shared, w_up_shared, w_down_shared):
    b, s, d = x.shape
    for layer in range(w_ada.shape[0]):
        mod = _adaln(c, w_ada[layer], b_ada[layer])
        x1, h2, h2t = _mixer_sublayer(x, mod, norm_mix_gain[layer], w_in[layer], q_norm_gain[layer], k_norm_gain[layer],
                                 rel_bias_table, conv_w[layer], conv_b[layer], dt_bias[layer], a_log[layer],
                                 d_skip[layer], ssm_norm_gain[layer], w_out[layer], norm_ffn_gain[layer])
        gate_f = mod[:, 5 * d:]
        out = _moe_sublayer(x1, h2, h2t, gate_f, w_router[layer], router_bias[layer], w_gate_experts[layer],
                            w_up_experts[layer], w_down_experts[layer], w_gate_shared[layer], w_up_shared[layer],
                            w_down_shared[layer], b, s)
        x = out.reshape(b, s, d)
    return x
```

```python
import functools
import math

import numpy as np
import jax
import jax.numpy as jnp
from jax import lax
from jax.experimental import pallas as pl
from jax.experimental.pallas import tpu as pltpu

F32 = jnp.float32
BF16 = jnp.bfloat16
I32 = jnp.int32

D_MODEL = 1024
ATTN_HEADS = 8
HEAD_DIM = 64
ATTN_WIDTH = ATTN_HEADS * HEAD_DIM
PATTERNS = ((128, 1), (512, 4), (2048, 16))
WIN_STEPS = 128
REL_BUCKETS = 32
REL_MAX_DISTANCE = 2048
SSM_HEADS = 24
SSM_HEAD_DIM = 64
SSM_WIDTH = SSM_HEADS * SSM_HEAD_DIM
SSM_GROUPS = 4
HEADS_PER_GROUP = SSM_HEADS // SSM_GROUPS
GROUP_WIDTH = SSM_WIDTH // SSM_GROUPS
SSM_STATE = 128
SSM_CONV = 4
SSM_CHUNK = 128
CONV_CH = SSM_WIDTH + 2 * SSM_GROUPS * SSM_STATE
N_EXPERTS = 256
TOP_K = 8
N_EXPERT_GROUPS = 8
EXPERTS_PER_GROUP = N_EXPERTS // N_EXPERT_GROUPS
TOPK_GROUPS = 4
EXPERT_FF = 256
ROUTED_SCALE = 2.5
NORM_EPS = 1e-6

LANES = 128
SUBLANES = 8
NEG_BIG = -1e30
VMEM_LIMIT = 56 * 1024 * 1024


def _params(sem, vmem=VMEM_LIMIT):
    return pltpu.CompilerParams(dimension_semantics=sem, vmem_limit_bytes=vmem)


def _sigmoid(x):
    return 1.0 / (1.0 + jnp.exp(-x))


def _silu(x):
    return x * _sigmoid(x)


def _split3(x):
    hi = x.astype(BF16)
    r = x - hi.astype(F32)
    mid = r.astype(BF16)
    lo = (r - mid.astype(F32)).astype(BF16)
    return hi, mid, lo


def _dot(a, b):
    return jnp.dot(a, b, preferred_element_type=F32)


def _dot_nt(a, b):
    return lax.dot_general(a, b, (((1,), (1,)), ((), ())), preferred_element_type=F32)


def _dot_exact_rhs(a, b_exact):
    hi, mid, lo = _split3(a)
    return _dot(hi, b_exact) + _dot(mid, b_exact) + _dot(lo, b_exact)


def _dot_exact_lhs(a_exact, b):
    hi, mid, lo = _split3(b)
    return _dot(a_exact, hi) + _dot(a_exact, mid) + _dot(a_exact, lo)


def _adaln_kernel(c_ref, w_ref, b_ref, o_ref):
    s = _silu(c_ref[...]).astype(BF16)
    o_ref[...] = _dot(s, w_ref[...].astype(BF16)) + b_ref[...]


def _adaln(c, w_ada, b_ada):
    b, d = c.shape
    n = w_ada.shape[1]
    rows = SUBLANES
    c_pad = jnp.zeros((rows, d), F32).at[:b].set(c)
    tn = 1024
    out = pl.pallas_call(
        _adaln_kernel,
        grid=(n // tn,),
        in_specs=[pl.BlockSpec((rows, d), lambda j: (0, 0)),
                  pl.BlockSpec((d, tn), lambda j: (0, j)),
                  pl.BlockSpec((1, tn), lambda j: (0, j))],
        out_specs=pl.BlockSpec((rows, tn), lambda j: (0, j)),
        out_shape=jax.ShapeDtypeStruct((rows, n), F32),
        compiler_params=_params(("arbitrary",)),
        name="adaln",
    )(c_pad, w_ada, b_ada.reshape(1, n))
    return out[:b]


def _inproj_kernel(x_ref, shift_ref, scale_ref, g_ref, wqkv_ref, wz_ref, wxbc_ref, wdt_ref,
                   qg_ref, kg_ref, hmean_ref, q_ref, k_ref, v_ref, z_ref, xbc_ref, dt_ref):
    x = x_ref[...]
    ms = jnp.mean(x * x, axis=-1, keepdims=True)
    h = x * lax.rsqrt(ms + NORM_EPS) * g_ref[...]
    h = h * (1.0 + scale_ref[...]) + shift_ref[...]
    hb = h.astype(BF16)

    hmean = hmean_ref[...]

    def head_norm(t, gain):
        ss = _dot_exact_rhs(t * t, hmean)
        return t * lax.rsqrt(ss + NORM_EPS) * gain

    q = _dot(hb, wqkv_ref[:, 0:ATTN_WIDTH])
    q_ref[...] = head_norm(q, qg_ref[...]) * (HEAD_DIM ** -0.5)
    k = _dot(hb, wqkv_ref[:, ATTN_WIDTH:2 * ATTN_WIDTH])
    k_ref[...] = head_norm(k, kg_ref[...])
    v_ref[...] = _dot(hb, wqkv_ref[:, 2 * ATTN_WIDTH:3 * ATTN_WIDTH])
    for c0 in range(0, SSM_WIDTH, 512):
        z_ref[:, c0:c0 + 512] = _dot(hb, wz_ref[:, c0:c0 + 512])
    for c0 in range(0, CONV_CH, 512):
        xbc_ref[:, c0:c0 + 512] = _dot(hb, wxbc_ref[:, c0:c0 + 512])
    dt_ref[...] = _dot(hb, wdt_ref[...])


def _in_proj(x, shift, scale, gain, w_in, q_gain, k_gain, tm=256):
    b, s, d = x.shape
    t = b * s
    tiles_per_seq = s // tm
    w = w_in.astype(BF16)
    o_z = 3 * ATTN_WIDTH
    o_x = o_z + SSM_WIDTH
    o_dt = o_x + CONV_CH
    w_qkv, w_z, w_xbc = w[:, :o_z], w[:, o_z:o_x], w[:, o_x:o_dt]
    w_dt = jnp.zeros((d, LANES), BF16).at[:, :SSM_HEADS].set(w[:, o_dt:])
    head_of = np.arange(ATTN_WIDTH) // HEAD_DIM
    hmean = jnp.asarray((head_of[:, None] == head_of[None, :]).astype(np.float32) / HEAD_DIM, BF16)
    full = lambda shp: pl.BlockSpec(shp, lambda i: (0,) * len(shp))
    row = lambda n: pl.BlockSpec((tm, n), lambda i: (i, 0))
    per_batch = pl.BlockSpec((None, 1, d), lambda i: (i // tiles_per_seq, 0, 0))
    outs = pl.pallas_call(
        _inproj_kernel,
        grid=(t // tm,),
        in_specs=[row(d), per_batch, per_batch, full((1, d)),
                  full((d, o_z)), full((d, SSM_WIDTH)), full((d, CONV_CH)), full((d, LANES)),
                  full((1, ATTN_WIDTH)), full((1, ATTN_WIDTH)), full((ATTN_WIDTH, ATTN_WIDTH))],
        out_specs=[row(ATTN_WIDTH), row(ATTN_WIDTH), row(ATTN_WIDTH), row(SSM_WIDTH), row(CONV_CH), row(LANES)],
        out_shape=[jax.ShapeDtypeStruct((t, n), F32)
                   for n in (ATTN_WIDTH, ATTN_WIDTH, ATTN_WIDTH, SSM_WIDTH, CONV_CH, LANES)],
        compiler_params=_params(("arbitrary",)),
        name="in_proj",
    )(x.reshape(t, d), shift.reshape(b, 1, d), scale.reshape(b, 1, d), gain.reshape(1, d),
      w_qkv, w_z, w_xbc, w_dt,
      jnp.tile(q_gain, ATTN_HEADS).reshape(1, ATTN_WIDTH), jnp.tile(k_gain, ATTN_HEADS).reshape(1, ATTN_WIDTH), hmean)
    return outs


def _t5_causal_buckets(distance):
    n = np.maximum(distance, 0)
    max_exact = REL_BUCKETS // 2
    large = max_exact + (np.log(np.maximum(n, 1) / max_exact) / math.log(REL_MAX_DISTANCE / max_exact)
                         * (REL_BUCKETS - max_exact)).astype(np.int64)
    large = np.minimum(large, REL_BUCKETS - 1)
    return np.where(n < max_exact, n, large).astype(np.int32)


def _window_bias(rel_bias_table, dilation):
    qi = np.arange(WIN_STEPS)[:, None]
    kj = np.arange(2 * WIN_STEPS)[None, :]
    dist = qi + WIN_STEPS - kj
    band = (dist >= 0) & (dist <= WIN_STEPS)
    onehot = (_t5_causal_buckets(dist * dilation).reshape(-1, 1) == np.arange(REL_BUCKETS)[None, :]).astype(np.float32)
    bias = jnp.dot(rel_bias_table.astype(F32).T, jnp.asarray(onehot).T, precision=lax.Precision.HIGHEST)
    bias = bias.reshape(ATTN_HEADS, WIN_STEPS, 2 * WIN_STEPS)
    return jnp.where(jnp.asarray(band)[None], bias, NEG_BIG)


ATTN_TOKENS = max(w for w, _ in PATTERNS)
ATTN_UNROLL = 4


def _attn_kernel(q_ref, kp_ref, kc_ref, vp_ref, vc_ref, bias_ref, out_ref, kw, vw, o_acc, l_acc):
    tb = ATTN_TOKENS
    first = pl.program_id(2) == 0
    kw[0:tb] = kp_ref[...]
    kw[tb:2 * tb] = kc_ref[...]
    vw[0:tb] = vp_ref[...]
    vw[tb:2 * tb] = vc_ref[...]
    lane = lax.broadcasted_iota(I32, (WIN_STEPS, LANES), 1)
    head0 = lane < HEAD_DIM
    col = lax.broadcasted_iota(I32, (WIN_STEPS, 2 * WIN_STEPS), 1)
    in_prev = col < WIN_STEPS

    for p, (_, d) in enumerate(PATTERNS):
        shift = d.bit_length() - 1
        n_blocks = tb // WIN_STEPS

        def rows(start, n, d=d):
            return pl.ds(start, n, stride=d) if d > 1 else pl.ds(start, n)

        def body(it, carry, p=p, d=d, shift=shift, rows=rows):
            for u in range(ATTN_UNROLL):
                idx = it * ATTN_UNROLL + u
                r = jnp.bitwise_and(idx, d - 1)
                j = jnp.right_shift(idx, shift)
                qs = j * (WIN_STEPS * d) + r
                q = q_ref[rows(qs, WIN_STEPS), :]
                k = kw[rows(tb + qs - WIN_STEPS * d, 2 * WIN_STEPS), :].astype(BF16)
                v = vw[rows(tb + qs - WIN_STEPS * d, 2 * WIN_STEPS), :].astype(BF16)
                no_prev = jnp.logical_and(in_prev, jnp.logical_and(first, j == 0))
                o_h, lse_h = [], []
                for h in range(2):
                    qh = jnp.where(head0 if h == 0 else jnp.logical_not(head0), q, 0.0).astype(BF16)
                    s = _dot_nt(qh, k) + bias_ref[p, h]
                    s = jnp.where(no_prev, NEG_BIG, s)
                    m = jnp.max(s, axis=-1, keepdims=True)
                    e = jnp.exp(s - m)
                    denom = jnp.sum(e, axis=-1, keepdims=True)
                    o_h.append(_dot(e.astype(BF16), v) / denom)
                    lse_h.append(m + jnp.log(denom))
                o_acc[p, rows(qs, WIN_STEPS), :] = jnp.where(head0, o_h[0], o_h[1])
                l_acc[p, rows(qs, WIN_STEPS), :] = jnp.where(head0, lse_h[0], lse_h[1])
            return carry

        lax.fori_loop(0, n_blocks // ATTN_UNROLL, body, 0)

    chunk = 256
    for c0 in range(0, tb, chunk):
        l1, l2, l3 = (l_acc[p, c0:c0 + chunk, :] for p in range(3))
        m = jnp.maximum(jnp.maximum(l1, l2), l3)
        e1, e2, e3 = jnp.exp(l1 - m), jnp.exp(l2 - m), jnp.exp(l3 - m)
        num = e1 * o_acc[0, c0:c0 + chunk, :] + e2 * o_acc[1, c0:c0 + chunk, :] + e3 * o_acc[2, c0:c0 + chunk, :]
        out_ref[c0:c0 + chunk, :] = num / (e1 + e2 + e3)


def _attention(q, k, v, bias):
    b, s, w = q.shape
    tb = ATTN_TOKENS
    pairs = ATTN_HEADS // 2
    cur = pl.BlockSpec((None, tb, LANES), lambda bi, hp, i: (bi, i, hp))
    prev = pl.BlockSpec((None, tb, LANES), lambda bi, hp, i: (bi, jnp.maximum(i - 1, 0), hp))
    return pl.pallas_call(
        _attn_kernel,
        grid=(b, pairs, s // tb),
        in_specs=[cur, prev, cur, prev, cur,
                  pl.BlockSpec((len(PATTERNS), 2, WIN_STEPS, 2 * WIN_STEPS), lambda bi, hp, i: (0, hp, 0, 0))],
        out_specs=cur,
        out_shape=jax.ShapeDtypeStruct((b, s, w), F32),
        scratch_shapes=[pltpu.VMEM((2 * tb, LANES), F32), pltpu.VMEM((2 * tb, LANES), F32),
                        pltpu.VMEM((len(PATTERNS), tb, LANES), F32), pltpu.VMEM((len(PATTERNS), tb, LANES), F32)],
        compiler_params=_params(("arbitrary",) * 3),
        name="attention",
    )(q, k, k, v, v, bias)


def _ssd_kernel(xbc_ref, halo_ref, z_ref, dtraw_ref, convw_ref, convb_ref, dtb_ref, alog_ref, dskip_ref, gain_ref,
                expand_ref, tril_ref, y_ref, state_ref):
    c = pl.program_id(1)

    @pl.when(c == 0)
    def _():
        state_ref[...] = jnp.zeros_like(state_ref)

    x = xbc_ref[...]
    halo = jnp.where(c == 0, 0.0, halo_ref[...])
    w = convw_ref[...]
    acc = x * w[SSM_CONV - 1:SSM_CONV, :] + convb_ref[...]
    row8 = lax.broadcasted_iota(I32, (SUBLANES, CONV_CH), 0)
    for shift in range(1, SSM_CONV):
        xs = pltpu.roll(x, shift, axis=0)
        hs = pltpu.roll(halo, shift, axis=0)
        head = jnp.where(row8 < shift, hs, xs[0:SUBLANES])
        xs = jnp.concatenate([head, xs[SUBLANES:]], axis=0)
        acc = acc + xs * w[SSM_CONV - 1 - shift:SSM_CONV - shift, :]
    act = _silu(acc)
    x_s = act[:, :SSM_WIDTH]
    bc0 = SSM_WIDTH
    cc0 = SSM_WIDTH + SSM_GROUPS * SSM_STATE

    t = dtraw_ref[...] + dtb_ref[...]
    dt = jnp.maximum(t, 0.0) + jnp.log(1.0 + jnp.exp(-jnp.abs(t)))
    a = dt * (-jnp.exp(alog_ref[...]))
    a_cs = _dot_exact_lhs(tril_ref[...], a)
    a_cs_t = a_cs.T
    a_last = a_cs[SSM_CHUNK - 1:SSM_CHUNK, :]
    expand = expand_ref[...]
    dt_e = _dot_exact_rhs(dt, expand)
    ea_e = _dot_exact_rhs(jnp.exp(a_cs), expand)
    dte_e = _dot_exact_rhs(jnp.exp(a_last - a_cs), expand)
    xdt = x_s * dt_e
    xw = (xdt * dte_e).astype(BF16)
    xdt_b = xdt.astype(BF16)

    li = lax.broadcasted_iota(I32, (SSM_CHUNK, SSM_CHUNK), 0)
    si = lax.broadcasted_iota(I32, (SSM_CHUNK, SSM_CHUNK), 1)
    causal = li >= si

    ys = []
    for g in range(SSM_GROUPS):
        gs = slice(g * GROUP_WIDTH, (g + 1) * GROUP_WIDTH)
        b_g = act[:, bc0 + g * SSM_STATE:bc0 + (g + 1) * SSM_STATE]
        c_g = act[:, cc0 + g * SSM_STATE:cc0 + (g + 1) * SSM_STATE].astype(BF16)
        cb = _dot_nt(c_g, b_g.astype(BF16))
        state = state_ref[g]
        y_off = _dot(c_g, state.astype(BF16)) * ea_e[:, gs]
        parts = []
        for j in range(HEADS_PER_GROUP):
            hh = g * HEADS_PER_GROUP + j
            seg = a_cs[:, hh:hh + 1] - a_cs_t[hh:hh + 1, :]
            decay = jnp.exp(jnp.where(causal, seg, NEG_BIG))
            m = (cb * decay).astype(BF16)
            parts.append(_dot(m, xdt_b[:, hh * SSM_HEAD_DIM:(hh + 1) * SSM_HEAD_DIM]))
        ys.append(jnp.concatenate(parts, axis=1) + y_off)
        state_ref[g] = state * ea_e[SSM_CHUNK - 1:SSM_CHUNK, gs] + _dot(b_g.T.astype(BF16), xw[:, gs])
    y = jnp.concatenate(ys, axis=1) + dskip_ref[...] * x_s
    y = y * _silu(z_ref[...])
    gain = gain_ref[...]
    for g in range(SSM_GROUPS):
        gs = slice(g * GROUP_WIDTH, (g + 1) * GROUP_WIDTH)
        yg = y[:, gs]
        ms = jnp.mean(yg * yg, axis=-1, keepdims=True)
        y_ref[:, gs] = yg * lax.rsqrt(ms + NORM_EPS) * gain[:, gs]


def _ssd(xbc, z, dt_raw, conv_w, conv_b, dt_bias, a_log, d_skip, norm_gain, b, s):
    t = b * s
    nc = s // SSM_CHUNK
    pad_heads = lambda v: jnp.zeros((1, LANES), F32).at[0, :SSM_HEADS].set(v)
    head_of_lane = np.arange(SSM_WIDTH) // SSM_HEAD_DIM
    expand = jnp.asarray((np.arange(LANES)[:, None] == head_of_lane[None, :]).astype(np.float32), BF16)
    tril = jnp.asarray(np.tril(np.ones((SSM_CHUNK, SSM_CHUNK), np.float32)), BF16)
    halo_blocks = SSM_CHUNK // SUBLANES
    chunk = lambda n: pl.BlockSpec((SSM_CHUNK, n), lambda bi, c: (bi * nc + c, 0))
    full = lambda shp: pl.BlockSpec(shp, lambda bi, c: (0,) * len(shp))
    halo = pl.BlockSpec((SUBLANES, CONV_CH), lambda bi, c: (jnp.maximum((bi * nc + c) * halo_blocks - 1, 0), 0))
    return pl.pallas_call(
        _ssd_kernel,
        grid=(b, nc),
        in_specs=[chunk(CONV_CH), halo, chunk(SSM_WIDTH), chunk(LANES),
                  full((SSM_CONV, CONV_CH)), full((1, CONV_CH)), full((1, LANES)), full((1, LANES)),
                  full((1, SSM_WIDTH)), full((1, SSM_WIDTH)), full((LANES, SSM_WIDTH)), full((SSM_CHUNK, SSM_CHUNK))],
        out_specs=chunk(SSM_WIDTH),
        out_shape=jax.ShapeDtypeStruct((t, SSM_WIDTH), F32),
        scratch_shapes=[pltpu.VMEM((SSM_GROUPS, SSM_STATE, GROUP_WIDTH), F32)],
        compiler_params=_params(("arbitrary", "arbitrary")),
        name="ssd",
    )(xbc, xbc, z, dt_raw, conv_w, conv_b.reshape(1, CONV_CH), pad_heads(dt_bias), pad_heads(a_log),
      jnp.repeat(d_skip, SSM_HEAD_DIM).reshape(1, SSM_WIDTH), norm_gain.reshape(1, SSM_WIDTH), expand, tril)


TOKEN_ROWS = D_MODEL // LANES


def _to_token_tiles(ref, x):
    n = x.shape[0]
    for c in range(TOKEN_ROWS):
        ref[pl.ds(c, n, stride=TOKEN_ROWS), :] = x[:, c * LANES:(c + 1) * LANES]


def _from_token_tiles(ref, n):
    return jnp.concatenate([ref[pl.ds(c, n, stride=TOKEN_ROWS), :] for c in range(TOKEN_ROWS)], axis=1)


def _outproj_kernel(attn_ref, ssm_ref, x_ref, gate_ref, shift_ref, scale_ref, g_ref, wa_ref, ws_ref, x1_ref, h2_ref,
                    h2t_ref):
    mixed = _dot(attn_ref[...].astype(BF16), wa_ref[...]) + _dot(ssm_ref[...].astype(BF16), ws_ref[...])
    x1 = x_ref[...] + gate_ref[...] * mixed
    x1_ref[...] = x1
    ms = jnp.mean(x1 * x1, axis=-1, keepdims=True)
    h = x1 * lax.rsqrt(ms + NORM_EPS) * g_ref[...]
    h2 = h * (1.0 + scale_ref[...]) + shift_ref[...]
    h2_ref[...] = h2
    _to_token_tiles(h2t_ref, h2)


def _out_proj(attn, ssm, x, gate, shift, scale, gain, w_out, b, s, tm=256):
    t = b * s
    d = D_MODEL
    tiles_per_seq = s // tm
    w = w_out.astype(BF16)
    row = lambda n: pl.BlockSpec((tm, n), lambda i: (i, 0))
    full = lambda shp: pl.BlockSpec(shp, lambda i: (0,) * len(shp))
    per_batch = pl.BlockSpec((None, 1, d), lambda i: (i // tiles_per_seq, 0, 0))
    return pl.pallas_call(
        _outproj_kernel,
        grid=(t // tm,),
        in_specs=[row(ATTN_WIDTH), row(SSM_WIDTH), row(d), per_batch, per_batch, per_batch, full((1, d)),
                  full((ATTN_WIDTH, d)), full((SSM_WIDTH, d))],
        out_specs=[row(d), row(d), pl.BlockSpec((tm * TOKEN_ROWS, LANES), lambda i: (i, 0))],
        out_shape=[jax.ShapeDtypeStruct((t, d), F32)] * 2 + [jax.ShapeDtypeStruct((t * TOKEN_ROWS, LANES), F32)],
        compiler_params=_params(("arbitrary",)),
        name="out_proj",
    )(attn.reshape(t, ATTN_WIDTH), ssm, x.reshape(t, d),
      gate.reshape(b, 1, d), shift.reshape(b, 1, d), scale.reshape(b, 1, d), gain.reshape(1, d),
      w[:ATTN_WIDTH], w[ATTN_WIDTH:])


def _mixer_sublayer(x, mod, norm_mix_gain, w_in, q_norm_gain, k_norm_gain, rel_bias_table, conv_w, conv_b, dt_bias,
                    a_log, d_skip, ssm_norm_gain, w_out, norm_ffn_gain):
    b, s, d = x.shape
    shift_m, scale_m, gate_m, shift_f, scale_f, _ = jnp.split(mod, 6, axis=-1)
    q, k, v, z, xbc, dt_raw = _in_proj(x, shift_m, scale_m, norm_mix_gain, w_in, q_norm_gain, k_norm_gain)
    bias = jnp.stack([_window_bias(rel_bias_table, dilation) for _, dilation in PATTERNS])
    attn = _attention(q.reshape(b, s, ATTN_WIDTH), k.reshape(b, s, ATTN_WIDTH), v.reshape(b, s, ATTN_WIDTH), bias)
    ssm = _ssd(xbc, z, dt_raw, conv_w, conv_b, dt_bias, a_log, d_skip, ssm_norm_gain, b, s)
    return _out_proj(attn, ssm, x, gate_m, shift_f, scale_f, norm_ffn_gain, w_out, b, s)


def _first_argmax(v, iota, limit):
    m = jnp.max(v, axis=0, keepdims=True)
    idx = jnp.min(jnp.where(v == m, iota, limit), axis=0, keepdims=True)
    return m, idx


def _router_kernel(h_ref, wt_ref, bias_ref, upper_ref, eidx_ref, rank_ref, gate_ref, counts_ref, carry_ref):
    @pl.when(pl.program_id(0) == 0)
    def _():
        carry_ref[...] = jnp.zeros_like(carry_ref)

    tm = h_ref.shape[0]
    h = h_ref[...]
    wt = wt_ref[...]
    h_hi = h.astype(BF16)
    h_lo = (h - h_hi.astype(F32)).astype(BF16)
    w_hi = wt.astype(BF16)
    w_lo = (wt - w_hi.astype(F32)).astype(BF16)
    logits = _dot_nt(w_hi, h_hi) + _dot_nt(w_hi, h_lo) + _dot_nt(w_lo, h_hi)
    scores = _sigmoid(logits)
    choice = scores + bias_ref[...]
    neg_inf = -jnp.inf

    iota_g = lax.broadcasted_iota(I32, (EXPERTS_PER_GROUP, tm), 0).astype(F32)
    group_rows = []
    for g in range(N_EXPERT_GROUPS):
        v = choice[g * EXPERTS_PER_GROUP:(g + 1) * EXPERTS_PER_GROUP]
        m1, i1 = _first_argmax(v, iota_g, float(EXPERTS_PER_GROUP))
        m2 = jnp.max(jnp.where(iota_g == i1, neg_inf, v), axis=0, keepdims=True)
        group_rows.append(m1 + m2)
    group_scores = jnp.concatenate(group_rows, axis=0)

    iota_n = lax.broadcasted_iota(I32, (N_EXPERT_GROUPS, tm), 0).astype(F32)
    chosen = jnp.zeros((N_EXPERT_GROUPS, tm), F32)
    for _ in range(TOPK_GROUPS):
        _, gi = _first_argmax(group_scores, iota_n, float(N_EXPERT_GROUPS))
        hit = iota_n == gi
        chosen = jnp.where(hit, 1.0, chosen)
        group_scores = jnp.where(hit, neg_inf, group_scores)

    masked = jnp.concatenate(
        [jnp.where(chosen[g:g + 1] > 0.0, choice[g * EXPERTS_PER_GROUP:(g + 1) * EXPERTS_PER_GROUP], neg_inf)
         for g in range(N_EXPERT_GROUPS)], axis=0)

    iota_e = lax.broadcasted_iota(I32, (N_EXPERTS, tm), 0).astype(F32)
    picked, gates = [], []
    onehot = jnp.zeros((N_EXPERTS, tm), F32)
    for _ in range(TOP_K):
        _, ei = _first_argmax(masked, iota_e, float(N_EXPERTS))
        hit = iota_e == ei
        gates.append(jnp.sum(jnp.where(hit, scores, 0.0), axis=0, keepdims=True))
        masked = jnp.where(hit, neg_inf, masked)
        onehot = jnp.where(hit, 1.0, onehot)
        picked.append(ei)
    gate_sum = gates[0]
    for gk in gates[1:]:
        gate_sum = gate_sum + gk

    base = _dot(onehot.astype(BF16), upper_ref[...]) + carry_ref[...]
    ranks = [jnp.sum(jnp.where(iota_e == ei, base, 0.0), axis=0, keepdims=True) for ei in picked]
    carry_ref[...] = carry_ref[...] + jnp.sum(onehot, axis=1, keepdims=True)

    eidx_ref[...] = jnp.concatenate(picked, axis=0).astype(I32)
    rank_ref[...] = jnp.concatenate(ranks, axis=0).astype(I32)
    gate_ref[...] = jnp.concatenate([gk / gate_sum * ROUTED_SCALE for gk in gates], axis=0)
    counts_ref[...] = carry_ref[...].astype(I32)


def _router(h2, w_router, router_bias, tm=256):
    t, d = h2.shape
    upper = jnp.asarray(np.triu(np.ones((tm, tm), np.float32), 1), BF16)
    tok = pl.BlockSpec((TOP_K, tm), lambda i: (0, i))
    full = lambda shp: pl.BlockSpec(shp, lambda i: (0,) * len(shp))
    return pl.pallas_call(
        _router_kernel,
        grid=(t // tm,),
        in_specs=[pl.BlockSpec((tm, d), lambda i: (i, 0)), full((N_EXPERTS, d)), full((N_EXPERTS, 1)), full((tm, tm))],
        out_specs=[tok, tok, tok, full((N_EXPERTS, 1))],
        out_shape=[jax.ShapeDtypeStruct((TOP_K, t), I32), jax.ShapeDtypeStruct((TOP_K, t), I32),
                   jax.ShapeDtypeStruct((TOP_K, t), F32), jax.ShapeDtypeStruct((N_EXPERTS, 1), I32)],
        scratch_shapes=[pltpu.VMEM((N_EXPERTS, 1), F32)],
        compiler_params=_params(("arbitrary",)),
        name="router",
    )(h2, w_router.T, router_bias.reshape(N_EXPERTS, 1), upper)


def _positions_kernel(counts_ref, lower_ref, eidx_ref, rank_ref, pos_ref):
    tm = eidx_ref.shape[1]
    counts = jnp.broadcast_to(counts_ref[...].astype(F32), (N_EXPERTS, LANES))
    offsets = _dot_exact_lhs(lower_ref[...], counts)[:, 0:1]
    iota_e = lax.broadcasted_iota(I32, (N_EXPERTS, tm), 0).astype(F32)
    e = eidx_ref[...].astype(F32)
    rows = [jnp.sum(jnp.where(iota_e == e[k:k + 1], offsets, 0.0), axis=0, keepdims=True) for k in range(TOP_K)]
    pos_ref[0] = jnp.concatenate(rows, axis=0).astype(I32) + rank_ref[...]


def _positions(counts, eidx, rank, tm):
    t = eidx.shape[1]
    lower = jnp.asarray(np.tril(np.ones((N_EXPERTS, N_EXPERTS), np.float32), -1), BF16)
    tok = pl.BlockSpec((TOP_K, tm), lambda i: (0, i))
    return pl.pallas_call(
        _positions_kernel,
        grid=(t // tm,),
        in_specs=[pl.BlockSpec((N_EXPERTS, 1), lambda i: (0, 0)), pl.BlockSpec((N_EXPERTS, N_EXPERTS), lambda i: (0, 0)),
                  tok, tok],
        out_specs=pl.BlockSpec((1, TOP_K, tm), lambda i: (i, 0, 0)),
        out_shape=jax.ShapeDtypeStruct((t // tm, TOP_K, tm), I32),
        compiler_params=_params(("arbitrary",)),
        name="positions",
    )(counts, lower, eidx, rank)


def _token_rows(i):
    return pl.ds(pl.multiple_of(i * TOKEN_ROWS, TOKEN_ROWS), TOKEN_ROWS)


def _dispatch_kernel(pos_ref, h_ref, xs_ref, sem):
    tm = h_ref.shape[0] // TOKEN_ROWS

    def row_copy(t, k):
        return pltpu.make_async_copy(h_ref.at[_token_rows(t)], xs_ref.at[_token_rows(pos_ref[0, k, t])], sem)

    def start(t, carry):
        for k in range(TOP_K):
            row_copy(t, k).start(priority=k % 2)
        return carry

    def wait(t, carry):
        for k in range(TOP_K):
            row_copy(t, k).wait()
        return carry

    lax.fori_loop(0, tm, start, 0)
    lax.fori_loop(0, tm, wait, 0)


def _dispatch(h2t, pos_tiles, tm):
    t = h2t.shape[0] // TOKEN_ROWS
    return pl.pallas_call(
        _dispatch_kernel,
        grid=(t // tm,),
        in_specs=[pl.BlockSpec((1, TOP_K, tm), lambda i: (i, 0, 0), memory_space=pltpu.SMEM),
                  pl.BlockSpec((tm * TOKEN_ROWS, LANES), lambda i: (i, 0))],
        out_specs=pl.BlockSpec(memory_space=pl.ANY),
        out_shape=jax.ShapeDtypeStruct((t * TOP_K * TOKEN_ROWS, LANES), h2t.dtype),
        scratch_shapes=[pltpu.SemaphoreType.DMA(())],
        compiler_params=_params(("arbitrary",)),
        name="dispatch",
    )(pos_tiles, h2t)


def _experts_kernel(tile_ref, exp_ref, lo_ref, hi_ref, nxt_ref, slot_ref, xs_ref, wg_hbm, wu_hbm, wd_hbm, ys_ref,
                    wg_buf, wu_buf, wd_buf, wg_bf, wu_bf, wd_bf, sems):
    w = pl.program_id(0)
    tmg = xs_ref.shape[0] // TOKEN_ROWS
    tile = tile_ref[w]
    lo, hi = lo_ref[w], hi_ref[w]
    expert = exp_ref[w]
    prev_w = jnp.maximum(w - 1, 0)
    new_tile = jnp.logical_or(w == 0, tile_ref[prev_w] != tile)
    new_expert = jnp.logical_or(w == 0, exp_ref[prev_w] != expert)
    slot = slot_ref[w]
    nxt = nxt_ref[w]

    def fetch(e, s):
        return (pltpu.make_async_copy(wg_hbm.at[e], wg_buf.at[s], sems.at[s, 0]),
                pltpu.make_async_copy(wu_hbm.at[e], wu_buf.at[s], sems.at[s, 1]),
                pltpu.make_async_copy(wd_hbm.at[e], wd_buf.at[s], sems.at[s, 2]))

    @pl.when(w == 0)
    def _():
        for cp in fetch(expert, slot):
            cp.start()

    @pl.when(new_expert)
    def _():
        for cp in fetch(expert, slot):
            cp.wait()

        @pl.when(nxt >= 0)
        def _():
            for cp in fetch(nxt, 1 - slot):
                cp.start()

        wg_bf[...] = wg_buf[slot].astype(BF16)
        wu_bf[...] = wu_buf[slot].astype(BF16)
        wd_bf[...] = wd_buf[slot].astype(BF16)

    @pl.when(new_tile)
    def _():
        ys_ref[...] = jnp.zeros_like(ys_ref)

    @pl.when(hi > lo)
    def _():
        x = _from_token_tiles(xs_ref, tmg).astype(BF16)
        g = _dot(x, wg_bf[...])
        u = _dot(x, wu_bf[...])
        y = _dot((_silu(g) * u).astype(BF16), wd_bf[...])
        row = tile * tmg + lax.broadcasted_iota(I32, (tmg, 1), 0)
        inside = jnp.logical_and(row >= lo, row < hi)
        _to_token_tiles(ys_ref, jnp.where(inside, y, _from_token_tiles(ys_ref, tmg)))


def _group_metadata(counts, n_rows, tmg):
    n_tiles = n_rows // tmg
    n_work = n_tiles + N_EXPERTS
    ends = jnp.cumsum(counts)
    starts = ends - counts
    first_tile = starts // tmg
    n_items = jnp.where(counts > 0, (ends - 1) // tmg - first_tile + 1, 0)
    item_ends = jnp.cumsum(n_items)
    w = jnp.arange(n_work, dtype=I32)
    valid = w < item_ends[-1]
    e = jnp.minimum(jnp.searchsorted(item_ends, w, side="right"), N_EXPERTS - 1).astype(I32)
    tile = first_tile[e] + (w - (item_ends[e] - n_items[e]))
    lo = jnp.maximum(starts[e], tile * tmg)
    hi = jnp.minimum(ends[e], (tile + 1) * tmg)
    ids = jnp.arange(N_EXPERTS, dtype=I32)
    used = counts > 0
    last_e = jnp.max(jnp.where(used, ids, 0))
    tile = jnp.where(valid, tile, n_tiles - 1).astype(I32)
    e = jnp.where(valid, e, last_e).astype(I32)
    lo = jnp.where(valid, lo, 0).astype(I32)
    hi = jnp.where(valid, hi, 0).astype(I32)
    next_used = lax.cummin(jnp.where(used, ids, N_EXPERTS), reverse=True)
    next_after = jnp.concatenate([next_used[1:], jnp.full((1,), N_EXPERTS, I32)])
    nxt = jnp.where(next_after[e] < N_EXPERTS, next_after[e], -1).astype(I32)
    slot = ((jnp.cumsum(used.astype(I32)) - 1)[e] % 2).astype(I32)
    return tile, e, lo, hi, nxt, slot


def _experts(xs, counts, w_gate, w_up, w_down, tmg=256):
    d = D_MODEL
    n = xs.shape[0] // TOKEN_ROWS
    tile, e, lo, hi, nxt, slot = _group_metadata(counts, n, tmg)
    n_work = tile.shape[0]
    rows = pl.BlockSpec((tmg * TOKEN_ROWS, LANES), lambda w, tile, *_: (tile[w], 0))
    hbm = pl.BlockSpec(memory_space=pl.ANY)
    grid_spec = pltpu.PrefetchScalarGridSpec(
        num_scalar_prefetch=6,
        grid=(n_work,),
        in_specs=[rows, hbm, hbm, hbm],
        out_specs=rows,
        scratch_shapes=[pltpu.VMEM((2, d, EXPERT_FF), F32), pltpu.VMEM((2, d, EXPERT_FF), F32),
                        pltpu.VMEM((2, EXPERT_FF, d), F32),
                        pltpu.VMEM((d, EXPERT_FF), BF16), pltpu.VMEM((d, EXPERT_FF), BF16),
                        pltpu.VMEM((EXPERT_FF, d), BF16), pltpu.SemaphoreType.DMA((2, 3))],
    )
    return pl.pallas_call(
        _experts_kernel,
        grid_spec=grid_spec,
        out_shape=jax.ShapeDtypeStruct((n * TOKEN_ROWS, LANES), F32),
        compiler_params=_params(("arbitrary",)),
        name="experts",
    )(tile, e, lo, hi, nxt, slot, xs, w_gate, w_up, w_down)


def _combine_kernel(pos_ref, gates_ref, h_ref, x1_ref, gatef_ref, wg_ref, wu_ref, wd_ref, ys_ref, out_ref, buf, sem):
    tm = h_ref.shape[0]

    def row_copy(t, k):
        return pltpu.make_async_copy(ys_ref.at[_token_rows(pos_ref[0, k, t])], buf.at[k, _token_rows(t)], sem)

    def start(t, carry):
        for k in range(TOP_K):
            row_copy(t, k).start(priority=k % 2)
        return carry

    def wait(t, carry):
        for k in range(TOP_K):
            row_copy(t, k).wait()
        return carry

    lax.fori_loop(0, tm, start, 0)
    hb = h_ref[...].astype(BF16)
    shared = _dot((_silu(_dot(hb, wg_ref[...])) * _dot(hb, wu_ref[...])).astype(BF16), wd_ref[...])
    lax.fori_loop(0, tm, wait, 0)
    gates = gates_ref[...]
    routed = _from_token_tiles(buf.at[0], tm) * gates[:, 0:1]
    for k in range(1, TOP_K):
        routed = routed + _from_token_tiles(buf.at[k], tm) * gates[:, k:k + 1]
    out_ref[...] = x1_ref[...] + gatef_ref[...] * (shared + routed)


def _combine(ys, pos_tiles, gates_t, h2, x1, gate_f, w_gate_s, w_up_s, w_down_s, b, s, tm):
    t, d = h2.shape
    tiles_per_seq = s // tm
    row = lambda n: pl.BlockSpec((tm, n), lambda i: (i, 0))
    full = lambda shp: pl.BlockSpec(shp, lambda i: (0,) * len(shp))
    return pl.pallas_call(
        _combine_kernel,
        grid=(t // tm,),
        in_specs=[pl.BlockSpec((1, TOP_K, tm), lambda i: (i, 0, 0), memory_space=pltpu.SMEM),
                  row(TOP_K), row(d), row(d),
                  pl.BlockSpec((None, 1, d), lambda i: (i // tiles_per_seq, 0, 0)),
                  full((d, EXPERT_FF)), full((d, EXPERT_FF)), full((EXPERT_FF, d)),
                  pl.BlockSpec(memory_space=pl.ANY)],
        out_specs=row(d),
        out_shape=jax.ShapeDtypeStruct((t, d), F32),
        scratch_shapes=[pltpu.VMEM((TOP_K, tm * TOKEN_ROWS, LANES), F32), pltpu.SemaphoreType.DMA(())],
        compiler_params=_params(("arbitrary",)),
        name="combine",
    )(pos_tiles, gates_t, h2, x1, gate_f.reshape(b, 1, d),
      w_gate_s.astype(BF16), w_up_s.astype(BF16), w_down_s.astype(BF16), ys)


def _moe_sublayer(x1, h2, h2t, gate_f, w_router, router_bias, w_gate, w_up, w_down, w_gate_s, w_up_s, w_down_s, b, s,
                  tm=256):
    t = b * s
    eidx, rank, gates, counts = _router(h2, w_router, router_bias)
    pos_tiles = _positions(counts, eidx, rank, tm)
    xs = _dispatch(h2t, pos_tiles, tm)
    ys = _experts(xs, counts[:, 0], w_gate, w_up, w_down)
    return _combine(ys, pos_tiles, gates.T, h2, x1, gate_f, w_gate_s, w_up_s, w_down_s, b, s, tm)


def kernel(x, c, w_ada, b_ada, norm_mix_gain, w_in, q_norm_gain, k_norm_gain, rel_bias_table, conv_w, conv_b, dt_bias,
           a_log, d_skip, ssm_norm_gain, w_out, norm_ffn_gain, w_router, router_bias, w_gate_experts, w_up_experts,
           w_down_experts, w_gate_shared, w_up_shared, w_down_shared):
    b, s, d = x.shape
    for layer in range(w_ada.shape[0]):
        mod = _adaln(c, w_ada[layer], b_ada[layer])
        x1, h2, h2t = _mixer_sublayer(x, mod, norm_mix_gain[layer], w_in[layer], q_norm_gain[layer], k_norm_gain[layer],
                                 rel_bias_table, conv_w[layer], conv_b[layer], dt_bias[layer], a_log[layer],
                                 d_skip[layer], ssm_norm_gain[layer], w_out[layer], norm_ffn_gain[layer])
        gate_f = mod[:, 5 * d:]
        out = _moe_sublayer(x1, h2, h2t, gate_f, w_router[layer], router_bias[layer], w_gate_experts[layer],
                            w_up_experts[layer], w_down_experts[layer], w_gate_shared[layer], w_up_shared[layer],
                            w_down_shared[layer], b, s)
        x = out.reshape(b, s, d)
    return x
```

```python
import functools
import math

import numpy as np
import jax
import jax.numpy as jnp
from jax import lax
from jax.experimental import pallas as pl
from jax.experimental.pallas import tpu as pltpu

F32 = jnp.float32
BF16 = jnp.bfloat16
I32 = jnp.int32

D_MODEL = 1024
ATTN_HEADS = 8
HEAD_DIM = 64
ATTN_WIDTH = ATTN_HEADS * HEAD_DIM
PATTERNS = ((128, 1), (512, 4), (2048, 16))
WIN_STEPS = 128
REL_BUCKETS = 32
REL_MAX_DISTANCE = 2048
SSM_HEADS = 24
SSM_HEAD_DIM = 64
SSM_WIDTH = SSM_HEADS * SSM_HEAD_DIM
SSM_GROUPS = 4
HEADS_PER_GROUP = SSM_HEADS // SSM_GROUPS
GROUP_WIDTH = SSM_WIDTH // SSM_GROUPS
SSM_STATE = 128
SSM_CONV = 4
SSM_CHUNK = 128
CONV_CH = SSM_WIDTH + 2 * SSM_GROUPS * SSM_STATE
N_EXPERTS = 256
TOP_K = 8
N_EXPERT_GROUPS = 8
EXPERTS_PER_GROUP = N_EXPERTS // N_EXPERT_GROUPS
TOPK_GROUPS = 4
EXPERT_FF = 256
ROUTED_SCALE = 2.5
NORM_EPS = 1e-6

LANES = 128
SUBLANES = 8
NEG_BIG = -1e30
VMEM_LIMIT = 56 * 1024 * 1024


def _params(sem, vmem=VMEM_LIMIT):
    return pltpu.CompilerParams(dimension_semantics=sem, vmem_limit_bytes=vmem)


def _sigmoid(x):
    return 1.0 / (1.0 + jnp.exp(-x))


def _silu(x):
    return x * _sigmoid(x)


def _split3(x):
    hi = x.astype(BF16)
    r = x - hi.astype(F32)
    mid = r.astype(BF16)
    lo = (r - mid.astype(F32)).astype(BF16)
    return hi, mid, lo


def _dot(a, b):
    return jnp.dot(a, b, preferred_element_type=F32)


def _dot_nt(a, b):
    return lax.dot_general(a, b, (((1,), (1,)), ((), ())), preferred_element_type=F32)


def _dot_exact_rhs(a, b_exact):
    hi, mid, lo = _split3(a)
    return _dot(hi, b_exact) + _dot(mid, b_exact) + _dot(lo, b_exact)


def _dot_exact_lhs(a_exact, b):
    hi, mid, lo = _split3(b)
    return _dot(a_exact, hi) + _dot(a_exact, mid) + _dot(a_exact, lo)


def _adaln_kernel(c_ref, w_ref, b_ref, o_ref):
    s = _silu(c_ref[...]).astype(BF16)
    o_ref[...] = _dot(s, w_ref[...].astype(BF16)) + b_ref[...]


def _adaln(c, w_ada, b_ada):
    b, d = c.shape
    n = w_ada.shape[1]
    rows = SUBLANES
    c_pad = jnp.zeros((rows, d), F32).at[:b].set(c)
    tn = 1024
    out = pl.pallas_call(
        _adaln_kernel,
        grid=(n // tn,),
        in_specs=[pl.BlockSpec((rows, d), lambda j: (0, 0)),
                  pl.BlockSpec((d, tn), lambda j: (0, j)),
                  pl.BlockSpec((1, tn), lambda j: (0, j))],
        out_specs=pl.BlockSpec((rows, tn), lambda j: (0, j)),
        out_shape=jax.ShapeDtypeStruct((rows, n), F32),
        compiler_params=_params(("arbitrary",)),
        name="adaln",
    )(c_pad, w_ada, b_ada.reshape(1, n))
    return out[:b]


def _inproj_kernel(x_ref, shift_ref, scale_ref, g_ref, wqkv_ref, wz_ref, wxbc_ref, wdt_ref,
                   qg_ref, kg_ref, hmean_ref, q_ref, k_ref, v_ref, z_ref, xbc_ref, dt_ref):
    x = x_ref[...]
    ms = jnp.mean(x * x, axis=-1, keepdims=True)
    h = x * lax.rsqrt(ms + NORM_EPS) * g_ref[...]
    h = h * (1.0 + scale_ref[...]) + shift_ref[...]
    hb = h.astype(BF16)

    hmean = hmean_ref[...]

    def head_norm(t, gain):
        ss = _dot_exact_rhs(t * t, hmean)
        return t * lax.rsqrt(ss + NORM_EPS) * gain

    q = _dot(hb, wqkv_ref[:, 0:ATTN_WIDTH])
    q_ref[...] = head_norm(q, qg_ref[...]) * (HEAD_DIM ** -0.5)
    k = _dot(hb, wqkv_ref[:, ATTN_WIDTH:2 * ATTN_WIDTH])
    k_ref[...] = head_norm(k, kg_ref[...])
    v_ref[...] = _dot(hb, wqkv_ref[:, 2 * ATTN_WIDTH:3 * ATTN_WIDTH])
    for c0 in range(0, SSM_WIDTH, 512):
        z_ref[:, c0:c0 + 512] = _dot(hb, wz_ref[:, c0:c0 + 512])
    for c0 in range(0, CONV_CH, 512):
        xbc_ref[:, c0:c0 + 512] = _dot(hb, wxbc_ref[:, c0:c0 + 512])
    dt_ref[...] = _dot(hb, wdt_ref[...])


def _in_proj(x, shift, scale, gain, w_in, q_gain, k_gain, tm=256):
    b, s, d = x.shape
    t = b * s
    tiles_per_seq = s // tm
    w = w_in.astype(BF16)
    o_z = 3 * ATTN_WIDTH
    o_x = o_z + SSM_WIDTH
    o_dt = o_x + CONV_CH
    w_qkv, w_z, w_xbc = w[:, :o_z], w[:, o_z:o_x], w[:, o_x:o_dt]
    w_dt = jnp.zeros((d, LANES), BF16).at[:, :SSM_HEADS].set(w[:, o_dt:])
    head_of = np.arange(ATTN_WIDTH) // HEAD_DIM
    hmean = jnp.asarray((head_of[:, None] == head_of[None, :]).astype(np.float32) / HEAD_DIM, BF16)
    full = lambda shp: pl.BlockSpec(shp, lambda i: (0,) * len(shp))
    row = lambda n: pl.BlockSpec((tm, n), lambda i: (i, 0))
    per_batch = pl.BlockSpec((None, 1, d), lambda i: (i // tiles_per_seq, 0, 0))
    outs = pl.pallas_call(
        _inproj_kernel,
        grid=(t // tm,),
        in_specs=[row(d), per_batch, per_batch, full((1, d)),
                  full((d, o_z)), full((d, SSM_WIDTH)), full((d, CONV_CH)), full((d, LANES)),
                  full((1, ATTN_WIDTH)), full((1, ATTN_WIDTH)), full((ATTN_WIDTH, ATTN_WIDTH))],
        out_specs=[row(ATTN_WIDTH), row(ATTN_WIDTH), row(ATTN_WIDTH), row(SSM_WIDTH), row(CONV_CH), row(LANES)],
        out_shape=[jax.ShapeDtypeStruct((t, n), F32)
                   for n in (ATTN_WIDTH, ATTN_WIDTH, ATTN_WIDTH, SSM_WIDTH, CONV_CH, LANES)],
        compiler_params=_params(("arbitrary",)),
        name="in_proj",
    )(x.reshape(t, d), shift.reshape(b, 1, d), scale.reshape(b, 1, d), gain.reshape(1, d),
      w_qkv, w_z, w_xbc, w_dt,
      jnp.tile(q_gain, ATTN_HEADS).reshape(1, ATTN_WIDTH), jnp.tile(k_gain, ATTN_HEADS).reshape(1, ATTN_WIDTH), hmean)
    return outs


def _t5_causal_buckets(distance):
    n = np.maximum(distance, 0)
    max_exact = REL_BUCKETS // 2
    large = max_exact + (np.log(np.maximum(n, 1) / max_exact) / math.log(REL_MAX_DISTANCE / max_exact)
                         * (REL_BUCKETS - max_exact)).astype(np.int64)
    large = np.minimum(large, REL_BUCKETS - 1)
    return np.where(n < max_exact, n, large).astype(np.int32)


def _window_bias(rel_bias_table, dilation):
    qi = np.arange(WIN_STEPS)[:, None]
    kj = np.arange(2 * WIN_STEPS)[None, :]
    dist = qi + WIN_STEPS - kj
    band = (dist >= 0) & (dist <= WIN_STEPS)
    onehot = (_t5_causal_buckets(dist * dilation).reshape(-1, 1) == np.arange(REL_BUCKETS)[None, :]).astype(np.float32)
    bias = jnp.dot(rel_bias_table.astype(F32).T, jnp.asarray(onehot).T, precision=lax.Precision.HIGHEST)
    bias = bias.reshape(ATTN_HEADS, WIN_STEPS, 2 * WIN_STEPS)
    return jnp.where(jnp.asarray(band)[None], bias, NEG_BIG)


ATTN_TOKENS = max(w for w, _ in PATTERNS)
ATTN_UNROLL = 4


def _attn_kernel(q_ref, kp_ref, kc_ref, vp_ref, vc_ref, bias_ref, out_ref, kw, vw, o_acc, l_acc):
    tb = ATTN_TOKENS
    first = pl.program_id(2) == 0
    kw[0:tb] = kp_ref[...]
    kw[tb:2 * tb] = kc_ref[...]
    vw[0:tb] = vp_ref[...]
    vw[tb:2 * tb] = vc_ref[...]
    lane = lax.broadcasted_iota(I32, (WIN_STEPS, LANES), 1)
    head0 = lane < HEAD_DIM
    col = lax.broadcasted_iota(I32, (WIN_STEPS, 2 * WIN_STEPS), 1)
    in_prev = col < WIN_STEPS

    for p, (_, d) in enumerate(PATTERNS):
        shift = d.bit_length() - 1
        n_blocks = tb // WIN_STEPS

        def rows(start, n, d=d):
            return pl.ds(start, n, stride=d) if d > 1 else pl.ds(start, n)

        def body(it, carry, p=p, d=d, shift=shift, rows=rows):
            for u in range(ATTN_UNROLL):
                idx = it * ATTN_UNROLL + u
                r = jnp.bitwise_and(idx, d - 1)
                j = jnp.right_shift(idx, shift)
                qs = j * (WIN_STEPS * d) + r
                q = q_ref[rows(qs, WIN_STEPS), :]
                k = kw[rows(tb + qs - WIN_STEPS * d, 2 * WIN_STEPS), :].astype(BF16)
                v = vw[rows(tb + qs - WIN_STEPS * d, 2 * WIN_STEPS), :].astype(BF16)
                no_prev = jnp.logical_and(in_prev, jnp.logical_and(first, j == 0))
                o_h, lse_h = [], []
                for h in range(2):
                    qh = jnp.where(head0 if h == 0 else jnp.logical_not(head0), q, 0.0).astype(BF16)
                    s = _dot_nt(qh, k) + bias_ref[p, h]
                    s = jnp.where(no_prev, NEG_BIG, s)
                    m = jnp.max(s, axis=-1, keepdims=True)
                    e = jnp.exp(s - m)
                    denom = jnp.sum(e, axis=-1, keepdims=True)
                    o_h.append(_dot(e.astype(BF16), v) / denom)
                    lse_h.append(m + jnp.log(denom))
                o_acc[p, rows(qs, WIN_STEPS), :] = jnp.where(head0, o_h[0], o_h[1])
                l_acc[p, rows(qs, WIN_STEPS), :] = jnp.where(head0, lse_h[0], lse_h[1])
            return carry

        lax.fori_loop(0, n_blocks // ATTN_UNROLL, body, 0)

    chunk = 256
    for c0 in range(0, tb, chunk):
        l1, l2, l3 = (l_acc[p, c0:c0 + chunk, :] for p in range(3))
        m = jnp.maximum(jnp.maximum(l1, l2), l3)
        e1, e2, e3 = jnp.exp(l1 - m), jnp.exp(l2 - m), jnp.exp(l3 - m)
        num = e1 * o_acc[0, c0:c0 + chunk, :] + e2 * o_acc[1, c0:c0 + chunk, :] + e3 * o_acc[2, c0:c0 + chunk, :]
        out_ref[c0:c0 + chunk, :] = num / (e1 + e2 + e3)


def _attention(q, k, v, bias):
    b, s, w = q.shape
    tb = ATTN_TOKENS
    pairs = ATTN_HEADS // 2
    cur = pl.BlockSpec((None, tb, LANES), lambda bi, hp, i: (bi, i, hp))
    prev = pl.BlockSpec((None, tb, LANES), lambda bi, hp, i: (bi, jnp.maximum(i - 1, 0), hp))
    return pl.pallas_call(
        _attn_kernel,
        grid=(b, pairs, s // tb),
        in_specs=[cur, prev, cur, prev, cur,
                  pl.BlockSpec((len(PATTERNS), 2, WIN_STEPS, 2 * WIN_STEPS), lambda bi, hp, i: (0, hp, 0, 0))],
        out_specs=cur,
        out_shape=jax.ShapeDtypeStruct((b, s, w), F32),
        scratch_shapes=[pltpu.VMEM((2 * tb, LANES), F32), pltpu.VMEM((2 * tb, LANES), F32),
                        pltpu.VMEM((len(PATTERNS), tb, LANES), F32), pltpu.VMEM((len(PATTERNS), tb, LANES), F32)],
        compiler_params=_params(("arbitrary",) * 3),
        name="attention",
    )(q, k, k, v, v, bias)


def _ssd_kernel(xbc_ref, halo_ref, z_ref, dtraw_ref, convw_ref, convb_ref, dtb_ref, alog_ref, dskip_ref, gain_ref,
                expand_ref, tril_ref, y_ref, state_ref):
    c = pl.program_id(1)

    @pl.when(c == 0)
    def _():
        state_ref[...] = jnp.zeros_like(state_ref)

    x = xbc_ref[...]
    halo = jnp.where(c == 0, 0.0, halo_ref[...])
    w = convw_ref[...]
    acc = x * w[SSM_CONV - 1:SSM_CONV, :] + convb_ref[...]
    row8 = lax.broadcasted_iota(I32, (SUBLANES, CONV_CH), 0)
    for shift in range(1, SSM_CONV):
        xs = pltpu.roll(x, shift, axis=0)
        hs = pltpu.roll(halo, shift, axis=0)
        head = jnp.where(row8 < shift, hs, xs[0:SUBLANES])
        xs = jnp.concatenate([head, xs[SUBLANES:]], axis=0)
        acc = acc + xs * w[SSM_CONV - 1 - shift:SSM_CONV - shift, :]
    act = _silu(acc)
    x_s = act[:, :SSM_WIDTH]
    bc0 = SSM_WIDTH
    cc0 = SSM_WIDTH + SSM_GROUPS * SSM_STATE

    t = dtraw_ref[...] + dtb_ref[...]
    dt = jnp.maximum(t, 0.0) + jnp.log(1.0 + jnp.exp(-jnp.abs(t)))
    a = dt * (-jnp.exp(alog_ref[...]))
    a_cs = _dot_exact_lhs(tril_ref[...], a)
    a_cs_t = a_cs.T
    a_last = a_cs[SSM_CHUNK - 1:SSM_CHUNK, :]
    expand = expand_ref[...]
    dt_e = _dot_exact_rhs(dt, expand)
    ea_e = _dot_exact_rhs(jnp.exp(a_cs), expand)
    dte_e = _dot_exact_rhs(jnp.exp(a_last - a_cs), expand)
    xdt = x_s * dt_e
    xw = (xdt * dte_e).astype(BF16)
    xdt_b = xdt.astype(BF16)

    li = lax.broadcasted_iota(I32, (SSM_CHUNK, SSM_CHUNK), 0)
    si = lax.broadcasted_iota(I32, (SSM_CHUNK, SSM_CHUNK), 1)
    causal = li >= si

    ys = []
    for g in range(SSM_GROUPS):
        gs = slice(g * GROUP_WIDTH, (g + 1) * GROUP_WIDTH)
        b_g = act[:, bc0 + g * SSM_STATE:bc0 + (g + 1) * SSM_STATE]
        c_g = act[:, cc0 + g * SSM_STATE:cc0 + (g + 1) * SSM_STATE].astype(BF16)
        cb = _dot_nt(c_g, b_g.astype(BF16))
        state = state_ref[g]
        y_off = _dot(c_g, state.astype(BF16)) * ea_e[:, gs]
        parts = []
        for j in range(HEADS_PER_GROUP):
            hh = g * HEADS_PER_GROUP + j
            seg = a_cs[:, hh:hh + 1] - a_cs_t[hh:hh + 1, :]
            decay = jnp.exp(jnp.where(causal, seg, NEG_BIG))
            m = (cb * decay).astype(BF16)
            parts.append(_dot(m, xdt_b[:, hh * SSM_HEAD_DIM:(hh + 1) * SSM_HEAD_DIM]))
        ys.append(jnp.concatenate(parts, axis=1) + y_off)
        state_ref[g] = state * ea_e[SSM_CHUNK - 1:SSM_CHUNK, gs] + _dot(b_g.T.astype(BF16), xw[:, gs])
    y = jnp.concatenate(ys, axis=1) + dskip_ref[...] * x_s
    y = y * _silu(z_ref[...])
    gain = gain_ref[...]
    for g in range(SSM_GROUPS):
        gs = slice(g * GROUP_WIDTH, (g + 1) * GROUP_WIDTH)
        yg = y[:, gs]
        ms = jnp.mean(yg * yg, axis=-1, keepdims=True)
        y_ref[:, gs] = yg * lax.rsqrt(ms + NORM_EPS) * gain[:, gs]


def _ssd(xbc, z, dt_raw, conv_w, conv_b, dt_bias, a_log, d_skip, norm_gain, b, s):
    t = b * s
    nc = s // SSM_CHUNK
    pad_heads = lambda v: jnp.zeros((1, LANES), F32).at[0, :SSM_HEADS].set(v)
    head_of_lane = np.arange(SSM_WIDTH) // SSM_HEAD_DIM
    expand = jnp.asarray((np.arange(LANES)[:, None] == head_of_lane[None, :]).astype(np.float32), BF16)
    tril = jnp.asarray(np.tril(np.ones((SSM_CHUNK, SSM_CHUNK), np.float32)), BF16)
    halo_blocks = SSM_CHUNK // SUBLANES
    chunk = lambda n: pl.BlockSpec((SSM_CHUNK, n), lambda bi, c: (bi * nc + c, 0))
    full = lambda shp: pl.BlockSpec(shp, lambda bi, c: (0,) * len(shp))
    halo = pl.BlockSpec((SUBLANES, CONV_CH), lambda bi, c: (jnp.maximum((bi * nc + c) * halo_blocks - 1, 0), 0))
    return pl.pallas_call(
        _ssd_kernel,
        grid=(b, nc),
        in_specs=[chunk(CONV_CH), halo, chunk(SSM_WIDTH), chunk(LANES),
                  full((SSM_CONV, CONV_CH)), full((1, CONV_CH)), full((1, LANES)), full((1, LANES)),
                  full((1, SSM_WIDTH)), full((1, SSM_WIDTH)), full((LANES, SSM_WIDTH)), full((SSM_CHUNK, SSM_CHUNK))],
        out_specs=chunk(SSM_WIDTH),
        out_shape=jax.ShapeDtypeStruct((t, SSM_WIDTH), F32),
        scratch_shapes=[pltpu.VMEM((SSM_GROUPS, SSM_STATE, GROUP_WIDTH), F32)],
        compiler_params=_params(("arbitrary", "arbitrary")),
        name="ssd",
    )(xbc, xbc, z, dt_raw, conv_w, conv_b.reshape(1, CONV_CH), pad_heads(dt_bias), pad_heads(a_log),
      jnp.repeat(d_skip, SSM_HEAD_DIM).reshape(1, SSM_WIDTH), norm_gain.reshape(1, SSM_WIDTH), expand, tril)


U32 = jnp.uint32
TOKEN_ROWS = D_MODEL // (2 * LANES)
HIGH_HALF = np.uint32(0xFFFF0000)


def _to_token_tiles(ref, x):
    n = x.shape[0]
    for c in range(TOKEN_ROWS):
        lo = lax.bitcast_convert_type(x[:, c * LANES:(c + 1) * LANES].astype(BF16).astype(F32), U32)
        hi = lax.bitcast_convert_type(x[:, (c + TOKEN_ROWS) * LANES:(c + TOKEN_ROWS + 1) * LANES]
                                      .astype(BF16).astype(F32), U32)
        ref[pl.ds(c, n, stride=TOKEN_ROWS), :] = jnp.bitwise_or(jnp.right_shift(lo, 16), jnp.bitwise_and(hi, HIGH_HALF))


def _from_token_tiles(ref, n, token0=0):
    lows, highs = [], []
    for c in range(TOKEN_ROWS):
        word = ref[pl.ds(token0 * TOKEN_ROWS + c, n, stride=TOKEN_ROWS), :]
        lows.append(lax.bitcast_convert_type(jnp.left_shift(word, 16), F32))
        highs.append(lax.bitcast_convert_type(jnp.bitwise_and(word, HIGH_HALF), F32))
    return jnp.concatenate(lows + highs, axis=1)


def _outproj_kernel(attn_ref, ssm_ref, x_ref, gate_ref, shift_ref, scale_ref, g_ref, wa_ref, ws_ref, x1_ref, h2_ref,
                    h2t_ref):
    mixed = _dot(attn_ref[...].astype(BF16), wa_ref[...]) + _dot(ssm_ref[...].astype(BF16), ws_ref[...])
    x1 = x_ref[...] + gate_ref[...] * mixed
    x1_ref[...] = x1
    ms = jnp.mean(x1 * x1, axis=-1, keepdims=True)
    h = x1 * lax.rsqrt(ms + NORM_EPS) * g_ref[...]
    h2 = h * (1.0 + scale_ref[...]) + shift_ref[...]
    h2_ref[...] = h2
    _to_token_tiles(h2t_ref, h2)


def _out_proj(attn, ssm, x, gate, shift, scale, gain, w_out, b, s, tm=256):
    t = b * s
    d = D_MODEL
    tiles_per_seq = s // tm
    w = w_out.astype(BF16)
    row = lambda n: pl.BlockSpec((tm, n), lambda i: (i, 0))
    full = lambda shp: pl.BlockSpec(shp, lambda i: (0,) * len(shp))
    per_batch = pl.BlockSpec((None, 1, d), lambda i: (i // tiles_per_seq, 0, 0))
    return pl.pallas_call(
        _outproj_kernel,
        grid=(t // tm,),
        in_specs=[row(ATTN_WIDTH), row(SSM_WIDTH), row(d), per_batch, per_batch, per_batch, full((1, d)),
                  full((ATTN_WIDTH, d)), full((SSM_WIDTH, d))],
        out_specs=[row(d), row(d), pl.BlockSpec((tm * TOKEN_ROWS, LANES), lambda i: (i, 0))],
        out_shape=[jax.ShapeDtypeStruct((t, d), F32)] * 2 + [jax.ShapeDtypeStruct((t * TOKEN_ROWS, LANES), U32)],
        compiler_params=_params(("arbitrary",)),
        name="out_proj",
    )(attn.reshape(t, ATTN_WIDTH), ssm, x.reshape(t, d),
      gate.reshape(b, 1, d), shift.reshape(b, 1, d), scale.reshape(b, 1, d), gain.reshape(1, d),
      w[:ATTN_WIDTH], w[ATTN_WIDTH:])


def _mixer_sublayer(x, mod, norm_mix_gain, w_in, q_norm_gain, k_norm_gain, rel_bias_table, conv_w, conv_b, dt_bias,
                    a_log, d_skip, ssm_norm_gain, w_out, norm_ffn_gain):
    b, s, d = x.shape
    shift_m, scale_m, gate_m, shift_f, scale_f, _ = jnp.split(mod, 6, axis=-1)
    q, k, v, z, xbc, dt_raw = _in_proj(x, shift_m, scale_m, norm_mix_gain, w_in, q_norm_gain, k_norm_gain)
    bias = jnp.stack([_window_bias(rel_bias_table, dilation) for _, dilation in PATTERNS])
    attn = _attention(q.reshape(b, s, ATTN_WIDTH), k.reshape(b, s, ATTN_WIDTH), v.reshape(b, s, ATTN_WIDTH), bias)
    ssm = _ssd(xbc, z, dt_raw, conv_w, conv_b, dt_bias, a_log, d_skip, ssm_norm_gain, b, s)
    return _out_proj(attn, ssm, x, gate_m, shift_f, scale_f, norm_ffn_gain, w_out, b, s)


def _first_argmax(v, iota, limit):
    m = jnp.max(v, axis=0, keepdims=True)
    idx = jnp.min(jnp.where(v == m, iota, limit), axis=0, keepdims=True)
    return m, idx


def _router_kernel(h_ref, wt_ref, bias_ref, upper_ref, eidx_ref, rank_ref, gate_ref, counts_ref, carry_ref):
    @pl.when(pl.program_id(0) == 0)
    def _():
        carry_ref[...] = jnp.zeros_like(carry_ref)

    tm = h_ref.shape[0]
    h = h_ref[...]
    wt = wt_ref[...]
    h_hi = h.astype(BF16)
    h_lo = (h - h_hi.astype(F32)).astype(BF16)
    w_hi = wt.astype(BF16)
    w_lo = (wt - w_hi.astype(F32)).astype(BF16)
    logits = _dot_nt(w_hi, h_hi) + _dot_nt(w_hi, h_lo) + _dot_nt(w_lo, h_hi)
    scores = _sigmoid(logits)
    choice = scores + bias_ref[...]
    neg_inf = -jnp.inf

    iota_g = lax.broadcasted_iota(I32, (EXPERTS_PER_GROUP, tm), 0).astype(F32)
    group_rows = []
    for g in range(N_EXPERT_GROUPS):
        v = choice[g * EXPERTS_PER_GROUP:(g + 1) * EXPERTS_PER_GROUP]
        m1, i1 = _first_argmax(v, iota_g, float(EXPERTS_PER_GROUP))
        m2 = jnp.max(jnp.where(iota_g == i1, neg_inf, v), axis=0, keepdims=True)
        group_rows.append(m1 + m2)
    group_scores = jnp.concatenate(group_rows, axis=0)

    iota_n = lax.broadcasted_iota(I32, (N_EXPERT_GROUPS, tm), 0).astype(F32)
    chosen = jnp.zeros((N_EXPERT_GROUPS, tm), F32)
    for _ in range(TOPK_GROUPS):
        _, gi = _first_argmax(group_scores, iota_n, float(N_EXPERT_GROUPS))
        hit = iota_n == gi
        chosen = jnp.where(hit, 1.0, chosen)
        group_scores = jnp.where(hit, neg_inf, group_scores)

    masked = jnp.concatenate(
        [jnp.where(chosen[g:g + 1] > 0.0, choice[g * EXPERTS_PER_GROUP:(g + 1) * EXPERTS_PER_GROUP], neg_inf)
         for g in range(N_EXPERT_GROUPS)], axis=0)

    iota_e = lax.broadcasted_iota(I32, (N_EXPERTS, tm), 0).astype(F32)
    picked, gates = [], []
    onehot = jnp.zeros((N_EXPERTS, tm), F32)
    for _ in range(TOP_K):
        _, ei = _first_argmax(masked, iota_e, float(N_EXPERTS))
        hit = iota_e == ei
        gates.append(jnp.sum(jnp.where(hit, scores, 0.0), axis=0, keepdims=True))
        masked = jnp.where(hit, neg_inf, masked)
        onehot = jnp.where(hit, 1.0, onehot)
        picked.append(ei)
    gate_sum = gates[0]
    for gk in gates[1:]:
        gate_sum = gate_sum + gk

    base = _dot(onehot.astype(BF16), upper_ref[...]) + carry_ref[...]
    ranks = [jnp.sum(jnp.where(iota_e == ei, base, 0.0), axis=0, keepdims=True) for ei in picked]
    carry_ref[...] = carry_ref[...] + jnp.sum(onehot, axis=1, keepdims=True)

    eidx_ref[...] = jnp.concatenate(picked, axis=0).astype(I32)
    rank_ref[...] = jnp.concatenate(ranks, axis=0).astype(I32)
    gate_ref[...] = jnp.concatenate([gk / gate_sum * ROUTED_SCALE for gk in gates], axis=0)
    counts_ref[...] = carry_ref[...].astype(I32)


def _router(h2, w_router, router_bias, tm=256):
    t, d = h2.shape
    upper = jnp.asarray(np.triu(np.ones((tm, tm), np.float32), 1), BF16)
    tok = pl.BlockSpec((TOP_K, tm), lambda i: (0, i))
    full = lambda shp: pl.BlockSpec(shp, lambda i: (0,) * len(shp))
    return pl.pallas_call(
        _router_kernel,
        grid=(t // tm,),
        in_specs=[pl.BlockSpec((tm, d), lambda i: (i, 0)), full((N_EXPERTS, d)), full((N_EXPERTS, 1)), full((tm, tm))],
        out_specs=[tok, tok, tok, full((N_EXPERTS, 1))],
        out_shape=[jax.ShapeDtypeStruct((TOP_K, t), I32), jax.ShapeDtypeStruct((TOP_K, t), I32),
                   jax.ShapeDtypeStruct((TOP_K, t), F32), jax.ShapeDtypeStruct((N_EXPERTS, 1), I32)],
        scratch_shapes=[pltpu.VMEM((N_EXPERTS, 1), F32)],
        compiler_params=_params(("arbitrary",)),
        name="router",
    )(h2, w_router.T, router_bias.reshape(N_EXPERTS, 1), upper)


def _positions_kernel(counts_ref, lower_ref, eidx_ref, rank_ref, pos_ref):
    tm = eidx_ref.shape[1]
    counts = jnp.broadcast_to(counts_ref[...].astype(F32), (N_EXPERTS, LANES))
    offsets = _dot_exact_lhs(lower_ref[...], counts)[:, 0:1]
    iota_e = lax.broadcasted_iota(I32, (N_EXPERTS, tm), 0).astype(F32)
    e = eidx_ref[...].astype(F32)
    rows = [jnp.sum(jnp.where(iota_e == e[k:k + 1], offsets, 0.0), axis=0, keepdims=True) for k in range(TOP_K)]
    pos_ref[0] = jnp.concatenate(rows, axis=0).astype(I32) + rank_ref[...]


def _positions(counts, eidx, rank, tm):
    t = eidx.shape[1]
    lower = jnp.asarray(np.tril(np.ones((N_EXPERTS, N_EXPERTS), np.float32), -1), BF16)
    tok = pl.BlockSpec((TOP_K, tm), lambda i: (0, i))
    return pl.pallas_call(
        _positions_kernel,
        grid=(t // tm,),
        in_specs=[pl.BlockSpec((N_EXPERTS, 1), lambda i: (0, 0)), pl.BlockSpec((N_EXPERTS, N_EXPERTS), lambda i: (0, 0)),
                  tok, tok],
        out_specs=pl.BlockSpec((1, TOP_K, tm), lambda i: (i, 0, 0)),
        out_shape=jax.ShapeDtypeStruct((t // tm, TOP_K, tm), I32),
        compiler_params=_params(("arbitrary",)),
        name="positions",
    )(counts, lower, eidx, rank)


def _token_rows(i):
    return pl.ds(pl.multiple_of(i * TOKEN_ROWS, TOKEN_ROWS), TOKEN_ROWS)


def _dispatch_kernel(pos_ref, h_ref, xs_ref, sem):
    tm = h_ref.shape[0] // TOKEN_ROWS

    def row_copy(t, k):
        return pltpu.make_async_copy(h_ref.at[_token_rows(t)], xs_ref.at[_token_rows(pos_ref[0, k, t])], sem)

    def start(t, carry):
        for k in range(TOP_K):
            row_copy(t, k).start(priority=k % 2)
        return carry

    def wait(t, carry):
        for k in range(TOP_K):
            row_copy(t, k).wait()
        return carry

    lax.fori_loop(0, tm, start, 0)
    lax.fori_loop(0, tm, wait, 0)


def _dispatch(h2t, pos_tiles, tm):
    t = h2t.shape[0] // TOKEN_ROWS
    return pl.pallas_call(
        _dispatch_kernel,
        grid=(t // tm,),
        in_specs=[pl.BlockSpec((1, TOP_K, tm), lambda i: (i, 0, 0), memory_space=pltpu.SMEM),
                  pl.BlockSpec((tm * TOKEN_ROWS, LANES), lambda i: (i, 0))],
        out_specs=pl.BlockSpec(memory_space=pl.ANY),
        out_shape=jax.ShapeDtypeStruct((t * TOP_K * TOKEN_ROWS, LANES), h2t.dtype),
        scratch_shapes=[pltpu.SemaphoreType.DMA(())],
        compiler_params=_params(("arbitrary",)),
        name="dispatch",
    )(pos_tiles, h2t)


EXPERT_SUB_ROWS = 128


def _experts_kernel(tile_ref, exp_ref, lo_ref, hi_ref, nxt_ref, slot_ref, last_ref, xs_ref, wg_hbm, wu_hbm, wd_hbm,
                    ys_ref, wg_buf, wu_buf, wd_buf, wg_bf, wu_bf, wd_bf, acc, sems):
    w = pl.program_id(0)
    tmg = xs_ref.shape[0] // TOKEN_ROWS
    tile = tile_ref[w]
    lo, hi = lo_ref[w], hi_ref[w]
    expert = exp_ref[w]
    prev_w = jnp.maximum(w - 1, 0)
    new_tile = jnp.logical_or(w == 0, tile_ref[prev_w] != tile)
    new_expert = jnp.logical_or(w == 0, exp_ref[prev_w] != expert)
    slot = slot_ref[w]
    nxt = nxt_ref[w]

    def fetch(e, s):
        return (pltpu.make_async_copy(wg_hbm.at[e], wg_buf.at[s], sems.at[s, 0]),
                pltpu.make_async_copy(wu_hbm.at[e], wu_buf.at[s], sems.at[s, 1]),
                pltpu.make_async_copy(wd_hbm.at[e], wd_buf.at[s], sems.at[s, 2]))

    @pl.when(w == 0)
    def _():
        for cp in fetch(expert, slot):
            cp.start()

    @pl.when(new_expert)
    def _():
        for cp in fetch(expert, slot):
            cp.wait()

        @pl.when(nxt >= 0)
        def _():
            for cp in fetch(nxt, 1 - slot):
                cp.start()

        wg_bf[...] = wg_buf[slot].astype(BF16)
        wu_bf[...] = wu_buf[slot].astype(BF16)
        wd_bf[...] = wd_buf[slot].astype(BF16)

    @pl.when(new_tile)
    def _():
        acc[...] = jnp.zeros_like(acc)

    for sb in range(tmg // EXPERT_SUB_ROWS):
        row0 = tile * tmg + sb * EXPERT_SUB_ROWS

        @pl.when(jnp.logical_and(lo < row0 + EXPERT_SUB_ROWS, hi > row0))
        def _(sb=sb, row0=row0):
            rows = slice(sb * EXPERT_SUB_ROWS, (sb + 1) * EXPERT_SUB_ROWS)
            x = _from_token_tiles(xs_ref, EXPERT_SUB_ROWS, sb * EXPERT_SUB_ROWS).astype(BF16)
            g = _dot(x, wg_bf[...])
            u = _dot(x, wu_bf[...])
            y = _dot((_silu(g) * u).astype(BF16), wd_bf[...])
            row = row0 + lax.broadcasted_iota(I32, (EXPERT_SUB_ROWS, 1), 0)
            inside = jnp.logical_and(row >= lo, row < hi)
            acc[rows, :] = jnp.where(inside, y, acc[rows, :])

    @pl.when(last_ref[w] == 1)
    def _():
        _to_token_tiles(ys_ref, acc[...])


def _group_metadata(counts, n_rows, tmg):
    n_tiles = n_rows // tmg
    n_work = n_tiles + N_EXPERTS
    ends = jnp.cumsum(counts)
    starts = ends - counts
    first_tile = starts // tmg
    n_items = jnp.where(counts > 0, (ends - 1) // tmg - first_tile + 1, 0)
    item_ends = jnp.cumsum(n_items)
    w = jnp.arange(n_work, dtype=I32)
    valid = w < item_ends[-1]
    e = jnp.minimum(jnp.searchsorted(item_ends, w, side="right"), N_EXPERTS - 1).astype(I32)
    tile = first_tile[e] + (w - (item_ends[e] - n_items[e]))
    lo = jnp.maximum(starts[e], tile * tmg)
    hi = jnp.minimum(ends[e], (tile + 1) * tmg)
    ids = jnp.arange(N_EXPERTS, dtype=I32)
    used = counts > 0
    last_e = jnp.max(jnp.where(used, ids, 0))
    tile = jnp.where(valid, tile, n_tiles - 1).astype(I32)
    e = jnp.where(valid, e, last_e).astype(I32)
    lo = jnp.where(valid, lo, 0).astype(I32)
    hi = jnp.where(valid, hi, 0).astype(I32)
    next_used = lax.cummin(jnp.where(used, ids, N_EXPERTS), reverse=True)
    next_after = jnp.concatenate([next_used[1:], jnp.full((1,), N_EXPERTS, I32)])
    nxt = jnp.where(next_after[e] < N_EXPERTS, next_after[e], -1).astype(I32)
    slot = ((jnp.cumsum(used.astype(I32)) - 1)[e] % 2).astype(I32)
    last = jnp.concatenate([tile[1:] != tile[:-1], jnp.ones((1,), bool)]).astype(I32)
    return tile, e, lo, hi, nxt, slot, last


def _experts(xs, counts, w_gate, w_up, w_down, tmg=512):
    d = D_MODEL
    n = xs.shape[0] // TOKEN_ROWS
    tile, e, lo, hi, nxt, slot, last = _group_metadata(counts, n, tmg)
    n_work = tile.shape[0]
    rows = pl.BlockSpec((tmg * TOKEN_ROWS, LANES), lambda w, tile, *_: (tile[w], 0))
    hbm = pl.BlockSpec(memory_space=pl.ANY)
    grid_spec = pltpu.PrefetchScalarGridSpec(
        num_scalar_prefetch=7,
        grid=(n_work,),
        in_specs=[rows, hbm, hbm, hbm],
        out_specs=rows,
        scratch_shapes=[pltpu.VMEM((2, d, EXPERT_FF), F32), pltpu.VMEM((2, d, EXPERT_FF), F32),
                        pltpu.VMEM((2, EXPERT_FF, d), F32),
                        pltpu.VMEM((d, EXPERT_FF), BF16), pltpu.VMEM((d, EXPERT_FF), BF16),
                        pltpu.VMEM((EXPERT_FF, d), BF16), pltpu.VMEM((tmg, d), F32),
                        pltpu.SemaphoreType.DMA((2, 3))],
    )
    return pl.pallas_call(
        _experts_kernel,
        grid_spec=grid_spec,
        out_shape=jax.ShapeDtypeStruct((n * TOKEN_ROWS, LANES), U32),
        compiler_params=_params(("arbitrary",)),
        name="experts",
    )(tile, e, lo, hi, nxt, slot, last, xs, w_gate, w_up, w_down)


def _combine_kernel(pos_ref, gates_ref, h_ref, x1_ref, gatef_ref, wg_ref, wu_ref, wd_ref, ys_ref, out_ref, buf, sem):
    tm = h_ref.shape[0]

    def row_copy(t, k):
        return pltpu.make_async_copy(ys_ref.at[_token_rows(pos_ref[0, k, t])], buf.at[k, _token_rows(t)], sem)

    def start(t, carry):
        for k in range(TOP_K):
            row_copy(t, k).start(priority=k % 2)
        return carry

    def wait(t, carry):
        for k in range(TOP_K):
            row_copy(t, k).wait()
        return carry

    lax.fori_loop(0, tm, start, 0)
    hb = h_ref[...].astype(BF16)
    shared = _dot((_silu(_dot(hb, wg_ref[...])) * _dot(hb, wu_ref[...])).astype(BF16), wd_ref[...])
    lax.fori_loop(0, tm, wait, 0)
    gates = gates_ref[...]
    routed = _from_token_tiles(buf.at[0], tm) * gates[:, 0:1]
    for k in range(1, TOP_K):
        routed = routed + _from_token_tiles(buf.at[k], tm) * gates[:, k:k + 1]
    out_ref[...] = x1_ref[...] + gatef_ref[...] * (shared + routed)


def _combine(ys, pos_tiles, gates_t, h2, x1, gate_f, w_gate_s, w_up_s, w_down_s, b, s, tm):
    t, d = h2.shape
    tiles_per_seq = s // tm
    row = lambda n: pl.BlockSpec((tm, n), lambda i: (i, 0))
    full = lambda shp: pl.BlockSpec(shp, lambda i: (0,) * len(shp))
    return pl.pallas_call(
        _combine_kernel,
        grid=(t // tm,),
        in_specs=[pl.BlockSpec((1, TOP_K, tm), lambda i: (i, 0, 0), memory_space=pltpu.SMEM),
                  row(TOP_K), row(d), row(d),
                  pl.BlockSpec((None, 1, d), lambda i: (i // tiles_per_seq, 0, 0)),
                  full((d, EXPERT_FF)), full((d, EXPERT_FF)), full((EXPERT_FF, d)),
                  pl.BlockSpec(memory_space=pl.ANY)],
        out_specs=row(d),
        out_shape=jax.ShapeDtypeStruct((t, d), F32),
        scratch_shapes=[pltpu.VMEM((TOP_K, tm * TOKEN_ROWS, LANES), U32), pltpu.SemaphoreType.DMA(())],
        compiler_params=_params(("arbitrary",)),
        name="combine",
    )(pos_tiles, gates_t, h2, x1, gate_f.reshape(b, 1, d),
      w_gate_s.astype(BF16), w_up_s.astype(BF16), w_down_s.astype(BF16), ys)


def _moe_sublayer(x1, h2, h2t, gate_f, w_router, router_bias, w_gate, w_up, w_down, w_gate_s, w_up_s, w_down_s, b, s,
                  tm=256):
    t = b * s
    eidx, rank, gates, counts = _router(h2, w_router, router_bias)
    pos_tiles = _positions(counts, eidx, rank, tm)
    xs = _dispatch(h2t, pos_tiles, tm)
    ys = _experts(xs, counts[:, 0], w_gate, w_up, w_down)
    return _combine(ys, pos_tiles, gates.T, h2, x1, gate_f, w_gate_s, w_up_s, w_down_s, b, s, tm)


def kernel(x, c, w_ada, b_ada, norm_mix_gain, w_in, q_norm_gain, k_norm_gain, rel_bias_table, conv_w, conv_b, dt_bias,
           a_log, d_skip, ssm_norm_gain, w_out, norm_ffn_gain, w_router, router_bias, w_gate_experts, w_up_experts,
           w_down_experts, w_gate_shared, w_up_shared, w_down_shared):
    b, s, d = x.shape
    for layer in range(w_ada.shape[0]):
        mod = _adaln(c, w_ada[layer], b_ada[layer])
        x1, h2, h2t = _mixer_sublayer(x, mod, norm_mix_gain[layer], w_in[layer], q_norm_gain[layer], k_norm_gain[layer],
                                 rel_bias_table, conv_w[layer], conv_b[layer], dt_bias[layer], a_log[layer],
                                 d_skip[layer], ssm_norm_gain[layer], w_out[layer], norm_ffn_gain[layer])
        gate_f = mod[:, 5 * d:]
        out = _moe_sublayer(x1, h2, h2t, gate_f, w_router[layer], router_bias[layer], w_gate_experts[layer],
                            w_up_experts[layer], w_down_experts[layer], w_gate_shared[layer], w_up_shared[layer],
                            w_down_shared[layer], b, s)
        x = out.reshape(b, s, d)
    return x
```

```python
import functools
import math

import numpy as np
import jax
import jax.numpy as jnp
from jax import lax
from jax.experimental import pallas as pl
from jax.experimental.pallas import tpu as pltpu

F32 = jnp.float32
BF16 = jnp.bfloat16
I32 = jnp.int32

D_MODEL = 1024
ATTN_HEADS = 8
HEAD_DIM = 64
ATTN_WIDTH = ATTN_HEADS * HEAD_DIM
PATTERNS = ((128, 1), (512, 4), (2048, 16))
WIN_STEPS = 128
REL_BUCKETS = 32
REL_MAX_DISTANCE = 2048
SSM_HEADS = 24
SSM_HEAD_DIM = 64
SSM_WIDTH = SSM_HEADS * SSM_HEAD_DIM
SSM_GROUPS = 4
HEADS_PER_GROUP = SSM_HEADS // SSM_GROUPS
GROUP_WIDTH = SSM_WIDTH // SSM_GROUPS
SSM_STATE = 128
SSM_CONV = 4
SSM_CHUNK = 128
CONV_CH = SSM_WIDTH + 2 * SSM_GROUPS * SSM_STATE
N_EXPERTS = 256
TOP_K = 8
N_EXPERT_GROUPS = 8
EXPERTS_PER_GROUP = N_EXPERTS // N_EXPERT_GROUPS
TOPK_GROUPS = 4
EXPERT_FF = 256
ROUTED_SCALE = 2.5
NORM_EPS = 1e-6

LANES = 128
SUBLANES = 8
NEG_BIG = -1e30
VMEM_LIMIT = 56 * 1024 * 1024


def _params(sem, vmem=VMEM_LIMIT):
    return pltpu.CompilerParams(dimension_semantics=sem, vmem_limit_bytes=vmem)


def _sigmoid(x):
    return 1.0 / (1.0 + jnp.exp(-x))


def _silu(x):
    return x * _sigmoid(x)


def _split3(x):
    hi = x.astype(BF16)
    r = x - hi.astype(F32)
    mid = r.astype(BF16)
    lo = (r - mid.astype(F32)).astype(BF16)
    return hi, mid, lo


def _dot(a, b):
    return jnp.dot(a, b, preferred_element_type=F32)


def _dot_nt(a, b):
    return lax.dot_general(a, b, (((1,), (1,)), ((), ())), preferred_element_type=F32)


def _dot_exact_rhs(a, b_exact):
    hi, mid, lo = _split3(a)
    return _dot(hi, b_exact) + _dot(mid, b_exact) + _dot(lo, b_exact)


def _dot_exact_lhs(a_exact, b):
    hi, mid, lo = _split3(b)
    return _dot(a_exact, hi) + _dot(a_exact, mid) + _dot(a_exact, lo)


def _adaln_kernel(c_ref, w_ref, b_ref, o_ref):
    s = _silu(c_ref[...]).astype(BF16)
    o_ref[...] = _dot(s, w_ref[...].astype(BF16)) + b_ref[...]


def _adaln(c, w_ada, b_ada):
    b, d = c.shape
    n = w_ada.shape[1]
    rows = SUBLANES
    c_pad = jnp.zeros((rows, d), F32).at[:b].set(c)
    tn = 1024
    out = pl.pallas_call(
        _adaln_kernel,
        grid=(n // tn,),
        in_specs=[pl.BlockSpec((rows, d), lambda j: (0, 0)),
                  pl.BlockSpec((d, tn), lambda j: (0, j)),
                  pl.BlockSpec((1, tn), lambda j: (0, j))],
        out_specs=pl.BlockSpec((rows, tn), lambda j: (0, j)),
        out_shape=jax.ShapeDtypeStruct((rows, n), F32),
        compiler_params=_params(("arbitrary",)),
        name="adaln",
    )(c_pad, w_ada, b_ada.reshape(1, n))
    return out[:b]


def _inproj_kernel(x_ref, shift_ref, scale_ref, g_ref, wqkv_ref, wz_ref, wxbc_ref, wdt_ref,
                   qg_ref, kg_ref, hmean_ref, q_ref, k_ref, v_ref, z_ref, xbc_ref, dt_ref):
    x = x_ref[...]
    ms = jnp.mean(x * x, axis=-1, keepdims=True)
    h = x * lax.rsqrt(ms + NORM_EPS) * g_ref[...]
    h = h * (1.0 + scale_ref[...]) + shift_ref[...]
    hb = h.astype(BF16)

    hmean = hmean_ref[...]

    def head_norm(t, gain):
        ss = _dot_exact_rhs(t * t, hmean)
        return t * lax.rsqrt(ss + NORM_EPS) * gain

    q = _dot(hb, wqkv_ref[:, 0:ATTN_WIDTH])
    q_ref[...] = head_norm(q, qg_ref[...]) * (HEAD_DIM ** -0.5)
    k = _dot(hb, wqkv_ref[:, ATTN_WIDTH:2 * ATTN_WIDTH])
    k_ref[...] = head_norm(k, kg_ref[...])
    v_ref[...] = _dot(hb, wqkv_ref[:, 2 * ATTN_WIDTH:3 * ATTN_WIDTH])
    for c0 in range(0, SSM_WIDTH, 512):
        z_ref[:, c0:c0 + 512] = _dot(hb, wz_ref[:, c0:c0 + 512])
    for c0 in range(0, CONV_CH, 512):
        xbc_ref[:, c0:c0 + 512] = _dot(hb, wxbc_ref[:, c0:c0 + 512])
    dt_ref[...] = _dot(hb, wdt_ref[...])


def _in_proj(x, shift, scale, gain, w_in, q_gain, k_gain, tm=256):
    b, s, d = x.shape
    t = b * s
    tiles_per_seq = s // tm
    w = w_in.astype(BF16)
    o_z = 3 * ATTN_WIDTH
    o_x = o_z + SSM_WIDTH
    o_dt = o_x + CONV_CH
    w_qkv, w_z, w_xbc = w[:, :o_z], w[:, o_z:o_x], w[:, o_x:o_dt]
    w_dt = jnp.zeros((d, LANES), BF16).at[:, :SSM_HEADS].set(w[:, o_dt:])
    head_of = np.arange(ATTN_WIDTH) // HEAD_DIM
    hmean = jnp.asarray((head_of[:, None] == head_of[None, :]).astype(np.float32) / HEAD_DIM, BF16)
    full = lambda shp: pl.BlockSpec(shp, lambda i: (0,) * len(shp))
    row = lambda n: pl.BlockSpec((tm, n), lambda i: (i, 0))
    per_batch = pl.BlockSpec((None, 1, d), lambda i: (i // tiles_per_seq, 0, 0))
    outs = pl.pallas_call(
        _inproj_kernel,
        grid=(t // tm,),
        in_specs=[row(d), per_batch, per_batch, full((1, d)),
                  full((d, o_z)), full((d, SSM_WIDTH)), full((d, CONV_CH)), full((d, LANES)),
                  full((1, ATTN_WIDTH)), full((1, ATTN_WIDTH)), full((ATTN_WIDTH, ATTN_WIDTH))],
        out_specs=[row(ATTN_WIDTH), row(ATTN_WIDTH), row(ATTN_WIDTH), row(SSM_WIDTH), row(CONV_CH), row(LANES)],
        out_shape=[jax.ShapeDtypeStruct((t, n), F32)
                   for n in (ATTN_WIDTH, ATTN_WIDTH, ATTN_WIDTH, SSM_WIDTH, CONV_CH, LANES)],
        compiler_params=_params(("arbitrary",)),
        name="in_proj",
    )(x.reshape(t, d), shift.reshape(b, 1, d), scale.reshape(b, 1, d), gain.reshape(1, d),
      w_qkv, w_z, w_xbc, w_dt,
      jnp.tile(q_gain, ATTN_HEADS).reshape(1, ATTN_WIDTH), jnp.tile(k_gain, ATTN_HEADS).reshape(1, ATTN_WIDTH), hmean)
    return outs


def _t5_causal_buckets(distance):
    n = np.maximum(distance, 0)
    max_exact = REL_BUCKETS // 2
    large = max_exact + (np.log(np.maximum(n, 1) / max_exact) / math.log(REL_MAX_DISTANCE / max_exact)
                         * (REL_BUCKETS - max_exact)).astype(np.int64)
    large = np.minimum(large, REL_BUCKETS - 1)
    return np.where(n < max_exact, n, large).astype(np.int32)


def _window_bias(rel_bias_table, dilation):
    qi = np.arange(WIN_STEPS)[:, None]
    kj = np.arange(2 * WIN_STEPS)[None, :]
    dist = qi + WIN_STEPS - kj
    band = (dist >= 0) & (dist <= WIN_STEPS)
    onehot = (_t5_causal_buckets(dist * dilation).reshape(-1, 1) == np.arange(REL_BUCKETS)[None, :]).astype(np.float32)
    bias = jnp.dot(rel_bias_table.astype(F32).T, jnp.asarray(onehot).T, precision=lax.Precision.HIGHEST)
    bias = bias.reshape(ATTN_HEADS, WIN_STEPS, 2 * WIN_STEPS)
    return jnp.where(jnp.asarray(band)[None], bias, NEG_BIG)


ATTN_TOKENS = max(w for w, _ in PATTERNS)
ATTN_UNROLL = 4


def _attn_kernel(q_ref, kp_ref, kc_ref, vp_ref, vc_ref, bias_ref, out_ref, kw, vw, o_acc, l_acc):
    tb = ATTN_TOKENS
    first = pl.program_id(2) == 0
    kw[0:tb] = kp_ref[...]
    kw[tb:2 * tb] = kc_ref[...]
    vw[0:tb] = vp_ref[...]
    vw[tb:2 * tb] = vc_ref[...]
    lane = lax.broadcasted_iota(I32, (WIN_STEPS, LANES), 1)
    head0 = lane < HEAD_DIM
    col = lax.broadcasted_iota(I32, (WIN_STEPS, 2 * WIN_STEPS), 1)
    in_prev = col < WIN_STEPS

    for p, (_, d) in enumerate(PATTERNS):
        shift = d.bit_length() - 1
        n_blocks = tb // WIN_STEPS

        def rows(start, n, d=d):
            return pl.ds(start, n, stride=d) if d > 1 else pl.ds(start, n)

        def body(it, carry, p=p, d=d, shift=shift, rows=rows):
            for u in range(ATTN_UNROLL):
                idx = it * ATTN_UNROLL + u
                r = jnp.bitwise_and(idx, d - 1)
                j = jnp.right_shift(idx, shift)
                qs = j * (WIN_STEPS * d) + r
                q = q_ref[rows(qs, WIN_STEPS), :]
                k = kw[rows(tb + qs - WIN_STEPS * d, 2 * WIN_STEPS), :].astype(BF16)
                v = vw[rows(tb + qs - WIN_STEPS * d, 2 * WIN_STEPS), :].astype(BF16)
                no_prev = jnp.logical_and(in_prev, jnp.logical_and(first, j == 0))
                o_h, lse_h = [], []
                for h in range(2):
                    qh = jnp.where(head0 if h == 0 else jnp.logical_not(head0), q, 0.0).astype(BF16)
                    s = _dot_nt(qh, k) + bias_ref[p, h]
                    s = jnp.where(no_prev, NEG_BIG, s)
                    m = jnp.max(s, axis=-1, keepdims=True)
                    e = jnp.exp(s - m)
                    denom = jnp.sum(e, axis=-1, keepdims=True)
                    o_h.append(_dot(e.astype(BF16), v) / denom)
                    lse_h.append(m + jnp.log(denom))
                o_acc[p, rows(qs, WIN_STEPS), :] = jnp.where(head0, o_h[0], o_h[1])
                l_acc[p, rows(qs, WIN_STEPS), :] = jnp.where(head0, lse_h[0], lse_h[1])
            return carry

        lax.fori_loop(0, n_blocks // ATTN_UNROLL, body, 0)

    chunk = 256
    for c0 in range(0, tb, chunk):
        l1, l2, l3 = (l_acc[p, c0:c0 + chunk, :] for p in range(3))
        m = jnp.maximum(jnp.maximum(l1, l2), l3)
        e1, e2, e3 = jnp.exp(l1 - m), jnp.exp(l2 - m), jnp.exp(l3 - m)
        num = e1 * o_acc[0, c0:c0 + chunk, :] + e2 * o_acc[1, c0:c0 + chunk, :] + e3 * o_acc[2, c0:c0 + chunk, :]
        out_ref[c0:c0 + chunk, :] = num / (e1 + e2 + e3)


def _attention(q, k, v, bias):
    b, s, w = q.shape
    tb = ATTN_TOKENS
    pairs = ATTN_HEADS // 2
    cur = pl.BlockSpec((None, tb, LANES), lambda bi, hp, i: (bi, i, hp))
    prev = pl.BlockSpec((None, tb, LANES), lambda bi, hp, i: (bi, jnp.maximum(i - 1, 0), hp))
    return pl.pallas_call(
        _attn_kernel,
        grid=(b, pairs, s // tb),
        in_specs=[cur, prev, cur, prev, cur,
                  pl.BlockSpec((len(PATTERNS), 2, WIN_STEPS, 2 * WIN_STEPS), lambda bi, hp, i: (0, hp, 0, 0))],
        out_specs=cur,
        out_shape=jax.ShapeDtypeStruct((b, s, w), F32),
        scratch_shapes=[pltpu.VMEM((2 * tb, LANES), F32), pltpu.VMEM((2 * tb, LANES), F32),
                        pltpu.VMEM((len(PATTERNS), tb, LANES), F32), pltpu.VMEM((len(PATTERNS), tb, LANES), F32)],
        compiler_params=_params(("arbitrary",) * 3),
        name="attention",
    )(q, k, k, v, v, bias)


def _ssd_kernel(xbc_ref, halo_ref, z_ref, dtraw_ref, convw_ref, convb_ref, dtb_ref, alog_ref, dskip_ref, gain_ref,
                expand_ref, tril_ref, y_ref, state_ref):
    c = pl.program_id(1)

    @pl.when(c == 0)
    def _():
        state_ref[...] = jnp.zeros_like(state_ref)

    x = xbc_ref[...]
    halo = jnp.where(c == 0, 0.0, halo_ref[...])
    w = convw_ref[...]
    acc = x * w[SSM_CONV - 1:SSM_CONV, :] + convb_ref[...]
    row8 = lax.broadcasted_iota(I32, (SUBLANES, CONV_CH), 0)
    for shift in range(1, SSM_CONV):
        xs = pltpu.roll(x, shift, axis=0)
        hs = pltpu.roll(halo, shift, axis=0)
        head = jnp.where(row8 < shift, hs, xs[0:SUBLANES])
        xs = jnp.concatenate([head, xs[SUBLANES:]], axis=0)
        acc = acc + xs * w[SSM_CONV - 1 - shift:SSM_CONV - shift, :]
    act = _silu(acc)
    x_s = act[:, :SSM_WIDTH]
    bc0 = SSM_WIDTH
    cc0 = SSM_WIDTH + SSM_GROUPS * SSM_STATE

    t = dtraw_ref[...] + dtb_ref[...]
    dt = jnp.maximum(t, 0.0) + jnp.log(1.0 + jnp.exp(-jnp.abs(t)))
    a = dt * (-jnp.exp(alog_ref[...]))
    a_cs = _dot_exact_lhs(tril_ref[...], a)
    a_cs_t = a_cs.T
    a_last = a_cs[SSM_CHUNK - 1:SSM_CHUNK, :]
    expand = expand_ref[...]
    dt_e = _dot_exact_rhs(dt, expand)
    ea_e = _dot_exact_rhs(jnp.exp(a_cs), expand)
    dte_e = _dot_exact_rhs(jnp.exp(a_last - a_cs), expand)
    xdt = x_s * dt_e
    xw = (xdt * dte_e).astype(BF16)
    xdt_b = xdt.astype(BF16)

    li = lax.broadcasted_iota(I32, (SSM_CHUNK, SSM_CHUNK), 0)
    si = lax.broadcasted_iota(I32, (SSM_CHUNK, SSM_CHUNK), 1)
    causal = li >= si

    ys = []
    for g in range(SSM_GROUPS):
        gs = slice(g * GROUP_WIDTH, (g + 1) * GROUP_WIDTH)
        b_g = act[:, bc0 + g * SSM_STATE:bc0 + (g + 1) * SSM_STATE]
        c_g = act[:, cc0 + g * SSM_STATE:cc0 + (g + 1) * SSM_STATE].astype(BF16)
        cb = _dot_nt(c_g, b_g.astype(BF16))
        state = state_ref[g]
        y_off = _dot(c_g, state.astype(BF16)) * ea_e[:, gs]
        parts = []
        for j in range(HEADS_PER_GROUP):
            hh = g * HEADS_PER_GROUP + j
            seg = a_cs[:, hh:hh + 1] - a_cs_t[hh:hh + 1, :]
            decay = jnp.exp(jnp.where(causal, seg, NEG_BIG))
            m = (cb * decay).astype(BF16)
            parts.append(_dot(m, xdt_b[:, hh * SSM_HEAD_DIM:(hh + 1) * SSM_HEAD_DIM]))
        ys.append(jnp.concatenate(parts, axis=1) + y_off)
        state_ref[g] = state * ea_e[SSM_CHUNK - 1:SSM_CHUNK, gs] + _dot(b_g.T.astype(BF16), xw[:, gs])
    y = jnp.concatenate(ys, axis=1) + dskip_ref[...] * x_s
    y = y * _silu(z_ref[...])
    gain = gain_ref[...]
    for g in range(SSM_GROUPS):
        gs = slice(g * GROUP_WIDTH, (g + 1) * GROUP_WIDTH)
        yg = y[:, gs]
        ms = jnp.mean(yg * yg, axis=-1, keepdims=True)
        y_ref[:, gs] = yg * lax.rsqrt(ms + NORM_EPS) * gain[:, gs]


def _ssd(xbc, z, dt_raw, conv_w, conv_b, dt_bias, a_log, d_skip, norm_gain, b, s):
    t = b * s
    nc = s // SSM_CHUNK
    pad_heads = lambda v: jnp.zeros((1, LANES), F32).at[0, :SSM_HEADS].set(v)
    head_of_lane = np.arange(SSM_WIDTH) // SSM_HEAD_DIM
    expand = jnp.asarray((np.arange(LANES)[:, None] == head_of_lane[None, :]).astype(np.float32), BF16)
    tril = jnp.asarray(np.tril(np.ones((SSM_CHUNK, SSM_CHUNK), np.float32)), BF16)
    halo_blocks = SSM_CHUNK // SUBLANES
    chunk = lambda n: pl.BlockSpec((SSM_CHUNK, n), lambda bi, c: (bi * nc + c, 0))
    full = lambda shp: pl.BlockSpec(shp, lambda bi, c: (0,) * len(shp))
    halo = pl.BlockSpec((SUBLANES, CONV_CH), lambda bi, c: (jnp.maximum((bi * nc + c) * halo_blocks - 1, 0), 0))
    return pl.pallas_call(
        _ssd_kernel,
        grid=(b, nc),
        in_specs=[chunk(CONV_CH), halo, chunk(SSM_WIDTH), chunk(LANES),
                  full((SSM_CONV, CONV_CH)), full((1, CONV_CH)), full((1, LANES)), full((1, LANES)),
                  full((1, SSM_WIDTH)), full((1, SSM_WIDTH)), full((LANES, SSM_WIDTH)), full((SSM_CHUNK, SSM_CHUNK))],
        out_specs=chunk(SSM_WIDTH),
        out_shape=jax.ShapeDtypeStruct((t, SSM_WIDTH), F32),
        scratch_shapes=[pltpu.VMEM((SSM_GROUPS, SSM_STATE, GROUP_WIDTH), F32)],
        compiler_params=_params(("arbitrary", "arbitrary")),
        name="ssd",
    )(xbc, xbc, z, dt_raw, conv_w, conv_b.reshape(1, CONV_CH), pad_heads(dt_bias), pad_heads(a_log),
      jnp.repeat(d_skip, SSM_HEAD_DIM).reshape(1, SSM_WIDTH), norm_gain.reshape(1, SSM_WIDTH), expand, tril)


U32 = jnp.uint32
TOKEN_ROWS = D_MODEL // (2 * LANES)
HIGH_HALF = np.uint32(0xFFFF0000)


def _to_token_tiles(ref, x):
    n = x.shape[0]
    for c in range(TOKEN_ROWS):
        lo = lax.bitcast_convert_type(x[:, c * LANES:(c + 1) * LANES].astype(BF16).astype(F32), U32)
        hi = lax.bitcast_convert_type(x[:, (c + TOKEN_ROWS) * LANES:(c + TOKEN_ROWS + 1) * LANES]
                                      .astype(BF16).astype(F32), U32)
        ref[pl.ds(c, n, stride=TOKEN_ROWS), :] = jnp.bitwise_or(jnp.right_shift(lo, 16), jnp.bitwise_and(hi, HIGH_HALF))


def _from_token_tiles(ref, n, token0=0):
    lows, highs = [], []
    for c in range(TOKEN_ROWS):
        word = ref[pl.ds(token0 * TOKEN_ROWS + c, n, stride=TOKEN_ROWS), :]
        lows.append(lax.bitcast_convert_type(jnp.left_shift(word, 16), F32))
        highs.append(lax.bitcast_convert_type(jnp.bitwise_and(word, HIGH_HALF), F32))
    return jnp.concatenate(lows + highs, axis=1)


def _outproj_kernel(attn_ref, ssm_ref, x_ref, gate_ref, shift_ref, scale_ref, g_ref, wa_ref, ws_ref, x1_ref, h2_ref,
                    h2t_ref):
    mixed = _dot(attn_ref[...].astype(BF16), wa_ref[...]) + _dot(ssm_ref[...].astype(BF16), ws_ref[...])
    x1 = x_ref[...] + gate_ref[...] * mixed
    x1_ref[...] = x1
    ms = jnp.mean(x1 * x1, axis=-1, keepdims=True)
    h = x1 * lax.rsqrt(ms + NORM_EPS) * g_ref[...]
    h2 = h * (1.0 + scale_ref[...]) + shift_ref[...]
    h2_ref[...] = h2
    _to_token_tiles(h2t_ref, h2)


def _out_proj(attn, ssm, x, gate, shift, scale, gain, w_out, b, s, tm=256):
    t = b * s
    d = D_MODEL
    tiles_per_seq = s // tm
    w = w_out.astype(BF16)
    row = lambda n: pl.BlockSpec((tm, n), lambda i: (i, 0))
    full = lambda shp: pl.BlockSpec(shp, lambda i: (0,) * len(shp))
    per_batch = pl.BlockSpec((None, 1, d), lambda i: (i // tiles_per_seq, 0, 0))
    return pl.pallas_call(
        _outproj_kernel,
        grid=(t // tm,),
        in_specs=[row(ATTN_WIDTH), row(SSM_WIDTH), row(d), per_batch, per_batch, per_batch, full((1, d)),
                  full((ATTN_WIDTH, d)), full((SSM_WIDTH, d))],
        out_specs=[row(d), row(d), pl.BlockSpec((tm * TOKEN_ROWS, LANES), lambda i: (i, 0))],
        out_shape=[jax.ShapeDtypeStruct((t, d), F32)] * 2 + [jax.ShapeDtypeStruct((t * TOKEN_ROWS, LANES), U32)],
        compiler_params=_params(("arbitrary",)),
        name="out_proj",
    )(attn.reshape(t, ATTN_WIDTH), ssm, x.reshape(t, d),
      gate.reshape(b, 1, d), shift.reshape(b, 1, d), scale.reshape(b, 1, d), gain.reshape(1, d),
      w[:ATTN_WIDTH], w[ATTN_WIDTH:])


def _mixer_sublayer(x, mod, norm_mix_gain, w_in, q_norm_gain, k_norm_gain, rel_bias_table, conv_w, conv_b, dt_bias,
                    a_log, d_skip, ssm_norm_gain, w_out, norm_ffn_gain):
    b, s, d = x.shape
    shift_m, scale_m, gate_m, shift_f, scale_f, _ = jnp.split(mod, 6, axis=-1)
    q, k, v, z, xbc, dt_raw = _in_proj(x, shift_m, scale_m, norm_mix_gain, w_in, q_norm_gain, k_norm_gain)
    bias = jnp.stack([_window_bias(rel_bias_table, dilation) for _, dilation in PATTERNS])
    attn = _attention(q.reshape(b, s, ATTN_WIDTH), k.reshape(b, s, ATTN_WIDTH), v.reshape(b, s, ATTN_WIDTH), bias)
    ssm = _ssd(xbc, z, dt_raw, conv_w, conv_b, dt_bias, a_log, d_skip, ssm_norm_gain, b, s)
    return _out_proj(attn, ssm, x, gate_m, shift_f, scale_f, norm_ffn_gain, w_out, b, s)


def _first_argmax(v, iota, limit):
    m = jnp.max(v, axis=0, keepdims=True)
    idx = jnp.min(jnp.where(v == m, iota, limit), axis=0, keepdims=True)
    return m, idx


def _router_kernel(h_ref, wt_ref, bias_ref, upper_ref, eidx_ref, rank_ref, gate_ref, counts_ref, carry_ref):
    @pl.when(pl.program_id(0) == 0)
    def _():
        carry_ref[...] = jnp.zeros_like(carry_ref)

    tm = h_ref.shape[0]
    h = h_ref[...]
    wt = wt_ref[...]
    h_hi = h.astype(BF16)
    h_lo = (h - h_hi.astype(F32)).astype(BF16)
    w_hi = wt.astype(BF16)
    w_lo = (wt - w_hi.astype(F32)).astype(BF16)
    logits = _dot_nt(w_hi, h_hi) + _dot_nt(w_hi, h_lo) + _dot_nt(w_lo, h_hi)
    scores = _sigmoid(logits)
    choice = scores + bias_ref[...]
    neg_inf = -jnp.inf

    iota_g = lax.broadcasted_iota(I32, (EXPERTS_PER_GROUP, tm), 0).astype(F32)
    group_rows = []
    for g in range(N_EXPERT_GROUPS):
        v = choice[g * EXPERTS_PER_GROUP:(g + 1) * EXPERTS_PER_GROUP]
        m1, i1 = _first_argmax(v, iota_g, float(EXPERTS_PER_GROUP))
        m2 = jnp.max(jnp.where(iota_g == i1, neg_inf, v), axis=0, keepdims=True)
        group_rows.append(m1 + m2)
    group_scores = jnp.concatenate(group_rows, axis=0)

    iota_n = lax.broadcasted_iota(I32, (N_EXPERT_GROUPS, tm), 0).astype(F32)
    chosen = jnp.zeros((N_EXPERT_GROUPS, tm), F32)
    for _ in range(TOPK_GROUPS):
        _, gi = _first_argmax(group_scores, iota_n, float(N_EXPERT_GROUPS))
        hit = iota_n == gi
        chosen = jnp.where(hit, 1.0, chosen)
        group_scores = jnp.where(hit, neg_inf, group_scores)

    masked = jnp.concatenate(
        [jnp.where(chosen[g:g + 1] > 0.0, choice[g * EXPERTS_PER_GROUP:(g + 1) * EXPERTS_PER_GROUP], neg_inf)
         for g in range(N_EXPERT_GROUPS)], axis=0)

    iota_e = lax.broadcasted_iota(I32, (N_EXPERTS, tm), 0).astype(F32)
    picked, gates = [], []
    onehot = jnp.zeros((N_EXPERTS, tm), F32)
    for _ in range(TOP_K):
        _, ei = _first_argmax(masked, iota_e, float(N_EXPERTS))
        hit = iota_e == ei
        gates.append(jnp.sum(jnp.where(hit, scores, 0.0), axis=0, keepdims=True))
        masked = jnp.where(hit, neg_inf, masked)
        onehot = jnp.where(hit, 1.0, onehot)
        picked.append(ei)
    gate_sum = gates[0]
    for gk in gates[1:]:
        gate_sum = gate_sum + gk

    base = _dot(onehot.astype(BF16), upper_ref[...]) + carry_ref[...]
    ranks = [jnp.sum(jnp.where(iota_e == ei, base, 0.0), axis=0, keepdims=True) for ei in picked]
    carry_ref[...] = carry_ref[...] + jnp.sum(onehot, axis=1, keepdims=True)

    eidx_ref[...] = jnp.concatenate(picked, axis=0).astype(I32)
    rank_ref[...] = jnp.concatenate(ranks, axis=0).astype(I32)
    gate_ref[...] = jnp.concatenate([gk / gate_sum * ROUTED_SCALE for gk in gates], axis=0)
    counts_ref[...] = carry_ref[...].astype(I32)


def _router(h2, w_router, router_bias, tm=256):
    t, d = h2.shape
    upper = jnp.asarray(np.triu(np.ones((tm, tm), np.float32), 1), BF16)
    tok = pl.BlockSpec((TOP_K, tm), lambda i: (0, i))
    full = lambda shp: pl.BlockSpec(shp, lambda i: (0,) * len(shp))
    return pl.pallas_call(
        _router_kernel,
        grid=(t // tm,),
        in_specs=[pl.BlockSpec((tm, d), lambda i: (i, 0)), full((N_EXPERTS, d)), full((N_EXPERTS, 1)), full((tm, tm))],
        out_specs=[tok, tok, tok, full((N_EXPERTS, 1))],
        out_shape=[jax.ShapeDtypeStruct((TOP_K, t), I32), jax.ShapeDtypeStruct((TOP_K, t), I32),
                   jax.ShapeDtypeStruct((TOP_K, t), F32), jax.ShapeDtypeStruct((N_EXPERTS, 1), I32)],
        scratch_shapes=[pltpu.VMEM((N_EXPERTS, 1), F32)],
        compiler_params=_params(("arbitrary",)),
        name="router",
    )(h2, w_router.T, router_bias.reshape(N_EXPERTS, 1), upper)


def _positions_kernel(counts_ref, lower_ref, eidx_ref, rank_ref, pos_ref):
    tm = eidx_ref.shape[1]
    counts = jnp.broadcast_to(counts_ref[...].astype(F32), (N_EXPERTS, LANES))
    offsets = _dot_exact_lhs(lower_ref[...], counts)[:, 0:1]
    iota_e = lax.broadcasted_iota(I32, (N_EXPERTS, tm), 0).astype(F32)
    e = eidx_ref[...].astype(F32)
    rows = [jnp.sum(jnp.where(iota_e == e[k:k + 1], offsets, 0.0), axis=0, keepdims=True) for k in range(TOP_K)]
    pos_ref[0] = jnp.concatenate(rows, axis=0).astype(I32) + rank_ref[...]


def _positions(counts, eidx, rank, tm):
    t = eidx.shape[1]
    lower = jnp.asarray(np.tril(np.ones((N_EXPERTS, N_EXPERTS), np.float32), -1), BF16)
    tok = pl.BlockSpec((TOP_K, tm), lambda i: (0, i))
    return pl.pallas_call(
        _positions_kernel,
        grid=(t // tm,),
        in_specs=[pl.BlockSpec((N_EXPERTS, 1), lambda i: (0, 0)), pl.BlockSpec((N_EXPERTS, N_EXPERTS), lambda i: (0, 0)),
                  tok, tok],
        out_specs=pl.BlockSpec((1, TOP_K, tm), lambda i: (i, 0, 0)),
        out_shape=jax.ShapeDtypeStruct((t // tm, TOP_K, tm), I32),
        compiler_params=_params(("arbitrary",)),
        name="positions",
    )(counts, lower, eidx, rank)


def _token_rows(i):
    return pl.ds(pl.multiple_of(i * TOKEN_ROWS, TOKEN_ROWS), TOKEN_ROWS)


def _dispatch_kernel(pos_ref, h_ref, xs_ref, sem):
    tm = h_ref.shape[0] // TOKEN_ROWS

    def row_copy(t, k):
        return pltpu.make_async_copy(h_ref.at[_token_rows(t)], xs_ref.at[_token_rows(pos_ref[0, k, t])], sem)

    def start(t, carry):
        for k in range(TOP_K):
            row_copy(t, k).start(priority=k % 2)
        return carry

    def wait(t, carry):
        for k in range(TOP_K):
            row_copy(t, k).wait()
        return carry

    lax.fori_loop(0, tm, start, 0)
    lax.fori_loop(0, tm, wait, 0)


def _dispatch(h2t, pos_tiles, tm):
    t = h2t.shape[0] // TOKEN_ROWS
    return pl.pallas_call(
        _dispatch_kernel,
        grid=(t // tm,),
        in_specs=[pl.BlockSpec((1, TOP_K, tm), lambda i: (i, 0, 0), memory_space=pltpu.SMEM),
                  pl.BlockSpec((tm * TOKEN_ROWS, LANES), lambda i: (i, 0))],
        out_specs=pl.BlockSpec(memory_space=pl.ANY),
        out_shape=jax.ShapeDtypeStruct(((t * TOP_K + EXPERT_BLOCK) * TOKEN_ROWS, LANES), h2t.dtype),
        scratch_shapes=[pltpu.SemaphoreType.DMA(())],
        compiler_params=_params(("arbitrary",)),
        name="dispatch",
    )(pos_tiles, h2t)


EXPERT_BLOCK = 128
TAIL_PIECES = tuple(1 << i for i in reversed(range(EXPERT_BLOCK.bit_length() - 1)))


def _experts_kernel(start_ref, count_ref, nxt_ref, slot_ref, first_ref, blk0_ref, full0_ref, ptail_ref, ltail_ref,
                    xs_hbm, wg_hbm, wu_hbm, wd_hbm, ys_hbm,
                    wg_buf, wu_buf, wd_buf, wg_bf, wu_bf, wd_bf, xbuf, ybuf, ytail, wsem, xsem, ysem, tsem):
    e = pl.program_id(0)
    last_step = e == pl.num_programs(0) - 1
    start, count = start_ref[e], count_ref[e]
    n_full = jnp.right_shift(count, EXPERT_BLOCK.bit_length() - 1)
    tail = jnp.bitwise_and(count, EXPERT_BLOCK - 1)
    n_blk = n_full + (tail > 0).astype(I32)
    blk0, full0 = blk0_ref[e], full0_ref[e]
    slot, nxt = slot_ref[e], nxt_ref[e]

    def token_rows(token, n):
        return pl.ds(pl.multiple_of(token * TOKEN_ROWS, TOKEN_ROWS), n * TOKEN_ROWS)

    def fetch(ex, s):
        return (pltpu.make_async_copy(wg_hbm.at[ex], wg_buf.at[s], wsem.at[s, 0]),
                pltpu.make_async_copy(wu_hbm.at[ex], wu_buf.at[s], wsem.at[s, 1]),
                pltpu.make_async_copy(wd_hbm.at[ex], wd_buf.at[s], wsem.at[s, 2]))

    def x_copy(token, s):
        return pltpu.make_async_copy(xs_hbm.at[token_rows(token, EXPERT_BLOCK)], xbuf.at[s], xsem.at[s])

    def y_copy(token, s):
        return pltpu.make_async_copy(ybuf.at[s], ys_hbm.at[token_rows(token, EXPERT_BLOCK)], ysem.at[s])

    def tail_copies(token, length):
        out = []
        for piece in TAIL_PIECES:
            bigger = (EXPERT_BLOCK - 1) & ~(2 * piece - 1)
            done = jnp.bitwise_and(length, bigger)
            cp = pltpu.make_async_copy(ytail.at[token_rows(done, piece)], ys_hbm.at[token_rows(token + done, piece)],
                                       tsem)
            out.append((jnp.bitwise_and(length, piece) != 0, cp))
        return out

    def block(s):
        x = _from_token_tiles(xbuf.at[s], EXPERT_BLOCK).astype(BF16)
        g = _dot(x, wg_bf[...])
        u = _dot(x, wu_bf[...])
        return _dot((_silu(g) * u).astype(BF16), wd_bf[...])

    @pl.when(count > 0)
    def _():
        @pl.when(first_ref[e] == 1)
        def _():
            x_copy(start, jnp.bitwise_and(blk0, 1)).start()
            for cp in fetch(e, slot):
                cp.start()

        for cp in fetch(e, slot):
            cp.wait()

        @pl.when(nxt >= 0)
        def _():
            for cp in fetch(nxt, 1 - slot):
                cp.start()

        wg_bf[...] = wg_buf[slot].astype(BF16)
        wu_bf[...] = wu_buf[slot].astype(BF16)
        wd_bf[...] = wd_buf[slot].astype(BF16)

        def step(i):
            xs_slot = jnp.bitwise_and(blk0 + i, 1)
            x_copy(start, xs_slot).wait()

            @pl.when(i + 1 < n_blk)
            def _():
                x_copy(start + (i + 1) * EXPERT_BLOCK, 1 - xs_slot).start()

            @pl.when(jnp.logical_and(i + 1 >= n_blk, nxt >= 0))
            def _():
                x_copy(start_ref[jnp.maximum(nxt, 0)], 1 - xs_slot).start()

            return block(xs_slot)

        def full_block(i, carry):
            y = step(i)
            j = full0 + i
            ys_slot = jnp.bitwise_and(j, 1)

            @pl.when(j >= 2)
            def _():
                y_copy(start, ys_slot).wait()

            _to_token_tiles(ybuf.at[ys_slot], y)
            y_copy(start + i * EXPERT_BLOCK, ys_slot).start()
            return carry

        lax.fori_loop(0, n_full, full_block, 0)

        @pl.when(tail > 0)
        def _():
            y = step(n_full)
            for pred, cp in tail_copies(start, ptail_ref[e]):
                @pl.when(pred)
                def _(cp=cp):
                    cp.wait()
            _to_token_tiles(ytail, y)
            for pred, cp in tail_copies(start + n_full * EXPERT_BLOCK, tail):
                @pl.when(pred)
                def _(cp=cp):
                    cp.start()

    @pl.when(last_step)
    def _():
        total_full = full0 + n_full
        for back in (1, 2):
            @pl.when(total_full >= back)
            def _(back=back):
                y_copy(0, jnp.bitwise_and(total_full - back, 1)).wait()
        for pred, cp in tail_copies(0, ltail_ref[0]):
            @pl.when(pred)
            def _(cp=cp):
                cp.wait()


def _expert_metadata(counts):
    ids = jnp.arange(N_EXPERTS, dtype=I32)
    used = counts > 0
    starts = jnp.cumsum(counts) - counts
    n_blk = (counts + EXPERT_BLOCK - 1) // EXPERT_BLOCK
    n_full = counts // EXPERT_BLOCK
    tail = counts % EXPERT_BLOCK
    blk0 = jnp.cumsum(n_blk) - n_blk
    full0 = jnp.cumsum(n_full) - n_full
    next_used = lax.cummin(jnp.where(used, ids, N_EXPERTS), reverse=True)
    next_after = jnp.concatenate([next_used[1:], jnp.full((1,), N_EXPERTS, I32)])
    nxt = jnp.where(next_after < N_EXPERTS, next_after, -1)
    ordinal = jnp.cumsum(used.astype(I32)) - 1
    slot = ordinal % 2
    first = jnp.logical_and(used, ordinal == 0)
    latest = lax.cummax(jnp.where(tail > 0, ids, -1))
    before = jnp.concatenate([jnp.full((1,), -1, I32), latest[:-1]])
    ptail = jnp.where(before >= 0, tail[jnp.maximum(before, 0)], 0)
    ltail = jnp.where(latest[-1] >= 0, tail[jnp.maximum(latest[-1], 0)], 0).reshape(1)
    return tuple(v.astype(I32) for v in (starts, counts, nxt, slot, first, blk0, full0, ptail, ltail))


def _experts(xs, counts, w_gate, w_up, w_down):
    d = D_MODEL
    meta = _expert_metadata(counts)
    hbm = pl.BlockSpec(memory_space=pl.ANY)
    blk = (EXPERT_BLOCK * TOKEN_ROWS, LANES)
    grid_spec = pltpu.PrefetchScalarGridSpec(
        num_scalar_prefetch=len(meta),
        grid=(N_EXPERTS,),
        in_specs=[hbm, hbm, hbm, hbm],
        out_specs=hbm,
        scratch_shapes=[pltpu.VMEM((2, d, EXPERT_FF), F32), pltpu.VMEM((2, d, EXPERT_FF), F32),
                        pltpu.VMEM((2, EXPERT_FF, d), F32),
                        pltpu.VMEM((d, EXPERT_FF), BF16), pltpu.VMEM((d, EXPERT_FF), BF16),
                        pltpu.VMEM((EXPERT_FF, d), BF16),
                        pltpu.VMEM((2,) + blk, U32), pltpu.VMEM((2,) + blk, U32), pltpu.VMEM(blk, U32),
                        pltpu.SemaphoreType.DMA((2, 3)), pltpu.SemaphoreType.DMA((2,)),
                        pltpu.SemaphoreType.DMA((2,)), pltpu.SemaphoreType.DMA(())],
    )
    return pl.pallas_call(
        _experts_kernel,
        grid_spec=grid_spec,
        out_shape=jax.ShapeDtypeStruct(xs.shape, U32),
        compiler_params=_params(("arbitrary",)),
        name="experts",
    )(*meta, xs, w_gate, w_up, w_down)


def _combine_kernel(pos_ref, gates_ref, h_ref, x1_ref, gatef_ref, wg_ref, wu_ref, wd_ref, ys_ref, out_ref, buf, sem):
    tm = h_ref.shape[0]

    def row_copy(t, k):
        return pltpu.make_async_copy(ys_ref.at[_token_rows(pos_ref[0, k, t])], buf.at[k, _token_rows(t)], sem)

    def start(t, carry):
        for k in range(TOP_K):
            row_copy(t, k).start(priority=k % 2)
        return carry

    def wait(t, carry):
        for k in range(TOP_K):
            row_copy(t, k).wait()
        return carry

    lax.fori_loop(0, tm, start, 0)
    hb = h_ref[...].astype(BF16)
    shared = _dot((_silu(_dot(hb, wg_ref[...])) * _dot(hb, wu_ref[...])).astype(BF16), wd_ref[...])
    lax.fori_loop(0, tm, wait, 0)
    gates = gates_ref[...]
    routed = _from_token_tiles(buf.at[0], tm) * gates[:, 0:1]
    for k in range(1, TOP_K):
        routed = routed + _from_token_tiles(buf.at[k], tm) * gates[:, k:k + 1]
    out_ref[...] = x1_ref[...] + gatef_ref[...] * (shared + routed)


def _combine(ys, pos_tiles, gates_t, h2, x1, gate_f, w_gate_s, w_up_s, w_down_s, b, s, tm):
    t, d = h2.shape
    tiles_per_seq = s // tm
    row = lambda n: pl.BlockSpec((tm, n), lambda i: (i, 0))
    full = lambda shp: pl.BlockSpec(shp, lambda i: (0,) * len(shp))
    return pl.pallas_call(
        _combine_kernel,
        grid=(t // tm,),
        in_specs=[pl.BlockSpec((1, TOP_K, tm), lambda i: (i, 0, 0), memory_space=pltpu.SMEM),
                  row(TOP_K), row(d), row(d),
                  pl.BlockSpec((None, 1, d), lambda i: (i // tiles_per_seq, 0, 0)),
                  full((d, EXPERT_FF)), full((d, EXPERT_FF)), full((EXPERT_FF, d)),
                  pl.BlockSpec(memory_space=pl.ANY)],
        out_specs=row(d),
        out_shape=jax.ShapeDtypeStruct((t, d), F32),
        scratch_shapes=[pltpu.VMEM((TOP_K, tm * TOKEN_ROWS, LANES), U32), pltpu.SemaphoreType.DMA(())],
        compiler_params=_params(("arbitrary",)),
        name="combine",
    )(pos_tiles, gates_t, h2, x1, gate_f.reshape(b, 1, d),
      w_gate_s.astype(BF16), w_up_s.astype(BF16), w_down_s.astype(BF16), ys)


def _moe_sublayer(x1, h2, h2t, gate_f, w_router, router_bias, w_gate, w_up, w_down, w_gate_s, w_up_s, w_down_s, b, s,
                  tm=256):
    t = b * s
    eidx, rank, gates, counts = _router(h2, w_router, router_bias)
    pos_tiles = _positions(counts, eidx, rank, tm)
    xs = _dispatch(h2t, pos_tiles, tm)
    ys = _experts(xs, counts[:, 0], w_gate, w_up, w_down)
    return _combine(ys, pos_tiles, gates.T, h2, x1, gate_f, w_gate_s, w_up_s, w_down_s, b, s, tm)


def kernel(x, c, w_ada, b_ada, norm_mix_gain, w_in, q_norm_gain, k_norm_gain, rel_bias_table, conv_w, conv_b, dt_bias,
           a_log, d_skip, ssm_norm_gain, w_out, norm_ffn_gain, w_router, router_bias, w_gate_experts, w_up_experts,
           w_down_experts, w_gate_shared, w_up_shared, w_down_shared):
    b, s, d = x.shape
    for layer in range(w_ada.shape[0]):
        mod = _adaln(c, w_ada[layer], b_ada[layer])
        x1, h2, h2t = _mixer_sublayer(x, mod, norm_mix_gain[layer], w_in[layer], q_norm_gain[layer], k_norm_gain[layer],
                                 rel_bias_table, conv_w[layer], conv_b[layer], dt_bias[layer], a_log[layer],
                                 d_skip[layer], ssm_norm_gain[layer], w_out[layer], norm_ffn_gain[layer])
        gate_f = mod[:, 5 * d:]
        out = _moe_sublayer(x1, h2, h2t, gate_f, w_router[layer], router_bias[layer], w_gate_experts[layer],
                            w_up_experts[layer], w_down_experts[layer], w_gate_shared[layer], w_up_shared[layer],
                            w_down_shared[layer], b, s)
        x = out.reshape(b, s, d)
    return x
```

```python
import functools
import math

import numpy as np
import jax
import jax.numpy as jnp
from jax import lax
from jax.experimental import pallas as pl
from jax.experimental.pallas import tpu as pltpu

F32 = jnp.float32
BF16 = jnp.bfloat16
I32 = jnp.int32

D_MODEL = 1024
ATTN_HEADS = 8
HEAD_DIM = 64
ATTN_WIDTH = ATTN_HEADS * HEAD_DIM
PATTERNS = ((128, 1), (512, 4), (2048, 16))
WIN_STEPS = 128
REL_BUCKETS = 32
REL_MAX_DISTANCE = 2048
SSM_HEADS = 24
SSM_HEAD_DIM = 64
SSM_WIDTH = SSM_HEADS * SSM_HEAD_DIM
SSM_GROUPS = 4
HEADS_PER_GROUP = SSM_HEADS // SSM_GROUPS
GROUP_WIDTH = SSM_WIDTH // SSM_GROUPS
SSM_STATE = 128
SSM_CONV = 4
SSM_CHUNK = 128
CONV_CH = SSM_WIDTH + 2 * SSM_GROUPS * SSM_STATE
N_EXPERTS = 256
TOP_K = 8
N_EXPERT_GROUPS = 8
EXPERTS_PER_GROUP = N_EXPERTS // N_EXPERT_GROUPS
TOPK_GROUPS = 4
EXPERT_FF = 256
ROUTED_SCALE = 2.5
NORM_EPS = 1e-6

LANES = 128
SUBLANES = 8
NEG_BIG = -1e30
VMEM_LIMIT = 56 * 1024 * 1024


def _params(sem, vmem=VMEM_LIMIT):
    return pltpu.CompilerParams(dimension_semantics=sem, vmem_limit_bytes=vmem)


def _sigmoid(x):
    return 1.0 / (1.0 + jnp.exp(-x))


def _silu(x):
    return x * _sigmoid(x)


def _split3(x):
    hi = x.astype(BF16)
    r = x - hi.astype(F32)
    mid = r.astype(BF16)
    lo = (r - mid.astype(F32)).astype(BF16)
    return hi, mid, lo


def _dot(a, b):
    return jnp.dot(a, b, preferred_element_type=F32)


def _dot_nt(a, b):
    return lax.dot_general(a, b, (((1,), (1,)), ((), ())), preferred_element_type=F32)


def _dot_exact_rhs(a, b_exact):
    hi, mid, lo = _split3(a)
    return _dot(hi, b_exact) + _dot(mid, b_exact) + _dot(lo, b_exact)


def _dot_exact_lhs(a_exact, b):
    hi, mid, lo = _split3(b)
    return _dot(a_exact, hi) + _dot(a_exact, mid) + _dot(a_exact, lo)


def _adaln_kernel(c_ref, w_ref, b_ref, o_ref):
    s = _silu(c_ref[...]).astype(BF16)
    o_ref[...] = _dot(s, w_ref[...].astype(BF16)) + b_ref[...]


def _adaln(c, w_ada, b_ada):
    b, d = c.shape
    n = w_ada.shape[1]
    rows = SUBLANES
    c_pad = jnp.zeros((rows, d), F32).at[:b].set(c)
    tn = 1024
    out = pl.pallas_call(
        _adaln_kernel,
        grid=(n // tn,),
        in_specs=[pl.BlockSpec((rows, d), lambda j: (0, 0)),
                  pl.BlockSpec((d, tn), lambda j: (0, j)),
                  pl.BlockSpec((1, tn), lambda j: (0, j))],
        out_specs=pl.BlockSpec((rows, tn), lambda j: (0, j)),
        out_shape=jax.ShapeDtypeStruct((rows, n), F32),
        compiler_params=_params(("arbitrary",)),
        name="adaln",
    )(c_pad, w_ada, b_ada.reshape(1, n))
    return out[:b]


def _inproj_kernel(x_ref, shift_ref, scale_ref, g_ref, wqkv_ref, wz_ref, wxbc_ref, wdt_ref,
                   qg_ref, kg_ref, hmean_ref, q_ref, k_ref, v_ref, z_ref, xbc_ref, dt_ref):
    x = x_ref[...]
    ms = jnp.mean(x * x, axis=-1, keepdims=True)
    h = x * lax.rsqrt(ms + NORM_EPS) * g_ref[...]
    h = h * (1.0 + scale_ref[...]) + shift_ref[...]
    hb = h.astype(BF16)

    hmean = hmean_ref[...]

    def head_norm(t, gain):
        ss = _dot_exact_rhs(t * t, hmean)
        return t * lax.rsqrt(ss + NORM_EPS) * gain

    q = _dot(hb, wqkv_ref[:, 0:ATTN_WIDTH])
    q_ref[...] = head_norm(q, qg_ref[...]) * (HEAD_DIM ** -0.5)
    k = _dot(hb, wqkv_ref[:, ATTN_WIDTH:2 * ATTN_WIDTH])
    k_ref[...] = head_norm(k, kg_ref[...])
    v_ref[...] = _dot(hb, wqkv_ref[:, 2 * ATTN_WIDTH:3 * ATTN_WIDTH])
    for c0 in range(0, SSM_WIDTH, 512):
        z_ref[:, c0:c0 + 512] = _dot(hb, wz_ref[:, c0:c0 + 512])
    for c0 in range(0, CONV_CH, 512):
        xbc_ref[:, c0:c0 + 512] = _dot(hb, wxbc_ref[:, c0:c0 + 512])
    dt_ref[...] = _dot(hb, wdt_ref[...])


def _in_proj(x, shift, scale, gain, w_in, q_gain, k_gain, tm=256):
    b, s, d = x.shape
    t = b * s
    tiles_per_seq = s // tm
    w = w_in.astype(BF16)
    o_z = 3 * ATTN_WIDTH
    o_x = o_z + SSM_WIDTH
    o_dt = o_x + CONV_CH
    w_qkv, w_z, w_xbc = w[:, :o_z], w[:, o_z:o_x], w[:, o_x:o_dt]
    w_dt = jnp.zeros((d, LANES), BF16).at[:, :SSM_HEADS].set(w[:, o_dt:])
    head_of = np.arange(ATTN_WIDTH) // HEAD_DIM
    hmean = jnp.asarray((head_of[:, None] == head_of[None, :]).astype(np.float32) / HEAD_DIM, BF16)
    full = lambda shp: pl.BlockSpec(shp, lambda i: (0,) * len(shp))
    row = lambda n: pl.BlockSpec((tm, n), lambda i: (i, 0))
    per_batch = pl.BlockSpec((None, 1, d), lambda i: (i // tiles_per_seq, 0, 0))
    outs = pl.pallas_call(
        _inproj_kernel,
        grid=(t // tm,),
        in_specs=[row(d), per_batch, per_batch, full((1, d)),
                  full((d, o_z)), full((d, SSM_WIDTH)), full((d, CONV_CH)), full((d, LANES)),
                  full((1, ATTN_WIDTH)), full((1, ATTN_WIDTH)), full((ATTN_WIDTH, ATTN_WIDTH))],
        out_specs=[row(ATTN_WIDTH), row(ATTN_WIDTH), row(ATTN_WIDTH), row(SSM_WIDTH), row(CONV_CH), row(LANES)],
        out_shape=[jax.ShapeDtypeStruct((t, n), F32)
                   for n in (ATTN_WIDTH, ATTN_WIDTH, ATTN_WIDTH, SSM_WIDTH, CONV_CH, LANES)],
        compiler_params=_params(("arbitrary",)),
        name="in_proj",
    )(x.reshape(t, d), shift.reshape(b, 1, d), scale.reshape(b, 1, d), gain.reshape(1, d),
      w_qkv, w_z, w_xbc, w_dt,
      jnp.tile(q_gain, ATTN_HEADS).reshape(1, ATTN_WIDTH), jnp.tile(k_gain, ATTN_HEADS).reshape(1, ATTN_WIDTH), hmean)
    return outs


def _t5_causal_buckets(distance):
    n = np.maximum(distance, 0)
    max_exact = REL_BUCKETS // 2
    large = max_exact + (np.log(np.maximum(n, 1) / max_exact) / math.log(REL_MAX_DISTANCE / max_exact)
                         * (REL_BUCKETS - max_exact)).astype(np.int64)
    large = np.minimum(large, REL_BUCKETS - 1)
    return np.where(n < max_exact, n, large).astype(np.int32)


def _window_bias(rel_bias_table, dilation):
    qi = np.arange(WIN_STEPS)[:, None]
    kj = np.arange(2 * WIN_STEPS)[None, :]
    dist = qi + WIN_STEPS - kj
    band = (dist >= 0) & (dist <= WIN_STEPS)
    onehot = (_t5_causal_buckets(dist * dilation).reshape(-1, 1) == np.arange(REL_BUCKETS)[None, :]).astype(np.float32)
    bias = jnp.dot(rel_bias_table.astype(F32).T, jnp.asarray(onehot).T, precision=lax.Precision.HIGHEST)
    bias = bias.reshape(ATTN_HEADS, WIN_STEPS, 2 * WIN_STEPS)
    return jnp.where(jnp.asarray(band)[None], bias, NEG_BIG)


ATTN_TOKENS = max(w for w, _ in PATTERNS)
ATTN_UNROLL = 4


def _attn_kernel(q_ref, kp_ref, kc_ref, vp_ref, vc_ref, bias_ref, out_ref, kw, vw, o_acc, l_acc):
    tb = ATTN_TOKENS
    first = pl.program_id(2) == 0
    kw[0:tb] = kp_ref[...]
    kw[tb:2 * tb] = kc_ref[...]
    vw[0:tb] = vp_ref[...]
    vw[tb:2 * tb] = vc_ref[...]
    lane = lax.broadcasted_iota(I32, (WIN_STEPS, LANES), 1)
    head0 = lane < HEAD_DIM
    col = lax.broadcasted_iota(I32, (WIN_STEPS, 2 * WIN_STEPS), 1)
    in_prev = col < WIN_STEPS

    for p, (_, d) in enumerate(PATTERNS):
        shift = d.bit_length() - 1
        n_blocks = tb // WIN_STEPS

        def rows(start, n, d=d):
            return pl.ds(start, n, stride=d) if d > 1 else pl.ds(start, n)

        def body(it, carry, p=p, d=d, shift=shift, rows=rows):
            for u in range(ATTN_UNROLL):
                idx = it * ATTN_UNROLL + u
                r = jnp.bitwise_and(idx, d - 1)
                j = jnp.right_shift(idx, shift)
                qs = j * (WIN_STEPS * d) + r
                q = q_ref[rows(qs, WIN_STEPS), :]
                k = kw[rows(tb + qs - WIN_STEPS * d, 2 * WIN_STEPS), :].astype(BF16)
                v = vw[rows(tb + qs - WIN_STEPS * d, 2 * WIN_STEPS), :].astype(BF16)
                no_prev = jnp.logical_and(in_prev, jnp.logical_and(first, j == 0))
                o_h, lse_h = [], []
                for h in range(2):
                    qh = jnp.where(head0 if h == 0 else jnp.logical_not(head0), q, 0.0).astype(BF16)
                    s = _dot_nt(qh, k) + bias_ref[p, h]
                    s = jnp.where(no_prev, NEG_BIG, s)
                    m = jnp.max(s, axis=-1, keepdims=True)
                    e = jnp.exp(s - m)
                    denom = jnp.sum(e, axis=-1, keepdims=True)
                    o_h.append(_dot(e.astype(BF16), v) / denom)
                    lse_h.append(m + jnp.log(denom))
                o_acc[p, rows(qs, WIN_STEPS), :] = jnp.where(head0, o_h[0], o_h[1])
                l_acc[p, rows(qs, WIN_STEPS), :] = jnp.where(head0, lse_h[0], lse_h[1])
            return carry

        lax.fori_loop(0, n_blocks // ATTN_UNROLL, body, 0)

    chunk = 256
    for c0 in range(0, tb, chunk):
        l1, l2, l3 = (l_acc[p, c0:c0 + chunk, :] for p in range(3))
        m = jnp.maximum(jnp.maximum(l1, l2), l3)
        e1, e2, e3 = jnp.exp(l1 - m), jnp.exp(l2 - m), jnp.exp(l3 - m)
        num = e1 * o_acc[0, c0:c0 + chunk, :] + e2 * o_acc[1, c0:c0 + chunk, :] + e3 * o_acc[2, c0:c0 + chunk, :]
        out_ref[c0:c0 + chunk, :] = num / (e1 + e2 + e3)


def _attention(q, k, v, bias):
    b, s, w = q.shape
    tb = ATTN_TOKENS
    pairs = ATTN_HEADS // 2
    cur = pl.BlockSpec((None, tb, LANES), lambda bi, hp, i: (bi, i, hp))
    prev = pl.BlockSpec((None, tb, LANES), lambda bi, hp, i: (bi, jnp.maximum(i - 1, 0), hp))
    return pl.pallas_call(
        _attn_kernel,
        grid=(b, pairs, s // tb),
        in_specs=[cur, prev, cur, prev, cur,
                  pl.BlockSpec((len(PATTERNS), 2, WIN_STEPS, 2 * WIN_STEPS), lambda bi, hp, i: (0, hp, 0, 0))],
        out_specs=cur,
        out_shape=jax.ShapeDtypeStruct((b, s, w), F32),
        scratch_shapes=[pltpu.VMEM((2 * tb, LANES), F32), pltpu.VMEM((2 * tb, LANES), F32),
                        pltpu.VMEM((len(PATTERNS), tb, LANES), F32), pltpu.VMEM((len(PATTERNS), tb, LANES), F32)],
        compiler_params=_params(("arbitrary",) * 3),
        name="attention",
    )(q, k, k, v, v, bias)


def _ssd_kernel(xbc_ref, halo_ref, z_ref, dtraw_ref, convw_ref, convb_ref, dtb_ref, alog_ref, dskip_ref, gain_ref,
                expand_ref, tril_ref, y_ref, state_ref):
    c = pl.program_id(1)

    @pl.when(c == 0)
    def _():
        state_ref[...] = jnp.zeros_like(state_ref)

    x = xbc_ref[...]
    halo = jnp.where(c == 0, 0.0, halo_ref[...])
    w = convw_ref[...]
    acc = x * w[SSM_CONV - 1:SSM_CONV, :] + convb_ref[...]
    row8 = lax.broadcasted_iota(I32, (SUBLANES, CONV_CH), 0)
    for shift in range(1, SSM_CONV):
        xs = pltpu.roll(x, shift, axis=0)
        hs = pltpu.roll(halo, shift, axis=0)
        head = jnp.where(row8 < shift, hs, xs[0:SUBLANES])
        xs = jnp.concatenate([head, xs[SUBLANES:]], axis=0)
        acc = acc + xs * w[SSM_CONV - 1 - shift:SSM_CONV - shift, :]
    act = _silu(acc)
    x_s = act[:, :SSM_WIDTH]
    bc0 = SSM_WIDTH
    cc0 = SSM_WIDTH + SSM_GROUPS * SSM_STATE

    t = dtraw_ref[...] + dtb_ref[...]
    dt = jnp.maximum(t, 0.0) + jnp.log(1.0 + jnp.exp(-jnp.abs(t)))
    a = dt * (-jnp.exp(alog_ref[...]))
    a_cs = _dot_exact_lhs(tril_ref[...], a)
    a_cs_t = a_cs.T
    a_last = a_cs[SSM_CHUNK - 1:SSM_CHUNK, :]
    expand = expand_ref[...]
    dt_e = _dot_exact_rhs(dt, expand)
    ea_e = _dot_exact_rhs(jnp.exp(a_cs), expand)
    dte_e = _dot_exact_rhs(jnp.exp(a_last - a_cs), expand)
    xdt = x_s * dt_e
    xw = (xdt * dte_e).astype(BF16)
    xdt_b = xdt.astype(BF16)

    li = lax.broadcasted_iota(I32, (SSM_CHUNK, SSM_CHUNK), 0)
    si = lax.broadcasted_iota(I32, (SSM_CHUNK, SSM_CHUNK), 1)
    causal = li >= si

    ys = []
    for g in range(SSM_GROUPS):
        gs = slice(g * GROUP_WIDTH, (g + 1) * GROUP_WIDTH)
        b_g = act[:, bc0 + g * SSM_STATE:bc0 + (g + 1) * SSM_STATE]
        c_g = act[:, cc0 + g * SSM_STATE:cc0 + (g + 1) * SSM_STATE].astype(BF16)
        cb = _dot_nt(c_g, b_g.astype(BF16))
        state = state_ref[g]
        y_off = _dot(c_g, state.astype(BF16)) * ea_e[:, gs]
        parts = []
        for j in range(HEADS_PER_GROUP):
            hh = g * HEADS_PER_GROUP + j
            seg = a_cs[:, hh:hh + 1] - a_cs_t[hh:hh + 1, :]
            decay = jnp.exp(jnp.where(causal, seg, NEG_BIG))
            m = (cb * decay).astype(BF16)
            parts.append(_dot(m, xdt_b[:, hh * SSM_HEAD_DIM:(hh + 1) * SSM_HEAD_DIM]))
        ys.append(jnp.concatenate(parts, axis=1) + y_off)
        state_ref[g] = state * ea_e[SSM_CHUNK - 1:SSM_CHUNK, gs] + _dot(b_g.T.astype(BF16), xw[:, gs])
    y = jnp.concatenate(ys, axis=1) + dskip_ref[...] * x_s
    y = y * _silu(z_ref[...])
    gain = gain_ref[...]
    for g in range(SSM_GROUPS):
        gs = slice(g * GROUP_WIDTH, (g + 1) * GROUP_WIDTH)
        yg = y[:, gs]
        ms = jnp.mean(yg * yg, axis=-1, keepdims=True)
        y_ref[:, gs] = yg * lax.rsqrt(ms + NORM_EPS) * gain[:, gs]


def _ssd(xbc, z, dt_raw, conv_w, conv_b, dt_bias, a_log, d_skip, norm_gain, b, s):
    t = b * s
    nc = s // SSM_CHUNK
    pad_heads = lambda v: jnp.zeros((1, LANES), F32).at[0, :SSM_HEADS].set(v)
    head_of_lane = np.arange(SSM_WIDTH) // SSM_HEAD_DIM
    expand = jnp.asarray((np.arange(LANES)[:, None] == head_of_lane[None, :]).astype(np.float32), BF16)
    tril = jnp.asarray(np.tril(np.ones((SSM_CHUNK, SSM_CHUNK), np.float32)), BF16)
    halo_blocks = SSM_CHUNK // SUBLANES
    chunk = lambda n: pl.BlockSpec((SSM_CHUNK, n), lambda bi, c: (bi * nc + c, 0))
    full = lambda shp: pl.BlockSpec(shp, lambda bi, c: (0,) * len(shp))
    halo = pl.BlockSpec((SUBLANES, CONV_CH), lambda bi, c: (jnp.maximum((bi * nc + c) * halo_blocks - 1, 0), 0))
    return pl.pallas_call(
        _ssd_kernel,
        grid=(b, nc),
        in_specs=[chunk(CONV_CH), halo, chunk(SSM_WIDTH), chunk(LANES),
                  full((SSM_CONV, CONV_CH)), full((1, CONV_CH)), full((1, LANES)), full((1, LANES)),
                  full((1, SSM_WIDTH)), full((1, SSM_WIDTH)), full((LANES, SSM_WIDTH)), full((SSM_CHUNK, SSM_CHUNK))],
        out_specs=chunk(SSM_WIDTH),
        out_shape=jax.ShapeDtypeStruct((t, SSM_WIDTH), F32),
        scratch_shapes=[pltpu.VMEM((SSM_GROUPS, SSM_STATE, GROUP_WIDTH), F32)],
        compiler_params=_params(("arbitrary", "arbitrary")),
        name="ssd",
    )(xbc, xbc, z, dt_raw, conv_w, conv_b.reshape(1, CONV_CH), pad_heads(dt_bias), pad_heads(a_log),
      jnp.repeat(d_skip, SSM_HEAD_DIM).reshape(1, SSM_WIDTH), norm_gain.reshape(1, SSM_WIDTH), expand, tril)


U32 = jnp.uint32
TOKEN_ROWS = D_MODEL // (2 * LANES)
HIGH_HALF = np.uint32(0xFFFF0000)


def _to_token_tiles(ref, x):
    n = x.shape[0]
    for c in range(TOKEN_ROWS):
        lo = lax.bitcast_convert_type(x[:, c * LANES:(c + 1) * LANES].astype(BF16).astype(F32), U32)
        hi = lax.bitcast_convert_type(x[:, (c + TOKEN_ROWS) * LANES:(c + TOKEN_ROWS + 1) * LANES]
                                      .astype(BF16).astype(F32), U32)
        ref[pl.ds(c, n, stride=TOKEN_ROWS), :] = jnp.bitwise_or(jnp.right_shift(lo, 16), jnp.bitwise_and(hi, HIGH_HALF))


def _from_token_tiles(ref, n, token0=0):
    lows, highs = [], []
    for c in range(TOKEN_ROWS):
        word = ref[pl.ds(token0 * TOKEN_ROWS + c, n, stride=TOKEN_ROWS), :]
        lows.append(lax.bitcast_convert_type(jnp.left_shift(word, 16), F32))
        highs.append(lax.bitcast_convert_type(jnp.bitwise_and(word, HIGH_HALF), F32))
    return jnp.concatenate(lows + highs, axis=1)


def _outproj_kernel(attn_ref, ssm_ref, x_ref, gate_ref, shift_ref, scale_ref, g_ref, wa_ref, ws_ref, x1_ref, h2_ref,
                    h2t_ref):
    mixed = _dot(attn_ref[...].astype(BF16), wa_ref[...]) + _dot(ssm_ref[...].astype(BF16), ws_ref[...])
    x1 = x_ref[...] + gate_ref[...] * mixed
    x1_ref[...] = x1
    ms = jnp.mean(x1 * x1, axis=-1, keepdims=True)
    h = x1 * lax.rsqrt(ms + NORM_EPS) * g_ref[...]
    h2 = h * (1.0 + scale_ref[...]) + shift_ref[...]
    h2_ref[...] = h2
    _to_token_tiles(h2t_ref, h2)


def _out_proj(attn, ssm, x, gate, shift, scale, gain, w_out, b, s, tm=256):
    t = b * s
    d = D_MODEL
    tiles_per_seq = s // tm
    w = w_out.astype(BF16)
    row = lambda n: pl.BlockSpec((tm, n), lambda i: (i, 0))
    full = lambda shp: pl.BlockSpec(shp, lambda i: (0,) * len(shp))
    per_batch = pl.BlockSpec((None, 1, d), lambda i: (i // tiles_per_seq, 0, 0))
    return pl.pallas_call(
        _outproj_kernel,
        grid=(t // tm,),
        in_specs=[row(ATTN_WIDTH), row(SSM_WIDTH), row(d), per_batch, per_batch, per_batch, full((1, d)),
                  full((ATTN_WIDTH, d)), full((SSM_WIDTH, d))],
        out_specs=[row(d), row(d), pl.BlockSpec((tm * TOKEN_ROWS, LANES), lambda i: (i, 0))],
        out_shape=[jax.ShapeDtypeStruct((t, d), F32)] * 2 + [jax.ShapeDtypeStruct((t * TOKEN_ROWS, LANES), U32)],
        compiler_params=_params(("arbitrary",)),
        name="out_proj",
    )(attn.reshape(t, ATTN_WIDTH), ssm, x.reshape(t, d),
      gate.reshape(b, 1, d), shift.reshape(b, 1, d), scale.reshape(b, 1, d), gain.reshape(1, d),
      w[:ATTN_WIDTH], w[ATTN_WIDTH:])


def _mixer_sublayer(x, mod, norm_mix_gain, w_in, q_norm_gain, k_norm_gain, rel_bias_table, conv_w, conv_b, dt_bias,
                    a_log, d_skip, ssm_norm_gain, w_out, norm_ffn_gain):
    b, s, d = x.shape
    shift_m, scale_m, gate_m, shift_f, scale_f, _ = jnp.split(mod, 6, axis=-1)
    q, k, v, z, xbc, dt_raw = _in_proj(x, shift_m, scale_m, norm_mix_gain, w_in, q_norm_gain, k_norm_gain)
    bias = jnp.stack([_window_bias(rel_bias_table, dilation) for _, dilation in PATTERNS])
    attn = _attention(q.reshape(b, s, ATTN_WIDTH), k.reshape(b, s, ATTN_WIDTH), v.reshape(b, s, ATTN_WIDTH), bias)
    ssm = _ssd(xbc, z, dt_raw, conv_w, conv_b, dt_bias, a_log, d_skip, ssm_norm_gain, b, s)
    return _out_proj(attn, ssm, x, gate_m, shift_f, scale_f, norm_ffn_gain, w_out, b, s)


def _first_argmax(v, iota, limit):
    m = jnp.max(v, axis=0, keepdims=True)
    idx = jnp.min(jnp.where(v == m, iota, limit), axis=0, keepdims=True)
    return m, idx


def _router_kernel(h_ref, wt_ref, bias_ref, upper_ref, eidx_ref, rank_ref, gate_ref, counts_ref, carry_ref):
    @pl.when(pl.program_id(0) == 0)
    def _():
        carry_ref[...] = jnp.zeros_like(carry_ref)

    tm = h_ref.shape[0]
    h = h_ref[...]
    wt = wt_ref[...]
    h_hi = h.astype(BF16)
    h_lo = (h - h_hi.astype(F32)).astype(BF16)
    w_hi = wt.astype(BF16)
    w_lo = (wt - w_hi.astype(F32)).astype(BF16)
    logits = _dot_nt(w_hi, h_hi) + _dot_nt(w_hi, h_lo) + _dot_nt(w_lo, h_hi)
    scores = _sigmoid(logits)
    choice = scores + bias_ref[...]
    neg_inf = -jnp.inf

    iota_g = lax.broadcasted_iota(I32, (EXPERTS_PER_GROUP, tm), 0).astype(F32)
    group_rows = []
    for g in range(N_EXPERT_GROUPS):
        v = choice[g * EXPERTS_PER_GROUP:(g + 1) * EXPERTS_PER_GROUP]
        m1, i1 = _first_argmax(v, iota_g, float(EXPERTS_PER_GROUP))
        m2 = jnp.max(jnp.where(iota_g == i1, neg_inf, v), axis=0, keepdims=True)
        group_rows.append(m1 + m2)
    group_scores = jnp.concatenate(group_rows, axis=0)

    iota_n = lax.broadcasted_iota(I32, (N_EXPERT_GROUPS, tm), 0).astype(F32)
    chosen = jnp.zeros((N_EXPERT_GROUPS, tm), F32)
    for _ in range(TOPK_GROUPS):
        _, gi = _first_argmax(group_scores, iota_n, float(N_EXPERT_GROUPS))
        hit = iota_n == gi
        chosen = jnp.where(hit, 1.0, chosen)
        group_scores = jnp.where(hit, neg_inf, group_scores)

    masked = jnp.concatenate(
        [jnp.where(chosen[g:g + 1] > 0.0, choice[g * EXPERTS_PER_GROUP:(g + 1) * EXPERTS_PER_GROUP], neg_inf)
         for g in range(N_EXPERT_GROUPS)], axis=0)

    iota_e = lax.broadcasted_iota(I32, (N_EXPERTS, tm), 0).astype(F32)
    picked, gates = [], []
    onehot = jnp.zeros((N_EXPERTS, tm), F32)
    for _ in range(TOP_K):
        _, ei = _first_argmax(masked, iota_e, float(N_EXPERTS))
        hit = iota_e == ei
        gates.append(jnp.sum(jnp.where(hit, scores, 0.0), axis=0, keepdims=True))
        masked = jnp.where(hit, neg_inf, masked)
        onehot = jnp.where(hit, 1.0, onehot)
        picked.append(ei)
    gate_sum = gates[0]
    for gk in gates[1:]:
        gate_sum = gate_sum + gk

    base = _dot(onehot.astype(BF16), upper_ref[...]) + carry_ref[...]
    ranks = [jnp.sum(jnp.where(iota_e == ei, base, 0.0), axis=0, keepdims=True) for ei in picked]
    carry_ref[...] = carry_ref[...] + jnp.sum(onehot, axis=1, keepdims=True)

    eidx_ref[...] = jnp.concatenate(picked, axis=0).astype(I32)
    rank_ref[...] = jnp.concatenate(ranks, axis=0).astype(I32)
    gate_ref[...] = jnp.concatenate([gk / gate_sum * ROUTED_SCALE for gk in gates], axis=0)
    counts_ref[...] = carry_ref[...].astype(I32)


def _router(h2, w_router, router_bias, tm=256):
    t, d = h2.shape
    upper = jnp.asarray(np.triu(np.ones((tm, tm), np.float32), 1), BF16)
    tok = pl.BlockSpec((TOP_K, tm), lambda i: (0, i))
    full = lambda shp: pl.BlockSpec(shp, lambda i: (0,) * len(shp))
    return pl.pallas_call(
        _router_kernel,
        grid=(t // tm,),
        in_specs=[pl.BlockSpec((tm, d), lambda i: (i, 0)), full((N_EXPERTS, d)), full((N_EXPERTS, 1)), full((tm, tm))],
        out_specs=[tok, tok, tok, full((N_EXPERTS, 1))],
        out_shape=[jax.ShapeDtypeStruct((TOP_K, t), I32), jax.ShapeDtypeStruct((TOP_K, t), I32),
                   jax.ShapeDtypeStruct((TOP_K, t), F32), jax.ShapeDtypeStruct((N_EXPERTS, 1), I32)],
        scratch_shapes=[pltpu.VMEM((N_EXPERTS, 1), F32)],
        compiler_params=_params(("arbitrary",)),
        name="router",
    )(h2, w_router.T, router_bias.reshape(N_EXPERTS, 1), upper)


def _positions_kernel(counts_ref, lower_ref, eidx_ref, rank_ref, pos_ref):
    tm = eidx_ref.shape[1]
    counts = jnp.broadcast_to(counts_ref[...].astype(F32), (N_EXPERTS, LANES))
    offsets = _dot_exact_lhs(lower_ref[...], counts)[:, 0:1]
    iota_e = lax.broadcasted_iota(I32, (N_EXPERTS, tm), 0).astype(F32)
    e = eidx_ref[...].astype(F32)
    rows = [jnp.sum(jnp.where(iota_e == e[k:k + 1], offsets, 0.0), axis=0, keepdims=True) for k in range(TOP_K)]
    pos_ref[0] = jnp.concatenate(rows, axis=0).astype(I32) + rank_ref[...]


def _positions(counts, eidx, rank, tm):
    t = eidx.shape[1]
    lower = jnp.asarray(np.tril(np.ones((N_EXPERTS, N_EXPERTS), np.float32), -1), BF16)
    tok = pl.BlockSpec((TOP_K, tm), lambda i: (0, i))
    return pl.pallas_call(
        _positions_kernel,
        grid=(t // tm,),
        in_specs=[pl.BlockSpec((N_EXPERTS, 1), lambda i: (0, 0)), pl.BlockSpec((N_EXPERTS, N_EXPERTS), lambda i: (0, 0)),
                  tok, tok],
        out_specs=pl.BlockSpec((1, TOP_K, tm), lambda i: (i, 0, 0)),
        out_shape=jax.ShapeDtypeStruct((t // tm, TOP_K, tm), I32),
        compiler_params=_params(("arbitrary",)),
        name="positions",
    )(counts, lower, eidx, rank)


def _token_rows(i):
    return pl.ds(pl.multiple_of(i * TOKEN_ROWS, TOKEN_ROWS), TOKEN_ROWS)


def _dispatch_kernel(pos_ref, h_ref, xs_ref, sem):
    tm = h_ref.shape[0] // TOKEN_ROWS

    def row_copy(t, k):
        return pltpu.make_async_copy(h_ref.at[_token_rows(t)], xs_ref.at[_token_rows(pos_ref[0, k, t])], sem)

    def start(t, carry):
        for k in range(TOP_K):
            row_copy(t, k).start(priority=k % 2)
        return carry

    def wait(t, carry):
        for k in range(TOP_K):
            row_copy(t, k).wait()
        return carry

    lax.fori_loop(0, tm, start, 0)
    lax.fori_loop(0, tm, wait, 0)


def _dispatch(h2t, pos_tiles, tm):
    t = h2t.shape[0] // TOKEN_ROWS
    return pl.pallas_call(
        _dispatch_kernel,
        grid=(t // tm,),
        in_specs=[pl.BlockSpec((1, TOP_K, tm), lambda i: (i, 0, 0), memory_space=pltpu.SMEM),
                  pl.BlockSpec((tm * TOKEN_ROWS, LANES), lambda i: (i, 0))],
        out_specs=pl.BlockSpec(memory_space=pl.ANY),
        out_shape=jax.ShapeDtypeStruct(((t * TOP_K + EXPERT_BLOCK) * TOKEN_ROWS, LANES), h2t.dtype),
        scratch_shapes=[pltpu.SemaphoreType.DMA(())],
        compiler_params=_params(("arbitrary",)),
        name="dispatch",
    )(pos_tiles, h2t)


EXPERT_BLOCK = 128
TAIL_PIECES = tuple(1 << i for i in reversed(range(EXPERT_BLOCK.bit_length() - 1)))
X_SLOTS = 4
Y_SLOTS = 4


def _experts_kernel(start_ref, count_ref, nxt_ref, slot_ref, first_ref, blk0_ref, full0_ref, ptail_ref, ltail_ref,
                    blktok_ref, nblocks_ref, xs_hbm, wg_hbm, wu_hbm, wd_hbm, ys_hbm,
                    wg_buf, wu_buf, wd_buf, wg_bf, wu_bf, wd_bf, xbuf, ybuf, ytail, wsem, xsem, ysem, tsem):
    e = pl.program_id(0)
    last_step = e == pl.num_programs(0) - 1
    start, count = start_ref[e], count_ref[e]
    n_full = jnp.right_shift(count, EXPERT_BLOCK.bit_length() - 1)
    tail = jnp.bitwise_and(count, EXPERT_BLOCK - 1)
    n_blk = n_full + (tail > 0).astype(I32)
    blk0, full0 = blk0_ref[e], full0_ref[e]
    slot, nxt = slot_ref[e], nxt_ref[e]

    def token_rows(token, n):
        return pl.ds(pl.multiple_of(token * TOKEN_ROWS, TOKEN_ROWS), n * TOKEN_ROWS)

    def fetch(ex, s):
        return (pltpu.make_async_copy(wg_hbm.at[ex], wg_buf.at[s], wsem.at[s, 0]),
                pltpu.make_async_copy(wu_hbm.at[ex], wu_buf.at[s], wsem.at[s, 1]),
                pltpu.make_async_copy(wd_hbm.at[ex], wd_buf.at[s], wsem.at[s, 2]))

    def x_copy(token, s):
        return pltpu.make_async_copy(xs_hbm.at[token_rows(token, EXPERT_BLOCK)], xbuf.at[s], xsem.at[s])

    def y_copy(token, s):
        return pltpu.make_async_copy(ybuf.at[s], ys_hbm.at[token_rows(token, EXPERT_BLOCK)], ysem.at[s])

    def tail_copies(token, length):
        out = []
        for piece in TAIL_PIECES:
            bigger = (EXPERT_BLOCK - 1) & ~(2 * piece - 1)
            done = jnp.bitwise_and(length, bigger)
            cp = pltpu.make_async_copy(ytail.at[token_rows(done, piece)], ys_hbm.at[token_rows(token + done, piece)],
                                       tsem)
            out.append((jnp.bitwise_and(length, piece) != 0, cp))
        return out

    def block(s):
        x = _from_token_tiles(xbuf.at[s], EXPERT_BLOCK).astype(BF16)
        g = _dot(x, wg_bf[...])
        u = _dot(x, wu_bf[...])
        return _dot((_silu(g) * u).astype(BF16), wd_bf[...])

    @pl.when(count > 0)
    def _():
        @pl.when(first_ref[e] == 1)
        def _():
            for g in range(X_SLOTS - 1):
                @pl.when(g < nblocks_ref[0])
                def _(g=g):
                    x_copy(blktok_ref[g], g).start()
            for cp in fetch(e, slot):
                cp.start()

        for cp in fetch(e, slot):
            cp.wait()

        @pl.when(nxt >= 0)
        def _():
            for cp in fetch(nxt, 1 - slot):
                cp.start()

        wg_bf[...] = wg_buf[slot].astype(BF16)
        wu_bf[...] = wu_buf[slot].astype(BF16)
        wd_bf[...] = wd_buf[slot].astype(BF16)

        def step(i):
            g = blk0 + i
            xs_slot = jnp.bitwise_and(g, X_SLOTS - 1)
            x_copy(start, xs_slot).wait()
            ahead = g + (X_SLOTS - 1)

            @pl.when(ahead < nblocks_ref[0])
            def _():
                x_copy(blktok_ref[ahead], jnp.bitwise_and(ahead, X_SLOTS - 1)).start()

            return block(xs_slot)

        def full_block(i, carry):
            y = step(i)
            j = full0 + i
            ys_slot = jnp.bitwise_and(j, Y_SLOTS - 1)

            @pl.when(j >= Y_SLOTS)
            def _():
                y_copy(start, ys_slot).wait()

            _to_token_tiles(ybuf.at[ys_slot], y)
            y_copy(start + i * EXPERT_BLOCK, ys_slot).start()
            return carry

        lax.fori_loop(0, n_full, full_block, 0)

        @pl.when(tail > 0)
        def _():
            y = step(n_full)
            for pred, cp in tail_copies(start, ptail_ref[e]):
                @pl.when(pred)
                def _(cp=cp):
                    cp.wait()
            _to_token_tiles(ytail, y)
            for pred, cp in tail_copies(start + n_full * EXPERT_BLOCK, tail):
                @pl.when(pred)
                def _(cp=cp):
                    cp.start()

    @pl.when(last_step)
    def _():
        total_full = full0 + n_full
        for back in range(1, Y_SLOTS + 1):
            @pl.when(total_full >= back)
            def _(back=back):
                y_copy(0, jnp.bitwise_and(total_full - back, Y_SLOTS - 1)).wait()
        for pred, cp in tail_copies(0, ltail_ref[0]):
            @pl.when(pred)
            def _(cp=cp):
                cp.wait()


def _max_expert_blocks(n_rows):
    return n_rows // EXPERT_BLOCK + N_EXPERTS


def _expert_metadata(counts, n_rows):
    ids = jnp.arange(N_EXPERTS, dtype=I32)
    used = counts > 0
    starts = jnp.cumsum(counts) - counts
    n_blk = (counts + EXPERT_BLOCK - 1) // EXPERT_BLOCK
    n_full = counts // EXPERT_BLOCK
    tail = counts % EXPERT_BLOCK
    blk0 = jnp.cumsum(n_blk) - n_blk
    full0 = jnp.cumsum(n_full) - n_full
    next_used = lax.cummin(jnp.where(used, ids, N_EXPERTS), reverse=True)
    next_after = jnp.concatenate([next_used[1:], jnp.full((1,), N_EXPERTS, I32)])
    nxt = jnp.where(next_after < N_EXPERTS, next_after, -1)
    ordinal = jnp.cumsum(used.astype(I32)) - 1
    slot = ordinal % 2
    first = jnp.logical_and(used, ordinal == 0)
    latest = lax.cummax(jnp.where(tail > 0, ids, -1))
    before = jnp.concatenate([jnp.full((1,), -1, I32), latest[:-1]])
    ptail = jnp.where(before >= 0, tail[jnp.maximum(before, 0)], 0)
    ltail = jnp.where(latest[-1] >= 0, tail[jnp.maximum(latest[-1], 0)], 0).reshape(1)
    block_ends = jnp.cumsum(n_blk)
    g = jnp.arange(_max_expert_blocks(n_rows), dtype=I32)
    eg = jnp.minimum(jnp.searchsorted(block_ends, g, side="right"), N_EXPERTS - 1)
    blktok = jnp.where(g < block_ends[-1], starts[eg] + (g - blk0[eg]) * EXPERT_BLOCK, 0)
    return tuple(v.astype(I32) for v in (starts, counts, nxt, slot, first, blk0, full0, ptail, ltail, blktok,
                                         block_ends[-1:]))


def _experts(xs, counts, w_gate, w_up, w_down):
    d = D_MODEL
    meta = _expert_metadata(counts, xs.shape[0] // TOKEN_ROWS - EXPERT_BLOCK)
    hbm = pl.BlockSpec(memory_space=pl.ANY)
    blk = (EXPERT_BLOCK * TOKEN_ROWS, LANES)
    grid_spec = pltpu.PrefetchScalarGridSpec(
        num_scalar_prefetch=len(meta),
        grid=(N_EXPERTS,),
        in_specs=[hbm, hbm, hbm, hbm],
        out_specs=hbm,
        scratch_shapes=[pltpu.VMEM((2, d, EXPERT_FF), F32), pltpu.VMEM((2, d, EXPERT_FF), F32),
                        pltpu.VMEM((2, EXPERT_FF, d), F32),
                        pltpu.VMEM((d, EXPERT_FF), BF16), pltpu.VMEM((d, EXPERT_FF), BF16),
                        pltpu.VMEM((EXPERT_FF, d), BF16),
                        pltpu.VMEM((X_SLOTS,) + blk, U32), pltpu.VMEM((Y_SLOTS,) + blk, U32), pltpu.VMEM(blk, U32),
                        pltpu.SemaphoreType.DMA((2, 3)), pltpu.SemaphoreType.DMA((X_SLOTS,)),
                        pltpu.SemaphoreType.DMA((Y_SLOTS,)), pltpu.SemaphoreType.DMA(())],
    )
    return pl.pallas_call(
        _experts_kernel,
        grid_spec=grid_spec,
        out_shape=jax.ShapeDtypeStruct(xs.shape, U32),
        compiler_params=_params(("arbitrary",)),
        name="experts",
    )(*meta, xs, w_gate, w_up, w_down)


def _combine_kernel(pos_ref, gates_ref, h_ref, x1_ref, gatef_ref, wg_ref, wu_ref, wd_ref, ys_ref, out_ref, buf, sem):
    tm = h_ref.shape[0]

    def row_copy(t, k):
        return pltpu.make_async_copy(ys_ref.at[_token_rows(pos_ref[0, k, t])], buf.at[k, _token_rows(t)], sem)

    def start(t, carry):
        for k in range(TOP_K):
            row_copy(t, k).start(priority=k % 2)
        return carry

    def wait(t, carry):
        for k in range(TOP_K):
            row_copy(t, k).wait()
        return carry

    lax.fori_loop(0, tm, start, 0)
    hb = h_ref[...].astype(BF16)
    shared = _dot((_silu(_dot(hb, wg_ref[...])) * _dot(hb, wu_ref[...])).astype(BF16), wd_ref[...])
    lax.fori_loop(0, tm, wait, 0)
    gates = gates_ref[...]
    routed = _from_token_tiles(buf.at[0], tm) * gates[:, 0:1]
    for k in range(1, TOP_K):
        routed = routed + _from_token_tiles(buf.at[k], tm) * gates[:, k:k + 1]
    out_ref[...] = x1_ref[...] + gatef_ref[...] * (shared + routed)


def _combine(ys, pos_tiles, gates_t, h2, x1, gate_f, w_gate_s, w_up_s, w_down_s, b, s, tm):
    t, d = h2.shape
    tiles_per_seq = s // tm
    row = lambda n: pl.BlockSpec((tm, n), lambda i: (i, 0))
    full = lambda shp: pl.BlockSpec(shp, lambda i: (0,) * len(shp))
    return pl.pallas_call(
        _combine_kernel,
        grid=(t // tm,),
        in_specs=[pl.BlockSpec((1, TOP_K, tm), lambda i: (i, 0, 0), memory_space=pltpu.SMEM),
                  row(TOP_K), row(d), row(d),
                  pl.BlockSpec((None, 1, d), lambda i: (i // tiles_per_seq, 0, 0)),
                  full((d, EXPERT_FF)), full((d, EXPERT_FF)), full((EXPERT_FF, d)),
                  pl.BlockSpec(memory_space=pl.ANY)],
        out_specs=row(d),
        out_shape=jax.ShapeDtypeStruct((t, d), F32),
        scratch_shapes=[pltpu.VMEM((TOP_K, tm * TOKEN_ROWS, LANES), U32), pltpu.SemaphoreType.DMA(())],
        compiler_params=_params(("arbitrary",)),
        name="combine",
    )(pos_tiles, gates_t, h2, x1, gate_f.reshape(b, 1, d),
      w_gate_s.astype(BF16), w_up_s.astype(BF16), w_down_s.astype(BF16), ys)


def _moe_sublayer(x1, h2, h2t, gate_f, w_router, router_bias, w_gate, w_up, w_down, w_gate_s, w_up_s, w_down_s, b, s,
                  tm=256):
    t = b * s
    eidx, rank, gates, counts = _router(h2, w_router, router_bias)
    pos_tiles = _positions(counts, eidx, rank, tm)
    xs = _dispatch(h2t, pos_tiles, tm)
    ys = _experts(xs, counts[:, 0], w_gate, w_up, w_down)
    return _combine(ys, pos_tiles, gates.T, h2, x1, gate_f, w_gate_s, w_up_s, w_down_s, b, s, tm)


def kernel(x, c, w_ada, b_ada, norm_mix_gain, w_in, q_norm_gain, k_norm_gain, rel_bias_table, conv_w, conv_b, dt_bias,
           a_log, d_skip, ssm_norm_gain, w_out, norm_ffn_gain, w_router, router_bias, w_gate_experts, w_up_experts,
           w_down_experts, w_gate_shared, w_up_shared, w_down_shared):
    b, s, d = x.shape
    for layer in range(w_ada.shape[0]):
        mod = _adaln(c, w_ada[layer], b_ada[layer])
        x1, h2, h2t = _mixer_sublayer(x, mod, norm_mix_gain[layer], w_in[layer], q_norm_gain[layer], k_norm_gain[layer],
                                 rel_bias_table, conv_w[layer], conv_b[layer], dt_bias[layer], a_log[layer],
                                 d_skip[layer], ssm_norm_gain[layer], w_out[layer], norm_ffn_gain[layer])
        gate_f = mod[:, 5 * d:]
        out = _moe_sublayer(x1, h2, h2t, gate_f, w_router[layer], router_bias[layer], w_gate_experts[layer],
                            w_up_experts[layer], w_down_experts[layer], w_gate_shared[layer], w_up_shared[layer],
                            w_down_shared[layer], b, s)
        x = out.reshape(b, s, d)
    return x
```

```python
import functools
import math

import numpy as np
import jax
import jax.numpy as jnp
from jax import lax
from jax.experimental import pallas as pl
from jax.experimental.pallas import tpu as pltpu

F32 = jnp.float32
BF16 = jnp.bfloat16
I32 = jnp.int32

D_MODEL = 1024
ATTN_HEADS = 8
HEAD_DIM = 64
ATTN_WIDTH = ATTN_HEADS * HEAD_DIM
PATTERNS = ((128, 1), (512, 4), (2048, 16))
WIN_STEPS = 128
REL_BUCKETS = 32
REL_MAX_DISTANCE = 2048
SSM_HEADS = 24
SSM_HEAD_DIM = 64
SSM_WIDTH = SSM_HEADS * SSM_HEAD_DIM
SSM_GROUPS = 4
HEADS_PER_GROUP = SSM_HEADS // SSM_GROUPS
GROUP_WIDTH = SSM_WIDTH // SSM_GROUPS
SSM_STATE = 128
SSM_CONV = 4
SSM_CHUNK = 128
CONV_CH = SSM_WIDTH + 2 * SSM_GROUPS * SSM_STATE
N_EXPERTS = 256
TOP_K = 8
N_EXPERT_GROUPS = 8
EXPERTS_PER_GROUP = N_EXPERTS // N_EXPERT_GROUPS
TOPK_GROUPS = 4
EXPERT_FF = 256
ROUTED_SCALE = 2.5
NORM_EPS = 1e-6

LANES = 128
SUBLANES = 8
NEG_BIG = -1e30
VMEM_LIMIT = 56 * 1024 * 1024


def _params(sem, vmem=VMEM_LIMIT):
    return pltpu.CompilerParams(dimension_semantics=sem, vmem_limit_bytes=vmem)


def _sigmoid(x):
    return 1.0 / (1.0 + jnp.exp(-x))


def _silu(x):
    return x * _sigmoid(x)


def _split3(x):
    hi = x.astype(BF16)
    r = x - hi.astype(F32)
    mid = r.astype(BF16)
    lo = (r - mid.astype(F32)).astype(BF16)
    return hi, mid, lo


def _dot(a, b):
    return jnp.dot(a, b, preferred_element_type=F32)


def _dot_nt(a, b):
    return lax.dot_general(a, b, (((1,), (1,)), ((), ())), preferred_element_type=F32)


def _dot_exact_rhs(a, b_exact):
    hi, mid, lo = _split3(a)
    return _dot(hi, b_exact) + _dot(mid, b_exact) + _dot(lo, b_exact)


def _dot_exact_lhs(a_exact, b):
    hi, mid, lo = _split3(b)
    return _dot(a_exact, hi) + _dot(a_exact, mid) + _dot(a_exact, lo)


def _adaln_kernel(c_ref, w_ref, b_ref, o_ref):
    s = _silu(c_ref[...]).astype(BF16)
    o_ref[...] = _dot(s, w_ref[...].astype(BF16)) + b_ref[...]


def _adaln(c, w_ada, b_ada):
    b, d = c.shape
    n = w_ada.shape[1]
    rows = SUBLANES
    c_pad = jnp.zeros((rows, d), F32).at[:b].set(c)
    tn = 1024
    out = pl.pallas_call(
        _adaln_kernel,
        grid=(n // tn,),
        in_specs=[pl.BlockSpec((rows, d), lambda j: (0, 0)),
                  pl.BlockSpec((d, tn), lambda j: (0, j)),
                  pl.BlockSpec((1, tn), lambda j: (0, j))],
        out_specs=pl.BlockSpec((rows, tn), lambda j: (0, j)),
        out_shape=jax.ShapeDtypeStruct((rows, n), F32),
        compiler_params=_params(("arbitrary",)),
        name="adaln",
    )(c_pad, w_ada, b_ada.reshape(1, n))
    return out[:b]


def _inproj_kernel(x_ref, shift_ref, scale_ref, g_ref, wqkv_ref, wz_ref, wxbc_ref, wdt_ref,
                   qg_ref, kg_ref, hmean_ref, q_ref, k_ref, v_ref, z_ref, xbc_ref, dt_ref):
    x = x_ref[...]
    ms = jnp.mean(x * x, axis=-1, keepdims=True)
    h = x * lax.rsqrt(ms + NORM_EPS) * g_ref[...]
    h = h * (1.0 + scale_ref[...]) + shift_ref[...]
    hb = h.astype(BF16)

    hmean = hmean_ref[...]

    def head_norm(t, gain):
        ss = _dot_exact_rhs(t * t, hmean)
        return t * lax.rsqrt(ss + NORM_EPS) * gain

    q = _dot(hb, wqkv_ref[:, 0:ATTN_WIDTH])
    q_ref[...] = head_norm(q, qg_ref[...]) * (HEAD_DIM ** -0.5)
    k = _dot(hb, wqkv_ref[:, ATTN_WIDTH:2 * ATTN_WIDTH])
    k_ref[...] = head_norm(k, kg_ref[...])
    v_ref[...] = _dot(hb, wqkv_ref[:, 2 * ATTN_WIDTH:3 * ATTN_WIDTH])
    for c0 in range(0, SSM_WIDTH, 512):
        z_ref[:, c0:c0 + 512] = _dot(hb, wz_ref[:, c0:c0 + 512])
    for c0 in range(0, CONV_CH, 512):
        xbc_ref[:, c0:c0 + 512] = _dot(hb, wxbc_ref[:, c0:c0 + 512])
    dt_ref[...] = _dot(hb, wdt_ref[...])


def _in_proj(x, shift, scale, gain, w_in, q_gain, k_gain, tm=256):
    b, s, d = x.shape
    t = b * s
    tiles_per_seq = s // tm
    w = w_in.astype(BF16)
    o_z = 3 * ATTN_WIDTH
    o_x = o_z + SSM_WIDTH
    o_dt = o_x + CONV_CH
    w_qkv, w_z, w_xbc = w[:, :o_z], w[:, o_z:o_x], w[:, o_x:o_dt]
    w_dt = jnp.zeros((d, LANES), BF16).at[:, :SSM_HEADS].set(w[:, o_dt:])
    head_of = np.arange(ATTN_WIDTH) // HEAD_DIM
    hmean = jnp.asarray((head_of[:, None] == head_of[None, :]).astype(np.float32) / HEAD_DIM, BF16)
    full = lambda shp: pl.BlockSpec(shp, lambda i: (0,) * len(shp))
    row = lambda n: pl.BlockSpec((tm, n), lambda i: (i, 0))
    per_batch = pl.BlockSpec((None, 1, d), lambda i: (i // tiles_per_seq, 0, 0))
    outs = pl.pallas_call(
        _inproj_kernel,
        grid=(t // tm,),
        in_specs=[row(d), per_batch, per_batch, full((1, d)),
                  full((d, o_z)), full((d, SSM_WIDTH)), full((d, CONV_CH)), full((d, LANES)),
                  full((1, ATTN_WIDTH)), full((1, ATTN_WIDTH)), full((ATTN_WIDTH, ATTN_WIDTH))],
        out_specs=[row(ATTN_WIDTH), row(ATTN_WIDTH), row(ATTN_WIDTH), row(SSM_WIDTH), row(CONV_CH), row(LANES)],
        out_shape=[jax.ShapeDtypeStruct((t, n), F32)
                   for n in (ATTN_WIDTH, ATTN_WIDTH, ATTN_WIDTH, SSM_WIDTH, CONV_CH, LANES)],
        compiler_params=_params(("arbitrary",)),
        name="in_proj",
    )(x.reshape(t, d), shift.reshape(b, 1, d), scale.reshape(b, 1, d), gain.reshape(1, d),
      w_qkv, w_z, w_xbc, w_dt,
      jnp.tile(q_gain, ATTN_HEADS).reshape(1, ATTN_WIDTH), jnp.tile(k_gain, ATTN_HEADS).reshape(1, ATTN_WIDTH), hmean)
    return outs


def _t5_causal_buckets(distance):
    n = np.maximum(distance, 0)
    max_exact = REL_BUCKETS // 2
    large = max_exact + (np.log(np.maximum(n, 1) / max_exact) / math.log(REL_MAX_DISTANCE / max_exact)
                         * (REL_BUCKETS - max_exact)).astype(np.int64)
    large = np.minimum(large, REL_BUCKETS - 1)
    return np.where(n < max_exact, n, large).astype(np.int32)


def _window_bias(rel_bias_table, dilation):
    qi = np.arange(WIN_STEPS)[:, None]
    kj = np.arange(2 * WIN_STEPS)[None, :]
    dist = qi + WIN_STEPS - kj
    band = (dist >= 0) & (dist <= WIN_STEPS)
    onehot = (_t5_causal_buckets(dist * dilation).reshape(-1, 1) == np.arange(REL_BUCKETS)[None, :]).astype(np.float32)
    bias = jnp.dot(rel_bias_table.astype(F32).T, jnp.asarray(onehot).T, precision=lax.Precision.HIGHEST)
    bias = bias.reshape(ATTN_HEADS, WIN_STEPS, 2 * WIN_STEPS)
    return jnp.where(jnp.asarray(band)[None], bias, NEG_BIG)


ATTN_TOKENS = max(w for w, _ in PATTERNS)
ATTN_UNROLL = 4


def _attn_kernel(q_ref, kp_ref, kc_ref, vp_ref, vc_ref, bias_ref, out_ref, kw, vw, o_acc, l_acc):
    tb = ATTN_TOKENS
    first = pl.program_id(2) == 0
    kw[0:tb] = kp_ref[...]
    kw[tb:2 * tb] = kc_ref[...]
    vw[0:tb] = vp_ref[...]
    vw[tb:2 * tb] = vc_ref[...]
    lane = lax.broadcasted_iota(I32, (WIN_STEPS, LANES), 1)
    head0 = lane < HEAD_DIM
    col = lax.broadcasted_iota(I32, (WIN_STEPS, 2 * WIN_STEPS), 1)
    in_prev = col < WIN_STEPS

    for p, (_, d) in enumerate(PATTERNS):
        shift = d.bit_length() - 1
        n_blocks = tb // WIN_STEPS

        def rows(start, n, d=d):
            return pl.ds(start, n, stride=d) if d > 1 else pl.ds(start, n)

        def body(it, carry, p=p, d=d, shift=shift, rows=rows):
            for u in range(ATTN_UNROLL):
                idx = it * ATTN_UNROLL + u
                r = jnp.bitwise_and(idx, d - 1)
                j = jnp.right_shift(idx, shift)
                qs = j * (WIN_STEPS * d) + r
                q = q_ref[rows(qs, WIN_STEPS), :]
                k = kw[rows(tb + qs - WIN_STEPS * d, 2 * WIN_STEPS), :].astype(BF16)
                v = vw[rows(tb + qs - WIN_STEPS * d, 2 * WIN_STEPS), :].astype(BF16)
                no_prev = jnp.logical_and(in_prev, jnp.logical_and(first, j == 0))
                o_h, lse_h = [], []
                for h in range(2):
                    qh = jnp.where(head0 if h == 0 else jnp.logical_not(head0), q, 0.0).astype(BF16)
                    s = _dot_nt(qh, k) + bias_ref[p, h]
                    s = jnp.where(no_prev, NEG_BIG, s)
                    m = jnp.max(s, axis=-1, keepdims=True)
                    e = jnp.exp(s - m)
                    denom = jnp.sum(e, axis=-1, keepdims=True)
                    o_h.append(_dot(e.astype(BF16), v) / denom)
                    lse_h.append(m + jnp.log(denom))
                o_acc[p, rows(qs, WIN_STEPS), :] = jnp.where(head0, o_h[0], o_h[1])
                l_acc[p, rows(qs, WIN_STEPS), :] = jnp.where(head0, lse_h[0], lse_h[1])
            return carry

        lax.fori_loop(0, n_blocks // ATTN_UNROLL, body, 0)

    chunk = 256
    for c0 in range(0, tb, chunk):
        l1, l2, l3 = (l_acc[p, c0:c0 + chunk, :] for p in range(3))
        m = jnp.maximum(jnp.maximum(l1, l2), l3)
        e1, e2, e3 = jnp.exp(l1 - m), jnp.exp(l2 - m), jnp.exp(l3 - m)
        num = e1 * o_acc[0, c0:c0 + chunk, :] + e2 * o_acc[1, c0:c0 + chunk, :] + e3 * o_acc[2, c0:c0 + chunk, :]
        out_ref[c0:c0 + chunk, :] = num / (e1 + e2 + e3)


def _attention(q, k, v, bias):
    b, s, w = q.shape
    tb = ATTN_TOKENS
    pairs = ATTN_HEADS // 2
    cur = pl.BlockSpec((None, tb, LANES), lambda bi, hp, i: (bi, i, hp))
    prev = pl.BlockSpec((None, tb, LANES), lambda bi, hp, i: (bi, jnp.maximum(i - 1, 0), hp))
    return pl.pallas_call(
        _attn_kernel,
        grid=(b, pairs, s // tb),
        in_specs=[cur, prev, cur, prev, cur,
                  pl.BlockSpec((len(PATTERNS), 2, WIN_STEPS, 2 * WIN_STEPS), lambda bi, hp, i: (0, hp, 0, 0))],
        out_specs=cur,
        out_shape=jax.ShapeDtypeStruct((b, s, w), F32),
        scratch_shapes=[pltpu.VMEM((2 * tb, LANES), F32), pltpu.VMEM((2 * tb, LANES), F32),
                        pltpu.VMEM((len(PATTERNS), tb, LANES), F32), pltpu.VMEM((len(PATTERNS), tb, LANES), F32)],
        compiler_params=_params(("arbitrary",) * 3),
        name="attention",
    )(q, k, k, v, v, bias)


def _ssd_kernel(xbc_ref, halo_ref, z_ref, dtraw_ref, convw_ref, convb_ref, dtb_ref, alog_ref, dskip_ref, gain_ref,
                expand_ref, tril_ref, y_ref, state_ref):
    c = pl.program_id(1)

    @pl.when(c == 0)
    def _():
        state_ref[...] = jnp.zeros_like(state_ref)

    x = xbc_ref[...]
    halo = jnp.where(c == 0, 0.0, halo_ref[...])
    w = convw_ref[...]
    acc = x * w[SSM_CONV - 1:SSM_CONV, :] + convb_ref[...]
    row8 = lax.broadcasted_iota(I32, (SUBLANES, CONV_CH), 0)
    for shift in range(1, SSM_CONV):
        xs = pltpu.roll(x, shift, axis=0)
        hs = pltpu.roll(halo, shift, axis=0)
        head = jnp.where(row8 < shift, hs, xs[0:SUBLANES])
        xs = jnp.concatenate([head, xs[SUBLANES:]], axis=0)
        acc = acc + xs * w[SSM_CONV - 1 - shift:SSM_CONV - shift, :]
    act = _silu(acc)
    x_s = act[:, :SSM_WIDTH]
    bc0 = SSM_WIDTH
    cc0 = SSM_WIDTH + SSM_GROUPS * SSM_STATE

    t = dtraw_ref[...] + dtb_ref[...]
    dt = jnp.maximum(t, 0.0) + jnp.log(1.0 + jnp.exp(-jnp.abs(t)))
    a = dt * (-jnp.exp(alog_ref[...]))
    a_cs = _dot_exact_lhs(tril_ref[...], a)
    a_cs_t = a_cs.T
    a_last = a_cs[SSM_CHUNK - 1:SSM_CHUNK, :]
    expand = expand_ref[...]
    dt_e = _dot_exact_rhs(dt, expand)
    ea_e = _dot_exact_rhs(jnp.exp(a_cs), expand)
    dte_e = _dot_exact_rhs(jnp.exp(a_last - a_cs), expand)
    xdt = x_s * dt_e
    xw = (xdt * dte_e).astype(BF16)
    xdt_b = xdt.astype(BF16)

    li = lax.broadcasted_iota(I32, (SSM_CHUNK, SSM_CHUNK), 0)
    si = lax.broadcasted_iota(I32, (SSM_CHUNK, SSM_CHUNK), 1)
    causal = li >= si

    ys = []
    for g in range(SSM_GROUPS):
        gs = slice(g * GROUP_WIDTH, (g + 1) * GROUP_WIDTH)
        b_g = act[:, bc0 + g * SSM_STATE:bc0 + (g + 1) * SSM_STATE]
        c_g = act[:, cc0 + g * SSM_STATE:cc0 + (g + 1) * SSM_STATE].astype(BF16)
        cb = _dot_nt(c_g, b_g.astype(BF16))
        state = state_ref[g]
        y_off = _dot(c_g, state.astype(BF16)) * ea_e[:, gs]
        parts = []
        for j in range(HEADS_PER_GROUP):
            hh = g * HEADS_PER_GROUP + j
            seg = a_cs[:, hh:hh + 1] - a_cs_t[hh:hh + 1, :]
            decay = jnp.exp(jnp.where(causal, seg, NEG_BIG))
            m = (cb * decay).astype(BF16)
            parts.append(_dot(m, xdt_b[:, hh * SSM_HEAD_DIM:(hh + 1) * SSM_HEAD_DIM]))
        ys.append(jnp.concatenate(parts, axis=1) + y_off)
        state_ref[g] = state * ea_e[SSM_CHUNK - 1:SSM_CHUNK, gs] + _dot(b_g.T.astype(BF16), xw[:, gs])
    y = jnp.concatenate(ys, axis=1) + dskip_ref[...] * x_s
    y = y * _silu(z_ref[...])
    gain = gain_ref[...]
    for g in range(SSM_GROUPS):
        gs = slice(g * GROUP_WIDTH, (g + 1) * GROUP_WIDTH)
        yg = y[:, gs]
        ms = jnp.mean(yg * yg, axis=-1, keepdims=True)
        y_ref[:, gs] = yg * lax.rsqrt(ms + NORM_EPS) * gain[:, gs]


def _ssd(xbc, z, dt_raw, conv_w, conv_b, dt_bias, a_log, d_skip, norm_gain, b, s):
    t = b * s
    nc = s // SSM_CHUNK
    pad_heads = lambda v: jnp.zeros((1, LANES), F32).at[0, :SSM_HEADS].set(v)
    head_of_lane = np.arange(SSM_WIDTH) // SSM_HEAD_DIM
    expand = jnp.asarray((np.arange(LANES)[:, None] == head_of_lane[None, :]).astype(np.float32), BF16)
    tril = jnp.asarray(np.tril(np.ones((SSM_CHUNK, SSM_CHUNK), np.float32)), BF16)
    halo_blocks = SSM_CHUNK // SUBLANES
    chunk = lambda n: pl.BlockSpec((SSM_CHUNK, n), lambda bi, c: (bi * nc + c, 0))
    full = lambda shp: pl.BlockSpec(shp, lambda bi, c: (0,) * len(shp))
    halo = pl.BlockSpec((SUBLANES, CONV_CH), lambda bi, c: (jnp.maximum((bi * nc + c) * halo_blocks - 1, 0), 0))
    return pl.pallas_call(
        _ssd_kernel,
        grid=(b, nc),
        in_specs=[chunk(CONV_CH), halo, chunk(SSM_WIDTH), chunk(LANES),
                  full((SSM_CONV, CONV_CH)), full((1, CONV_CH)), full((1, LANES)), full((1, LANES)),
                  full((1, SSM_WIDTH)), full((1, SSM_WIDTH)), full((LANES, SSM_WIDTH)), full((SSM_CHUNK, SSM_CHUNK))],
        out_specs=chunk(SSM_WIDTH),
        out_shape=jax.ShapeDtypeStruct((t, SSM_WIDTH), F32),
        scratch_shapes=[pltpu.VMEM((SSM_GROUPS, SSM_STATE, GROUP_WIDTH), F32)],
        compiler_params=_params(("arbitrary", "arbitrary")),
        name="ssd",
    )(xbc, xbc, z, dt_raw, conv_w, conv_b.reshape(1, CONV_CH), pad_heads(dt_bias), pad_heads(a_log),
      jnp.repeat(d_skip, SSM_HEAD_DIM).reshape(1, SSM_WIDTH), norm_gain.reshape(1, SSM_WIDTH), expand, tril)


U32 = jnp.uint32
TOKEN_ROWS = D_MODEL // (2 * LANES)
HIGH_HALF = np.uint32(0xFFFF0000)


def _to_token_tiles(ref, x):
    n = x.shape[0]
    for c in range(TOKEN_ROWS):
        lo = lax.bitcast_convert_type(x[:, c * LANES:(c + 1) * LANES].astype(BF16).astype(F32), U32)
        hi = lax.bitcast_convert_type(x[:, (c + TOKEN_ROWS) * LANES:(c + TOKEN_ROWS + 1) * LANES]
                                      .astype(BF16).astype(F32), U32)
        ref[pl.ds(c, n, stride=TOKEN_ROWS), :] = jnp.bitwise_or(jnp.right_shift(lo, 16), jnp.bitwise_and(hi, HIGH_HALF))


def _from_token_tiles(ref, n, token0=0):
    lows, highs = [], []
    for c in range(TOKEN_ROWS):
        word = ref[pl.ds(token0 * TOKEN_ROWS + c, n, stride=TOKEN_ROWS), :]
        lows.append(lax.bitcast_convert_type(jnp.left_shift(word, 16), F32))
        highs.append(lax.bitcast_convert_type(jnp.bitwise_and(word, HIGH_HALF), F32))
    return jnp.concatenate(lows + highs, axis=1)


def _outproj_kernel(attn_ref, ssm_ref, x_ref, gate_ref, shift_ref, scale_ref, g_ref, wa_ref, ws_ref, x1_ref, h2_ref,
                    h2t_ref):
    mixed = _dot(attn_ref[...].astype(BF16), wa_ref[...]) + _dot(ssm_ref[...].astype(BF16), ws_ref[...])
    x1 = x_ref[...] + gate_ref[...] * mixed
    x1_ref[...] = x1
    ms = jnp.mean(x1 * x1, axis=-1, keepdims=True)
    h = x1 * lax.rsqrt(ms + NORM_EPS) * g_ref[...]
    h2 = h * (1.0 + scale_ref[...]) + shift_ref[...]
    h2_ref[...] = h2
    _to_token_tiles(h2t_ref, h2)


def _out_proj(attn, ssm, x, gate, shift, scale, gain, w_out, b, s, tm=256):
    t = b * s
    d = D_MODEL
    tiles_per_seq = s // tm
    w = w_out.astype(BF16)
    row = lambda n: pl.BlockSpec((tm, n), lambda i: (i, 0))
    full = lambda shp: pl.BlockSpec(shp, lambda i: (0,) * len(shp))
    per_batch = pl.BlockSpec((None, 1, d), lambda i: (i // tiles_per_seq, 0, 0))
    return pl.pallas_call(
        _outproj_kernel,
        grid=(t // tm,),
        in_specs=[row(ATTN_WIDTH), row(SSM_WIDTH), row(d), per_batch, per_batch, per_batch, full((1, d)),
                  full((ATTN_WIDTH, d)), full((SSM_WIDTH, d))],
        out_specs=[row(d), row(d), pl.BlockSpec((tm * TOKEN_ROWS, LANES), lambda i: (i, 0))],
        out_shape=[jax.ShapeDtypeStruct((t, d), F32)] * 2 + [jax.ShapeDtypeStruct((t * TOKEN_ROWS, LANES), U32)],
        compiler_params=_params(("arbitrary",)),
        name="out_proj",
    )(attn.reshape(t, ATTN_WIDTH), ssm, x.reshape(t, d),
      gate.reshape(b, 1, d), shift.reshape(b, 1, d), scale.reshape(b, 1, d), gain.reshape(1, d),
      w[:ATTN_WIDTH], w[ATTN_WIDTH:])


def _mixer_sublayer(x, mod, norm_mix_gain, w_in, q_norm_gain, k_norm_gain, rel_bias_table, conv_w, conv_b, dt_bias,
                    a_log, d_skip, ssm_norm_gain, w_out, norm_ffn_gain):
    b, s, d = x.shape
    shift_m, scale_m, gate_m, shift_f, scale_f, _ = jnp.split(mod, 6, axis=-1)
    q, k, v, z, xbc, dt_raw = _in_proj(x, shift_m, scale_m, norm_mix_gain, w_in, q_norm_gain, k_norm_gain)
    bias = jnp.stack([_window_bias(rel_bias_table, dilation) for _, dilation in PATTERNS])
    attn = _attention(q.reshape(b, s, ATTN_WIDTH), k.reshape(b, s, ATTN_WIDTH), v.reshape(b, s, ATTN_WIDTH), bias)
    ssm = _ssd(xbc, z, dt_raw, conv_w, conv_b, dt_bias, a_log, d_skip, ssm_norm_gain, b, s)
    return _out_proj(attn, ssm, x, gate_m, shift_f, scale_f, norm_ffn_gain, w_out, b, s)


def _first_argmax(v, iota, limit):
    m = jnp.max(v, axis=0, keepdims=True)
    idx = jnp.min(jnp.where(v == m, iota, limit), axis=0, keepdims=True)
    return m, idx


def _router_kernel(h_ref, wt_ref, bias_ref, upper_ref, eidx_ref, rank_ref, gate_ref, counts_ref, carry_ref):
    @pl.when(pl.program_id(0) == 0)
    def _():
        carry_ref[...] = jnp.zeros_like(carry_ref)

    tm = h_ref.shape[0]
    h = h_ref[...]
    wt = wt_ref[...]
    h_hi = h.astype(BF16)
    h_lo = (h - h_hi.astype(F32)).astype(BF16)
    w_hi = wt.astype(BF16)
    w_lo = (wt - w_hi.astype(F32)).astype(BF16)
    logits = _dot_nt(w_hi, h_hi) + _dot_nt(w_hi, h_lo) + _dot_nt(w_lo, h_hi)
    scores = _sigmoid(logits)
    choice = scores + bias_ref[...]
    neg_inf = -jnp.inf

    iota_g = lax.broadcasted_iota(I32, (EXPERTS_PER_GROUP, tm), 0).astype(F32)
    group_rows = []
    for g in range(N_EXPERT_GROUPS):
        v = choice[g * EXPERTS_PER_GROUP:(g + 1) * EXPERTS_PER_GROUP]
        m1, i1 = _first_argmax(v, iota_g, float(EXPERTS_PER_GROUP))
        m2 = jnp.max(jnp.where(iota_g == i1, neg_inf, v), axis=0, keepdims=True)
        group_rows.append(m1 + m2)
    group_scores = jnp.concatenate(group_rows, axis=0)

    iota_n = lax.broadcasted_iota(I32, (N_EXPERT_GROUPS, tm), 0).astype(F32)
    chosen = jnp.zeros((N_EXPERT_GROUPS, tm), F32)
    for _ in range(TOPK_GROUPS):
        _, gi = _first_argmax(group_scores, iota_n, float(N_EXPERT_GROUPS))
        hit = iota_n == gi
        chosen = jnp.where(hit, 1.0, chosen)
        group_scores = jnp.where(hit, neg_inf, group_scores)

    masked = jnp.concatenate(
        [jnp.where(chosen[g:g + 1] > 0.0, choice[g * EXPERTS_PER_GROUP:(g + 1) * EXPERTS_PER_GROUP], neg_inf)
         for g in range(N_EXPERT_GROUPS)], axis=0)

    iota_e = lax.broadcasted_iota(I32, (N_EXPERTS, tm), 0).astype(F32)
    picked, gates = [], []
    onehot = jnp.zeros((N_EXPERTS, tm), F32)
    for _ in range(TOP_K):
        _, ei = _first_argmax(masked, iota_e, float(N_EXPERTS))
        hit = iota_e == ei
        gates.append(jnp.sum(jnp.where(hit, scores, 0.0), axis=0, keepdims=True))
        masked = jnp.where(hit, neg_inf, masked)
        onehot = jnp.where(hit, 1.0, onehot)
        picked.append(ei)
    gate_sum = gates[0]
    for gk in gates[1:]:
        gate_sum = gate_sum + gk

    base = _dot(onehot.astype(BF16), upper_ref[...]) + carry_ref[...]
    ranks = [jnp.sum(jnp.where(iota_e == ei, base, 0.0), axis=0, keepdims=True) for ei in picked]
    carry_ref[...] = carry_ref[...] + jnp.sum(onehot, axis=1, keepdims=True)

    eidx_ref[...] = jnp.concatenate(picked, axis=0).astype(I32)
    rank_ref[...] = jnp.concatenate(ranks, axis=0).astype(I32)
    gate_ref[...] = jnp.concatenate([gk / gate_sum * ROUTED_SCALE for gk in gates], axis=0)
    counts_ref[...] = carry_ref[...].astype(I32)


def _router(h2, w_router, router_bias, tm=256):
    t, d = h2.shape
    upper = jnp.asarray(np.triu(np.ones((tm, tm), np.float32), 1), BF16)
    tok = pl.BlockSpec((TOP_K, tm), lambda i: (0, i))
    full = lambda shp: pl.BlockSpec(shp, lambda i: (0,) * len(shp))
    return pl.pallas_call(
        _router_kernel,
        grid=(t // tm,),
        in_specs=[pl.BlockSpec((tm, d), lambda i: (i, 0)), full((N_EXPERTS, d)), full((N_EXPERTS, 1)), full((tm, tm))],
        out_specs=[tok, tok, tok, full((N_EXPERTS, 1))],
        out_shape=[jax.ShapeDtypeStruct((TOP_K, t), I32), jax.ShapeDtypeStruct((TOP_K, t), I32),
                   jax.ShapeDtypeStruct((TOP_K, t), F32), jax.ShapeDtypeStruct((N_EXPERTS, 1), I32)],
        scratch_shapes=[pltpu.VMEM((N_EXPERTS, 1), F32)],
        compiler_params=_params(("arbitrary",)),
        name="router",
    )(h2, w_router.T, router_bias.reshape(N_EXPERTS, 1), upper)


def _positions_kernel(counts_ref, lower_ref, eidx_ref, rank_ref, pos_ref):
    tm = eidx_ref.shape[1]
    counts = jnp.broadcast_to(counts_ref[...].astype(F32), (N_EXPERTS, LANES))
    offsets = _dot_exact_lhs(lower_ref[...], counts)[:, 0:1]
    iota_e = lax.broadcasted_iota(I32, (N_EXPERTS, tm), 0).astype(F32)
    e = eidx_ref[...].astype(F32)
    rows = [jnp.sum(jnp.where(iota_e == e[k:k + 1], offsets, 0.0), axis=0, keepdims=True) for k in range(TOP_K)]
    pos_ref[0] = jnp.concatenate(rows, axis=0).astype(I32) + rank_ref[...]


def _positions(counts, eidx, rank, tm):
    t = eidx.shape[1]
    lower = jnp.asarray(np.tril(np.ones((N_EXPERTS, N_EXPERTS), np.float32), -1), BF16)
    tok = pl.BlockSpec((TOP_K, tm), lambda i: (0, i))
    return pl.pallas_call(
        _positions_kernel,
        grid=(t // tm,),
        in_specs=[pl.BlockSpec((N_EXPERTS, 1), lambda i: (0, 0)), pl.BlockSpec((N_EXPERTS, N_EXPERTS), lambda i: (0, 0)),
                  tok, tok],
        out_specs=pl.BlockSpec((1, TOP_K, tm), lambda i: (i, 0, 0)),
        out_shape=jax.ShapeDtypeStruct((t // tm, TOP_K, tm), I32),
        compiler_params=_params(("arbitrary",)),
        name="positions",
    )(counts, lower, eidx, rank)


def _token_rows(i):
    return pl.ds(pl.multiple_of(i * TOKEN_ROWS, TOKEN_ROWS), TOKEN_ROWS)


def _dispatch_kernel(pos_ref, h_ref, xs_ref, sem):
    tm = h_ref.shape[0] // TOKEN_ROWS

    def row_copy(t, k):
        return pltpu.make_async_copy(h_ref.at[_token_rows(t)], xs_ref.at[_token_rows(pos_ref[0, k, t])], sem)

    def start(t, carry):
        for k in range(TOP_K):
            row_copy(t, k).start(priority=k % 2)
        return carry

    def wait(t, carry):
        for k in range(TOP_K):
            row_copy(t, k).wait()
        return carry

    lax.fori_loop(0, tm, start, 0)
    lax.fori_loop(0, tm, wait, 0)


def _dispatch(h2t, pos_tiles, tm):
    t = h2t.shape[0] // TOKEN_ROWS
    return pl.pallas_call(
        _dispatch_kernel,
        grid=(t // tm,),
        in_specs=[pl.BlockSpec((1, TOP_K, tm), lambda i: (i, 0, 0), memory_space=pltpu.SMEM),
                  pl.BlockSpec((tm * TOKEN_ROWS, LANES), lambda i: (i, 0))],
        out_specs=pl.BlockSpec(memory_space=pl.ANY),
        out_shape=jax.ShapeDtypeStruct(((t * TOP_K + EXPERT_BLOCK) * TOKEN_ROWS, LANES), h2t.dtype),
        scratch_shapes=[pltpu.SemaphoreType.DMA(())],
        compiler_params=_params(("arbitrary",)),
        name="dispatch",
    )(pos_tiles, h2t)


EXPERT_BLOCK = 256
TAIL_PIECES = tuple(1 << i for i in reversed(range(EXPERT_BLOCK.bit_length() - 1)))
X_SLOTS = 4
Y_SLOTS = 4


def _experts_kernel(start_ref, count_ref, nxt_ref, slot_ref, first_ref, blk0_ref, full0_ref, ptail_ref, ltail_ref,
                    blktok_ref, nblocks_ref, xs_hbm, wg_hbm, wu_hbm, wd_hbm, ys_hbm,
                    wg_buf, wu_buf, wd_buf, wg_bf, wu_bf, wd_bf, xbuf, ybuf, ytail, wsem, xsem, ysem, tsem):
    e = pl.program_id(0)
    last_step = e == pl.num_programs(0) - 1
    start, count = start_ref[e], count_ref[e]
    n_full = jnp.right_shift(count, EXPERT_BLOCK.bit_length() - 1)
    tail = jnp.bitwise_and(count, EXPERT_BLOCK - 1)
    n_blk = n_full + (tail > 0).astype(I32)
    blk0, full0 = blk0_ref[e], full0_ref[e]
    slot, nxt = slot_ref[e], nxt_ref[e]

    def token_rows(token, n):
        return pl.ds(pl.multiple_of(token * TOKEN_ROWS, TOKEN_ROWS), n * TOKEN_ROWS)

    def fetch(ex, s):
        return (pltpu.make_async_copy(wg_hbm.at[ex], wg_buf.at[s], wsem.at[s, 0]),
                pltpu.make_async_copy(wu_hbm.at[ex], wu_buf.at[s], wsem.at[s, 1]),
                pltpu.make_async_copy(wd_hbm.at[ex], wd_buf.at[s], wsem.at[s, 2]))

    def x_copy(token, s):
        return pltpu.make_async_copy(xs_hbm.at[token_rows(token, EXPERT_BLOCK)], xbuf.at[s], xsem.at[s])

    def y_copy(token, s):
        return pltpu.make_async_copy(ybuf.at[s], ys_hbm.at[token_rows(token, EXPERT_BLOCK)], ysem.at[s])

    def tail_copies(token, length):
        out = []
        for piece in TAIL_PIECES:
            bigger = (EXPERT_BLOCK - 1) & ~(2 * piece - 1)
            done = jnp.bitwise_and(length, bigger)
            cp = pltpu.make_async_copy(ytail.at[token_rows(done, piece)], ys_hbm.at[token_rows(token + done, piece)],
                                       tsem)
            out.append((jnp.bitwise_and(length, piece) != 0, cp))
        return out

    def block(s):
        x = _from_token_tiles(xbuf.at[s], EXPERT_BLOCK).astype(BF16)
        g = _dot(x, wg_bf[...])
        u = _dot(x, wu_bf[...])
        return _dot((_silu(g) * u).astype(BF16), wd_bf[...])

    @pl.when(count > 0)
    def _():
        @pl.when(first_ref[e] == 1)
        def _():
            for g in range(X_SLOTS - 1):
                @pl.when(g < nblocks_ref[0])
                def _(g=g):
                    x_copy(blktok_ref[g], g).start()
            for cp in fetch(e, slot):
                cp.start()

        for cp in fetch(e, slot):
            cp.wait()

        @pl.when(nxt >= 0)
        def _():
            for cp in fetch(nxt, 1 - slot):
                cp.start()

        wg_bf[...] = wg_buf[slot].astype(BF16)
        wu_bf[...] = wu_buf[slot].astype(BF16)
        wd_bf[...] = wd_buf[slot].astype(BF16)

        def step(i):
            g = blk0 + i
            xs_slot = jnp.bitwise_and(g, X_SLOTS - 1)
            x_copy(start, xs_slot).wait()
            ahead = g + (X_SLOTS - 1)

            @pl.when(ahead < nblocks_ref[0])
            def _():
                x_copy(blktok_ref[ahead], jnp.bitwise_and(ahead, X_SLOTS - 1)).start()

            return block(xs_slot)

        def full_block(i, carry):
            y = step(i)
            j = full0 + i
            ys_slot = jnp.bitwise_and(j, Y_SLOTS - 1)

            @pl.when(j >= Y_SLOTS)
            def _():
                y_copy(start, ys_slot).wait()

            _to_token_tiles(ybuf.at[ys_slot], y)
            y_copy(start + i * EXPERT_BLOCK, ys_slot).start()
            return carry

        lax.fori_loop(0, n_full, full_block, 0)

        @pl.when(tail > 0)
        def _():
            y = step(n_full)
            for pred, cp in tail_copies(start, ptail_ref[e]):
                @pl.when(pred)
                def _(cp=cp):
                    cp.wait()
            _to_token_tiles(ytail, y)
            for pred, cp in tail_copies(start + n_full * EXPERT_BLOCK, tail):
                @pl.when(pred)
                def _(cp=cp):
                    cp.start()

    @pl.when(last_step)
    def _():
        total_full = full0 + n_full
        for back in range(1, Y_SLOTS + 1):
            @pl.when(total_full >= back)
            def _(back=back):
                y_copy(0, jnp.bitwise_and(total_full - back, Y_SLOTS - 1)).wait()
        for pred, cp in tail_copies(0, ltail_ref[0]):
            @pl.when(pred)
            def _(cp=cp):
                cp.wait()


def _max_expert_blocks(n_rows):
    return n_rows // EXPERT_BLOCK + N_EXPERTS


def _expert_metadata(counts, n_rows):
    ids = jnp.arange(N_EXPERTS, dtype=I32)
    used = counts > 0
    starts = jnp.cumsum(counts) - counts
    n_blk = (counts + EXPERT_BLOCK - 1) // EXPERT_BLOCK
    n_full = counts // EXPERT_BLOCK
    tail = counts % EXPERT_BLOCK
    blk0 = jnp.cumsum(n_blk) - n_blk
    full0 = jnp.cumsum(n_full) - n_full
    next_used = lax.cummin(jnp.where(used, ids, N_EXPERTS), reverse=True)
    next_after = jnp.concatenate([next_used[1:], jnp.full((1,), N_EXPERTS, I32)])
    nxt = jnp.where(next_after < N_EXPERTS, next_after, -1)
    ordinal = jnp.cumsum(used.astype(I32)) - 1
    slot = ordinal % 2
    first = jnp.logical_and(used, ordinal == 0)
    latest = lax.cummax(jnp.where(tail > 0, ids, -1))
    before = jnp.concatenate([jnp.full((1,), -1, I32), latest[:-1]])
    ptail = jnp.where(before >= 0, tail[jnp.maximum(before, 0)], 0)
    ltail = jnp.where(latest[-1] >= 0, tail[jnp.maximum(latest[-1], 0)], 0).reshape(1)
    block_ends = jnp.cumsum(n_blk)
    g = jnp.arange(_max_expert_blocks(n_rows), dtype=I32)
    eg = jnp.minimum(jnp.searchsorted(block_ends, g, side="right"), N_EXPERTS - 1)
    blktok = jnp.where(g < block_ends[-1], starts[eg] + (g - blk0[eg]) * EXPERT_BLOCK, 0)
    return tuple(v.astype(I32) for v in (starts, counts, nxt, slot, first, blk0, full0, ptail, ltail, blktok,
                                         block_ends[-1:]))


def _experts(xs, counts, w_gate, w_up, w_down):
    d = D_MODEL
    meta = _expert_metadata(counts, xs.shape[0] // TOKEN_ROWS - EXPERT_BLOCK)
    hbm = pl.BlockSpec(memory_space=pl.ANY)
    blk = (EXPERT_BLOCK * TOKEN_ROWS, LANES)
    grid_spec = pltpu.PrefetchScalarGridSpec(
        num_scalar_prefetch=len(meta),
        grid=(N_EXPERTS,),
        in_specs=[hbm, hbm, hbm, hbm],
        out_specs=hbm,
        scratch_shapes=[pltpu.VMEM((2, d, EXPERT_FF), F32), pltpu.VMEM((2, d, EXPERT_FF), F32),
                        pltpu.VMEM((2, EXPERT_FF, d), F32),
                        pltpu.VMEM((d, EXPERT_FF), BF16), pltpu.VMEM((d, EXPERT_FF), BF16),
                        pltpu.VMEM((EXPERT_FF, d), BF16),
                        pltpu.VMEM((X_SLOTS,) + blk, U32), pltpu.VMEM((Y_SLOTS,) + blk, U32), pltpu.VMEM(blk, U32),
                        pltpu.SemaphoreType.DMA((2, 3)), pltpu.SemaphoreType.DMA((X_SLOTS,)),
                        pltpu.SemaphoreType.DMA((Y_SLOTS,)), pltpu.SemaphoreType.DMA(())],
    )
    return pl.pallas_call(
        _experts_kernel,
        grid_spec=grid_spec,
        out_shape=jax.ShapeDtypeStruct(xs.shape, U32),
        compiler_params=_params(("arbitrary",)),
        name="experts",
    )(*meta, xs, w_gate, w_up, w_down)


def _combine_kernel(pos_ref, gates_ref, h_ref, x1_ref, gatef_ref, wg_ref, wu_ref, wd_ref, ys_ref, out_ref, buf, sem):
    tm = h_ref.shape[0]

    def row_copy(t, k):
        return pltpu.make_async_copy(ys_ref.at[_token_rows(pos_ref[0, k, t])], buf.at[k, _token_rows(t)], sem)

    def start(t, carry):
        for k in range(TOP_K):
            row_copy(t, k).start(priority=k % 2)
        return carry

    def wait(t, carry):
        for k in range(TOP_K):
            row_copy(t, k).wait()
        return carry

    lax.fori_loop(0, tm, start, 0)
    hb = h_ref[...].astype(BF16)
    shared = _dot((_silu(_dot(hb, wg_ref[...])) * _dot(hb, wu_ref[...])).astype(BF16), wd_ref[...])
    lax.fori_loop(0, tm, wait, 0)
    gates = gates_ref[...]
    routed = _from_token_tiles(buf.at[0], tm) * gates[:, 0:1]
    for k in range(1, TOP_K):
        routed = routed + _from_token_tiles(buf.at[k], tm) * gates[:, k:k + 1]
    out_ref[...] = x1_ref[...] + gatef_ref[...] * (shared + routed)


def _combine(ys, pos_tiles, gates_t, h2, x1, gate_f, w_gate_s, w_up_s, w_down_s, b, s, tm):
    t, d = h2.shape
    tiles_per_seq = s // tm
    row = lambda n: pl.BlockSpec((tm, n), lambda i: (i, 0))
    full = lambda shp: pl.BlockSpec(shp, lambda i: (0,) * len(shp))
    return pl.pallas_call(
        _combine_kernel,
        grid=(t // tm,),
        in_specs=[pl.BlockSpec((1, TOP_K, tm), lambda i: (i, 0, 0), memory_space=pltpu.SMEM),
                  row(TOP_K), row(d), row(d),
                  pl.BlockSpec((None, 1, d), lambda i: (i // tiles_per_seq, 0, 0)),
                  full((d, EXPERT_FF)), full((d, EXPERT_FF)), full((EXPERT_FF, d)),
                  pl.BlockSpec(memory_space=pl.ANY)],
        out_specs=row(d),
        out_shape=jax.ShapeDtypeStruct((t, d), F32),
        scratch_shapes=[pltpu.VMEM((TOP_K, tm * TOKEN_ROWS, LANES), U32), pltpu.SemaphoreType.DMA(())],
        compiler_params=_params(("arbitrary",)),
        name="combine",
    )(pos_tiles, gates_t, h2, x1, gate_f.reshape(b, 1, d),
      w_gate_s.astype(BF16), w_up_s.astype(BF16), w_down_s.astype(BF16), ys)


def _moe_sublayer(x1, h2, h2t, gate_f, w_router, router_bias, w_gate, w_up, w_down, w_gate_s, w_up_s, w_down_s, b, s,
                  tm=256):
    t = b * s
    eidx, rank, gates, counts = _router(h2, w_router, router_bias)
    pos_tiles = _positions(counts, eidx, rank, tm)
    xs = _dispatch(h2t, pos_tiles, tm)
    ys = _experts(xs, counts[:, 0], w_gate, w_up, w_down)
    return _combine(ys, pos_tiles, gates.T, h2, x1, gate_f, w_gate_s, w_up_s, w_down_s, b, s, tm)


def kernel(x, c, w_ada, b_ada, norm_mix_gain, w_in, q_norm_gain, k_norm_gain, rel_bias_table, conv_w, conv_b, dt_bias,
           a_log, d_skip, ssm_norm_gain, w_out, norm_ffn_gain, w_router, router_bias, w_gate_experts, w_up_experts,
           w_down_experts, w_gate_shared, w_up_shared, w_down_shared):
    b, s, d = x.shape
    for layer in range(w_ada.shape[0]):
        mod = _adaln(c, w_ada[layer], b_ada[layer])
        x1, h2, h2t = _mixer_sublayer(x, mod, norm_mix_gain[layer], w_in[layer], q_norm_gain[layer], k_norm_gain[layer],
                                 rel_bias_table, conv_w[layer], conv_b[layer], dt_bias[layer], a_log[layer],
                                 d_skip[layer], ssm_norm_gain[layer], w_out[layer], norm_ffn_gain[layer])
        gate_f = mod[:, 5 * d:]
        out = _moe_sublayer(x1, h2, h2t, gate_f, w_router[layer], router_bias[layer], w_gate_experts[layer],
                            w_up_experts[layer], w_down_experts[layer], w_gate_shared[layer], w_up_shared[layer],
                            w_down_shared[layer], b, s)
        x = out.reshape(b, s, d)
    return x
```

```python
import functools
import math

import numpy as np
import jax
import jax.numpy as jnp
from jax import lax
from jax.experimental import pallas as pl
from jax.experimental.pallas import tpu as pltpu

F32 = jnp.float32
BF16 = jnp.bfloat16
I32 = jnp.int32

D_MODEL = 1024
ATTN_HEADS = 8
HEAD_DIM = 64
ATTN_WIDTH = ATTN_HEADS * HEAD_DIM
PATTERNS = ((128, 1), (512, 4), (2048, 16))
WIN_STEPS = 128
REL_BUCKETS = 32
REL_MAX_DISTANCE = 2048
SSM_HEADS = 24
SSM_HEAD_DIM = 64
SSM_WIDTH = SSM_HEADS * SSM_HEAD_DIM
SSM_GROUPS = 4
HEADS_PER_GROUP = SSM_HEADS // SSM_GROUPS
GROUP_WIDTH = SSM_WIDTH // SSM_GROUPS
SSM_STATE = 128
SSM_CONV = 4
SSM_CHUNK = 128
CONV_CH = SSM_WIDTH + 2 * SSM_GROUPS * SSM_STATE
N_EXPERTS = 256
TOP_K = 8
N_EXPERT_GROUPS = 8
EXPERTS_PER_GROUP = N_EXPERTS // N_EXPERT_GROUPS
TOPK_GROUPS = 4
EXPERT_FF = 256
ROUTED_SCALE = 2.5
NORM_EPS = 1e-6

LANES = 128
SUBLANES = 8
NEG_BIG = -1e30
VMEM_LIMIT = 56 * 1024 * 1024


def _params(sem, vmem=VMEM_LIMIT):
    return pltpu.CompilerParams(dimension_semantics=sem, vmem_limit_bytes=vmem)


def _sigmoid(x):
    return 1.0 / (1.0 + jnp.exp(-x))


def _silu(x):
    return x * _sigmoid(x)


def _split3(x):
    hi = x.astype(BF16)
    r = x - hi.astype(F32)
    mid = r.astype(BF16)
    lo = (r - mid.astype(F32)).astype(BF16)
    return hi, mid, lo


def _dot(a, b):
    return jnp.dot(a, b, preferred_element_type=F32)


def _dot_nt(a, b):
    return lax.dot_general(a, b, (((1,), (1,)), ((), ())), preferred_element_type=F32)


def _dot_exact_rhs(a, b_exact):
    hi, mid, lo = _split3(a)
    return _dot(hi, b_exact) + _dot(mid, b_exact) + _dot(lo, b_exact)


def _dot_exact_lhs(a_exact, b):
    hi, mid, lo = _split3(b)
    return _dot(a_exact, hi) + _dot(a_exact, mid) + _dot(a_exact, lo)


def _adaln_kernel(c_ref, w_ref, b_ref, o_ref):
    s = _silu(c_ref[...]).astype(BF16)
    o_ref[...] = _dot(s, w_ref[...].astype(BF16)) + b_ref[...]


def _adaln(c, w_ada, b_ada):
    b, d = c.shape
    n = w_ada.shape[1]
    rows = SUBLANES
    c_pad = jnp.zeros((rows, d), F32).at[:b].set(c)
    tn = 1024
    out = pl.pallas_call(
        _adaln_kernel,
        grid=(n // tn,),
        in_specs=[pl.BlockSpec((rows, d), lambda j: (0, 0)),
                  pl.BlockSpec((d, tn), lambda j: (0, j)),
                  pl.BlockSpec((1, tn), lambda j: (0, j))],
        out_specs=pl.BlockSpec((rows, tn), lambda j: (0, j)),
        out_shape=jax.ShapeDtypeStruct((rows, n), F32),
        compiler_params=_params(("arbitrary",)),
        name="adaln",
    )(c_pad, w_ada, b_ada.reshape(1, n))
    return out[:b]


def _inproj_kernel(x_ref, shift_ref, scale_ref, g_ref, wqkv_ref, wz_ref, wxbc_ref, wdt_ref,
                   qg_ref, kg_ref, hmean_ref, q_ref, k_ref, v_ref, z_ref, xbc_ref, dt_ref):
    x = x_ref[...]
    ms = jnp.mean(x * x, axis=-1, keepdims=True)
    h = x * lax.rsqrt(ms + NORM_EPS) * g_ref[...]
    h = h * (1.0 + scale_ref[...]) + shift_ref[...]
    hb = h.astype(BF16)

    hmean = hmean_ref[...]

    def head_norm(t, gain):
        ss = _dot_exact_rhs(t * t, hmean)
        return t * lax.rsqrt(ss + NORM_EPS) * gain

    q = _dot(hb, wqkv_ref[:, 0:ATTN_WIDTH])
    q_ref[...] = head_norm(q, qg_ref[...]) * (HEAD_DIM ** -0.5)
    k = _dot(hb, wqkv_ref[:, ATTN_WIDTH:2 * ATTN_WIDTH])
    k_ref[...] = head_norm(k, kg_ref[...])
    v_ref[...] = _dot(hb, wqkv_ref[:, 2 * ATTN_WIDTH:3 * ATTN_WIDTH])
    for c0 in range(0, SSM_WIDTH, 512):
        z_ref[:, c0:c0 + 512] = _dot(hb, wz_ref[:, c0:c0 + 512])
    for c0 in range(0, CONV_CH, 512):
        xbc_ref[:, c0:c0 + 512] = _dot(hb, wxbc_ref[:, c0:c0 + 512])
    dt_ref[...] = _dot(hb, wdt_ref[...])


def _in_proj(x, shift, scale, gain, w_in, q_gain, k_gain, tm=256):
    b, s, d = x.shape
    t = b * s
    tiles_per_seq = s // tm
    w = w_in.astype(BF16)
    o_z = 3 * ATTN_WIDTH
    o_x = o_z + SSM_WIDTH
    o_dt = o_x + CONV_CH
    w_qkv, w_z, w_xbc = w[:, :o_z], w[:, o_z:o_x], w[:, o_x:o_dt]
    w_dt = jnp.zeros((d, LANES), BF16).at[:, :SSM_HEADS].set(w[:, o_dt:])
    head_of = np.arange(ATTN_WIDTH) // HEAD_DIM
    hmean = jnp.asarray((head_of[:, None] == head_of[None, :]).astype(np.float32) / HEAD_DIM, BF16)
    full = lambda shp: pl.BlockSpec(shp, lambda i: (0,) * len(shp))
    row = lambda n: pl.BlockSpec((tm, n), lambda i: (i, 0))
    per_batch = pl.BlockSpec((None, 1, d), lambda i: (i // tiles_per_seq, 0, 0))
    outs = pl.pallas_call(
        _inproj_kernel,
        grid=(t // tm,),
        in_specs=[row(d), per_batch, per_batch, full((1, d)),
                  full((d, o_z)), full((d, SSM_WIDTH)), full((d, CONV_CH)), full((d, LANES)),
                  full((1, ATTN_WIDTH)), full((1, ATTN_WIDTH)), full((ATTN_WIDTH, ATTN_WIDTH))],
        out_specs=[row(ATTN_WIDTH), row(ATTN_WIDTH), row(ATTN_WIDTH), row(SSM_WIDTH), row(CONV_CH), row(LANES)],
        out_shape=[jax.ShapeDtypeStruct((t, n), F32)
                   for n in (ATTN_WIDTH, ATTN_WIDTH, ATTN_WIDTH, SSM_WIDTH, CONV_CH, LANES)],
        compiler_params=_params(("arbitrary",)),
        name="in_proj",
    )(x.reshape(t, d), shift.reshape(b, 1, d), scale.reshape(b, 1, d), gain.reshape(1, d),
      w_qkv, w_z, w_xbc, w_dt,
      jnp.tile(q_gain, ATTN_HEADS).reshape(1, ATTN_WIDTH), jnp.tile(k_gain, ATTN_HEADS).reshape(1, ATTN_WIDTH), hmean)
    return outs


def _t5_causal_buckets(distance):
    n = np.maximum(distance, 0)
    max_exact = REL_BUCKETS // 2
    large = max_exact + (np.log(np.maximum(n, 1) / max_exact) / math.log(REL_MAX_DISTANCE / max_exact)
                         * (REL_BUCKETS - max_exact)).astype(np.int64)
    large = np.minimum(large, REL_BUCKETS - 1)
    return np.where(n < max_exact, n, large).astype(np.int32)


def _window_bias(rel_bias_table, dilation):
    qi = np.arange(WIN_STEPS)[:, None]
    kj = np.arange(2 * WIN_STEPS)[None, :]
    dist = qi + WIN_STEPS - kj
    band = (dist >= 0) & (dist <= WIN_STEPS)
    onehot = (_t5_causal_buckets(dist * dilation).reshape(-1, 1) == np.arange(REL_BUCKETS)[None, :]).astype(np.float32)
    bias = jnp.dot(rel_bias_table.astype(F32).T, jnp.asarray(onehot).T, precision=lax.Precision.HIGHEST)
    bias = bias.reshape(ATTN_HEADS, WIN_STEPS, 2 * WIN_STEPS)
    return jnp.where(jnp.asarray(band)[None], bias, NEG_BIG)


ATTN_TOKENS = max(w for w, _ in PATTERNS)
ATTN_UNROLL = 4


def _attn_kernel(q_ref, kp_ref, kc_ref, vp_ref, vc_ref, bias_ref, out_ref, kw, vw, o_acc, l_acc):
    tb = ATTN_TOKENS
    first = pl.program_id(2) == 0
    kw[0:tb] = kp_ref[...]
    kw[tb:2 * tb] = kc_ref[...]
    vw[0:tb] = vp_ref[...]
    vw[tb:2 * tb] = vc_ref[...]
    lane = lax.broadcasted_iota(I32, (WIN_STEPS, LANES), 1)
    head0 = lane < HEAD_DIM
    col = lax.broadcasted_iota(I32, (WIN_STEPS, 2 * WIN_STEPS), 1)
    in_prev = col < WIN_STEPS

    for p, (_, d) in enumerate(PATTERNS):
        shift = d.bit_length() - 1
        n_blocks = tb // WIN_STEPS

        def rows(start, n, d=d):
            return pl.ds(start, n, stride=d) if d > 1 else pl.ds(start, n)

        def body(it, carry, p=p, d=d, shift=shift, rows=rows):
            for u in range(ATTN_UNROLL):
                idx = it * ATTN_UNROLL + u
                r = jnp.bitwise_and(idx, d - 1)
                j = jnp.right_shift(idx, shift)
                qs = j * (WIN_STEPS * d) + r
                q = q_ref[rows(qs, WIN_STEPS), :]
                k = kw[rows(tb + qs - WIN_STEPS * d, 2 * WIN_STEPS), :].astype(BF16)
                v = vw[rows(tb + qs - WIN_STEPS * d, 2 * WIN_STEPS), :].astype(BF16)
                no_prev = jnp.logical_and(in_prev, jnp.logical_and(first, j == 0))
                o_h, lse_h = [], []
                for h in range(2):
                    qh = jnp.where(head0 if h == 0 else jnp.logical_not(head0), q, 0.0).astype(BF16)
                    s = _dot_nt(qh, k) + bias_ref[p, h]
                    s = jnp.where(no_prev, NEG_BIG, s)
                    m = jnp.max(s, axis=-1, keepdims=True)
                    e = jnp.exp(s - m)
                    denom = jnp.sum(e, axis=-1, keepdims=True)
                    o_h.append(_dot(e.astype(BF16), v) / denom)
                    lse_h.append(m + jnp.log(denom))
                o_acc[p, rows(qs, WIN_STEPS), :] = jnp.where(head0, o_h[0], o_h[1])
                l_acc[p, rows(qs, WIN_STEPS), :] = jnp.where(head0, lse_h[0], lse_h[1])
            return carry

        lax.fori_loop(0, n_blocks // ATTN_UNROLL, body, 0)

    chunk = 256
    for c0 in range(0, tb, chunk):
        l1, l2, l3 = (l_acc[p, c0:c0 + chunk, :] for p in range(3))
        m = jnp.maximum(jnp.maximum(l1, l2), l3)
        e1, e2, e3 = jnp.exp(l1 - m), jnp.exp(l2 - m), jnp.exp(l3 - m)
        num = e1 * o_acc[0, c0:c0 + chunk, :] + e2 * o_acc[1, c0:c0 + chunk, :] + e3 * o_acc[2, c0:c0 + chunk, :]
        out_ref[c0:c0 + chunk, :] = num / (e1 + e2 + e3)


def _attention(q, k, v, bias):
    b, s, w = q.shape
    tb = ATTN_TOKENS
    pairs = ATTN_HEADS // 2
    cur = pl.BlockSpec((None, tb, LANES), lambda bi, hp, i: (bi, i, hp))
    prev = pl.BlockSpec((None, tb, LANES), lambda bi, hp, i: (bi, jnp.maximum(i - 1, 0), hp))
    return pl.pallas_call(
        _attn_kernel,
        grid=(b, pairs, s // tb),
        in_specs=[cur, prev, cur, prev, cur,
                  pl.BlockSpec((len(PATTERNS), 2, WIN_STEPS, 2 * WIN_STEPS), lambda bi, hp, i: (0, hp, 0, 0))],
        out_specs=cur,
        out_shape=jax.ShapeDtypeStruct((b, s, w), F32),
        scratch_shapes=[pltpu.VMEM((2 * tb, LANES), F32), pltpu.VMEM((2 * tb, LANES), F32),
                        pltpu.VMEM((len(PATTERNS), tb, LANES), F32), pltpu.VMEM((len(PATTERNS), tb, LANES), F32)],
        compiler_params=_params(("arbitrary",) * 3),
        name="attention",
    )(q, k, k, v, v, bias)


def _ssd_kernel(xbc_ref, halo_ref, z_ref, dtraw_ref, convw_ref, convb_ref, dtb_ref, alog_ref, dskip_ref, gain_ref,
                expand_ref, tril_ref, y_ref, state_ref):
    c = pl.program_id(1)

    @pl.when(c == 0)
    def _():
        state_ref[...] = jnp.zeros_like(state_ref)

    x = xbc_ref[...]
    halo = jnp.where(c == 0, 0.0, halo_ref[...])
    w = convw_ref[...]
    acc = x * w[SSM_CONV - 1:SSM_CONV, :] + convb_ref[...]
    row8 = lax.broadcasted_iota(I32, (SUBLANES, CONV_CH), 0)
    for shift in range(1, SSM_CONV):
        xs = pltpu.roll(x, shift, axis=0)
        hs = pltpu.roll(halo, shift, axis=0)
        head = jnp.where(row8 < shift, hs, xs[0:SUBLANES])
        xs = jnp.concatenate([head, xs[SUBLANES:]], axis=0)
        acc = acc + xs * w[SSM_CONV - 1 - shift:SSM_CONV - shift, :]
    act = _silu(acc)
    x_s = act[:, :SSM_WIDTH]
    bc0 = SSM_WIDTH
    cc0 = SSM_WIDTH + SSM_GROUPS * SSM_STATE

    t = dtraw_ref[...] + dtb_ref[...]
    dt = jnp.maximum(t, 0.0) + jnp.log(1.0 + jnp.exp(-jnp.abs(t)))
    a = dt * (-jnp.exp(alog_ref[...]))
    a_cs = _dot_exact_lhs(tril_ref[...], a)
    a_cs_t = a_cs.T
    a_last = a_cs[SSM_CHUNK - 1:SSM_CHUNK, :]
    expand = expand_ref[...]
    dt_e = _dot_exact_rhs(dt, expand)
    ea_e = _dot_exact_rhs(jnp.exp(a_cs), expand)
    dte_e = _dot_exact_rhs(jnp.exp(a_last - a_cs), expand)
    xdt = x_s * dt_e
    xw = (xdt * dte_e).astype(BF16)
    xdt_b = xdt.astype(BF16)

    li = lax.broadcasted_iota(I32, (SSM_CHUNK, SSM_CHUNK), 0)
    si = lax.broadcasted_iota(I32, (SSM_CHUNK, SSM_CHUNK), 1)
    causal = li >= si

    ys = []
    for g in range(SSM_GROUPS):
        gs = slice(g * GROUP_WIDTH, (g + 1) * GROUP_WIDTH)
        b_g = act[:, bc0 + g * SSM_STATE:bc0 + (g + 1) * SSM_STATE]
        c_g = act[:, cc0 + g * SSM_STATE:cc0 + (g + 1) * SSM_STATE].astype(BF16)
        cb = _dot_nt(c_g, b_g.astype(BF16))
        state = state_ref[g]
        y_off = _dot(c_g, state.astype(BF16)) * ea_e[:, gs]
        parts = []
        for j in range(HEADS_PER_GROUP):
            hh = g * HEADS_PER_GROUP + j
            seg = a_cs[:, hh:hh + 1] - a_cs_t[hh:hh + 1, :]
            decay = jnp.exp(jnp.where(causal, seg, NEG_BIG))
            m = (cb * decay).astype(BF16)
            parts.append(_dot(m, xdt_b[:, hh * SSM_HEAD_DIM:(hh + 1) * SSM_HEAD_DIM]))
        ys.append(jnp.concatenate(parts, axis=1) + y_off)
        state_ref[g] = state * ea_e[SSM_CHUNK - 1:SSM_CHUNK, gs] + _dot(b_g.T.astype(BF16), xw[:, gs])
    y = jnp.concatenate(ys, axis=1) + dskip_ref[...] * x_s
    y = y * _silu(z_ref[...])
    gain = gain_ref[...]
    for g in range(SSM_GROUPS):
        gs = slice(g * GROUP_WIDTH, (g + 1) * GROUP_WIDTH)
        yg = y[:, gs]
        ms = jnp.mean(yg * yg, axis=-1, keepdims=True)
        y_ref[:, gs] = yg * lax.rsqrt(ms + NORM_EPS) * gain[:, gs]


def _ssd(xbc, z, dt_raw, conv_w, conv_b, dt_bias, a_log, d_skip, norm_gain, b, s):
    t = b * s
    nc = s // SSM_CHUNK
    pad_heads = lambda v: jnp.zeros((1, LANES), F32).at[0, :SSM_HEADS].set(v)
    head_of_lane = np.arange(SSM_WIDTH) // SSM_HEAD_DIM
    expand = jnp.asarray((np.arange(LANES)[:, None] == head_of_lane[None, :]).astype(np.float32), BF16)
    tril = jnp.asarray(np.tril(np.ones((SSM_CHUNK, SSM_CHUNK), np.float32)), BF16)
    halo_blocks = SSM_CHUNK // SUBLANES
    chunk = lambda n: pl.BlockSpec((SSM_CHUNK, n), lambda bi, c: (bi * nc + c, 0))
    full = lambda shp: pl.BlockSpec(shp, lambda bi, c: (0,) * len(shp))
    halo = pl.BlockSpec((SUBLANES, CONV_CH), lambda bi, c: (jnp.maximum((bi * nc + c) * halo_blocks - 1, 0), 0))
    return pl.pallas_call(
        _ssd_kernel,
        grid=(b, nc),
        in_specs=[chunk(CONV_CH), halo, chunk(SSM_WIDTH), chunk(LANES),
                  full((SSM_CONV, CONV_CH)), full((1, CONV_CH)), full((1, LANES)), full((1, LANES)),
                  full((1, SSM_WIDTH)), full((1, SSM_WIDTH)), full((LANES, SSM_WIDTH)), full((SSM_CHUNK, SSM_CHUNK))],
        out_specs=chunk(SSM_WIDTH),
        out_shape=jax.ShapeDtypeStruct((t, SSM_WIDTH), F32),
        scratch_shapes=[pltpu.VMEM((SSM_GROUPS, SSM_STATE, GROUP_WIDTH), F32)],
        compiler_params=_params(("arbitrary", "arbitrary")),
        name="ssd",
    )(xbc, xbc, z, dt_raw, conv_w, conv_b.reshape(1, CONV_CH), pad_heads(dt_bias), pad_heads(a_log),
      jnp.repeat(d_skip, SSM_HEAD_DIM).reshape(1, SSM_WIDTH), norm_gain.reshape(1, SSM_WIDTH), expand, tril)


U32 = jnp.uint32
TOKEN_ROWS = D_MODEL // (2 * LANES)
HIGH_HALF = np.uint32(0xFFFF0000)


def _to_token_tiles(ref, x):
    n = x.shape[0]
    for c in range(TOKEN_ROWS):
        lo = lax.bitcast_convert_type(x[:, c * LANES:(c + 1) * LANES].astype(BF16).astype(F32), U32)
        hi = lax.bitcast_convert_type(x[:, (c + TOKEN_ROWS) * LANES:(c + TOKEN_ROWS + 1) * LANES]
                                      .astype(BF16).astype(F32), U32)
        ref[pl.ds(c, n, stride=TOKEN_ROWS), :] = jnp.bitwise_or(jnp.right_shift(lo, 16), jnp.bitwise_and(hi, HIGH_HALF))


def _from_token_tiles(ref, n, token0=0):
    lows, highs = [], []
    for c in range(TOKEN_ROWS):
        word = ref[pl.ds(token0 * TOKEN_ROWS + c, n, stride=TOKEN_ROWS), :]
        lows.append(lax.bitcast_convert_type(jnp.left_shift(word, 16), F32))
        highs.append(lax.bitcast_convert_type(jnp.bitwise_and(word, HIGH_HALF), F32))
    return jnp.concatenate(lows + highs, axis=1)


def _outproj_kernel(attn_ref, ssm_ref, x_ref, gate_ref, shift_ref, scale_ref, g_ref, wa_ref, ws_ref, x1_ref, h2_ref,
                    h2t_ref):
    mixed = _dot(attn_ref[...].astype(BF16), wa_ref[...]) + _dot(ssm_ref[...].astype(BF16), ws_ref[...])
    x1 = x_ref[...] + gate_ref[...] * mixed
    x1_ref[...] = x1
    ms = jnp.mean(x1 * x1, axis=-1, keepdims=True)
    h = x1 * lax.rsqrt(ms + NORM_EPS) * g_ref[...]
    h2 = h * (1.0 + scale_ref[...]) + shift_ref[...]
    h2_ref[...] = h2
    _to_token_tiles(h2t_ref, h2)


def _out_proj(attn, ssm, x, gate, shift, scale, gain, w_out, b, s, tm=256):
    t = b * s
    d = D_MODEL
    tiles_per_seq = s // tm
    w = w_out.astype(BF16)
    row = lambda n: pl.BlockSpec((tm, n), lambda i: (i, 0))
    full = lambda shp: pl.BlockSpec(shp, lambda i: (0,) * len(shp))
    per_batch = pl.BlockSpec((None, 1, d), lambda i: (i // tiles_per_seq, 0, 0))
    return pl.pallas_call(
        _outproj_kernel,
        grid=(t // tm,),
        in_specs=[row(ATTN_WIDTH), row(SSM_WIDTH), row(d), per_batch, per_batch, per_batch, full((1, d)),
                  full((ATTN_WIDTH, d)), full((SSM_WIDTH, d))],
        out_specs=[row(d), row(d), pl.BlockSpec((tm * TOKEN_ROWS, LANES), lambda i: (i, 0))],
        out_shape=[jax.ShapeDtypeStruct((t, d), F32)] * 2 + [jax.ShapeDtypeStruct((t * TOKEN_ROWS, LANES), U32)],
        compiler_params=_params(("arbitrary",)),
        name="out_proj",
    )(attn.reshape(t, ATTN_WIDTH), ssm, x.reshape(t, d),
      gate.reshape(b, 1, d), shift.reshape(b, 1, d), scale.reshape(b, 1, d), gain.reshape(1, d),
      w[:ATTN_WIDTH], w[ATTN_WIDTH:])


def _mixer_sublayer(x, mod, norm_mix_gain, w_in, q_norm_gain, k_norm_gain, rel_bias_table, conv_w, conv_b, dt_bias,
                    a_log, d_skip, ssm_norm_gain, w_out, norm_ffn_gain):
    b, s, d = x.shape
    shift_m, scale_m, gate_m, shift_f, scale_f, _ = jnp.split(mod, 6, axis=-1)
    q, k, v, z, xbc, dt_raw = _in_proj(x, shift_m, scale_m, norm_mix_gain, w_in, q_norm_gain, k_norm_gain)
    bias = jnp.stack([_window_bias(rel_bias_table, dilation) for _, dilation in PATTERNS])
    attn = _attention(q.reshape(b, s, ATTN_WIDTH), k.reshape(b, s, ATTN_WIDTH), v.reshape(b, s, ATTN_WIDTH), bias)
    ssm = _ssd(xbc, z, dt_raw, conv_w, conv_b, dt_bias, a_log, d_skip, ssm_norm_gain, b, s)
    return _out_proj(attn, ssm, x, gate_m, shift_f, scale_f, norm_ffn_gain, w_out, b, s)


def _first_argmax(v, iota, limit):
    m = jnp.max(v, axis=0, keepdims=True)
    idx = jnp.min(jnp.where(v == m, iota, limit), axis=0, keepdims=True)
    return m, idx


def _router_kernel(h_ref, wt_ref, bias_ref, upper_ref, eidx_ref, rank_ref, gate_ref, counts_ref, before_ref,
                   carry_ref, carry_row_ref):
    @pl.when(pl.program_id(0) == 0)
    def _():
        carry_ref[...] = jnp.zeros_like(carry_ref)
        carry_row_ref[...] = jnp.zeros_like(carry_row_ref)

    tm = h_ref.shape[0]
    h = h_ref[...]
    wt = wt_ref[...]
    h_hi = h.astype(BF16)
    h_lo = (h - h_hi.astype(F32)).astype(BF16)
    w_hi = wt.astype(BF16)
    w_lo = (wt - w_hi.astype(F32)).astype(BF16)
    logits = _dot_nt(w_hi, h_hi) + _dot_nt(w_hi, h_lo) + _dot_nt(w_lo, h_hi)
    scores = _sigmoid(logits)
    choice = scores + bias_ref[...]
    neg_inf = -jnp.inf

    iota_g = lax.broadcasted_iota(I32, (EXPERTS_PER_GROUP, tm), 0).astype(F32)
    group_rows = []
    for g in range(N_EXPERT_GROUPS):
        v = choice[g * EXPERTS_PER_GROUP:(g + 1) * EXPERTS_PER_GROUP]
        m1, i1 = _first_argmax(v, iota_g, float(EXPERTS_PER_GROUP))
        m2 = jnp.max(jnp.where(iota_g == i1, neg_inf, v), axis=0, keepdims=True)
        group_rows.append(m1 + m2)
    group_scores = jnp.concatenate(group_rows, axis=0)

    iota_n = lax.broadcasted_iota(I32, (N_EXPERT_GROUPS, tm), 0).astype(F32)
    chosen = jnp.zeros((N_EXPERT_GROUPS, tm), F32)
    for _ in range(TOPK_GROUPS):
        _, gi = _first_argmax(group_scores, iota_n, float(N_EXPERT_GROUPS))
        hit = iota_n == gi
        chosen = jnp.where(hit, 1.0, chosen)
        group_scores = jnp.where(hit, neg_inf, group_scores)

    masked = jnp.concatenate(
        [jnp.where(chosen[g:g + 1] > 0.0, choice[g * EXPERTS_PER_GROUP:(g + 1) * EXPERTS_PER_GROUP], neg_inf)
         for g in range(N_EXPERT_GROUPS)], axis=0)

    iota_e = lax.broadcasted_iota(I32, (N_EXPERTS, tm), 0).astype(F32)
    picked, gates = [], []
    onehot = jnp.zeros((N_EXPERTS, tm), F32)
    for _ in range(TOP_K):
        _, ei = _first_argmax(masked, iota_e, float(N_EXPERTS))
        hit = iota_e == ei
        gates.append(jnp.sum(jnp.where(hit, scores, 0.0), axis=0, keepdims=True))
        masked = jnp.where(hit, neg_inf, masked)
        onehot = jnp.where(hit, 1.0, onehot)
        picked.append(ei)
    gate_sum = gates[0]
    for gk in gates[1:]:
        gate_sum = gate_sum + gk

    base = _dot(onehot.astype(BF16), upper_ref[...]) + carry_ref[...]
    ranks = [jnp.sum(jnp.where(iota_e == ei, base, 0.0), axis=0, keepdims=True) for ei in picked]
    carry_ref[...] = carry_ref[...] + jnp.sum(onehot, axis=1, keepdims=True)
    before_ref[0] = carry_row_ref[...].astype(I32)
    carry_row_ref[...] = carry_row_ref[...] + _dot_nt(jnp.ones((SUBLANES, tm), BF16), onehot.astype(BF16))

    eidx_ref[...] = jnp.concatenate(picked, axis=0).astype(I32)
    rank_ref[...] = jnp.concatenate(ranks, axis=0).astype(I32)
    gate_ref[...] = jnp.concatenate([gk / gate_sum * ROUTED_SCALE for gk in gates], axis=0)
    counts_ref[...] = carry_ref[...].astype(I32)


def _router(h2, w_router, router_bias, tm=256):
    t, d = h2.shape
    upper = jnp.asarray(np.triu(np.ones((tm, tm), np.float32), 1), BF16)
    tok = pl.BlockSpec((TOP_K, tm), lambda i: (0, i))
    full = lambda shp: pl.BlockSpec(shp, lambda i: (0,) * len(shp))
    return pl.pallas_call(
        _router_kernel,
        grid=(t // tm,),
        in_specs=[pl.BlockSpec((tm, d), lambda i: (i, 0)), full((N_EXPERTS, d)), full((N_EXPERTS, 1)), full((tm, tm))],
        out_specs=[tok, tok, tok, full((N_EXPERTS, 1)), pl.BlockSpec((1, SUBLANES, N_EXPERTS), lambda i: (i, 0, 0))],
        out_shape=[jax.ShapeDtypeStruct((TOP_K, t), I32), jax.ShapeDtypeStruct((TOP_K, t), I32),
                   jax.ShapeDtypeStruct((TOP_K, t), F32), jax.ShapeDtypeStruct((N_EXPERTS, 1), I32),
                   jax.ShapeDtypeStruct((t // tm, SUBLANES, N_EXPERTS), I32)],
        scratch_shapes=[pltpu.VMEM((N_EXPERTS, 1), F32), pltpu.VMEM((SUBLANES, N_EXPERTS), F32)],
        compiler_params=_params(("arbitrary",)),
        name="router",
    )(h2, w_router.T, router_bias.reshape(N_EXPERTS, 1), upper)


def _positions_kernel(counts_ref, lower_ref, adj_ref, eidx_ref, rank_ref, pos_ref, local_ref):
    tm = eidx_ref.shape[1]
    counts = jnp.broadcast_to(counts_ref[...].astype(F32), (N_EXPERTS, LANES))
    offsets = _dot_exact_lhs(lower_ref[...], counts)[:, 0:1]
    adjust = adj_ref[...].astype(F32)
    iota_e = lax.broadcasted_iota(I32, (N_EXPERTS, tm), 0).astype(F32)
    e = eidx_ref[...].astype(F32)
    hits = [iota_e == e[k:k + 1] for k in range(TOP_K)]
    rows = [jnp.sum(jnp.where(hit, offsets, 0.0), axis=0, keepdims=True) for hit in hits]
    pos_ref[0] = jnp.concatenate(rows, axis=0).astype(I32) + rank_ref[...]
    rows = [jnp.sum(jnp.where(hit, adjust, 0.0), axis=0, keepdims=True) for hit in hits]
    local_ref[...] = jnp.concatenate(rows, axis=0).astype(I32) + rank_ref[...]


def _positions(counts, adjust, eidx, rank, tm, combine_tm):
    t = eidx.shape[1]
    lower = jnp.asarray(np.tril(np.ones((N_EXPERTS, N_EXPERTS), np.float32), -1), BF16)
    tok = pl.BlockSpec((TOP_K, tm), lambda i: (0, i))
    return pl.pallas_call(
        _positions_kernel,
        grid=(t // tm,),
        in_specs=[pl.BlockSpec((N_EXPERTS, 1), lambda i: (0, 0)), pl.BlockSpec((N_EXPERTS, N_EXPERTS), lambda i: (0, 0)),
                  pl.BlockSpec((None, N_EXPERTS, 1), lambda i: (i // (combine_tm // tm), 0, 0)), tok, tok],
        out_specs=[pl.BlockSpec((1, TOP_K, tm), lambda i: (i, 0, 0)), tok],
        out_shape=[jax.ShapeDtypeStruct((t // tm, TOP_K, tm), I32), jax.ShapeDtypeStruct((TOP_K, t), I32)],
        compiler_params=_params(("arbitrary",)),
        name="positions",
    )(counts, lower, adjust, eidx, rank)


def _token_rows(i):
    return pl.ds(pl.multiple_of(i * TOKEN_ROWS, TOKEN_ROWS), TOKEN_ROWS)


def _dispatch_kernel(pos_ref, h_ref, xs_ref, sem):
    tm = h_ref.shape[0] // TOKEN_ROWS

    def row_copy(t, k):
        return pltpu.make_async_copy(h_ref.at[_token_rows(t)], xs_ref.at[_token_rows(pos_ref[0, k, t])], sem)

    def start(t, carry):
        for k in range(TOP_K):
            row_copy(t, k).start(priority=k % 2)
        return carry

    def wait(t, carry):
        for k in range(TOP_K):
            row_copy(t, k).wait()
        return carry

    lax.fori_loop(0, tm, start, 0)
    lax.fori_loop(0, tm, wait, 0)


def _dispatch(h2t, pos_tiles, tm):
    t = h2t.shape[0] // TOKEN_ROWS
    return pl.pallas_call(
        _dispatch_kernel,
        grid=(t // tm,),
        in_specs=[pl.BlockSpec((1, TOP_K, tm), lambda i: (i, 0, 0), memory_space=pltpu.SMEM),
                  pl.BlockSpec((tm * TOKEN_ROWS, LANES), lambda i: (i, 0))],
        out_specs=pl.BlockSpec(memory_space=pl.ANY),
        out_shape=jax.ShapeDtypeStruct(((t * TOP_K + EXPERT_BLOCK) * TOKEN_ROWS, LANES), h2t.dtype),
        scratch_shapes=[pltpu.SemaphoreType.DMA(())],
        compiler_params=_params(("arbitrary",)),
        name="dispatch",
    )(pos_tiles, h2t)


EXPERT_BLOCK = 256
TAIL_PIECES = tuple(1 << i for i in reversed(range(EXPERT_BLOCK.bit_length() - 1)))
X_SLOTS = 4
Y_SLOTS = 4


def _experts_kernel(start_ref, count_ref, nxt_ref, slot_ref, first_ref, blk0_ref, full0_ref, ptail_ref, ltail_ref,
                    blktok_ref, nblocks_ref, xs_hbm, wg_hbm, wu_hbm, wd_hbm, ys_hbm,
                    wg_buf, wu_buf, wd_buf, wg_bf, wu_bf, wd_bf, xbuf, ybuf, ytail, wsem, xsem, ysem, tsem):
    e = pl.program_id(0)
    last_step = e == pl.num_programs(0) - 1
    start, count = start_ref[e], count_ref[e]
    n_full = jnp.right_shift(count, EXPERT_BLOCK.bit_length() - 1)
    tail = jnp.bitwise_and(count, EXPERT_BLOCK - 1)
    n_blk = n_full + (tail > 0).astype(I32)
    blk0, full0 = blk0_ref[e], full0_ref[e]
    slot, nxt = slot_ref[e], nxt_ref[e]

    def token_rows(token, n):
        return pl.ds(pl.multiple_of(token * TOKEN_ROWS, TOKEN_ROWS), n * TOKEN_ROWS)

    def fetch(ex, s):
        return (pltpu.make_async_copy(wg_hbm.at[ex], wg_buf.at[s], wsem.at[s, 0]),
                pltpu.make_async_copy(wu_hbm.at[ex], wu_buf.at[s], wsem.at[s, 1]),
                pltpu.make_async_copy(wd_hbm.at[ex], wd_buf.at[s], wsem.at[s, 2]))

    def x_copy(token, s):
        return pltpu.make_async_copy(xs_hbm.at[token_rows(token, EXPERT_BLOCK)], xbuf.at[s], xsem.at[s])

    def y_copy(token, s):
        return pltpu.make_async_copy(ybuf.at[s], ys_hbm.at[token_rows(token, EXPERT_BLOCK)], ysem.at[s])

    def tail_copies(token, length):
        out = []
        for piece in TAIL_PIECES:
            bigger = (EXPERT_BLOCK - 1) & ~(2 * piece - 1)
            done = jnp.bitwise_and(length, bigger)
            cp = pltpu.make_async_copy(ytail.at[token_rows(done, piece)], ys_hbm.at[token_rows(token + done, piece)],
                                       tsem)
            out.append((jnp.bitwise_and(length, piece) != 0, cp))
        return out

    def block(s):
        x = _from_token_tiles(xbuf.at[s], EXPERT_BLOCK).astype(BF16)
        g = _dot(x, wg_bf[...])
        u = _dot(x, wu_bf[...])
        return _dot((_silu(g) * u).astype(BF16), wd_bf[...])

    @pl.when(count > 0)
    def _():
        @pl.when(first_ref[e] == 1)
        def _():
            for g in range(X_SLOTS - 1):
                @pl.when(g < nblocks_ref[0])
                def _(g=g):
                    x_copy(blktok_ref[g], g).start()
            for cp in fetch(e, slot):
                cp.start()

        for cp in fetch(e, slot):
            cp.wait()

        @pl.when(nxt >= 0)
        def _():
            for cp in fetch(nxt, 1 - slot):
                cp.start()

        wg_bf[...] = wg_buf[slot].astype(BF16)
        wu_bf[...] = wu_buf[slot].astype(BF16)
        wd_bf[...] = wd_buf[slot].astype(BF16)

        def step(i):
            g = blk0 + i
            xs_slot = jnp.bitwise_and(g, X_SLOTS - 1)
            x_copy(start, xs_slot).wait()
            ahead = g + (X_SLOTS - 1)

            @pl.when(ahead < nblocks_ref[0])
            def _():
                x_copy(blktok_ref[ahead], jnp.bitwise_and(ahead, X_SLOTS - 1)).start()

            return block(xs_slot)

        def full_block(i, carry):
            y = step(i)
            j = full0 + i
            ys_slot = jnp.bitwise_and(j, Y_SLOTS - 1)

            @pl.when(j >= Y_SLOTS)
            def _():
                y_copy(start, ys_slot).wait()

            _to_token_tiles(ybuf.at[ys_slot], y)
            y_copy(start + i * EXPERT_BLOCK, ys_slot).start()
            return carry

        lax.fori_loop(0, n_full, full_block, 0)

        @pl.when(tail > 0)
        def _():
            y = step(n_full)
            for pred, cp in tail_copies(start, ptail_ref[e]):
                @pl.when(pred)
                def _(cp=cp):
                    cp.wait()
            _to_token_tiles(ytail, y)
            for pred, cp in tail_copies(start + n_full * EXPERT_BLOCK, tail):
                @pl.when(pred)
                def _(cp=cp):
                    cp.start()

    @pl.when(last_step)
    def _():
        total_full = full0 + n_full
        for back in range(1, Y_SLOTS + 1):
            @pl.when(total_full >= back)
            def _(back=back):
                y_copy(0, jnp.bitwise_and(total_full - back, Y_SLOTS - 1)).wait()
        for pred, cp in tail_copies(0, ltail_ref[0]):
            @pl.when(pred)
            def _(cp=cp):
                cp.wait()


def _max_expert_blocks(n_rows):
    return n_rows // EXPERT_BLOCK + N_EXPERTS


def _expert_metadata(counts, n_rows):
    ids = jnp.arange(N_EXPERTS, dtype=I32)
    used = counts > 0
    starts = jnp.cumsum(counts) - counts
    n_blk = (counts + EXPERT_BLOCK - 1) // EXPERT_BLOCK
    n_full = counts // EXPERT_BLOCK
    tail = counts % EXPERT_BLOCK
    blk0 = jnp.cumsum(n_blk) - n_blk
    full0 = jnp.cumsum(n_full) - n_full
    next_used = lax.cummin(jnp.where(used, ids, N_EXPERTS), reverse=True)
    next_after = jnp.concatenate([next_used[1:], jnp.full((1,), N_EXPERTS, I32)])
    nxt = jnp.where(next_after < N_EXPERTS, next_after, -1)
    ordinal = jnp.cumsum(used.astype(I32)) - 1
    slot = ordinal % 2
    first = jnp.logical_and(used, ordinal == 0)
    latest = lax.cummax(jnp.where(tail > 0, ids, -1))
    before = jnp.concatenate([jnp.full((1,), -1, I32), latest[:-1]])
    ptail = jnp.where(before >= 0, tail[jnp.maximum(before, 0)], 0)
    ltail = jnp.where(latest[-1] >= 0, tail[jnp.maximum(latest[-1], 0)], 0).reshape(1)
    block_ends = jnp.cumsum(n_blk)
    g = jnp.arange(_max_expert_blocks(n_rows), dtype=I32)
    eg = jnp.minimum(jnp.searchsorted(block_ends, g, side="right"), N_EXPERTS - 1)
    blktok = jnp.where(g < block_ends[-1], starts[eg] + (g - blk0[eg]) * EXPERT_BLOCK, 0)
    return tuple(v.astype(I32) for v in (starts, counts, nxt, slot, first, blk0, full0, ptail, ltail, blktok,
                                         block_ends[-1:]))


def _experts(xs, counts, w_gate, w_up, w_down):
    d = D_MODEL
    meta = _expert_metadata(counts, xs.shape[0] // TOKEN_ROWS - EXPERT_BLOCK)
    hbm = pl.BlockSpec(memory_space=pl.ANY)
    blk = (EXPERT_BLOCK * TOKEN_ROWS, LANES)
    grid_spec = pltpu.PrefetchScalarGridSpec(
        num_scalar_prefetch=len(meta),
        grid=(N_EXPERTS,),
        in_specs=[hbm, hbm, hbm, hbm],
        out_specs=hbm,
        scratch_shapes=[pltpu.VMEM((2, d, EXPERT_FF), F32), pltpu.VMEM((2, d, EXPERT_FF), F32),
                        pltpu.VMEM((2, EXPERT_FF, d), F32),
                        pltpu.VMEM((d, EXPERT_FF), BF16), pltpu.VMEM((d, EXPERT_FF), BF16),
                        pltpu.VMEM((EXPERT_FF, d), BF16),
                        pltpu.VMEM((X_SLOTS,) + blk, U32), pltpu.VMEM((Y_SLOTS,) + blk, U32), pltpu.VMEM(blk, U32),
                        pltpu.SemaphoreType.DMA((2, 3)), pltpu.SemaphoreType.DMA((X_SLOTS,)),
                        pltpu.SemaphoreType.DMA((Y_SLOTS,)), pltpu.SemaphoreType.DMA(())],
    )
    return pl.pallas_call(
        _experts_kernel,
        grid_spec=grid_spec,
        out_shape=jax.ShapeDtypeStruct(xs.shape, U32),
        compiler_params=_params(("arbitrary",)),
        name="experts",
    )(*meta, xs, w_gate, w_up, w_down)


COMBINE_TOKENS = 512
RUN_CHUNK = 8


def _combine_kernel(src_ref, nchunk_ref, dst_ref, total_ref, local_ref, gates_ref, h_ref, x1_ref, gatef_ref,
                    wg_ref, wu_ref, wd_ref, ys_hbm, out_ref, buf, acc_lo, acc_hi, sem):
    i = pl.program_id(0)
    tm = h_ref.shape[0]

    def rows(token, n):
        return pl.ds(pl.multiple_of(token * TOKEN_ROWS, TOKEN_ROWS), n * TOKEN_ROWS)

    def chunk_copy(src_token, dst_token):
        return pltpu.make_async_copy(ys_hbm.at[rows(src_token, RUN_CHUNK)], buf.at[rows(dst_token, RUN_CHUNK)], sem)

    def issue(e, carry):
        src, dst = src_ref[i, e], dst_ref[i, e]

        def one(c, inner):
            chunk_copy(src + c * RUN_CHUNK, dst + c * RUN_CHUNK).start()
            return inner

        lax.fori_loop(0, nchunk_ref[i, e], one, 0)
        return carry

    lax.fori_loop(0, N_EXPERTS, issue, 0)

    hb = h_ref[...].astype(BF16)
    shared = _dot((_silu(_dot(hb, wg_ref[...])) * _dot(hb, wu_ref[...])).astype(BF16), wd_ref[...])

    def drain(c, carry):
        chunk_copy(0, 0).wait()
        return carry

    lax.fori_loop(0, total_ref[i], drain, 0)

    def token(t, carry):
        lo = jnp.zeros((TOKEN_ROWS, LANES), F32)
        hi = jnp.zeros((TOKEN_ROWS, LANES), F32)
        for k in range(TOP_K):
            word = buf[rows(local_ref[k, t], 1), :]
            gate = gates_ref[k, t]
            lo = lo + gate * lax.bitcast_convert_type(jnp.left_shift(word, 16), F32)
            hi = hi + gate * lax.bitcast_convert_type(jnp.bitwise_and(word, HIGH_HALF), F32)
        acc_lo[rows(t, 1), :] = lo
        acc_hi[rows(t, 1), :] = hi
        return carry

    lax.fori_loop(0, tm, token, 0)
    routed = jnp.concatenate([acc[pl.ds(c, tm, stride=TOKEN_ROWS), :]
                              for acc in (acc_lo, acc_hi) for c in range(TOKEN_ROWS)], axis=1)
    out_ref[...] = x1_ref[...] + gatef_ref[...] * (shared + routed)


def _combine_tables(counts, before_tiles, router_tm):
    step = COMBINE_TOKENS // router_tm
    before = before_tiles[::step, 0, :]
    after = jnp.concatenate([before[1:], counts[None, :]], axis=0)
    nchunk = (after - before + RUN_CHUNK - 1) // RUN_CHUNK
    padded = nchunk * RUN_CHUNK
    dst = jnp.cumsum(padded, axis=1) - padded
    src = (jnp.cumsum(counts) - counts)[None, :] + before
    return src, nchunk, dst, jnp.sum(nchunk, axis=1), (dst - before)[..., None]


def _combine(ys, tables, local, gates, h2, x1, gate_f, w_gate_s, w_up_s, w_down_s, b, s):
    t, d = h2.shape
    tm = COMBINE_TOKENS
    tiles_per_seq = s // tm
    buf_tokens = tm * TOP_K + N_EXPERTS * (RUN_CHUNK - 1)
    buf_tokens += -buf_tokens % RUN_CHUNK
    row = lambda n: pl.BlockSpec((tm, n), lambda i, *_: (i, 0))
    full = lambda shp: pl.BlockSpec(shp, lambda i, *_: (0,) * len(shp))
    smem = pl.BlockSpec((TOP_K, tm), lambda i, *_: (0, i), memory_space=pltpu.SMEM)
    grid_spec = pltpu.PrefetchScalarGridSpec(
        num_scalar_prefetch=4,
        grid=(t // tm,),
        in_specs=[smem, smem, row(d), row(d),
                  pl.BlockSpec((None, 1, d), lambda i, *_: (i // tiles_per_seq, 0, 0)),
                  full((d, EXPERT_FF)), full((d, EXPERT_FF)), full((EXPERT_FF, d)),
                  pl.BlockSpec(memory_space=pl.ANY)],
        out_specs=row(d),
        scratch_shapes=[pltpu.VMEM((buf_tokens * TOKEN_ROWS, LANES), U32),
                        pltpu.VMEM((tm * TOKEN_ROWS, LANES), F32), pltpu.VMEM((tm * TOKEN_ROWS, LANES), F32),
                        pltpu.SemaphoreType.DMA(())],
    )
    return pl.pallas_call(
        _combine_kernel,
        grid_spec=grid_spec,
        out_shape=jax.ShapeDtypeStruct((t, d), F32),
        compiler_params=_params(("arbitrary",)),
        name="combine",
    )(*tables, local, gates, h2, x1, gate_f.reshape(b, 1, d),
      w_gate_s.astype(BF16), w_up_s.astype(BF16), w_down_s.astype(BF16), ys)


def _moe_sublayer(x1, h2, h2t, gate_f, w_router, router_bias, w_gate, w_up, w_down, w_gate_s, w_up_s, w_down_s, b, s,
                  tm=256):
    eidx, rank, gates, counts, before_tiles = _router(h2, w_router, router_bias, tm)
    src, nchunk, dst, total, adjust = _combine_tables(counts[:, 0], before_tiles, tm)
    pos_tiles, local = _positions(counts, adjust, eidx, rank, tm, COMBINE_TOKENS)
    xs = _dispatch(h2t, pos_tiles, tm)
    ys = _experts(xs, counts[:, 0], w_gate, w_up, w_down)
    return _combine(ys, (src, nchunk, dst, total), local, gates, h2, x1, gate_f, w_gate_s, w_up_s, w_down_s, b, s)


def kernel(x, c, w_ada, b_ada, norm_mix_gain, w_in, q_norm_gain, k_norm_gain, rel_bias_table, conv_w, conv_b, dt_bias,
           a_log, d_skip, ssm_norm_gain, w_out, norm_ffn_gain, w_router, router_bias, w_gate_experts, w_up_experts,
           w_down_experts, w_gate_shared, w_up_shared, w_down_shared):
    b, s, d = x.shape
    for layer in range(w_ada.shape[0]):
        mod = _adaln(c, w_ada[layer], b_ada[layer])
        x1, h2, h2t = _mixer_sublayer(x, mod, norm_mix_gain[layer], w_in[layer], q_norm_gain[layer], k_norm_gain[layer],
                                 rel_bias_table, conv_w[layer], conv_b[layer], dt_bias[layer], a_log[layer],
                                 d_skip[layer], ssm_norm_gain[layer], w_out[layer], norm_ffn_gain[layer])
        gate_f = mod[:, 5 * d:]
        out = _moe_sublayer(x1, h2, h2t, gate_f, w_router[layer], router_bias[layer], w_gate_experts[layer],
                            w_up_experts[layer], w_down_experts[layer], w_gate_shared[layer], w_up_shared[layer],
                            w_down_shared[layer], b, s)
        x = out.reshape(b, s, d)
    return x
```

```python
import functools
import math

import numpy as np
import jax
import jax.numpy as jnp
from jax import lax
from jax.experimental import pallas as pl
from jax.experimental.pallas import tpu as pltpu

F32 = jnp.float32
BF16 = jnp.bfloat16
I32 = jnp.int32

D_MODEL = 1024
ATTN_HEADS = 8
HEAD_DIM = 64
ATTN_WIDTH = ATTN_HEADS * HEAD_DIM
PATTERNS = ((128, 1), (512, 4), (2048, 16))
WIN_STEPS = 128
REL_BUCKETS = 32
REL_MAX_DISTANCE = 2048
SSM_HEADS = 24
SSM_HEAD_DIM = 64
SSM_WIDTH = SSM_HEADS * SSM_HEAD_DIM
SSM_GROUPS = 4
HEADS_PER_GROUP = SSM_HEADS // SSM_GROUPS
GROUP_WIDTH = SSM_WIDTH // SSM_GROUPS
SSM_STATE = 128
SSM_CONV = 4
SSM_CHUNK = 128
CONV_CH = SSM_WIDTH + 2 * SSM_GROUPS * SSM_STATE
N_EXPERTS = 256
TOP_K = 8
N_EXPERT_GROUPS = 8
EXPERTS_PER_GROUP = N_EXPERTS // N_EXPERT_GROUPS
TOPK_GROUPS = 4
EXPERT_FF = 256
ROUTED_SCALE = 2.5
NORM_EPS = 1e-6

LANES = 128
SUBLANES = 8
NEG_BIG = -1e30
VMEM_LIMIT = 56 * 1024 * 1024


def _params(sem, vmem=VMEM_LIMIT):
    return pltpu.CompilerParams(dimension_semantics=sem, vmem_limit_bytes=vmem)


def _sigmoid(x):
    return 1.0 / (1.0 + jnp.exp(-x))


def _silu(x):
    return x * _sigmoid(x)


def _split3(x):
    hi = x.astype(BF16)
    r = x - hi.astype(F32)
    mid = r.astype(BF16)
    lo = (r - mid.astype(F32)).astype(BF16)
    return hi, mid, lo


def _dot(a, b):
    return jnp.dot(a, b, preferred_element_type=F32)


def _dot_nt(a, b):
    return lax.dot_general(a, b, (((1,), (1,)), ((), ())), preferred_element_type=F32)


def _dot_exact_rhs(a, b_exact):
    hi, mid, lo = _split3(a)
    return _dot(hi, b_exact) + _dot(mid, b_exact) + _dot(lo, b_exact)


def _dot_exact_lhs(a_exact, b):
    hi, mid, lo = _split3(b)
    return _dot(a_exact, hi) + _dot(a_exact, mid) + _dot(a_exact, lo)


def _adaln_kernel(c_ref, w_ref, b_ref, o_ref):
    s = _silu(c_ref[...]).astype(BF16)
    o_ref[...] = _dot(s, w_ref[...].astype(BF16)) + b_ref[...]


def _adaln(c, w_ada, b_ada):
    b, d = c.shape
    n = w_ada.shape[1]
    rows = SUBLANES
    c_pad = jnp.zeros((rows, d), F32).at[:b].set(c)
    tn = 1024
    out = pl.pallas_call(
        _adaln_kernel,
        grid=(n // tn,),
        in_specs=[pl.BlockSpec((rows, d), lambda j: (0, 0)),
                  pl.BlockSpec((d, tn), lambda j: (0, j)),
                  pl.BlockSpec((1, tn), lambda j: (0, j))],
        out_specs=pl.BlockSpec((rows, tn), lambda j: (0, j)),
        out_shape=jax.ShapeDtypeStruct((rows, n), F32),
        compiler_params=_params(("arbitrary",)),
        name="adaln",
    )(c_pad, w_ada, b_ada.reshape(1, n))
    return out[:b]


def _inproj_kernel(x_ref, shift_ref, scale_ref, g_ref, wqkv_ref, wz_ref, wxbc_ref, wdt_ref,
                   qg_ref, kg_ref, hmean_ref, q_ref, k_ref, v_ref, z_ref, xbc_ref, dt_ref):
    x = x_ref[...]
    ms = jnp.mean(x * x, axis=-1, keepdims=True)
    h = x * lax.rsqrt(ms + NORM_EPS) * g_ref[...]
    h = h * (1.0 + scale_ref[...]) + shift_ref[...]
    hb = h.astype(BF16)

    hmean = hmean_ref[...]

    def head_norm(t, gain):
        ss = _dot_exact_rhs(t * t, hmean)
        return t * lax.rsqrt(ss + NORM_EPS) * gain

    q = _dot(hb, wqkv_ref[:, 0:ATTN_WIDTH])
    q_ref[...] = head_norm(q, qg_ref[...]) * (HEAD_DIM ** -0.5)
    k = _dot(hb, wqkv_ref[:, ATTN_WIDTH:2 * ATTN_WIDTH])
    k_ref[...] = head_norm(k, kg_ref[...])
    v_ref[...] = _dot(hb, wqkv_ref[:, 2 * ATTN_WIDTH:3 * ATTN_WIDTH])
    for c0 in range(0, SSM_WIDTH, 512):
        z_ref[:, c0:c0 + 512] = _dot(hb, wz_ref[:, c0:c0 + 512])
    for c0 in range(0, CONV_CH, 512):
        xbc_ref[:, c0:c0 + 512] = _dot(hb, wxbc_ref[:, c0:c0 + 512])
    dt_ref[...] = _dot(hb, wdt_ref[...])


def _in_proj(x, shift, scale, gain, w_in, q_gain, k_gain, tm=256):
    b, s, d = x.shape
    t = b * s
    tiles_per_seq = s // tm
    w = w_in.astype(BF16)
    o_z = 3 * ATTN_WIDTH
    o_x = o_z + SSM_WIDTH
    o_dt = o_x + CONV_CH
    w_qkv, w_z, w_xbc = w[:, :o_z], w[:, o_z:o_x], w[:, o_x:o_dt]
    w_dt = jnp.zeros((d, LANES), BF16).at[:, :SSM_HEADS].set(w[:, o_dt:])
    head_of = np.arange(ATTN_WIDTH) // HEAD_DIM
    hmean = jnp.asarray((head_of[:, None] == head_of[None, :]).astype(np.float32) / HEAD_DIM, BF16)
    full = lambda shp: pl.BlockSpec(shp, lambda i: (0,) * len(shp))
    row = lambda n: pl.BlockSpec((tm, n), lambda i: (i, 0))
    per_batch = pl.BlockSpec((None, 1, d), lambda i: (i // tiles_per_seq, 0, 0))
    outs = pl.pallas_call(
        _inproj_kernel,
        grid=(t // tm,),
        in_specs=[row(d), per_batch, per_batch, full((1, d)),
                  full((d, o_z)), full((d, SSM_WIDTH)), full((d, CONV_CH)), full((d, LANES)),
                  full((1, ATTN_WIDTH)), full((1, ATTN_WIDTH)), full((ATTN_WIDTH, ATTN_WIDTH))],
        out_specs=[row(ATTN_WIDTH), row(ATTN_WIDTH), row(ATTN_WIDTH), row(SSM_WIDTH), row(CONV_CH), row(LANES)],
        out_shape=[jax.ShapeDtypeStruct((t, n), F32)
                   for n in (ATTN_WIDTH, ATTN_WIDTH, ATTN_WIDTH, SSM_WIDTH, CONV_CH, LANES)],
        compiler_params=_params(("arbitrary",)),
        name="in_proj",
    )(x.reshape(t, d), shift.reshape(b, 1, d), scale.reshape(b, 1, d), gain.reshape(1, d),
      w_qkv, w_z, w_xbc, w_dt,
      jnp.tile(q_gain, ATTN_HEADS).reshape(1, ATTN_WIDTH), jnp.tile(k_gain, ATTN_HEADS).reshape(1, ATTN_WIDTH), hmean)
    return outs


def _t5_causal_buckets(distance):
    n = np.maximum(distance, 0)
    max_exact = REL_BUCKETS // 2
    large = max_exact + (np.log(np.maximum(n, 1) / max_exact) / math.log(REL_MAX_DISTANCE / max_exact)
                         * (REL_BUCKETS - max_exact)).astype(np.int64)
    large = np.minimum(large, REL_BUCKETS - 1)
    return np.where(n < max_exact, n, large).astype(np.int32)


def _window_bias(rel_bias_table, dilation):
    qi = np.arange(WIN_STEPS)[:, None]
    kj = np.arange(2 * WIN_STEPS)[None, :]
    dist = qi + WIN_STEPS - kj
    band = (dist >= 0) & (dist <= WIN_STEPS)
    onehot = (_t5_causal_buckets(dist * dilation).reshape(-1, 1) == np.arange(REL_BUCKETS)[None, :]).astype(np.float32)
    bias = jnp.dot(rel_bias_table.astype(F32).T, jnp.asarray(onehot).T, precision=lax.Precision.HIGHEST)
    bias = bias.reshape(ATTN_HEADS, WIN_STEPS, 2 * WIN_STEPS)
    return jnp.where(jnp.asarray(band)[None], bias, NEG_BIG)


ATTN_TOKENS = max(w for w, _ in PATTERNS)
ATTN_UNROLL = 4


def _attn_kernel(q_ref, kp_ref, kc_ref, vp_ref, vc_ref, bias_ref, out_ref, kw, vw, o_acc, l_acc):
    tb = ATTN_TOKENS
    first = pl.program_id(2) == 0
    kw[0:tb] = kp_ref[...]
    kw[tb:2 * tb] = kc_ref[...]
    vw[0:tb] = vp_ref[...]
    vw[tb:2 * tb] = vc_ref[...]
    lane = lax.broadcasted_iota(I32, (WIN_STEPS, LANES), 1)
    head0 = lane < HEAD_DIM
    col = lax.broadcasted_iota(I32, (WIN_STEPS, 2 * WIN_STEPS), 1)
    in_prev = col < WIN_STEPS

    for p, (_, d) in enumerate(PATTERNS):
        shift = d.bit_length() - 1
        n_blocks = tb // WIN_STEPS

        def rows(start, n, d=d):
            return pl.ds(start, n, stride=d) if d > 1 else pl.ds(start, n)

        def body(it, carry, p=p, d=d, shift=shift, rows=rows):
            for u in range(ATTN_UNROLL):
                idx = it * ATTN_UNROLL + u
                r = jnp.bitwise_and(idx, d - 1)
                j = jnp.right_shift(idx, shift)
                qs = j * (WIN_STEPS * d) + r
                q = q_ref[rows(qs, WIN_STEPS), :]
                k = kw[rows(tb + qs - WIN_STEPS * d, 2 * WIN_STEPS), :].astype(BF16)
                v = vw[rows(tb + qs - WIN_STEPS * d, 2 * WIN_STEPS), :].astype(BF16)
                no_prev = jnp.logical_and(in_prev, jnp.logical_and(first, j == 0))
                o_h, lse_h = [], []
                for h in range(2):
                    qh = jnp.where(head0 if h == 0 else jnp.logical_not(head0), q, 0.0).astype(BF16)
                    s = _dot_nt(qh, k) + bias_ref[p, h]
                    s = jnp.where(no_prev, NEG_BIG, s)
                    m = jnp.max(s, axis=-1, keepdims=True)
                    e = jnp.exp(s - m)
                    denom = jnp.sum(e, axis=-1, keepdims=True)
                    o_h.append(_dot(e.astype(BF16), v) / denom)
                    lse_h.append(m + jnp.log(denom))
                o_acc[p, rows(qs, WIN_STEPS), :] = jnp.where(head0, o_h[0], o_h[1])
                l_acc[p, rows(qs, WIN_STEPS), :] = jnp.where(head0, lse_h[0], lse_h[1])
            return carry

        lax.fori_loop(0, n_blocks // ATTN_UNROLL, body, 0)

    chunk = 256
    for c0 in range(0, tb, chunk):
        l1, l2, l3 = (l_acc[p, c0:c0 + chunk, :] for p in range(3))
        m = jnp.maximum(jnp.maximum(l1, l2), l3)
        e1, e2, e3 = jnp.exp(l1 - m), jnp.exp(l2 - m), jnp.exp(l3 - m)
        num = e1 * o_acc[0, c0:c0 + chunk, :] + e2 * o_acc[1, c0:c0 + chunk, :] + e3 * o_acc[2, c0:c0 + chunk, :]
        out_ref[c0:c0 + chunk, :] = num / (e1 + e2 + e3)


def _attention(q, k, v, bias):
    b, s, w = q.shape
    tb = ATTN_TOKENS
    pairs = ATTN_HEADS // 2
    cur = pl.BlockSpec((None, tb, LANES), lambda bi, hp, i: (bi, i, hp))
    prev = pl.BlockSpec((None, tb, LANES), lambda bi, hp, i: (bi, jnp.maximum(i - 1, 0), hp))
    return pl.pallas_call(
        _attn_kernel,
        grid=(b, pairs, s // tb),
        in_specs=[cur, prev, cur, prev, cur,
                  pl.BlockSpec((len(PATTERNS), 2, WIN_STEPS, 2 * WIN_STEPS), lambda bi, hp, i: (0, hp, 0, 0))],
        out_specs=cur,
        out_shape=jax.ShapeDtypeStruct((b, s, w), F32),
        scratch_shapes=[pltpu.VMEM((2 * tb, LANES), F32), pltpu.VMEM((2 * tb, LANES), F32),
                        pltpu.VMEM((len(PATTERNS), tb, LANES), F32), pltpu.VMEM((len(PATTERNS), tb, LANES), F32)],
        compiler_params=_params(("arbitrary",) * 3),
        name="attention",
    )(q, k, k, v, v, bias)


def _ssd_kernel(xbc_ref, halo_ref, z_ref, dtraw_ref, convw_ref, convb_ref, dtb_ref, alog_ref, dskip_ref, gain_ref,
                expand_ref, tril_ref, y_ref, state_ref):
    c = pl.program_id(1)

    @pl.when(c == 0)
    def _():
        state_ref[...] = jnp.zeros_like(state_ref)

    x = xbc_ref[...]
    halo = jnp.where(c == 0, 0.0, halo_ref[...])
    w = convw_ref[...]
    acc = x * w[SSM_CONV - 1:SSM_CONV, :] + convb_ref[...]
    row8 = lax.broadcasted_iota(I32, (SUBLANES, CONV_CH), 0)
    for shift in range(1, SSM_CONV):
        xs = pltpu.roll(x, shift, axis=0)
        hs = pltpu.roll(halo, shift, axis=0)
        head = jnp.where(row8 < shift, hs, xs[0:SUBLANES])
        xs = jnp.concatenate([head, xs[SUBLANES:]], axis=0)
        acc = acc + xs * w[SSM_CONV - 1 - shift:SSM_CONV - shift, :]
    act = _silu(acc)
    x_s = act[:, :SSM_WIDTH]
    bc0 = SSM_WIDTH
    cc0 = SSM_WIDTH + SSM_GROUPS * SSM_STATE

    t = dtraw_ref[...] + dtb_ref[...]
    dt = jnp.maximum(t, 0.0) + jnp.log(1.0 + jnp.exp(-jnp.abs(t)))
    a = dt * (-jnp.exp(alog_ref[...]))
    a_cs = _dot_exact_lhs(tril_ref[...], a)
    a_cs_t = a_cs.T
    a_last = a_cs[SSM_CHUNK - 1:SSM_CHUNK, :]
    expand = expand_ref[...]
    dt_e = _dot_exact_rhs(dt, expand)
    ea_e = _dot_exact_rhs(jnp.exp(a_cs), expand)
    dte_e = _dot_exact_rhs(jnp.exp(a_last - a_cs), expand)
    xdt = x_s * dt_e
    xw = (xdt * dte_e).astype(BF16)
    xdt_b = xdt.astype(BF16)

    li = lax.broadcasted_iota(I32, (SSM_CHUNK, SSM_CHUNK), 0)
    si = lax.broadcasted_iota(I32, (SSM_CHUNK, SSM_CHUNK), 1)
    causal = li >= si

    ys = []
    for g in range(SSM_GROUPS):
        gs = slice(g * GROUP_WIDTH, (g + 1) * GROUP_WIDTH)
        b_g = act[:, bc0 + g * SSM_STATE:bc0 + (g + 1) * SSM_STATE]
        c_g = act[:, cc0 + g * SSM_STATE:cc0 + (g + 1) * SSM_STATE].astype(BF16)
        cb = _dot_nt(c_g, b_g.astype(BF16))
        state = state_ref[g]
        y_off = _dot(c_g, state.astype(BF16)) * ea_e[:, gs]
        parts = []
        for j in range(HEADS_PER_GROUP):
            hh = g * HEADS_PER_GROUP + j
            seg = a_cs[:, hh:hh + 1] - a_cs_t[hh:hh + 1, :]
            decay = jnp.exp(jnp.where(causal, seg, NEG_BIG))
            m = (cb * decay).astype(BF16)
            parts.append(_dot(m, xdt_b[:, hh * SSM_HEAD_DIM:(hh + 1) * SSM_HEAD_DIM]))
        ys.append(jnp.concatenate(parts, axis=1) + y_off)
        state_ref[g] = state * ea_e[SSM_CHUNK - 1:SSM_CHUNK, gs] + _dot(b_g.T.astype(BF16), xw[:, gs])
    y = jnp.concatenate(ys, axis=1) + dskip_ref[...] * x_s
    y = y * _silu(z_ref[...])
    gain = gain_ref[...]
    for g in range(SSM_GROUPS):
        gs = slice(g * GROUP_WIDTH, (g + 1) * GROUP_WIDTH)
        yg = y[:, gs]
        ms = jnp.mean(yg * yg, axis=-1, keepdims=True)
        y_ref[:, gs] = yg * lax.rsqrt(ms + NORM_EPS) * gain[:, gs]


def _ssd(xbc, z, dt_raw, conv_w, conv_b, dt_bias, a_log, d_skip, norm_gain, b, s):
    t = b * s
    nc = s // SSM_CHUNK
    pad_heads = lambda v: jnp.zeros((1, LANES), F32).at[0, :SSM_HEADS].set(v)
    head_of_lane = np.arange(SSM_WIDTH) // SSM_HEAD_DIM
    expand = jnp.asarray((np.arange(LANES)[:, None] == head_of_lane[None, :]).astype(np.float32), BF16)
    tril = jnp.asarray(np.tril(np.ones((SSM_CHUNK, SSM_CHUNK), np.float32)), BF16)
    halo_blocks = SSM_CHUNK // SUBLANES
    chunk = lambda n: pl.BlockSpec((SSM_CHUNK, n), lambda bi, c: (bi * nc + c, 0))
    full = lambda shp: pl.BlockSpec(shp, lambda bi, c: (0,) * len(shp))
    halo = pl.BlockSpec((SUBLANES, CONV_CH), lambda bi, c: (jnp.maximum((bi * nc + c) * halo_blocks - 1, 0), 0))
    return pl.pallas_call(
        _ssd_kernel,
        grid=(b, nc),
        in_specs=[chunk(CONV_CH), halo, chunk(SSM_WIDTH), chunk(LANES),
                  full((SSM_CONV, CONV_CH)), full((1, CONV_CH)), full((1, LANES)), full((1, LANES)),
                  full((1, SSM_WIDTH)), full((1, SSM_WIDTH)), full((LANES, SSM_WIDTH)), full((SSM_CHUNK, SSM_CHUNK))],
        out_specs=chunk(SSM_WIDTH),
        out_shape=jax.ShapeDtypeStruct((t, SSM_WIDTH), F32),
        scratch_shapes=[pltpu.VMEM((SSM_GROUPS, SSM_STATE, GROUP_WIDTH), F32)],
        compiler_params=_params(("arbitrary", "arbitrary")),
        name="ssd",
    )(xbc, xbc, z, dt_raw, conv_w, conv_b.reshape(1, CONV_CH), pad_heads(dt_bias), pad_heads(a_log),
      jnp.repeat(d_skip, SSM_HEAD_DIM).reshape(1, SSM_WIDTH), norm_gain.reshape(1, SSM_WIDTH), expand, tril)


U32 = jnp.uint32
TOKEN_ROWS = D_MODEL // (2 * LANES)
HIGH_HALF = np.uint32(0xFFFF0000)


def _to_token_tiles(ref, x):
    n = x.shape[0]
    for c in range(TOKEN_ROWS):
        lo = lax.bitcast_convert_type(x[:, c * LANES:(c + 1) * LANES].astype(BF16).astype(F32), U32)
        hi = lax.bitcast_convert_type(x[:, (c + TOKEN_ROWS) * LANES:(c + TOKEN_ROWS + 1) * LANES]
                                      .astype(BF16).astype(F32), U32)
        ref[pl.ds(c, n, stride=TOKEN_ROWS), :] = jnp.bitwise_or(jnp.right_shift(lo, 16), jnp.bitwise_and(hi, HIGH_HALF))


def _from_token_tiles(ref, n, token0=0):
    lows, highs = [], []
    for c in range(TOKEN_ROWS):
        word = ref[pl.ds(token0 * TOKEN_ROWS + c, n, stride=TOKEN_ROWS), :]
        lows.append(lax.bitcast_convert_type(jnp.left_shift(word, 16), F32))
        highs.append(lax.bitcast_convert_type(jnp.bitwise_and(word, HIGH_HALF), F32))
    return jnp.concatenate(lows + highs, axis=1)


def _outproj_kernel(attn_ref, ssm_ref, x_ref, gate_ref, shift_ref, scale_ref, g_ref, wa_ref, ws_ref, x1_ref, h2_ref,
                    h2t_ref):
    mixed = _dot(attn_ref[...].astype(BF16), wa_ref[...]) + _dot(ssm_ref[...].astype(BF16), ws_ref[...])
    x1 = x_ref[...] + gate_ref[...] * mixed
    x1_ref[...] = x1
    ms = jnp.mean(x1 * x1, axis=-1, keepdims=True)
    h = x1 * lax.rsqrt(ms + NORM_EPS) * g_ref[...]
    h2 = h * (1.0 + scale_ref[...]) + shift_ref[...]
    h2_ref[...] = h2
    _to_token_tiles(h2t_ref, h2)


def _out_proj(attn, ssm, x, gate, shift, scale, gain, w_out, b, s, tm=256):
    t = b * s
    d = D_MODEL
    tiles_per_seq = s // tm
    w = w_out.astype(BF16)
    row = lambda n: pl.BlockSpec((tm, n), lambda i: (i, 0))
    full = lambda shp: pl.BlockSpec(shp, lambda i: (0,) * len(shp))
    per_batch = pl.BlockSpec((None, 1, d), lambda i: (i // tiles_per_seq, 0, 0))
    return pl.pallas_call(
        _outproj_kernel,
        grid=(t // tm,),
        in_specs=[row(ATTN_WIDTH), row(SSM_WIDTH), row(d), per_batch, per_batch, per_batch, full((1, d)),
                  full((ATTN_WIDTH, d)), full((SSM_WIDTH, d))],
        out_specs=[row(d), row(d), pl.BlockSpec((tm * TOKEN_ROWS, LANES), lambda i: (i, 0))],
        out_shape=[jax.ShapeDtypeStruct((t, d), F32)] * 2 + [jax.ShapeDtypeStruct((t * TOKEN_ROWS, LANES), U32)],
        compiler_params=_params(("arbitrary",)),
        name="out_proj",
    )(attn.reshape(t, ATTN_WIDTH), ssm, x.reshape(t, d),
      gate.reshape(b, 1, d), shift.reshape(b, 1, d), scale.reshape(b, 1, d), gain.reshape(1, d),
      w[:ATTN_WIDTH], w[ATTN_WIDTH:])


def _mixer_sublayer(x, mod, norm_mix_gain, w_in, q_norm_gain, k_norm_gain, rel_bias_table, conv_w, conv_b, dt_bias,
                    a_log, d_skip, ssm_norm_gain, w_out, norm_ffn_gain):
    b, s, d = x.shape
    shift_m, scale_m, gate_m, shift_f, scale_f, _ = jnp.split(mod, 6, axis=-1)
    q, k, v, z, xbc, dt_raw = _in_proj(x, shift_m, scale_m, norm_mix_gain, w_in, q_norm_gain, k_norm_gain)
    bias = jnp.stack([_window_bias(rel_bias_table, dilation) for _, dilation in PATTERNS])
    attn = _attention(q.reshape(b, s, ATTN_WIDTH), k.reshape(b, s, ATTN_WIDTH), v.reshape(b, s, ATTN_WIDTH), bias)
    ssm = _ssd(xbc, z, dt_raw, conv_w, conv_b, dt_bias, a_log, d_skip, ssm_norm_gain, b, s)
    return _out_proj(attn, ssm, x, gate_m, shift_f, scale_f, norm_ffn_gain, w_out, b, s)


def _first_argmax(v, iota, limit):
    m = jnp.max(v, axis=0, keepdims=True)
    idx = jnp.min(jnp.where(v == m, iota, limit), axis=0, keepdims=True)
    return m, idx


def _router_kernel(h_ref, wt_ref, bias_ref, upper_ref, eidx_ref, rank_ref, gate_ref, counts_ref, carry_ref):
    @pl.when(pl.program_id(0) == 0)
    def _():
        carry_ref[...] = jnp.zeros_like(carry_ref)

    tm = h_ref.shape[0]
    h = h_ref[...]
    wt = wt_ref[...]
    h_hi = h.astype(BF16)
    h_lo = (h - h_hi.astype(F32)).astype(BF16)
    w_hi = wt.astype(BF16)
    w_lo = (wt - w_hi.astype(F32)).astype(BF16)
    logits = _dot_nt(w_hi, h_hi) + _dot_nt(w_hi, h_lo) + _dot_nt(w_lo, h_hi)
    scores = _sigmoid(logits)
    choice = scores + bias_ref[...]
    neg_inf = -jnp.inf

    iota_g = lax.broadcasted_iota(I32, (EXPERTS_PER_GROUP, tm), 0).astype(F32)
    group_rows = []
    for g in range(N_EXPERT_GROUPS):
        v = choice[g * EXPERTS_PER_GROUP:(g + 1) * EXPERTS_PER_GROUP]
        m1, i1 = _first_argmax(v, iota_g, float(EXPERTS_PER_GROUP))
        m2 = jnp.max(jnp.where(iota_g == i1, neg_inf, v), axis=0, keepdims=True)
        group_rows.append(m1 + m2)
    group_scores = jnp.concatenate(group_rows, axis=0)

    iota_n = lax.broadcasted_iota(I32, (N_EXPERT_GROUPS, tm), 0).astype(F32)
    chosen = jnp.zeros((N_EXPERT_GROUPS, tm), F32)
    for _ in range(TOPK_GROUPS):
        _, gi = _first_argmax(group_scores, iota_n, float(N_EXPERT_GROUPS))
        hit = iota_n == gi
        chosen = jnp.where(hit, 1.0, chosen)
        group_scores = jnp.where(hit, neg_inf, group_scores)

    masked = jnp.concatenate(
        [jnp.where(chosen[g:g + 1] > 0.0, choice[g * EXPERTS_PER_GROUP:(g + 1) * EXPERTS_PER_GROUP], neg_inf)
         for g in range(N_EXPERT_GROUPS)], axis=0)

    iota_e = lax.broadcasted_iota(I32, (N_EXPERTS, tm), 0).astype(F32)
    picked, gates = [], []
    onehot = jnp.zeros((N_EXPERTS, tm), F32)
    for _ in range(TOP_K):
        _, ei = _first_argmax(masked, iota_e, float(N_EXPERTS))
        hit = iota_e == ei
        gates.append(jnp.sum(jnp.where(hit, scores, 0.0), axis=0, keepdims=True))
        masked = jnp.where(hit, neg_inf, masked)
        onehot = jnp.where(hit, 1.0, onehot)
        picked.append(ei)
    gate_sum = gates[0]
    for gk in gates[1:]:
        gate_sum = gate_sum + gk

    base = _dot(onehot.astype(BF16), upper_ref[...]) + carry_ref[...]
    ranks = [jnp.sum(jnp.where(iota_e == ei, base, 0.0), axis=0, keepdims=True) for ei in picked]
    carry_ref[...] = carry_ref[...] + jnp.sum(onehot, axis=1, keepdims=True)

    eidx_ref[...] = jnp.concatenate(picked, axis=0).astype(I32)
    rank_ref[...] = jnp.concatenate(ranks, axis=0).astype(I32)
    gate_ref[...] = jnp.concatenate([gk / gate_sum * ROUTED_SCALE for gk in gates], axis=0)
    counts_ref[...] = carry_ref[...].astype(I32)


def _router(h2, w_router, router_bias, tm=256):
    t, d = h2.shape
    upper = jnp.asarray(np.triu(np.ones((tm, tm), np.float32), 1), BF16)
    tok = pl.BlockSpec((TOP_K, tm), lambda i: (0, i))
    full = lambda shp: pl.BlockSpec(shp, lambda i: (0,) * len(shp))
    return pl.pallas_call(
        _router_kernel,
        grid=(t // tm,),
        in_specs=[pl.BlockSpec((tm, d), lambda i: (i, 0)), full((N_EXPERTS, d)), full((N_EXPERTS, 1)), full((tm, tm))],
        out_specs=[tok, tok, tok, full((N_EXPERTS, 1))],
        out_shape=[jax.ShapeDtypeStruct((TOP_K, t), I32), jax.ShapeDtypeStruct((TOP_K, t), I32),
                   jax.ShapeDtypeStruct((TOP_K, t), F32), jax.ShapeDtypeStruct((N_EXPERTS, 1), I32)],
        scratch_shapes=[pltpu.VMEM((N_EXPERTS, 1), F32)],
        compiler_params=_params(("arbitrary",)),
        name="router",
    )(h2, w_router.T, router_bias.reshape(N_EXPERTS, 1), upper)


def _positions_kernel(counts_ref, lower_ref, eidx_ref, rank_ref, pos_ref):
    tm = eidx_ref.shape[1]
    counts = jnp.broadcast_to(counts_ref[...].astype(F32), (N_EXPERTS, LANES))
    offsets = _dot_exact_lhs(lower_ref[...], counts)[:, 0:1]
    iota_e = lax.broadcasted_iota(I32, (N_EXPERTS, tm), 0).astype(F32)
    e = eidx_ref[...].astype(F32)
    rows = [jnp.sum(jnp.where(iota_e == e[k:k + 1], offsets, 0.0), axis=0, keepdims=True) for k in range(TOP_K)]
    pos_ref[0] = jnp.concatenate(rows, axis=0).astype(I32) + rank_ref[...]


def _positions(counts, eidx, rank, tm):
    t = eidx.shape[1]
    lower = jnp.asarray(np.tril(np.ones((N_EXPERTS, N_EXPERTS), np.float32), -1), BF16)
    tok = pl.BlockSpec((TOP_K, tm), lambda i: (0, i))
    return pl.pallas_call(
        _positions_kernel,
        grid=(t // tm,),
        in_specs=[pl.BlockSpec((N_EXPERTS, 1), lambda i: (0, 0)), pl.BlockSpec((N_EXPERTS, N_EXPERTS), lambda i: (0, 0)),
                  tok, tok],
        out_specs=pl.BlockSpec((1, TOP_K, tm), lambda i: (i, 0, 0)),
        out_shape=jax.ShapeDtypeStruct((t // tm, TOP_K, tm), I32),
        compiler_params=_params(("arbitrary",)),
        name="positions",
    )(counts, lower, eidx, rank)


def _token_rows(i):
    return pl.ds(pl.multiple_of(i * TOKEN_ROWS, TOKEN_ROWS), TOKEN_ROWS)


def _dispatch_kernel(pos_ref, h_ref, xs_ref, sem):
    tm = h_ref.shape[0] // TOKEN_ROWS
    sub = pos_ref.shape[2]
    shift = sub.bit_length() - 1

    def row_copy(t, k):
        p = pos_ref[jnp.right_shift(t, shift), k, jnp.bitwise_and(t, sub - 1)]
        return pltpu.make_async_copy(h_ref.at[_token_rows(t)], xs_ref.at[_token_rows(p)], sem)

    def start(t, carry):
        for k in range(TOP_K):
            row_copy(t, k).start(priority=k % 2)
        return carry

    def wait(t, carry):
        for k in range(TOP_K):
            row_copy(t, k).wait()
        return carry

    lax.fori_loop(0, tm, start, 0)
    lax.fori_loop(0, tm, wait, 0)


def _dispatch(h2t, pos_tiles, tm=1024):
    t = h2t.shape[0] // TOKEN_ROWS
    sub = pos_tiles.shape[2]
    return pl.pallas_call(
        _dispatch_kernel,
        grid=(t // tm,),
        in_specs=[pl.BlockSpec((tm // sub, TOP_K, sub), lambda i: (i, 0, 0), memory_space=pltpu.SMEM),
                  pl.BlockSpec((tm * TOKEN_ROWS, LANES), lambda i: (i, 0))],
        out_specs=pl.BlockSpec(memory_space=pl.ANY),
        out_shape=jax.ShapeDtypeStruct(((t * TOP_K + EXPERT_BLOCK) * TOKEN_ROWS, LANES), h2t.dtype),
        scratch_shapes=[pltpu.SemaphoreType.DMA(())],
        compiler_params=_params(("arbitrary",)),
        name="dispatch",
    )(pos_tiles, h2t)


EXPERT_BLOCK = 256
TAIL_PIECES = tuple(1 << i for i in reversed(range(EXPERT_BLOCK.bit_length() - 1)))
X_SLOTS = 4
Y_SLOTS = 4


def _experts_kernel(start_ref, count_ref, nxt_ref, slot_ref, first_ref, blk0_ref, full0_ref, ptail_ref, ltail_ref,
                    blktok_ref, nblocks_ref, xs_hbm, wg_hbm, wu_hbm, wd_hbm, ys_hbm,
                    wg_buf, wu_buf, wd_buf, wg_bf, wu_bf, wd_bf, xbuf, ybuf, ytail, wsem, xsem, ysem, tsem):
    e = pl.program_id(0)
    last_step = e == pl.num_programs(0) - 1
    start, count = start_ref[e], count_ref[e]
    n_full = jnp.right_shift(count, EXPERT_BLOCK.bit_length() - 1)
    tail = jnp.bitwise_and(count, EXPERT_BLOCK - 1)
    n_blk = n_full + (tail > 0).astype(I32)
    blk0, full0 = blk0_ref[e], full0_ref[e]
    slot, nxt = slot_ref[e], nxt_ref[e]

    def token_rows(token, n):
        return pl.ds(pl.multiple_of(token * TOKEN_ROWS, TOKEN_ROWS), n * TOKEN_ROWS)

    def fetch(ex, s):
        return (pltpu.make_async_copy(wg_hbm.at[ex], wg_buf.at[s], wsem.at[s, 0]),
                pltpu.make_async_copy(wu_hbm.at[ex], wu_buf.at[s], wsem.at[s, 1]),
                pltpu.make_async_copy(wd_hbm.at[ex], wd_buf.at[s], wsem.at[s, 2]))

    def x_copy(token, s):
        return pltpu.make_async_copy(xs_hbm.at[token_rows(token, EXPERT_BLOCK)], xbuf.at[s], xsem.at[s])

    def y_copy(token, s):
        return pltpu.make_async_copy(ybuf.at[s], ys_hbm.at[token_rows(token, EXPERT_BLOCK)], ysem.at[s])

    def tail_copies(token, length):
        out = []
        for piece in TAIL_PIECES:
            bigger = (EXPERT_BLOCK - 1) & ~(2 * piece - 1)
            done = jnp.bitwise_and(length, bigger)
            cp = pltpu.make_async_copy(ytail.at[token_rows(done, piece)], ys_hbm.at[token_rows(token + done, piece)],
                                       tsem)
            out.append((jnp.bitwise_and(length, piece) != 0, cp))
        return out

    def block(s):
        x = _from_token_tiles(xbuf.at[s], EXPERT_BLOCK).astype(BF16)
        g = _dot(x, wg_bf[...])
        u = _dot(x, wu_bf[...])
        return _dot((_silu(g) * u).astype(BF16), wd_bf[...])

    @pl.when(count > 0)
    def _():
        @pl.when(first_ref[e] == 1)
        def _():
            for g in range(X_SLOTS - 1):
                @pl.when(g < nblocks_ref[0])
                def _(g=g):
                    x_copy(blktok_ref[g], g).start()
            for cp in fetch(e, slot):
                cp.start()

        for cp in fetch(e, slot):
            cp.wait()

        @pl.when(nxt >= 0)
        def _():
            for cp in fetch(nxt, 1 - slot):
                cp.start()

        wg_bf[...] = wg_buf[slot].astype(BF16)
        wu_bf[...] = wu_buf[slot].astype(BF16)
        wd_bf[...] = wd_buf[slot].astype(BF16)

        def step(i):
            g = blk0 + i
            xs_slot = jnp.bitwise_and(g, X_SLOTS - 1)
            x_copy(start, xs_slot).wait()
            ahead = g + (X_SLOTS - 1)

            @pl.when(ahead < nblocks_ref[0])
            def _():
                x_copy(blktok_ref[ahead], jnp.bitwise_and(ahead, X_SLOTS - 1)).start()

            return block(xs_slot)

        def full_block(i, carry):
            y = step(i)
            j = full0 + i
            ys_slot = jnp.bitwise_and(j, Y_SLOTS - 1)

            @pl.when(j >= Y_SLOTS)
            def _():
                y_copy(start, ys_slot).wait()

            _to_token_tiles(ybuf.at[ys_slot], y)
            y_copy(start + i * EXPERT_BLOCK, ys_slot).start()
            return carry

        lax.fori_loop(0, n_full, full_block, 0)

        @pl.when(tail > 0)
        def _():
            y = step(n_full)
            for pred, cp in tail_copies(start, ptail_ref[e]):
                @pl.when(pred)
                def _(cp=cp):
                    cp.wait()
            _to_token_tiles(ytail, y)
            for pred, cp in tail_copies(start + n_full * EXPERT_BLOCK, tail):
                @pl.when(pred)
                def _(cp=cp):
                    cp.start()

    @pl.when(last_step)
    def _():
        total_full = full0 + n_full
        for back in range(1, Y_SLOTS + 1):
            @pl.when(total_full >= back)
            def _(back=back):
                y_copy(0, jnp.bitwise_and(total_full - back, Y_SLOTS - 1)).wait()
        for pred, cp in tail_copies(0, ltail_ref[0]):
            @pl.when(pred)
            def _(cp=cp):
                cp.wait()


def _max_expert_blocks(n_rows):
    return n_rows // EXPERT_BLOCK + N_EXPERTS


def _expert_metadata(counts, n_rows):
    ids = jnp.arange(N_EXPERTS, dtype=I32)
    used = counts > 0
    starts = jnp.cumsum(counts) - counts
    n_blk = (counts + EXPERT_BLOCK - 1) // EXPERT_BLOCK
    n_full = counts // EXPERT_BLOCK
    tail = counts % EXPERT_BLOCK
    blk0 = jnp.cumsum(n_blk) - n_blk
    full0 = jnp.cumsum(n_full) - n_full
    next_used = lax.cummin(jnp.where(used, ids, N_EXPERTS), reverse=True)
    next_after = jnp.concatenate([next_used[1:], jnp.full((1,), N_EXPERTS, I32)])
    nxt = jnp.where(next_after < N_EXPERTS, next_after, -1)
    ordinal = jnp.cumsum(used.astype(I32)) - 1
    slot = ordinal % 2
    first = jnp.logical_and(used, ordinal == 0)
    latest = lax.cummax(jnp.where(tail > 0, ids, -1))
    before = jnp.concatenate([jnp.full((1,), -1, I32), latest[:-1]])
    pick = lambda index, values: jnp.sum(jnp.where(index[:, None] == ids[None, :], values[None, :], 0), axis=1)
    ptail = pick(before, tail)
    ltail = pick(latest[-1:], tail)
    block_ends = jnp.cumsum(n_blk)
    g = jnp.arange(_max_expert_blocks(n_rows), dtype=I32)
    eg = jnp.sum((g[:, None] >= block_ends[None, :]).astype(I32), axis=1)
    blktok = g * EXPERT_BLOCK + pick(eg, starts - blk0 * EXPERT_BLOCK)
    return tuple(v.astype(I32) for v in (starts, counts, nxt, slot, first, blk0, full0, ptail, ltail, blktok,
                                         block_ends[-1:]))


def _experts(xs, counts, w_gate, w_up, w_down):
    d = D_MODEL
    meta = _expert_metadata(counts, xs.shape[0] // TOKEN_ROWS - EXPERT_BLOCK)
    hbm = pl.BlockSpec(memory_space=pl.ANY)
    blk = (EXPERT_BLOCK * TOKEN_ROWS, LANES)
    grid_spec = pltpu.PrefetchScalarGridSpec(
        num_scalar_prefetch=len(meta),
        grid=(N_EXPERTS,),
        in_specs=[hbm, hbm, hbm, hbm],
        out_specs=hbm,
        scratch_shapes=[pltpu.VMEM((2, d, EXPERT_FF), F32), pltpu.VMEM((2, d, EXPERT_FF), F32),
                        pltpu.VMEM((2, EXPERT_FF, d), F32),
                        pltpu.VMEM((d, EXPERT_FF), BF16), pltpu.VMEM((d, EXPERT_FF), BF16),
                        pltpu.VMEM((EXPERT_FF, d), BF16),
                        pltpu.VMEM((X_SLOTS,) + blk, U32), pltpu.VMEM((Y_SLOTS,) + blk, U32), pltpu.VMEM(blk, U32),
                        pltpu.SemaphoreType.DMA((2, 3)), pltpu.SemaphoreType.DMA((X_SLOTS,)),
                        pltpu.SemaphoreType.DMA((Y_SLOTS,)), pltpu.SemaphoreType.DMA(())],
    )
    return pl.pallas_call(
        _experts_kernel,
        grid_spec=grid_spec,
        out_shape=jax.ShapeDtypeStruct(xs.shape, U32),
        compiler_params=_params(("arbitrary",)),
        name="experts",
    )(*meta, xs, w_gate, w_up, w_down)


def _combine_kernel(pos_ref, posn_ref, gates_ref, h_ref, x1_ref, gatef_ref, wg_ref, wu_ref, wd_ref, ys_ref, out_ref,
                    buf, sems):
    i = pl.program_id(0)
    n = pl.num_programs(0)
    tm = h_ref.shape[0]
    slot = jnp.bitwise_and(i, 1)

    def row_copy(p_ref, s, t, k):
        return pltpu.make_async_copy(ys_ref.at[_token_rows(p_ref[0, k, t])], buf.at[s, k, _token_rows(t)], sems.at[s])

    def issue(p_ref, s):
        def start(t, carry):
            for k in range(TOP_K):
                row_copy(p_ref, s, t, k).start()
            return carry

        lax.fori_loop(0, tm, start, 0)

    @pl.when(i == 0)
    def _():
        issue(pos_ref, slot)

    @pl.when(i + 1 < n)
    def _():
        issue(posn_ref, 1 - slot)

    hb = h_ref[...].astype(BF16)
    shared = _dot((_silu(_dot(hb, wg_ref[...])) * _dot(hb, wu_ref[...])).astype(BF16), wd_ref[...])

    def wait(t, carry):
        for k in range(TOP_K):
            row_copy(pos_ref, slot, t, k).wait()
        return carry

    lax.fori_loop(0, tm, wait, 0)
    gates = gates_ref[...]
    routed = _from_token_tiles(buf.at[slot, 0], tm) * gates[:, 0:1]
    for k in range(1, TOP_K):
        routed = routed + _from_token_tiles(buf.at[slot, k], tm) * gates[:, k:k + 1]
    out_ref[...] = x1_ref[...] + gatef_ref[...] * (shared + routed)


def _combine(ys, pos_tiles, gates_t, h2, x1, gate_f, w_gate_s, w_up_s, w_down_s, b, s, tm):
    t, d = h2.shape
    tiles_per_seq = s // tm
    n_tiles = t // tm
    row = lambda n: pl.BlockSpec((tm, n), lambda i: (i, 0))
    full = lambda shp: pl.BlockSpec(shp, lambda i: (0,) * len(shp))
    return pl.pallas_call(
        _combine_kernel,
        grid=(n_tiles,),
        in_specs=[pl.BlockSpec((1, TOP_K, tm), lambda i: (i, 0, 0), memory_space=pltpu.SMEM),
                  pl.BlockSpec((1, TOP_K, tm), lambda i: (jnp.minimum(i + 1, n_tiles - 1), 0, 0),
                               memory_space=pltpu.SMEM),
                  row(TOP_K), row(d), row(d),
                  pl.BlockSpec((None, 1, d), lambda i: (i // tiles_per_seq, 0, 0)),
                  full((d, EXPERT_FF)), full((d, EXPERT_FF)), full((EXPERT_FF, d)),
                  pl.BlockSpec(memory_space=pl.ANY)],
        out_specs=row(d),
        out_shape=jax.ShapeDtypeStruct((t, d), F32),
        scratch_shapes=[pltpu.VMEM((2, TOP_K, tm * TOKEN_ROWS, LANES), U32), pltpu.SemaphoreType.DMA((2,))],
        compiler_params=_params(("arbitrary",)),
        name="combine",
    )(pos_tiles, pos_tiles, gates_t, h2, x1, gate_f.reshape(b, 1, d),
      w_gate_s.astype(BF16), w_up_s.astype(BF16), w_down_s.astype(BF16), ys)


def _moe_sublayer(x1, h2, h2t, gate_f, w_router, router_bias, w_gate, w_up, w_down, w_gate_s, w_up_s, w_down_s, b, s,
                  tm=256):
    eidx, rank, gates, counts = _router(h2, w_router, router_bias)
    pos_tiles = _positions(counts, eidx, rank, tm)
    xs = _dispatch(h2t, pos_tiles)
    ys = _experts(xs, counts[:, 0], w_gate, w_up, w_down)
    return _combine(ys, pos_tiles, gates.T, h2, x1, gate_f, w_gate_s, w_up_s, w_down_s, b, s, tm)


def kernel(x, c, w_ada, b_ada, norm_mix_gain, w_in, q_norm_gain, k_norm_gain, rel_bias_table, conv_w, conv_b, dt_bias,
           a_log, d_skip, ssm_norm_gain, w_out, norm_ffn_gain, w_router, router_bias, w_gate_experts, w_up_experts,
           w_down_experts, w_gate_shared, w_up_shared, w_down_shared):
    b, s, d = x.shape
    for layer in range(w_ada.shape[0]):
        mod = _adaln(c, w_ada[layer], b_ada[layer])
        x1, h2, h2t = _mixer_sublayer(x, mod, norm_mix_gain[layer], w_in[layer], q_norm_gain[layer], k_norm_gain[layer],
                                 rel_bias_table, conv_w[layer], conv_b[layer], dt_bias[layer], a_log[layer],
                                 d_skip[layer], ssm_norm_gain[layer], w_out[layer], norm_ffn_gain[layer])
        gate_f = mod[:, 5 * d:]
        out = _moe_sublayer(x1, h2, h2t, gate_f, w_router[layer], router_bias[layer], w_gate_experts[layer],
                            w_up_experts[layer], w_down_experts[layer], w_gate_shared[layer], w_up_shared[layer],
                            w_down_shared[layer], b, s)
        x = out.reshape(b, s, d)
    return x
```

```python
import functools
import math

import numpy as np
import jax
import jax.numpy as jnp
from jax import lax
from jax.experimental import pallas as pl
from jax.experimental.pallas import tpu as pltpu

F32 = jnp.float32
BF16 = jnp.bfloat16
I32 = jnp.int32

D_MODEL = 1024
ATTN_HEADS = 8
HEAD_DIM = 64
ATTN_WIDTH = ATTN_HEADS * HEAD_DIM
PATTERNS = ((128, 1), (512, 4), (2048, 16))
WIN_STEPS = 128
REL_BUCKETS = 32
REL_MAX_DISTANCE = 2048
SSM_HEADS = 24
SSM_HEAD_DIM = 64
SSM_WIDTH = SSM_HEADS * SSM_HEAD_DIM
SSM_GROUPS = 4
HEADS_PER_GROUP = SSM_HEADS // SSM_GROUPS
GROUP_WIDTH = SSM_WIDTH // SSM_GROUPS
SSM_STATE = 128
SSM_CONV = 4
SSM_CHUNK = 128
CONV_CH = SSM_WIDTH + 2 * SSM_GROUPS * SSM_STATE
N_EXPERTS = 256
TOP_K = 8
N_EXPERT_GROUPS = 8
EXPERTS_PER_GROUP = N_EXPERTS // N_EXPERT_GROUPS
TOPK_GROUPS = 4
EXPERT_FF = 256
ROUTED_SCALE = 2.5
NORM_EPS = 1e-6

LANES = 128
SUBLANES = 8
NEG_BIG = -1e30
VMEM_LIMIT = 56 * 1024 * 1024


def _params(sem, vmem=VMEM_LIMIT):
    return pltpu.CompilerParams(dimension_semantics=sem, vmem_limit_bytes=vmem)


def _sigmoid(x):
    return 1.0 / (1.0 + jnp.exp(-x))


def _silu(x):
    return x * _sigmoid(x)


def _split3(x):
    hi = x.astype(BF16)
    r = x - hi.astype(F32)
    mid = r.astype(BF16)
    lo = (r - mid.astype(F32)).astype(BF16)
    return hi, mid, lo


def _dot(a, b):
    return jnp.dot(a, b, preferred_element_type=F32)


def _dot_nt(a, b):
    return lax.dot_general(a, b, (((1,), (1,)), ((), ())), preferred_element_type=F32)


def _dot_exact_rhs(a, b_exact):
    hi, mid, lo = _split3(a)
    return _dot(hi, b_exact) + _dot(mid, b_exact) + _dot(lo, b_exact)


def _dot_exact_lhs(a_exact, b):
    hi, mid, lo = _split3(b)
    return _dot(a_exact, hi) + _dot(a_exact, mid) + _dot(a_exact, lo)


def _adaln_kernel(c_ref, w_ref, b_ref, o_ref):
    s = _silu(c_ref[...]).astype(BF16)
    o_ref[...] = _dot(s, w_ref[...].astype(BF16)) + b_ref[...]


def _adaln(c, w_ada, b_ada):
    b, d = c.shape
    n = w_ada.shape[1]
    rows = SUBLANES
    c_pad = jnp.zeros((rows, d), F32).at[:b].set(c)
    tn = 1024
    out = pl.pallas_call(
        _adaln_kernel,
        grid=(n // tn,),
        in_specs=[pl.BlockSpec((rows, d), lambda j: (0, 0)),
                  pl.BlockSpec((d, tn), lambda j: (0, j)),
                  pl.BlockSpec((1, tn), lambda j: (0, j))],
        out_specs=pl.BlockSpec((rows, tn), lambda j: (0, j)),
        out_shape=jax.ShapeDtypeStruct((rows, n), F32),
        compiler_params=_params(("arbitrary",)),
        name="adaln",
    )(c_pad, w_ada, b_ada.reshape(1, n))
    return out[:b]


def _inproj_kernel(x_ref, shift_ref, scale_ref, g_ref, wqkv_ref, wz_ref, wxbc_ref, wdt_ref,
                   qg_ref, kg_ref, hmean_ref, q_ref, k_ref, v_ref, z_ref, xbc_ref, dt_ref):
    x = x_ref[...]
    ms = jnp.mean(x * x, axis=-1, keepdims=True)
    h = x * lax.rsqrt(ms + NORM_EPS) * g_ref[...]
    h = h * (1.0 + scale_ref[...]) + shift_ref[...]
    hb = h.astype(BF16)

    hmean = hmean_ref[...]

    def head_norm(t, gain):
        ss = _dot_exact_rhs(t * t, hmean)
        return t * lax.rsqrt(ss + NORM_EPS) * gain

    q = _dot(hb, wqkv_ref[:, 0:ATTN_WIDTH])
    q_ref[...] = head_norm(q, qg_ref[...]) * (HEAD_DIM ** -0.5)
    k = _dot(hb, wqkv_ref[:, ATTN_WIDTH:2 * ATTN_WIDTH])
    k_ref[...] = head_norm(k, kg_ref[...])
    v_ref[...] = _dot(hb, wqkv_ref[:, 2 * ATTN_WIDTH:3 * ATTN_WIDTH])
    for c0 in range(0, SSM_WIDTH, 512):
        z_ref[:, c0:c0 + 512] = _dot(hb, wz_ref[:, c0:c0 + 512])
    for c0 in range(0, CONV_CH, 512):
        xbc_ref[:, c0:c0 + 512] = _dot(hb, wxbc_ref[:, c0:c0 + 512])
    dt_ref[...] = _dot(hb, wdt_ref[...])


def _in_proj(x, shift, scale, gain, w_in, q_gain, k_gain, tm=256):
    b, s, d = x.shape
    t = b * s
    tiles_per_seq = s // tm
    w = w_in.astype(BF16)
    o_z = 3 * ATTN_WIDTH
    o_x = o_z + SSM_WIDTH
    o_dt = o_x + CONV_CH
    w_qkv, w_z, w_xbc = w[:, :o_z], w[:, o_z:o_x], w[:, o_x:o_dt]
    w_dt = jnp.zeros((d, LANES), BF16).at[:, :SSM_HEADS].set(w[:, o_dt:])
    head_of = np.arange(ATTN_WIDTH) // HEAD_DIM
    hmean = jnp.asarray((head_of[:, None] == head_of[None, :]).astype(np.float32) / HEAD_DIM, BF16)
    full = lambda shp: pl.BlockSpec(shp, lambda i: (0,) * len(shp))
    row = lambda n: pl.BlockSpec((tm, n), lambda i: (i, 0))
    per_batch = pl.BlockSpec((None, 1, d), lambda i: (i // tiles_per_seq, 0, 0))
    outs = pl.pallas_call(
        _inproj_kernel,
        grid=(t // tm,),
        in_specs=[row(d), per_batch, per_batch, full((1, d)),
                  full((d, o_z)), full((d, SSM_WIDTH)), full((d, CONV_CH)), full((d, LANES)),
                  full((1, ATTN_WIDTH)), full((1, ATTN_WIDTH)), full((ATTN_WIDTH, ATTN_WIDTH))],
        out_specs=[row(ATTN_WIDTH), row(ATTN_WIDTH), row(ATTN_WIDTH), row(SSM_WIDTH), row(CONV_CH), row(LANES)],
        out_shape=[jax.ShapeDtypeStruct((t, n), F32)
                   for n in (ATTN_WIDTH, ATTN_WIDTH, ATTN_WIDTH, SSM_WIDTH, CONV_CH, LANES)],
        compiler_params=_params(("arbitrary",)),
        name="in_proj",
    )(x.reshape(t, d), shift.reshape(b, 1, d), scale.reshape(b, 1, d), gain.reshape(1, d),
      w_qkv, w_z, w_xbc, w_dt,
      jnp.tile(q_gain, ATTN_HEADS).reshape(1, ATTN_WIDTH), jnp.tile(k_gain, ATTN_HEADS).reshape(1, ATTN_WIDTH), hmean)
    return outs


def _t5_causal_buckets(distance):
    n = np.maximum(distance, 0)
    max_exact = REL_BUCKETS // 2
    large = max_exact + (np.log(np.maximum(n, 1) / max_exact) / math.log(REL_MAX_DISTANCE / max_exact)
                         * (REL_BUCKETS - max_exact)).astype(np.int64)
    large = np.minimum(large, REL_BUCKETS - 1)
    return np.where(n < max_exact, n, large).astype(np.int32)


def _window_bias(rel_bias_table, dilation):
    qi = np.arange(WIN_STEPS)[:, None]
    kj = np.arange(2 * WIN_STEPS)[None, :]
    dist = qi + WIN_STEPS - kj
    band = (dist >= 0) & (dist <= WIN_STEPS)
    onehot = (_t5_causal_buckets(dist * dilation).reshape(-1, 1) == np.arange(REL_BUCKETS)[None, :]).astype(np.float32)
    bias = jnp.dot(rel_bias_table.astype(F32).T, jnp.asarray(onehot).T, precision=lax.Precision.HIGHEST)
    bias = bias.reshape(ATTN_HEADS, WIN_STEPS, 2 * WIN_STEPS)
    return jnp.where(jnp.asarray(band)[None], bias, NEG_BIG)


ATTN_TOKENS = max(w for w, _ in PATTERNS)
ATTN_UNROLL = 8


def _attn_kernel(q_ref, kp_ref, kc_ref, vp_ref, vc_ref, bias_ref, out_ref, kw, vw, o_acc, l_acc):
    tb = ATTN_TOKENS
    first = pl.program_id(2) == 0
    kw[0:tb] = kp_ref[...]
    kw[tb:2 * tb] = kc_ref[...]
    vw[0:tb] = vp_ref[...]
    vw[tb:2 * tb] = vc_ref[...]
    lane = lax.broadcasted_iota(I32, (WIN_STEPS, LANES), 1)
    head0 = lane < HEAD_DIM
    col = lax.broadcasted_iota(I32, (WIN_STEPS, 2 * WIN_STEPS), 1)
    in_prev = col < WIN_STEPS

    for p, (_, d) in enumerate(PATTERNS):
        shift = d.bit_length() - 1
        n_blocks = tb // WIN_STEPS

        def rows(start, n, d=d):
            return pl.ds(start, n, stride=d) if d > 1 else pl.ds(start, n)

        def body(it, carry, p=p, d=d, shift=shift, rows=rows):
            for u in range(ATTN_UNROLL):
                idx = it * ATTN_UNROLL + u
                r = jnp.bitwise_and(idx, d - 1)
                j = jnp.right_shift(idx, shift)
                qs = j * (WIN_STEPS * d) + r
                q = q_ref[rows(qs, WIN_STEPS), :]
                k = kw[rows(tb + qs - WIN_STEPS * d, 2 * WIN_STEPS), :].astype(BF16)
                v = vw[rows(tb + qs - WIN_STEPS * d, 2 * WIN_STEPS), :].astype(BF16)
                no_prev = jnp.logical_and(in_prev, jnp.logical_and(first, j == 0))
                o_h, lse_h = [], []
                for h in range(2):
                    qh = jnp.where(head0 if h == 0 else jnp.logical_not(head0), q, 0.0).astype(BF16)
                    s = _dot_nt(qh, k) + bias_ref[p, h]
                    s = jnp.where(no_prev, NEG_BIG, s)
                    m = jnp.max(s, axis=-1, keepdims=True)
                    e = jnp.exp(s - m)
                    denom = jnp.sum(e, axis=-1, keepdims=True)
                    o_h.append(_dot(e.astype(BF16), v) / denom)
                    lse_h.append(m + jnp.log(denom))
                o_acc[p, rows(qs, WIN_STEPS), :] = jnp.where(head0, o_h[0], o_h[1])
                l_acc[p, rows(qs, WIN_STEPS), :] = jnp.where(head0, lse_h[0], lse_h[1])
            return carry

        lax.fori_loop(0, n_blocks // ATTN_UNROLL, body, 0)

    chunk = 256
    for c0 in range(0, tb, chunk):
        l1, l2, l3 = (l_acc[p, c0:c0 + chunk, :] for p in range(3))
        m = jnp.maximum(jnp.maximum(l1, l2), l3)
        e1, e2, e3 = jnp.exp(l1 - m), jnp.exp(l2 - m), jnp.exp(l3 - m)
        num = e1 * o_acc[0, c0:c0 + chunk, :] + e2 * o_acc[1, c0:c0 + chunk, :] + e3 * o_acc[2, c0:c0 + chunk, :]
        out_ref[c0:c0 + chunk, :] = num / (e1 + e2 + e3)


def _attention(q, k, v, bias):
    b, s, w = q.shape
    tb = ATTN_TOKENS
    pairs = ATTN_HEADS // 2
    cur = pl.BlockSpec((None, tb, LANES), lambda bi, hp, i: (bi, i, hp))
    prev = pl.BlockSpec((None, tb, LANES), lambda bi, hp, i: (bi, jnp.maximum(i - 1, 0), hp))
    return pl.pallas_call(
        _attn_kernel,
        grid=(b, pairs, s // tb),
        in_specs=[cur, prev, cur, prev, cur,
                  pl.BlockSpec((len(PATTERNS), 2, WIN_STEPS, 2 * WIN_STEPS), lambda bi, hp, i: (0, hp, 0, 0))],
        out_specs=cur,
        out_shape=jax.ShapeDtypeStruct((b, s, w), F32),
        scratch_shapes=[pltpu.VMEM((2 * tb, LANES), F32), pltpu.VMEM((2 * tb, LANES), F32),
                        pltpu.VMEM((len(PATTERNS), tb, LANES), F32), pltpu.VMEM((len(PATTERNS), tb, LANES), F32)],
        compiler_params=_params(("arbitrary",) * 3),
        name="attention",
    )(q, k, k, v, v, bias)


def _ssd_kernel(xbc_ref, halo_ref, z_ref, dtraw_ref, convw_ref, convb_ref, dtb_ref, alog_ref, dskip_ref, gain_ref,
                expand_ref, tril_ref, y_ref, state_ref):
    c = pl.program_id(1)

    @pl.when(c == 0)
    def _():
        state_ref[...] = jnp.zeros_like(state_ref)

    x = xbc_ref[...]
    halo = jnp.where(c == 0, 0.0, halo_ref[...])
    w = convw_ref[...]
    acc = x * w[SSM_CONV - 1:SSM_CONV, :] + convb_ref[...]
    row8 = lax.broadcasted_iota(I32, (SUBLANES, CONV_CH), 0)
    for shift in range(1, SSM_CONV):
        xs = pltpu.roll(x, shift, axis=0)
        hs = pltpu.roll(halo, shift, axis=0)
        head = jnp.where(row8 < shift, hs, xs[0:SUBLANES])
        xs = jnp.concatenate([head, xs[SUBLANES:]], axis=0)
        acc = acc + xs * w[SSM_CONV - 1 - shift:SSM_CONV - shift, :]
    act = _silu(acc)
    x_s = act[:, :SSM_WIDTH]
    bc0 = SSM_WIDTH
    cc0 = SSM_WIDTH + SSM_GROUPS * SSM_STATE

    t = dtraw_ref[...] + dtb_ref[...]
    dt = jnp.maximum(t, 0.0) + jnp.log(1.0 + jnp.exp(-jnp.abs(t)))
    a = dt * (-jnp.exp(alog_ref[...]))
    a_cs = _dot_exact_lhs(tril_ref[...], a)
    a_cs_t = a_cs.T
    a_last = a_cs[SSM_CHUNK - 1:SSM_CHUNK, :]
    expand = expand_ref[...]
    dt_e = _dot_exact_rhs(dt, expand)
    ea_e = _dot_exact_rhs(jnp.exp(a_cs), expand)
    dte_e = _dot_exact_rhs(jnp.exp(a_last - a_cs), expand)
    xdt = x_s * dt_e
    xw = (xdt * dte_e).astype(BF16)
    xdt_b = xdt.astype(BF16)

    li = lax.broadcasted_iota(I32, (SSM_CHUNK, SSM_CHUNK), 0)
    si = lax.broadcasted_iota(I32, (SSM_CHUNK, SSM_CHUNK), 1)
    causal = li >= si

    ys = []
    for g in range(SSM_GROUPS):
        gs = slice(g * GROUP_WIDTH, (g + 1) * GROUP_WIDTH)
        b_g = act[:, bc0 + g * SSM_STATE:bc0 + (g + 1) * SSM_STATE]
        c_g = act[:, cc0 + g * SSM_STATE:cc0 + (g + 1) * SSM_STATE].astype(BF16)
        cb = _dot_nt(c_g, b_g.astype(BF16))
        state = state_ref[g]
        y_off = _dot(c_g, state.astype(BF16)) * ea_e[:, gs]
        parts = []
        for j in range(HEADS_PER_GROUP):
            hh = g * HEADS_PER_GROUP + j
            seg = a_cs[:, hh:hh + 1] - a_cs_t[hh:hh + 1, :]
            decay = jnp.exp(jnp.where(causal, seg, NEG_BIG))
            m = (cb * decay).astype(BF16)
            parts.append(_dot(m, xdt_b[:, hh * SSM_HEAD_DIM:(hh + 1) * SSM_HEAD_DIM]))
        ys.append(jnp.concatenate(parts, axis=1) + y_off)
        state_ref[g] = state * ea_e[SSM_CHUNK - 1:SSM_CHUNK, gs] + _dot(b_g.T.astype(BF16), xw[:, gs])
    y = jnp.concatenate(ys, axis=1) + dskip_ref[...] * x_s
    y = y * _silu(z_ref[...])
    gain = gain_ref[...]
    for g in range(SSM_GROUPS):
        gs = slice(g * GROUP_WIDTH, (g + 1) * GROUP_WIDTH)
        yg = y[:, gs]
        ms = jnp.mean(yg * yg, axis=-1, keepdims=True)
        y_ref[:, gs] = yg * lax.rsqrt(ms + NORM_EPS) * gain[:, gs]


def _ssd(xbc, z, dt_raw, conv_w, conv_b, dt_bias, a_log, d_skip, norm_gain, b, s):
    t = b * s
    nc = s // SSM_CHUNK
    pad_heads = lambda v: jnp.zeros((1, LANES), F32).at[0, :SSM_HEADS].set(v)
    head_of_lane = np.arange(SSM_WIDTH) // SSM_HEAD_DIM
    expand = jnp.asarray((np.arange(LANES)[:, None] == head_of_lane[None, :]).astype(np.float32), BF16)
    tril = jnp.asarray(np.tril(np.ones((SSM_CHUNK, SSM_CHUNK), np.float32)), BF16)
    halo_blocks = SSM_CHUNK // SUBLANES
    chunk = lambda n: pl.BlockSpec((SSM_CHUNK, n), lambda bi, c: (bi * nc + c, 0))
    full = lambda shp: pl.BlockSpec(shp, lambda bi, c: (0,) * len(shp))
    halo = pl.BlockSpec((SUBLANES, CONV_CH), lambda bi, c: (jnp.maximum((bi * nc + c) * halo_blocks - 1, 0), 0))
    return pl.pallas_call(
        _ssd_kernel,
        grid=(b, nc),
        in_specs=[chunk(CONV_CH), halo, chunk(SSM_WIDTH), chunk(LANES),
                  full((SSM_CONV, CONV_CH)), full((1, CONV_CH)), full((1, LANES)), full((1, LANES)),
                  full((1, SSM_WIDTH)), full((1, SSM_WIDTH)), full((LANES, SSM_WIDTH)), full((SSM_CHUNK, SSM_CHUNK))],
        out_specs=chunk(SSM_WIDTH),
        out_shape=jax.ShapeDtypeStruct((t, SSM_WIDTH), F32),
        scratch_shapes=[pltpu.VMEM((SSM_GROUPS, SSM_STATE, GROUP_WIDTH), F32)],
        compiler_params=_params(("arbitrary", "arbitrary")),
        name="ssd",
    )(xbc, xbc, z, dt_raw, conv_w, conv_b.reshape(1, CONV_CH), pad_heads(dt_bias), pad_heads(a_log),
      jnp.repeat(d_skip, SSM_HEAD_DIM).reshape(1, SSM_WIDTH), norm_gain.reshape(1, SSM_WIDTH), expand, tril)


U32 = jnp.uint32
TOKEN_ROWS = D_MODEL // (2 * LANES)
HIGH_HALF = np.uint32(0xFFFF0000)


def _to_token_tiles(ref, x):
    n = x.shape[0]
    for c in range(TOKEN_ROWS):
        lo = lax.bitcast_convert_type(x[:, c * LANES:(c + 1) * LANES].astype(BF16).astype(F32), U32)
        hi = lax.bitcast_convert_type(x[:, (c + TOKEN_ROWS) * LANES:(c + TOKEN_ROWS + 1) * LANES]
                                      .astype(BF16).astype(F32), U32)
        ref[pl.ds(c, n, stride=TOKEN_ROWS), :] = jnp.bitwise_or(jnp.right_shift(lo, 16), jnp.bitwise_and(hi, HIGH_HALF))


def _from_token_tiles(ref, n, token0=0):
    lows, highs = [], []
    for c in range(TOKEN_ROWS):
        word = ref[pl.ds(token0 * TOKEN_ROWS + c, n, stride=TOKEN_ROWS), :]
        lows.append(lax.bitcast_convert_type(jnp.left_shift(word, 16), F32))
        highs.append(lax.bitcast_convert_type(jnp.bitwise_and(word, HIGH_HALF), F32))
    return jnp.concatenate(lows + highs, axis=1)


def _outproj_kernel(attn_ref, ssm_ref, x_ref, gate_ref, shift_ref, scale_ref, g_ref, wa_ref, ws_ref, x1_ref, h2_ref,
                    h2t_ref):
    mixed = _dot(attn_ref[...].astype(BF16), wa_ref[...]) + _dot(ssm_ref[...].astype(BF16), ws_ref[...])
    x1 = x_ref[...] + gate_ref[...] * mixed
    x1_ref[...] = x1
    ms = jnp.mean(x1 * x1, axis=-1, keepdims=True)
    h = x1 * lax.rsqrt(ms + NORM_EPS) * g_ref[...]
    h2 = h * (1.0 + scale_ref[...]) + shift_ref[...]
    h2_ref[...] = h2
    _to_token_tiles(h2t_ref, h2)


def _out_proj(attn, ssm, x, gate, shift, scale, gain, w_out, b, s, tm=256):
    t = b * s
    d = D_MODEL
    tiles_per_seq = s // tm
    w = w_out.astype(BF16)
    row = lambda n: pl.BlockSpec((tm, n), lambda i: (i, 0))
    full = lambda shp: pl.BlockSpec(shp, lambda i: (0,) * len(shp))
    per_batch = pl.BlockSpec((None, 1, d), lambda i: (i // tiles_per_seq, 0, 0))
    return pl.pallas_call(
        _outproj_kernel,
        grid=(t // tm,),
        in_specs=[row(ATTN_WIDTH), row(SSM_WIDTH), row(d), per_batch, per_batch, per_batch, full((1, d)),
                  full((ATTN_WIDTH, d)), full((SSM_WIDTH, d))],
        out_specs=[row(d), row(d), pl.BlockSpec((tm * TOKEN_ROWS, LANES), lambda i: (i, 0))],
        out_shape=[jax.ShapeDtypeStruct((t, d), F32)] * 2 + [jax.ShapeDtypeStruct((t * TOKEN_ROWS, LANES), U32)],
        compiler_params=_params(("arbitrary",)),
        name="out_proj",
    )(attn.reshape(t, ATTN_WIDTH), ssm, x.reshape(t, d),
      gate.reshape(b, 1, d), shift.reshape(b, 1, d), scale.reshape(b, 1, d), gain.reshape(1, d),
      w[:ATTN_WIDTH], w[ATTN_WIDTH:])


def _mixer_sublayer(x, mod, norm_mix_gain, w_in, q_norm_gain, k_norm_gain, rel_bias_table, conv_w, conv_b, dt_bias,
                    a_log, d_skip, ssm_norm_gain, w_out, norm_ffn_gain):
    b, s, d = x.shape
    shift_m, scale_m, gate_m, shift_f, scale_f, _ = jnp.split(mod, 6, axis=-1)
    q, k, v, z, xbc, dt_raw = _in_proj(x, shift_m, scale_m, norm_mix_gain, w_in, q_norm_gain, k_norm_gain)
    bias = jnp.stack([_window_bias(rel_bias_table, dilation) for _, dilation in PATTERNS])
    attn = _attention(q.reshape(b, s, ATTN_WIDTH), k.reshape(b, s, ATTN_WIDTH), v.reshape(b, s, ATTN_WIDTH), bias)
    ssm = _ssd(xbc, z, dt_raw, conv_w, conv_b, dt_bias, a_log, d_skip, ssm_norm_gain, b, s)
    return _out_proj(attn, ssm, x, gate_m, shift_f, scale_f, norm_ffn_gain, w_out, b, s)


def _first_argmax(v, iota, limit):
    m = jnp.max(v, axis=0, keepdims=True)
    idx = jnp.min(jnp.where(v == m, iota, limit), axis=0, keepdims=True)
    return m, idx


def _router_kernel(h_ref, wt_ref, bias_ref, upper_ref, eidx_ref, rank_ref, gate_ref, counts_ref, carry_ref):
    @pl.when(pl.program_id(0) == 0)
    def _():
        carry_ref[...] = jnp.zeros_like(carry_ref)

    tm = h_ref.shape[0]
    h = h_ref[...]
    wt = wt_ref[...]
    h_hi = h.astype(BF16)
    h_lo = (h - h_hi.astype(F32)).astype(BF16)
    w_hi = wt.astype(BF16)
    w_lo = (wt - w_hi.astype(F32)).astype(BF16)
    logits = _dot_nt(w_hi, h_hi) + _dot_nt(w_hi, h_lo) + _dot_nt(w_lo, h_hi)
    scores = _sigmoid(logits)
    choice = scores + bias_ref[...]
    neg_inf = -jnp.inf

    iota_g = lax.broadcasted_iota(I32, (EXPERTS_PER_GROUP, tm), 0).astype(F32)
    group_rows = []
    for g in range(N_EXPERT_GROUPS):
        v = choice[g * EXPERTS_PER_GROUP:(g + 1) * EXPERTS_PER_GROUP]
        m1, i1 = _first_argmax(v, iota_g, float(EXPERTS_PER_GROUP))
        m2 = jnp.max(jnp.where(iota_g == i1, neg_inf, v), axis=0, keepdims=True)
        group_rows.append(m1 + m2)
    group_scores = jnp.concatenate(group_rows, axis=0)

    iota_n = lax.broadcasted_iota(I32, (N_EXPERT_GROUPS, tm), 0).astype(F32)
    chosen = jnp.zeros((N_EXPERT_GROUPS, tm), F32)
    for _ in range(TOPK_GROUPS):
        _, gi = _first_argmax(group_scores, iota_n, float(N_EXPERT_GROUPS))
        hit = iota_n == gi
        chosen = jnp.where(hit, 1.0, chosen)
        group_scores = jnp.where(hit, neg_inf, group_scores)

    masked = jnp.concatenate(
        [jnp.where(chosen[g:g + 1] > 0.0, choice[g * EXPERTS_PER_GROUP:(g + 1) * EXPERTS_PER_GROUP], neg_inf)
         for g in range(N_EXPERT_GROUPS)], axis=0)

    iota_e = lax.broadcasted_iota(I32, (N_EXPERTS, tm), 0).astype(F32)
    picked, gates = [], []
    onehot = jnp.zeros((N_EXPERTS, tm), F32)
    for _ in range(TOP_K):
        _, ei = _first_argmax(masked, iota_e, float(N_EXPERTS))
        hit = iota_e == ei
        gates.append(jnp.sum(jnp.where(hit, scores, 0.0), axis=0, keepdims=True))
        masked = jnp.where(hit, neg_inf, masked)
        onehot = jnp.where(hit, 1.0, onehot)
        picked.append(ei)
    gate_sum = gates[0]
    for gk in gates[1:]:
        gate_sum = gate_sum + gk

    base = _dot(onehot.astype(BF16), upper_ref[...]) + carry_ref[...]
    ranks = [jnp.sum(jnp.where(iota_e == ei, base, 0.0), axis=0, keepdims=True) for ei in picked]
    carry_ref[...] = carry_ref[...] + jnp.sum(onehot, axis=1, keepdims=True)

    eidx_ref[...] = jnp.concatenate(picked, axis=0).astype(I32)
    rank_ref[...] = jnp.concatenate(ranks, axis=0).astype(I32)
    gate_ref[...] = jnp.concatenate([gk / gate_sum * ROUTED_SCALE for gk in gates], axis=0)
    counts_ref[...] = carry_ref[...].astype(I32)


def _router(h2, w_router, router_bias, tm=256):
    t, d = h2.shape
    upper = jnp.asarray(np.triu(np.ones((tm, tm), np.float32), 1), BF16)
    tok = pl.BlockSpec((TOP_K, tm), lambda i: (0, i))
    full = lambda shp: pl.BlockSpec(shp, lambda i: (0,) * len(shp))
    return pl.pallas_call(
        _router_kernel,
        grid=(t // tm,),
        in_specs=[pl.BlockSpec((tm, d), lambda i: (i, 0)), full((N_EXPERTS, d)), full((N_EXPERTS, 1)), full((tm, tm))],
        out_specs=[tok, tok, tok, full((N_EXPERTS, 1))],
        out_shape=[jax.ShapeDtypeStruct((TOP_K, t), I32), jax.ShapeDtypeStruct((TOP_K, t), I32),
                   jax.ShapeDtypeStruct((TOP_K, t), F32), jax.ShapeDtypeStruct((N_EXPERTS, 1), I32)],
        scratch_shapes=[pltpu.VMEM((N_EXPERTS, 1), F32)],
        compiler_params=_params(("arbitrary",)),
        name="router",
    )(h2, w_router.T, router_bias.reshape(N_EXPERTS, 1), upper)


def _positions_kernel(counts_ref, lower_ref, eidx_ref, rank_ref, pos_ref):
    tm = eidx_ref.shape[1]
    counts = jnp.broadcast_to(counts_ref[...].astype(F32), (N_EXPERTS, LANES))
    offsets = _dot_exact_lhs(lower_ref[...], counts)[:, 0:1]
    iota_e = lax.broadcasted_iota(I32, (N_EXPERTS, tm), 0).astype(F32)
    e = eidx_ref[...].astype(F32)
    rows = [jnp.sum(jnp.where(iota_e == e[k:k + 1], offsets, 0.0), axis=0, keepdims=True) for k in range(TOP_K)]
    pos_ref[0] = jnp.concatenate(rows, axis=0).astype(I32) + rank_ref[...]


def _positions(counts, eidx, rank, tm):
    t = eidx.shape[1]
    lower = jnp.asarray(np.tril(np.ones((N_EXPERTS, N_EXPERTS), np.float32), -1), BF16)
    tok = pl.BlockSpec((TOP_K, tm), lambda i: (0, i))
    return pl.pallas_call(
        _positions_kernel,
        grid=(t // tm,),
        in_specs=[pl.BlockSpec((N_EXPERTS, 1), lambda i: (0, 0)), pl.BlockSpec((N_EXPERTS, N_EXPERTS), lambda i: (0, 0)),
                  tok, tok],
        out_specs=pl.BlockSpec((1, TOP_K, tm), lambda i: (i, 0, 0)),
        out_shape=jax.ShapeDtypeStruct((t // tm, TOP_K, tm), I32),
        compiler_params=_params(("arbitrary",)),
        name="positions",
    )(counts, lower, eidx, rank)


def _token_rows(i):
    return pl.ds(pl.multiple_of(i * TOKEN_ROWS, TOKEN_ROWS), TOKEN_ROWS)


def _dispatch_kernel(pos_ref, h_ref, xs_ref, sem):
    tm = h_ref.shape[0] // TOKEN_ROWS
    sub = pos_ref.shape[2]
    shift = sub.bit_length() - 1

    def row_copy(t, k):
        p = pos_ref[jnp.right_shift(t, shift), k, jnp.bitwise_and(t, sub - 1)]
        return pltpu.make_async_copy(h_ref.at[_token_rows(t)], xs_ref.at[_token_rows(p)], sem)

    def start(t, carry):
        for k in range(TOP_K):
            row_copy(t, k).start(priority=k % 2)
        return carry

    def wait(t, carry):
        for k in range(TOP_K):
            row_copy(t, k).wait()
        return carry

    lax.fori_loop(0, tm, start, 0)
    lax.fori_loop(0, tm, wait, 0)


def _dispatch(h2t, pos_tiles, tm=256):
    t = h2t.shape[0] // TOKEN_ROWS
    sub = pos_tiles.shape[2]
    return pl.pallas_call(
        _dispatch_kernel,
        grid=(t // tm,),
        in_specs=[pl.BlockSpec((tm // sub, TOP_K, sub), lambda i: (i, 0, 0), memory_space=pltpu.SMEM),
                  pl.BlockSpec((tm * TOKEN_ROWS, LANES), lambda i: (i, 0))],
        out_specs=pl.BlockSpec(memory_space=pl.ANY),
        out_shape=jax.ShapeDtypeStruct(((t * TOP_K + EXPERT_BLOCK) * TOKEN_ROWS, LANES), h2t.dtype),
        scratch_shapes=[pltpu.SemaphoreType.DMA(())],
        compiler_params=_params(("arbitrary",)),
        name="dispatch",
    )(pos_tiles, h2t)


EXPERT_BLOCK = 256
TAIL_PIECES = tuple(1 << i for i in reversed(range(EXPERT_BLOCK.bit_length() - 1)))
X_SLOTS = 4
Y_SLOTS = 4


def _experts_kernel(start_ref, count_ref, nxt_ref, slot_ref, first_ref, blk0_ref, full0_ref, ptail_ref, ltail_ref,
                    blktok_ref, nblocks_ref, xs_hbm, wg_hbm, wu_hbm, wd_hbm, ys_hbm,
                    wg_buf, wu_buf, wd_buf, wg_bf, wu_bf, wd_bf, xbuf, ybuf, ytail, wsem, xsem, ysem, tsem):
    e = pl.program_id(0)
    last_step = e == pl.num_programs(0) - 1
    start, count = start_ref[e], count_ref[e]
    n_full = jnp.right_shift(count, EXPERT_BLOCK.bit_length() - 1)
    tail = jnp.bitwise_and(count, EXPERT_BLOCK - 1)
    n_blk = n_full + (tail > 0).astype(I32)
    blk0, full0 = blk0_ref[e], full0_ref[e]
    slot, nxt = slot_ref[e], nxt_ref[e]

    def token_rows(token, n):
        return pl.ds(pl.multiple_of(token * TOKEN_ROWS, TOKEN_ROWS), n * TOKEN_ROWS)

    def fetch(ex, s):
        return (pltpu.make_async_copy(wg_hbm.at[ex], wg_buf.at[s], wsem.at[s, 0]),
                pltpu.make_async_copy(wu_hbm.at[ex], wu_buf.at[s], wsem.at[s, 1]),
                pltpu.make_async_copy(wd_hbm.at[ex], wd_buf.at[s], wsem.at[s, 2]))

    def x_copy(token, s):
        return pltpu.make_async_copy(xs_hbm.at[token_rows(token, EXPERT_BLOCK)], xbuf.at[s], xsem.at[s])

    def y_copy(token, s):
        return pltpu.make_async_copy(ybuf.at[s], ys_hbm.at[token_rows(token, EXPERT_BLOCK)], ysem.at[s])

    def tail_copies(token, length):
        out = []
        for piece in TAIL_PIECES:
            bigger = (EXPERT_BLOCK - 1) & ~(2 * piece - 1)
            done = jnp.bitwise_and(length, bigger)
            cp = pltpu.make_async_copy(ytail.at[token_rows(done, piece)], ys_hbm.at[token_rows(token + done, piece)],
                                       tsem)
            out.append((jnp.bitwise_and(length, piece) != 0, cp))
        return out

    def block(s):
        x = _from_token_tiles(xbuf.at[s], EXPERT_BLOCK).astype(BF16)
        g = _dot(x, wg_bf[...])
        u = _dot(x, wu_bf[...])
        return _dot((_silu(g) * u).astype(BF16), wd_bf[...])

    @pl.when(count > 0)
    def _():
        @pl.when(first_ref[e] == 1)
        def _():
            for g in range(X_SLOTS - 1):
                @pl.when(g < nblocks_ref[0])
                def _(g=g):
                    x_copy(blktok_ref[g], g).start()
            for cp in fetch(e, slot):
                cp.start()

        for cp in fetch(e, slot):
            cp.wait()

        @pl.when(nxt >= 0)
        def _():
            for cp in fetch(nxt, 1 - slot):
                cp.start()

        wg_bf[...] = wg_buf[slot].astype(BF16)
        wu_bf[...] = wu_buf[slot].astype(BF16)
        wd_bf[...] = wd_buf[slot].astype(BF16)

        def step(i):
            g = blk0 + i
            xs_slot = jnp.bitwise_and(g, X_SLOTS - 1)
            x_copy(start, xs_slot).wait()
            ahead = g + (X_SLOTS - 1)

            @pl.when(ahead < nblocks_ref[0])
            def _():
                x_copy(blktok_ref[ahead], jnp.bitwise_and(ahead, X_SLOTS - 1)).start()

            return block(xs_slot)

        def full_block(i, carry):
            y = step(i)
            j = full0 + i
            ys_slot = jnp.bitwise_and(j, Y_SLOTS - 1)

            @pl.when(j >= Y_SLOTS)
            def _():
                y_copy(start, ys_slot).wait()

            _to_token_tiles(ybuf.at[ys_slot], y)
            y_copy(start + i * EXPERT_BLOCK, ys_slot).start()
            return carry

        lax.fori_loop(0, n_full, full_block, 0)

        @pl.when(tail > 0)
        def _():
            y = step(n_full)
            for pred, cp in tail_copies(start, ptail_ref[e]):
                @pl.when(pred)
                def _(cp=cp):
                    cp.wait()
            _to_token_tiles(ytail, y)
            for pred, cp in tail_copies(start + n_full * EXPERT_BLOCK, tail):
                @pl.when(pred)
                def _(cp=cp):
                    cp.start()

    @pl.when(last_step)
    def _():
        total_full = full0 + n_full
        for back in range(1, Y_SLOTS + 1):
            @pl.when(total_full >= back)
            def _(back=back):
                y_copy(0, jnp.bitwise_and(total_full - back, Y_SLOTS - 1)).wait()
        for pred, cp in tail_copies(0, ltail_ref[0]):
            @pl.when(pred)
            def _(cp=cp):
                cp.wait()


def _max_expert_blocks(n_rows):
    return n_rows // EXPERT_BLOCK + N_EXPERTS


def _expert_metadata(counts, n_rows):
    ids = jnp.arange(N_EXPERTS, dtype=I32)
    used = counts > 0
    starts = jnp.cumsum(counts) - counts
    n_blk = (counts + EXPERT_BLOCK - 1) // EXPERT_BLOCK
    n_full = counts // EXPERT_BLOCK
    tail = counts % EXPERT_BLOCK
    blk0 = jnp.cumsum(n_blk) - n_blk
    full0 = jnp.cumsum(n_full) - n_full
    next_used = lax.cummin(jnp.where(used, ids, N_EXPERTS), reverse=True)
    next_after = jnp.concatenate([next_used[1:], jnp.full((1,), N_EXPERTS, I32)])
    nxt = jnp.where(next_after < N_EXPERTS, next_after, -1)
    ordinal = jnp.cumsum(used.astype(I32)) - 1
    slot = ordinal % 2
    first = jnp.logical_and(used, ordinal == 0)
    latest = lax.cummax(jnp.where(tail > 0, ids, -1))
    before = jnp.concatenate([jnp.full((1,), -1, I32), latest[:-1]])
    pick = lambda index, values: jnp.sum(jnp.where(index[:, None] == ids[None, :], values[None, :], 0), axis=1)
    ptail = pick(before, tail)
    ltail = pick(latest[-1:], tail)
    block_ends = jnp.cumsum(n_blk)
    g = jnp.arange(_max_expert_blocks(n_rows), dtype=I32)
    eg = jnp.sum((g[:, None] >= block_ends[None, :]).astype(I32), axis=1)
    blktok = g * EXPERT_BLOCK + pick(eg, starts - blk0 * EXPERT_BLOCK)
    return tuple(v.astype(I32) for v in (starts, counts, nxt, slot, first, blk0, full0, ptail, ltail, blktok,
                                         block_ends[-1:]))


def _experts(xs, counts, w_gate, w_up, w_down):
    d = D_MODEL
    meta = _expert_metadata(counts, xs.shape[0] // TOKEN_ROWS - EXPERT_BLOCK)
    hbm = pl.BlockSpec(memory_space=pl.ANY)
    blk = (EXPERT_BLOCK * TOKEN_ROWS, LANES)
    grid_spec = pltpu.PrefetchScalarGridSpec(
        num_scalar_prefetch=len(meta),
        grid=(N_EXPERTS,),
        in_specs=[hbm, hbm, hbm, hbm],
        out_specs=hbm,
        scratch_shapes=[pltpu.VMEM((2, d, EXPERT_FF), F32), pltpu.VMEM((2, d, EXPERT_FF), F32),
                        pltpu.VMEM((2, EXPERT_FF, d), F32),
                        pltpu.VMEM((d, EXPERT_FF), BF16), pltpu.VMEM((d, EXPERT_FF), BF16),
                        pltpu.VMEM((EXPERT_FF, d), BF16),
                        pltpu.VMEM((X_SLOTS,) + blk, U32), pltpu.VMEM((Y_SLOTS,) + blk, U32), pltpu.VMEM(blk, U32),
                        pltpu.SemaphoreType.DMA((2, 3)), pltpu.SemaphoreType.DMA((X_SLOTS,)),
                        pltpu.SemaphoreType.DMA((Y_SLOTS,)), pltpu.SemaphoreType.DMA(())],
    )
    return pl.pallas_call(
        _experts_kernel,
        grid_spec=grid_spec,
        out_shape=jax.ShapeDtypeStruct(xs.shape, U32),
        compiler_params=_params(("arbitrary",)),
        name="experts",
    )(*meta, xs, w_gate, w_up, w_down)


def _combine_kernel(pos_ref, posn_ref, gates_ref, h_ref, x1_ref, gatef_ref, wg_ref, wu_ref, wd_ref, ys_ref, out_ref,
                    buf, sems):
    i = pl.program_id(0)
    n = pl.num_programs(0)
    tm = h_ref.shape[0]
    slot = jnp.bitwise_and(i, 1)

    def row_copy(p_ref, s, t, k):
        return pltpu.make_async_copy(ys_ref.at[_token_rows(p_ref[0, k, t])], buf.at[s, k, _token_rows(t)], sems.at[s])

    def issue(p_ref, s):
        def start(t, carry):
            for k in range(TOP_K):
                row_copy(p_ref, s, t, k).start(priority=k % 2)
            return carry

        lax.fori_loop(0, tm, start, 0)

    @pl.when(i == 0)
    def _():
        issue(pos_ref, slot)

    @pl.when(i + 1 < n)
    def _():
        issue(posn_ref, 1 - slot)

    hb = h_ref[...].astype(BF16)
    shared = _dot((_silu(_dot(hb, wg_ref[...])) * _dot(hb, wu_ref[...])).astype(BF16), wd_ref[...])

    def wait(t, carry):
        for k in range(TOP_K):
            row_copy(pos_ref, slot, t, k).wait()
        return carry

    lax.fori_loop(0, tm, wait, 0)
    gates = gates_ref[...]
    routed = _from_token_tiles(buf.at[slot, 0], tm) * gates[:, 0:1]
    for k in range(1, TOP_K):
        routed = routed + _from_token_tiles(buf.at[slot, k], tm) * gates[:, k:k + 1]
    out_ref[...] = x1_ref[...] + gatef_ref[...] * (shared + routed)


def _combine(ys, pos_tiles, gates_t, h2, x1, gate_f, w_gate_s, w_up_s, w_down_s, b, s, tm):
    t, d = h2.shape
    tiles_per_seq = s // tm
    n_tiles = t // tm
    row = lambda n: pl.BlockSpec((tm, n), lambda i: (i, 0))
    full = lambda shp: pl.BlockSpec(shp, lambda i: (0,) * len(shp))
    return pl.pallas_call(
        _combine_kernel,
        grid=(n_tiles,),
        in_specs=[pl.BlockSpec((1, TOP_K, tm), lambda i: (i, 0, 0), memory_space=pltpu.SMEM),
                  pl.BlockSpec((1, TOP_K, tm), lambda i: (jnp.minimum(i + 1, n_tiles - 1), 0, 0),
                               memory_space=pltpu.SMEM),
                  row(TOP_K), row(d), row(d),
                  pl.BlockSpec((None, 1, d), lambda i: (i // tiles_per_seq, 0, 0)),
                  full((d, EXPERT_FF)), full((d, EXPERT_FF)), full((EXPERT_FF, d)),
                  pl.BlockSpec(memory_space=pl.ANY)],
        out_specs=row(d),
        out_shape=jax.ShapeDtypeStruct((t, d), F32),
        scratch_shapes=[pltpu.VMEM((2, TOP_K, tm * TOKEN_ROWS, LANES), U32), pltpu.SemaphoreType.DMA((2,))],
        compiler_params=_params(("arbitrary",)),
        name="combine",
    )(pos_tiles, pos_tiles, gates_t, h2, x1, gate_f.reshape(b, 1, d),
      w_gate_s.astype(BF16), w_up_s.astype(BF16), w_down_s.astype(BF16), ys)


def _moe_sublayer(x1, h2, h2t, gate_f, w_router, router_bias, w_gate, w_up, w_down, w_gate_s, w_up_s, w_down_s, b, s,
                  tm=256):
    eidx, rank, gates, counts = _router(h2, w_router, router_bias)
    pos_tiles = _positions(counts, eidx, rank, tm)
    xs = _dispatch(h2t, pos_tiles)
    ys = _experts(xs, counts[:, 0], w_gate, w_up, w_down)
    return _combine(ys, pos_tiles, gates.T, h2, x1, gate_f, w_gate_s, w_up_s, w_down_s, b, s, tm)


def kernel(x, c, w_ada, b_ada, norm_mix_gain, w_in, q_norm_gain, k_norm_gain, rel_bias_table, conv_w, conv_b, dt_bias,
           a_log, d_skip, ssm_norm_gain, w_out, norm_ffn_gain, w_router, router_bias, w_gate_experts, w_up_experts,
           w_down_experts, w_gate_shared, w_up_shared, w_down_shared):
    b, s, d = x.shape
    for layer in range(w_ada.shape[0]):
        mod = _adaln(c, w_ada[layer], b_ada[layer])
        x1, h2, h2t = _mixer_sublayer(x, mod, norm_mix_gain[layer], w_in[layer], q_norm_gain[layer], k_norm_gain[layer],
                                 rel_bias_table, conv_w[layer], conv_b[layer], dt_bias[layer], a_log[layer],
                                 d_skip[layer], ssm_norm_gain[layer], w_out[layer], norm_ffn_gain[layer])
        gate_f = mod[:, 5 * d:]
        out = _moe_sublayer(x1, h2, h2t, gate_f, w_router[layer], router_bias[layer], w_gate_experts[layer],
                            w_up_experts[layer], w_down_experts[layer], w_gate_shared[layer], w_up_shared[layer],
                            w_down_shared[layer], b, s)
        x = out.reshape(b, s, d)
    return x
```

```python
import functools
import math

import numpy as np
import jax
import jax.numpy as jnp
from jax import lax
from jax.experimental import pallas as pl
from jax.experimental.pallas import tpu as pltpu
from jax.experimental.pallas import tpu_sc as plsc

F32 = jnp.float32
BF16 = jnp.bfloat16
I32 = jnp.int32

D_MODEL = 1024
ATTN_HEADS = 8
HEAD_DIM = 64
ATTN_WIDTH = ATTN_HEADS * HEAD_DIM
PATTERNS = ((128, 1), (512, 4), (2048, 16))
WIN_STEPS = 128
REL_BUCKETS = 32
REL_MAX_DISTANCE = 2048
SSM_HEADS = 24
SSM_HEAD_DIM = 64
SSM_WIDTH = SSM_HEADS * SSM_HEAD_DIM
SSM_GROUPS = 4
HEADS_PER_GROUP = SSM_HEADS // SSM_GROUPS
GROUP_WIDTH = SSM_WIDTH // SSM_GROUPS
SSM_STATE = 128
SSM_CONV = 4
SSM_CHUNK = 128
CONV_CH = SSM_WIDTH + 2 * SSM_GROUPS * SSM_STATE
N_EXPERTS = 256
TOP_K = 8
N_EXPERT_GROUPS = 8
EXPERTS_PER_GROUP = N_EXPERTS // N_EXPERT_GROUPS
TOPK_GROUPS = 4
EXPERT_FF = 256
ROUTED_SCALE = 2.5
NORM_EPS = 1e-6

LANES = 128
SUBLANES = 8
NEG_BIG = -1e30
VMEM_LIMIT = 56 * 1024 * 1024


def _params(sem, vmem=VMEM_LIMIT):
    return pltpu.CompilerParams(dimension_semantics=sem, vmem_limit_bytes=vmem)


def _sigmoid(x):
    return 1.0 / (1.0 + jnp.exp(-x))


def _silu(x):
    return x * _sigmoid(x)


def _split3(x):
    hi = x.astype(BF16)
    r = x - hi.astype(F32)
    mid = r.astype(BF16)
    lo = (r - mid.astype(F32)).astype(BF16)
    return hi, mid, lo


def _dot(a, b):
    return jnp.dot(a, b, preferred_element_type=F32)


def _dot_nt(a, b):
    return lax.dot_general(a, b, (((1,), (1,)), ((), ())), preferred_element_type=F32)


def _dot_exact_rhs(a, b_exact):
    hi, mid, lo = _split3(a)
    return _dot(hi, b_exact) + _dot(mid, b_exact) + _dot(lo, b_exact)


def _dot_exact_lhs(a_exact, b):
    hi, mid, lo = _split3(b)
    return _dot(a_exact, hi) + _dot(a_exact, mid) + _dot(a_exact, lo)


def _adaln_kernel(c_ref, w_ref, b_ref, o_ref):
    s = _silu(c_ref[...]).astype(BF16)
    o_ref[...] = _dot(s, w_ref[...].astype(BF16)) + b_ref[...]


def _adaln(c, w_ada, b_ada):
    b, d = c.shape
    n = w_ada.shape[1]
    rows = SUBLANES
    c_pad = jnp.zeros((rows, d), F32).at[:b].set(c)
    tn = 1024
    out = pl.pallas_call(
        _adaln_kernel,
        grid=(n // tn,),
        in_specs=[pl.BlockSpec((rows, d), lambda j: (0, 0)),
                  pl.BlockSpec((d, tn), lambda j: (0, j)),
                  pl.BlockSpec((1, tn), lambda j: (0, j))],
        out_specs=pl.BlockSpec((rows, tn), lambda j: (0, j)),
        out_shape=jax.ShapeDtypeStruct((rows, n), F32),
        compiler_params=_params(("arbitrary",)),
        name="adaln",
    )(c_pad, w_ada, b_ada.reshape(1, n))
    return out[:b]


def _inproj_kernel(x_ref, shift_ref, scale_ref, g_ref, wqkv_ref, wz_ref, wxbc_ref, wdt_ref,
                   qg_ref, kg_ref, hmean_ref, q_ref, k_ref, v_ref, z_ref, xbc_ref, dt_ref):
    x = x_ref[...]
    ms = jnp.mean(x * x, axis=-1, keepdims=True)
    h = x * lax.rsqrt(ms + NORM_EPS) * g_ref[...]
    h = h * (1.0 + scale_ref[...]) + shift_ref[...]
    hb = h.astype(BF16)

    hmean = hmean_ref[...]

    def head_norm(t, gain):
        ss = _dot_exact_rhs(t * t, hmean)
        return t * lax.rsqrt(ss + NORM_EPS) * gain

    q = _dot(hb, wqkv_ref[:, 0:ATTN_WIDTH])
    q_ref[...] = head_norm(q, qg_ref[...]) * (HEAD_DIM ** -0.5)
    k = _dot(hb, wqkv_ref[:, ATTN_WIDTH:2 * ATTN_WIDTH])
    k_ref[...] = head_norm(k, kg_ref[...])
    v_ref[...] = _dot(hb, wqkv_ref[:, 2 * ATTN_WIDTH:3 * ATTN_WIDTH])
    for c0 in range(0, SSM_WIDTH, 512):
        z_ref[:, c0:c0 + 512] = _dot(hb, wz_ref[:, c0:c0 + 512])
    for c0 in range(0, CONV_CH, 512):
        xbc_ref[:, c0:c0 + 512] = _dot(hb, wxbc_ref[:, c0:c0 + 512])
    dt_ref[...] = _dot(hb, wdt_ref[...])


def _in_proj(x, shift, scale, gain, w_in, q_gain, k_gain, tm=256):
    b, s, d = x.shape
    t = b * s
    tiles_per_seq = s // tm
    w = w_in.astype(BF16)
    o_z = 3 * ATTN_WIDTH
    o_x = o_z + SSM_WIDTH
    o_dt = o_x + CONV_CH
    w_qkv, w_z, w_xbc = w[:, :o_z], w[:, o_z:o_x], w[:, o_x:o_dt]
    w_dt = jnp.zeros((d, LANES), BF16).at[:, :SSM_HEADS].set(w[:, o_dt:])
    head_of = np.arange(ATTN_WIDTH) // HEAD_DIM
    hmean = jnp.asarray((head_of[:, None] == head_of[None, :]).astype(np.float32) / HEAD_DIM, BF16)
    full = lambda shp: pl.BlockSpec(shp, lambda i: (0,) * len(shp))
    row = lambda n: pl.BlockSpec((tm, n), lambda i: (i, 0))
    per_batch = pl.BlockSpec((None, 1, d), lambda i: (i // tiles_per_seq, 0, 0))
    outs = pl.pallas_call(
        _inproj_kernel,
        grid=(t // tm,),
        in_specs=[row(d), per_batch, per_batch, full((1, d)),
                  full((d, o_z)), full((d, SSM_WIDTH)), full((d, CONV_CH)), full((d, LANES)),
                  full((1, ATTN_WIDTH)), full((1, ATTN_WIDTH)), full((ATTN_WIDTH, ATTN_WIDTH))],
        out_specs=[row(ATTN_WIDTH), row(ATTN_WIDTH), row(ATTN_WIDTH), row(SSM_WIDTH), row(CONV_CH), row(LANES)],
        out_shape=[jax.ShapeDtypeStruct((t, n), F32)
                   for n in (ATTN_WIDTH, ATTN_WIDTH, ATTN_WIDTH, SSM_WIDTH, CONV_CH, LANES)],
        compiler_params=_params(("arbitrary",)),
        name="in_proj",
    )(x.reshape(t, d), shift.reshape(b, 1, d), scale.reshape(b, 1, d), gain.reshape(1, d),
      w_qkv, w_z, w_xbc, w_dt,
      jnp.tile(q_gain, ATTN_HEADS).reshape(1, ATTN_WIDTH), jnp.tile(k_gain, ATTN_HEADS).reshape(1, ATTN_WIDTH), hmean)
    return outs


def _t5_causal_buckets(distance):
    n = np.maximum(distance, 0)
    max_exact = REL_BUCKETS // 2
    large = max_exact + (np.log(np.maximum(n, 1) / max_exact) / math.log(REL_MAX_DISTANCE / max_exact)
                         * (REL_BUCKETS - max_exact)).astype(np.int64)
    large = np.minimum(large, REL_BUCKETS - 1)
    return np.where(n < max_exact, n, large).astype(np.int32)


def _window_bias(rel_bias_table, dilation):
    qi = np.arange(WIN_STEPS)[:, None]
    kj = np.arange(2 * WIN_STEPS)[None, :]
    dist = qi + WIN_STEPS - kj
    band = (dist >= 0) & (dist <= WIN_STEPS)
    onehot = (_t5_causal_buckets(dist * dilation).reshape(-1, 1) == np.arange(REL_BUCKETS)[None, :]).astype(np.float32)
    bias = jnp.dot(rel_bias_table.astype(F32).T, jnp.asarray(onehot).T, precision=lax.Precision.HIGHEST)
    bias = bias.reshape(ATTN_HEADS, WIN_STEPS, 2 * WIN_STEPS)
    return jnp.where(jnp.asarray(band)[None], bias, NEG_BIG)


ATTN_TOKENS = max(w for w, _ in PATTERNS)
ATTN_UNROLL = 8


def _attn_kernel(q_ref, kp_ref, kc_ref, vp_ref, vc_ref, bias_ref, out_ref, kw, vw, o_acc, l_acc):
    tb = ATTN_TOKENS
    first = pl.program_id(2) == 0
    kw[0:tb] = kp_ref[...]
    kw[tb:2 * tb] = kc_ref[...]
    vw[0:tb] = vp_ref[...]
    vw[tb:2 * tb] = vc_ref[...]
    lane = lax.broadcasted_iota(I32, (WIN_STEPS, LANES), 1)
    head0 = lane < HEAD_DIM
    col = lax.broadcasted_iota(I32, (WIN_STEPS, 2 * WIN_STEPS), 1)
    in_prev = col < WIN_STEPS

    for p, (_, d) in enumerate(PATTERNS):
        shift = d.bit_length() - 1
        n_blocks = tb // WIN_STEPS

        def rows(start, n, d=d):
            return pl.ds(start, n, stride=d) if d > 1 else pl.ds(start, n)

        def body(it, carry, p=p, d=d, shift=shift, rows=rows):
            for u in range(ATTN_UNROLL):
                idx = it * ATTN_UNROLL + u
                r = jnp.bitwise_and(idx, d - 1)
                j = jnp.right_shift(idx, shift)
                qs = j * (WIN_STEPS * d) + r
                q = q_ref[rows(qs, WIN_STEPS), :]
                k = kw[rows(tb + qs - WIN_STEPS * d, 2 * WIN_STEPS), :].astype(BF16)
                v = vw[rows(tb + qs - WIN_STEPS * d, 2 * WIN_STEPS), :].astype(BF16)
                no_prev = jnp.logical_and(in_prev, jnp.logical_and(first, j == 0))
                o_h, lse_h = [], []
                for h in range(2):
                    qh = jnp.where(head0 if h == 0 else jnp.logical_not(head0), q, 0.0).astype(BF16)
                    s = _dot_nt(qh, k) + bias_ref[p, h]
                    s = jnp.where(no_prev, NEG_BIG, s)
                    m = jnp.max(s, axis=-1, keepdims=True)
                    e = jnp.exp(s - m)
                    denom = jnp.sum(e, axis=-1, keepdims=True)
                    o_h.append(_dot(e.astype(BF16), v) / denom)
                    lse_h.append(m + jnp.log(denom))
                o_acc[p, rows(qs, WIN_STEPS), :] = jnp.where(head0, o_h[0], o_h[1])
                l_acc[p, rows(qs, WIN_STEPS), :] = jnp.where(head0, lse_h[0], lse_h[1])
            return carry

        lax.fori_loop(0, n_blocks // ATTN_UNROLL, body, 0)

    chunk = 256
    for c0 in range(0, tb, chunk):
        l1, l2, l3 = (l_acc[p, c0:c0 + chunk, :] for p in range(3))
        m = jnp.maximum(jnp.maximum(l1, l2), l3)
        e1, e2, e3 = jnp.exp(l1 - m), jnp.exp(l2 - m), jnp.exp(l3 - m)
        num = e1 * o_acc[0, c0:c0 + chunk, :] + e2 * o_acc[1, c0:c0 + chunk, :] + e3 * o_acc[2, c0:c0 + chunk, :]
        out_ref[c0:c0 + chunk, :] = num / (e1 + e2 + e3)


def _attention(q, k, v, bias):
    b, s, w = q.shape
    tb = ATTN_TOKENS
    pairs = ATTN_HEADS // 2
    cur = pl.BlockSpec((None, tb, LANES), lambda bi, hp, i: (bi, i, hp))
    prev = pl.BlockSpec((None, tb, LANES), lambda bi, hp, i: (bi, jnp.maximum(i - 1, 0), hp))
    return pl.pallas_call(
        _attn_kernel,
        grid=(b, pairs, s // tb),
        in_specs=[cur, prev, cur, prev, cur,
                  pl.BlockSpec((len(PATTERNS), 2, WIN_STEPS, 2 * WIN_STEPS), lambda bi, hp, i: (0, hp, 0, 0))],
        out_specs=cur,
        out_shape=jax.ShapeDtypeStruct((b, s, w), F32),
        scratch_shapes=[pltpu.VMEM((2 * tb, LANES), F32), pltpu.VMEM((2 * tb, LANES), F32),
                        pltpu.VMEM((len(PATTERNS), tb, LANES), F32), pltpu.VMEM((len(PATTERNS), tb, LANES), F32)],
        compiler_params=_params(("arbitrary",) * 3),
        name="attention",
    )(q, k, k, v, v, bias)


def _ssd_kernel(xbc_ref, halo_ref, z_ref, dtraw_ref, convw_ref, convb_ref, dtb_ref, alog_ref, dskip_ref, gain_ref,
                expand_ref, tril_ref, y_ref, state_ref):
    c = pl.program_id(1)

    @pl.when(c == 0)
    def _():
        state_ref[...] = jnp.zeros_like(state_ref)

    x = xbc_ref[...]
    halo = jnp.where(c == 0, 0.0, halo_ref[...])
    w = convw_ref[...]
    acc = x * w[SSM_CONV - 1:SSM_CONV, :] + convb_ref[...]
    row8 = lax.broadcasted_iota(I32, (SUBLANES, CONV_CH), 0)
    for shift in range(1, SSM_CONV):
        xs = pltpu.roll(x, shift, axis=0)
        hs = pltpu.roll(halo, shift, axis=0)
        head = jnp.where(row8 < shift, hs, xs[0:SUBLANES])
        xs = jnp.concatenate([head, xs[SUBLANES:]], axis=0)
        acc = acc + xs * w[SSM_CONV - 1 - shift:SSM_CONV - shift, :]
    act = _silu(acc)
    x_s = act[:, :SSM_WIDTH]
    bc0 = SSM_WIDTH
    cc0 = SSM_WIDTH + SSM_GROUPS * SSM_STATE

    t = dtraw_ref[...] + dtb_ref[...]
    dt = jnp.maximum(t, 0.0) + jnp.log(1.0 + jnp.exp(-jnp.abs(t)))
    a = dt * (-jnp.exp(alog_ref[...]))
    a_cs = _dot_exact_lhs(tril_ref[...], a)
    a_cs_t = a_cs.T
    a_last = a_cs[SSM_CHUNK - 1:SSM_CHUNK, :]
    expand = expand_ref[...]
    dt_e = _dot_exact_rhs(dt, expand)
    ea_e = _dot_exact_rhs(jnp.exp(a_cs), expand)
    dte_e = _dot_exact_rhs(jnp.exp(a_last - a_cs), expand)
    xdt = x_s * dt_e
    xw = (xdt * dte_e).astype(BF16)
    xdt_b = xdt.astype(BF16)

    li = lax.broadcasted_iota(I32, (SSM_CHUNK, SSM_CHUNK), 0)
    si = lax.broadcasted_iota(I32, (SSM_CHUNK, SSM_CHUNK), 1)
    causal = li >= si

    ys = []
    for g in range(SSM_GROUPS):
        gs = slice(g * GROUP_WIDTH, (g + 1) * GROUP_WIDTH)
        b_g = act[:, bc0 + g * SSM_STATE:bc0 + (g + 1) * SSM_STATE]
        c_g = act[:, cc0 + g * SSM_STATE:cc0 + (g + 1) * SSM_STATE].astype(BF16)
        cb = _dot_nt(c_g, b_g.astype(BF16))
        state = state_ref[g]
        y_off = _dot(c_g, state.astype(BF16)) * ea_e[:, gs]
        parts = []
        for j in range(HEADS_PER_GROUP):
            hh = g * HEADS_PER_GROUP + j
            seg = a_cs[:, hh:hh + 1] - a_cs_t[hh:hh + 1, :]
            decay = jnp.exp(jnp.where(causal, seg, NEG_BIG))
            m = (cb * decay).astype(BF16)
            parts.append(_dot(m, xdt_b[:, hh * SSM_HEAD_DIM:(hh + 1) * SSM_HEAD_DIM]))
        ys.append(jnp.concatenate(parts, axis=1) + y_off)
        state_ref[g] = state * ea_e[SSM_CHUNK - 1:SSM_CHUNK, gs] + _dot(b_g.T.astype(BF16), xw[:, gs])
    y = jnp.concatenate(ys, axis=1) + dskip_ref[...] * x_s
    y = y * _silu(z_ref[...])
    gain = gain_ref[...]
    for g in range(SSM_GROUPS):
        gs = slice(g * GROUP_WIDTH, (g + 1) * GROUP_WIDTH)
        yg = y[:, gs]
        ms = jnp.mean(yg * yg, axis=-1, keepdims=True)
        y_ref[:, gs] = yg * lax.rsqrt(ms + NORM_EPS) * gain[:, gs]


def _ssd(xbc, z, dt_raw, conv_w, conv_b, dt_bias, a_log, d_skip, norm_gain, b, s):
    t = b * s
    nc = s // SSM_CHUNK
    pad_heads = lambda v: jnp.zeros((1, LANES), F32).at[0, :SSM_HEADS].set(v)
    head_of_lane = np.arange(SSM_WIDTH) // SSM_HEAD_DIM
    expand = jnp.asarray((np.arange(LANES)[:, None] == head_of_lane[None, :]).astype(np.float32), BF16)
    tril = jnp.asarray(np.tril(np.ones((SSM_CHUNK, SSM_CHUNK), np.float32)), BF16)
    halo_blocks = SSM_CHUNK // SUBLANES
    chunk = lambda n: pl.BlockSpec((SSM_CHUNK, n), lambda bi, c: (bi * nc + c, 0))
    full = lambda shp: pl.BlockSpec(shp, lambda bi, c: (0,) * len(shp))
    halo = pl.BlockSpec((SUBLANES, CONV_CH), lambda bi, c: (jnp.maximum((bi * nc + c) * halo_blocks - 1, 0), 0))
    return pl.pallas_call(
        _ssd_kernel,
        grid=(b, nc),
        in_specs=[chunk(CONV_CH), halo, chunk(SSM_WIDTH), chunk(LANES),
                  full((SSM_CONV, CONV_CH)), full((1, CONV_CH)), full((1, LANES)), full((1, LANES)),
                  full((1, SSM_WIDTH)), full((1, SSM_WIDTH)), full((LANES, SSM_WIDTH)), full((SSM_CHUNK, SSM_CHUNK))],
        out_specs=chunk(SSM_WIDTH),
        out_shape=jax.ShapeDtypeStruct((t, SSM_WIDTH), F32),
        scratch_shapes=[pltpu.VMEM((SSM_GROUPS, SSM_STATE, GROUP_WIDTH), F32)],
        compiler_params=_params(("arbitrary", "arbitrary")),
        name="ssd",
    )(xbc, xbc, z, dt_raw, conv_w, conv_b.reshape(1, CONV_CH), pad_heads(dt_bias), pad_heads(a_log),
      jnp.repeat(d_skip, SSM_HEAD_DIM).reshape(1, SSM_WIDTH), norm_gain.reshape(1, SSM_WIDTH), expand, tril)


U32 = jnp.uint32
TOKEN_ROWS = D_MODEL // (2 * LANES)
HIGH_HALF = np.uint32(0xFFFF0000)


def _to_token_tiles(ref, x):
    n = x.shape[0]
    for c in range(TOKEN_ROWS):
        lo = lax.bitcast_convert_type(x[:, c * LANES:(c + 1) * LANES].astype(BF16).astype(F32), U32)
        hi = lax.bitcast_convert_type(x[:, (c + TOKEN_ROWS) * LANES:(c + TOKEN_ROWS + 1) * LANES]
                                      .astype(BF16).astype(F32), U32)
        ref[pl.ds(c, n, stride=TOKEN_ROWS), :] = jnp.bitwise_or(jnp.right_shift(lo, 16), jnp.bitwise_and(hi, HIGH_HALF))


def _from_token_tiles(ref, n, token0=0):
    lows, highs = [], []
    for c in range(TOKEN_ROWS):
        word = ref[pl.ds(token0 * TOKEN_ROWS + c, n, stride=TOKEN_ROWS), :]
        lows.append(lax.bitcast_convert_type(jnp.left_shift(word, 16), F32))
        highs.append(lax.bitcast_convert_type(jnp.bitwise_and(word, HIGH_HALF), F32))
    return jnp.concatenate(lows + highs, axis=1)


def _outproj_kernel(attn_ref, ssm_ref, x_ref, gate_ref, shift_ref, scale_ref, g_ref, wa_ref, ws_ref, x1_ref, h2_ref,
                    h2t_ref):
    mixed = _dot(attn_ref[...].astype(BF16), wa_ref[...]) + _dot(ssm_ref[...].astype(BF16), ws_ref[...])
    x1 = x_ref[...] + gate_ref[...] * mixed
    x1_ref[...] = x1
    ms = jnp.mean(x1 * x1, axis=-1, keepdims=True)
    h = x1 * lax.rsqrt(ms + NORM_EPS) * g_ref[...]
    h2 = h * (1.0 + scale_ref[...]) + shift_ref[...]
    h2_ref[...] = h2
    _to_token_tiles(h2t_ref, h2)


def _out_proj(attn, ssm, x, gate, shift, scale, gain, w_out, b, s, tm=256):
    t = b * s
    d = D_MODEL
    tiles_per_seq = s // tm
    w = w_out.astype(BF16)
    row = lambda n: pl.BlockSpec((tm, n), lambda i: (i, 0))
    full = lambda shp: pl.BlockSpec(shp, lambda i: (0,) * len(shp))
    per_batch = pl.BlockSpec((None, 1, d), lambda i: (i // tiles_per_seq, 0, 0))
    return pl.pallas_call(
        _outproj_kernel,
        grid=(t // tm,),
        in_specs=[row(ATTN_WIDTH), row(SSM_WIDTH), row(d), per_batch, per_batch, per_batch, full((1, d)),
                  full((ATTN_WIDTH, d)), full((SSM_WIDTH, d))],
        out_specs=[row(d), row(d), pl.BlockSpec((tm * TOKEN_ROWS, LANES), lambda i: (i, 0))],
        out_shape=[jax.ShapeDtypeStruct((t, d), F32)] * 2 + [jax.ShapeDtypeStruct((t * TOKEN_ROWS, LANES), U32)],
        compiler_params=_params(("arbitrary",)),
        name="out_proj",
    )(attn.reshape(t, ATTN_WIDTH), ssm, x.reshape(t, d),
      gate.reshape(b, 1, d), shift.reshape(b, 1, d), scale.reshape(b, 1, d), gain.reshape(1, d),
      w[:ATTN_WIDTH], w[ATTN_WIDTH:])


def _mixer_sublayer(x, mod, norm_mix_gain, w_in, q_norm_gain, k_norm_gain, rel_bias_table, conv_w, conv_b, dt_bias,
                    a_log, d_skip, ssm_norm_gain, w_out, norm_ffn_gain):
    b, s, d = x.shape
    shift_m, scale_m, gate_m, shift_f, scale_f, _ = jnp.split(mod, 6, axis=-1)
    q, k, v, z, xbc, dt_raw = _in_proj(x, shift_m, scale_m, norm_mix_gain, w_in, q_norm_gain, k_norm_gain)
    bias = jnp.stack([_window_bias(rel_bias_table, dilation) for _, dilation in PATTERNS])
    attn = _attention(q.reshape(b, s, ATTN_WIDTH), k.reshape(b, s, ATTN_WIDTH), v.reshape(b, s, ATTN_WIDTH), bias)
    ssm = _ssd(xbc, z, dt_raw, conv_w, conv_b, dt_bias, a_log, d_skip, ssm_norm_gain, b, s)
    return _out_proj(attn, ssm, x, gate_m, shift_f, scale_f, norm_ffn_gain, w_out, b, s)


def _first_argmax(v, iota, limit):
    m = jnp.max(v, axis=0, keepdims=True)
    idx = jnp.min(jnp.where(v == m, iota, limit), axis=0, keepdims=True)
    return m, idx


def _router_kernel(h_ref, wt_ref, bias_ref, upper_ref, eidx_ref, rank_ref, gate_ref, counts_ref, carry_ref):
    @pl.when(pl.program_id(0) == 0)
    def _():
        carry_ref[...] = jnp.zeros_like(carry_ref)

    tm = h_ref.shape[0]
    h = h_ref[...]
    wt = wt_ref[...]
    h_hi = h.astype(BF16)
    h_lo = (h - h_hi.astype(F32)).astype(BF16)
    w_hi = wt.astype(BF16)
    w_lo = (wt - w_hi.astype(F32)).astype(BF16)
    logits = _dot_nt(w_hi, h_hi) + _dot_nt(w_hi, h_lo) + _dot_nt(w_lo, h_hi)
    scores = _sigmoid(logits)
    choice = scores + bias_ref[...]
    neg_inf = -jnp.inf

    iota_g = lax.broadcasted_iota(I32, (EXPERTS_PER_GROUP, tm), 0).astype(F32)
    group_rows = []
    for g in range(N_EXPERT_GROUPS):
        v = choice[g * EXPERTS_PER_GROUP:(g + 1) * EXPERTS_PER_GROUP]
        m1, i1 = _first_argmax(v, iota_g, float(EXPERTS_PER_GROUP))
        m2 = jnp.max(jnp.where(iota_g == i1, neg_inf, v), axis=0, keepdims=True)
        group_rows.append(m1 + m2)
    group_scores = jnp.concatenate(group_rows, axis=0)

    iota_n = lax.broadcasted_iota(I32, (N_EXPERT_GROUPS, tm), 0).astype(F32)
    chosen = jnp.zeros((N_EXPERT_GROUPS, tm), F32)
    for _ in range(TOPK_GROUPS):
        _, gi = _first_argmax(group_scores, iota_n, float(N_EXPERT_GROUPS))
        hit = iota_n == gi
        chosen = jnp.where(hit, 1.0, chosen)
        group_scores = jnp.where(hit, neg_inf, group_scores)

    masked = jnp.concatenate(
        [jnp.where(chosen[g:g + 1] > 0.0, choice[g * EXPERTS_PER_GROUP:(g + 1) * EXPERTS_PER_GROUP], neg_inf)
         for g in range(N_EXPERT_GROUPS)], axis=0)

    iota_e = lax.broadcasted_iota(I32, (N_EXPERTS, tm), 0).astype(F32)
    picked, gates = [], []
    onehot = jnp.zeros((N_EXPERTS, tm), F32)
    for _ in range(TOP_K):
        _, ei = _first_argmax(masked, iota_e, float(N_EXPERTS))
        hit = iota_e == ei
        gates.append(jnp.sum(jnp.where(hit, scores, 0.0), axis=0, keepdims=True))
        masked = jnp.where(hit, neg_inf, masked)
        onehot = jnp.where(hit, 1.0, onehot)
        picked.append(ei)
    gate_sum = gates[0]
    for gk in gates[1:]:
        gate_sum = gate_sum + gk

    base = _dot(onehot.astype(BF16), upper_ref[...]) + carry_ref[...]
    ranks = [jnp.sum(jnp.where(iota_e == ei, base, 0.0), axis=0, keepdims=True) for ei in picked]
    carry_ref[...] = carry_ref[...] + jnp.sum(onehot, axis=1, keepdims=True)

    eidx_ref[...] = jnp.concatenate(picked, axis=0).astype(I32)
    rank_ref[...] = jnp.concatenate(ranks, axis=0).astype(I32)
    gate_ref[...] = jnp.concatenate([gk / gate_sum * ROUTED_SCALE for gk in gates], axis=0)
    counts_ref[...] = carry_ref[...].astype(I32)


def _router(h2, w_router, router_bias, tm=256):
    t, d = h2.shape
    upper = jnp.asarray(np.triu(np.ones((tm, tm), np.float32), 1), BF16)
    tok = pl.BlockSpec((TOP_K, tm), lambda i: (0, i))
    full = lambda shp: pl.BlockSpec(shp, lambda i: (0,) * len(shp))
    return pl.pallas_call(
        _router_kernel,
        grid=(t // tm,),
        in_specs=[pl.BlockSpec((tm, d), lambda i: (i, 0)), full((N_EXPERTS, d)), full((N_EXPERTS, 1)), full((tm, tm))],
        out_specs=[tok, tok, tok, full((N_EXPERTS, 1))],
        out_shape=[jax.ShapeDtypeStruct((TOP_K, t), I32), jax.ShapeDtypeStruct((TOP_K, t), I32),
                   jax.ShapeDtypeStruct((TOP_K, t), F32), jax.ShapeDtypeStruct((N_EXPERTS, 1), I32)],
        scratch_shapes=[pltpu.VMEM((N_EXPERTS, 1), F32)],
        compiler_params=_params(("arbitrary",)),
        name="router",
    )(h2, w_router.T, router_bias.reshape(N_EXPERTS, 1), upper)


def _positions_kernel(counts_ref, lower_ref, eidx_ref, rank_ref, pos_ref):
    tm = eidx_ref.shape[1]
    counts = jnp.broadcast_to(counts_ref[...].astype(F32), (N_EXPERTS, LANES))
    offsets = _dot_exact_lhs(lower_ref[...], counts)[:, 0:1]
    iota_e = lax.broadcasted_iota(I32, (N_EXPERTS, tm), 0).astype(F32)
    e = eidx_ref[...].astype(F32)
    rows = [jnp.sum(jnp.where(iota_e == e[k:k + 1], offsets, 0.0), axis=0, keepdims=True) for k in range(TOP_K)]
    pos_ref[0] = jnp.concatenate(rows, axis=0).astype(I32) + rank_ref[...]


def _positions(counts, eidx, rank, tm):
    t = eidx.shape[1]
    lower = jnp.asarray(np.tril(np.ones((N_EXPERTS, N_EXPERTS), np.float32), -1), BF16)
    tok = pl.BlockSpec((TOP_K, tm), lambda i: (0, i))
    return pl.pallas_call(
        _positions_kernel,
        grid=(t // tm,),
        in_specs=[pl.BlockSpec((N_EXPERTS, 1), lambda i: (0, 0)), pl.BlockSpec((N_EXPERTS, N_EXPERTS), lambda i: (0, 0)),
                  tok, tok],
        out_specs=pl.BlockSpec((1, TOP_K, tm), lambda i: (i, 0, 0)),
        out_shape=jax.ShapeDtypeStruct((t // tm, TOP_K, tm), I32),
        compiler_params=_params(("arbitrary",)),
        name="positions",
    )(counts, lower, eidx, rank)


def _token_rows(i):
    return pl.ds(pl.multiple_of(i * TOKEN_ROWS, TOKEN_ROWS), TOKEN_ROWS)


def _dispatch_kernel(pos_ref, h_ref, xs_ref, sem):
    tm = h_ref.shape[0] // TOKEN_ROWS

    def row_copy(t, k):
        return pltpu.make_async_copy(h_ref.at[_token_rows(t)], xs_ref.at[_token_rows(pos_ref[0, k, t])], sem)

    def start(t, carry):
        for k in range(TOP_K):
            row_copy(t, k).start(priority=k % 2)
        return carry

    def wait(t, carry):
        for k in range(TOP_K):
            row_copy(t, k).wait()
        return carry

    lax.fori_loop(0, tm, start, 0)
    lax.fori_loop(0, tm, wait, 0)


def _dispatch(h2t, pos_tiles):
    t = h2t.shape[0] // TOKEN_ROWS
    tm = pos_tiles.shape[2]
    return pl.pallas_call(
        _dispatch_kernel,
        grid=(t // tm,),
        in_specs=[pl.BlockSpec((1, TOP_K, tm), lambda i: (i, 0, 0), memory_space=pltpu.SMEM),
                  pl.BlockSpec((tm * TOKEN_ROWS, LANES), lambda i: (i, 0))],
        out_specs=pl.BlockSpec(memory_space=pl.ANY),
        out_shape=jax.ShapeDtypeStruct(((t * TOP_K + EXPERT_BLOCK) * TOKEN_ROWS, LANES), h2t.dtype),
        scratch_shapes=[pltpu.SemaphoreType.DMA(())],
        compiler_params=_params(("arbitrary",)),
        name="dispatch",
    )(pos_tiles, h2t)


EXPERT_BLOCK = 256
TAIL_PIECES = tuple(1 << i for i in reversed(range(EXPERT_BLOCK.bit_length() - 1)))
X_SLOTS = 4
Y_SLOTS = 4


def _experts_kernel(start_ref, count_ref, nxt_ref, slot_ref, first_ref, blk0_ref, full0_ref, ptail_ref, ltail_ref,
                    blktok_ref, nblocks_ref, xs_hbm, wg_hbm, wu_hbm, wd_hbm, ys_hbm,
                    wg_buf, wu_buf, wd_buf, wg_bf, wu_bf, wd_bf, xbuf, ybuf, ytail, wsem, xsem, ysem, tsem):
    e = pl.program_id(0)
    last_step = e == pl.num_programs(0) - 1
    start, count = start_ref[e], count_ref[e]
    n_full = jnp.right_shift(count, EXPERT_BLOCK.bit_length() - 1)
    tail = jnp.bitwise_and(count, EXPERT_BLOCK - 1)
    n_blk = n_full + (tail > 0).astype(I32)
    blk0, full0 = blk0_ref[e], full0_ref[e]
    slot, nxt = slot_ref[e], nxt_ref[e]

    def token_rows(token, n):
        return pl.ds(pl.multiple_of(token * TOKEN_ROWS, TOKEN_ROWS), n * TOKEN_ROWS)

    def fetch(ex, s):
        return (pltpu.make_async_copy(wg_hbm.at[ex], wg_buf.at[s], wsem.at[s, 0]),
                pltpu.make_async_copy(wu_hbm.at[ex], wu_buf.at[s], wsem.at[s, 1]),
                pltpu.make_async_copy(wd_hbm.at[ex], wd_buf.at[s], wsem.at[s, 2]))

    def x_copy(token, s):
        return pltpu.make_async_copy(xs_hbm.at[token_rows(token, EXPERT_BLOCK)], xbuf.at[s], xsem.at[s])

    def y_copy(token, s):
        return pltpu.make_async_copy(ybuf.at[s], ys_hbm.at[token_rows(token, EXPERT_BLOCK)], ysem.at[s])

    def tail_copies(token, length):
        out = []
        for piece in TAIL_PIECES:
            bigger = (EXPERT_BLOCK - 1) & ~(2 * piece - 1)
            done = jnp.bitwise_and(length, bigger)
            cp = pltpu.make_async_copy(ytail.at[token_rows(done, piece)], ys_hbm.at[token_rows(token + done, piece)],
                                       tsem)
            out.append((jnp.bitwise_and(length, piece) != 0, cp))
        return out

    def block(s):
        x = _from_token_tiles(xbuf.at[s], EXPERT_BLOCK).astype(BF16)
        g = _dot(x, wg_bf[...])
        u = _dot(x, wu_bf[...])
        return _dot((_silu(g) * u).astype(BF16), wd_bf[...])

    @pl.when(count > 0)
    def _():
        @pl.when(first_ref[e] == 1)
        def _():
            for g in range(X_SLOTS - 1):
                @pl.when(g < nblocks_ref[0])
                def _(g=g):
                    x_copy(blktok_ref[g], g).start()
            for cp in fetch(e, slot):
                cp.start()

        for cp in fetch(e, slot):
            cp.wait()

        @pl.when(nxt >= 0)
        def _():
            for cp in fetch(nxt, 1 - slot):
                cp.start()

        wg_bf[...] = wg_buf[slot].astype(BF16)
        wu_bf[...] = wu_buf[slot].astype(BF16)
        wd_bf[...] = wd_buf[slot].astype(BF16)

        def step(i):
            g = blk0 + i
            xs_slot = jnp.bitwise_and(g, X_SLOTS - 1)
            x_copy(start, xs_slot).wait()
            ahead = g + (X_SLOTS - 1)

            @pl.when(ahead < nblocks_ref[0])
            def _():
                x_copy(blktok_ref[ahead], jnp.bitwise_and(ahead, X_SLOTS - 1)).start()

            return block(xs_slot)

        def full_block(i, carry):
            y = step(i)
            j = full0 + i
            ys_slot = jnp.bitwise_and(j, Y_SLOTS - 1)

            @pl.when(j >= Y_SLOTS)
            def _():
                y_copy(start, ys_slot).wait()

            _to_token_tiles(ybuf.at[ys_slot], y)
            y_copy(start + i * EXPERT_BLOCK, ys_slot).start()
            return carry

        lax.fori_loop(0, n_full, full_block, 0)

        @pl.when(tail > 0)
        def _():
            y = step(n_full)
            for pred, cp in tail_copies(start, ptail_ref[e]):
                @pl.when(pred)
                def _(cp=cp):
                    cp.wait()
            _to_token_tiles(ytail, y)
            for pred, cp in tail_copies(start + n_full * EXPERT_BLOCK, tail):
                @pl.when(pred)
                def _(cp=cp):
                    cp.start()

    @pl.when(last_step)
    def _():
        total_full = full0 + n_full
        for back in range(1, Y_SLOTS + 1):
            @pl.when(total_full >= back)
            def _(back=back):
                y_copy(0, jnp.bitwise_and(total_full - back, Y_SLOTS - 1)).wait()
        for pred, cp in tail_copies(0, ltail_ref[0]):
            @pl.when(pred)
            def _(cp=cp):
                cp.wait()


def _max_expert_blocks(n_rows):
    return n_rows // EXPERT_BLOCK + N_EXPERTS


def _expert_metadata(counts, n_rows):
    ids = jnp.arange(N_EXPERTS, dtype=I32)
    used = counts > 0
    starts = jnp.cumsum(counts) - counts
    n_blk = (counts + EXPERT_BLOCK - 1) // EXPERT_BLOCK
    n_full = counts // EXPERT_BLOCK
    tail = counts % EXPERT_BLOCK
    blk0 = jnp.cumsum(n_blk) - n_blk
    full0 = jnp.cumsum(n_full) - n_full
    next_used = lax.cummin(jnp.where(used, ids, N_EXPERTS), reverse=True)
    next_after = jnp.concatenate([next_used[1:], jnp.full((1,), N_EXPERTS, I32)])
    nxt = jnp.where(next_after < N_EXPERTS, next_after, -1)
    ordinal = jnp.cumsum(used.astype(I32)) - 1
    slot = ordinal % 2
    first = jnp.logical_and(used, ordinal == 0)
    latest = lax.cummax(jnp.where(tail > 0, ids, -1))
    before = jnp.concatenate([jnp.full((1,), -1, I32), latest[:-1]])
    pick = lambda index, values: jnp.sum(jnp.where(index[:, None] == ids[None, :], values[None, :], 0), axis=1)
    ptail = pick(before, tail)
    ltail = pick(latest[-1:], tail)
    block_ends = jnp.cumsum(n_blk)
    g = jnp.arange(_max_expert_blocks(n_rows), dtype=I32)
    eg = jnp.sum((g[:, None] >= block_ends[None, :]).astype(I32), axis=1)
    blktok = g * EXPERT_BLOCK + pick(eg, starts - blk0 * EXPERT_BLOCK)
    return tuple(v.astype(I32) for v in (starts, counts, nxt, slot, first, blk0, full0, ptail, ltail, blktok,
                                         block_ends[-1:]))


def _experts(xs, counts, w_gate, w_up, w_down):
    d = D_MODEL
    meta = _expert_metadata(counts, xs.shape[0] // TOKEN_ROWS - EXPERT_BLOCK)
    hbm = pl.BlockSpec(memory_space=pl.ANY)
    blk = (EXPERT_BLOCK * TOKEN_ROWS, LANES)
    grid_spec = pltpu.PrefetchScalarGridSpec(
        num_scalar_prefetch=len(meta),
        grid=(N_EXPERTS,),
        in_specs=[hbm, hbm, hbm, hbm],
        out_specs=hbm,
        scratch_shapes=[pltpu.VMEM((2, d, EXPERT_FF), F32), pltpu.VMEM((2, d, EXPERT_FF), F32),
                        pltpu.VMEM((2, EXPERT_FF, d), F32),
                        pltpu.VMEM((d, EXPERT_FF), BF16), pltpu.VMEM((d, EXPERT_FF), BF16),
                        pltpu.VMEM((EXPERT_FF, d), BF16),
                        pltpu.VMEM((X_SLOTS,) + blk, U32), pltpu.VMEM((Y_SLOTS,) + blk, U32), pltpu.VMEM(blk, U32),
                        pltpu.SemaphoreType.DMA((2, 3)), pltpu.SemaphoreType.DMA((X_SLOTS,)),
                        pltpu.SemaphoreType.DMA((Y_SLOTS,)), pltpu.SemaphoreType.DMA(())],
    )
    return pl.pallas_call(
        _experts_kernel,
        grid_spec=grid_spec,
        out_shape=jax.ShapeDtypeStruct(xs.shape, U32),
        compiler_params=_params(("arbitrary",)),
        name="experts",
    )(*meta, xs, w_gate, w_up, w_down)


SC_CORES = 2
SC_SUBCORES = 16
SC_WORKERS = SC_CORES * SC_SUBCORES
SC_CHUNK = 128
SC_RING = 4


def _sc_gather_rows(table, idx):
    n_workers, n_chunks, chunk = idx.shape
    rows_per_worker = n_chunks * chunk
    mesh = plsc.VectorSubcoreMesh(core_axis_name="c", subcore_axis_name="s", num_cores=SC_CORES,
                                  num_subcores=SC_SUBCORES)

    def body(table_hbm, idx_hbm, out_hbm, idx_v, rows_v, gsem, wsem):
        wid = lax.axis_index("s") * SC_CORES + lax.axis_index("c")
        base = wid * rows_per_worker
        pltpu.sync_copy(idx_hbm.at[wid], idx_v)

        def write(slot, c):
            return pltpu.make_async_copy(rows_v.at[slot], out_hbm.at[pl.ds(base + c * chunk, chunk)], wsem.at[slot])

        def gather(slot, c):
            return pltpu.make_async_copy(table_hbm.at[idx_v.at[c]], rows_v.at[slot], gsem.at[slot])

        @pl.loop(0, n_chunks, step=SC_RING)
        def _(g):
            for slot in range(SC_RING):
                @pl.when(g > 0)
                def _(slot=slot):
                    write(slot, 0).wait()
                gather(slot, g + slot).start()
            for slot in range(SC_RING):
                gather(slot, g + slot).wait()
                write(slot, g + slot).start()

        for slot in range(SC_RING):
            write(slot, 0).wait()

    return pl.kernel(
        body, mesh=mesh,
        out_type=jax.ShapeDtypeStruct((n_workers * rows_per_worker, LANES), table.dtype),
        scratch_types=[pltpu.VMEM((n_chunks, chunk), I32), pltpu.VMEM((SC_RING, chunk, LANES), table.dtype),
                       pltpu.SemaphoreType.DMA((SC_RING,)), pltpu.SemaphoreType.DMA((SC_RING,))],
        name="sc_gather",
    )(table, idx)


def _combine_kernel(rows_ref, gates_ref, h_ref, x1_ref, gatef_ref, wg_ref, wu_ref, wd_ref, out_ref):
    tm = h_ref.shape[0]
    hb = h_ref[...].astype(BF16)
    shared = _dot((_silu(_dot(hb, wg_ref[...])) * _dot(hb, wu_ref[...])).astype(BF16), wd_ref[...])
    gates = gates_ref[...]
    routed = _from_token_tiles(rows_ref.at[0], tm) * gates[:, 0:1]
    for k in range(1, TOP_K):
        routed = routed + _from_token_tiles(rows_ref.at[k], tm) * gates[:, k:k + 1]
    out_ref[...] = x1_ref[...] + gatef_ref[...] * (shared + routed)


def _combine(gathered, gates_t, h2, x1, gate_f, w_gate_s, w_up_s, w_down_s, b, s, tm):
    t, d = h2.shape
    tiles_per_seq = s // tm
    row = lambda n: pl.BlockSpec((tm, n), lambda i: (i, 0))
    full = lambda shp: pl.BlockSpec(shp, lambda i: (0,) * len(shp))
    return pl.pallas_call(
        _combine_kernel,
        grid=(t // tm,),
        in_specs=[pl.BlockSpec((TOP_K, tm * TOKEN_ROWS, LANES), lambda i: (0, i, 0)),
                  row(TOP_K), row(d), row(d),
                  pl.BlockSpec((None, 1, d), lambda i: (i // tiles_per_seq, 0, 0)),
                  full((d, EXPERT_FF)), full((d, EXPERT_FF)), full((EXPERT_FF, d))],
        out_specs=row(d),
        out_shape=jax.ShapeDtypeStruct((t, d), F32),
        compiler_params=_params(("arbitrary",)),
        name="combine",
    )(gathered, gates_t, h2, x1, gate_f.reshape(b, 1, d),
      w_gate_s.astype(BF16), w_up_s.astype(BF16), w_down_s.astype(BF16))


def _moe_sublayer(x1, h2, h2t, gate_f, w_router, router_bias, w_gate, w_up, w_down, w_gate_s, w_up_s, w_down_s, b, s,
                  tm=256):
    t = b * s
    eidx, rank, gates, counts = _router(h2, w_router, router_bias)
    pos_tiles = _positions(counts, eidx, rank, tm)
    xs = _dispatch(h2t, pos_tiles)
    ys = _experts(xs, counts[:, 0], w_gate, w_up, w_down)
    pos = pos_tiles.transpose(1, 0, 2).reshape(TOP_K, t)
    idx = (pos[..., None] * TOKEN_ROWS + jnp.arange(TOKEN_ROWS, dtype=I32)).reshape(SC_WORKERS, -1, SC_CHUNK)
    gathered = _sc_gather_rows(lax.bitcast_convert_type(ys, I32), idx)
    gathered = lax.bitcast_convert_type(gathered, U32).reshape(TOP_K, t * TOKEN_ROWS, LANES)
    return _combine(gathered, gates.T, h2, x1, gate_f, w_gate_s, w_up_s, w_down_s, b, s, tm)


def kernel(x, c, w_ada, b_ada, norm_mix_gain, w_in, q_norm_gain, k_norm_gain, rel_bias_table, conv_w, conv_b, dt_bias,
           a_log, d_skip, ssm_norm_gain, w_out, norm_ffn_gain, w_router, router_bias, w_gate_experts, w_up_experts,
           w_down_experts, w_gate_shared, w_up_shared, w_down_shared):
    b, s, d = x.shape
    for layer in range(w_ada.shape[0]):
        mod = _adaln(c, w_ada[layer], b_ada[layer])
        x1, h2, h2t = _mixer_sublayer(x, mod, norm_mix_gain[layer], w_in[layer], q_norm_gain[layer], k_norm_gain[layer],
                                 rel_bias_table, conv_w[layer], conv_b[layer], dt_bias[layer], a_log[layer],
                                 d_skip[layer], ssm_norm_gain[layer], w_out[layer], norm_ffn_gain[layer])
        gate_f = mod[:, 5 * d:]
        out = _moe_sublayer(x1, h2, h2t, gate_f, w_router[layer], router_bias[layer], w_gate_experts[layer],
                            w_up_experts[layer], w_down_experts[layer], w_gate_shared[layer], w_up_shared[layer],
                            w_down_shared[layer], b, s)
        x = out.reshape(b, s, d)
    return x
```

```python
import functools
import math

import numpy as np
import jax
import jax.numpy as jnp
from jax import lax
from jax.experimental import pallas as pl
from jax.experimental.pallas import tpu as pltpu
from jax.experimental.pallas import tpu_sc as plsc

F32 = jnp.float32
BF16 = jnp.bfloat16
I32 = jnp.int32

D_MODEL = 1024
ATTN_HEADS = 8
HEAD_DIM = 64
ATTN_WIDTH = ATTN_HEADS * HEAD_DIM
PATTERNS = ((128, 1), (512, 4), (2048, 16))
WIN_STEPS = 128
REL_BUCKETS = 32
REL_MAX_DISTANCE = 2048
SSM_HEADS = 24
SSM_HEAD_DIM = 64
SSM_WIDTH = SSM_HEADS * SSM_HEAD_DIM
SSM_GROUPS = 4
HEADS_PER_GROUP = SSM_HEADS // SSM_GROUPS
GROUP_WIDTH = SSM_WIDTH // SSM_GROUPS
SSM_STATE = 128
SSM_CONV = 4
SSM_CHUNK = 128
CONV_CH = SSM_WIDTH + 2 * SSM_GROUPS * SSM_STATE
N_EXPERTS = 256
TOP_K = 8
N_EXPERT_GROUPS = 8
EXPERTS_PER_GROUP = N_EXPERTS // N_EXPERT_GROUPS
TOPK_GROUPS = 4
EXPERT_FF = 256
ROUTED_SCALE = 2.5
NORM_EPS = 1e-6

LANES = 128
SUBLANES = 8
NEG_BIG = -1e30
VMEM_LIMIT = 56 * 1024 * 1024


def _params(sem, vmem=VMEM_LIMIT):
    return pltpu.CompilerParams(dimension_semantics=sem, vmem_limit_bytes=vmem)


def _sigmoid(x):
    return 1.0 / (1.0 + jnp.exp(-x))


def _silu(x):
    return x * _sigmoid(x)


def _split3(x):
    hi = x.astype(BF16)
    r = x - hi.astype(F32)
    mid = r.astype(BF16)
    lo = (r - mid.astype(F32)).astype(BF16)
    return hi, mid, lo


def _dot(a, b):
    return jnp.dot(a, b, preferred_element_type=F32)


def _dot_nt(a, b):
    return lax.dot_general(a, b, (((1,), (1,)), ((), ())), preferred_element_type=F32)


def _dot_exact_rhs(a, b_exact):
    hi, mid, lo = _split3(a)
    return _dot(hi, b_exact) + _dot(mid, b_exact) + _dot(lo, b_exact)


def _dot_exact_lhs(a_exact, b):
    hi, mid, lo = _split3(b)
    return _dot(a_exact, hi) + _dot(a_exact, mid) + _dot(a_exact, lo)


def _adaln_kernel(c_ref, w_ref, b_ref, o_ref):
    s = _silu(c_ref[...]).astype(BF16)
    o_ref[...] = _dot(s, w_ref[...].astype(BF16)) + b_ref[...]


def _adaln(c, w_ada, b_ada):
    b, d = c.shape
    n = w_ada.shape[1]
    rows = SUBLANES
    c_pad = jnp.zeros((rows, d), F32).at[:b].set(c)
    tn = 1024
    out = pl.pallas_call(
        _adaln_kernel,
        grid=(n // tn,),
        in_specs=[pl.BlockSpec((rows, d), lambda j: (0, 0)),
                  pl.BlockSpec((d, tn), lambda j: (0, j)),
                  pl.BlockSpec((1, tn), lambda j: (0, j))],
        out_specs=pl.BlockSpec((rows, tn), lambda j: (0, j)),
        out_shape=jax.ShapeDtypeStruct((rows, n), F32),
        compiler_params=_params(("arbitrary",)),
        name="adaln",
    )(c_pad, w_ada, b_ada.reshape(1, n))
    return out[:b]


def _inproj_kernel(x_ref, shift_ref, scale_ref, g_ref, wqkv_ref, wz_ref, wxbc_ref, wdt_ref,
                   qg_ref, kg_ref, hmean_ref, q_ref, k_ref, v_ref, z_ref, xbc_ref, dt_ref):
    x = x_ref[...]
    ms = jnp.mean(x * x, axis=-1, keepdims=True)
    h = x * lax.rsqrt(ms + NORM_EPS) * g_ref[...]
    h = h * (1.0 + scale_ref[...]) + shift_ref[...]
    hb = h.astype(BF16)

    hmean = hmean_ref[...]

    def head_norm(t, gain):
        ss = _dot_exact_rhs(t * t, hmean)
        return t * lax.rsqrt(ss + NORM_EPS) * gain

    q = _dot(hb, wqkv_ref[:, 0:ATTN_WIDTH])
    q_ref[...] = head_norm(q, qg_ref[...]) * (HEAD_DIM ** -0.5)
    k = _dot(hb, wqkv_ref[:, ATTN_WIDTH:2 * ATTN_WIDTH])
    k_ref[...] = head_norm(k, kg_ref[...])
    v_ref[...] = _dot(hb, wqkv_ref[:, 2 * ATTN_WIDTH:3 * ATTN_WIDTH])
    for c0 in range(0, SSM_WIDTH, 512):
        z_ref[:, c0:c0 + 512] = _dot(hb, wz_ref[:, c0:c0 + 512])
    for c0 in range(0, CONV_CH, 512):
        xbc_ref[:, c0:c0 + 512] = _dot(hb, wxbc_ref[:, c0:c0 + 512])
    dt_ref[...] = _dot(hb, wdt_ref[...])


def _in_proj(x, shift, scale, gain, w_in, q_gain, k_gain, tm=256):
    b, s, d = x.shape
    t = b * s
    tiles_per_seq = s // tm
    w = w_in.astype(BF16)
    o_z = 3 * ATTN_WIDTH
    o_x = o_z + SSM_WIDTH
    o_dt = o_x + CONV_CH
    w_qkv, w_z, w_xbc = w[:, :o_z], w[:, o_z:o_x], w[:, o_x:o_dt]
    w_dt = jnp.zeros((d, LANES), BF16).at[:, :SSM_HEADS].set(w[:, o_dt:])
    head_of = np.arange(ATTN_WIDTH) // HEAD_DIM
    hmean = jnp.asarray((head_of[:, None] == head_of[None, :]).astype(np.float32) / HEAD_DIM, BF16)
    full = lambda shp: pl.BlockSpec(shp, lambda i: (0,) * len(shp))
    row = lambda n: pl.BlockSpec((tm, n), lambda i: (i, 0))
    per_batch = pl.BlockSpec((None, 1, d), lambda i: (i // tiles_per_seq, 0, 0))
    outs = pl.pallas_call(
        _inproj_kernel,
        grid=(t // tm,),
        in_specs=[row(d), per_batch, per_batch, full((1, d)),
                  full((d, o_z)), full((d, SSM_WIDTH)), full((d, CONV_CH)), full((d, LANES)),
                  full((1, ATTN_WIDTH)), full((1, ATTN_WIDTH)), full((ATTN_WIDTH, ATTN_WIDTH))],
        out_specs=[row(ATTN_WIDTH), row(ATTN_WIDTH), row(ATTN_WIDTH), row(SSM_WIDTH), row(CONV_CH), row(LANES)],
        out_shape=[jax.ShapeDtypeStruct((t, n), F32)
                   for n in (ATTN_WIDTH, ATTN_WIDTH, ATTN_WIDTH, SSM_WIDTH, CONV_CH, LANES)],
        compiler_params=_params(("arbitrary",)),
        name="in_proj",
    )(x.reshape(t, d), shift.reshape(b, 1, d), scale.reshape(b, 1, d), gain.reshape(1, d),
      w_qkv, w_z, w_xbc, w_dt,
      jnp.tile(q_gain, ATTN_HEADS).reshape(1, ATTN_WIDTH), jnp.tile(k_gain, ATTN_HEADS).reshape(1, ATTN_WIDTH), hmean)
    return outs


def _t5_causal_buckets(distance):
    n = np.maximum(distance, 0)
    max_exact = REL_BUCKETS // 2
    large = max_exact + (np.log(np.maximum(n, 1) / max_exact) / math.log(REL_MAX_DISTANCE / max_exact)
                         * (REL_BUCKETS - max_exact)).astype(np.int64)
    large = np.minimum(large, REL_BUCKETS - 1)
    return np.where(n < max_exact, n, large).astype(np.int32)


def _window_bias(rel_bias_table, dilation):
    qi = np.arange(WIN_STEPS)[:, None]
    kj = np.arange(2 * WIN_STEPS)[None, :]
    dist = qi + WIN_STEPS - kj
    band = (dist >= 0) & (dist <= WIN_STEPS)
    onehot = (_t5_causal_buckets(dist * dilation).reshape(-1, 1) == np.arange(REL_BUCKETS)[None, :]).astype(np.float32)
    bias = jnp.dot(rel_bias_table.astype(F32).T, jnp.asarray(onehot).T, precision=lax.Precision.HIGHEST)
    bias = bias.reshape(ATTN_HEADS, WIN_STEPS, 2 * WIN_STEPS)
    return jnp.where(jnp.asarray(band)[None], bias, NEG_BIG)


ATTN_TOKENS = max(w for w, _ in PATTERNS)
ATTN_UNROLL = 8


def _attn_kernel(q_ref, kp_ref, kc_ref, vp_ref, vc_ref, bias_ref, out_ref, kw, vw, o_acc, l_acc):
    tb = ATTN_TOKENS
    first = pl.program_id(2) == 0
    kw[0:tb] = kp_ref[...]
    kw[tb:2 * tb] = kc_ref[...]
    vw[0:tb] = vp_ref[...]
    vw[tb:2 * tb] = vc_ref[...]
    lane = lax.broadcasted_iota(I32, (WIN_STEPS, LANES), 1)
    head0 = lane < HEAD_DIM
    col = lax.broadcasted_iota(I32, (WIN_STEPS, 2 * WIN_STEPS), 1)
    in_prev = col < WIN_STEPS

    for p, (_, d) in enumerate(PATTERNS):
        shift = d.bit_length() - 1
        n_blocks = tb // WIN_STEPS

        def rows(start, n, d=d):
            return pl.ds(start, n, stride=d) if d > 1 else pl.ds(start, n)

        def body(it, carry, p=p, d=d, shift=shift, rows=rows):
            for u in range(ATTN_UNROLL):
                idx = it * ATTN_UNROLL + u
                r = jnp.bitwise_and(idx, d - 1)
                j = jnp.right_shift(idx, shift)
                qs = j * (WIN_STEPS * d) + r
                q = q_ref[rows(qs, WIN_STEPS), :]
                k = kw[rows(tb + qs - WIN_STEPS * d, 2 * WIN_STEPS), :].astype(BF16)
                v = vw[rows(tb + qs - WIN_STEPS * d, 2 * WIN_STEPS), :].astype(BF16)
                no_prev = jnp.logical_and(in_prev, jnp.logical_and(first, j == 0))
                o_h, lse_h = [], []
                for h in range(2):
                    qh = jnp.where(head0 if h == 0 else jnp.logical_not(head0), q, 0.0).astype(BF16)
                    s = _dot_nt(qh, k) + bias_ref[p, h]
                    s = jnp.where(no_prev, NEG_BIG, s)
                    m = jnp.max(s, axis=-1, keepdims=True)
                    e = jnp.exp(s - m)
                    denom = jnp.sum(e, axis=-1, keepdims=True)
                    o_h.append(_dot(e.astype(BF16), v) / denom)
                    lse_h.append(m + jnp.log(denom))
                o_acc[p, rows(qs, WIN_STEPS), :] = jnp.where(head0, o_h[0], o_h[1])
                l_acc[p, rows(qs, WIN_STEPS), :] = jnp.where(head0, lse_h[0], lse_h[1])
            return carry

        lax.fori_loop(0, n_blocks // ATTN_UNROLL, body, 0)

    chunk = 256
    for c0 in range(0, tb, chunk):
        l1, l2, l3 = (l_acc[p, c0:c0 + chunk, :] for p in range(3))
        m = jnp.maximum(jnp.maximum(l1, l2), l3)
        e1, e2, e3 = jnp.exp(l1 - m), jnp.exp(l2 - m), jnp.exp(l3 - m)
        num = e1 * o_acc[0, c0:c0 + chunk, :] + e2 * o_acc[1, c0:c0 + chunk, :] + e3 * o_acc[2, c0:c0 + chunk, :]
        out_ref[c0:c0 + chunk, :] = num / (e1 + e2 + e3)


def _attention(q, k, v, bias):
    b, s, w = q.shape
    tb = ATTN_TOKENS
    pairs = ATTN_HEADS // 2
    cur = pl.BlockSpec((None, tb, LANES), lambda bi, hp, i: (bi, i, hp))
    prev = pl.BlockSpec((None, tb, LANES), lambda bi, hp, i: (bi, jnp.maximum(i - 1, 0), hp))
    return pl.pallas_call(
        _attn_kernel,
        grid=(b, pairs, s // tb),
        in_specs=[cur, prev, cur, prev, cur,
                  pl.BlockSpec((len(PATTERNS), 2, WIN_STEPS, 2 * WIN_STEPS), lambda bi, hp, i: (0, hp, 0, 0))],
        out_specs=cur,
        out_shape=jax.ShapeDtypeStruct((b, s, w), F32),
        scratch_shapes=[pltpu.VMEM((2 * tb, LANES), F32), pltpu.VMEM((2 * tb, LANES), F32),
                        pltpu.VMEM((len(PATTERNS), tb, LANES), F32), pltpu.VMEM((len(PATTERNS), tb, LANES), F32)],
        compiler_params=_params(("arbitrary",) * 3),
        name="attention",
    )(q, k, k, v, v, bias)


def _ssd_kernel(xbc_ref, halo_ref, z_ref, dtraw_ref, convw_ref, convb_ref, dtb_ref, alog_ref, dskip_ref, gain_ref,
                expand_ref, tril_ref, y_ref, state_ref):
    c = pl.program_id(1)

    @pl.when(c == 0)
    def _():
        state_ref[...] = jnp.zeros_like(state_ref)

    x = xbc_ref[...]
    halo = jnp.where(c == 0, 0.0, halo_ref[...])
    w = convw_ref[...]
    acc = x * w[SSM_CONV - 1:SSM_CONV, :] + convb_ref[...]
    row8 = lax.broadcasted_iota(I32, (SUBLANES, CONV_CH), 0)
    for shift in range(1, SSM_CONV):
        xs = pltpu.roll(x, shift, axis=0)
        hs = pltpu.roll(halo, shift, axis=0)
        head = jnp.where(row8 < shift, hs, xs[0:SUBLANES])
        xs = jnp.concatenate([head, xs[SUBLANES:]], axis=0)
        acc = acc + xs * w[SSM_CONV - 1 - shift:SSM_CONV - shift, :]
    act = _silu(acc)
    x_s = act[:, :SSM_WIDTH]
    bc0 = SSM_WIDTH
    cc0 = SSM_WIDTH + SSM_GROUPS * SSM_STATE

    t = dtraw_ref[...] + dtb_ref[...]
    dt = jnp.maximum(t, 0.0) + jnp.log(1.0 + jnp.exp(-jnp.abs(t)))
    a = dt * (-jnp.exp(alog_ref[...]))
    a_cs = _dot_exact_lhs(tril_ref[...], a)
    a_cs_t = a_cs.T
    a_last = a_cs[SSM_CHUNK - 1:SSM_CHUNK, :]
    expand = expand_ref[...]
    dt_e = _dot_exact_rhs(dt, expand)
    ea_e = _dot_exact_rhs(jnp.exp(a_cs), expand)
    dte_e = _dot_exact_rhs(jnp.exp(a_last - a_cs), expand)
    xdt = x_s * dt_e
    xw = (xdt * dte_e).astype(BF16)
    xdt_b = xdt.astype(BF16)

    li = lax.broadcasted_iota(I32, (SSM_CHUNK, SSM_CHUNK), 0)
    si = lax.broadcasted_iota(I32, (SSM_CHUNK, SSM_CHUNK), 1)
    causal = li >= si

    ys = []
    for g in range(SSM_GROUPS):
        gs = slice(g * GROUP_WIDTH, (g + 1) * GROUP_WIDTH)
        b_g = act[:, bc0 + g * SSM_STATE:bc0 + (g + 1) * SSM_STATE]
        c_g = act[:, cc0 + g * SSM_STATE:cc0 + (g + 1) * SSM_STATE].astype(BF16)
        cb = _dot_nt(c_g, b_g.astype(BF16))
        state = state_ref[g]
        y_off = _dot(c_g, state.astype(BF16)) * ea_e[:, gs]
        parts = []
        for j in range(HEADS_PER_GROUP):
            hh = g * HEADS_PER_GROUP + j
            seg = a_cs[:, hh:hh + 1] - a_cs_t[hh:hh + 1, :]
            decay = jnp.exp(jnp.where(causal, seg, NEG_BIG))
            m = (cb * decay).astype(BF16)
            parts.append(_dot(m, xdt_b[:, hh * SSM_HEAD_DIM:(hh + 1) * SSM_HEAD_DIM]))
        ys.append(jnp.concatenate(parts, axis=1) + y_off)
        state_ref[g] = state * ea_e[SSM_CHUNK - 1:SSM_CHUNK, gs] + _dot(b_g.T.astype(BF16), xw[:, gs])
    y = jnp.concatenate(ys, axis=1) + dskip_ref[...] * x_s
    y = y * _silu(z_ref[...])
    gain = gain_ref[...]
    for g in range(SSM_GROUPS):
        gs = slice(g * GROUP_WIDTH, (g + 1) * GROUP_WIDTH)
        yg = y[:, gs]
        ms = jnp.mean(yg * yg, axis=-1, keepdims=True)
        y_ref[:, gs] = yg * lax.rsqrt(ms + NORM_EPS) * gain[:, gs]


def _ssd(xbc, z, dt_raw, conv_w, conv_b, dt_bias, a_log, d_skip, norm_gain, b, s):
    t = b * s
    nc = s // SSM_CHUNK
    pad_heads = lambda v: jnp.zeros((1, LANES), F32).at[0, :SSM_HEADS].set(v)
    head_of_lane = np.arange(SSM_WIDTH) // SSM_HEAD_DIM
    expand = jnp.asarray((np.arange(LANES)[:, None] == head_of_lane[None, :]).astype(np.float32), BF16)
    tril = jnp.asarray(np.tril(np.ones((SSM_CHUNK, SSM_CHUNK), np.float32)), BF16)
    halo_blocks = SSM_CHUNK // SUBLANES
    chunk = lambda n: pl.BlockSpec((SSM_CHUNK, n), lambda bi, c: (bi * nc + c, 0))
    full = lambda shp: pl.BlockSpec(shp, lambda bi, c: (0,) * len(shp))
    halo = pl.BlockSpec((SUBLANES, CONV_CH), lambda bi, c: (jnp.maximum((bi * nc + c) * halo_blocks - 1, 0), 0))
    return pl.pallas_call(
        _ssd_kernel,
        grid=(b, nc),
        in_specs=[chunk(CONV_CH), halo, chunk(SSM_WIDTH), chunk(LANES),
                  full((SSM_CONV, CONV_CH)), full((1, CONV_CH)), full((1, LANES)), full((1, LANES)),
                  full((1, SSM_WIDTH)), full((1, SSM_WIDTH)), full((LANES, SSM_WIDTH)), full((SSM_CHUNK, SSM_CHUNK))],
        out_specs=chunk(SSM_WIDTH),
        out_shape=jax.ShapeDtypeStruct((t, SSM_WIDTH), F32),
        scratch_shapes=[pltpu.VMEM((SSM_GROUPS, SSM_STATE, GROUP_WIDTH), F32)],
        compiler_params=_params(("arbitrary", "arbitrary")),
        name="ssd",
    )(xbc, xbc, z, dt_raw, conv_w, conv_b.reshape(1, CONV_CH), pad_heads(dt_bias), pad_heads(a_log),
      jnp.repeat(d_skip, SSM_HEAD_DIM).reshape(1, SSM_WIDTH), norm_gain.reshape(1, SSM_WIDTH), expand, tril)


WORD = jnp.int32
TOKEN_ROWS = D_MODEL // (2 * LANES)
HIGH_HALF = np.int32(-65536)


def _to_token_tiles(ref, x):
    n = x.shape[0]
    for c in range(TOKEN_ROWS):
        lo = lax.bitcast_convert_type(x[:, c * LANES:(c + 1) * LANES].astype(BF16).astype(F32), WORD)
        hi = lax.bitcast_convert_type(x[:, (c + TOKEN_ROWS) * LANES:(c + TOKEN_ROWS + 1) * LANES]
                                      .astype(BF16).astype(F32), WORD)
        ref[pl.ds(c, n, stride=TOKEN_ROWS), :] = jnp.bitwise_or(lax.shift_right_logical(lo, 16),
                                                                 jnp.bitwise_and(hi, HIGH_HALF))


def _from_token_tiles(ref, n, token0=0):
    lows, highs = [], []
    for c in range(TOKEN_ROWS):
        word = ref[pl.ds(token0 * TOKEN_ROWS + c, n, stride=TOKEN_ROWS), :]
        lows.append(lax.bitcast_convert_type(jnp.left_shift(word, 16), F32))
        highs.append(lax.bitcast_convert_type(jnp.bitwise_and(word, HIGH_HALF), F32))
    return jnp.concatenate(lows + highs, axis=1)


def _outproj_kernel(attn_ref, ssm_ref, x_ref, gate_ref, shift_ref, scale_ref, g_ref, wa_ref, ws_ref, x1_ref, h2_ref,
                    h2t_ref):
    mixed = _dot(attn_ref[...].astype(BF16), wa_ref[...]) + _dot(ssm_ref[...].astype(BF16), ws_ref[...])
    x1 = x_ref[...] + gate_ref[...] * mixed
    x1_ref[...] = x1
    ms = jnp.mean(x1 * x1, axis=-1, keepdims=True)
    h = x1 * lax.rsqrt(ms + NORM_EPS) * g_ref[...]
    h2 = h * (1.0 + scale_ref[...]) + shift_ref[...]
    h2_ref[...] = h2
    _to_token_tiles(h2t_ref, h2)


def _out_proj(attn, ssm, x, gate, shift, scale, gain, w_out, b, s, tm=256):
    t = b * s
    d = D_MODEL
    tiles_per_seq = s // tm
    w = w_out.astype(BF16)
    row = lambda n: pl.BlockSpec((tm, n), lambda i: (i, 0))
    full = lambda shp: pl.BlockSpec(shp, lambda i: (0,) * len(shp))
    per_batch = pl.BlockSpec((None, 1, d), lambda i: (i // tiles_per_seq, 0, 0))
    return pl.pallas_call(
        _outproj_kernel,
        grid=(t // tm,),
        in_specs=[row(ATTN_WIDTH), row(SSM_WIDTH), row(d), per_batch, per_batch, per_batch, full((1, d)),
                  full((ATTN_WIDTH, d)), full((SSM_WIDTH, d))],
        out_specs=[row(d), row(d), pl.BlockSpec((tm * TOKEN_ROWS, LANES), lambda i: (i, 0))],
        out_shape=[jax.ShapeDtypeStruct((t, d), F32)] * 2 + [jax.ShapeDtypeStruct((t * TOKEN_ROWS, LANES), WORD)],
        compiler_params=_params(("arbitrary",)),
        name="out_proj",
    )(attn.reshape(t, ATTN_WIDTH), ssm, x.reshape(t, d),
      gate.reshape(b, 1, d), shift.reshape(b, 1, d), scale.reshape(b, 1, d), gain.reshape(1, d),
      w[:ATTN_WIDTH], w[ATTN_WIDTH:])


def _mixer_sublayer(x, mod, norm_mix_gain, w_in, q_norm_gain, k_norm_gain, rel_bias_table, conv_w, conv_b, dt_bias,
                    a_log, d_skip, ssm_norm_gain, w_out, norm_ffn_gain):
    b, s, d = x.shape
    shift_m, scale_m, gate_m, shift_f, scale_f, _ = jnp.split(mod, 6, axis=-1)
    q, k, v, z, xbc, dt_raw = _in_proj(x, shift_m, scale_m, norm_mix_gain, w_in, q_norm_gain, k_norm_gain)
    bias = jnp.stack([_window_bias(rel_bias_table, dilation) for _, dilation in PATTERNS])
    attn = _attention(q.reshape(b, s, ATTN_WIDTH), k.reshape(b, s, ATTN_WIDTH), v.reshape(b, s, ATTN_WIDTH), bias)
    ssm = _ssd(xbc, z, dt_raw, conv_w, conv_b, dt_bias, a_log, d_skip, ssm_norm_gain, b, s)
    return _out_proj(attn, ssm, x, gate_m, shift_f, scale_f, norm_ffn_gain, w_out, b, s)


def _first_argmax(v, iota, limit):
    m = jnp.max(v, axis=0, keepdims=True)
    idx = jnp.min(jnp.where(v == m, iota, limit), axis=0, keepdims=True)
    return m, idx


def _router_kernel(h_ref, wt_ref, bias_ref, upper_ref, eidx_ref, rank_ref, gate_ref, counts_ref, carry_ref):
    @pl.when(pl.program_id(0) == 0)
    def _():
        carry_ref[...] = jnp.zeros_like(carry_ref)

    tm = h_ref.shape[0]
    h = h_ref[...]
    wt = wt_ref[...]
    h_hi = h.astype(BF16)
    h_lo = (h - h_hi.astype(F32)).astype(BF16)
    w_hi = wt.astype(BF16)
    w_lo = (wt - w_hi.astype(F32)).astype(BF16)
    logits = _dot_nt(w_hi, h_hi) + _dot_nt(w_hi, h_lo) + _dot_nt(w_lo, h_hi)
    scores = _sigmoid(logits)
    choice = scores + bias_ref[...]
    neg_inf = -jnp.inf

    iota_g = lax.broadcasted_iota(I32, (EXPERTS_PER_GROUP, tm), 0).astype(F32)
    group_rows = []
    for g in range(N_EXPERT_GROUPS):
        v = choice[g * EXPERTS_PER_GROUP:(g + 1) * EXPERTS_PER_GROUP]
        m1, i1 = _first_argmax(v, iota_g, float(EXPERTS_PER_GROUP))
        m2 = jnp.max(jnp.where(iota_g == i1, neg_inf, v), axis=0, keepdims=True)
        group_rows.append(m1 + m2)
    group_scores = jnp.concatenate(group_rows, axis=0)

    iota_n = lax.broadcasted_iota(I32, (N_EXPERT_GROUPS, tm), 0).astype(F32)
    chosen = jnp.zeros((N_EXPERT_GROUPS, tm), F32)
    for _ in range(TOPK_GROUPS):
        _, gi = _first_argmax(group_scores, iota_n, float(N_EXPERT_GROUPS))
        hit = iota_n == gi
        chosen = jnp.where(hit, 1.0, chosen)
        group_scores = jnp.where(hit, neg_inf, group_scores)

    masked = jnp.concatenate(
        [jnp.where(chosen[g:g + 1] > 0.0, choice[g * EXPERTS_PER_GROUP:(g + 1) * EXPERTS_PER_GROUP], neg_inf)
         for g in range(N_EXPERT_GROUPS)], axis=0)

    iota_e = lax.broadcasted_iota(I32, (N_EXPERTS, tm), 0).astype(F32)
    picked, gates = [], []
    onehot = jnp.zeros((N_EXPERTS, tm), F32)
    for _ in range(TOP_K):
        _, ei = _first_argmax(masked, iota_e, float(N_EXPERTS))
        hit = iota_e == ei
        gates.append(jnp.sum(jnp.where(hit, scores, 0.0), axis=0, keepdims=True))
        masked = jnp.where(hit, neg_inf, masked)
        onehot = jnp.where(hit, 1.0, onehot)
        picked.append(ei)
    gate_sum = gates[0]
    for gk in gates[1:]:
        gate_sum = gate_sum + gk

    base = _dot(onehot.astype(BF16), upper_ref[...]) + carry_ref[...]
    ranks = [jnp.sum(jnp.where(iota_e == ei, base, 0.0), axis=0, keepdims=True) for ei in picked]
    carry_ref[...] = carry_ref[...] + jnp.sum(onehot, axis=1, keepdims=True)

    eidx_ref[...] = jnp.concatenate(picked, axis=0).astype(I32)
    rank_ref[...] = jnp.concatenate(ranks, axis=0).astype(I32)
    gate_ref[...] = jnp.concatenate([gk / gate_sum * ROUTED_SCALE for gk in gates], axis=0)
    counts_ref[...] = carry_ref[...].astype(I32)


def _router(h2, w_router, router_bias, tm=256):
    t, d = h2.shape
    upper = jnp.asarray(np.triu(np.ones((tm, tm), np.float32), 1), BF16)
    tok = pl.BlockSpec((TOP_K, tm), lambda i: (0, i))
    full = lambda shp: pl.BlockSpec(shp, lambda i: (0,) * len(shp))
    return pl.pallas_call(
        _router_kernel,
        grid=(t // tm,),
        in_specs=[pl.BlockSpec((tm, d), lambda i: (i, 0)), full((N_EXPERTS, d)), full((N_EXPERTS, 1)), full((tm, tm))],
        out_specs=[tok, tok, tok, full((N_EXPERTS, 1))],
        out_shape=[jax.ShapeDtypeStruct((TOP_K, t), I32), jax.ShapeDtypeStruct((TOP_K, t), I32),
                   jax.ShapeDtypeStruct((TOP_K, t), F32), jax.ShapeDtypeStruct((N_EXPERTS, 1), I32)],
        scratch_shapes=[pltpu.VMEM((N_EXPERTS, 1), F32)],
        compiler_params=_params(("arbitrary",)),
        name="router",
    )(h2, w_router.T, router_bias.reshape(N_EXPERTS, 1), upper)


def _positions_kernel(counts_ref, lower_ref, eidx_ref, rank_ref, pos_ref, flat_ref):
    tm = eidx_ref.shape[1]
    counts = jnp.broadcast_to(counts_ref[...].astype(F32), (N_EXPERTS, LANES))
    offsets = _dot_exact_lhs(lower_ref[...], counts)[:, 0:1]
    iota_e = lax.broadcasted_iota(I32, (N_EXPERTS, tm), 0).astype(F32)
    e = eidx_ref[...].astype(F32)
    rows = [jnp.sum(jnp.where(iota_e == e[k:k + 1], offsets, 0.0), axis=0, keepdims=True) for k in range(TOP_K)]
    pos = jnp.concatenate(rows, axis=0).astype(I32) + rank_ref[...]
    pos_ref[0] = pos
    flat_ref[...] = pos


def _positions(counts, eidx, rank, tm):
    t = eidx.shape[1]
    lower = jnp.asarray(np.tril(np.ones((N_EXPERTS, N_EXPERTS), np.float32), -1), BF16)
    tok = pl.BlockSpec((TOP_K, tm), lambda i: (0, i))
    return pl.pallas_call(
        _positions_kernel,
        grid=(t // tm,),
        in_specs=[pl.BlockSpec((N_EXPERTS, 1), lambda i: (0, 0)), pl.BlockSpec((N_EXPERTS, N_EXPERTS), lambda i: (0, 0)),
                  tok, tok],
        out_specs=[pl.BlockSpec((1, TOP_K, tm), lambda i: (i, 0, 0)), tok],
        out_shape=[jax.ShapeDtypeStruct((t // tm, TOP_K, tm), I32), jax.ShapeDtypeStruct((TOP_K, t), I32)],
        compiler_params=_params(("arbitrary",)),
        name="positions",
    )(counts, lower, eidx, rank)


def _token_rows(i):
    return pl.ds(pl.multiple_of(i * TOKEN_ROWS, TOKEN_ROWS), TOKEN_ROWS)


def _dispatch_kernel(pos_ref, h_ref, xs_ref, sem):
    tm = h_ref.shape[0] // TOKEN_ROWS

    def row_copy(t, k):
        return pltpu.make_async_copy(h_ref.at[_token_rows(t)], xs_ref.at[_token_rows(pos_ref[0, k, t])], sem)

    def start(t, carry):
        for k in range(TOP_K):
            row_copy(t, k).start(priority=k % 2)
        return carry

    def wait(t, carry):
        for k in range(TOP_K):
            row_copy(t, k).wait()
        return carry

    lax.fori_loop(0, tm, start, 0)
    lax.fori_loop(0, tm, wait, 0)


def _dispatch(h2t, pos_tiles):
    t = h2t.shape[0] // TOKEN_ROWS
    tm = pos_tiles.shape[2]
    return pl.pallas_call(
        _dispatch_kernel,
        grid=(t // tm,),
        in_specs=[pl.BlockSpec((1, TOP_K, tm), lambda i: (i, 0, 0), memory_space=pltpu.SMEM),
                  pl.BlockSpec((tm * TOKEN_ROWS, LANES), lambda i: (i, 0))],
        out_specs=pl.BlockSpec(memory_space=pl.ANY),
        out_shape=jax.ShapeDtypeStruct(((t * TOP_K + EXPERT_BLOCK) * TOKEN_ROWS, LANES), h2t.dtype),
        scratch_shapes=[pltpu.SemaphoreType.DMA(())],
        compiler_params=_params(("arbitrary",)),
        name="dispatch",
    )(pos_tiles, h2t)


EXPERT_BLOCK = 256
TAIL_PIECES = tuple(1 << i for i in reversed(range(EXPERT_BLOCK.bit_length() - 1)))
X_SLOTS = 4
Y_SLOTS = 4


def _experts_kernel(start_ref, count_ref, nxt_ref, slot_ref, first_ref, blk0_ref, full0_ref, ptail_ref, ltail_ref,
                    blktok_ref, nblocks_ref, xs_hbm, wg_hbm, wu_hbm, wd_hbm, ys_hbm,
                    wg_buf, wu_buf, wd_buf, wg_bf, wu_bf, wd_bf, xbuf, ybuf, ytail, wsem, xsem, ysem, tsem):
    e = pl.program_id(0)
    last_step = e == pl.num_programs(0) - 1
    start, count = start_ref[e], count_ref[e]
    n_full = jnp.right_shift(count, EXPERT_BLOCK.bit_length() - 1)
    tail = jnp.bitwise_and(count, EXPERT_BLOCK - 1)
    n_blk = n_full + (tail > 0).astype(I32)
    blk0, full0 = blk0_ref[e], full0_ref[e]
    slot, nxt = slot_ref[e], nxt_ref[e]

    def token_rows(token, n):
        return pl.ds(pl.multiple_of(token * TOKEN_ROWS, TOKEN_ROWS), n * TOKEN_ROWS)

    def fetch(ex, s):
        return (pltpu.make_async_copy(wg_hbm.at[ex], wg_buf.at[s], wsem.at[s, 0]),
                pltpu.make_async_copy(wu_hbm.at[ex], wu_buf.at[s], wsem.at[s, 1]),
                pltpu.make_async_copy(wd_hbm.at[ex], wd_buf.at[s], wsem.at[s, 2]))

    def x_copy(token, s):
        return pltpu.make_async_copy(xs_hbm.at[token_rows(token, EXPERT_BLOCK)], xbuf.at[s], xsem.at[s])

    def y_copy(token, s):
        return pltpu.make_async_copy(ybuf.at[s], ys_hbm.at[token_rows(token, EXPERT_BLOCK)], ysem.at[s])

    def tail_copies(token, length):
        out = []
        for piece in TAIL_PIECES:
            bigger = (EXPERT_BLOCK - 1) & ~(2 * piece - 1)
            done = jnp.bitwise_and(length, bigger)
            cp = pltpu.make_async_copy(ytail.at[token_rows(done, piece)], ys_hbm.at[token_rows(token + done, piece)],
                                       tsem)
            out.append((jnp.bitwise_and(length, piece) != 0, cp))
        return out

    def block(s):
        x = _from_token_tiles(xbuf.at[s], EXPERT_BLOCK).astype(BF16)
        g = _dot(x, wg_bf[...])
        u = _dot(x, wu_bf[...])
        return _dot((_silu(g) * u).astype(BF16), wd_bf[...])

    @pl.when(count > 0)
    def _():
        @pl.when(first_ref[e] == 1)
        def _():
            for g in range(X_SLOTS - 1):
                @pl.when(g < nblocks_ref[0])
                def _(g=g):
                    x_copy(blktok_ref[g], g).start()
            for cp in fetch(e, slot):
                cp.start()

        for cp in fetch(e, slot):
            cp.wait()

        @pl.when(nxt >= 0)
        def _():
            for cp in fetch(nxt, 1 - slot):
                cp.start()

        wg_bf[...] = wg_buf[slot].astype(BF16)
        wu_bf[...] = wu_buf[slot].astype(BF16)
        wd_bf[...] = wd_buf[slot].astype(BF16)

        def step(i):
            g = blk0 + i
            xs_slot = jnp.bitwise_and(g, X_SLOTS - 1)
            x_copy(start, xs_slot).wait()
            ahead = g + (X_SLOTS - 1)

            @pl.when(ahead < nblocks_ref[0])
            def _():
                x_copy(blktok_ref[ahead], jnp.bitwise_and(ahead, X_SLOTS - 1)).start()

            return block(xs_slot)

        def full_block(i, carry):
            y = step(i)
            j = full0 + i
            ys_slot = jnp.bitwise_and(j, Y_SLOTS - 1)

            @pl.when(j >= Y_SLOTS)
            def _():
                y_copy(start, ys_slot).wait()

            _to_token_tiles(ybuf.at[ys_slot], y)
            y_copy(start + i * EXPERT_BLOCK, ys_slot).start()
            return carry

        lax.fori_loop(0, n_full, full_block, 0)

        @pl.when(tail > 0)
        def _():
            y = step(n_full)
            for pred, cp in tail_copies(start, ptail_ref[e]):
                @pl.when(pred)
                def _(cp=cp):
                    cp.wait()
            _to_token_tiles(ytail, y)
            for pred, cp in tail_copies(start + n_full * EXPERT_BLOCK, tail):
                @pl.when(pred)
                def _(cp=cp):
                    cp.start()

    @pl.when(last_step)
    def _():
        total_full = full0 + n_full
        for back in range(1, Y_SLOTS + 1):
            @pl.when(total_full >= back)
            def _(back=back):
                y_copy(0, jnp.bitwise_and(total_full - back, Y_SLOTS - 1)).wait()
        for pred, cp in tail_copies(0, ltail_ref[0]):
            @pl.when(pred)
            def _(cp=cp):
                cp.wait()


def _max_expert_blocks(n_rows):
    return n_rows // EXPERT_BLOCK + N_EXPERTS


def _expert_metadata(counts, n_rows):
    ids = jnp.arange(N_EXPERTS, dtype=I32)
    used = counts > 0
    starts = jnp.cumsum(counts) - counts
    n_blk = (counts + EXPERT_BLOCK - 1) // EXPERT_BLOCK
    n_full = counts // EXPERT_BLOCK
    tail = counts % EXPERT_BLOCK
    blk0 = jnp.cumsum(n_blk) - n_blk
    full0 = jnp.cumsum(n_full) - n_full
    next_used = lax.cummin(jnp.where(used, ids, N_EXPERTS), reverse=True)
    next_after = jnp.concatenate([next_used[1:], jnp.full((1,), N_EXPERTS, I32)])
    nxt = jnp.where(next_after < N_EXPERTS, next_after, -1)
    ordinal = jnp.cumsum(used.astype(I32)) - 1
    slot = ordinal % 2
    first = jnp.logical_and(used, ordinal == 0)
    latest = lax.cummax(jnp.where(tail > 0, ids, -1))
    before = jnp.concatenate([jnp.full((1,), -1, I32), latest[:-1]])
    pick = lambda index, values: jnp.sum(jnp.where(index[:, None] == ids[None, :], values[None, :], 0), axis=1)
    ptail = pick(before, tail)
    ltail = pick(latest[-1:], tail)
    block_ends = jnp.cumsum(n_blk)
    g = jnp.arange(_max_expert_blocks(n_rows), dtype=I32)
    eg = jnp.sum((g[:, None] >= block_ends[None, :]).astype(I32), axis=1)
    blktok = g * EXPERT_BLOCK + pick(eg, starts - blk0 * EXPERT_BLOCK)
    return tuple(v.astype(I32) for v in (starts, counts, nxt, slot, first, blk0, full0, ptail, ltail, blktok,
                                         block_ends[-1:]))


def _experts(xs, counts, w_gate, w_up, w_down):
    d = D_MODEL
    meta = _expert_metadata(counts, xs.shape[0] // TOKEN_ROWS - EXPERT_BLOCK)
    hbm = pl.BlockSpec(memory_space=pl.ANY)
    blk = (EXPERT_BLOCK * TOKEN_ROWS, LANES)
    grid_spec = pltpu.PrefetchScalarGridSpec(
        num_scalar_prefetch=len(meta),
        grid=(N_EXPERTS,),
        in_specs=[hbm, hbm, hbm, hbm],
        out_specs=hbm,
        scratch_shapes=[pltpu.VMEM((2, d, EXPERT_FF), F32), pltpu.VMEM((2, d, EXPERT_FF), F32),
                        pltpu.VMEM((2, EXPERT_FF, d), F32),
                        pltpu.VMEM((d, EXPERT_FF), BF16), pltpu.VMEM((d, EXPERT_FF), BF16),
                        pltpu.VMEM((EXPERT_FF, d), BF16),
                        pltpu.VMEM((X_SLOTS,) + blk, WORD), pltpu.VMEM((Y_SLOTS,) + blk, WORD), pltpu.VMEM(blk, WORD),
                        pltpu.SemaphoreType.DMA((2, 3)), pltpu.SemaphoreType.DMA((X_SLOTS,)),
                        pltpu.SemaphoreType.DMA((Y_SLOTS,)), pltpu.SemaphoreType.DMA(())],
    )
    return pl.pallas_call(
        _experts_kernel,
        grid_spec=grid_spec,
        out_shape=jax.ShapeDtypeStruct(xs.shape, WORD),
        compiler_params=_params(("arbitrary",)),
        name="experts",
    )(*meta, xs, w_gate, w_up, w_down)


SC_CORES = 2
SC_SUBCORES = 16
SC_WORKERS = SC_CORES * SC_SUBCORES
SC_CHUNK = 128
SC_RING = 4


def _sc_gather_rows(table, idx):
    n_workers, n_chunks, chunk = idx.shape
    rows_per_worker = n_chunks * chunk
    mesh = plsc.VectorSubcoreMesh(core_axis_name="c", subcore_axis_name="s", num_cores=SC_CORES,
                                  num_subcores=SC_SUBCORES)

    def body(table_hbm, idx_hbm, out_hbm, idx_v, rows_v, gsem, wsem):
        wid = lax.axis_index("s") * SC_CORES + lax.axis_index("c")
        base = wid * rows_per_worker
        pltpu.sync_copy(idx_hbm.at[wid], idx_v)

        def write(slot, c):
            return pltpu.make_async_copy(rows_v.at[slot], out_hbm.at[pl.ds(base + c * chunk, chunk)], wsem.at[slot])

        def gather(slot, c):
            return pltpu.make_async_copy(table_hbm.at[idx_v.at[c]], rows_v.at[slot], gsem.at[slot])

        @pl.loop(0, n_chunks, step=SC_RING)
        def _(g):
            for slot in range(SC_RING):
                @pl.when(g > 0)
                def _(slot=slot):
                    write(slot, 0).wait()
                gather(slot, g + slot).start()
            for slot in range(SC_RING):
                gather(slot, g + slot).wait()
                write(slot, g + slot).start()

        for slot in range(SC_RING):
            write(slot, 0).wait()

    return pl.kernel(
        body, mesh=mesh,
        out_type=jax.ShapeDtypeStruct((n_workers * rows_per_worker, LANES), table.dtype),
        scratch_types=[pltpu.VMEM((n_chunks, chunk), I32), pltpu.VMEM((SC_RING, chunk, LANES), table.dtype),
                       pltpu.SemaphoreType.DMA((SC_RING,)), pltpu.SemaphoreType.DMA((SC_RING,))],
        name="sc_gather",
    )(table, idx)


def _combine_kernel(rows_ref, gates_ref, h_ref, x1_ref, gatef_ref, wg_ref, wu_ref, wd_ref, out_ref):
    hb = h_ref[...].astype(BF16)
    shared = _dot((_silu(_dot(hb, wg_ref[...])) * _dot(hb, wu_ref[...])).astype(BF16), wd_ref[...])
    gates = gates_ref[...]

    def expert_rows(k):
        words = [rows_ref[k * TOKEN_ROWS + c] for c in range(TOKEN_ROWS)]
        lows = [lax.bitcast_convert_type(jnp.left_shift(w, 16), F32) for w in words]
        highs = [lax.bitcast_convert_type(jnp.bitwise_and(w, HIGH_HALF), F32) for w in words]
        return jnp.concatenate(lows + highs, axis=1)

    routed = expert_rows(0) * gates[:, 0:1]
    for k in range(1, TOP_K):
        routed = routed + expert_rows(k) * gates[:, k:k + 1]
    out_ref[...] = x1_ref[...] + gatef_ref[...] * (shared + routed)


def _combine(gathered, gates_t, h2, x1, gate_f, w_gate_s, w_up_s, w_down_s, b, s, tm):
    t, d = h2.shape
    tiles_per_seq = s // tm
    row = lambda n: pl.BlockSpec((tm, n), lambda i: (i, 0))
    full = lambda shp: pl.BlockSpec(shp, lambda i: (0,) * len(shp))
    return pl.pallas_call(
        _combine_kernel,
        grid=(t // tm,),
        in_specs=[pl.BlockSpec((TOP_K * TOKEN_ROWS, tm, LANES), lambda i: (0, i, 0)),
                  row(TOP_K), row(d), row(d),
                  pl.BlockSpec((None, 1, d), lambda i: (i // tiles_per_seq, 0, 0)),
                  full((d, EXPERT_FF)), full((d, EXPERT_FF)), full((EXPERT_FF, d))],
        out_specs=row(d),
        out_shape=jax.ShapeDtypeStruct((t, d), F32),
        compiler_params=_params(("arbitrary",)),
        name="combine",
    )(gathered, gates_t, h2, x1, gate_f.reshape(b, 1, d),
      w_gate_s.astype(BF16), w_up_s.astype(BF16), w_down_s.astype(BF16))


def _moe_sublayer(x1, h2, h2t, gate_f, w_router, router_bias, w_gate, w_up, w_down, w_gate_s, w_up_s, w_down_s, b, s,
                  tm=256):
    t = b * s
    eidx, rank, gates, counts = _router(h2, w_router, router_bias)
    pos_tiles, pos = _positions(counts, eidx, rank, tm)
    xs = _dispatch(h2t, pos_tiles)
    ys = _experts(xs, counts[:, 0], w_gate, w_up, w_down)
    idx = pos[:, None, :] * TOKEN_ROWS + jnp.arange(TOKEN_ROWS, dtype=I32)[None, :, None]
    gathered = _sc_gather_rows(ys, idx.reshape(SC_WORKERS, -1, SC_CHUNK)).reshape(TOP_K * TOKEN_ROWS, t, LANES)
    return _combine(gathered, gates.T, h2, x1, gate_f, w_gate_s, w_up_s, w_down_s, b, s, tm)


def kernel(x, c, w_ada, b_ada, norm_mix_gain, w_in, q_norm_gain, k_norm_gain, rel_bias_table, conv_w, conv_b, dt_bias,
           a_log, d_skip, ssm_norm_gain, w_out, norm_ffn_gain, w_router, router_bias, w_gate_experts, w_up_experts,
           w_down_experts, w_gate_shared, w_up_shared, w_down_shared):
    b, s, d = x.shape
    for layer in range(w_ada.shape[0]):
        mod = _adaln(c, w_ada[layer], b_ada[layer])
        x1, h2, h2t = _mixer_sublayer(x, mod, norm_mix_gain[layer], w_in[layer], q_norm_gain[layer], k_norm_gain[layer],
                                 rel_bias_table, conv_w[layer], conv_b[layer], dt_bias[layer], a_log[layer],
                                 d_skip[layer], ssm_norm_gain[layer], w_out[layer], norm_ffn_gain[layer])
        gate_f = mod[:, 5 * d:]
        out = _moe_sublayer(x1, h2, h2t, gate_f, w_router[layer], router_bias[layer], w_gate_experts[layer],
                            w_up_experts[layer], w_down_experts[layer], w_gate_shared[layer], w_up_shared[layer],
                            w_down_shared[layer], b, s)
        x = out.reshape(b, s, d)
    return x
```

```python
import functools
import math

import numpy as np
import jax
import jax.numpy as jnp
from jax import lax
from jax.experimental import pallas as pl
from jax.experimental.pallas import tpu as pltpu
from jax.experimental.pallas import tpu_sc as plsc

F32 = jnp.float32
BF16 = jnp.bfloat16
I32 = jnp.int32

D_MODEL = 1024
ATTN_HEADS = 8
HEAD_DIM = 64
ATTN_WIDTH = ATTN_HEADS * HEAD_DIM
PATTERNS = ((128, 1), (512, 4), (2048, 16))
WIN_STEPS = 128
REL_BUCKETS = 32
REL_MAX_DISTANCE = 2048
SSM_HEADS = 24
SSM_HEAD_DIM = 64
SSM_WIDTH = SSM_HEADS * SSM_HEAD_DIM
SSM_GROUPS = 4
HEADS_PER_GROUP = SSM_HEADS // SSM_GROUPS
GROUP_WIDTH = SSM_WIDTH // SSM_GROUPS
SSM_STATE = 128
SSM_CONV = 4
SSM_CHUNK = 128
CONV_CH = SSM_WIDTH + 2 * SSM_GROUPS * SSM_STATE
N_EXPERTS = 256
TOP_K = 8
N_EXPERT_GROUPS = 8
EXPERTS_PER_GROUP = N_EXPERTS // N_EXPERT_GROUPS
TOPK_GROUPS = 4
EXPERT_FF = 256
ROUTED_SCALE = 2.5
NORM_EPS = 1e-6

LANES = 128
SUBLANES = 8
NEG_BIG = -1e30
VMEM_LIMIT = 56 * 1024 * 1024


def _params(sem, vmem=VMEM_LIMIT):
    return pltpu.CompilerParams(dimension_semantics=sem, vmem_limit_bytes=vmem)


def _sigmoid(x):
    return 1.0 / (1.0 + jnp.exp(-x))


def _silu(x):
    return x * _sigmoid(x)


def _split3(x):
    hi = x.astype(BF16)
    r = x - hi.astype(F32)
    mid = r.astype(BF16)
    lo = (r - mid.astype(F32)).astype(BF16)
    return hi, mid, lo


def _dot(a, b):
    return jnp.dot(a, b, preferred_element_type=F32)


def _dot_nt(a, b):
    return lax.dot_general(a, b, (((1,), (1,)), ((), ())), preferred_element_type=F32)


def _dot_exact_rhs(a, b_exact):
    hi, mid, lo = _split3(a)
    return _dot(hi, b_exact) + _dot(mid, b_exact) + _dot(lo, b_exact)


def _dot_exact_lhs(a_exact, b):
    hi, mid, lo = _split3(b)
    return _dot(a_exact, hi) + _dot(a_exact, mid) + _dot(a_exact, lo)


def _adaln_kernel(c_ref, w_ref, b_ref, o_ref):
    s = _silu(c_ref[...]).astype(BF16)
    o_ref[...] = _dot(s, w_ref[...].astype(BF16)) + b_ref[...]


def _adaln(c, w_ada, b_ada):
    b, d = c.shape
    n = w_ada.shape[1]
    rows = SUBLANES
    c_pad = jnp.zeros((rows, d), F32).at[:b].set(c)
    tn = 1024
    out = pl.pallas_call(
        _adaln_kernel,
        grid=(n // tn,),
        in_specs=[pl.BlockSpec((rows, d), lambda j: (0, 0)),
                  pl.BlockSpec((d, tn), lambda j: (0, j)),
                  pl.BlockSpec((1, tn), lambda j: (0, j))],
        out_specs=pl.BlockSpec((rows, tn), lambda j: (0, j)),
        out_shape=jax.ShapeDtypeStruct((rows, n), F32),
        compiler_params=_params(("arbitrary",)),
        name="adaln",
    )(c_pad, w_ada, b_ada.reshape(1, n))
    return out[:b]


def _inproj_kernel(x_ref, shift_ref, scale_ref, g_ref, wqkv_ref, wz_ref, wxbc_ref, wdt_ref,
                   qg_ref, kg_ref, hmean_ref, q_ref, k_ref, v_ref, z_ref, xbc_ref, dt_ref):
    x = x_ref[...]
    ms = jnp.mean(x * x, axis=-1, keepdims=True)
    h = x * lax.rsqrt(ms + NORM_EPS) * g_ref[...]
    h = h * (1.0 + scale_ref[...]) + shift_ref[...]
    hb = h.astype(BF16)

    hmean = hmean_ref[...]

    def head_norm(t, gain):
        ss = _dot_exact_rhs(t * t, hmean)
        return t * lax.rsqrt(ss + NORM_EPS) * gain

    q = _dot(hb, wqkv_ref[:, 0:ATTN_WIDTH])
    q_ref[...] = head_norm(q, qg_ref[...]) * (HEAD_DIM ** -0.5)
    k = _dot(hb, wqkv_ref[:, ATTN_WIDTH:2 * ATTN_WIDTH])
    k_ref[...] = head_norm(k, kg_ref[...])
    v_ref[...] = _dot(hb, wqkv_ref[:, 2 * ATTN_WIDTH:3 * ATTN_WIDTH])
    for c0 in range(0, SSM_WIDTH, 512):
        z_ref[:, c0:c0 + 512] = _dot(hb, wz_ref[:, c0:c0 + 512])
    for c0 in range(0, CONV_CH, 512):
        xbc_ref[:, c0:c0 + 512] = _dot(hb, wxbc_ref[:, c0:c0 + 512])
    dt_ref[...] = _dot(hb, wdt_ref[...])


def _in_proj(x, shift, scale, gain, w_in, q_gain, k_gain, tm=256):
    b, s, d = x.shape
    t = b * s
    tiles_per_seq = s // tm
    w = w_in.astype(BF16)
    o_z = 3 * ATTN_WIDTH
    o_x = o_z + SSM_WIDTH
    o_dt = o_x + CONV_CH
    w_qkv, w_z, w_xbc = w[:, :o_z], w[:, o_z:o_x], w[:, o_x:o_dt]
    w_dt = jnp.zeros((d, LANES), BF16).at[:, :SSM_HEADS].set(w[:, o_dt:])
    head_of = np.arange(ATTN_WIDTH) // HEAD_DIM
    hmean = jnp.asarray((head_of[:, None] == head_of[None, :]).astype(np.float32) / HEAD_DIM, BF16)
    full = lambda shp: pl.BlockSpec(shp, lambda i: (0,) * len(shp))
    row = lambda n: pl.BlockSpec((tm, n), lambda i: (i, 0))
    per_batch = pl.BlockSpec((None, 1, d), lambda i: (i // tiles_per_seq, 0, 0))
    outs = pl.pallas_call(
        _inproj_kernel,
        grid=(t // tm,),
        in_specs=[row(d), per_batch, per_batch, full((1, d)),
                  full((d, o_z)), full((d, SSM_WIDTH)), full((d, CONV_CH)), full((d, LANES)),
                  full((1, ATTN_WIDTH)), full((1, ATTN_WIDTH)), full((ATTN_WIDTH, ATTN_WIDTH))],
        out_specs=[row(ATTN_WIDTH), row(ATTN_WIDTH), row(ATTN_WIDTH), row(SSM_WIDTH), row(CONV_CH), row(LANES)],
        out_shape=[jax.ShapeDtypeStruct((t, n), F32)
                   for n in (ATTN_WIDTH, ATTN_WIDTH, ATTN_WIDTH, SSM_WIDTH, CONV_CH, LANES)],
        compiler_params=_params(("arbitrary",)),
        name="in_proj",
    )(x.reshape(t, d), shift.reshape(b, 1, d), scale.reshape(b, 1, d), gain.reshape(1, d),
      w_qkv, w_z, w_xbc, w_dt,
      jnp.tile(q_gain, ATTN_HEADS).reshape(1, ATTN_WIDTH), jnp.tile(k_gain, ATTN_HEADS).reshape(1, ATTN_WIDTH), hmean)
    return outs


def _t5_causal_buckets(distance):
    n = np.maximum(distance, 0)
    max_exact = REL_BUCKETS // 2
    large = max_exact + (np.log(np.maximum(n, 1) / max_exact) / math.log(REL_MAX_DISTANCE / max_exact)
                         * (REL_BUCKETS - max_exact)).astype(np.int64)
    large = np.minimum(large, REL_BUCKETS - 1)
    return np.where(n < max_exact, n, large).astype(np.int32)


def _window_bias(rel_bias_table, dilation):
    qi = np.arange(WIN_STEPS)[:, None]
    kj = np.arange(2 * WIN_STEPS)[None, :]
    dist = qi + WIN_STEPS - kj
    band = (dist >= 0) & (dist <= WIN_STEPS)
    onehot = (_t5_causal_buckets(dist * dilation).reshape(-1, 1) == np.arange(REL_BUCKETS)[None, :]).astype(np.float32)
    bias = jnp.dot(rel_bias_table.astype(F32).T, jnp.asarray(onehot).T, precision=lax.Precision.HIGHEST)
    bias = bias.reshape(ATTN_HEADS, WIN_STEPS, 2 * WIN_STEPS)
    return jnp.where(jnp.asarray(band)[None], bias, NEG_BIG)


ATTN_TOKENS = max(w for w, _ in PATTERNS)
ATTN_UNROLL = 8


def _attn_kernel(q_ref, kp_ref, kc_ref, vp_ref, vc_ref, bias_ref, out_ref, kw, vw, o_acc, l_acc):
    tb = ATTN_TOKENS
    first = pl.program_id(2) == 0
    kw[0:tb] = kp_ref[...]
    kw[tb:2 * tb] = kc_ref[...]
    vw[0:tb] = vp_ref[...]
    vw[tb:2 * tb] = vc_ref[...]
    lane = lax.broadcasted_iota(I32, (WIN_STEPS, LANES), 1)
    head0 = lane < HEAD_DIM
    col = lax.broadcasted_iota(I32, (WIN_STEPS, 2 * WIN_STEPS), 1)
    in_prev = col < WIN_STEPS

    for p, (_, d) in enumerate(PATTERNS):
        shift = d.bit_length() - 1
        n_blocks = tb // WIN_STEPS

        def rows(start, n, d=d):
            return pl.ds(start, n, stride=d) if d > 1 else pl.ds(start, n)

        def body(it, carry, p=p, d=d, shift=shift, rows=rows):
            for u in range(ATTN_UNROLL):
                idx = it * ATTN_UNROLL + u
                r = jnp.bitwise_and(idx, d - 1)
                j = jnp.right_shift(idx, shift)
                qs = j * (WIN_STEPS * d) + r
                q = q_ref[rows(qs, WIN_STEPS), :]
                k = kw[rows(tb + qs - WIN_STEPS * d, 2 * WIN_STEPS), :].astype(BF16)
                v = vw[rows(tb + qs - WIN_STEPS * d, 2 * WIN_STEPS), :].astype(BF16)
                no_prev = jnp.logical_and(in_prev, jnp.logical_and(first, j == 0))
                o_h, lse_h = [], []
                for h in range(2):
                    qh = jnp.where(head0 if h == 0 else jnp.logical_not(head0), q, 0.0).astype(BF16)
                    s = _dot_nt(qh, k) + bias_ref[p, h]
                    s = jnp.where(no_prev, NEG_BIG, s)
                    m = jnp.max(s, axis=-1, keepdims=True)
                    e = jnp.exp(s - m)
                    denom = jnp.sum(e, axis=-1, keepdims=True)
                    o_h.append(_dot(e.astype(BF16), v) / denom)
                    lse_h.append(m + jnp.log(denom))
                o_acc[p, rows(qs, WIN_STEPS), :] = jnp.where(head0, o_h[0], o_h[1])
                l_acc[p, rows(qs, WIN_STEPS), :] = jnp.where(head0, lse_h[0], lse_h[1])
            return carry

        lax.fori_loop(0, n_blocks // ATTN_UNROLL, body, 0)

    chunk = 256
    for c0 in range(0, tb, chunk):
        l1, l2, l3 = (l_acc[p, c0:c0 + chunk, :] for p in range(3))
        m = jnp.maximum(jnp.maximum(l1, l2), l3)
        e1, e2, e3 = jnp.exp(l1 - m), jnp.exp(l2 - m), jnp.exp(l3 - m)
        num = e1 * o_acc[0, c0:c0 + chunk, :] + e2 * o_acc[1, c0:c0 + chunk, :] + e3 * o_acc[2, c0:c0 + chunk, :]
        out_ref[c0:c0 + chunk, :] = num / (e1 + e2 + e3)


def _attention(q, k, v, bias):
    b, s, w = q.shape
    tb = ATTN_TOKENS
    pairs = ATTN_HEADS // 2
    cur = pl.BlockSpec((None, tb, LANES), lambda bi, hp, i: (bi, i, hp))
    prev = pl.BlockSpec((None, tb, LANES), lambda bi, hp, i: (bi, jnp.maximum(i - 1, 0), hp))
    return pl.pallas_call(
        _attn_kernel,
        grid=(b, pairs, s // tb),
        in_specs=[cur, prev, cur, prev, cur,
                  pl.BlockSpec((len(PATTERNS), 2, WIN_STEPS, 2 * WIN_STEPS), lambda bi, hp, i: (0, hp, 0, 0))],
        out_specs=cur,
        out_shape=jax.ShapeDtypeStruct((b, s, w), F32),
        scratch_shapes=[pltpu.VMEM((2 * tb, LANES), F32), pltpu.VMEM((2 * tb, LANES), F32),
                        pltpu.VMEM((len(PATTERNS), tb, LANES), F32), pltpu.VMEM((len(PATTERNS), tb, LANES), F32)],
        compiler_params=_params(("arbitrary",) * 3),
        name="attention",
    )(q, k, k, v, v, bias)


def _ssd_kernel(xbc_ref, halo_ref, z_ref, dtraw_ref, convw_ref, convb_ref, dtb_ref, alog_ref, dskip_ref, gain_ref,
                expand_ref, tril_ref, y_ref, state_ref):
    c = pl.program_id(1)

    @pl.when(c == 0)
    def _():
        state_ref[...] = jnp.zeros_like(state_ref)

    x = xbc_ref[...]
    halo = jnp.where(c == 0, 0.0, halo_ref[...])
    w = convw_ref[...]
    acc = x * w[SSM_CONV - 1:SSM_CONV, :] + convb_ref[...]
    row8 = lax.broadcasted_iota(I32, (SUBLANES, CONV_CH), 0)
    for shift in range(1, SSM_CONV):
        xs = pltpu.roll(x, shift, axis=0)
        hs = pltpu.roll(halo, shift, axis=0)
        head = jnp.where(row8 < shift, hs, xs[0:SUBLANES])
        xs = jnp.concatenate([head, xs[SUBLANES:]], axis=0)
        acc = acc + xs * w[SSM_CONV - 1 - shift:SSM_CONV - shift, :]
    act = _silu(acc)
    x_s = act[:, :SSM_WIDTH]
    bc0 = SSM_WIDTH
    cc0 = SSM_WIDTH + SSM_GROUPS * SSM_STATE

    t = dtraw_ref[...] + dtb_ref[...]
    dt = jnp.maximum(t, 0.0) + jnp.log(1.0 + jnp.exp(-jnp.abs(t)))
    a = dt * (-jnp.exp(alog_ref[...]))
    a_cs = _dot_exact_lhs(tril_ref[...], a)
    a_cs_t = a_cs.T
    a_last = a_cs[SSM_CHUNK - 1:SSM_CHUNK, :]
    expand = expand_ref[...]
    dt_e = _dot_exact_rhs(dt, expand)
    ea_e = _dot_exact_rhs(jnp.exp(a_cs), expand)
    dte_e = _dot_exact_rhs(jnp.exp(a_last - a_cs), expand)
    xdt = x_s * dt_e
    xw = (xdt * dte_e).astype(BF16)
    xdt_b = xdt.astype(BF16)

    li = lax.broadcasted_iota(I32, (SSM_CHUNK, SSM_CHUNK), 0)
    si = lax.broadcasted_iota(I32, (SSM_CHUNK, SSM_CHUNK), 1)
    causal = li >= si

    ys = []
    for g in range(SSM_GROUPS):
        gs = slice(g * GROUP_WIDTH, (g + 1) * GROUP_WIDTH)
        b_g = act[:, bc0 + g * SSM_STATE:bc0 + (g + 1) * SSM_STATE]
        c_g = act[:, cc0 + g * SSM_STATE:cc0 + (g + 1) * SSM_STATE].astype(BF16)
        cb = _dot_nt(c_g, b_g.astype(BF16))
        state = state_ref[g]
        y_off = _dot(c_g, state.astype(BF16)) * ea_e[:, gs]
        parts = []
        for j in range(HEADS_PER_GROUP):
            hh = g * HEADS_PER_GROUP + j
            seg = a_cs[:, hh:hh + 1] - a_cs_t[hh:hh + 1, :]
            decay = jnp.exp(jnp.where(causal, seg, NEG_BIG))
            m = (cb * decay).astype(BF16)
            parts.append(_dot(m, xdt_b[:, hh * SSM_HEAD_DIM:(hh + 1) * SSM_HEAD_DIM]))
        ys.append(jnp.concatenate(parts, axis=1) + y_off)
        state_ref[g] = state * ea_e[SSM_CHUNK - 1:SSM_CHUNK, gs] + _dot(b_g.T.astype(BF16), xw[:, gs])
    y = jnp.concatenate(ys, axis=1) + dskip_ref[...] * x_s
    y = y * _silu(z_ref[...])
    gain = gain_ref[...]
    for g in range(SSM_GROUPS):
        gs = slice(g * GROUP_WIDTH, (g + 1) * GROUP_WIDTH)
        yg = y[:, gs]
        ms = jnp.mean(yg * yg, axis=-1, keepdims=True)
        y_ref[:, gs] = yg * lax.rsqrt(ms + NORM_EPS) * gain[:, gs]


def _ssd(xbc, z, dt_raw, conv_w, conv_b, dt_bias, a_log, d_skip, norm_gain, b, s):
    t = b * s
    nc = s // SSM_CHUNK
    pad_heads = lambda v: jnp.zeros((1, LANES), F32).at[0, :SSM_HEADS].set(v)
    head_of_lane = np.arange(SSM_WIDTH) // SSM_HEAD_DIM
    expand = jnp.asarray((np.arange(LANES)[:, None] == head_of_lane[None, :]).astype(np.float32), BF16)
    tril = jnp.asarray(np.tril(np.ones((SSM_CHUNK, SSM_CHUNK), np.float32)), BF16)
    halo_blocks = SSM_CHUNK // SUBLANES
    chunk = lambda n: pl.BlockSpec((SSM_CHUNK, n), lambda bi, c: (bi * nc + c, 0))
    full = lambda shp: pl.BlockSpec(shp, lambda bi, c: (0,) * len(shp))
    halo = pl.BlockSpec((SUBLANES, CONV_CH), lambda bi, c: (jnp.maximum((bi * nc + c) * halo_blocks - 1, 0), 0))
    return pl.pallas_call(
        _ssd_kernel,
        grid=(b, nc),
        in_specs=[chunk(CONV_CH), halo, chunk(SSM_WIDTH), chunk(LANES),
                  full((SSM_CONV, CONV_CH)), full((1, CONV_CH)), full((1, LANES)), full((1, LANES)),
                  full((1, SSM_WIDTH)), full((1, SSM_WIDTH)), full((LANES, SSM_WIDTH)), full((SSM_CHUNK, SSM_CHUNK))],
        out_specs=chunk(SSM_WIDTH),
        out_shape=jax.ShapeDtypeStruct((t, SSM_WIDTH), F32),
        scratch_shapes=[pltpu.VMEM((SSM_GROUPS, SSM_STATE, GROUP_WIDTH), F32)],
        compiler_params=_params(("arbitrary", "arbitrary")),
        name="ssd",
    )(xbc, xbc, z, dt_raw, conv_w, conv_b.reshape(1, CONV_CH), pad_heads(dt_bias), pad_heads(a_log),
      jnp.repeat(d_skip, SSM_HEAD_DIM).reshape(1, SSM_WIDTH), norm_gain.reshape(1, SSM_WIDTH), expand, tril)


WORD = jnp.int32
TOKEN_ROWS = D_MODEL // (2 * LANES)
HIGH_HALF = np.int32(-65536)


def _to_token_tiles(ref, x):
    n = x.shape[0]
    for c in range(TOKEN_ROWS):
        lo = lax.bitcast_convert_type(x[:, c * LANES:(c + 1) * LANES].astype(BF16).astype(F32), WORD)
        hi = lax.bitcast_convert_type(x[:, (c + TOKEN_ROWS) * LANES:(c + TOKEN_ROWS + 1) * LANES]
                                      .astype(BF16).astype(F32), WORD)
        ref[pl.ds(c, n, stride=TOKEN_ROWS), :] = jnp.bitwise_or(lax.shift_right_logical(lo, 16),
                                                                 jnp.bitwise_and(hi, HIGH_HALF))


def _from_token_tiles(ref, n, token0=0):
    lows, highs = [], []
    for c in range(TOKEN_ROWS):
        word = ref[pl.ds(token0 * TOKEN_ROWS + c, n, stride=TOKEN_ROWS), :]
        lows.append(lax.bitcast_convert_type(jnp.left_shift(word, 16), F32))
        highs.append(lax.bitcast_convert_type(jnp.bitwise_and(word, HIGH_HALF), F32))
    return jnp.concatenate(lows + highs, axis=1)


def _outproj_kernel(attn_ref, ssm_ref, x_ref, gate_ref, shift_ref, scale_ref, g_ref, wa_ref, ws_ref, x1_ref, h2_ref,
                    h2t_ref):
    mixed = _dot(attn_ref[...].astype(BF16), wa_ref[...]) + _dot(ssm_ref[...].astype(BF16), ws_ref[...])
    x1 = x_ref[...] + gate_ref[...] * mixed
    x1_ref[...] = x1
    ms = jnp.mean(x1 * x1, axis=-1, keepdims=True)
    h = x1 * lax.rsqrt(ms + NORM_EPS) * g_ref[...]
    h2 = h * (1.0 + scale_ref[...]) + shift_ref[...]
    h2_ref[...] = h2
    for c in range(TOKEN_ROWS):
        lo = lax.bitcast_convert_type(h2[:, c * LANES:(c + 1) * LANES].astype(BF16).astype(F32), WORD)
        hi = lax.bitcast_convert_type(h2[:, (c + TOKEN_ROWS) * LANES:(c + TOKEN_ROWS + 1) * LANES]
                                      .astype(BF16).astype(F32), WORD)
        h2t_ref[c] = jnp.bitwise_or(lax.shift_right_logical(lo, 16), jnp.bitwise_and(hi, HIGH_HALF))


def _out_proj(attn, ssm, x, gate, shift, scale, gain, w_out, b, s, tm=256):
    t = b * s
    d = D_MODEL
    tiles_per_seq = s // tm
    w = w_out.astype(BF16)
    row = lambda n: pl.BlockSpec((tm, n), lambda i: (i, 0))
    full = lambda shp: pl.BlockSpec(shp, lambda i: (0,) * len(shp))
    per_batch = pl.BlockSpec((None, 1, d), lambda i: (i // tiles_per_seq, 0, 0))
    return pl.pallas_call(
        _outproj_kernel,
        grid=(t // tm,),
        in_specs=[row(ATTN_WIDTH), row(SSM_WIDTH), row(d), per_batch, per_batch, per_batch, full((1, d)),
                  full((ATTN_WIDTH, d)), full((SSM_WIDTH, d))],
        out_specs=[row(d), row(d), pl.BlockSpec((TOKEN_ROWS, tm, LANES), lambda i: (0, i, 0))],
        out_shape=[jax.ShapeDtypeStruct((t, d), F32)] * 2 + [jax.ShapeDtypeStruct((TOKEN_ROWS, t, LANES), WORD)],
        compiler_params=_params(("arbitrary",)),
        name="out_proj",
    )(attn.reshape(t, ATTN_WIDTH), ssm, x.reshape(t, d),
      gate.reshape(b, 1, d), shift.reshape(b, 1, d), scale.reshape(b, 1, d), gain.reshape(1, d),
      w[:ATTN_WIDTH], w[ATTN_WIDTH:])


def _mixer_sublayer(x, mod, norm_mix_gain, w_in, q_norm_gain, k_norm_gain, rel_bias_table, conv_w, conv_b, dt_bias,
                    a_log, d_skip, ssm_norm_gain, w_out, norm_ffn_gain):
    b, s, d = x.shape
    shift_m, scale_m, gate_m, shift_f, scale_f, _ = jnp.split(mod, 6, axis=-1)
    q, k, v, z, xbc, dt_raw = _in_proj(x, shift_m, scale_m, norm_mix_gain, w_in, q_norm_gain, k_norm_gain)
    bias = jnp.stack([_window_bias(rel_bias_table, dilation) for _, dilation in PATTERNS])
    attn = _attention(q.reshape(b, s, ATTN_WIDTH), k.reshape(b, s, ATTN_WIDTH), v.reshape(b, s, ATTN_WIDTH), bias)
    ssm = _ssd(xbc, z, dt_raw, conv_w, conv_b, dt_bias, a_log, d_skip, ssm_norm_gain, b, s)
    return _out_proj(attn, ssm, x, gate_m, shift_f, scale_f, norm_ffn_gain, w_out, b, s)


def _first_argmax(v, iota, limit):
    m = jnp.max(v, axis=0, keepdims=True)
    idx = jnp.min(jnp.where(v == m, iota, limit), axis=0, keepdims=True)
    return m, idx


def _router_kernel(h_ref, wt_ref, bias_ref, upper_ref, eidx_ref, rank_ref, gate_ref, counts_ref, carry_ref):
    @pl.when(pl.program_id(0) == 0)
    def _():
        carry_ref[...] = jnp.zeros_like(carry_ref)

    tm = h_ref.shape[0]
    h = h_ref[...]
    wt = wt_ref[...]
    h_hi = h.astype(BF16)
    h_lo = (h - h_hi.astype(F32)).astype(BF16)
    w_hi = wt.astype(BF16)
    w_lo = (wt - w_hi.astype(F32)).astype(BF16)
    logits = _dot_nt(w_hi, h_hi) + _dot_nt(w_hi, h_lo) + _dot_nt(w_lo, h_hi)
    scores = _sigmoid(logits)
    choice = scores + bias_ref[...]
    neg_inf = -jnp.inf

    iota_g = lax.broadcasted_iota(I32, (EXPERTS_PER_GROUP, tm), 0).astype(F32)
    group_rows = []
    for g in range(N_EXPERT_GROUPS):
        v = choice[g * EXPERTS_PER_GROUP:(g + 1) * EXPERTS_PER_GROUP]
        m1, i1 = _first_argmax(v, iota_g, float(EXPERTS_PER_GROUP))
        m2 = jnp.max(jnp.where(iota_g == i1, neg_inf, v), axis=0, keepdims=True)
        group_rows.append(m1 + m2)
    group_scores = jnp.concatenate(group_rows, axis=0)

    iota_n = lax.broadcasted_iota(I32, (N_EXPERT_GROUPS, tm), 0).astype(F32)
    chosen = jnp.zeros((N_EXPERT_GROUPS, tm), F32)
    for _ in range(TOPK_GROUPS):
        _, gi = _first_argmax(group_scores, iota_n, float(N_EXPERT_GROUPS))
        hit = iota_n == gi
        chosen = jnp.where(hit, 1.0, chosen)
        group_scores = jnp.where(hit, neg_inf, group_scores)

    masked = jnp.concatenate(
        [jnp.where(chosen[g:g + 1] > 0.0, choice[g * EXPERTS_PER_GROUP:(g + 1) * EXPERTS_PER_GROUP], neg_inf)
         for g in range(N_EXPERT_GROUPS)], axis=0)

    iota_e = lax.broadcasted_iota(I32, (N_EXPERTS, tm), 0).astype(F32)
    picked, gates = [], []
    onehot = jnp.zeros((N_EXPERTS, tm), F32)
    for _ in range(TOP_K):
        _, ei = _first_argmax(masked, iota_e, float(N_EXPERTS))
        hit = iota_e == ei
        gates.append(jnp.sum(jnp.where(hit, scores, 0.0), axis=0, keepdims=True))
        masked = jnp.where(hit, neg_inf, masked)
        onehot = jnp.where(hit, 1.0, onehot)
        picked.append(ei)
    gate_sum = gates[0]
    for gk in gates[1:]:
        gate_sum = gate_sum + gk

    base = _dot(onehot.astype(BF16), upper_ref[...]) + carry_ref[...]
    ranks = [jnp.sum(jnp.where(iota_e == ei, base, 0.0), axis=0, keepdims=True) for ei in picked]
    carry_ref[...] = carry_ref[...] + jnp.sum(onehot, axis=1, keepdims=True)

    eidx_ref[...] = jnp.concatenate(picked, axis=0).astype(I32)
    rank_ref[...] = jnp.concatenate(ranks, axis=0).astype(I32)
    gate_ref[...] = jnp.concatenate([gk / gate_sum * ROUTED_SCALE for gk in gates], axis=0)
    counts_ref[...] = carry_ref[...].astype(I32)


def _router(h2, w_router, router_bias, tm=256):
    t, d = h2.shape
    upper = jnp.asarray(np.triu(np.ones((tm, tm), np.float32), 1), BF16)
    tok = pl.BlockSpec((TOP_K, tm), lambda i: (0, i))
    full = lambda shp: pl.BlockSpec(shp, lambda i: (0,) * len(shp))
    return pl.pallas_call(
        _router_kernel,
        grid=(t // tm,),
        in_specs=[pl.BlockSpec((tm, d), lambda i: (i, 0)), full((N_EXPERTS, d)), full((N_EXPERTS, 1)), full((tm, tm))],
        out_specs=[tok, tok, tok, full((N_EXPERTS, 1))],
        out_shape=[jax.ShapeDtypeStruct((TOP_K, t), I32), jax.ShapeDtypeStruct((TOP_K, t), I32),
                   jax.ShapeDtypeStruct((TOP_K, t), F32), jax.ShapeDtypeStruct((N_EXPERTS, 1), I32)],
        scratch_shapes=[pltpu.VMEM((N_EXPERTS, 1), F32)],
        compiler_params=_params(("arbitrary",)),
        name="router",
    )(h2, w_router.T, router_bias.reshape(N_EXPERTS, 1), upper)


def _positions_kernel(counts_ref, lower_ref, eidx_ref, rank_ref, pos_ref, flat_ref):
    tm = eidx_ref.shape[1]
    counts = jnp.broadcast_to(counts_ref[...].astype(F32), (N_EXPERTS, LANES))
    offsets = _dot_exact_lhs(lower_ref[...], counts)[:, 0:1]
    iota_e = lax.broadcasted_iota(I32, (N_EXPERTS, tm), 0).astype(F32)
    e = eidx_ref[...].astype(F32)
    rows = [jnp.sum(jnp.where(iota_e == e[k:k + 1], offsets, 0.0), axis=0, keepdims=True) for k in range(TOP_K)]
    pos = jnp.concatenate(rows, axis=0).astype(I32) + rank_ref[...]
    pos_ref[0] = pos
    flat_ref[...] = pos


def _positions(counts, eidx, rank, tm):
    t = eidx.shape[1]
    lower = jnp.asarray(np.tril(np.ones((N_EXPERTS, N_EXPERTS), np.float32), -1), BF16)
    tok = pl.BlockSpec((TOP_K, tm), lambda i: (0, i))
    return pl.pallas_call(
        _positions_kernel,
        grid=(t // tm,),
        in_specs=[pl.BlockSpec((N_EXPERTS, 1), lambda i: (0, 0)), pl.BlockSpec((N_EXPERTS, N_EXPERTS), lambda i: (0, 0)),
                  tok, tok],
        out_specs=[pl.BlockSpec((1, TOP_K, tm), lambda i: (i, 0, 0)), tok],
        out_shape=[jax.ShapeDtypeStruct((t // tm, TOP_K, tm), I32), jax.ShapeDtypeStruct((TOP_K, t), I32)],
        compiler_params=_params(("arbitrary",)),
        name="positions",
    )(counts, lower, eidx, rank)


def _token_rows(i):
    return pl.ds(pl.multiple_of(i * TOKEN_ROWS, TOKEN_ROWS), TOKEN_ROWS)


def _dispatch_kernel(pos_ref, h_ref, xs_ref, sem):
    tm = h_ref.shape[0] // TOKEN_ROWS

    def row_copy(t, k):
        return pltpu.make_async_copy(h_ref.at[_token_rows(t)], xs_ref.at[_token_rows(pos_ref[0, k, t])], sem)

    def start(t, carry):
        for k in range(TOP_K):
            row_copy(t, k).start(priority=k % 2)
        return carry

    def wait(t, carry):
        for k in range(TOP_K):
            row_copy(t, k).wait()
        return carry

    lax.fori_loop(0, tm, start, 0)
    lax.fori_loop(0, tm, wait, 0)


def _dispatch(h2t, pos_tiles):
    t = h2t.shape[0] // TOKEN_ROWS
    tm = pos_tiles.shape[2]
    return pl.pallas_call(
        _dispatch_kernel,
        grid=(t // tm,),
        in_specs=[pl.BlockSpec((1, TOP_K, tm), lambda i: (i, 0, 0), memory_space=pltpu.SMEM),
                  pl.BlockSpec((tm * TOKEN_ROWS, LANES), lambda i: (i, 0))],
        out_specs=pl.BlockSpec(memory_space=pl.ANY),
        out_shape=jax.ShapeDtypeStruct(((t * TOP_K + EXPERT_BLOCK) * TOKEN_ROWS, LANES), h2t.dtype),
        scratch_shapes=[pltpu.SemaphoreType.DMA(())],
        compiler_params=_params(("arbitrary",)),
        name="dispatch",
    )(pos_tiles, h2t)


EXPERT_BLOCK = 256
TAIL_PIECES = tuple(1 << i for i in reversed(range(EXPERT_BLOCK.bit_length() - 1)))
X_SLOTS = 4
Y_SLOTS = 4


def _experts_kernel(start_ref, count_ref, nxt_ref, slot_ref, first_ref, blk0_ref, full0_ref, ptail_ref, ltail_ref,
                    blktok_ref, nblocks_ref, xs_hbm, wg_hbm, wu_hbm, wd_hbm, ys_hbm,
                    wg_buf, wu_buf, wd_buf, wg_bf, wu_bf, wd_bf, xbuf, ybuf, ytail, wsem, xsem, ysem, tsem):
    e = pl.program_id(0)
    last_step = e == pl.num_programs(0) - 1
    start, count = start_ref[e], count_ref[e]
    n_full = jnp.right_shift(count, EXPERT_BLOCK.bit_length() - 1)
    tail = jnp.bitwise_and(count, EXPERT_BLOCK - 1)
    n_blk = n_full + (tail > 0).astype(I32)
    blk0, full0 = blk0_ref[e], full0_ref[e]
    slot, nxt = slot_ref[e], nxt_ref[e]

    def token_rows(token, n):
        return pl.ds(pl.multiple_of(token * TOKEN_ROWS, TOKEN_ROWS), n * TOKEN_ROWS)

    def fetch(ex, s):
        return (pltpu.make_async_copy(wg_hbm.at[ex], wg_buf.at[s], wsem.at[s, 0]),
                pltpu.make_async_copy(wu_hbm.at[ex], wu_buf.at[s], wsem.at[s, 1]),
                pltpu.make_async_copy(wd_hbm.at[ex], wd_buf.at[s], wsem.at[s, 2]))

    def x_copy(token, s):
        return pltpu.make_async_copy(xs_hbm.at[token_rows(token, EXPERT_BLOCK)], xbuf.at[s], xsem.at[s])

    def y_copy(token, s):
        return pltpu.make_async_copy(ybuf.at[s], ys_hbm.at[token_rows(token, EXPERT_BLOCK)], ysem.at[s])

    def tail_copies(token, length):
        out = []
        for piece in TAIL_PIECES:
            bigger = (EXPERT_BLOCK - 1) & ~(2 * piece - 1)
            done = jnp.bitwise_and(length, bigger)
            cp = pltpu.make_async_copy(ytail.at[token_rows(done, piece)], ys_hbm.at[token_rows(token + done, piece)],
                                       tsem)
            out.append((jnp.bitwise_and(length, piece) != 0, cp))
        return out

    def block(s):
        x = _from_token_tiles(xbuf.at[s], EXPERT_BLOCK).astype(BF16)
        g = _dot(x, wg_bf[...])
        u = _dot(x, wu_bf[...])
        return _dot((_silu(g) * u).astype(BF16), wd_bf[...])

    @pl.when(count > 0)
    def _():
        @pl.when(first_ref[e] == 1)
        def _():
            for g in range(X_SLOTS - 1):
                @pl.when(g < nblocks_ref[0])
                def _(g=g):
                    x_copy(blktok_ref[g], g).start()
            for cp in fetch(e, slot):
                cp.start()

        for cp in fetch(e, slot):
            cp.wait()

        @pl.when(nxt >= 0)
        def _():
            for cp in fetch(nxt, 1 - slot):
                cp.start()

        wg_bf[...] = wg_buf[slot].astype(BF16)
        wu_bf[...] = wu_buf[slot].astype(BF16)
        wd_bf[...] = wd_buf[slot].astype(BF16)

        def step(i):
            g = blk0 + i
            xs_slot = jnp.bitwise_and(g, X_SLOTS - 1)
            x_copy(start, xs_slot).wait()
            ahead = g + (X_SLOTS - 1)

            @pl.when(ahead < nblocks_ref[0])
            def _():
                x_copy(blktok_ref[ahead], jnp.bitwise_and(ahead, X_SLOTS - 1)).start()

            return block(xs_slot)

        def full_block(i, carry):
            y = step(i)
            j = full0 + i
            ys_slot = jnp.bitwise_and(j, Y_SLOTS - 1)

            @pl.when(j >= Y_SLOTS)
            def _():
                y_copy(start, ys_slot).wait()

            _to_token_tiles(ybuf.at[ys_slot], y)
            y_copy(start + i * EXPERT_BLOCK, ys_slot).start()
            return carry

        lax.fori_loop(0, n_full, full_block, 0)

        @pl.when(tail > 0)
        def _():
            y = step(n_full)
            for pred, cp in tail_copies(start, ptail_ref[e]):
                @pl.when(pred)
                def _(cp=cp):
                    cp.wait()
            _to_token_tiles(ytail, y)
            for pred, cp in tail_copies(start + n_full * EXPERT_BLOCK, tail):
                @pl.when(pred)
                def _(cp=cp):
                    cp.start()

    @pl.when(last_step)
    def _():
        total_full = full0 + n_full
        for back in range(1, Y_SLOTS + 1):
            @pl.when(total_full >= back)
            def _(back=back):
                y_copy(0, jnp.bitwise_and(total_full - back, Y_SLOTS - 1)).wait()
        for pred, cp in tail_copies(0, ltail_ref[0]):
            @pl.when(pred)
            def _(cp=cp):
                cp.wait()


def _max_expert_blocks(n_rows):
    return n_rows // EXPERT_BLOCK + N_EXPERTS


def _expert_metadata(counts, n_rows):
    ids = jnp.arange(N_EXPERTS, dtype=I32)
    used = counts > 0
    starts = jnp.cumsum(counts) - counts
    n_blk = (counts + EXPERT_BLOCK - 1) // EXPERT_BLOCK
    n_full = counts // EXPERT_BLOCK
    tail = counts % EXPERT_BLOCK
    blk0 = jnp.cumsum(n_blk) - n_blk
    full0 = jnp.cumsum(n_full) - n_full
    next_used = lax.cummin(jnp.where(used, ids, N_EXPERTS), reverse=True)
    next_after = jnp.concatenate([next_used[1:], jnp.full((1,), N_EXPERTS, I32)])
    nxt = jnp.where(next_after < N_EXPERTS, next_after, -1)
    ordinal = jnp.cumsum(used.astype(I32)) - 1
    slot = ordinal % 2
    first = jnp.logical_and(used, ordinal == 0)
    latest = lax.cummax(jnp.where(tail > 0, ids, -1))
    before = jnp.concatenate([jnp.full((1,), -1, I32), latest[:-1]])
    pick = lambda index, values: jnp.sum(jnp.where(index[:, None] == ids[None, :], values[None, :], 0), axis=1)
    ptail = pick(before, tail)
    ltail = pick(latest[-1:], tail)
    block_ends = jnp.cumsum(n_blk)
    g = jnp.arange(_max_expert_blocks(n_rows), dtype=I32)
    eg = jnp.sum((g[:, None] >= block_ends[None, :]).astype(I32), axis=1)
    blktok = g * EXPERT_BLOCK + pick(eg, starts - blk0 * EXPERT_BLOCK)
    return tuple(v.astype(I32) for v in (starts, counts, nxt, slot, first, blk0, full0, ptail, ltail, blktok,
                                         block_ends[-1:]))


def _experts(xs, counts, w_gate, w_up, w_down):
    d = D_MODEL
    meta = _expert_metadata(counts, xs.shape[0] // TOKEN_ROWS - EXPERT_BLOCK)
    hbm = pl.BlockSpec(memory_space=pl.ANY)
    blk = (EXPERT_BLOCK * TOKEN_ROWS, LANES)
    grid_spec = pltpu.PrefetchScalarGridSpec(
        num_scalar_prefetch=len(meta),
        grid=(N_EXPERTS,),
        in_specs=[hbm, hbm, hbm, hbm],
        out_specs=hbm,
        scratch_shapes=[pltpu.VMEM((2, d, EXPERT_FF), F32), pltpu.VMEM((2, d, EXPERT_FF), F32),
                        pltpu.VMEM((2, EXPERT_FF, d), F32),
                        pltpu.VMEM((d, EXPERT_FF), BF16), pltpu.VMEM((d, EXPERT_FF), BF16),
                        pltpu.VMEM((EXPERT_FF, d), BF16),
                        pltpu.VMEM((X_SLOTS,) + blk, WORD), pltpu.VMEM((Y_SLOTS,) + blk, WORD), pltpu.VMEM(blk, WORD),
                        pltpu.SemaphoreType.DMA((2, 3)), pltpu.SemaphoreType.DMA((X_SLOTS,)),
                        pltpu.SemaphoreType.DMA((Y_SLOTS,)), pltpu.SemaphoreType.DMA(())],
    )
    return pl.pallas_call(
        _experts_kernel,
        grid_spec=grid_spec,
        out_shape=jax.ShapeDtypeStruct(xs.shape, WORD),
        compiler_params=_params(("arbitrary",)),
        name="experts",
    )(*meta, xs, w_gate, w_up, w_down)


SC_CORES = 2
SC_SUBCORES = 16
SC_WORKERS = SC_CORES * SC_SUBCORES
SC_CHUNK = 128
SC_RING = 4


def _sc_scatter_rows(planes, idx, n_out_rows):
    n_planes, t, _ = planes.shape
    ranges = SC_WORKERS // n_planes
    n_chunks = t // ranges // SC_CHUNK
    idx = idx.reshape(TOP_K, n_planes, ranges, n_chunks, SC_CHUNK)
    mesh = plsc.VectorSubcoreMesh(core_axis_name="c", subcore_axis_name="s", num_cores=SC_CORES,
                                  num_subcores=SC_SUBCORES)

    def body(planes_hbm, idx_hbm, out_hbm, idx_v, rows_v, lsem, ssem):
        wid = lax.axis_index("s") * SC_CORES + lax.axis_index("c")
        plane = wid % n_planes
        token0 = (wid // n_planes) * (n_chunks * SC_CHUNK)
        for k in range(TOP_K):
            pltpu.sync_copy(idx_hbm.at[k, plane, wid // n_planes], idx_v.at[k])

        def load(slot, c):
            return pltpu.make_async_copy(planes_hbm.at[plane, pl.ds(token0 + c * SC_CHUNK, SC_CHUNK)], rows_v.at[slot],
                                         lsem.at[slot])

        def scatter(slot, c, k):
            return pltpu.make_async_copy(rows_v.at[slot], out_hbm.at[idx_v.at[k, c]], ssem.at[slot])

        @pl.loop(0, n_chunks, step=SC_RING)
        def _(g):
            for slot in range(SC_RING):
                @pl.when(g > 0)
                def _(slot=slot):
                    for k in range(TOP_K):
                        scatter(slot, 0, k).wait()
                load(slot, g + slot).start()
            for slot in range(SC_RING):
                load(slot, g + slot).wait()
                for k in range(TOP_K):
                    scatter(slot, g + slot, k).start()

        for slot in range(SC_RING):
            for k in range(TOP_K):
                scatter(slot, 0, k).wait()

    return pl.kernel(
        body, mesh=mesh,
        out_type=jax.ShapeDtypeStruct((n_out_rows, LANES), planes.dtype),
        scratch_types=[pltpu.VMEM((TOP_K, n_chunks, SC_CHUNK), I32), pltpu.VMEM((SC_RING, SC_CHUNK, LANES), planes.dtype),
                       pltpu.SemaphoreType.DMA((SC_RING,)), pltpu.SemaphoreType.DMA((SC_RING,))],
        name="sc_scatter",
    )(planes, idx)


def _sc_gather_rows(table, idx):
    n_workers, n_chunks, chunk = idx.shape
    rows_per_worker = n_chunks * chunk
    mesh = plsc.VectorSubcoreMesh(core_axis_name="c", subcore_axis_name="s", num_cores=SC_CORES,
                                  num_subcores=SC_SUBCORES)

    def body(table_hbm, idx_hbm, out_hbm, idx_v, rows_v, gsem, wsem):
        wid = lax.axis_index("s") * SC_CORES + lax.axis_index("c")
        base = wid * rows_per_worker
        pltpu.sync_copy(idx_hbm.at[wid], idx_v)

        def write(slot, c):
            return pltpu.make_async_copy(rows_v.at[slot], out_hbm.at[pl.ds(base + c * chunk, chunk)], wsem.at[slot])

        def gather(slot, c):
            return pltpu.make_async_copy(table_hbm.at[idx_v.at[c]], rows_v.at[slot], gsem.at[slot])

        @pl.loop(0, n_chunks, step=SC_RING)
        def _(g):
            for slot in range(SC_RING):
                @pl.when(g > 0)
                def _(slot=slot):
                    write(slot, 0).wait()
                gather(slot, g + slot).start()
            for slot in range(SC_RING):
                gather(slot, g + slot).wait()
                write(slot, g + slot).start()

        for slot in range(SC_RING):
            write(slot, 0).wait()

    return pl.kernel(
        body, mesh=mesh,
        out_type=jax.ShapeDtypeStruct((n_workers * rows_per_worker, LANES), table.dtype),
        scratch_types=[pltpu.VMEM((n_chunks, chunk), I32), pltpu.VMEM((SC_RING, chunk, LANES), table.dtype),
                       pltpu.SemaphoreType.DMA((SC_RING,)), pltpu.SemaphoreType.DMA((SC_RING,))],
        name="sc_gather",
    )(table, idx)


def _combine_kernel(rows_ref, gates_ref, h_ref, x1_ref, gatef_ref, wg_ref, wu_ref, wd_ref, out_ref):
    hb = h_ref[...].astype(BF16)
    shared = _dot((_silu(_dot(hb, wg_ref[...])) * _dot(hb, wu_ref[...])).astype(BF16), wd_ref[...])
    gates = gates_ref[...]

    def expert_rows(k):
        words = [rows_ref[k * TOKEN_ROWS + c] for c in range(TOKEN_ROWS)]
        lows = [lax.bitcast_convert_type(jnp.left_shift(w, 16), F32) for w in words]
        highs = [lax.bitcast_convert_type(jnp.bitwise_and(w, HIGH_HALF), F32) for w in words]
        return jnp.concatenate(lows + highs, axis=1)

    routed = expert_rows(0) * gates[:, 0:1]
    for k in range(1, TOP_K):
        routed = routed + expert_rows(k) * gates[:, k:k + 1]
    out_ref[...] = x1_ref[...] + gatef_ref[...] * (shared + routed)


def _combine(gathered, gates_t, h2, x1, gate_f, w_gate_s, w_up_s, w_down_s, b, s, tm):
    t, d = h2.shape
    tiles_per_seq = s // tm
    row = lambda n: pl.BlockSpec((tm, n), lambda i: (i, 0))
    full = lambda shp: pl.BlockSpec(shp, lambda i: (0,) * len(shp))
    return pl.pallas_call(
        _combine_kernel,
        grid=(t // tm,),
        in_specs=[pl.BlockSpec((TOP_K * TOKEN_ROWS, tm, LANES), lambda i: (0, i, 0)),
                  row(TOP_K), row(d), row(d),
                  pl.BlockSpec((None, 1, d), lambda i: (i // tiles_per_seq, 0, 0)),
                  full((d, EXPERT_FF)), full((d, EXPERT_FF)), full((EXPERT_FF, d))],
        out_specs=row(d),
        out_shape=jax.ShapeDtypeStruct((t, d), F32),
        compiler_params=_params(("arbitrary",)),
        name="combine",
    )(gathered, gates_t, h2, x1, gate_f.reshape(b, 1, d),
      w_gate_s.astype(BF16), w_up_s.astype(BF16), w_down_s.astype(BF16))


def _moe_sublayer(x1, h2, h2t, gate_f, w_router, router_bias, w_gate, w_up, w_down, w_gate_s, w_up_s, w_down_s, b, s,
                  tm=256):
    t = b * s
    eidx, rank, gates, counts = _router(h2, w_router, router_bias)
    _, pos = _positions(counts, eidx, rank, tm)
    idx = pos[:, None, :] * TOKEN_ROWS + jnp.arange(TOKEN_ROWS, dtype=I32)[None, :, None]
    xs = _sc_scatter_rows(h2t, idx, (t * TOP_K + EXPERT_BLOCK) * TOKEN_ROWS)
    ys = _experts(xs, counts[:, 0], w_gate, w_up, w_down)
    gathered = _sc_gather_rows(ys, idx.reshape(SC_WORKERS, -1, SC_CHUNK)).reshape(TOP_K * TOKEN_ROWS, t, LANES)
    return _combine(gathered, gates.T, h2, x1, gate_f, w_gate_s, w_up_s, w_down_s, b, s, tm)


def kernel(x, c, w_ada, b_ada, norm_mix_gain, w_in, q_norm_gain, k_norm_gain, rel_bias_table, conv_w, conv_b, dt_bias,
           a_log, d_skip, ssm_norm_gain, w_out, norm_ffn_gain, w_router, router_bias, w_gate_experts, w_up_experts,
           w_down_experts, w_gate_shared, w_up_shared, w_down_shared):
    b, s, d = x.shape
    for layer in range(w_ada.shape[0]):
        mod = _adaln(c, w_ada[layer], b_ada[layer])
        x1, h2, h2t = _mixer_sublayer(x, mod, norm_mix_gain[layer], w_in[layer], q_norm_gain[layer], k_norm_gain[layer],
                                 rel_bias_table, conv_w[layer], conv_b[layer], dt_bias[layer], a_log[layer],
                                 d_skip[layer], ssm_norm_gain[layer], w_out[layer], norm_ffn_gain[layer])
        gate_f = mod[:, 5 * d:]
        out = _moe_sublayer(x1, h2, h2t, gate_f, w_router[layer], router_bias[layer], w_gate_experts[layer],
                            w_up_experts[layer], w_down_experts[layer], w_gate_shared[layer], w_up_shared[layer],
                            w_down_shared[layer], b, s)
        x = out.reshape(b, s, d)
    return x
```

```python
import functools
import math

import numpy as np
import jax
import jax.numpy as jnp
from jax import lax
from jax.experimental import pallas as pl
from jax.experimental.pallas import tpu as pltpu
from jax.experimental.pallas import tpu_sc as plsc

F32 = jnp.float32
BF16 = jnp.bfloat16
I32 = jnp.int32

D_MODEL = 1024
ATTN_HEADS = 8
HEAD_DIM = 64
ATTN_WIDTH = ATTN_HEADS * HEAD_DIM
PATTERNS = ((128, 1), (512, 4), (2048, 16))
WIN_STEPS = 128
REL_BUCKETS = 32
REL_MAX_DISTANCE = 2048
SSM_HEADS = 24
SSM_HEAD_DIM = 64
SSM_WIDTH = SSM_HEADS * SSM_HEAD_DIM
SSM_GROUPS = 4
HEADS_PER_GROUP = SSM_HEADS // SSM_GROUPS
GROUP_WIDTH = SSM_WIDTH // SSM_GROUPS
SSM_STATE = 128
SSM_CONV = 4
SSM_CHUNK = 128
CONV_CH = SSM_WIDTH + 2 * SSM_GROUPS * SSM_STATE
N_EXPERTS = 256
TOP_K = 8
N_EXPERT_GROUPS = 8
EXPERTS_PER_GROUP = N_EXPERTS // N_EXPERT_GROUPS
TOPK_GROUPS = 4
EXPERT_FF = 256
ROUTED_SCALE = 2.5
NORM_EPS = 1e-6

LANES = 128
SUBLANES = 8
NEG_BIG = -1e30
VMEM_LIMIT = 56 * 1024 * 1024


def _params(sem, vmem=VMEM_LIMIT):
    return pltpu.CompilerParams(dimension_semantics=sem, vmem_limit_bytes=vmem)


def _sigmoid(x):
    return 1.0 / (1.0 + jnp.exp(-x))


def _silu(x):
    return x * _sigmoid(x)


def _split3(x):
    hi = x.astype(BF16)
    r = x - hi.astype(F32)
    mid = r.astype(BF16)
    lo = (r - mid.astype(F32)).astype(BF16)
    return hi, mid, lo


def _dot(a, b):
    return jnp.dot(a, b, preferred_element_type=F32)


def _dot_nt(a, b):
    return lax.dot_general(a, b, (((1,), (1,)), ((), ())), preferred_element_type=F32)


def _dot_exact_rhs(a, b_exact):
    hi, mid, lo = _split3(a)
    return _dot(hi, b_exact) + _dot(mid, b_exact) + _dot(lo, b_exact)


def _dot_exact_lhs(a_exact, b):
    hi, mid, lo = _split3(b)
    return _dot(a_exact, hi) + _dot(a_exact, mid) + _dot(a_exact, lo)


def _adaln_kernel(c_ref, w_ref, b_ref, o_ref):
    s = _silu(c_ref[...]).astype(BF16)
    o_ref[...] = _dot(s, w_ref[...].astype(BF16)) + b_ref[...]


def _adaln(c, w_ada, b_ada):
    b, d = c.shape
    n = w_ada.shape[1]
    rows = SUBLANES
    c_pad = jnp.zeros((rows, d), F32).at[:b].set(c)
    tn = 1024
    out = pl.pallas_call(
        _adaln_kernel,
        grid=(n // tn,),
        in_specs=[pl.BlockSpec((rows, d), lambda j: (0, 0)),
                  pl.BlockSpec((d, tn), lambda j: (0, j)),
                  pl.BlockSpec((1, tn), lambda j: (0, j))],
        out_specs=pl.BlockSpec((rows, tn), lambda j: (0, j)),
        out_shape=jax.ShapeDtypeStruct((rows, n), F32),
        compiler_params=_params(("arbitrary",)),
        name="adaln",
    )(c_pad, w_ada, b_ada.reshape(1, n))
    return out[:b]


def _inproj_kernel(x_ref, shift_ref, scale_ref, g_ref, wqkv_ref, wz_ref, wxbc_ref, wdt_ref,
                   qg_ref, kg_ref, hmean_ref, q_ref, k_ref, v_ref, z_ref, xbc_ref, dt_ref):
    x = x_ref[...]
    ms = jnp.mean(x * x, axis=-1, keepdims=True)
    h = x * lax.rsqrt(ms + NORM_EPS) * g_ref[...]
    h = h * (1.0 + scale_ref[...]) + shift_ref[...]
    hb = h.astype(BF16)

    hmean = hmean_ref[...]

    def head_norm(t, gain):
        ss = _dot_exact_rhs(t * t, hmean)
        return t * lax.rsqrt(ss + NORM_EPS) * gain

    q = _dot(hb, wqkv_ref[:, 0:ATTN_WIDTH])
    q_ref[...] = head_norm(q, qg_ref[...]) * (HEAD_DIM ** -0.5)
    k = _dot(hb, wqkv_ref[:, ATTN_WIDTH:2 * ATTN_WIDTH])
    k_ref[...] = head_norm(k, kg_ref[...])
    v_ref[...] = _dot(hb, wqkv_ref[:, 2 * ATTN_WIDTH:3 * ATTN_WIDTH])
    for c0 in range(0, SSM_WIDTH, 512):
        z_ref[:, c0:c0 + 512] = _dot(hb, wz_ref[:, c0:c0 + 512])
    for c0 in range(0, CONV_CH, 512):
        xbc_ref[:, c0:c0 + 512] = _dot(hb, wxbc_ref[:, c0:c0 + 512])
    dt_ref[...] = _dot(hb, wdt_ref[...])


def _in_proj(x, shift, scale, gain, w_in, q_gain, k_gain, tm=256):
    b, s, d = x.shape
    t = b * s
    tiles_per_seq = s // tm
    w = w_in.astype(BF16)
    o_z = 3 * ATTN_WIDTH
    o_x = o_z + SSM_WIDTH
    o_dt = o_x + CONV_CH
    w_qkv, w_z, w_xbc = w[:, :o_z], w[:, o_z:o_x], w[:, o_x:o_dt]
    w_dt = jnp.zeros((d, LANES), BF16).at[:, :SSM_HEADS].set(w[:, o_dt:])
    head_of = np.arange(ATTN_WIDTH) // HEAD_DIM
    hmean = jnp.asarray((head_of[:, None] == head_of[None, :]).astype(np.float32) / HEAD_DIM, BF16)
    full = lambda shp: pl.BlockSpec(shp, lambda i: (0,) * len(shp))
    row = lambda n: pl.BlockSpec((tm, n), lambda i: (i, 0))
    per_batch = pl.BlockSpec((None, 1, d), lambda i: (i // tiles_per_seq, 0, 0))
    outs = pl.pallas_call(
        _inproj_kernel,
        grid=(t // tm,),
        in_specs=[row(d), per_batch, per_batch, full((1, d)),
                  full((d, o_z)), full((d, SSM_WIDTH)), full((d, CONV_CH)), full((d, LANES)),
                  full((1, ATTN_WIDTH)), full((1, ATTN_WIDTH)), full((ATTN_WIDTH, ATTN_WIDTH))],
        out_specs=[row(ATTN_WIDTH), row(ATTN_WIDTH), row(ATTN_WIDTH), row(SSM_WIDTH), row(CONV_CH), row(LANES)],
        out_shape=[jax.ShapeDtypeStruct((t, n), F32)
                   for n in (ATTN_WIDTH, ATTN_WIDTH, ATTN_WIDTH, SSM_WIDTH, CONV_CH, LANES)],
        compiler_params=_params(("arbitrary",)),
        name="in_proj",
    )(x.reshape(t, d), shift.reshape(b, 1, d), scale.reshape(b, 1, d), gain.reshape(1, d),
      w_qkv, w_z, w_xbc, w_dt,
      jnp.tile(q_gain, ATTN_HEADS).reshape(1, ATTN_WIDTH), jnp.tile(k_gain, ATTN_HEADS).reshape(1, ATTN_WIDTH), hmean)
    return outs


def _t5_causal_buckets(distance):
    n = np.maximum(distance, 0)
    max_exact = REL_BUCKETS // 2
    large = max_exact + (np.log(np.maximum(n, 1) / max_exact) / math.log(REL_MAX_DISTANCE / max_exact)
                         * (REL_BUCKETS - max_exact)).astype(np.int64)
    large = np.minimum(large, REL_BUCKETS - 1)
    return np.where(n < max_exact, n, large).astype(np.int32)


def _window_bias(rel_bias_table, dilation):
    qi = np.arange(WIN_STEPS)[:, None]
    kj = np.arange(2 * WIN_STEPS)[None, :]
    dist = qi + WIN_STEPS - kj
    band = (dist >= 0) & (dist <= WIN_STEPS)
    onehot = (_t5_causal_buckets(dist * dilation).reshape(-1, 1) == np.arange(REL_BUCKETS)[None, :]).astype(np.float32)
    bias = jnp.dot(rel_bias_table.astype(F32).T, jnp.asarray(onehot).T, precision=lax.Precision.HIGHEST)
    bias = bias.reshape(ATTN_HEADS, WIN_STEPS, 2 * WIN_STEPS)
    return jnp.where(jnp.asarray(band)[None], bias, NEG_BIG)


ATTN_TOKENS = max(w for w, _ in PATTERNS)
ATTN_UNROLL = 8


def _attn_kernel(q_ref, kp_ref, kc_ref, vp_ref, vc_ref, bias_ref, out_ref, kw, vw, o_acc, l_acc):
    tb = ATTN_TOKENS
    first = pl.program_id(2) == 0
    kw[0:tb] = kp_ref[...]
    kw[tb:2 * tb] = kc_ref[...]
    vw[0:tb] = vp_ref[...]
    vw[tb:2 * tb] = vc_ref[...]
    lane = lax.broadcasted_iota(I32, (WIN_STEPS, LANES), 1)
    head0 = lane < HEAD_DIM
    col = lax.broadcasted_iota(I32, (WIN_STEPS, 2 * WIN_STEPS), 1)
    in_prev = col < WIN_STEPS

    for p, (_, d) in enumerate(PATTERNS):
        shift = d.bit_length() - 1
        n_blocks = tb // WIN_STEPS

        def rows(start, n, d=d):
            return pl.ds(start, n, stride=d) if d > 1 else pl.ds(start, n)

        def body(it, carry, p=p, d=d, shift=shift, rows=rows):
            for u in range(ATTN_UNROLL):
                idx = it * ATTN_UNROLL + u
                r = jnp.bitwise_and(idx, d - 1)
                j = jnp.right_shift(idx, shift)
                qs = j * (WIN_STEPS * d) + r
                q = q_ref[rows(qs, WIN_STEPS), :]
                k = kw[rows(tb + qs - WIN_STEPS * d, 2 * WIN_STEPS), :].astype(BF16)
                v = vw[rows(tb + qs - WIN_STEPS * d, 2 * WIN_STEPS), :].astype(BF16)
                no_prev = jnp.logical_and(in_prev, jnp.logical_and(first, j == 0))
                o_h, lse_h = [], []
                for h in range(2):
                    qh = jnp.where(head0 if h == 0 else jnp.logical_not(head0), q, 0.0).astype(BF16)
                    s = _dot_nt(qh, k) + bias_ref[p, h]
                    s = jnp.where(no_prev, NEG_BIG, s)
                    m = jnp.max(s, axis=-1, keepdims=True)
                    e = jnp.exp(s - m)
                    denom = jnp.sum(e, axis=-1, keepdims=True)
                    o_h.append(_dot(e.astype(BF16), v) / denom)
                    lse_h.append(m + jnp.log(denom))
                o_acc[p, rows(qs, WIN_STEPS), :] = jnp.where(head0, o_h[0], o_h[1])
                l_acc[p, rows(qs, WIN_STEPS), :] = jnp.where(head0, lse_h[0], lse_h[1])
            return carry

        lax.fori_loop(0, n_blocks // ATTN_UNROLL, body, 0)

    chunk = 256
    for c0 in range(0, tb, chunk):
        l1, l2, l3 = (l_acc[p, c0:c0 + chunk, :] for p in range(3))
        m = jnp.maximum(jnp.maximum(l1, l2), l3)
        e1, e2, e3 = jnp.exp(l1 - m), jnp.exp(l2 - m), jnp.exp(l3 - m)
        num = e1 * o_acc[0, c0:c0 + chunk, :] + e2 * o_acc[1, c0:c0 + chunk, :] + e3 * o_acc[2, c0:c0 + chunk, :]
        out_ref[c0:c0 + chunk, :] = num / (e1 + e2 + e3)


def _attention(q, k, v, bias):
    b, s, w = q.shape
    tb = ATTN_TOKENS
    pairs = ATTN_HEADS // 2
    cur = pl.BlockSpec((None, tb, LANES), lambda bi, hp, i: (bi, i, hp))
    prev = pl.BlockSpec((None, tb, LANES), lambda bi, hp, i: (bi, jnp.maximum(i - 1, 0), hp))
    return pl.pallas_call(
        _attn_kernel,
        grid=(b, pairs, s // tb),
        in_specs=[cur, prev, cur, prev, cur,
                  pl.BlockSpec((len(PATTERNS), 2, WIN_STEPS, 2 * WIN_STEPS), lambda bi, hp, i: (0, hp, 0, 0))],
        out_specs=cur,
        out_shape=jax.ShapeDtypeStruct((b, s, w), F32),
        scratch_shapes=[pltpu.VMEM((2 * tb, LANES), F32), pltpu.VMEM((2 * tb, LANES), F32),
                        pltpu.VMEM((len(PATTERNS), tb, LANES), F32), pltpu.VMEM((len(PATTERNS), tb, LANES), F32)],
        compiler_params=_params(("arbitrary",) * 3),
        name="attention",
    )(q, k, k, v, v, bias)


def _ssd_kernel(xbc_ref, halo_ref, z_ref, dtraw_ref, convw_ref, convb_ref, dtb_ref, alog_ref, dskip_ref, gain_ref,
                expand_ref, tril_ref, y_ref, state_ref):
    c = pl.program_id(1)

    @pl.when(c == 0)
    def _():
        state_ref[...] = jnp.zeros_like(state_ref)

    x = xbc_ref[...]
    halo = jnp.where(c == 0, 0.0, halo_ref[...])
    w = convw_ref[...]
    acc = x * w[SSM_CONV - 1:SSM_CONV, :] + convb_ref[...]
    row8 = lax.broadcasted_iota(I32, (SUBLANES, CONV_CH), 0)
    for shift in range(1, SSM_CONV):
        xs = pltpu.roll(x, shift, axis=0)
        hs = pltpu.roll(halo, shift, axis=0)
        head = jnp.where(row8 < shift, hs, xs[0:SUBLANES])
        xs = jnp.concatenate([head, xs[SUBLANES:]], axis=0)
        acc = acc + xs * w[SSM_CONV - 1 - shift:SSM_CONV - shift, :]
    act = _silu(acc)
    x_s = act[:, :SSM_WIDTH]
    bc0 = SSM_WIDTH
    cc0 = SSM_WIDTH + SSM_GROUPS * SSM_STATE

    t = dtraw_ref[...] + dtb_ref[...]
    dt = jnp.maximum(t, 0.0) + jnp.log(1.0 + jnp.exp(-jnp.abs(t)))
    a = dt * (-jnp.exp(alog_ref[...]))
    a_cs = _dot_exact_lhs(tril_ref[...], a)
    a_cs_t = a_cs.T
    a_last = a_cs[SSM_CHUNK - 1:SSM_CHUNK, :]
    expand = expand_ref[...]
    dt_e = _dot_exact_rhs(dt, expand)
    ea_e = _dot_exact_rhs(jnp.exp(a_cs), expand)
    dte_e = _dot_exact_rhs(jnp.exp(a_last - a_cs), expand)
    xdt = x_s * dt_e
    xw = (xdt * dte_e).astype(BF16)
    xdt_b = xdt.astype(BF16)

    li = lax.broadcasted_iota(I32, (SSM_CHUNK, SSM_CHUNK), 0)
    si = lax.broadcasted_iota(I32, (SSM_CHUNK, SSM_CHUNK), 1)
    causal = li >= si

    ys = []
    for g in range(SSM_GROUPS):
        gs = slice(g * GROUP_WIDTH, (g + 1) * GROUP_WIDTH)
        b_g = act[:, bc0 + g * SSM_STATE:bc0 + (g + 1) * SSM_STATE]
        c_g = act[:, cc0 + g * SSM_STATE:cc0 + (g + 1) * SSM_STATE].astype(BF16)
        cb = _dot_nt(c_g, b_g.astype(BF16))
        state = state_ref[g]
        y_off = _dot(c_g, state.astype(BF16)) * ea_e[:, gs]
        parts = []
        for j in range(HEADS_PER_GROUP):
            hh = g * HEADS_PER_GROUP + j
            seg = a_cs[:, hh:hh + 1] - a_cs_t[hh:hh + 1, :]
            decay = jnp.exp(jnp.where(causal, seg, NEG_BIG))
            m = (cb * decay).astype(BF16)
            parts.append(_dot(m, xdt_b[:, hh * SSM_HEAD_DIM:(hh + 1) * SSM_HEAD_DIM]))
        ys.append(jnp.concatenate(parts, axis=1) + y_off)
        state_ref[g] = state * ea_e[SSM_CHUNK - 1:SSM_CHUNK, gs] + _dot(b_g.T.astype(BF16), xw[:, gs])
    y = jnp.concatenate(ys, axis=1) + dskip_ref[...] * x_s
    y = y * _silu(z_ref[...])
    gain = gain_ref[...]
    for g in range(SSM_GROUPS):
        gs = slice(g * GROUP_WIDTH, (g + 1) * GROUP_WIDTH)
        yg = y[:, gs]
        ms = jnp.mean(yg * yg, axis=-1, keepdims=True)
        y_ref[:, gs] = yg * lax.rsqrt(ms + NORM_EPS) * gain[:, gs]


def _ssd(xbc, z, dt_raw, conv_w, conv_b, dt_bias, a_log, d_skip, norm_gain, b, s):
    t = b * s
    nc = s // SSM_CHUNK
    pad_heads = lambda v: jnp.zeros((1, LANES), F32).at[0, :SSM_HEADS].set(v)
    head_of_lane = np.arange(SSM_WIDTH) // SSM_HEAD_DIM
    expand = jnp.asarray((np.arange(LANES)[:, None] == head_of_lane[None, :]).astype(np.float32), BF16)
    tril = jnp.asarray(np.tril(np.ones((SSM_CHUNK, SSM_CHUNK), np.float32)), BF16)
    halo_blocks = SSM_CHUNK // SUBLANES
    chunk = lambda n: pl.BlockSpec((SSM_CHUNK, n), lambda bi, c: (bi * nc + c, 0))
    full = lambda shp: pl.BlockSpec(shp, lambda bi, c: (0,) * len(shp))
    halo = pl.BlockSpec((SUBLANES, CONV_CH), lambda bi, c: (jnp.maximum((bi * nc + c) * halo_blocks - 1, 0), 0))
    return pl.pallas_call(
        _ssd_kernel,
        grid=(b, nc),
        in_specs=[chunk(CONV_CH), halo, chunk(SSM_WIDTH), chunk(LANES),
                  full((SSM_CONV, CONV_CH)), full((1, CONV_CH)), full((1, LANES)), full((1, LANES)),
                  full((1, SSM_WIDTH)), full((1, SSM_WIDTH)), full((LANES, SSM_WIDTH)), full((SSM_CHUNK, SSM_CHUNK))],
        out_specs=chunk(SSM_WIDTH),
        out_shape=jax.ShapeDtypeStruct((t, SSM_WIDTH), F32),
        scratch_shapes=[pltpu.VMEM((SSM_GROUPS, SSM_STATE, GROUP_WIDTH), F32)],
        compiler_params=_params(("arbitrary", "arbitrary")),
        name="ssd",
    )(xbc, xbc, z, dt_raw, conv_w, conv_b.reshape(1, CONV_CH), pad_heads(dt_bias), pad_heads(a_log),
      jnp.repeat(d_skip, SSM_HEAD_DIM).reshape(1, SSM_WIDTH), norm_gain.reshape(1, SSM_WIDTH), expand, tril)


WORD = jnp.int32
TOKEN_ROWS = D_MODEL // (2 * LANES)
HIGH_HALF = np.int32(-65536)


def _to_token_tiles(ref, x):
    n = x.shape[0]
    for c in range(TOKEN_ROWS):
        lo = lax.bitcast_convert_type(x[:, c * LANES:(c + 1) * LANES].astype(BF16).astype(F32), WORD)
        hi = lax.bitcast_convert_type(x[:, (c + TOKEN_ROWS) * LANES:(c + TOKEN_ROWS + 1) * LANES]
                                      .astype(BF16).astype(F32), WORD)
        ref[pl.ds(c, n, stride=TOKEN_ROWS), :] = jnp.bitwise_or(lax.shift_right_logical(lo, 16),
                                                                 jnp.bitwise_and(hi, HIGH_HALF))


def _from_token_tiles(ref, n, token0=0):
    lows, highs = [], []
    for c in range(TOKEN_ROWS):
        word = ref[pl.ds(token0 * TOKEN_ROWS + c, n, stride=TOKEN_ROWS), :]
        lows.append(lax.bitcast_convert_type(jnp.left_shift(word, 16), F32))
        highs.append(lax.bitcast_convert_type(jnp.bitwise_and(word, HIGH_HALF), F32))
    return jnp.concatenate(lows + highs, axis=1)


def _outproj_kernel(attn_ref, ssm_ref, x_ref, gate_ref, shift_ref, scale_ref, g_ref, wa_ref, ws_ref, x1_ref, h2_ref,
                    h2t_ref):
    mixed = _dot(attn_ref[...].astype(BF16), wa_ref[...]) + _dot(ssm_ref[...].astype(BF16), ws_ref[...])
    x1 = x_ref[...] + gate_ref[...] * mixed
    x1_ref[...] = x1
    ms = jnp.mean(x1 * x1, axis=-1, keepdims=True)
    h = x1 * lax.rsqrt(ms + NORM_EPS) * g_ref[...]
    h2 = h * (1.0 + scale_ref[...]) + shift_ref[...]
    h2_ref[...] = h2
    for c in range(TOKEN_ROWS):
        lo = lax.bitcast_convert_type(h2[:, c * LANES:(c + 1) * LANES].astype(BF16).astype(F32), WORD)
        hi = lax.bitcast_convert_type(h2[:, (c + TOKEN_ROWS) * LANES:(c + TOKEN_ROWS + 1) * LANES]
                                      .astype(BF16).astype(F32), WORD)
        h2t_ref[c] = jnp.bitwise_or(lax.shift_right_logical(lo, 16), jnp.bitwise_and(hi, HIGH_HALF))


def _out_proj(attn, ssm, x, gate, shift, scale, gain, w_out, b, s, tm=256):
    t = b * s
    d = D_MODEL
    tiles_per_seq = s // tm
    w = w_out.astype(BF16)
    row = lambda n: pl.BlockSpec((tm, n), lambda i: (i, 0))
    full = lambda shp: pl.BlockSpec(shp, lambda i: (0,) * len(shp))
    per_batch = pl.BlockSpec((None, 1, d), lambda i: (i // tiles_per_seq, 0, 0))
    return pl.pallas_call(
        _outproj_kernel,
        grid=(t // tm,),
        in_specs=[row(ATTN_WIDTH), row(SSM_WIDTH), row(d), per_batch, per_batch, per_batch, full((1, d)),
                  full((ATTN_WIDTH, d)), full((SSM_WIDTH, d))],
        out_specs=[row(d), row(d), pl.BlockSpec((TOKEN_ROWS, tm, LANES), lambda i: (0, i, 0))],
        out_shape=[jax.ShapeDtypeStruct((t, d), F32)] * 2 + [jax.ShapeDtypeStruct((TOKEN_ROWS, t, LANES), WORD)],
        compiler_params=_params(("arbitrary",)),
        name="out_proj",
    )(attn.reshape(t, ATTN_WIDTH), ssm, x.reshape(t, d),
      gate.reshape(b, 1, d), shift.reshape(b, 1, d), scale.reshape(b, 1, d), gain.reshape(1, d),
      w[:ATTN_WIDTH], w[ATTN_WIDTH:])


def _mixer_sublayer(x, mod, norm_mix_gain, w_in, q_norm_gain, k_norm_gain, rel_bias_table, conv_w, conv_b, dt_bias,
                    a_log, d_skip, ssm_norm_gain, w_out, norm_ffn_gain):
    b, s, d = x.shape
    shift_m, scale_m, gate_m, shift_f, scale_f, _ = jnp.split(mod, 6, axis=-1)
    q, k, v, z, xbc, dt_raw = _in_proj(x, shift_m, scale_m, norm_mix_gain, w_in, q_norm_gain, k_norm_gain)
    bias = jnp.stack([_window_bias(rel_bias_table, dilation) for _, dilation in PATTERNS])
    attn = _attention(q.reshape(b, s, ATTN_WIDTH), k.reshape(b, s, ATTN_WIDTH), v.reshape(b, s, ATTN_WIDTH), bias)
    ssm = _ssd(xbc, z, dt_raw, conv_w, conv_b, dt_bias, a_log, d_skip, ssm_norm_gain, b, s)
    return _out_proj(attn, ssm, x, gate_m, shift_f, scale_f, norm_ffn_gain, w_out, b, s)


def _first_argmax(v, iota, limit):
    m = jnp.max(v, axis=0, keepdims=True)
    idx = jnp.min(jnp.where(v == m, iota, limit), axis=0, keepdims=True)
    return m, idx


def _router_kernel(h_ref, wt_ref, bias_ref, upper_ref, eidx_ref, rank_ref, gate_ref, counts_ref, carry_ref):
    @pl.when(pl.program_id(0) == 0)
    def _():
        carry_ref[...] = jnp.zeros_like(carry_ref)

    tm = h_ref.shape[0]
    h = h_ref[...]
    wt = wt_ref[...]
    h_hi = h.astype(BF16)
    h_lo = (h - h_hi.astype(F32)).astype(BF16)
    w_hi = wt.astype(BF16)
    w_lo = (wt - w_hi.astype(F32)).astype(BF16)
    logits = _dot_nt(w_hi, h_hi) + _dot_nt(w_hi, h_lo) + _dot_nt(w_lo, h_hi)
    scores = _sigmoid(logits)
    choice = scores + bias_ref[...]
    neg_inf = -jnp.inf

    iota_g = lax.broadcasted_iota(I32, (EXPERTS_PER_GROUP, tm), 0).astype(F32)
    group_rows = []
    for g in range(N_EXPERT_GROUPS):
        v = choice[g * EXPERTS_PER_GROUP:(g + 1) * EXPERTS_PER_GROUP]
        m1, i1 = _first_argmax(v, iota_g, float(EXPERTS_PER_GROUP))
        m2 = jnp.max(jnp.where(iota_g == i1, neg_inf, v), axis=0, keepdims=True)
        group_rows.append(m1 + m2)
    group_scores = jnp.concatenate(group_rows, axis=0)

    iota_n = lax.broadcasted_iota(I32, (N_EXPERT_GROUPS, tm), 0).astype(F32)
    chosen = jnp.zeros((N_EXPERT_GROUPS, tm), F32)
    for _ in range(TOPK_GROUPS):
        _, gi = _first_argmax(group_scores, iota_n, float(N_EXPERT_GROUPS))
        hit = iota_n == gi
        chosen = jnp.where(hit, 1.0, chosen)
        group_scores = jnp.where(hit, neg_inf, group_scores)

    masked = jnp.concatenate(
        [jnp.where(chosen[g:g + 1] > 0.0, choice[g * EXPERTS_PER_GROUP:(g + 1) * EXPERTS_PER_GROUP], neg_inf)
         for g in range(N_EXPERT_GROUPS)], axis=0)

    iota_e = lax.broadcasted_iota(I32, (N_EXPERTS, tm), 0).astype(F32)
    picked, gates = [], []
    onehot = jnp.zeros((N_EXPERTS, tm), F32)
    for _ in range(TOP_K):
        _, ei = _first_argmax(masked, iota_e, float(N_EXPERTS))
        hit = iota_e == ei
        gates.append(jnp.sum(jnp.where(hit, scores, 0.0), axis=0, keepdims=True))
        masked = jnp.where(hit, neg_inf, masked)
        onehot = jnp.where(hit, 1.0, onehot)
        picked.append(ei)
    gate_sum = gates[0]
    for gk in gates[1:]:
        gate_sum = gate_sum + gk

    base = _dot(onehot.astype(BF16), upper_ref[...]) + carry_ref[...]
    ranks = [jnp.sum(jnp.where(iota_e == ei, base, 0.0), axis=0, keepdims=True) for ei in picked]
    carry_ref[...] = carry_ref[...] + jnp.sum(onehot, axis=1, keepdims=True)

    eidx_ref[...] = jnp.concatenate(picked, axis=0).astype(I32)
    rank_ref[...] = jnp.concatenate(ranks, axis=0).astype(I32)
    gate_ref[...] = jnp.concatenate([gk / gate_sum * ROUTED_SCALE for gk in gates], axis=0)
    counts_ref[...] = carry_ref[...].astype(I32)


def _router(h2, w_router, router_bias, tm=256):
    t, d = h2.shape
    upper = jnp.asarray(np.triu(np.ones((tm, tm), np.float32), 1), BF16)
    tok = pl.BlockSpec((TOP_K, tm), lambda i: (0, i))
    full = lambda shp: pl.BlockSpec(shp, lambda i: (0,) * len(shp))
    return pl.pallas_call(
        _router_kernel,
        grid=(t // tm,),
        in_specs=[pl.BlockSpec((tm, d), lambda i: (i, 0)), full((N_EXPERTS, d)), full((N_EXPERTS, 1)), full((tm, tm))],
        out_specs=[tok, tok, tok, full((N_EXPERTS, 1))],
        out_shape=[jax.ShapeDtypeStruct((TOP_K, t), I32), jax.ShapeDtypeStruct((TOP_K, t), I32),
                   jax.ShapeDtypeStruct((TOP_K, t), F32), jax.ShapeDtypeStruct((N_EXPERTS, 1), I32)],
        scratch_shapes=[pltpu.VMEM((N_EXPERTS, 1), F32)],
        compiler_params=_params(("arbitrary",)),
        name="router",
    )(h2, w_router.T, router_bias.reshape(N_EXPERTS, 1), upper)


def _positions_kernel(counts_ref, lower_ref, eidx_ref, rank_ref, pos_ref):
    tm = eidx_ref.shape[1]
    counts = jnp.broadcast_to(counts_ref[...].astype(F32), (N_EXPERTS, LANES))
    offsets = _dot_exact_lhs(lower_ref[...], counts)[:, 0:1]
    iota_e = lax.broadcasted_iota(I32, (N_EXPERTS, tm), 0).astype(F32)
    e = eidx_ref[...].astype(F32)
    rows = [jnp.sum(jnp.where(iota_e == e[k:k + 1], offsets, 0.0), axis=0, keepdims=True) for k in range(TOP_K)]
    pos_ref[...] = jnp.concatenate(rows, axis=0).astype(I32) + rank_ref[...]


def _positions(counts, eidx, rank, tm):
    t = eidx.shape[1]
    lower = jnp.asarray(np.tril(np.ones((N_EXPERTS, N_EXPERTS), np.float32), -1), BF16)
    tok = pl.BlockSpec((TOP_K, tm), lambda i: (0, i))
    return pl.pallas_call(
        _positions_kernel,
        grid=(t // tm,),
        in_specs=[pl.BlockSpec((N_EXPERTS, 1), lambda i: (0, 0)), pl.BlockSpec((N_EXPERTS, N_EXPERTS), lambda i: (0, 0)),
                  tok, tok],
        out_specs=tok,
        out_shape=jax.ShapeDtypeStruct((TOP_K, t), I32),
        compiler_params=_params(("arbitrary",)),
        name="positions",
    )(counts, lower, eidx, rank)


EXPERT_BLOCK = 512
TAIL_UNIT = 128
TAIL_PIECES = tuple(1 << i for i in reversed(range(EXPERT_BLOCK.bit_length() - 1)))
X_SLOTS = 4
Y_SLOTS = 4


def _experts_kernel(start_ref, count_ref, nxt_ref, slot_ref, first_ref, blk0_ref, full0_ref, ptail_ref, ltail_ref,
                    blktok_ref, nblocks_ref, xs_hbm, wg_hbm, wu_hbm, wd_hbm, ys_hbm,
                    wg_buf, wu_buf, wd_buf, wg_bf, wu_bf, wd_bf, xbuf, ybuf, ytail, wsem, xsem, ysem, tsem):
    e = pl.program_id(0)
    last_step = e == pl.num_programs(0) - 1
    start, count = start_ref[e], count_ref[e]
    n_full = jnp.right_shift(count, EXPERT_BLOCK.bit_length() - 1)
    tail = jnp.bitwise_and(count, EXPERT_BLOCK - 1)
    n_blk = n_full + (tail > 0).astype(I32)
    blk0, full0 = blk0_ref[e], full0_ref[e]
    slot, nxt = slot_ref[e], nxt_ref[e]

    def token_rows(token, n):
        return pl.ds(pl.multiple_of(token * TOKEN_ROWS, TOKEN_ROWS), n * TOKEN_ROWS)

    def fetch(ex, s):
        return (pltpu.make_async_copy(wg_hbm.at[ex], wg_buf.at[s], wsem.at[s, 0]),
                pltpu.make_async_copy(wu_hbm.at[ex], wu_buf.at[s], wsem.at[s, 1]),
                pltpu.make_async_copy(wd_hbm.at[ex], wd_buf.at[s], wsem.at[s, 2]))

    def x_copy(token, s):
        return pltpu.make_async_copy(xs_hbm.at[token_rows(token, EXPERT_BLOCK)], xbuf.at[s], xsem.at[s])

    def y_copy(token, s):
        return pltpu.make_async_copy(ybuf.at[s], ys_hbm.at[token_rows(token, EXPERT_BLOCK)], ysem.at[s])

    def tail_copies(token, length):
        out = []
        for piece in TAIL_PIECES:
            bigger = (EXPERT_BLOCK - 1) & ~(2 * piece - 1)
            done = jnp.bitwise_and(length, bigger)
            cp = pltpu.make_async_copy(ytail.at[token_rows(done, piece)], ys_hbm.at[token_rows(token + done, piece)],
                                       tsem)
            out.append((jnp.bitwise_and(length, piece) != 0, cp))
        return out

    def block(s, rows=EXPERT_BLOCK):
        x = _from_token_tiles(xbuf.at[s], rows).astype(BF16)
        g = _dot(x, wg_bf[...])
        u = _dot(x, wu_bf[...])
        return _dot((_silu(g) * u).astype(BF16), wd_bf[...])

    @pl.when(count > 0)
    def _():
        @pl.when(first_ref[e] == 1)
        def _():
            for g in range(X_SLOTS - 1):
                @pl.when(g < nblocks_ref[0])
                def _(g=g):
                    x_copy(blktok_ref[g], g).start()
            for cp in fetch(e, slot):
                cp.start()

        for cp in fetch(e, slot):
            cp.wait()

        @pl.when(nxt >= 0)
        def _():
            for cp in fetch(nxt, 1 - slot):
                cp.start()

        wg_bf[...] = wg_buf[slot].astype(BF16)
        wu_bf[...] = wu_buf[slot].astype(BF16)
        wd_bf[...] = wd_buf[slot].astype(BF16)

        def take_x(i):
            g = blk0 + i
            xs_slot = jnp.bitwise_and(g, X_SLOTS - 1)
            x_copy(start, xs_slot).wait()
            ahead = g + (X_SLOTS - 1)

            @pl.when(ahead < nblocks_ref[0])
            def _():
                x_copy(blktok_ref[ahead], jnp.bitwise_and(ahead, X_SLOTS - 1)).start()

            return xs_slot

        def full_block(i, carry):
            y = block(take_x(i))
            j = full0 + i
            ys_slot = jnp.bitwise_and(j, Y_SLOTS - 1)

            @pl.when(j >= Y_SLOTS)
            def _():
                y_copy(start, ys_slot).wait()

            _to_token_tiles(ybuf.at[ys_slot], y)
            y_copy(start + i * EXPERT_BLOCK, ys_slot).start()
            return carry

        lax.fori_loop(0, n_full, full_block, 0)

        @pl.when(tail > 0)
        def _():
            xs_slot = take_x(n_full)
            for pred, cp in tail_copies(start, ptail_ref[e]):
                @pl.when(pred)
                def _(cp=cp):
                    cp.wait()
            units = jnp.right_shift(tail + (TAIL_UNIT - 1), TAIL_UNIT.bit_length() - 1)
            for u in range(1, EXPERT_BLOCK // TAIL_UNIT + 1):
                @pl.when(units == u)
                def _(u=u):
                    rows = u * TAIL_UNIT
                    _to_token_tiles(ytail.at[pl.ds(0, rows * TOKEN_ROWS)], block(xs_slot, rows))
            for pred, cp in tail_copies(start + n_full * EXPERT_BLOCK, tail):
                @pl.when(pred)
                def _(cp=cp):
                    cp.start()

    @pl.when(last_step)
    def _():
        total_full = full0 + n_full
        for back in range(1, Y_SLOTS + 1):
            @pl.when(total_full >= back)
            def _(back=back):
                y_copy(0, jnp.bitwise_and(total_full - back, Y_SLOTS - 1)).wait()
        for pred, cp in tail_copies(0, ltail_ref[0]):
            @pl.when(pred)
            def _(cp=cp):
                cp.wait()


def _max_expert_blocks(n_rows):
    return n_rows // EXPERT_BLOCK + N_EXPERTS


def _expert_metadata(counts, n_rows):
    ids = jnp.arange(N_EXPERTS, dtype=I32)
    used = counts > 0
    starts = jnp.cumsum(counts) - counts
    n_blk = (counts + EXPERT_BLOCK - 1) // EXPERT_BLOCK
    n_full = counts // EXPERT_BLOCK
    tail = counts % EXPERT_BLOCK
    blk0 = jnp.cumsum(n_blk) - n_blk
    full0 = jnp.cumsum(n_full) - n_full
    next_used = lax.cummin(jnp.where(used, ids, N_EXPERTS), reverse=True)
    next_after = jnp.concatenate([next_used[1:], jnp.full((1,), N_EXPERTS, I32)])
    nxt = jnp.where(next_after < N_EXPERTS, next_after, -1)
    ordinal = jnp.cumsum(used.astype(I32)) - 1
    slot = ordinal % 2
    first = jnp.logical_and(used, ordinal == 0)
    latest = lax.cummax(jnp.where(tail > 0, ids, -1))
    before = jnp.concatenate([jnp.full((1,), -1, I32), latest[:-1]])
    pick = lambda index, values: jnp.sum(jnp.where(index[:, None] == ids[None, :], values[None, :], 0), axis=1)
    ptail = pick(before, tail)
    ltail = pick(latest[-1:], tail)
    block_ends = jnp.cumsum(n_blk)
    g = jnp.arange(_max_expert_blocks(n_rows), dtype=I32)
    eg = jnp.sum((g[:, None] >= block_ends[None, :]).astype(I32), axis=1)
    blktok = g * EXPERT_BLOCK + pick(eg, starts - blk0 * EXPERT_BLOCK)
    return tuple(v.astype(I32) for v in (starts, counts, nxt, slot, first, blk0, full0, ptail, ltail, blktok,
                                         block_ends[-1:]))


def _experts(xs, counts, w_gate, w_up, w_down):
    d = D_MODEL
    meta = _expert_metadata(counts, xs.shape[0] // TOKEN_ROWS - EXPERT_BLOCK)
    hbm = pl.BlockSpec(memory_space=pl.ANY)
    blk = (EXPERT_BLOCK * TOKEN_ROWS, LANES)
    grid_spec = pltpu.PrefetchScalarGridSpec(
        num_scalar_prefetch=len(meta),
        grid=(N_EXPERTS,),
        in_specs=[hbm, hbm, hbm, hbm],
        out_specs=hbm,
        scratch_shapes=[pltpu.VMEM((2, d, EXPERT_FF), F32), pltpu.VMEM((2, d, EXPERT_FF), F32),
                        pltpu.VMEM((2, EXPERT_FF, d), F32),
                        pltpu.VMEM((d, EXPERT_FF), BF16), pltpu.VMEM((d, EXPERT_FF), BF16),
                        pltpu.VMEM((EXPERT_FF, d), BF16),
                        pltpu.VMEM((X_SLOTS,) + blk, WORD), pltpu.VMEM((Y_SLOTS,) + blk, WORD), pltpu.VMEM(blk, WORD),
                        pltpu.SemaphoreType.DMA((2, 3)), pltpu.SemaphoreType.DMA((X_SLOTS,)),
                        pltpu.SemaphoreType.DMA((Y_SLOTS,)), pltpu.SemaphoreType.DMA(())],
    )
    return pl.pallas_call(
        _experts_kernel,
        grid_spec=grid_spec,
        out_shape=jax.ShapeDtypeStruct(xs.shape, WORD),
        compiler_params=_params(("arbitrary",)),
        name="experts",
    )(*meta, xs, w_gate, w_up, w_down)


SC_CORES = 2
SC_SUBCORES = 16
SC_WORKERS = SC_CORES * SC_SUBCORES
SC_CHUNK = 128
SC_RING = 4


def _sc_scatter_rows(planes, idx, n_out_rows):
    n_planes, t, _ = planes.shape
    ranges = SC_WORKERS // n_planes
    n_chunks = t // ranges // SC_CHUNK
    idx = idx.reshape(TOP_K, n_planes, ranges, n_chunks, SC_CHUNK)
    mesh = plsc.VectorSubcoreMesh(core_axis_name="c", subcore_axis_name="s", num_cores=SC_CORES,
                                  num_subcores=SC_SUBCORES)

    def body(planes_hbm, idx_hbm, out_hbm, idx_v, rows_v, lsem, ssem):
        wid = lax.axis_index("s") * SC_CORES + lax.axis_index("c")
        plane = wid % n_planes
        token0 = (wid // n_planes) * (n_chunks * SC_CHUNK)
        for k in range(TOP_K):
            pltpu.sync_copy(idx_hbm.at[k, plane, wid // n_planes], idx_v.at[k])

        def load(slot, c):
            return pltpu.make_async_copy(planes_hbm.at[plane, pl.ds(token0 + c * SC_CHUNK, SC_CHUNK)], rows_v.at[slot],
                                         lsem.at[slot])

        def scatter(slot, c, k):
            return pltpu.make_async_copy(rows_v.at[slot], out_hbm.at[idx_v.at[k, c]], ssem.at[slot])

        @pl.loop(0, n_chunks, step=SC_RING)
        def _(g):
            for slot in range(SC_RING):
                @pl.when(g > 0)
                def _(slot=slot):
                    for k in range(TOP_K):
                        scatter(slot, 0, k).wait()
                load(slot, g + slot).start()
            for slot in range(SC_RING):
                load(slot, g + slot).wait()
                for k in range(TOP_K):
                    scatter(slot, g + slot, k).start()

        for slot in range(SC_RING):
            for k in range(TOP_K):
                scatter(slot, 0, k).wait()

    return pl.kernel(
        body, mesh=mesh,
        out_type=jax.ShapeDtypeStruct((n_out_rows, LANES), planes.dtype),
        scratch_types=[pltpu.VMEM((TOP_K, n_chunks, SC_CHUNK), I32), pltpu.VMEM((SC_RING, SC_CHUNK, LANES), planes.dtype),
                       pltpu.SemaphoreType.DMA((SC_RING,)), pltpu.SemaphoreType.DMA((SC_RING,))],
        name="sc_scatter",
    )(planes, idx)


def _sc_gather_rows(table, idx):
    n_workers, n_chunks, chunk = idx.shape
    rows_per_worker = n_chunks * chunk
    mesh = plsc.VectorSubcoreMesh(core_axis_name="c", subcore_axis_name="s", num_cores=SC_CORES,
                                  num_subcores=SC_SUBCORES)

    def body(table_hbm, idx_hbm, out_hbm, idx_v, rows_v, gsem, wsem):
        wid = lax.axis_index("s") * SC_CORES + lax.axis_index("c")
        base = wid * rows_per_worker
        pltpu.sync_copy(idx_hbm.at[wid], idx_v)

        def write(slot, c):
            return pltpu.make_async_copy(rows_v.at[slot], out_hbm.at[pl.ds(base + c * chunk, chunk)], wsem.at[slot])

        def gather(slot, c):
            return pltpu.make_async_copy(table_hbm.at[idx_v.at[c]], rows_v.at[slot], gsem.at[slot])

        @pl.loop(0, n_chunks, step=SC_RING)
        def _(g):
            for slot in range(SC_RING):
                @pl.when(g > 0)
                def _(slot=slot):
                    write(slot, 0).wait()
                gather(slot, g + slot).start()
            for slot in range(SC_RING):
                gather(slot, g + slot).wait()
                write(slot, g + slot).start()

        for slot in range(SC_RING):
            write(slot, 0).wait()

    return pl.kernel(
        body, mesh=mesh,
        out_type=jax.ShapeDtypeStruct((n_workers * rows_per_worker, LANES), table.dtype),
        scratch_types=[pltpu.VMEM((n_chunks, chunk), I32), pltpu.VMEM((SC_RING, chunk, LANES), table.dtype),
                       pltpu.SemaphoreType.DMA((SC_RING,)), pltpu.SemaphoreType.DMA((SC_RING,))],
        name="sc_gather",
    )(table, idx)


def _combine_kernel(rows_ref, gates_ref, h_ref, x1_ref, gatef_ref, wg_ref, wu_ref, wd_ref, out_ref):
    hb = h_ref[...].astype(BF16)
    shared = _dot((_silu(_dot(hb, wg_ref[...])) * _dot(hb, wu_ref[...])).astype(BF16), wd_ref[...])
    gates = gates_ref[...]

    def expert_rows(k):
        words = [rows_ref[k * TOKEN_ROWS + c] for c in range(TOKEN_ROWS)]
        lows = [lax.bitcast_convert_type(jnp.left_shift(w, 16), F32) for w in words]
        highs = [lax.bitcast_convert_type(jnp.bitwise_and(w, HIGH_HALF), F32) for w in words]
        return jnp.concatenate(lows + highs, axis=1)

    routed = expert_rows(0) * gates[:, 0:1]
    for k in range(1, TOP_K):
        routed = routed + expert_rows(k) * gates[:, k:k + 1]
    out_ref[...] = x1_ref[...] + gatef_ref[...] * (shared + routed)


def _combine(gathered, gates_t, h2, x1, gate_f, w_gate_s, w_up_s, w_down_s, b, s, tm):
    t, d = h2.shape
    tiles_per_seq = s // tm
    row = lambda n: pl.BlockSpec((tm, n), lambda i: (i, 0))
    full = lambda shp: pl.BlockSpec(shp, lambda i: (0,) * len(shp))
    return pl.pallas_call(
        _combine_kernel,
        grid=(t // tm,),
        in_specs=[pl.BlockSpec((TOP_K * TOKEN_ROWS, tm, LANES), lambda i: (0, i, 0)),
                  row(TOP_K), row(d), row(d),
                  pl.BlockSpec((None, 1, d), lambda i: (i // tiles_per_seq, 0, 0)),
                  full((d, EXPERT_FF)), full((d, EXPERT_FF)), full((EXPERT_FF, d))],
        out_specs=row(d),
        out_shape=jax.ShapeDtypeStruct((t, d), F32),
        compiler_params=_params(("arbitrary",)),
        name="combine",
    )(gathered, gates_t, h2, x1, gate_f.reshape(b, 1, d),
      w_gate_s.astype(BF16), w_up_s.astype(BF16), w_down_s.astype(BF16))


def _moe_sublayer(x1, h2, h2t, gate_f, w_router, router_bias, w_gate, w_up, w_down, w_gate_s, w_up_s, w_down_s, b, s,
                  tm=256):
    t = b * s
    eidx, rank, gates, counts = _router(h2, w_router, router_bias)
    pos = _positions(counts, eidx, rank, tm)
    idx = pos[:, None, :] * TOKEN_ROWS + jnp.arange(TOKEN_ROWS, dtype=I32)[None, :, None]
    xs = _sc_scatter_rows(h2t, idx, (t * TOP_K + EXPERT_BLOCK) * TOKEN_ROWS)
    ys = _experts(xs, counts[:, 0], w_gate, w_up, w_down)
    gathered = _sc_gather_rows(ys, idx.reshape(SC_WORKERS, -1, SC_CHUNK)).reshape(TOP_K * TOKEN_ROWS, t, LANES)
    return _combine(gathered, gates.T, h2, x1, gate_f, w_gate_s, w_up_s, w_down_s, b, s, tm)


def kernel(x, c, w_ada, b_ada, norm_mix_gain, w_in, q_norm_gain, k_norm_gain, rel_bias_table, conv_w, conv_b, dt_bias,
           a_log, d_skip, ssm_norm_gain, w_out, norm_ffn_gain, w_router, router_bias, w_gate_experts, w_up_experts,
           w_down_experts, w_gate_shared, w_up_shared, w_down_shared):
    b, s, d = x.shape
    for layer in range(w_ada.shape[0]):
        mod = _adaln(c, w_ada[layer], b_ada[layer])
        x1, h2, h2t = _mixer_sublayer(x, mod, norm_mix_gain[layer], w_in[layer], q_norm_gain[layer], k_norm_gain[layer],
                                 rel_bias_table, conv_w[layer], conv_b[layer], dt_bias[layer], a_log[layer],
                                 d_skip[layer], ssm_norm_gain[layer], w_out[layer], norm_ffn_gain[layer])
        gate_f = mod[:, 5 * d:]
        out = _moe_sublayer(x1, h2, h2t, gate_f, w_router[layer], router_bias[layer], w_gate_experts[layer],
                            w_up_experts[layer], w_down_experts[layer], w_gate_shared[layer], w_up_shared[layer],
                            w_down_shared[layer], b, s)
        x = out.reshape(b, s, d)
    return x
```

```python
import functools
import math

import numpy as np
import jax
import jax.numpy as jnp
from jax import lax
from jax.experimental import pallas as pl
from jax.experimental.pallas import tpu as pltpu
from jax.experimental.pallas import tpu_sc as plsc

F32 = jnp.float32
BF16 = jnp.bfloat16
I32 = jnp.int32

D_MODEL = 1024
ATTN_HEADS = 8
HEAD_DIM = 64
ATTN_WIDTH = ATTN_HEADS * HEAD_DIM
PATTERNS = ((128, 1), (512, 4), (2048, 16))
WIN_STEPS = 128
REL_BUCKETS = 32
REL_MAX_DISTANCE = 2048
SSM_HEADS = 24
SSM_HEAD_DIM = 64
SSM_WIDTH = SSM_HEADS * SSM_HEAD_DIM
SSM_GROUPS = 4
HEADS_PER_GROUP = SSM_HEADS // SSM_GROUPS
GROUP_WIDTH = SSM_WIDTH // SSM_GROUPS
SSM_STATE = 128
SSM_CONV = 4
SSM_CHUNK = 128
CONV_CH = SSM_WIDTH + 2 * SSM_GROUPS * SSM_STATE
N_EXPERTS = 256
TOP_K = 8
N_EXPERT_GROUPS = 8
EXPERTS_PER_GROUP = N_EXPERTS // N_EXPERT_GROUPS
TOPK_GROUPS = 4
EXPERT_FF = 256
ROUTED_SCALE = 2.5
NORM_EPS = 1e-6

LANES = 128
SUBLANES = 8
NEG_BIG = -1e30
VMEM_LIMIT = 56 * 1024 * 1024


def _params(sem, vmem=VMEM_LIMIT):
    return pltpu.CompilerParams(dimension_semantics=sem, vmem_limit_bytes=vmem)


def _sigmoid(x):
    return 1.0 / (1.0 + jnp.exp(-x))


def _silu(x):
    return x * _sigmoid(x)


def _split3(x):
    hi = x.astype(BF16)
    r = x - hi.astype(F32)
    mid = r.astype(BF16)
    lo = (r - mid.astype(F32)).astype(BF16)
    return hi, mid, lo


def _dot(a, b):
    return jnp.dot(a, b, preferred_element_type=F32)


def _dot_nt(a, b):
    return lax.dot_general(a, b, (((1,), (1,)), ((), ())), preferred_element_type=F32)


def _dot_exact_rhs(a, b_exact):
    hi, mid, lo = _split3(a)
    return _dot(hi, b_exact) + _dot(mid, b_exact) + _dot(lo, b_exact)


def _dot_exact_lhs(a_exact, b):
    hi, mid, lo = _split3(b)
    return _dot(a_exact, hi) + _dot(a_exact, mid) + _dot(a_exact, lo)


def _adaln_kernel(c_ref, w_ref, b_ref, o_ref):
    s = _silu(c_ref[...]).astype(BF16)
    o_ref[...] = _dot(s, w_ref[...].astype(BF16)) + b_ref[...]


def _adaln(c, w_ada, b_ada):
    b, d = c.shape
    n = w_ada.shape[1]
    rows = SUBLANES
    c_pad = jnp.zeros((rows, d), F32).at[:b].set(c)
    tn = 1024
    out = pl.pallas_call(
        _adaln_kernel,
        grid=(n // tn,),
        in_specs=[pl.BlockSpec((rows, d), lambda j: (0, 0)),
                  pl.BlockSpec((d, tn), lambda j: (0, j)),
                  pl.BlockSpec((1, tn), lambda j: (0, j))],
        out_specs=pl.BlockSpec((rows, tn), lambda j: (0, j)),
        out_shape=jax.ShapeDtypeStruct((rows, n), F32),
        compiler_params=_params(("arbitrary",)),
        name="adaln",
    )(c_pad, w_ada, b_ada.reshape(1, n))
    return out[:b]


def _inproj_kernel(x_ref, shift_ref, scale_ref, g_ref, wqkv_ref, wz_ref, wxbc_ref, wdt_ref,
                   qg_ref, kg_ref, hmean_ref, q_ref, k_ref, v_ref, z_ref, xbc_ref, dt_ref):
    x = x_ref[...]
    ms = jnp.mean(x * x, axis=-1, keepdims=True)
    h = x * lax.rsqrt(ms + NORM_EPS) * g_ref[...]
    h = h * (1.0 + scale_ref[...]) + shift_ref[...]
    hb = h.astype(BF16)

    hmean = hmean_ref[...]

    def head_norm(t, gain):
        ss = _dot_exact_rhs(t * t, hmean)
        return t * lax.rsqrt(ss + NORM_EPS) * gain

    q = _dot(hb, wqkv_ref[:, 0:ATTN_WIDTH])
    q_ref[...] = head_norm(q, qg_ref[...]) * (HEAD_DIM ** -0.5)
    k = _dot(hb, wqkv_ref[:, ATTN_WIDTH:2 * ATTN_WIDTH])
    k_ref[...] = head_norm(k, kg_ref[...])
    v_ref[...] = _dot(hb, wqkv_ref[:, 2 * ATTN_WIDTH:3 * ATTN_WIDTH])
    for c0 in range(0, SSM_WIDTH, 512):
        z_ref[:, c0:c0 + 512] = _dot(hb, wz_ref[:, c0:c0 + 512])
    for c0 in range(0, CONV_CH, 512):
        xbc_ref[:, c0:c0 + 512] = _dot(hb, wxbc_ref[:, c0:c0 + 512])
    dt_ref[...] = _dot(hb, wdt_ref[...])


def _in_proj(x, shift, scale, gain, w_in, q_gain, k_gain, tm=256):
    b, s, d = x.shape
    t = b * s
    tiles_per_seq = s // tm
    w = w_in.astype(BF16)
    o_z = 3 * ATTN_WIDTH
    o_x = o_z + SSM_WIDTH
    o_dt = o_x + CONV_CH
    w_qkv, w_z, w_xbc = w[:, :o_z], w[:, o_z:o_x], w[:, o_x:o_dt]
    w_dt = jnp.zeros((d, LANES), BF16).at[:, :SSM_HEADS].set(w[:, o_dt:])
    head_of = np.arange(ATTN_WIDTH) // HEAD_DIM
    hmean = jnp.asarray((head_of[:, None] == head_of[None, :]).astype(np.float32) / HEAD_DIM, BF16)
    full = lambda shp: pl.BlockSpec(shp, lambda i: (0,) * len(shp))
    row = lambda n: pl.BlockSpec((tm, n), lambda i: (i, 0))
    per_batch = pl.BlockSpec((None, 1, d), lambda i: (i // tiles_per_seq, 0, 0))
    outs = pl.pallas_call(
        _inproj_kernel,
        grid=(t // tm,),
        in_specs=[row(d), per_batch, per_batch, full((1, d)),
                  full((d, o_z)), full((d, SSM_WIDTH)), full((d, CONV_CH)), full((d, LANES)),
                  full((1, ATTN_WIDTH)), full((1, ATTN_WIDTH)), full((ATTN_WIDTH, ATTN_WIDTH))],
        out_specs=[row(ATTN_WIDTH), row(ATTN_WIDTH), row(ATTN_WIDTH), row(SSM_WIDTH), row(CONV_CH), row(LANES)],
        out_shape=[jax.ShapeDtypeStruct((t, n), F32)
                   for n in (ATTN_WIDTH, ATTN_WIDTH, ATTN_WIDTH, SSM_WIDTH, CONV_CH, LANES)],
        compiler_params=_params(("arbitrary",)),
        name="in_proj",
    )(x.reshape(t, d), shift.reshape(b, 1, d), scale.reshape(b, 1, d), gain.reshape(1, d),
      w_qkv, w_z, w_xbc, w_dt,
      jnp.tile(q_gain, ATTN_HEADS).reshape(1, ATTN_WIDTH), jnp.tile(k_gain, ATTN_HEADS).reshape(1, ATTN_WIDTH), hmean)
    return outs


def _t5_causal_buckets(distance):
    n = np.maximum(distance, 0)
    max_exact = REL_BUCKETS // 2
    large = max_exact + (np.log(np.maximum(n, 1) / max_exact) / math.log(REL_MAX_DISTANCE / max_exact)
                         * (REL_BUCKETS - max_exact)).astype(np.int64)
    large = np.minimum(large, REL_BUCKETS - 1)
    return np.where(n < max_exact, n, large).astype(np.int32)


def _window_bias(rel_bias_table, dilation):
    qi = np.arange(WIN_STEPS)[:, None]
    kj = np.arange(2 * WIN_STEPS)[None, :]
    dist = qi + WIN_STEPS - kj
    band = (dist >= 0) & (dist <= WIN_STEPS)
    onehot = (_t5_causal_buckets(dist * dilation).reshape(-1, 1) == np.arange(REL_BUCKETS)[None, :]).astype(np.float32)
    bias = jnp.dot(rel_bias_table.astype(F32).T, jnp.asarray(onehot).T, precision=lax.Precision.HIGHEST)
    bias = bias.reshape(ATTN_HEADS, WIN_STEPS, 2 * WIN_STEPS)
    return jnp.where(jnp.asarray(band)[None], bias, NEG_BIG)


ATTN_TOKENS = max(w for w, _ in PATTERNS)
ATTN_UNROLL = 16


def _attn_kernel(q_ref, kp_ref, kc_ref, vp_ref, vc_ref, bias_ref, out_ref, kw, vw, o_acc, l_acc):
    tb = ATTN_TOKENS
    first = pl.program_id(2) == 0
    kw[0:tb] = kp_ref[...]
    kw[tb:2 * tb] = kc_ref[...]
    vw[0:tb] = vp_ref[...]
    vw[tb:2 * tb] = vc_ref[...]
    lane = lax.broadcasted_iota(I32, (WIN_STEPS, LANES), 1)
    head0 = lane < HEAD_DIM
    col = lax.broadcasted_iota(I32, (WIN_STEPS, 2 * WIN_STEPS), 1)
    in_prev = col < WIN_STEPS

    for p, (_, d) in enumerate(PATTERNS):
        shift = d.bit_length() - 1
        n_blocks = tb // WIN_STEPS

        def rows(start, n, d=d):
            return pl.ds(start, n, stride=d) if d > 1 else pl.ds(start, n)

        def body(it, carry, p=p, d=d, shift=shift, rows=rows):
            for u in range(ATTN_UNROLL):
                idx = it * ATTN_UNROLL + u
                r = jnp.bitwise_and(idx, d - 1)
                j = jnp.right_shift(idx, shift)
                qs = j * (WIN_STEPS * d) + r
                q = q_ref[rows(qs, WIN_STEPS), :]
                k = kw[rows(tb + qs - WIN_STEPS * d, 2 * WIN_STEPS), :].astype(BF16)
                v = vw[rows(tb + qs - WIN_STEPS * d, 2 * WIN_STEPS), :].astype(BF16)
                no_prev = jnp.logical_and(in_prev, jnp.logical_and(first, j == 0))
                o_h, lse_h = [], []
                for h in range(2):
                    qh = jnp.where(head0 if h == 0 else jnp.logical_not(head0), q, 0.0).astype(BF16)
                    s = _dot_nt(qh, k) + bias_ref[p, h]
                    s = jnp.where(no_prev, NEG_BIG, s)
                    m = jnp.max(s, axis=-1, keepdims=True)
                    e = jnp.exp(s - m)
                    denom = jnp.sum(e, axis=-1, keepdims=True)
                    o_h.append(_dot(e.astype(BF16), v) / denom)
                    lse_h.append(m + jnp.log(denom))
                o_acc[p, rows(qs, WIN_STEPS), :] = jnp.where(head0, o_h[0], o_h[1])
                l_acc[p, rows(qs, WIN_STEPS), :] = jnp.where(head0, lse_h[0], lse_h[1])
            return carry

        lax.fori_loop(0, n_blocks // ATTN_UNROLL, body, 0)

    chunk = 256
    for c0 in range(0, tb, chunk):
        l1, l2, l3 = (l_acc[p, c0:c0 + chunk, :] for p in range(3))
        m = jnp.maximum(jnp.maximum(l1, l2), l3)
        e1, e2, e3 = jnp.exp(l1 - m), jnp.exp(l2 - m), jnp.exp(l3 - m)
        num = e1 * o_acc[0, c0:c0 + chunk, :] + e2 * o_acc[1, c0:c0 + chunk, :] + e3 * o_acc[2, c0:c0 + chunk, :]
        out_ref[c0:c0 + chunk, :] = num / (e1 + e2 + e3)


def _attention(q, k, v, bias):
    b, s, w = q.shape
    tb = ATTN_TOKENS
    pairs = ATTN_HEADS // 2
    cur = pl.BlockSpec((None, tb, LANES), lambda bi, hp, i: (bi, i, hp))
    prev = pl.BlockSpec((None, tb, LANES), lambda bi, hp, i: (bi, jnp.maximum(i - 1, 0), hp))
    return pl.pallas_call(
        _attn_kernel,
        grid=(b, pairs, s // tb),
        in_specs=[cur, prev, cur, prev, cur,
                  pl.BlockSpec((len(PATTERNS), 2, WIN_STEPS, 2 * WIN_STEPS), lambda bi, hp, i: (0, hp, 0, 0))],
        out_specs=cur,
        out_shape=jax.ShapeDtypeStruct((b, s, w), F32),
        scratch_shapes=[pltpu.VMEM((2 * tb, LANES), F32), pltpu.VMEM((2 * tb, LANES), F32),
                        pltpu.VMEM((len(PATTERNS), tb, LANES), F32), pltpu.VMEM((len(PATTERNS), tb, LANES), F32)],
        compiler_params=_params(("arbitrary",) * 3),
        name="attention",
    )(q, k, k, v, v, bias)


def _ssd_kernel(xbc_ref, halo_ref, z_ref, dtraw_ref, convw_ref, convb_ref, dtb_ref, alog_ref, dskip_ref, gain_ref,
                expand_ref, tril_ref, y_ref, state_ref):
    c = pl.program_id(1)

    @pl.when(c == 0)
    def _():
        state_ref[...] = jnp.zeros_like(state_ref)

    x = xbc_ref[...]
    halo = jnp.where(c == 0, 0.0, halo_ref[...])
    w = convw_ref[...]
    acc = x * w[SSM_CONV - 1:SSM_CONV, :] + convb_ref[...]
    row8 = lax.broadcasted_iota(I32, (SUBLANES, CONV_CH), 0)
    for shift in range(1, SSM_CONV):
        xs = pltpu.roll(x, shift, axis=0)
        hs = pltpu.roll(halo, shift, axis=0)
        head = jnp.where(row8 < shift, hs, xs[0:SUBLANES])
        xs = jnp.concatenate([head, xs[SUBLANES:]], axis=0)
        acc = acc + xs * w[SSM_CONV - 1 - shift:SSM_CONV - shift, :]
    act = _silu(acc)
    x_s = act[:, :SSM_WIDTH]
    bc0 = SSM_WIDTH
    cc0 = SSM_WIDTH + SSM_GROUPS * SSM_STATE

    t = dtraw_ref[...] + dtb_ref[...]
    dt = jnp.maximum(t, 0.0) + jnp.log(1.0 + jnp.exp(-jnp.abs(t)))
    a = dt * (-jnp.exp(alog_ref[...]))
    a_cs = _dot_exact_lhs(tril_ref[...], a)
    a_cs_t = a_cs.T
    a_last = a_cs[SSM_CHUNK - 1:SSM_CHUNK, :]
    expand = expand_ref[...]
    dt_e = _dot_exact_rhs(dt, expand)
    ea_e = _dot_exact_rhs(jnp.exp(a_cs), expand)
    dte_e = _dot_exact_rhs(jnp.exp(a_last - a_cs), expand)
    xdt = x_s * dt_e
    xw = (xdt * dte_e).astype(BF16)
    xdt_b = xdt.astype(BF16)

    li = lax.broadcasted_iota(I32, (SSM_CHUNK, SSM_CHUNK), 0)
    si = lax.broadcasted_iota(I32, (SSM_CHUNK, SSM_CHUNK), 1)
    causal = li >= si

    ys = []
    for g in range(SSM_GROUPS):
        gs = slice(g * GROUP_WIDTH, (g + 1) * GROUP_WIDTH)
        b_g = act[:, bc0 + g * SSM_STATE:bc0 + (g + 1) * SSM_STATE]
        c_g = act[:, cc0 + g * SSM_STATE:cc0 + (g + 1) * SSM_STATE].astype(BF16)
        cb = _dot_nt(c_g, b_g.astype(BF16))
        state = state_ref[g]
        y_off = _dot(c_g, state.astype(BF16)) * ea_e[:, gs]
        parts = []
        for j in range(HEADS_PER_GROUP):
            hh = g * HEADS_PER_GROUP + j
            seg = a_cs[:, hh:hh + 1] - a_cs_t[hh:hh + 1, :]
            decay = jnp.exp(jnp.where(causal, seg, NEG_BIG))
            m = (cb * decay).astype(BF16)
            parts.append(_dot(m, xdt_b[:, hh * SSM_HEAD_DIM:(hh + 1) * SSM_HEAD_DIM]))
        ys.append(jnp.concatenate(parts, axis=1) + y_off)
        state_ref[g] = state * ea_e[SSM_CHUNK - 1:SSM_CHUNK, gs] + _dot(b_g.T.astype(BF16), xw[:, gs])
    y = jnp.concatenate(ys, axis=1) + dskip_ref[...] * x_s
    y = y * _silu(z_ref[...])
    gain = gain_ref[...]
    for g in range(SSM_GROUPS):
        gs = slice(g * GROUP_WIDTH, (g + 1) * GROUP_WIDTH)
        yg = y[:, gs]
        ms = jnp.mean(yg * yg, axis=-1, keepdims=True)
        y_ref[:, gs] = yg * lax.rsqrt(ms + NORM_EPS) * gain[:, gs]


def _ssd(xbc, z, dt_raw, conv_w, conv_b, dt_bias, a_log, d_skip, norm_gain, b, s):
    t = b * s
    nc = s // SSM_CHUNK
    pad_heads = lambda v: jnp.zeros((1, LANES), F32).at[0, :SSM_HEADS].set(v)
    head_of_lane = np.arange(SSM_WIDTH) // SSM_HEAD_DIM
    expand = jnp.asarray((np.arange(LANES)[:, None] == head_of_lane[None, :]).astype(np.float32), BF16)
    tril = jnp.asarray(np.tril(np.ones((SSM_CHUNK, SSM_CHUNK), np.float32)), BF16)
    halo_blocks = SSM_CHUNK // SUBLANES
    chunk = lambda n: pl.BlockSpec((SSM_CHUNK, n), lambda bi, c: (bi * nc + c, 0))
    full = lambda shp: pl.BlockSpec(shp, lambda bi, c: (0,) * len(shp))
    halo = pl.BlockSpec((SUBLANES, CONV_CH), lambda bi, c: (jnp.maximum((bi * nc + c) * halo_blocks - 1, 0), 0))
    return pl.pallas_call(
        _ssd_kernel,
        grid=(b, nc),
        in_specs=[chunk(CONV_CH), halo, chunk(SSM_WIDTH), chunk(LANES),
                  full((SSM_CONV, CONV_CH)), full((1, CONV_CH)), full((1, LANES)), full((1, LANES)),
                  full((1, SSM_WIDTH)), full((1, SSM_WIDTH)), full((LANES, SSM_WIDTH)), full((SSM_CHUNK, SSM_CHUNK))],
        out_specs=chunk(SSM_WIDTH),
        out_shape=jax.ShapeDtypeStruct((t, SSM_WIDTH), F32),
        scratch_shapes=[pltpu.VMEM((SSM_GROUPS, SSM_STATE, GROUP_WIDTH), F32)],
        compiler_params=_params(("arbitrary", "arbitrary")),
        name="ssd",
    )(xbc, xbc, z, dt_raw, conv_w, conv_b.reshape(1, CONV_CH), pad_heads(dt_bias), pad_heads(a_log),
      jnp.repeat(d_skip, SSM_HEAD_DIM).reshape(1, SSM_WIDTH), norm_gain.reshape(1, SSM_WIDTH), expand, tril)


WORD = jnp.int32
TOKEN_ROWS = D_MODEL // (2 * LANES)
HIGH_HALF = np.int32(-65536)


def _to_token_tiles(ref, x):
    n = x.shape[0]
    for c in range(TOKEN_ROWS):
        lo = lax.bitcast_convert_type(x[:, c * LANES:(c + 1) * LANES].astype(BF16).astype(F32), WORD)
        hi = lax.bitcast_convert_type(x[:, (c + TOKEN_ROWS) * LANES:(c + TOKEN_ROWS + 1) * LANES]
                                      .astype(BF16).astype(F32), WORD)
        ref[pl.ds(c, n, stride=TOKEN_ROWS), :] = jnp.bitwise_or(lax.shift_right_logical(lo, 16),
                                                                 jnp.bitwise_and(hi, HIGH_HALF))


def _from_token_tiles(ref, n, token0=0):
    lows, highs = [], []
    for c in range(TOKEN_ROWS):
        word = ref[pl.ds(token0 * TOKEN_ROWS + c, n, stride=TOKEN_ROWS), :]
        lows.append(lax.bitcast_convert_type(jnp.left_shift(word, 16), F32))
        highs.append(lax.bitcast_convert_type(jnp.bitwise_and(word, HIGH_HALF), F32))
    return jnp.concatenate(lows + highs, axis=1)


def _outproj_kernel(attn_ref, ssm_ref, x_ref, gate_ref, shift_ref, scale_ref, g_ref, wa_ref, ws_ref, x1_ref, h2_ref,
                    h2t_ref):
    mixed = _dot(attn_ref[...].astype(BF16), wa_ref[...]) + _dot(ssm_ref[...].astype(BF16), ws_ref[...])
    x1 = x_ref[...] + gate_ref[...] * mixed
    x1_ref[...] = x1
    ms = jnp.mean(x1 * x1, axis=-1, keepdims=True)
    h = x1 * lax.rsqrt(ms + NORM_EPS) * g_ref[...]
    h2 = h * (1.0 + scale_ref[...]) + shift_ref[...]
    h2_ref[...] = h2
    for c in range(TOKEN_ROWS):
        lo = lax.bitcast_convert_type(h2[:, c * LANES:(c + 1) * LANES].astype(BF16).astype(F32), WORD)
        hi = lax.bitcast_convert_type(h2[:, (c + TOKEN_ROWS) * LANES:(c + TOKEN_ROWS + 1) * LANES]
                                      .astype(BF16).astype(F32), WORD)
        h2t_ref[c] = jnp.bitwise_or(lax.shift_right_logical(lo, 16), jnp.bitwise_and(hi, HIGH_HALF))


def _out_proj(attn, ssm, x, gate, shift, scale, gain, w_out, b, s, tm=256):
    t = b * s
    d = D_MODEL
    tiles_per_seq = s // tm
    w = w_out.astype(BF16)
    row = lambda n: pl.BlockSpec((tm, n), lambda i: (i, 0))
    full = lambda shp: pl.BlockSpec(shp, lambda i: (0,) * len(shp))
    per_batch = pl.BlockSpec((None, 1, d), lambda i: (i // tiles_per_seq, 0, 0))
    return pl.pallas_call(
        _outproj_kernel,
        grid=(t // tm,),
        in_specs=[row(ATTN_WIDTH), row(SSM_WIDTH), row(d), per_batch, per_batch, per_batch, full((1, d)),
                  full((ATTN_WIDTH, d)), full((SSM_WIDTH, d))],
        out_specs=[row(d), row(d), pl.BlockSpec((TOKEN_ROWS, tm, LANES), lambda i: (0, i, 0))],
        out_shape=[jax.ShapeDtypeStruct((t, d), F32)] * 2 + [jax.ShapeDtypeStruct((TOKEN_ROWS, t, LANES), WORD)],
        compiler_params=_params(("arbitrary",)),
        name="out_proj",
    )(attn.reshape(t, ATTN_WIDTH), ssm, x.reshape(t, d),
      gate.reshape(b, 1, d), shift.reshape(b, 1, d), scale.reshape(b, 1, d), gain.reshape(1, d),
      w[:ATTN_WIDTH], w[ATTN_WIDTH:])


def _mixer_sublayer(x, mod, norm_mix_gain, w_in, q_norm_gain, k_norm_gain, rel_bias_table, conv_w, conv_b, dt_bias,
                    a_log, d_skip, ssm_norm_gain, w_out, norm_ffn_gain):
    b, s, d = x.shape
    shift_m, scale_m, gate_m, shift_f, scale_f, _ = jnp.split(mod, 6, axis=-1)
    q, k, v, z, xbc, dt_raw = _in_proj(x, shift_m, scale_m, norm_mix_gain, w_in, q_norm_gain, k_norm_gain)
    bias = jnp.stack([_window_bias(rel_bias_table, dilation) for _, dilation in PATTERNS])
    attn = _attention(q.reshape(b, s, ATTN_WIDTH), k.reshape(b, s, ATTN_WIDTH), v.reshape(b, s, ATTN_WIDTH), bias)
    ssm = _ssd(xbc, z, dt_raw, conv_w, conv_b, dt_bias, a_log, d_skip, ssm_norm_gain, b, s)
    return _out_proj(attn, ssm, x, gate_m, shift_f, scale_f, norm_ffn_gain, w_out, b, s)


def _first_argmax(v, iota, limit):
    m = jnp.max(v, axis=0, keepdims=True)
    idx = jnp.min(jnp.where(v == m, iota, limit), axis=0, keepdims=True)
    return m, idx


def _router_kernel(h_ref, wt_ref, bias_ref, upper_ref, eidx_ref, rank_ref, gate_ref, counts_ref, carry_ref):
    @pl.when(pl.program_id(0) == 0)
    def _():
        carry_ref[...] = jnp.zeros_like(carry_ref)

    tm = h_ref.shape[0]
    h = h_ref[...]
    wt = wt_ref[...]
    h_hi = h.astype(BF16)
    h_lo = (h - h_hi.astype(F32)).astype(BF16)
    w_hi = wt.astype(BF16)
    w_lo = (wt - w_hi.astype(F32)).astype(BF16)
    logits = _dot_nt(w_hi, h_hi) + _dot_nt(w_hi, h_lo) + _dot_nt(w_lo, h_hi)
    scores = _sigmoid(logits)
    choice = scores + bias_ref[...]
    neg_inf = -jnp.inf

    iota_g = lax.broadcasted_iota(I32, (EXPERTS_PER_GROUP, tm), 0).astype(F32)
    group_rows = []
    for g in range(N_EXPERT_GROUPS):
        v = choice[g * EXPERTS_PER_GROUP:(g + 1) * EXPERTS_PER_GROUP]
        m1, i1 = _first_argmax(v, iota_g, float(EXPERTS_PER_GROUP))
        m2 = jnp.max(jnp.where(iota_g == i1, neg_inf, v), axis=0, keepdims=True)
        group_rows.append(m1 + m2)
    group_scores = jnp.concatenate(group_rows, axis=0)

    iota_n = lax.broadcasted_iota(I32, (N_EXPERT_GROUPS, tm), 0).astype(F32)
    chosen = jnp.zeros((N_EXPERT_GROUPS, tm), F32)
    for _ in range(TOPK_GROUPS):
        _, gi = _first_argmax(group_scores, iota_n, float(N_EXPERT_GROUPS))
        hit = iota_n == gi
        chosen = jnp.where(hit, 1.0, chosen)
        group_scores = jnp.where(hit, neg_inf, group_scores)

    masked = jnp.concatenate(
        [jnp.where(chosen[g:g + 1] > 0.0, choice[g * EXPERTS_PER_GROUP:(g + 1) * EXPERTS_PER_GROUP], neg_inf)
         for g in range(N_EXPERT_GROUPS)], axis=0)

    iota_e = lax.broadcasted_iota(I32, (N_EXPERTS, tm), 0).astype(F32)
    picked, gates = [], []
    onehot = jnp.zeros((N_EXPERTS, tm), F32)
    for _ in range(TOP_K):
        _, ei = _first_argmax(masked, iota_e, float(N_EXPERTS))
        hit = iota_e == ei
        gates.append(jnp.sum(jnp.where(hit, scores, 0.0), axis=0, keepdims=True))
        masked = jnp.where(hit, neg_inf, masked)
        onehot = jnp.where(hit, 1.0, onehot)
        picked.append(ei)
    gate_sum = gates[0]
    for gk in gates[1:]:
        gate_sum = gate_sum + gk

    base = _dot(onehot.astype(BF16), upper_ref[...]) + carry_ref[...]
    ranks = [jnp.sum(jnp.where(iota_e == ei, base, 0.0), axis=0, keepdims=True) for ei in picked]
    carry_ref[...] = carry_ref[...] + jnp.sum(onehot, axis=1, keepdims=True)

    eidx_ref[...] = jnp.concatenate(picked, axis=0).astype(I32)
    rank_ref[...] = jnp.concatenate(ranks, axis=0).astype(I32)
    gate_ref[...] = jnp.concatenate([gk / gate_sum * ROUTED_SCALE for gk in gates], axis=0)
    counts_ref[...] = carry_ref[...].astype(I32)


def _router(h2, w_router, router_bias, tm=256):
    t, d = h2.shape
    upper = jnp.asarray(np.triu(np.ones((tm, tm), np.float32), 1), BF16)
    tok = pl.BlockSpec((TOP_K, tm), lambda i: (0, i))
    full = lambda shp: pl.BlockSpec(shp, lambda i: (0,) * len(shp))
    return pl.pallas_call(
        _router_kernel,
        grid=(t // tm,),
        in_specs=[pl.BlockSpec((tm, d), lambda i: (i, 0)), full((N_EXPERTS, d)), full((N_EXPERTS, 1)), full((tm, tm))],
        out_specs=[tok, tok, tok, full((N_EXPERTS, 1))],
        out_shape=[jax.ShapeDtypeStruct((TOP_K, t), I32), jax.ShapeDtypeStruct((TOP_K, t), I32),
                   jax.ShapeDtypeStruct((TOP_K, t), F32), jax.ShapeDtypeStruct((N_EXPERTS, 1), I32)],
        scratch_shapes=[pltpu.VMEM((N_EXPERTS, 1), F32)],
        compiler_params=_params(("arbitrary",)),
        name="router",
    )(h2, w_router.T, router_bias.reshape(N_EXPERTS, 1), upper)


def _positions_kernel(counts_ref, lower_ref, eidx_ref, rank_ref, pos_ref):
    tm = eidx_ref.shape[1]
    counts = jnp.broadcast_to(counts_ref[...].astype(F32), (N_EXPERTS, LANES))
    offsets = _dot_exact_lhs(lower_ref[...], counts)[:, 0:1]
    iota_e = lax.broadcasted_iota(I32, (N_EXPERTS, tm), 0).astype(F32)
    e = eidx_ref[...].astype(F32)
    rows = [jnp.sum(jnp.where(iota_e == e[k:k + 1], offsets, 0.0), axis=0, keepdims=True) for k in range(TOP_K)]
    pos_ref[...] = jnp.concatenate(rows, axis=0).astype(I32) + rank_ref[...]


def _positions(counts, eidx, rank, tm):
    t = eidx.shape[1]
    lower = jnp.asarray(np.tril(np.ones((N_EXPERTS, N_EXPERTS), np.float32), -1), BF16)
    tok = pl.BlockSpec((TOP_K, tm), lambda i: (0, i))
    return pl.pallas_call(
        _positions_kernel,
        grid=(t // tm,),
        in_specs=[pl.BlockSpec((N_EXPERTS, 1), lambda i: (0, 0)), pl.BlockSpec((N_EXPERTS, N_EXPERTS), lambda i: (0, 0)),
                  tok, tok],
        out_specs=tok,
        out_shape=jax.ShapeDtypeStruct((TOP_K, t), I32),
        compiler_params=_params(("arbitrary",)),
        name="positions",
    )(counts, lower, eidx, rank)


EXPERT_BLOCK = 512
TAIL_UNIT = 128
TAIL_PIECES = tuple(1 << i for i in reversed(range(EXPERT_BLOCK.bit_length() - 1)))
X_SLOTS = 4
Y_SLOTS = 4


def _experts_kernel(start_ref, count_ref, nxt_ref, slot_ref, first_ref, blk0_ref, full0_ref, ptail_ref, ltail_ref,
                    blktok_ref, nblocks_ref, xs_hbm, wg_hbm, wu_hbm, wd_hbm, ys_hbm,
                    wg_buf, wu_buf, wd_buf, wg_bf, wu_bf, wd_bf, xbuf, ybuf, ytail, wsem, xsem, ysem, tsem):
    e = pl.program_id(0)
    last_step = e == pl.num_programs(0) - 1
    start, count = start_ref[e], count_ref[e]
    n_full = jnp.right_shift(count, EXPERT_BLOCK.bit_length() - 1)
    tail = jnp.bitwise_and(count, EXPERT_BLOCK - 1)
    n_blk = n_full + (tail > 0).astype(I32)
    blk0, full0 = blk0_ref[e], full0_ref[e]
    slot, nxt = slot_ref[e], nxt_ref[e]

    def token_rows(token, n):
        return pl.ds(pl.multiple_of(token * TOKEN_ROWS, TOKEN_ROWS), n * TOKEN_ROWS)

    def fetch(ex, s):
        return (pltpu.make_async_copy(wg_hbm.at[ex], wg_buf.at[s], wsem.at[s, 0]),
                pltpu.make_async_copy(wu_hbm.at[ex], wu_buf.at[s], wsem.at[s, 1]),
                pltpu.make_async_copy(wd_hbm.at[ex], wd_buf.at[s], wsem.at[s, 2]))

    def x_copy(token, s):
        return pltpu.make_async_copy(xs_hbm.at[token_rows(token, EXPERT_BLOCK)], xbuf.at[s], xsem.at[s])

    def y_copy(token, s):
        return pltpu.make_async_copy(ybuf.at[s], ys_hbm.at[token_rows(token, EXPERT_BLOCK)], ysem.at[s])

    def tail_copies(token, length):
        out = []
        for piece in TAIL_PIECES:
            bigger = (EXPERT_BLOCK - 1) & ~(2 * piece - 1)
            done = jnp.bitwise_and(length, bigger)
            cp = pltpu.make_async_copy(ytail.at[token_rows(done, piece)], ys_hbm.at[token_rows(token + done, piece)],
                                       tsem)
            out.append((jnp.bitwise_and(length, piece) != 0, cp))
        return out

    def block(s, rows=EXPERT_BLOCK):
        x = _from_token_tiles(xbuf.at[s], rows).astype(BF16)
        g = _dot(x, wg_bf[...])
        u = _dot(x, wu_bf[...])
        return _dot((_silu(g) * u).astype(BF16), wd_bf[...])

    @pl.when(count > 0)
    def _():
        @pl.when(first_ref[e] == 1)
        def _():
            for g in range(X_SLOTS - 1):
                @pl.when(g < nblocks_ref[0])
                def _(g=g):
                    x_copy(blktok_ref[g], g).start()
            for cp in fetch(e, slot):
                cp.start()

        for cp in fetch(e, slot):
            cp.wait()

        @pl.when(nxt >= 0)
        def _():
            for cp in fetch(nxt, 1 - slot):
                cp.start()

        wg_bf[...] = wg_buf[slot].astype(BF16)
        wu_bf[...] = wu_buf[slot].astype(BF16)
        wd_bf[...] = wd_buf[slot].astype(BF16)

        def take_x(i):
            g = blk0 + i
            xs_slot = jnp.bitwise_and(g, X_SLOTS - 1)
            x_copy(start, xs_slot).wait()
            ahead = g + (X_SLOTS - 1)

            @pl.when(ahead < nblocks_ref[0])
            def _():
                x_copy(blktok_ref[ahead], jnp.bitwise_and(ahead, X_SLOTS - 1)).start()

            return xs_slot

        def full_block(i, carry):
            y = block(take_x(i))
            j = full0 + i
            ys_slot = jnp.bitwise_and(j, Y_SLOTS - 1)

            @pl.when(j >= Y_SLOTS)
            def _():
                y_copy(start, ys_slot).wait()

            _to_token_tiles(ybuf.at[ys_slot], y)
            y_copy(start + i * EXPERT_BLOCK, ys_slot).start()
            return carry

        lax.fori_loop(0, n_full, full_block, 0)

        @pl.when(tail > 0)
        def _():
            xs_slot = take_x(n_full)
            for pred, cp in tail_copies(start, ptail_ref[e]):
                @pl.when(pred)
                def _(cp=cp):
                    cp.wait()
            units = jnp.right_shift(tail + (TAIL_UNIT - 1), TAIL_UNIT.bit_length() - 1)
            for u in range(1, EXPERT_BLOCK // TAIL_UNIT + 1):
                @pl.when(units == u)
                def _(u=u):
                    rows = u * TAIL_UNIT
                    _to_token_tiles(ytail.at[pl.ds(0, rows * TOKEN_ROWS)], block(xs_slot, rows))
            for pred, cp in tail_copies(start + n_full * EXPERT_BLOCK, tail):
                @pl.when(pred)
                def _(cp=cp):
                    cp.start()

    @pl.when(last_step)
    def _():
        total_full = full0 + n_full
        for back in range(1, Y_SLOTS + 1):
            @pl.when(total_full >= back)
            def _(back=back):
                y_copy(0, jnp.bitwise_and(total_full - back, Y_SLOTS - 1)).wait()
        for pred, cp in tail_copies(0, ltail_ref[0]):
            @pl.when(pred)
            def _(cp=cp):
                cp.wait()


def _max_expert_blocks(n_rows):
    return n_rows // EXPERT_BLOCK + N_EXPERTS


def _expert_metadata(counts, n_rows):
    ids = jnp.arange(N_EXPERTS, dtype=I32)
    used = counts > 0
    starts = jnp.cumsum(counts) - counts
    n_blk = (counts + EXPERT_BLOCK - 1) // EXPERT_BLOCK
    n_full = counts // EXPERT_BLOCK
    tail = counts % EXPERT_BLOCK
    blk0 = jnp.cumsum(n_blk) - n_blk
    full0 = jnp.cumsum(n_full) - n_full
    next_used = lax.cummin(jnp.where(used, ids, N_EXPERTS), reverse=True)
    next_after = jnp.concatenate([next_used[1:], jnp.full((1,), N_EXPERTS, I32)])
    nxt = jnp.where(next_after < N_EXPERTS, next_after, -1)
    ordinal = jnp.cumsum(used.astype(I32)) - 1
    slot = ordinal % 2
    first = jnp.logical_and(used, ordinal == 0)
    latest = lax.cummax(jnp.where(tail > 0, ids, -1))
    before = jnp.concatenate([jnp.full((1,), -1, I32), latest[:-1]])
    pick = lambda index, values: jnp.sum(jnp.where(index[:, None] == ids[None, :], values[None, :], 0), axis=1)
    ptail = pick(before, tail)
    ltail = pick(latest[-1:], tail)
    block_ends = jnp.cumsum(n_blk)
    g = jnp.arange(_max_expert_blocks(n_rows), dtype=I32)
    eg = jnp.sum((g[:, None] >= block_ends[None, :]).astype(I32), axis=1)
    blktok = g * EXPERT_BLOCK + pick(eg, starts - blk0 * EXPERT_BLOCK)
    return tuple(v.astype(I32) for v in (starts, counts, nxt, slot, first, blk0, full0, ptail, ltail, blktok,
                                         block_ends[-1:]))


def _experts(xs, counts, w_gate, w_up, w_down):
    d = D_MODEL
    meta = _expert_metadata(counts, xs.shape[0] // TOKEN_ROWS - EXPERT_BLOCK)
    hbm = pl.BlockSpec(memory_space=pl.ANY)
    blk = (EXPERT_BLOCK * TOKEN_ROWS, LANES)
    grid_spec = pltpu.PrefetchScalarGridSpec(
        num_scalar_prefetch=len(meta),
        grid=(N_EXPERTS,),
        in_specs=[hbm, hbm, hbm, hbm],
        out_specs=hbm,
        scratch_shapes=[pltpu.VMEM((2, d, EXPERT_FF), F32), pltpu.VMEM((2, d, EXPERT_FF), F32),
                        pltpu.VMEM((2, EXPERT_FF, d), F32),
                        pltpu.VMEM((d, EXPERT_FF), BF16), pltpu.VMEM((d, EXPERT_FF), BF16),
                        pltpu.VMEM((EXPERT_FF, d), BF16),
                        pltpu.VMEM((X_SLOTS,) + blk, WORD), pltpu.VMEM((Y_SLOTS,) + blk, WORD), pltpu.VMEM(blk, WORD),
                        pltpu.SemaphoreType.DMA((2, 3)), pltpu.SemaphoreType.DMA((X_SLOTS,)),
                        pltpu.SemaphoreType.DMA((Y_SLOTS,)), pltpu.SemaphoreType.DMA(())],
    )
    return pl.pallas_call(
        _experts_kernel,
        grid_spec=grid_spec,
        out_shape=jax.ShapeDtypeStruct(xs.shape, WORD),
        compiler_params=_params(("arbitrary",)),
        name="experts",
    )(*meta, xs, w_gate, w_up, w_down)


SC_CORES = 2
SC_SUBCORES = 16
SC_WORKERS = SC_CORES * SC_SUBCORES
SC_CHUNK = 128
SC_RING = 4


def _sc_scatter_rows(planes, idx, n_out_rows):
    n_planes, t, _ = planes.shape
    ranges = SC_WORKERS // n_planes
    n_chunks = t // ranges // SC_CHUNK
    idx = idx.reshape(TOP_K, n_planes, ranges, n_chunks, SC_CHUNK)
    mesh = plsc.VectorSubcoreMesh(core_axis_name="c", subcore_axis_name="s", num_cores=SC_CORES,
                                  num_subcores=SC_SUBCORES)

    def body(planes_hbm, idx_hbm, out_hbm, idx_v, rows_v, lsem, ssem):
        wid = lax.axis_index("s") * SC_CORES + lax.axis_index("c")
        plane = wid % n_planes
        token0 = (wid // n_planes) * (n_chunks * SC_CHUNK)
        for k in range(TOP_K):
            pltpu.sync_copy(idx_hbm.at[k, plane, wid // n_planes], idx_v.at[k])

        def load(slot, c):
            return pltpu.make_async_copy(planes_hbm.at[plane, pl.ds(token0 + c * SC_CHUNK, SC_CHUNK)], rows_v.at[slot],
                                         lsem.at[slot])

        def scatter(slot, c, k):
            return pltpu.make_async_copy(rows_v.at[slot], out_hbm.at[idx_v.at[k, c]], ssem.at[slot])

        @pl.loop(0, n_chunks, step=SC_RING)
        def _(g):
            for slot in range(SC_RING):
                @pl.when(g > 0)
                def _(slot=slot):
                    for k in range(TOP_K):
                        scatter(slot, 0, k).wait()
                load(slot, g + slot).start()
            for slot in range(SC_RING):
                load(slot, g + slot).wait()
                for k in range(TOP_K):
                    scatter(slot, g + slot, k).start()

        for slot in range(SC_RING):
            for k in range(TOP_K):
                scatter(slot, 0, k).wait()

    return pl.kernel(
        body, mesh=mesh,
        out_type=jax.ShapeDtypeStruct((n_out_rows, LANES), planes.dtype),
        scratch_types=[pltpu.VMEM((TOP_K, n_chunks, SC_CHUNK), I32), pltpu.VMEM((SC_RING, SC_CHUNK, LANES), planes.dtype),
                       pltpu.SemaphoreType.DMA((SC_RING,)), pltpu.SemaphoreType.DMA((SC_RING,))],
        name="sc_scatter",
    )(planes, idx)


def _sc_gather_rows(table, idx):
    n_workers, n_chunks, chunk = idx.shape
    rows_per_worker = n_chunks * chunk
    mesh = plsc.VectorSubcoreMesh(core_axis_name="c", subcore_axis_name="s", num_cores=SC_CORES,
                                  num_subcores=SC_SUBCORES)

    def body(table_hbm, idx_hbm, out_hbm, idx_v, rows_v, gsem, wsem):
        wid = lax.axis_index("s") * SC_CORES + lax.axis_index("c")
        base = wid * rows_per_worker
        pltpu.sync_copy(idx_hbm.at[wid], idx_v)

        def write(slot, c):
            return pltpu.make_async_copy(rows_v.at[slot], out_hbm.at[pl.ds(base + c * chunk, chunk)], wsem.at[slot])

        def gather(slot, c):
            return pltpu.make_async_copy(table_hbm.at[idx_v.at[c]], rows_v.at[slot], gsem.at[slot])

        @pl.loop(0, n_chunks, step=SC_RING)
        def _(g):
            for slot in range(SC_RING):
                @pl.when(g > 0)
                def _(slot=slot):
                    write(slot, 0).wait()
                gather(slot, g + slot).start()
            for slot in range(SC_RING):
                gather(slot, g + slot).wait()
                write(slot, g + slot).start()

        for slot in range(SC_RING):
            write(slot, 0).wait()

    return pl.kernel(
        body, mesh=mesh,
        out_type=jax.ShapeDtypeStruct((n_workers * rows_per_worker, LANES), table.dtype),
        scratch_types=[pltpu.VMEM((n_chunks, chunk), I32), pltpu.VMEM((SC_RING, chunk, LANES), table.dtype),
                       pltpu.SemaphoreType.DMA((SC_RING,)), pltpu.SemaphoreType.DMA((SC_RING,))],
        name="sc_gather",
    )(table, idx)


def _combine_kernel(rows_ref, gates_ref, h_ref, x1_ref, gatef_ref, wg_ref, wu_ref, wd_ref, out_ref):
    hb = h_ref[...].astype(BF16)
    shared = _dot((_silu(_dot(hb, wg_ref[...])) * _dot(hb, wu_ref[...])).astype(BF16), wd_ref[...])
    gates = gates_ref[...]

    def expert_rows(k):
        words = [rows_ref[k * TOKEN_ROWS + c] for c in range(TOKEN_ROWS)]
        lows = [lax.bitcast_convert_type(jnp.left_shift(w, 16), F32) for w in words]
        highs = [lax.bitcast_convert_type(jnp.bitwise_and(w, HIGH_HALF), F32) for w in words]
        return jnp.concatenate(lows + highs, axis=1)

    routed = expert_rows(0) * gates[:, 0:1]
    for k in range(1, TOP_K):
        routed = routed + expert_rows(k) * gates[:, k:k + 1]
    out_ref[...] = x1_ref[...] + gatef_ref[...] * (shared + routed)


def _combine(gathered, gates_t, h2, x1, gate_f, w_gate_s, w_up_s, w_down_s, b, s, tm):
    t, d = h2.shape
    tiles_per_seq = s // tm
    row = lambda n: pl.BlockSpec((tm, n), lambda i: (i, 0))
    full = lambda shp: pl.BlockSpec(shp, lambda i: (0,) * len(shp))
    return pl.pallas_call(
        _combine_kernel,
        grid=(t // tm,),
        in_specs=[pl.BlockSpec((TOP_K * TOKEN_ROWS, tm, LANES), lambda i: (0, i, 0)),
                  row(TOP_K), row(d), row(d),
                  pl.BlockSpec((None, 1, d), lambda i: (i // tiles_per_seq, 0, 0)),
                  full((d, EXPERT_FF)), full((d, EXPERT_FF)), full((EXPERT_FF, d))],
        out_specs=row(d),
        out_shape=jax.ShapeDtypeStruct((t, d), F32),
        compiler_params=_params(("arbitrary",)),
        name="combine",
    )(gathered, gates_t, h2, x1, gate_f.reshape(b, 1, d),
      w_gate_s.astype(BF16), w_up_s.astype(BF16), w_down_s.astype(BF16))


def _moe_sublayer(x1, h2, h2t, gate_f, w_router, router_bias, w_gate, w_up, w_down, w_gate_s, w_up_s, w_down_s, b, s,
                  tm=256):
    t = b * s
    eidx, rank, gates, counts = _router(h2, w_router, router_bias)
    pos = _positions(counts, eidx, rank, tm)
    idx = pos[:, None, :] * TOKEN_ROWS + jnp.arange(TOKEN_ROWS, dtype=I32)[None, :, None]
    xs = _sc_scatter_rows(h2t, idx, (t * TOP_K + EXPERT_BLOCK) * TOKEN_ROWS)
    ys = _experts(xs, counts[:, 0], w_gate, w_up, w_down)
    gathered = _sc_gather_rows(ys, idx.reshape(SC_WORKERS, -1, SC_CHUNK)).reshape(TOP_K * TOKEN_ROWS, t, LANES)
    return _combine(gathered, gates.T, h2, x1, gate_f, w_gate_s, w_up_s, w_down_s, b, s, tm)


def kernel(x, c, w_ada, b_ada, norm_mix_gain, w_in, q_norm_gain, k_norm_gain, rel_bias_table, conv_w, conv_b, dt_bias,
           a_log, d_skip, ssm_norm_gain, w_out, norm_ffn_gain, w_router, router_bias, w_gate_experts, w_up_experts,
           w_down_experts, w_gate_shared, w_up_shared, w_down_shared):
    b, s, d = x.shape
    for layer in range(w_ada.shape[0]):
        mod = _adaln(c, w_ada[layer], b_ada[layer])
        x1, h2, h2t = _mixer_sublayer(x, mod, norm_mix_gain[layer], w_in[layer], q_norm_gain[layer], k_norm_gain[layer],
                                 rel_bias_table, conv_w[layer], conv_b[layer], dt_bias[layer], a_log[layer],
                                 d_skip[layer], ssm_norm_gain[layer], w_out[layer], norm_ffn_gain[layer])
        gate_f = mod[:, 5 * d:]
        out = _moe_sublayer(x1, h2, h2t, gate_f, w_router[layer], router_bias[layer], w_gate_experts[layer],
                            w_up_experts[layer], w_down_experts[layer], w_gate_shared[layer], w_up_shared[layer],
                            w_down_shared[layer], b, s)
        x = out.reshape(b, s, d)
    return x
```

```python
import functools
import math

import numpy as np
import jax
import jax.numpy as jnp
from jax import lax
from jax.experimental import pallas as pl
from jax.experimental.pallas import tpu as pltpu
from jax.experimental.pallas import tpu_sc as plsc

F32 = jnp.float32
BF16 = jnp.bfloat16
I32 = jnp.int32

D_MODEL = 1024
ATTN_HEADS = 8
HEAD_DIM = 64
ATTN_WIDTH = ATTN_HEADS * HEAD_DIM
PATTERNS = ((128, 1), (512, 4), (2048, 16))
WIN_STEPS = 128
REL_BUCKETS = 32
REL_MAX_DISTANCE = 2048
SSM_HEADS = 24
SSM_HEAD_DIM = 64
SSM_WIDTH = SSM_HEADS * SSM_HEAD_DIM
SSM_GROUPS = 4
HEADS_PER_GROUP = SSM_HEADS // SSM_GROUPS
GROUP_WIDTH = SSM_WIDTH // SSM_GROUPS
SSM_STATE = 128
SSM_CONV = 4
SSM_CHUNK = 128
CONV_CH = SSM_WIDTH + 2 * SSM_GROUPS * SSM_STATE
N_EXPERTS = 256
TOP_K = 8
N_EXPERT_GROUPS = 8
EXPERTS_PER_GROUP = N_EXPERTS // N_EXPERT_GROUPS
TOPK_GROUPS = 4
EXPERT_FF = 256
ROUTED_SCALE = 2.5
NORM_EPS = 1e-6

LANES = 128
SUBLANES = 8
NEG_BIG = -1e30
VMEM_LIMIT = 56 * 1024 * 1024


def _params(sem, vmem=VMEM_LIMIT):
    return pltpu.CompilerParams(dimension_semantics=sem, vmem_limit_bytes=vmem)


def _sigmoid(x):
    return 1.0 / (1.0 + jnp.exp(-x))


def _silu(x):
    return x * _sigmoid(x)


def _split3(x):
    hi = x.astype(BF16)
    r = x - hi.astype(F32)
    mid = r.astype(BF16)
    lo = (r - mid.astype(F32)).astype(BF16)
    return hi, mid, lo


def _dot(a, b):
    return jnp.dot(a, b, preferred_element_type=F32)


def _dot_nt(a, b):
    return lax.dot_general(a, b, (((1,), (1,)), ((), ())), preferred_element_type=F32)


def _dot_exact_rhs(a, b_exact):
    hi, mid, lo = _split3(a)
    return _dot(hi, b_exact) + _dot(mid, b_exact) + _dot(lo, b_exact)


def _dot_exact_lhs(a_exact, b):
    hi, mid, lo = _split3(b)
    return _dot(a_exact, hi) + _dot(a_exact, mid) + _dot(a_exact, lo)


def _adaln_kernel(c_ref, w_ref, b_ref, o_ref):
    s = _silu(c_ref[...]).astype(BF16)
    o_ref[...] = _dot(s, w_ref[...].astype(BF16)) + b_ref[...]


def _adaln(c, w_ada, b_ada):
    b, d = c.shape
    n = w_ada.shape[1]
    rows = SUBLANES
    c_pad = jnp.zeros((rows, d), F32).at[:b].set(c)
    tn = 1024
    out = pl.pallas_call(
        _adaln_kernel,
        grid=(n // tn,),
        in_specs=[pl.BlockSpec((rows, d), lambda j: (0, 0)),
                  pl.BlockSpec((d, tn), lambda j: (0, j)),
                  pl.BlockSpec((1, tn), lambda j: (0, j))],
        out_specs=pl.BlockSpec((rows, tn), lambda j: (0, j)),
        out_shape=jax.ShapeDtypeStruct((rows, n), F32),
        compiler_params=_params(("arbitrary",)),
        name="adaln",
    )(c_pad, w_ada, b_ada.reshape(1, n))
    return out[:b]


def _inproj_kernel(x_ref, shift_ref, scale_ref, g_ref, wqkv_ref, wz_ref, wxbc_ref, wdt_ref,
                   qg_ref, kg_ref, hmean_ref, q_ref, k_ref, v_ref, z_ref, xbc_ref, dt_ref):
    x = x_ref[...]
    ms = jnp.mean(x * x, axis=-1, keepdims=True)
    h = x * lax.rsqrt(ms + NORM_EPS) * g_ref[...]
    h = h * (1.0 + scale_ref[...]) + shift_ref[...]
    hb = h.astype(BF16)

    hmean = hmean_ref[...]

    def head_norm(t, gain):
        ss = _dot_exact_rhs(t * t, hmean)
        return t * lax.rsqrt(ss + NORM_EPS) * gain

    q = _dot(hb, wqkv_ref[:, 0:ATTN_WIDTH])
    q_ref[...] = head_norm(q, qg_ref[...]) * (HEAD_DIM ** -0.5)
    k = _dot(hb, wqkv_ref[:, ATTN_WIDTH:2 * ATTN_WIDTH])
    k_ref[...] = head_norm(k, kg_ref[...])
    v_ref[...] = _dot(hb, wqkv_ref[:, 2 * ATTN_WIDTH:3 * ATTN_WIDTH])
    for c0 in range(0, SSM_WIDTH, 512):
        z_ref[:, c0:c0 + 512] = _dot(hb, wz_ref[:, c0:c0 + 512])
    for c0 in range(0, CONV_CH, 512):
        xbc_ref[:, c0:c0 + 512] = _dot(hb, wxbc_ref[:, c0:c0 + 512])
    dt_ref[...] = _dot(hb, wdt_ref[...])


def _in_proj(x, shift, scale, gain, w_in, q_gain, k_gain, tm=256):
    b, s, d = x.shape
    t = b * s
    tiles_per_seq = s // tm
    w = w_in.astype(BF16)
    o_z = 3 * ATTN_WIDTH
    o_x = o_z + SSM_WIDTH
    o_dt = o_x + CONV_CH
    w_qkv, w_z, w_xbc = w[:, :o_z], w[:, o_z:o_x], w[:, o_x:o_dt]
    w_dt = jnp.zeros((d, LANES), BF16).at[:, :SSM_HEADS].set(w[:, o_dt:])
    head_of = np.arange(ATTN_WIDTH) // HEAD_DIM
    hmean = jnp.asarray((head_of[:, None] == head_of[None, :]).astype(np.float32) / HEAD_DIM, BF16)
    full = lambda shp: pl.BlockSpec(shp, lambda i: (0,) * len(shp))
    row = lambda n: pl.BlockSpec((tm, n), lambda i: (i, 0))
    per_batch = pl.BlockSpec((None, 1, d), lambda i: (i // tiles_per_seq, 0, 0))
    outs = pl.pallas_call(
        _inproj_kernel,
        grid=(t // tm,),
        in_specs=[row(d), per_batch, per_batch, full((1, d)),
                  full((d, o_z)), full((d, SSM_WIDTH)), full((d, CONV_CH)), full((d, LANES)),
                  full((1, ATTN_WIDTH)), full((1, ATTN_WIDTH)), full((ATTN_WIDTH, ATTN_WIDTH))],
        out_specs=[row(ATTN_WIDTH), row(ATTN_WIDTH), row(ATTN_WIDTH), row(SSM_WIDTH), row(CONV_CH), row(LANES)],
        out_shape=[jax.ShapeDtypeStruct((t, n), F32)
                   for n in (ATTN_WIDTH, ATTN_WIDTH, ATTN_WIDTH, SSM_WIDTH, CONV_CH, LANES)],
        compiler_params=_params(("arbitrary",)),
        name="in_proj",
    )(x.reshape(t, d), shift.reshape(b, 1, d), scale.reshape(b, 1, d), gain.reshape(1, d),
      w_qkv, w_z, w_xbc, w_dt,
      jnp.tile(q_gain, ATTN_HEADS).reshape(1, ATTN_WIDTH), jnp.tile(k_gain, ATTN_HEADS).reshape(1, ATTN_WIDTH), hmean)
    return outs


def _t5_causal_buckets(distance):
    n = np.maximum(distance, 0)
    max_exact = REL_BUCKETS // 2
    large = max_exact + (np.log(np.maximum(n, 1) / max_exact) / math.log(REL_MAX_DISTANCE / max_exact)
                         * (REL_BUCKETS - max_exact)).astype(np.int64)
    large = np.minimum(large, REL_BUCKETS - 1)
    return np.where(n < max_exact, n, large).astype(np.int32)


def _window_bias(rel_bias_table, dilation):
    qi = np.arange(WIN_STEPS)[:, None]
    kj = np.arange(2 * WIN_STEPS)[None, :]
    dist = qi + WIN_STEPS - kj
    band = (dist >= 0) & (dist <= WIN_STEPS)
    onehot = (_t5_causal_buckets(dist * dilation).reshape(-1, 1) == np.arange(REL_BUCKETS)[None, :]).astype(np.float32)
    bias = jnp.dot(rel_bias_table.astype(F32).T, jnp.asarray(onehot).T, precision=lax.Precision.HIGHEST)
    bias = bias.reshape(ATTN_HEADS, WIN_STEPS, 2 * WIN_STEPS)
    return jnp.where(jnp.asarray(band)[None], bias, NEG_BIG)


ATTN_TOKENS = max(w for w, _ in PATTERNS)
ATTN_UNROLL = 16


def _attn_kernel(q_ref, kp_ref, kc_ref, vp_ref, vc_ref, bias_ref, out_ref, kw, vw, o_acc, l_acc):
    tb = ATTN_TOKENS
    first = pl.program_id(2) == 0
    kw[0:tb] = kp_ref[...]
    kw[tb:2 * tb] = kc_ref[...]
    vw[0:tb] = vp_ref[...]
    vw[tb:2 * tb] = vc_ref[...]
    lane = lax.broadcasted_iota(I32, (WIN_STEPS, LANES), 1)
    head0 = lane < HEAD_DIM
    col = lax.broadcasted_iota(I32, (WIN_STEPS, 2 * WIN_STEPS), 1)
    in_prev = col < WIN_STEPS

    for p, (_, d) in enumerate(PATTERNS):
        shift = d.bit_length() - 1
        n_blocks = tb // WIN_STEPS

        def rows(start, n, d=d):
            return pl.ds(start, n, stride=d) if d > 1 else pl.ds(start, n)

        def body(it, carry, p=p, d=d, shift=shift, rows=rows):
            for u in range(ATTN_UNROLL):
                idx = it * ATTN_UNROLL + u
                r = jnp.bitwise_and(idx, d - 1)
                j = jnp.right_shift(idx, shift)
                qs = j * (WIN_STEPS * d) + r
                q = q_ref[rows(qs, WIN_STEPS), :]
                k = kw[rows(tb + qs - WIN_STEPS * d, 2 * WIN_STEPS), :].astype(BF16)
                v = vw[rows(tb + qs - WIN_STEPS * d, 2 * WIN_STEPS), :].astype(BF16)
                no_prev = jnp.logical_and(in_prev, jnp.logical_and(first, j == 0))
                o_h, lse_h = [], []
                for h in range(2):
                    qh = jnp.where(head0 if h == 0 else jnp.logical_not(head0), q, 0.0).astype(BF16)
                    s = _dot_nt(qh, k) + bias_ref[p, h]
                    s = jnp.where(no_prev, NEG_BIG, s)
                    m = jnp.max(s, axis=-1, keepdims=True)
                    e = jnp.exp(s - m)
                    denom = jnp.sum(e, axis=-1, keepdims=True)
                    o_h.append(_dot(e.astype(BF16), v) / denom)
                    lse_h.append(m + jnp.log(denom))
                o_acc[p, rows(qs, WIN_STEPS), :] = jnp.where(head0, o_h[0], o_h[1])
                l_acc[p, rows(qs, WIN_STEPS), :] = jnp.where(head0, lse_h[0], lse_h[1])
            return carry

        lax.fori_loop(0, n_blocks // ATTN_UNROLL, body, 0)

    chunk = 256
    for c0 in range(0, tb, chunk):
        l1, l2, l3 = (l_acc[p, c0:c0 + chunk, :] for p in range(3))
        m = jnp.maximum(jnp.maximum(l1, l2), l3)
        e1, e2, e3 = jnp.exp(l1 - m), jnp.exp(l2 - m), jnp.exp(l3 - m)
        num = e1 * o_acc[0, c0:c0 + chunk, :] + e2 * o_acc[1, c0:c0 + chunk, :] + e3 * o_acc[2, c0:c0 + chunk, :]
        out_ref[c0:c0 + chunk, :] = num / (e1 + e2 + e3)


def _attention(q, k, v, bias):
    b, s, w = q.shape
    tb = ATTN_TOKENS
    pairs = ATTN_HEADS // 2
    cur = pl.BlockSpec((None, tb, LANES), lambda bi, hp, i: (bi, i, hp))
    prev = pl.BlockSpec((None, tb, LANES), lambda bi, hp, i: (bi, jnp.maximum(i - 1, 0), hp))
    return pl.pallas_call(
        _attn_kernel,
        grid=(b, pairs, s // tb),
        in_specs=[cur, prev, cur, prev, cur,
                  pl.BlockSpec((len(PATTERNS), 2, WIN_STEPS, 2 * WIN_STEPS), lambda bi, hp, i: (0, hp, 0, 0))],
        out_specs=cur,
        out_shape=jax.ShapeDtypeStruct((b, s, w), F32),
        scratch_shapes=[pltpu.VMEM((2 * tb, LANES), F32), pltpu.VMEM((2 * tb, LANES), F32),
                        pltpu.VMEM((len(PATTERNS), tb, LANES), F32), pltpu.VMEM((len(PATTERNS), tb, LANES), F32)],
        compiler_params=_params(("arbitrary",) * 3),
        name="attention",
    )(q, k, k, v, v, bias)


SSD_CHUNKS_PER_STEP = 2


def _ssd_kernel(xbc_ref, halo_ref, z_ref, dtraw_ref, convw_ref, convb_ref, dtb_ref, alog_ref, dskip_ref, gain_ref,
                expand_ref, tril_ref, y_ref, state_ref):
    c = pl.program_id(1)

    @pl.when(c == 0)
    def _():
        state_ref[...] = jnp.zeros_like(state_ref)

    halo = jnp.where(c == 0, 0.0, halo_ref[...])
    for sub in range(SSD_CHUNKS_PER_STEP):
        rows = slice(sub * SSM_CHUNK, (sub + 1) * SSM_CHUNK)
        x = xbc_ref[rows, :]
        w = convw_ref[...]
        acc = x * w[SSM_CONV - 1:SSM_CONV, :] + convb_ref[...]
        row8 = lax.broadcasted_iota(I32, (SUBLANES, CONV_CH), 0)
        for shift in range(1, SSM_CONV):
            xs = pltpu.roll(x, shift, axis=0)
            hs = pltpu.roll(halo, shift, axis=0)
            head = jnp.where(row8 < shift, hs, xs[0:SUBLANES])
            xs = jnp.concatenate([head, xs[SUBLANES:]], axis=0)
            acc = acc + xs * w[SSM_CONV - 1 - shift:SSM_CONV - shift, :]
        act = _silu(acc)
        x_s = act[:, :SSM_WIDTH]
        bc0 = SSM_WIDTH
        cc0 = SSM_WIDTH + SSM_GROUPS * SSM_STATE

        t = dtraw_ref[rows, :] + dtb_ref[...]
        dt = jnp.maximum(t, 0.0) + jnp.log(1.0 + jnp.exp(-jnp.abs(t)))
        a = dt * (-jnp.exp(alog_ref[...]))
        a_cs = _dot_exact_lhs(tril_ref[...], a)
        a_cs_t = a_cs.T
        a_last = a_cs[SSM_CHUNK - 1:SSM_CHUNK, :]
        expand = expand_ref[...]
        dt_e = _dot_exact_rhs(dt, expand)
        ea_e = _dot_exact_rhs(jnp.exp(a_cs), expand)
        dte_e = _dot_exact_rhs(jnp.exp(a_last - a_cs), expand)
        xdt = x_s * dt_e
        xw = (xdt * dte_e).astype(BF16)
        xdt_b = xdt.astype(BF16)

        li = lax.broadcasted_iota(I32, (SSM_CHUNK, SSM_CHUNK), 0)
        si = lax.broadcasted_iota(I32, (SSM_CHUNK, SSM_CHUNK), 1)
        causal = li >= si

        ys = []
        for g in range(SSM_GROUPS):
            gs = slice(g * GROUP_WIDTH, (g + 1) * GROUP_WIDTH)
            b_g = act[:, bc0 + g * SSM_STATE:bc0 + (g + 1) * SSM_STATE]
            c_g = act[:, cc0 + g * SSM_STATE:cc0 + (g + 1) * SSM_STATE].astype(BF16)
            cb = _dot_nt(c_g, b_g.astype(BF16))
            state = state_ref[g]
            y_off = _dot(c_g, state.astype(BF16)) * ea_e[:, gs]
            parts = []
            for j in range(HEADS_PER_GROUP):
                hh = g * HEADS_PER_GROUP + j
                seg = a_cs[:, hh:hh + 1] - a_cs_t[hh:hh + 1, :]
                decay = jnp.exp(jnp.where(causal, seg, NEG_BIG))
                m = (cb * decay).astype(BF16)
                parts.append(_dot(m, xdt_b[:, hh * SSM_HEAD_DIM:(hh + 1) * SSM_HEAD_DIM]))
            ys.append(jnp.concatenate(parts, axis=1) + y_off)
            state_ref[g] = state * ea_e[SSM_CHUNK - 1:SSM_CHUNK, gs] + _dot(b_g.T.astype(BF16), xw[:, gs])
        y = jnp.concatenate(ys, axis=1) + dskip_ref[...] * x_s
        y = y * _silu(z_ref[rows, :])
        gain = gain_ref[...]
        for g in range(SSM_GROUPS):
            gs = slice(g * GROUP_WIDTH, (g + 1) * GROUP_WIDTH)
            yg = y[:, gs]
            ms = jnp.mean(yg * yg, axis=-1, keepdims=True)
            y_ref[rows, gs] = yg * lax.rsqrt(ms + NORM_EPS) * gain[:, gs]
        halo = x[SSM_CHUNK - SUBLANES:, :]


def _ssd(xbc, z, dt_raw, conv_w, conv_b, dt_bias, a_log, d_skip, norm_gain, b, s):
    t = b * s
    step_rows = SSD_CHUNKS_PER_STEP * SSM_CHUNK
    nc = s // step_rows
    pad_heads = lambda v: jnp.zeros((1, LANES), F32).at[0, :SSM_HEADS].set(v)
    head_of_lane = np.arange(SSM_WIDTH) // SSM_HEAD_DIM
    expand = jnp.asarray((np.arange(LANES)[:, None] == head_of_lane[None, :]).astype(np.float32), BF16)
    tril = jnp.asarray(np.tril(np.ones((SSM_CHUNK, SSM_CHUNK), np.float32)), BF16)
    halo_blocks = step_rows // SUBLANES
    chunk = lambda n: pl.BlockSpec((step_rows, n), lambda bi, c: (bi * nc + c, 0))
    full = lambda shp: pl.BlockSpec(shp, lambda bi, c: (0,) * len(shp))
    halo = pl.BlockSpec((SUBLANES, CONV_CH), lambda bi, c: (jnp.maximum((bi * nc + c) * halo_blocks - 1, 0), 0))
    return pl.pallas_call(
        _ssd_kernel,
        grid=(b, nc),
        in_specs=[chunk(CONV_CH), halo, chunk(SSM_WIDTH), chunk(LANES),
                  full((SSM_CONV, CONV_CH)), full((1, CONV_CH)), full((1, LANES)), full((1, LANES)),
                  full((1, SSM_WIDTH)), full((1, SSM_WIDTH)), full((LANES, SSM_WIDTH)), full((SSM_CHUNK, SSM_CHUNK))],
        out_specs=chunk(SSM_WIDTH),
        out_shape=jax.ShapeDtypeStruct((t, SSM_WIDTH), F32),
        scratch_shapes=[pltpu.VMEM((SSM_GROUPS, SSM_STATE, GROUP_WIDTH), F32)],
        compiler_params=_params(("arbitrary", "arbitrary")),
        name="ssd",
    )(xbc, xbc, z, dt_raw, conv_w, conv_b.reshape(1, CONV_CH), pad_heads(dt_bias), pad_heads(a_log),
      jnp.repeat(d_skip, SSM_HEAD_DIM).reshape(1, SSM_WIDTH), norm_gain.reshape(1, SSM_WIDTH), expand, tril)


WORD = jnp.int32
TOKEN_ROWS = D_MODEL // (2 * LANES)
HIGH_HALF = np.int32(-65536)


def _to_token_tiles(ref, x):
    n = x.shape[0]
    for c in range(TOKEN_ROWS):
        lo = lax.bitcast_convert_type(x[:, c * LANES:(c + 1) * LANES].astype(BF16).astype(F32), WORD)
        hi = lax.bitcast_convert_type(x[:, (c + TOKEN_ROWS) * LANES:(c + TOKEN_ROWS + 1) * LANES]
                                      .astype(BF16).astype(F32), WORD)
        ref[pl.ds(c, n, stride=TOKEN_ROWS), :] = jnp.bitwise_or(lax.shift_right_logical(lo, 16),
                                                                 jnp.bitwise_and(hi, HIGH_HALF))


def _from_token_tiles(ref, n, token0=0):
    lows, highs = [], []
    for c in range(TOKEN_ROWS):
        word = ref[pl.ds(token0 * TOKEN_ROWS + c, n, stride=TOKEN_ROWS), :]
        lows.append(lax.bitcast_convert_type(jnp.left_shift(word, 16), F32))
        highs.append(lax.bitcast_convert_type(jnp.bitwise_and(word, HIGH_HALF), F32))
    return jnp.concatenate(lows + highs, axis=1)


def _outproj_kernel(attn_ref, ssm_ref, x_ref, gate_ref, shift_ref, scale_ref, g_ref, wa_ref, ws_ref, x1_ref, h2_ref,
                    h2t_ref):
    mixed = _dot(attn_ref[...].astype(BF16), wa_ref[...]) + _dot(ssm_ref[...].astype(BF16), ws_ref[...])
    x1 = x_ref[...] + gate_ref[...] * mixed
    x1_ref[...] = x1
    ms = jnp.mean(x1 * x1, axis=-1, keepdims=True)
    h = x1 * lax.rsqrt(ms + NORM_EPS) * g_ref[...]
    h2 = h * (1.0 + scale_ref[...]) + shift_ref[...]
    h2_ref[...] = h2
    for c in range(TOKEN_ROWS):
        lo = lax.bitcast_convert_type(h2[:, c * LANES:(c + 1) * LANES].astype(BF16).astype(F32), WORD)
        hi = lax.bitcast_convert_type(h2[:, (c + TOKEN_ROWS) * LANES:(c + TOKEN_ROWS + 1) * LANES]
                                      .astype(BF16).astype(F32), WORD)
        h2t_ref[c] = jnp.bitwise_or(lax.shift_right_logical(lo, 16), jnp.bitwise_and(hi, HIGH_HALF))


def _out_proj(attn, ssm, x, gate, shift, scale, gain, w_out, b, s, tm=256):
    t = b * s
    d = D_MODEL
    tiles_per_seq = s // tm
    w = w_out.astype(BF16)
    row = lambda n: pl.BlockSpec((tm, n), lambda i: (i, 0))
    full = lambda shp: pl.BlockSpec(shp, lambda i: (0,) * len(shp))
    per_batch = pl.BlockSpec((None, 1, d), lambda i: (i // tiles_per_seq, 0, 0))
    return pl.pallas_call(
        _outproj_kernel,
        grid=(t // tm,),
        in_specs=[row(ATTN_WIDTH), row(SSM_WIDTH), row(d), per_batch, per_batch, per_batch, full((1, d)),
                  full((ATTN_WIDTH, d)), full((SSM_WIDTH, d))],
        out_specs=[row(d), row(d), pl.BlockSpec((TOKEN_ROWS, tm, LANES), lambda i: (0, i, 0))],
        out_shape=[jax.ShapeDtypeStruct((t, d), F32)] * 2 + [jax.ShapeDtypeStruct((TOKEN_ROWS, t, LANES), WORD)],
        compiler_params=_params(("arbitrary",)),
        name="out_proj",
    )(attn.reshape(t, ATTN_WIDTH), ssm, x.reshape(t, d),
      gate.reshape(b, 1, d), shift.reshape(b, 1, d), scale.reshape(b, 1, d), gain.reshape(1, d),
      w[:ATTN_WIDTH], w[ATTN_WIDTH:])


def _mixer_sublayer(x, mod, norm_mix_gain, w_in, q_norm_gain, k_norm_gain, rel_bias_table, conv_w, conv_b, dt_bias,
                    a_log, d_skip, ssm_norm_gain, w_out, norm_ffn_gain):
    b, s, d = x.shape
    shift_m, scale_m, gate_m, shift_f, scale_f, _ = jnp.split(mod, 6, axis=-1)
    q, k, v, z, xbc, dt_raw = _in_proj(x, shift_m, scale_m, norm_mix_gain, w_in, q_norm_gain, k_norm_gain)
    bias = jnp.stack([_window_bias(rel_bias_table, dilation) for _, dilation in PATTERNS])
    attn = _attention(q.reshape(b, s, ATTN_WIDTH), k.reshape(b, s, ATTN_WIDTH), v.reshape(b, s, ATTN_WIDTH), bias)
    ssm = _ssd(xbc, z, dt_raw, conv_w, conv_b, dt_bias, a_log, d_skip, ssm_norm_gain, b, s)
    return _out_proj(attn, ssm, x, gate_m, shift_f, scale_f, norm_ffn_gain, w_out, b, s)


def _first_argmax(v, iota, limit):
    m = jnp.max(v, axis=0, keepdims=True)
    idx = jnp.min(jnp.where(v == m, iota, limit), axis=0, keepdims=True)
    return m, idx


def _router_kernel(h_ref, wt_ref, bias_ref, upper_ref, eidx_ref, rank_ref, gate_ref, counts_ref, carry_ref):
    @pl.when(pl.program_id(0) == 0)
    def _():
        carry_ref[...] = jnp.zeros_like(carry_ref)

    tm = h_ref.shape[0]
    h = h_ref[...]
    wt = wt_ref[...]
    h_hi = h.astype(BF16)
    h_lo = (h - h_hi.astype(F32)).astype(BF16)
    w_hi = wt.astype(BF16)
    w_lo = (wt - w_hi.astype(F32)).astype(BF16)
    logits = _dot_nt(w_hi, h_hi) + _dot_nt(w_hi, h_lo) + _dot_nt(w_lo, h_hi)
    scores = _sigmoid(logits)
    choice = scores + bias_ref[...]
    neg_inf = -jnp.inf

    iota_g = lax.broadcasted_iota(I32, (EXPERTS_PER_GROUP, tm), 0).astype(F32)
    group_rows = []
    for g in range(N_EXPERT_GROUPS):
        v = choice[g * EXPERTS_PER_GROUP:(g + 1) * EXPERTS_PER_GROUP]
        m1, i1 = _first_argmax(v, iota_g, float(EXPERTS_PER_GROUP))
        m2 = jnp.max(jnp.where(iota_g == i1, neg_inf, v), axis=0, keepdims=True)
        group_rows.append(m1 + m2)
    group_scores = jnp.concatenate(group_rows, axis=0)

    iota_n = lax.broadcasted_iota(I32, (N_EXPERT_GROUPS, tm), 0).astype(F32)
    chosen = jnp.zeros((N_EXPERT_GROUPS, tm), F32)
    for _ in range(TOPK_GROUPS):
        _, gi = _first_argmax(group_scores, iota_n, float(N_EXPERT_GROUPS))
        hit = iota_n == gi
        chosen = jnp.where(hit, 1.0, chosen)
        group_scores = jnp.where(hit, neg_inf, group_scores)

    masked = jnp.concatenate(
        [jnp.where(chosen[g:g + 1] > 0.0, choice[g * EXPERTS_PER_GROUP:(g + 1) * EXPERTS_PER_GROUP], neg_inf)
         for g in range(N_EXPERT_GROUPS)], axis=0)

    iota_e = lax.broadcasted_iota(I32, (N_EXPERTS, tm), 0).astype(F32)
    picked, gates = [], []
    onehot = jnp.zeros((N_EXPERTS, tm), F32)
    for _ in range(TOP_K):
        _, ei = _first_argmax(masked, iota_e, float(N_EXPERTS))
        hit = iota_e == ei
        gates.append(jnp.sum(jnp.where(hit, scores, 0.0), axis=0, keepdims=True))
        masked = jnp.where(hit, neg_inf, masked)
        onehot = jnp.where(hit, 1.0, onehot)
        picked.append(ei)
    gate_sum = gates[0]
    for gk in gates[1:]:
        gate_sum = gate_sum + gk

    base = _dot(onehot.astype(BF16), upper_ref[...]) + carry_ref[...]
    ranks = [jnp.sum(jnp.where(iota_e == ei, base, 0.0), axis=0, keepdims=True) for ei in picked]
    carry_ref[...] = carry_ref[...] + jnp.sum(onehot, axis=1, keepdims=True)

    eidx_ref[...] = jnp.concatenate(picked, axis=0).astype(I32)
    rank_ref[...] = jnp.concatenate(ranks, axis=0).astype(I32)
    gate_ref[...] = jnp.concatenate([gk / gate_sum * ROUTED_SCALE for gk in gates], axis=0)
    counts_ref[...] = carry_ref[...].astype(I32)


def _router(h2, w_router, router_bias, tm=256):
    t, d = h2.shape
    upper = jnp.asarray(np.triu(np.ones((tm, tm), np.float32), 1), BF16)
    tok = pl.BlockSpec((TOP_K, tm), lambda i: (0, i))
    full = lambda shp: pl.BlockSpec(shp, lambda i: (0,) * len(shp))
    return pl.pallas_call(
        _router_kernel,
        grid=(t // tm,),
        in_specs=[pl.BlockSpec((tm, d), lambda i: (i, 0)), full((N_EXPERTS, d)), full((N_EXPERTS, 1)), full((tm, tm))],
        out_specs=[tok, tok, tok, full((N_EXPERTS, 1))],
        out_shape=[jax.ShapeDtypeStruct((TOP_K, t), I32), jax.ShapeDtypeStruct((TOP_K, t), I32),
                   jax.ShapeDtypeStruct((TOP_K, t), F32), jax.ShapeDtypeStruct((N_EXPERTS, 1), I32)],
        scratch_shapes=[pltpu.VMEM((N_EXPERTS, 1), F32)],
        compiler_params=_params(("arbitrary",)),
        name="router",
    )(h2, w_router.T, router_bias.reshape(N_EXPERTS, 1), upper)


def _positions_kernel(counts_ref, lower_ref, eidx_ref, rank_ref, pos_ref):
    tm = eidx_ref.shape[1]
    counts = jnp.broadcast_to(counts_ref[...].astype(F32), (N_EXPERTS, LANES))
    offsets = _dot_exact_lhs(lower_ref[...], counts)[:, 0:1]
    iota_e = lax.broadcasted_iota(I32, (N_EXPERTS, tm), 0).astype(F32)
    e = eidx_ref[...].astype(F32)
    rows = [jnp.sum(jnp.where(iota_e == e[k:k + 1], offsets, 0.0), axis=0, keepdims=True) for k in range(TOP_K)]
    pos_ref[...] = jnp.concatenate(rows, axis=0).astype(I32) + rank_ref[...]


def _positions(counts, eidx, rank, tm):
    t = eidx.shape[1]
    lower = jnp.asarray(np.tril(np.ones((N_EXPERTS, N_EXPERTS), np.float32), -1), BF16)
    tok = pl.BlockSpec((TOP_K, tm), lambda i: (0, i))
    return pl.pallas_call(
        _positions_kernel,
        grid=(t // tm,),
        in_specs=[pl.BlockSpec((N_EXPERTS, 1), lambda i: (0, 0)), pl.BlockSpec((N_EXPERTS, N_EXPERTS), lambda i: (0, 0)),
                  tok, tok],
        out_specs=tok,
        out_shape=jax.ShapeDtypeStruct((TOP_K, t), I32),
        compiler_params=_params(("arbitrary",)),
        name="positions",
    )(counts, lower, eidx, rank)


EXPERT_BLOCK = 512
TAIL_UNIT = 128
TAIL_PIECES = tuple(1 << i for i in reversed(range(EXPERT_BLOCK.bit_length() - 1)))
X_SLOTS = 4
Y_SLOTS = 4


def _experts_kernel(start_ref, count_ref, nxt_ref, slot_ref, first_ref, blk0_ref, full0_ref, ptail_ref, ltail_ref,
                    blktok_ref, nblocks_ref, xs_hbm, wg_hbm, wu_hbm, wd_hbm, ys_hbm,
                    wg_buf, wu_buf, wd_buf, wg_bf, wu_bf, wd_bf, xbuf, ybuf, ytail, wsem, xsem, ysem, tsem):
    e = pl.program_id(0)
    last_step = e == pl.num_programs(0) - 1
    start, count = start_ref[e], count_ref[e]
    n_full = jnp.right_shift(count, EXPERT_BLOCK.bit_length() - 1)
    tail = jnp.bitwise_and(count, EXPERT_BLOCK - 1)
    n_blk = n_full + (tail > 0).astype(I32)
    blk0, full0 = blk0_ref[e], full0_ref[e]
    slot, nxt = slot_ref[e], nxt_ref[e]

    def token_rows(token, n):
        return pl.ds(pl.multiple_of(token * TOKEN_ROWS, TOKEN_ROWS), n * TOKEN_ROWS)

    def fetch(ex, s):
        return (pltpu.make_async_copy(wg_hbm.at[ex], wg_buf.at[s], wsem.at[s, 0]),
                pltpu.make_async_copy(wu_hbm.at[ex], wu_buf.at[s], wsem.at[s, 1]),
                pltpu.make_async_copy(wd_hbm.at[ex], wd_buf.at[s], wsem.at[s, 2]))

    def x_copy(token, s):
        return pltpu.make_async_copy(xs_hbm.at[token_rows(token, EXPERT_BLOCK)], xbuf.at[s], xsem.at[s])

    def y_copy(token, s):
        return pltpu.make_async_copy(ybuf.at[s], ys_hbm.at[token_rows(token, EXPERT_BLOCK)], ysem.at[s])

    def tail_copies(token, length):
        out = []
        for piece in TAIL_PIECES:
            bigger = (EXPERT_BLOCK - 1) & ~(2 * piece - 1)
            done = jnp.bitwise_and(length, bigger)
            cp = pltpu.make_async_copy(ytail.at[token_rows(done, piece)], ys_hbm.at[token_rows(token + done, piece)],
                                       tsem)
            out.append((jnp.bitwise_and(length, piece) != 0, cp))
        return out

    def block(s, rows=EXPERT_BLOCK):
        x = _from_token_tiles(xbuf.at[s], rows).astype(BF16)
        g = _dot(x, wg_bf[...])
        u = _dot(x, wu_bf[...])
        return _dot((_silu(g) * u).astype(BF16), wd_bf[...])

    @pl.when(count > 0)
    def _():
        @pl.when(first_ref[e] == 1)
        def _():
            for g in range(X_SLOTS - 1):
                @pl.when(g < nblocks_ref[0])
                def _(g=g):
                    x_copy(blktok_ref[g], g).start()
            for cp in fetch(e, slot):
                cp.start()

        for cp in fetch(e, slot):
            cp.wait()

        @pl.when(nxt >= 0)
        def _():
            for cp in fetch(nxt, 1 - slot):
                cp.start()

        wg_bf[...] = wg_buf[slot].astype(BF16)
        wu_bf[...] = wu_buf[slot].astype(BF16)
        wd_bf[...] = wd_buf[slot].astype(BF16)

        def take_x(i):
            g = blk0 + i
            xs_slot = jnp.bitwise_and(g, X_SLOTS - 1)
            x_copy(start, xs_slot).wait()
            ahead = g + (X_SLOTS - 1)

            @pl.when(ahead < nblocks_ref[0])
            def _():
                x_copy(blktok_ref[ahead], jnp.bitwise_and(ahead, X_SLOTS - 1)).start()

            return xs_slot

        def full_block(i, carry):
            y = block(take_x(i))
            j = full0 + i
            ys_slot = jnp.bitwise_and(j, Y_SLOTS - 1)

            @pl.when(j >= Y_SLOTS)
            def _():
                y_copy(start, ys_slot).wait()

            _to_token_tiles(ybuf.at[ys_slot], y)
            y_copy(start + i * EXPERT_BLOCK, ys_slot).start()
            return carry

        lax.fori_loop(0, n_full, full_block, 0)

        @pl.when(tail > 0)
        def _():
            xs_slot = take_x(n_full)
            for pred, cp in tail_copies(start, ptail_ref[e]):
                @pl.when(pred)
                def _(cp=cp):
                    cp.wait()
            units = jnp.right_shift(tail + (TAIL_UNIT - 1), TAIL_UNIT.bit_length() - 1)
            for u in range(1, EXPERT_BLOCK // TAIL_UNIT + 1):
                @pl.when(units == u)
                def _(u=u):
                    rows = u * TAIL_UNIT
                    _to_token_tiles(ytail.at[pl.ds(0, rows * TOKEN_ROWS)], block(xs_slot, rows))
            for pred, cp in tail_copies(start + n_full * EXPERT_BLOCK, tail):
                @pl.when(pred)
                def _(cp=cp):
                    cp.start()

    @pl.when(last_step)
    def _():
        total_full = full0 + n_full
        for back in range(1, Y_SLOTS + 1):
            @pl.when(total_full >= back)
            def _(back=back):
                y_copy(0, jnp.bitwise_and(total_full - back, Y_SLOTS - 1)).wait()
        for pred, cp in tail_copies(0, ltail_ref[0]):
            @pl.when(pred)
            def _(cp=cp):
                cp.wait()


def _max_expert_blocks(n_rows):
    return n_rows // EXPERT_BLOCK + N_EXPERTS


def _expert_metadata(counts, n_rows):
    ids = jnp.arange(N_EXPERTS, dtype=I32)
    used = counts > 0
    starts = jnp.cumsum(counts) - counts
    n_blk = (counts + EXPERT_BLOCK - 1) // EXPERT_BLOCK
    n_full = counts // EXPERT_BLOCK
    tail = counts % EXPERT_BLOCK
    blk0 = jnp.cumsum(n_blk) - n_blk
    full0 = jnp.cumsum(n_full) - n_full
    next_used = lax.cummin(jnp.where(used, ids, N_EXPERTS), reverse=True)
    next_after = jnp.concatenate([next_used[1:], jnp.full((1,), N_EXPERTS, I32)])
    nxt = jnp.where(next_after < N_EXPERTS, next_after, -1)
    ordinal = jnp.cumsum(used.astype(I32)) - 1
    slot = ordinal % 2
    first = jnp.logical_and(used, ordinal == 0)
    latest = lax.cummax(jnp.where(tail > 0, ids, -1))
    before = jnp.concatenate([jnp.full((1,), -1, I32), latest[:-1]])
    pick = lambda index, values: jnp.sum(jnp.where(index[:, None] == ids[None, :], values[None, :], 0), axis=1)
    ptail = pick(before, tail)
    ltail = pick(latest[-1:], tail)
    block_ends = jnp.cumsum(n_blk)
    g = jnp.arange(_max_expert_blocks(n_rows), dtype=I32)
    eg = jnp.sum((g[:, None] >= block_ends[None, :]).astype(I32), axis=1)
    blktok = g * EXPERT_BLOCK + pick(eg, starts - blk0 * EXPERT_BLOCK)
    return tuple(v.astype(I32) for v in (starts, counts, nxt, slot, first, blk0, full0, ptail, ltail, blktok,
                                         block_ends[-1:]))


def _experts(xs, counts, w_gate, w_up, w_down):
    d = D_MODEL
    meta = _expert_metadata(counts, xs.shape[0] // TOKEN_ROWS - EXPERT_BLOCK)
    hbm = pl.BlockSpec(memory_space=pl.ANY)
    blk = (EXPERT_BLOCK * TOKEN_ROWS, LANES)
    grid_spec = pltpu.PrefetchScalarGridSpec(
        num_scalar_prefetch=len(meta),
        grid=(N_EXPERTS,),
        in_specs=[hbm, hbm, hbm, hbm],
        out_specs=hbm,
        scratch_shapes=[pltpu.VMEM((2, d, EXPERT_FF), F32), pltpu.VMEM((2, d, EXPERT_FF), F32),
                        pltpu.VMEM((2, EXPERT_FF, d), F32),
                        pltpu.VMEM((d, EXPERT_FF), BF16), pltpu.VMEM((d, EXPERT_FF), BF16),
                        pltpu.VMEM((EXPERT_FF, d), BF16),
                        pltpu.VMEM((X_SLOTS,) + blk, WORD), pltpu.VMEM((Y_SLOTS,) + blk, WORD), pltpu.VMEM(blk, WORD),
                        pltpu.SemaphoreType.DMA((2, 3)), pltpu.SemaphoreType.DMA((X_SLOTS,)),
                        pltpu.SemaphoreType.DMA((Y_SLOTS,)), pltpu.SemaphoreType.DMA(())],
    )
    return pl.pallas_call(
        _experts_kernel,
        grid_spec=grid_spec,
        out_shape=jax.ShapeDtypeStruct(xs.shape, WORD),
        compiler_params=_params(("arbitrary",)),
        name="experts",
    )(*meta, xs, w_gate, w_up, w_down)


SC_CORES = 2
SC_SUBCORES = 16
SC_WORKERS = SC_CORES * SC_SUBCORES
SC_CHUNK = 128
SC_RING = 4


def _sc_scatter_rows(planes, idx, n_out_rows):
    n_planes, t, _ = planes.shape
    ranges = SC_WORKERS // n_planes
    n_chunks = t // ranges // SC_CHUNK
    idx = idx.reshape(TOP_K, n_planes, ranges, n_chunks, SC_CHUNK)
    mesh = plsc.VectorSubcoreMesh(core_axis_name="c", subcore_axis_name="s", num_cores=SC_CORES,
                                  num_subcores=SC_SUBCORES)

    def body(planes_hbm, idx_hbm, out_hbm, idx_v, rows_v, lsem, ssem):
        wid = lax.axis_index("s") * SC_CORES + lax.axis_index("c")
        plane = wid % n_planes
        token0 = (wid // n_planes) * (n_chunks * SC_CHUNK)
        for k in range(TOP_K):
            pltpu.sync_copy(idx_hbm.at[k, plane, wid // n_planes], idx_v.at[k])

        def load(slot, c):
            return pltpu.make_async_copy(planes_hbm.at[plane, pl.ds(token0 + c * SC_CHUNK, SC_CHUNK)], rows_v.at[slot],
                                         lsem.at[slot])

        def scatter(slot, c, k):
            return pltpu.make_async_copy(rows_v.at[slot], out_hbm.at[idx_v.at[k, c]], ssem.at[slot])

        @pl.loop(0, n_chunks, step=SC_RING)
        def _(g):
            for slot in range(SC_RING):
                @pl.when(g > 0)
                def _(slot=slot):
                    for k in range(TOP_K):
                        scatter(slot, 0, k).wait()
                load(slot, g + slot).start()
            for slot in range(SC_RING):
                load(slot, g + slot).wait()
                for k in range(TOP_K):
                    scatter(slot, g + slot, k).start()

        for slot in range(SC_RING):
            for k in range(TOP_K):
                scatter(slot, 0, k).wait()

    return pl.kernel(
        body, mesh=mesh,
        out_type=jax.ShapeDtypeStruct((n_out_rows, LANES), planes.dtype),
        scratch_types=[pltpu.VMEM((TOP_K, n_chunks, SC_CHUNK), I32), pltpu.VMEM((SC_RING, SC_CHUNK, LANES), planes.dtype),
                       pltpu.SemaphoreType.DMA((SC_RING,)), pltpu.SemaphoreType.DMA((SC_RING,))],
        name="sc_scatter",
    )(planes, idx)


def _sc_gather_rows(table, idx):
    n_workers, n_chunks, chunk = idx.shape
    rows_per_worker = n_chunks * chunk
    mesh = plsc.VectorSubcoreMesh(core_axis_name="c", subcore_axis_name="s", num_cores=SC_CORES,
                                  num_subcores=SC_SUBCORES)

    def body(table_hbm, idx_hbm, out_hbm, idx_v, rows_v, gsem, wsem):
        wid = lax.axis_index("s") * SC_CORES + lax.axis_index("c")
        base = wid * rows_per_worker
        pltpu.sync_copy(idx_hbm.at[wid], idx_v)

        def write(slot, c):
            return pltpu.make_async_copy(rows_v.at[slot], out_hbm.at[pl.ds(base + c * chunk, chunk)], wsem.at[slot])

        def gather(slot, c):
            return pltpu.make_async_copy(table_hbm.at[idx_v.at[c]], rows_v.at[slot], gsem.at[slot])

        @pl.loop(0, n_chunks, step=SC_RING)
        def _(g):
            for slot in range(SC_RING):
                @pl.when(g > 0)
                def _(slot=slot):
                    write(slot, 0).wait()
                gather(slot, g + slot).start()
            for slot in range(SC_RING):
                gather(slot, g + slot).wait()
                write(slot, g + slot).start()

        for slot in range(SC_RING):
            write(slot, 0).wait()

    return pl.kernel(
        body, mesh=mesh,
        out_type=jax.ShapeDtypeStruct((n_workers * rows_per_worker, LANES), table.dtype),
        scratch_types=[pltpu.VMEM((n_chunks, chunk), I32), pltpu.VMEM((SC_RING, chunk, LANES), table.dtype),
                       pltpu.SemaphoreType.DMA((SC_RING,)), pltpu.SemaphoreType.DMA((SC_RING,))],
        name="sc_gather",
    )(table, idx)


def _combine_kernel(rows_ref, gates_ref, h_ref, x1_ref, gatef_ref, wg_ref, wu_ref, wd_ref, out_ref):
    hb = h_ref[...].astype(BF16)
    shared = _dot((_silu(_dot(hb, wg_ref[...])) * _dot(hb, wu_ref[...])).astype(BF16), wd_ref[...])
    gates = gates_ref[...]

    def expert_rows(k):
        words = [rows_ref[k * TOKEN_ROWS + c] for c in range(TOKEN_ROWS)]
        lows = [lax.bitcast_convert_type(jnp.left_shift(w, 16), F32) for w in words]
        highs = [lax.bitcast_convert_type(jnp.bitwise_and(w, HIGH_HALF), F32) for w in words]
        return jnp.concatenate(lows + highs, axis=1)

    routed = expert_rows(0) * gates[:, 0:1]
    for k in range(1, TOP_K):
        routed = routed + expert_rows(k) * gates[:, k:k + 1]
    out_ref[...] = x1_ref[...] + gatef_ref[...] * (shared + routed)


def _combine(gathered, gates_t, h2, x1, gate_f, w_gate_s, w_up_s, w_down_s, b, s, tm):
    t, d = h2.shape
    tiles_per_seq = s // tm
    row = lambda n: pl.BlockSpec((tm, n), lambda i: (i, 0))
    full = lambda shp: pl.BlockSpec(shp, lambda i: (0,) * len(shp))
    return pl.pallas_call(
        _combine_kernel,
        grid=(t // tm,),
        in_specs=[pl.BlockSpec((TOP_K * TOKEN_ROWS, tm, LANES), lambda i: (0, i, 0)),
                  row(TOP_K), row(d), row(d),
                  pl.BlockSpec((None, 1, d), lambda i: (i // tiles_per_seq, 0, 0)),
                  full((d, EXPERT_FF)), full((d, EXPERT_FF)), full((EXPERT_FF, d))],
        out_specs=row(d),
        out_shape=jax.ShapeDtypeStruct((t, d), F32),
        compiler_params=_params(("arbitrary",)),
        name="combine",
    )(gathered, gates_t, h2, x1, gate_f.reshape(b, 1, d),
      w_gate_s.astype(BF16), w_up_s.astype(BF16), w_down_s.astype(BF16))


def _moe_sublayer(x1, h2, h2t, gate_f, w_router, router_bias, w_gate, w_up, w_down, w_gate_s, w_up_s, w_down_s, b, s,
                  tm=256):
    t = b * s
    eidx, rank, gates, counts = _router(h2, w_router, router_bias)
    pos = _positions(counts, eidx, rank, tm)
    idx = pos[:, None, :] * TOKEN_ROWS + jnp.arange(TOKEN_ROWS, dtype=I32)[None, :, None]
    xs = _sc_scatter_rows(h2t, idx, (t * TOP_K + EXPERT_BLOCK) * TOKEN_ROWS)
    ys = _experts(xs, counts[:, 0], w_gate, w_up, w_down)
    gathered = _sc_gather_rows(ys, idx.reshape(SC_WORKERS, -1, SC_CHUNK)).reshape(TOP_K * TOKEN_ROWS, t, LANES)
    return _combine(gathered, gates.T, h2, x1, gate_f, w_gate_s, w_up_s, w_down_s, b, s, tm)


def kernel(x, c, w_ada, b_ada, norm_mix_gain, w_in, q_norm_gain, k_norm_gain, rel_bias_table, conv_w, conv_b, dt_bias,
           a_log, d_skip, ssm_norm_gain, w_out, norm_ffn_gain, w_router, router_bias, w_gate_experts, w_up_experts,
           w_down_experts, w_gate_shared, w_up_shared, w_down_shared):
    b, s, d = x.shape
    for layer in range(w_ada.shape[0]):
        mod = _adaln(c, w_ada[layer], b_ada[layer])
        x1, h2, h2t = _mixer_sublayer(x, mod, norm_mix_gain[layer], w_in[layer], q_norm_gain[layer], k_norm_gain[layer],
                                 rel_bias_table, conv_w[layer], conv_b[layer], dt_bias[layer], a_log[layer],
                                 d_skip[layer], ssm_norm_gain[layer], w_out[layer], norm_ffn_gain[layer])
        gate_f = mod[:, 5 * d:]
        out = _moe_sublayer(x1, h2, h2t, gate_f, w_router[layer], router_bias[layer], w_gate_experts[layer],
                            w_up_experts[layer], w_down_experts[layer], w_gate_shared[layer], w_up_shared[layer],
                            w_down_shared[layer], b, s)
        x = out.reshape(b, s, d)
    return x
```

```python
import functools
import math

import numpy as np
import jax
import jax.numpy as jnp
from jax import lax
from jax.experimental import pallas as pl
from jax.experimental.pallas import tpu as pltpu
from jax.experimental.pallas import tpu_sc as plsc

F32 = jnp.float32
BF16 = jnp.bfloat16
I32 = jnp.int32

D_MODEL = 1024
ATTN_HEADS = 8
HEAD_DIM = 64
ATTN_WIDTH = ATTN_HEADS * HEAD_DIM
PATTERNS = ((128, 1), (512, 4), (2048, 16))
WIN_STEPS = 128
REL_BUCKETS = 32
REL_MAX_DISTANCE = 2048
SSM_HEADS = 24
SSM_HEAD_DIM = 64
SSM_WIDTH = SSM_HEADS * SSM_HEAD_DIM
SSM_GROUPS = 4
HEADS_PER_GROUP = SSM_HEADS // SSM_GROUPS
GROUP_WIDTH = SSM_WIDTH // SSM_GROUPS
SSM_STATE = 128
SSM_CONV = 4
SSM_CHUNK = 128
CONV_CH = SSM_WIDTH + 2 * SSM_GROUPS * SSM_STATE
N_EXPERTS = 256
TOP_K = 8
N_EXPERT_GROUPS = 8
EXPERTS_PER_GROUP = N_EXPERTS // N_EXPERT_GROUPS
TOPK_GROUPS = 4
EXPERT_FF = 256
ROUTED_SCALE = 2.5
NORM_EPS = 1e-6

LANES = 128
SUBLANES = 8
NEG_BIG = -1e30
VMEM_LIMIT = 56 * 1024 * 1024


def _params(sem, vmem=VMEM_LIMIT):
    return pltpu.CompilerParams(dimension_semantics=sem, vmem_limit_bytes=vmem)


def _sigmoid(x):
    return 1.0 / (1.0 + jnp.exp(-x))


def _silu(x):
    return x * _sigmoid(x)


def _split3(x):
    hi = x.astype(BF16)
    r = x - hi.astype(F32)
    mid = r.astype(BF16)
    lo = (r - mid.astype(F32)).astype(BF16)
    return hi, mid, lo


def _dot(a, b):
    return jnp.dot(a, b, preferred_element_type=F32)


def _dot_nt(a, b):
    return lax.dot_general(a, b, (((1,), (1,)), ((), ())), preferred_element_type=F32)


def _dot_exact_rhs(a, b_exact):
    hi, mid, lo = _split3(a)
    return _dot(hi, b_exact) + _dot(mid, b_exact) + _dot(lo, b_exact)


def _dot_exact_lhs(a_exact, b):
    hi, mid, lo = _split3(b)
    return _dot(a_exact, hi) + _dot(a_exact, mid) + _dot(a_exact, lo)


def _adaln_kernel(c_ref, w_ref, b_ref, o_ref):
    s = _silu(c_ref[...]).astype(BF16)
    o_ref[...] = _dot(s, w_ref[...].astype(BF16)) + b_ref[...]


def _adaln(c, w_ada, b_ada):
    b, d = c.shape
    n = w_ada.shape[1]
    rows = SUBLANES
    c_pad = jnp.zeros((rows, d), F32).at[:b].set(c)
    tn = 1024
    out = pl.pallas_call(
        _adaln_kernel,
        grid=(n // tn,),
        in_specs=[pl.BlockSpec((rows, d), lambda j: (0, 0)),
                  pl.BlockSpec((d, tn), lambda j: (0, j)),
                  pl.BlockSpec((1, tn), lambda j: (0, j))],
        out_specs=pl.BlockSpec((rows, tn), lambda j: (0, j)),
        out_shape=jax.ShapeDtypeStruct((rows, n), F32),
        compiler_params=_params(("arbitrary",)),
        name="adaln",
    )(c_pad, w_ada, b_ada.reshape(1, n))
    return out[:b]


def _inproj_kernel(x_ref, shift_ref, scale_ref, g_ref, wqkv_ref, wz_ref, wxbc_ref, wdt_ref,
                   qg_ref, kg_ref, hmean_ref, q_ref, k_ref, v_ref, z_ref, xbc_ref, dt_ref):
    x = x_ref[...]
    ms = jnp.mean(x * x, axis=-1, keepdims=True)
    h = x * lax.rsqrt(ms + NORM_EPS) * g_ref[...]
    h = h * (1.0 + scale_ref[...]) + shift_ref[...]
    hb = h.astype(BF16)

    hmean = hmean_ref[...]

    def head_norm(t, gain):
        ss = _dot_exact_rhs(t * t, hmean)
        return t * lax.rsqrt(ss + NORM_EPS) * gain

    q = _dot(hb, wqkv_ref[:, 0:ATTN_WIDTH])
    q_ref[...] = head_norm(q, qg_ref[...]) * (HEAD_DIM ** -0.5)
    k = _dot(hb, wqkv_ref[:, ATTN_WIDTH:2 * ATTN_WIDTH])
    k_ref[...] = head_norm(k, kg_ref[...])
    v_ref[...] = _dot(hb, wqkv_ref[:, 2 * ATTN_WIDTH:3 * ATTN_WIDTH])
    for c0 in range(0, SSM_WIDTH, 512):
        z_ref[:, c0:c0 + 512] = _dot(hb, wz_ref[:, c0:c0 + 512])
    for c0 in range(0, CONV_CH, 512):
        xbc_ref[:, c0:c0 + 512] = _dot(hb, wxbc_ref[:, c0:c0 + 512])
    dt_ref[...] = _dot(hb, wdt_ref[...])


def _in_proj(x, shift, scale, gain, w_in, q_gain, k_gain, tm=256):
    b, s, d = x.shape
    t = b * s
    tiles_per_seq = s // tm
    w = w_in.astype(BF16)
    o_z = 3 * ATTN_WIDTH
    o_x = o_z + SSM_WIDTH
    o_dt = o_x + CONV_CH
    w_qkv, w_z, w_xbc = w[:, :o_z], w[:, o_z:o_x], w[:, o_x:o_dt]
    w_dt = jnp.zeros((d, LANES), BF16).at[:, :SSM_HEADS].set(w[:, o_dt:])
    head_of = np.arange(ATTN_WIDTH) // HEAD_DIM
    hmean = jnp.asarray((head_of[:, None] == head_of[None, :]).astype(np.float32) / HEAD_DIM, BF16)
    full = lambda shp: pl.BlockSpec(shp, lambda i: (0,) * len(shp))
    row = lambda n: pl.BlockSpec((tm, n), lambda i: (i, 0))
    per_batch = pl.BlockSpec((None, 1, d), lambda i: (i // tiles_per_seq, 0, 0))
    outs = pl.pallas_call(
        _inproj_kernel,
        grid=(t // tm,),
        in_specs=[row(d), per_batch, per_batch, full((1, d)),
                  full((d, o_z)), full((d, SSM_WIDTH)), full((d, CONV_CH)), full((d, LANES)),
                  full((1, ATTN_WIDTH)), full((1, ATTN_WIDTH)), full((ATTN_WIDTH, ATTN_WIDTH))],
        out_specs=[row(ATTN_WIDTH), row(ATTN_WIDTH), row(ATTN_WIDTH), row(SSM_WIDTH), row(CONV_CH), row(LANES)],
        out_shape=[jax.ShapeDtypeStruct((t, n), F32)
                   for n in (ATTN_WIDTH, ATTN_WIDTH, ATTN_WIDTH, SSM_WIDTH, CONV_CH, LANES)],
        compiler_params=_params(("arbitrary",)),
        name="in_proj",
    )(x.reshape(t, d), shift.reshape(b, 1, d), scale.reshape(b, 1, d), gain.reshape(1, d),
      w_qkv, w_z, w_xbc, w_dt,
      jnp.tile(q_gain, ATTN_HEADS).reshape(1, ATTN_WIDTH), jnp.tile(k_gain, ATTN_HEADS).reshape(1, ATTN_WIDTH), hmean)
    return outs


def _t5_causal_buckets(distance):
    n = np.maximum(distance, 0)
    max_exact = REL_BUCKETS // 2
    large = max_exact + (np.log(np.maximum(n, 1) / max_exact) / math.log(REL_MAX_DISTANCE / max_exact)
                         * (REL_BUCKETS - max_exact)).astype(np.int64)
    large = np.minimum(large, REL_BUCKETS - 1)
    return np.where(n < max_exact, n, large).astype(np.int32)


def _window_bias(rel_bias_table, dilation):
    qi = np.arange(WIN_STEPS)[:, None]
    kj = np.arange(2 * WIN_STEPS)[None, :]
    dist = qi + WIN_STEPS - kj
    band = (dist >= 0) & (dist <= WIN_STEPS)
    onehot = (_t5_causal_buckets(dist * dilation).reshape(-1, 1) == np.arange(REL_BUCKETS)[None, :]).astype(np.float32)
    bias = jnp.dot(rel_bias_table.astype(F32).T, jnp.asarray(onehot).T, precision=lax.Precision.HIGHEST)
    bias = bias.reshape(ATTN_HEADS, WIN_STEPS, 2 * WIN_STEPS)
    return jnp.where(jnp.asarray(band)[None], bias, NEG_BIG)


ATTN_TOKENS = max(w for w, _ in PATTERNS)
ATTN_UNROLL = 16


def _attn_kernel(q_ref, kp_ref, kc_ref, vp_ref, vc_ref, bias_ref, out_ref, kw, vw, o_acc, l_acc):
    tb = ATTN_TOKENS
    first = pl.program_id(2) == 0
    kw[0:tb] = kp_ref[...]
    kw[tb:2 * tb] = kc_ref[...]
    vw[0:tb] = vp_ref[...]
    vw[tb:2 * tb] = vc_ref[...]
    lane = lax.broadcasted_iota(I32, (WIN_STEPS, LANES), 1)
    head0 = lane < HEAD_DIM
    col = lax.broadcasted_iota(I32, (WIN_STEPS, 2 * WIN_STEPS), 1)
    in_prev = col < WIN_STEPS

    for p, (_, d) in enumerate(PATTERNS):
        shift = d.bit_length() - 1
        n_blocks = tb // WIN_STEPS

        def rows(start, n, d=d):
            return pl.ds(start, n, stride=d) if d > 1 else pl.ds(start, n)

        def body(it, carry, p=p, d=d, shift=shift, rows=rows):
            for u in range(ATTN_UNROLL):
                idx = it * ATTN_UNROLL + u
                r = jnp.bitwise_and(idx, d - 1)
                j = jnp.right_shift(idx, shift)
                qs = j * (WIN_STEPS * d) + r
                q = q_ref[rows(qs, WIN_STEPS), :]
                k = kw[rows(tb + qs - WIN_STEPS * d, 2 * WIN_STEPS), :].astype(BF16)
                v = vw[rows(tb + qs - WIN_STEPS * d, 2 * WIN_STEPS), :].astype(BF16)
                no_prev = jnp.logical_and(in_prev, jnp.logical_and(first, j == 0))
                o_h, lse_h = [], []
                for h in range(2):
                    qh = jnp.where(head0 if h == 0 else jnp.logical_not(head0), q, 0.0).astype(BF16)
                    s = _dot_nt(qh, k) + bias_ref[p, h]
                    s = jnp.where(no_prev, NEG_BIG, s)
                    m = jnp.max(s, axis=-1, keepdims=True)
                    e = jnp.exp(s - m)
                    denom = jnp.sum(e, axis=-1, keepdims=True)
                    o_h.append(_dot(e.astype(BF16), v) / denom)
                    lse_h.append(m + jnp.log(denom))
                o_acc[p, rows(qs, WIN_STEPS), :] = jnp.where(head0, o_h[0], o_h[1])
                l_acc[p, rows(qs, WIN_STEPS), :] = jnp.where(head0, lse_h[0], lse_h[1])
            return carry

        lax.fori_loop(0, n_blocks // ATTN_UNROLL, body, 0)

    chunk = 256
    for c0 in range(0, tb, chunk):
        l1, l2, l3 = (l_acc[p, c0:c0 + chunk, :] for p in range(3))
        m = jnp.maximum(jnp.maximum(l1, l2), l3)
        e1, e2, e3 = jnp.exp(l1 - m), jnp.exp(l2 - m), jnp.exp(l3 - m)
        num = e1 * o_acc[0, c0:c0 + chunk, :] + e2 * o_acc[1, c0:c0 + chunk, :] + e3 * o_acc[2, c0:c0 + chunk, :]
        out_ref[c0:c0 + chunk, :] = num / (e1 + e2 + e3)


def _attention(q, k, v, bias):
    b, s, w = q.shape
    tb = ATTN_TOKENS
    pairs = ATTN_HEADS // 2
    cur = pl.BlockSpec((None, tb, LANES), lambda bi, hp, i: (bi, i, hp))
    prev = pl.BlockSpec((None, tb, LANES), lambda bi, hp, i: (bi, jnp.maximum(i - 1, 0), hp))
    return pl.pallas_call(
        _attn_kernel,
        grid=(b, pairs, s // tb),
        in_specs=[cur, prev, cur, prev, cur,
                  pl.BlockSpec((len(PATTERNS), 2, WIN_STEPS, 2 * WIN_STEPS), lambda bi, hp, i: (0, hp, 0, 0))],
        out_specs=cur,
        out_shape=jax.ShapeDtypeStruct((b, s, w), F32),
        scratch_shapes=[pltpu.VMEM((2 * tb, LANES), F32), pltpu.VMEM((2 * tb, LANES), F32),
                        pltpu.VMEM((len(PATTERNS), tb, LANES), F32), pltpu.VMEM((len(PATTERNS), tb, LANES), F32)],
        compiler_params=_params(("arbitrary",) * 3),
        name="attention",
    )(q, k, k, v, v, bias)


SSD_CHUNKS_PER_STEP = 2


def _ssd_kernel(xbc_ref, halo_ref, z_ref, dtraw_ref, convw_ref, convb_ref, dtb_ref, alog_ref, dskip_ref, gain_ref,
                expand_ref, tril_ref, y_ref, state_ref):
    c = pl.program_id(1)

    @pl.when(c == 0)
    def _():
        state_ref[...] = jnp.zeros_like(state_ref)

    halo = jnp.where(c == 0, 0.0, halo_ref[...])
    for sub in range(SSD_CHUNKS_PER_STEP):
        rows = slice(sub * SSM_CHUNK, (sub + 1) * SSM_CHUNK)
        x = xbc_ref[rows, :]
        w = convw_ref[...]
        acc = x * w[SSM_CONV - 1:SSM_CONV, :] + convb_ref[...]
        row8 = lax.broadcasted_iota(I32, (SUBLANES, CONV_CH), 0)
        for shift in range(1, SSM_CONV):
            xs = pltpu.roll(x, shift, axis=0)
            hs = pltpu.roll(halo, shift, axis=0)
            head = jnp.where(row8 < shift, hs, xs[0:SUBLANES])
            xs = jnp.concatenate([head, xs[SUBLANES:]], axis=0)
            acc = acc + xs * w[SSM_CONV - 1 - shift:SSM_CONV - shift, :]
        act = _silu(acc)
        x_s = act[:, :SSM_WIDTH]
        bc0 = SSM_WIDTH
        cc0 = SSM_WIDTH + SSM_GROUPS * SSM_STATE

        t = dtraw_ref[rows, :] + dtb_ref[...]
        dt = jnp.maximum(t, 0.0) + jnp.log(1.0 + jnp.exp(-jnp.abs(t)))
        a = dt * (-jnp.exp(alog_ref[...]))
        a_cs = _dot_exact_lhs(tril_ref[...], a)
        a_cs_t = a_cs.T
        a_last = a_cs[SSM_CHUNK - 1:SSM_CHUNK, :]
        expand = expand_ref[...]
        dt_e = _dot_exact_rhs(dt, expand)
        ea_e = _dot_exact_rhs(jnp.exp(a_cs), expand)
        dte_e = _dot_exact_rhs(jnp.exp(a_last - a_cs), expand)
        xdt = x_s * dt_e
        xw = (xdt * dte_e).astype(BF16)
        xdt_b = xdt.astype(BF16)

        li = lax.broadcasted_iota(I32, (SSM_CHUNK, SSM_CHUNK), 0)
        si = lax.broadcasted_iota(I32, (SSM_CHUNK, SSM_CHUNK), 1)
        causal = li >= si

        ys = []
        for g in range(SSM_GROUPS):
            gs = slice(g * GROUP_WIDTH, (g + 1) * GROUP_WIDTH)
            b_g = act[:, bc0 + g * SSM_STATE:bc0 + (g + 1) * SSM_STATE]
            c_g = act[:, cc0 + g * SSM_STATE:cc0 + (g + 1) * SSM_STATE].astype(BF16)
            cb = _dot_nt(c_g, b_g.astype(BF16))
            state = state_ref[g]
            y_off = _dot(c_g, state.astype(BF16)) * ea_e[:, gs]
            parts = []
            for j in range(HEADS_PER_GROUP):
                hh = g * HEADS_PER_GROUP + j
                seg = a_cs[:, hh:hh + 1] - a_cs_t[hh:hh + 1, :]
                decay = jnp.exp(jnp.where(causal, seg, NEG_BIG))
                m = (cb * decay).astype(BF16)
                parts.append(_dot(m, xdt_b[:, hh * SSM_HEAD_DIM:(hh + 1) * SSM_HEAD_DIM]))
            ys.append(jnp.concatenate(parts, axis=1) + y_off)
            state_ref[g] = state * ea_e[SSM_CHUNK - 1:SSM_CHUNK, gs] + _dot(b_g.T.astype(BF16), xw[:, gs])
        y = jnp.concatenate(ys, axis=1) + dskip_ref[...] * x_s
        y = y * _silu(z_ref[rows, :])
        gain = gain_ref[...]
        for g in range(SSM_GROUPS):
            gs = slice(g * GROUP_WIDTH, (g + 1) * GROUP_WIDTH)
            yg = y[:, gs]
            ms = jnp.mean(yg * yg, axis=-1, keepdims=True)
            y_ref[rows, gs] = yg * lax.rsqrt(ms + NORM_EPS) * gain[:, gs]
        halo = x[SSM_CHUNK - SUBLANES:, :]


def _ssd(xbc, z, dt_raw, conv_w, conv_b, dt_bias, a_log, d_skip, norm_gain, b, s):
    t = b * s
    step_rows = SSD_CHUNKS_PER_STEP * SSM_CHUNK
    nc = s // step_rows
    pad_heads = lambda v: jnp.zeros((1, LANES), F32).at[0, :SSM_HEADS].set(v)
    head_of_lane = np.arange(SSM_WIDTH) // SSM_HEAD_DIM
    expand = jnp.asarray((np.arange(LANES)[:, None] == head_of_lane[None, :]).astype(np.float32), BF16)
    tril = jnp.asarray(np.tril(np.ones((SSM_CHUNK, SSM_CHUNK), np.float32)), BF16)
    halo_blocks = step_rows // SUBLANES
    chunk = lambda n: pl.BlockSpec((step_rows, n), lambda bi, c: (bi * nc + c, 0))
    full = lambda shp: pl.BlockSpec(shp, lambda bi, c: (0,) * len(shp))
    halo = pl.BlockSpec((SUBLANES, CONV_CH), lambda bi, c: (jnp.maximum((bi * nc + c) * halo_blocks - 1, 0), 0))
    return pl.pallas_call(
        _ssd_kernel,
        grid=(b, nc),
        in_specs=[chunk(CONV_CH), halo, chunk(SSM_WIDTH), chunk(LANES),
                  full((SSM_CONV, CONV_CH)), full((1, CONV_CH)), full((1, LANES)), full((1, LANES)),
                  full((1, SSM_WIDTH)), full((1, SSM_WIDTH)), full((LANES, SSM_WIDTH)), full((SSM_CHUNK, SSM_CHUNK))],
        out_specs=chunk(SSM_WIDTH),
        out_shape=jax.ShapeDtypeStruct((t, SSM_WIDTH), F32),
        scratch_shapes=[pltpu.VMEM((SSM_GROUPS, SSM_STATE, GROUP_WIDTH), F32)],
        compiler_params=_params(("arbitrary", "arbitrary")),
        name="ssd",
    )(xbc, xbc, z, dt_raw, conv_w, conv_b.reshape(1, CONV_CH), pad_heads(dt_bias), pad_heads(a_log),
      jnp.repeat(d_skip, SSM_HEAD_DIM).reshape(1, SSM_WIDTH), norm_gain.reshape(1, SSM_WIDTH), expand, tril)


WORD = jnp.int32
TOKEN_ROWS = D_MODEL // (2 * LANES)
HIGH_HALF = np.int32(-65536)


def _to_token_tiles(ref, x):
    n = x.shape[0]
    for c in range(TOKEN_ROWS):
        lo = lax.bitcast_convert_type(x[:, c * LANES:(c + 1) * LANES].astype(BF16).astype(F32), WORD)
        hi = lax.bitcast_convert_type(x[:, (c + TOKEN_ROWS) * LANES:(c + TOKEN_ROWS + 1) * LANES]
                                      .astype(BF16).astype(F32), WORD)
        ref[pl.ds(c, n, stride=TOKEN_ROWS), :] = jnp.bitwise_or(lax.shift_right_logical(lo, 16),
                                                                 jnp.bitwise_and(hi, HIGH_HALF))


def _from_token_tiles(ref, n, token0=0):
    lows, highs = [], []
    for c in range(TOKEN_ROWS):
        word = ref[pl.ds(token0 * TOKEN_ROWS + c, n, stride=TOKEN_ROWS), :]
        lows.append(lax.bitcast_convert_type(jnp.left_shift(word, 16), F32))
        highs.append(lax.bitcast_convert_type(jnp.bitwise_and(word, HIGH_HALF), F32))
    return jnp.concatenate(lows + highs, axis=1)


def _outproj_kernel(attn_ref, ssm_ref, x_ref, gate_ref, shift_ref, scale_ref, g_ref, wa_ref, ws_ref, x1_ref, h2_ref,
                    h2t_ref):
    mixed = _dot(attn_ref[...].astype(BF16), wa_ref[...]) + _dot(ssm_ref[...].astype(BF16), ws_ref[...])
    x1 = x_ref[...] + gate_ref[...] * mixed
    x1_ref[...] = x1
    ms = jnp.mean(x1 * x1, axis=-1, keepdims=True)
    h = x1 * lax.rsqrt(ms + NORM_EPS) * g_ref[...]
    h2 = h * (1.0 + scale_ref[...]) + shift_ref[...]
    h2_ref[...] = h2
    for c in range(TOKEN_ROWS):
        lo = lax.bitcast_convert_type(h2[:, c * LANES:(c + 1) * LANES].astype(BF16).astype(F32), WORD)
        hi = lax.bitcast_convert_type(h2[:, (c + TOKEN_ROWS) * LANES:(c + TOKEN_ROWS + 1) * LANES]
                                      .astype(BF16).astype(F32), WORD)
        h2t_ref[c] = jnp.bitwise_or(lax.shift_right_logical(lo, 16), jnp.bitwise_and(hi, HIGH_HALF))


def _out_proj(attn, ssm, x, gate, shift, scale, gain, w_out, b, s, tm=256):
    t = b * s
    d = D_MODEL
    tiles_per_seq = s // tm
    w = w_out.astype(BF16)
    row = lambda n: pl.BlockSpec((tm, n), lambda i: (i, 0))
    full = lambda shp: pl.BlockSpec(shp, lambda i: (0,) * len(shp))
    per_batch = pl.BlockSpec((None, 1, d), lambda i: (i // tiles_per_seq, 0, 0))
    return pl.pallas_call(
        _outproj_kernel,
        grid=(t // tm,),
        in_specs=[row(ATTN_WIDTH), row(SSM_WIDTH), row(d), per_batch, per_batch, per_batch, full((1, d)),
                  full((ATTN_WIDTH, d)), full((SSM_WIDTH, d))],
        out_specs=[row(d), row(d), pl.BlockSpec((TOKEN_ROWS, tm, LANES), lambda i: (0, i, 0))],
        out_shape=[jax.ShapeDtypeStruct((t, d), F32)] * 2 + [jax.ShapeDtypeStruct((TOKEN_ROWS, t, LANES), WORD)],
        compiler_params=_params(("arbitrary",)),
        name="out_proj",
    )(attn.reshape(t, ATTN_WIDTH), ssm, x.reshape(t, d),
      gate.reshape(b, 1, d), shift.reshape(b, 1, d), scale.reshape(b, 1, d), gain.reshape(1, d),
      w[:ATTN_WIDTH], w[ATTN_WIDTH:])


def _mixer_sublayer(x, mod, norm_mix_gain, w_in, q_norm_gain, k_norm_gain, rel_bias_table, conv_w, conv_b, dt_bias,
                    a_log, d_skip, ssm_norm_gain, w_out, norm_ffn_gain):
    b, s, d = x.shape
    shift_m, scale_m, gate_m, shift_f, scale_f, _ = jnp.split(mod, 6, axis=-1)
    q, k, v, z, xbc, dt_raw = _in_proj(x, shift_m, scale_m, norm_mix_gain, w_in, q_norm_gain, k_norm_gain)
    bias = jnp.stack([_window_bias(rel_bias_table, dilation) for _, dilation in PATTERNS])
    attn = _attention(q.reshape(b, s, ATTN_WIDTH), k.reshape(b, s, ATTN_WIDTH), v.reshape(b, s, ATTN_WIDTH), bias)
    ssm = _ssd(xbc, z, dt_raw, conv_w, conv_b, dt_bias, a_log, d_skip, ssm_norm_gain, b, s)
    return _out_proj(attn, ssm, x, gate_m, shift_f, scale_f, norm_ffn_gain, w_out, b, s)


def _first_argmax(v, iota, limit):
    m = jnp.max(v, axis=0, keepdims=True)
    idx = jnp.min(jnp.where(v == m, iota, limit), axis=0, keepdims=True)
    return m, idx


def _router_kernel(h_ref, wt_ref, bias_ref, upper_ref, eidx_ref, rank_ref, gate_ref, counts_ref, carry_ref):
    @pl.when(pl.program_id(0) == 0)
    def _():
        carry_ref[...] = jnp.zeros_like(carry_ref)

    tm = h_ref.shape[0]
    h = h_ref[...]
    wt = wt_ref[...]
    h_hi = h.astype(BF16)
    h_lo = (h - h_hi.astype(F32)).astype(BF16)
    w_hi = wt.astype(BF16)
    w_lo = (wt - w_hi.astype(F32)).astype(BF16)
    logits = _dot_nt(w_hi, h_hi) + _dot_nt(w_hi, h_lo) + _dot_nt(w_lo, h_hi)
    scores = _sigmoid(logits)
    choice = scores + bias_ref[...]
    neg_inf = -jnp.inf

    iota_g = lax.broadcasted_iota(I32, (EXPERTS_PER_GROUP, tm), 0).astype(F32)
    group_rows = []
    for g in range(N_EXPERT_GROUPS):
        v = choice[g * EXPERTS_PER_GROUP:(g + 1) * EXPERTS_PER_GROUP]
        m1, i1 = _first_argmax(v, iota_g, float(EXPERTS_PER_GROUP))
        m2 = jnp.max(jnp.where(iota_g == i1, neg_inf, v), axis=0, keepdims=True)
        group_rows.append(m1 + m2)
    group_scores = jnp.concatenate(group_rows, axis=0)

    iota_n = lax.broadcasted_iota(I32, (N_EXPERT_GROUPS, tm), 0).astype(F32)
    chosen = jnp.zeros((N_EXPERT_GROUPS, tm), F32)
    for _ in range(TOPK_GROUPS):
        _, gi = _first_argmax(group_scores, iota_n, float(N_EXPERT_GROUPS))
        hit = iota_n == gi
        chosen = jnp.where(hit, 1.0, chosen)
        group_scores = jnp.where(hit, neg_inf, group_scores)

    masked = jnp.concatenate(
        [jnp.where(chosen[g:g + 1] > 0.0, choice[g * EXPERTS_PER_GROUP:(g + 1) * EXPERTS_PER_GROUP], neg_inf)
         for g in range(N_EXPERT_GROUPS)], axis=0)

    iota_e = lax.broadcasted_iota(I32, (N_EXPERTS, tm), 0).astype(F32)
    picked, gates = [], []
    onehot = jnp.zeros((N_EXPERTS, tm), F32)
    for _ in range(TOP_K):
        _, ei = _first_argmax(masked, iota_e, float(N_EXPERTS))
        hit = iota_e == ei
        gates.append(jnp.sum(jnp.where(hit, scores, 0.0), axis=0, keepdims=True))
        masked = jnp.where(hit, neg_inf, masked)
        onehot = jnp.where(hit, 1.0, onehot)
        picked.append(ei)
    gate_sum = gates[0]
    for gk in gates[1:]:
        gate_sum = gate_sum + gk

    base = _dot(onehot.astype(BF16), upper_ref[...]) + carry_ref[...]
    ranks = [jnp.sum(jnp.where(iota_e == ei, base, 0.0), axis=0, keepdims=True) for ei in picked]
    carry_ref[...] = carry_ref[...] + jnp.sum(onehot, axis=1, keepdims=True)

    eidx_ref[...] = jnp.concatenate(picked, axis=0).astype(I32)
    rank_ref[...] = jnp.concatenate(ranks, axis=0).astype(I32)
    gate_ref[...] = jnp.concatenate([gk / gate_sum * ROUTED_SCALE for gk in gates], axis=0)
    counts_ref[...] = carry_ref[...].astype(I32)


def _router(h2, w_router, router_bias, tm=256):
    t, d = h2.shape
    upper = jnp.asarray(np.triu(np.ones((tm, tm), np.float32), 1), BF16)
    tok = pl.BlockSpec((TOP_K, tm), lambda i: (0, i))
    full = lambda shp: pl.BlockSpec(shp, lambda i: (0,) * len(shp))
    return pl.pallas_call(
        _router_kernel,
        grid=(t // tm,),
        in_specs=[pl.BlockSpec((tm, d), lambda i: (i, 0)), full((N_EXPERTS, d)), full((N_EXPERTS, 1)), full((tm, tm))],
        out_specs=[tok, tok, tok, full((N_EXPERTS, 1))],
        out_shape=[jax.ShapeDtypeStruct((TOP_K, t), I32), jax.ShapeDtypeStruct((TOP_K, t), I32),
                   jax.ShapeDtypeStruct((TOP_K, t), F32), jax.ShapeDtypeStruct((N_EXPERTS, 1), I32)],
        scratch_shapes=[pltpu.VMEM((N_EXPERTS, 1), F32)],
        compiler_params=_params(("arbitrary",)),
        name="router",
    )(h2, w_router.T, router_bias.reshape(N_EXPERTS, 1), upper)


def _positions_kernel(counts_ref, lower_ref, eidx_ref, rank_ref, pos_ref):
    tm = eidx_ref.shape[1]
    counts = jnp.broadcast_to(counts_ref[...].astype(F32), (N_EXPERTS, LANES))
    offsets = _dot_exact_lhs(lower_ref[...], counts)[:, 0:1]
    iota_e = lax.broadcasted_iota(I32, (N_EXPERTS, tm), 0).astype(F32)
    e = eidx_ref[...].astype(F32)
    rows = [jnp.sum(jnp.where(iota_e == e[k:k + 1], offsets, 0.0), axis=0, keepdims=True) for k in range(TOP_K)]
    pos_ref[...] = jnp.concatenate(rows, axis=0).astype(I32) + rank_ref[...]


def _positions(counts, eidx, rank, tm):
    t = eidx.shape[1]
    lower = jnp.asarray(np.tril(np.ones((N_EXPERTS, N_EXPERTS), np.float32), -1), BF16)
    tok = pl.BlockSpec((TOP_K, tm), lambda i: (0, i))
    return pl.pallas_call(
        _positions_kernel,
        grid=(t // tm,),
        in_specs=[pl.BlockSpec((N_EXPERTS, 1), lambda i: (0, 0)), pl.BlockSpec((N_EXPERTS, N_EXPERTS), lambda i: (0, 0)),
                  tok, tok],
        out_specs=tok,
        out_shape=jax.ShapeDtypeStruct((TOP_K, t), I32),
        compiler_params=_params(("arbitrary",)),
        name="positions",
    )(counts, lower, eidx, rank)


EXPERT_BLOCK = 512
TAIL_UNIT = 128
TAIL_PIECES = tuple(1 << i for i in reversed(range(EXPERT_BLOCK.bit_length() - 1)))
W_SLOTS = 3
X_SLOTS = 4
Y_SLOTS = 4


def _experts_kernel(start_ref, count_ref, nxt_ref, nxt2_ref, slot_ref, first_ref, blk0_ref, full0_ref, ptail_ref,
                    ltail_ref, blktok_ref, nblocks_ref, xs_hbm, wg_hbm, wu_hbm, wd_hbm, ys_hbm,
                    wg_buf, wu_buf, wd_buf, wg_bf, wu_bf, wd_bf, xbuf, ybuf, ytail, wsem, xsem, ysem, tsem):
    e = pl.program_id(0)
    last_step = e == pl.num_programs(0) - 1
    start, count = start_ref[e], count_ref[e]
    n_full = jnp.right_shift(count, EXPERT_BLOCK.bit_length() - 1)
    tail = jnp.bitwise_and(count, EXPERT_BLOCK - 1)
    n_blk = n_full + (tail > 0).astype(I32)
    blk0, full0 = blk0_ref[e], full0_ref[e]
    slot, nxt, nxt2 = slot_ref[e], nxt_ref[e], nxt2_ref[e]
    slot1 = jnp.where(slot + 1 >= W_SLOTS, slot + 1 - W_SLOTS, slot + 1)
    slot2 = jnp.where(slot + 2 >= W_SLOTS, slot + 2 - W_SLOTS, slot + 2)

    def token_rows(token, n):
        return pl.ds(pl.multiple_of(token * TOKEN_ROWS, TOKEN_ROWS), n * TOKEN_ROWS)

    def fetch(ex, s):
        return (pltpu.make_async_copy(wg_hbm.at[ex], wg_buf.at[s], wsem.at[s, 0]),
                pltpu.make_async_copy(wu_hbm.at[ex], wu_buf.at[s], wsem.at[s, 1]),
                pltpu.make_async_copy(wd_hbm.at[ex], wd_buf.at[s], wsem.at[s, 2]))

    def x_copy(token, s):
        return pltpu.make_async_copy(xs_hbm.at[token_rows(token, EXPERT_BLOCK)], xbuf.at[s], xsem.at[s])

    def y_copy(token, s):
        return pltpu.make_async_copy(ybuf.at[s], ys_hbm.at[token_rows(token, EXPERT_BLOCK)], ysem.at[s])

    def tail_copies(token, length):
        out = []
        for piece in TAIL_PIECES:
            bigger = (EXPERT_BLOCK - 1) & ~(2 * piece - 1)
            done = jnp.bitwise_and(length, bigger)
            cp = pltpu.make_async_copy(ytail.at[token_rows(done, piece)], ys_hbm.at[token_rows(token + done, piece)],
                                       tsem)
            out.append((jnp.bitwise_and(length, piece) != 0, cp))
        return out

    def block(s, rows=EXPERT_BLOCK):
        x = _from_token_tiles(xbuf.at[s], rows).astype(BF16)
        g = _dot(x, wg_bf[...])
        u = _dot(x, wu_bf[...])
        return _dot((_silu(g) * u).astype(BF16), wd_bf[...])

    @pl.when(count > 0)
    def _():
        @pl.when(first_ref[e] == 1)
        def _():
            for g in range(X_SLOTS - 1):
                @pl.when(g < nblocks_ref[0])
                def _(g=g):
                    x_copy(blktok_ref[g], g).start()
            for cp in fetch(e, slot):
                cp.start()

            @pl.when(nxt >= 0)
            def _():
                for cp in fetch(nxt, slot1):
                    cp.start()

        for cp in fetch(e, slot):
            cp.wait()

        @pl.when(nxt2 >= 0)
        def _():
            for cp in fetch(nxt2, slot2):
                cp.start()

        wg_bf[...] = wg_buf[slot].astype(BF16)
        wu_bf[...] = wu_buf[slot].astype(BF16)
        wd_bf[...] = wd_buf[slot].astype(BF16)

        def take_x(i):
            g = blk0 + i
            xs_slot = jnp.bitwise_and(g, X_SLOTS - 1)
            x_copy(start, xs_slot).wait()
            ahead = g + (X_SLOTS - 1)

            @pl.when(ahead < nblocks_ref[0])
            def _():
                x_copy(blktok_ref[ahead], jnp.bitwise_and(ahead, X_SLOTS - 1)).start()

            return xs_slot

        def full_block(i, carry):
            y = block(take_x(i))
            j = full0 + i
            ys_slot = jnp.bitwise_and(j, Y_SLOTS - 1)

            @pl.when(j >= Y_SLOTS)
            def _():
                y_copy(start, ys_slot).wait()

            _to_token_tiles(ybuf.at[ys_slot], y)
            y_copy(start + i * EXPERT_BLOCK, ys_slot).start()
            return carry

        lax.fori_loop(0, n_full, full_block, 0)

        @pl.when(tail > 0)
        def _():
            xs_slot = take_x(n_full)
            for pred, cp in tail_copies(start, ptail_ref[e]):
                @pl.when(pred)
                def _(cp=cp):
                    cp.wait()
            units = jnp.right_shift(tail + (TAIL_UNIT - 1), TAIL_UNIT.bit_length() - 1)
            for u in range(1, EXPERT_BLOCK // TAIL_UNIT + 1):
                @pl.when(units == u)
                def _(u=u):
                    rows = u * TAIL_UNIT
                    _to_token_tiles(ytail.at[pl.ds(0, rows * TOKEN_ROWS)], block(xs_slot, rows))
            for pred, cp in tail_copies(start + n_full * EXPERT_BLOCK, tail):
                @pl.when(pred)
                def _(cp=cp):
                    cp.start()

    @pl.when(last_step)
    def _():
        total_full = full0 + n_full
        for back in range(1, Y_SLOTS + 1):
            @pl.when(total_full >= back)
            def _(back=back):
                y_copy(0, jnp.bitwise_and(total_full - back, Y_SLOTS - 1)).wait()
        for pred, cp in tail_copies(0, ltail_ref[0]):
            @pl.when(pred)
            def _(cp=cp):
                cp.wait()


def _max_expert_blocks(n_rows):
    return n_rows // EXPERT_BLOCK + N_EXPERTS


def _expert_metadata(counts, n_rows):
    ids = jnp.arange(N_EXPERTS, dtype=I32)
    used = counts > 0
    starts = jnp.cumsum(counts) - counts
    n_blk = (counts + EXPERT_BLOCK - 1) // EXPERT_BLOCK
    n_full = counts // EXPERT_BLOCK
    tail = counts % EXPERT_BLOCK
    blk0 = jnp.cumsum(n_blk) - n_blk
    full0 = jnp.cumsum(n_full) - n_full
    next_used = lax.cummin(jnp.where(used, ids, N_EXPERTS), reverse=True)
    next_after = jnp.concatenate([next_used[1:], jnp.full((1,), N_EXPERTS, I32)])
    nxt = jnp.where(next_after < N_EXPERTS, next_after, -1)
    ordinal = jnp.cumsum(used.astype(I32)) - 1
    slot = ordinal % W_SLOTS
    first = jnp.logical_and(used, ordinal == 0)
    latest = lax.cummax(jnp.where(tail > 0, ids, -1))
    before = jnp.concatenate([jnp.full((1,), -1, I32), latest[:-1]])
    pick = lambda index, values: jnp.sum(jnp.where(index[:, None] == ids[None, :], values[None, :], 0), axis=1)
    nxt2 = jnp.where(nxt >= 0, pick(nxt, nxt + 1), 0) - 1
    ptail = pick(before, tail)
    ltail = pick(latest[-1:], tail)
    block_ends = jnp.cumsum(n_blk)
    g = jnp.arange(_max_expert_blocks(n_rows), dtype=I32)
    eg = jnp.sum((g[:, None] >= block_ends[None, :]).astype(I32), axis=1)
    blktok = g * EXPERT_BLOCK + pick(eg, starts - blk0 * EXPERT_BLOCK)
    return tuple(v.astype(I32) for v in (starts, counts, nxt, nxt2, slot, first, blk0, full0, ptail, ltail, blktok,
                                         block_ends[-1:]))


def _experts(xs, counts, w_gate, w_up, w_down):
    d = D_MODEL
    meta = _expert_metadata(counts, xs.shape[0] // TOKEN_ROWS - EXPERT_BLOCK)
    hbm = pl.BlockSpec(memory_space=pl.ANY)
    blk = (EXPERT_BLOCK * TOKEN_ROWS, LANES)
    grid_spec = pltpu.PrefetchScalarGridSpec(
        num_scalar_prefetch=len(meta),
        grid=(N_EXPERTS,),
        in_specs=[hbm, hbm, hbm, hbm],
        out_specs=hbm,
        scratch_shapes=[pltpu.VMEM((W_SLOTS, d, EXPERT_FF), F32), pltpu.VMEM((W_SLOTS, d, EXPERT_FF), F32),
                        pltpu.VMEM((W_SLOTS, EXPERT_FF, d), F32),
                        pltpu.VMEM((d, EXPERT_FF), BF16), pltpu.VMEM((d, EXPERT_FF), BF16),
                        pltpu.VMEM((EXPERT_FF, d), BF16),
                        pltpu.VMEM((X_SLOTS,) + blk, WORD), pltpu.VMEM((Y_SLOTS,) + blk, WORD), pltpu.VMEM(blk, WORD),
                        pltpu.SemaphoreType.DMA((W_SLOTS, 3)), pltpu.SemaphoreType.DMA((X_SLOTS,)),
                        pltpu.SemaphoreType.DMA((Y_SLOTS,)), pltpu.SemaphoreType.DMA(())],
    )
    return pl.pallas_call(
        _experts_kernel,
        grid_spec=grid_spec,
        out_shape=jax.ShapeDtypeStruct(xs.shape, WORD),
        compiler_params=_params(("arbitrary",)),
        name="experts",
    )(*meta, xs, w_gate, w_up, w_down)


SC_CORES = 2
SC_SUBCORES = 16
SC_WORKERS = SC_CORES * SC_SUBCORES
SC_CHUNK = 128
SC_RING = 4


def _sc_scatter_rows(planes, idx, n_out_rows):
    n_planes, t, _ = planes.shape
    ranges = SC_WORKERS // n_planes
    n_chunks = t // ranges // SC_CHUNK
    idx = idx.reshape(TOP_K, n_planes, ranges, n_chunks, SC_CHUNK)
    mesh = plsc.VectorSubcoreMesh(core_axis_name="c", subcore_axis_name="s", num_cores=SC_CORES,
                                  num_subcores=SC_SUBCORES)

    def body(planes_hbm, idx_hbm, out_hbm, idx_v, rows_v, lsem, ssem):
        wid = lax.axis_index("s") * SC_CORES + lax.axis_index("c")
        plane = wid % n_planes
        token0 = (wid // n_planes) * (n_chunks * SC_CHUNK)
        for k in range(TOP_K):
            pltpu.sync_copy(idx_hbm.at[k, plane, wid // n_planes], idx_v.at[k])

        def load(slot, c):
            return pltpu.make_async_copy(planes_hbm.at[plane, pl.ds(token0 + c * SC_CHUNK, SC_CHUNK)], rows_v.at[slot],
                                         lsem.at[slot])

        def scatter(slot, c, k):
            return pltpu.make_async_copy(rows_v.at[slot], out_hbm.at[idx_v.at[k, c]], ssem.at[slot])

        @pl.loop(0, n_chunks, step=SC_RING)
        def _(g):
            for slot in range(SC_RING):
                @pl.when(g > 0)
                def _(slot=slot):
                    for k in range(TOP_K):
                        scatter(slot, 0, k).wait()
                load(slot, g + slot).start()
            for slot in range(SC_RING):
                load(slot, g + slot).wait()
                for k in range(TOP_K):
                    scatter(slot, g + slot, k).start()

        for slot in range(SC_RING):
            for k in range(TOP_K):
                scatter(slot, 0, k).wait()

    return pl.kernel(
        body, mesh=mesh,
        out_type=jax.ShapeDtypeStruct((n_out_rows, LANES), planes.dtype),
        scratch_types=[pltpu.VMEM((TOP_K, n_chunks, SC_CHUNK), I32), pltpu.VMEM((SC_RING, SC_CHUNK, LANES), planes.dtype),
                       pltpu.SemaphoreType.DMA((SC_RING,)), pltpu.SemaphoreType.DMA((SC_RING,))],
        name="sc_scatter",
    )(planes, idx)


def _sc_gather_rows(table, idx):
    n_workers, n_chunks, chunk = idx.shape
    rows_per_worker = n_chunks * chunk
    mesh = plsc.VectorSubcoreMesh(core_axis_name="c", subcore_axis_name="s", num_cores=SC_CORES,
                                  num_subcores=SC_SUBCORES)

    def body(table_hbm, idx_hbm, out_hbm, idx_v, rows_v, gsem, wsem):
        wid = lax.axis_index("s") * SC_CORES + lax.axis_index("c")
        base = wid * rows_per_worker
        pltpu.sync_copy(idx_hbm.at[wid], idx_v)

        def write(slot, c):
            return pltpu.make_async_copy(rows_v.at[slot], out_hbm.at[pl.ds(base + c * chunk, chunk)], wsem.at[slot])

        def gather(slot, c):
            return pltpu.make_async_copy(table_hbm.at[idx_v.at[c]], rows_v.at[slot], gsem.at[slot])

        @pl.loop(0, n_chunks, step=SC_RING)
        def _(g):
            for slot in range(SC_RING):
                @pl.when(g > 0)
                def _(slot=slot):
                    write(slot, 0).wait()
                gather(slot, g + slot).start()
            for slot in range(SC_RING):
                gather(slot, g + slot).wait()
                write(slot, g + slot).start()

        for slot in range(SC_RING):
            write(slot, 0).wait()

    return pl.kernel(
        body, mesh=mesh,
        out_type=jax.ShapeDtypeStruct((n_workers * rows_per_worker, LANES), table.dtype),
        scratch_types=[pltpu.VMEM((n_chunks, chunk), I32), pltpu.VMEM((SC_RING, chunk, LANES), table.dtype),
                       pltpu.SemaphoreType.DMA((SC_RING,)), pltpu.SemaphoreType.DMA((SC_RING,))],
        name="sc_gather",
    )(table, idx)


def _combine_kernel(rows_ref, gates_ref, h_ref, x1_ref, gatef_ref, wg_ref, wu_ref, wd_ref, out_ref):
    hb = h_ref[...].astype(BF16)
    shared = _dot((_silu(_dot(hb, wg_ref[...])) * _dot(hb, wu_ref[...])).astype(BF16), wd_ref[...])
    gates = gates_ref[...]

    def expert_rows(k):
        words = [rows_ref[k * TOKEN_ROWS + c] for c in range(TOKEN_ROWS)]
        lows = [lax.bitcast_convert_type(jnp.left_shift(w, 16), F32) for w in words]
        highs = [lax.bitcast_convert_type(jnp.bitwise_and(w, HIGH_HALF), F32) for w in words]
        return jnp.concatenate(lows + highs, axis=1)

    routed = expert_rows(0) * gates[:, 0:1]
    for k in range(1, TOP_K):
        routed = routed + expert_rows(k) * gates[:, k:k + 1]
    out_ref[...] = x1_ref[...] + gatef_ref[...] * (shared + routed)


def _combine(gathered, gates_t, h2, x1, gate_f, w_gate_s, w_up_s, w_down_s, b, s, tm):
    t, d = h2.shape
    tiles_per_seq = s // tm
    row = lambda n: pl.BlockSpec((tm, n), lambda i: (i, 0))
    full = lambda shp: pl.BlockSpec(shp, lambda i: (0,) * len(shp))
    return pl.pallas_call(
        _combine_kernel,
        grid=(t // tm,),
        in_specs=[pl.BlockSpec((TOP_K * TOKEN_ROWS, tm, LANES), lambda i: (0, i, 0)),
                  row(TOP_K), row(d), row(d),
                  pl.BlockSpec((None, 1, d), lambda i: (i // tiles_per_seq, 0, 0)),
                  full((d, EXPERT_FF)), full((d, EXPERT_FF)), full((EXPERT_FF, d))],
        out_specs=row(d),
        out_shape=jax.ShapeDtypeStruct((t, d), F32),
        compiler_params=_params(("arbitrary",)),
        name="combine",
    )(gathered, gates_t, h2, x1, gate_f.reshape(b, 1, d),
      w_gate_s.astype(BF16), w_up_s.astype(BF16), w_down_s.astype(BF16))


def _moe_sublayer(x1, h2, h2t, gate_f, w_router, router_bias, w_gate, w_up, w_down, w_gate_s, w_up_s, w_down_s, b, s,
                  tm=256):
    t = b * s
    eidx, rank, gates, counts = _router(h2, w_router, router_bias)
    pos = _positions(counts, eidx, rank, tm)
    idx = pos[:, None, :] * TOKEN_ROWS + jnp.arange(TOKEN_ROWS, dtype=I32)[None, :, None]
    xs = _sc_scatter_rows(h2t, idx, (t * TOP_K + EXPERT_BLOCK) * TOKEN_ROWS)
    ys = _experts(xs, counts[:, 0], w_gate, w_up, w_down)
    gathered = _sc_gather_rows(ys, idx.reshape(SC_WORKERS, -1, SC_CHUNK)).reshape(TOP_K * TOKEN_ROWS, t, LANES)
    return _combine(gathered, gates.T, h2, x1, gate_f, w_gate_s, w_up_s, w_down_s, b, s, tm)


def kernel(x, c, w_ada, b_ada, norm_mix_gain, w_in, q_norm_gain, k_norm_gain, rel_bias_table, conv_w, conv_b, dt_bias,
           a_log, d_skip, ssm_norm_gain, w_out, norm_ffn_gain, w_router, router_bias, w_gate_experts, w_up_experts,
           w_down_experts, w_gate_shared, w_up_shared, w_down_shared):
    b, s, d = x.shape
    for layer in range(w_ada.shape[0]):
        mod = _adaln(c, w_ada[layer], b_ada[layer])
        x1, h2, h2t = _mixer_sublayer(x, mod, norm_mix_gain[layer], w_in[layer], q_norm_gain[layer], k_norm_gain[layer],
                                 rel_bias_table, conv_w[layer], conv_b[layer], dt_bias[layer], a_log[layer],
                                 d_skip[layer], ssm_norm_gain[layer], w_out[layer], norm_ffn_gain[layer])
        gate_f = mod[:, 5 * d:]
        out = _moe_sublayer(x1, h2, h2t, gate_f, w_router[layer], router_bias[layer], w_gate_experts[layer],
                            w_up_experts[layer], w_down_experts[layer], w_gate_shared[layer], w_up_shared[layer],
                            w_down_shared[layer], b, s)
        x = out.reshape(b, s, d)
    return x
```

```python
import functools
import math

import numpy as np
import jax
import jax.numpy as jnp
from jax import lax
from jax.experimental import pallas as pl
from jax.experimental.pallas import tpu as pltpu
from jax.experimental.pallas import tpu_sc as plsc

F32 = jnp.float32
BF16 = jnp.bfloat16
I32 = jnp.int32

D_MODEL = 1024
ATTN_HEADS = 8
HEAD_DIM = 64
ATTN_WIDTH = ATTN_HEADS * HEAD_DIM
PATTERNS = ((128, 1), (512, 4), (2048, 16))
WIN_STEPS = 128
REL_BUCKETS = 32
REL_MAX_DISTANCE = 2048
SSM_HEADS = 24
SSM_HEAD_DIM = 64
SSM_WIDTH = SSM_HEADS * SSM_HEAD_DIM
SSM_GROUPS = 4
HEADS_PER_GROUP = SSM_HEADS // SSM_GROUPS
GROUP_WIDTH = SSM_WIDTH // SSM_GROUPS
SSM_STATE = 128
SSM_CONV = 4
SSM_CHUNK = 128
CONV_CH = SSM_WIDTH + 2 * SSM_GROUPS * SSM_STATE
N_EXPERTS = 256
TOP_K = 8
N_EXPERT_GROUPS = 8
EXPERTS_PER_GROUP = N_EXPERTS // N_EXPERT_GROUPS
TOPK_GROUPS = 4
EXPERT_FF = 256
ROUTED_SCALE = 2.5
NORM_EPS = 1e-6

LANES = 128
SUBLANES = 8
NEG_BIG = -1e30
VMEM_LIMIT = 56 * 1024 * 1024


def _params(sem, vmem=VMEM_LIMIT):
    return pltpu.CompilerParams(dimension_semantics=sem, vmem_limit_bytes=vmem)


def _sigmoid(x):
    return 1.0 / (1.0 + jnp.exp(-x))


def _silu(x):
    return x * _sigmoid(x)


def _split3(x):
    hi = x.astype(BF16)
    r = x - hi.astype(F32)
    mid = r.astype(BF16)
    lo = (r - mid.astype(F32)).astype(BF16)
    return hi, mid, lo


def _dot(a, b):
    return jnp.dot(a, b, preferred_element_type=F32)


def _dot_nt(a, b):
    return lax.dot_general(a, b, (((1,), (1,)), ((), ())), preferred_element_type=F32)


def _dot_exact_rhs(a, b_exact):
    hi, mid, lo = _split3(a)
    return _dot(hi, b_exact) + _dot(mid, b_exact) + _dot(lo, b_exact)


def _dot_exact_lhs(a_exact, b):
    hi, mid, lo = _split3(b)
    return _dot(a_exact, hi) + _dot(a_exact, mid) + _dot(a_exact, lo)


def _adaln_kernel(c_ref, w_ref, b_ref, o_ref):
    s = _silu(c_ref[...]).astype(BF16)
    o_ref[...] = _dot(s, w_ref[...].astype(BF16)) + b_ref[...]


def _adaln(c, w_ada, b_ada):
    b, d = c.shape
    n = w_ada.shape[1]
    rows = SUBLANES
    c_pad = jnp.zeros((rows, d), F32).at[:b].set(c)
    tn = 1024
    out = pl.pallas_call(
        _adaln_kernel,
        grid=(n // tn,),
        in_specs=[pl.BlockSpec((rows, d), lambda j: (0, 0)),
                  pl.BlockSpec((d, tn), lambda j: (0, j)),
                  pl.BlockSpec((1, tn), lambda j: (0, j))],
        out_specs=pl.BlockSpec((rows, tn), lambda j: (0, j)),
        out_shape=jax.ShapeDtypeStruct((rows, n), F32),
        compiler_params=_params(("arbitrary",)),
        name="adaln",
    )(c_pad, w_ada, b_ada.reshape(1, n))
    return out[:b]


def _inproj_kernel(x_ref, shift_ref, scale_ref, g_ref, wqkv_ref, wz_ref, wxbc_ref, wdt_ref,
                   qg_ref, kg_ref, hmean_ref, q_ref, k_ref, v_ref, z_ref, xbc_ref, dt_ref):
    x = x_ref[...]
    ms = jnp.mean(x * x, axis=-1, keepdims=True)
    h = x * lax.rsqrt(ms + NORM_EPS) * g_ref[...]
    h = h * (1.0 + scale_ref[...]) + shift_ref[...]
    hb = h.astype(BF16)

    hmean = hmean_ref[...]

    def head_norm(t, gain):
        ss = _dot_exact_rhs(t * t, hmean)
        return t * lax.rsqrt(ss + NORM_EPS) * gain

    q = _dot(hb, wqkv_ref[:, 0:ATTN_WIDTH])
    q_ref[...] = head_norm(q, qg_ref[...]) * (HEAD_DIM ** -0.5)
    k = _dot(hb, wqkv_ref[:, ATTN_WIDTH:2 * ATTN_WIDTH])
    k_ref[...] = head_norm(k, kg_ref[...])
    v_ref[...] = _dot(hb, wqkv_ref[:, 2 * ATTN_WIDTH:3 * ATTN_WIDTH])
    for c0 in range(0, SSM_WIDTH, 512):
        z_ref[:, c0:c0 + 512] = _dot(hb, wz_ref[:, c0:c0 + 512])
    for c0 in range(0, CONV_CH, 512):
        xbc_ref[:, c0:c0 + 512] = _dot(hb, wxbc_ref[:, c0:c0 + 512])
    dt_ref[...] = _dot(hb, wdt_ref[...])


def _in_proj(x, shift, scale, gain, w_in, q_gain, k_gain, tm=256):
    b, s, d = x.shape
    t = b * s
    tiles_per_seq = s // tm
    w = w_in.astype(BF16)
    o_z = 3 * ATTN_WIDTH
    o_x = o_z + SSM_WIDTH
    o_dt = o_x + CONV_CH
    w_qkv, w_z, w_xbc = w[:, :o_z], w[:, o_z:o_x], w[:, o_x:o_dt]
    w_dt = jnp.zeros((d, LANES), BF16).at[:, :SSM_HEADS].set(w[:, o_dt:])
    head_of = np.arange(ATTN_WIDTH) // HEAD_DIM
    hmean = jnp.asarray((head_of[:, None] == head_of[None, :]).astype(np.float32) / HEAD_DIM, BF16)
    full = lambda shp: pl.BlockSpec(shp, lambda i: (0,) * len(shp))
    row = lambda n: pl.BlockSpec((tm, n), lambda i: (i, 0))
    per_batch = pl.BlockSpec((None, 1, d), lambda i: (i // tiles_per_seq, 0, 0))
    outs = pl.pallas_call(
        _inproj_kernel,
        grid=(t // tm,),
        in_specs=[row(d), per_batch, per_batch, full((1, d)),
                  full((d, o_z)), full((d, SSM_WIDTH)), full((d, CONV_CH)), full((d, LANES)),
                  full((1, ATTN_WIDTH)), full((1, ATTN_WIDTH)), full((ATTN_WIDTH, ATTN_WIDTH))],
        out_specs=[row(ATTN_WIDTH), row(ATTN_WIDTH), row(ATTN_WIDTH), row(SSM_WIDTH), row(CONV_CH), row(LANES)],
        out_shape=[jax.ShapeDtypeStruct((t, n), F32)
                   for n in (ATTN_WIDTH, ATTN_WIDTH, ATTN_WIDTH, SSM_WIDTH, CONV_CH, LANES)],
        compiler_params=_params(("arbitrary",)),
        name="in_proj",
    )(x.reshape(t, d), shift.reshape(b, 1, d), scale.reshape(b, 1, d), gain.reshape(1, d),
      w_qkv, w_z, w_xbc, w_dt,
      jnp.tile(q_gain, ATTN_HEADS).reshape(1, ATTN_WIDTH), jnp.tile(k_gain, ATTN_HEADS).reshape(1, ATTN_WIDTH), hmean)
    return outs


def _t5_causal_buckets(distance):
    n = np.maximum(distance, 0)
    max_exact = REL_BUCKETS // 2
    large = max_exact + (np.log(np.maximum(n, 1) / max_exact) / math.log(REL_MAX_DISTANCE / max_exact)
                         * (REL_BUCKETS - max_exact)).astype(np.int64)
    large = np.minimum(large, REL_BUCKETS - 1)
    return np.where(n < max_exact, n, large).astype(np.int32)


def _window_bias(rel_bias_table, dilation):
    qi = np.arange(WIN_STEPS)[:, None]
    kj = np.arange(2 * WIN_STEPS)[None, :]
    dist = qi + WIN_STEPS - kj
    band = (dist >= 0) & (dist <= WIN_STEPS)
    onehot = (_t5_causal_buckets(dist * dilation).reshape(-1, 1) == np.arange(REL_BUCKETS)[None, :]).astype(np.float32)
    bias = jnp.dot(rel_bias_table.astype(F32).T, jnp.asarray(onehot).T, precision=lax.Precision.HIGHEST)
    bias = bias.reshape(ATTN_HEADS, WIN_STEPS, 2 * WIN_STEPS)
    return jnp.where(jnp.asarray(band)[None], bias, NEG_BIG)


ATTN_TOKENS = max(w for w, _ in PATTERNS)
ATTN_UNROLL = 16


def _attn_kernel(q_ref, kp_ref, kc_ref, vp_ref, vc_ref, bias_ref, out_ref, kw, vw, o_acc, l_acc):
    tb = ATTN_TOKENS
    first = pl.program_id(2) == 0
    kw[0:tb] = kp_ref[...]
    kw[tb:2 * tb] = kc_ref[...]
    vw[0:tb] = vp_ref[...]
    vw[tb:2 * tb] = vc_ref[...]
    lane = lax.broadcasted_iota(I32, (WIN_STEPS, LANES), 1)
    head0 = lane < HEAD_DIM
    col = lax.broadcasted_iota(I32, (WIN_STEPS, 2 * WIN_STEPS), 1)
    in_prev = col < WIN_STEPS

    for p, (_, d) in enumerate(PATTERNS):
        shift = d.bit_length() - 1
        n_blocks = tb // WIN_STEPS

        def rows(start, n, d=d):
            return pl.ds(start, n, stride=d) if d > 1 else pl.ds(start, n)

        def body(it, carry, p=p, d=d, shift=shift, rows=rows):
            for u in range(ATTN_UNROLL):
                idx = it * ATTN_UNROLL + u
                r = jnp.bitwise_and(idx, d - 1)
                j = jnp.right_shift(idx, shift)
                qs = j * (WIN_STEPS * d) + r
                q = q_ref[rows(qs, WIN_STEPS), :]
                k = kw[rows(tb + qs - WIN_STEPS * d, 2 * WIN_STEPS), :].astype(BF16)
                v = vw[rows(tb + qs - WIN_STEPS * d, 2 * WIN_STEPS), :].astype(BF16)
                no_prev = jnp.logical_and(in_prev, jnp.logical_and(first, j == 0))
                o_h, lse_h = [], []
                for h in range(2):
                    qh = jnp.where(head0 if h == 0 else jnp.logical_not(head0), q, 0.0).astype(BF16)
                    s = _dot_nt(qh, k) + bias_ref[p, h]
                    s = jnp.where(no_prev, NEG_BIG, s)
                    m = jnp.max(s, axis=-1, keepdims=True)
                    e = jnp.exp(s - m)
                    denom = jnp.sum(e, axis=-1, keepdims=True)
                    o_h.append(_dot(e.astype(BF16), v) / denom)
                    lse_h.append(m + jnp.log(denom))
                o_acc[p, rows(qs, WIN_STEPS), :] = jnp.where(head0, o_h[0], o_h[1])
                l_acc[p, rows(qs, WIN_STEPS), :] = jnp.where(head0, lse_h[0], lse_h[1])
            return carry

        lax.fori_loop(0, n_blocks // ATTN_UNROLL, body, 0)

    chunk = 256
    for c0 in range(0, tb, chunk):
        l1, l2, l3 = (l_acc[p, c0:c0 + chunk, :] for p in range(3))
        m = jnp.maximum(jnp.maximum(l1, l2), l3)
        e1, e2, e3 = jnp.exp(l1 - m), jnp.exp(l2 - m), jnp.exp(l3 - m)
        num = e1 * o_acc[0, c0:c0 + chunk, :] + e2 * o_acc[1, c0:c0 + chunk, :] + e3 * o_acc[2, c0:c0 + chunk, :]
        out_ref[c0:c0 + chunk, :] = num / (e1 + e2 + e3)


def _attention(q, k, v, bias):
    b, s, w = q.shape
    tb = ATTN_TOKENS
    pairs = ATTN_HEADS // 2
    cur = pl.BlockSpec((None, tb, LANES), lambda bi, hp, i: (bi, i, hp))
    prev = pl.BlockSpec((None, tb, LANES), lambda bi, hp, i: (bi, jnp.maximum(i - 1, 0), hp))
    return pl.pallas_call(
        _attn_kernel,
        grid=(b, pairs, s // tb),
        in_specs=[cur, prev, cur, prev, cur,
                  pl.BlockSpec((len(PATTERNS), 2, WIN_STEPS, 2 * WIN_STEPS), lambda bi, hp, i: (0, hp, 0, 0))],
        out_specs=cur,
        out_shape=jax.ShapeDtypeStruct((b, s, w), F32),
        scratch_shapes=[pltpu.VMEM((2 * tb, LANES), F32), pltpu.VMEM((2 * tb, LANES), F32),
                        pltpu.VMEM((len(PATTERNS), tb, LANES), F32), pltpu.VMEM((len(PATTERNS), tb, LANES), F32)],
        compiler_params=_params(("arbitrary",) * 3),
        name="attention",
    )(q, k, k, v, v, bias)


SSD_CHUNKS_PER_STEP = 4


def _ssd_kernel(xbc_ref, halo_ref, z_ref, dtraw_ref, convw_ref, convb_ref, dtb_ref, alog_ref, dskip_ref, gain_ref,
                expand_ref, tril_ref, y_ref, state_ref):
    c = pl.program_id(1)

    @pl.when(c == 0)
    def _():
        state_ref[...] = jnp.zeros_like(state_ref)

    halo = jnp.where(c == 0, 0.0, halo_ref[...])
    for sub in range(SSD_CHUNKS_PER_STEP):
        rows = slice(sub * SSM_CHUNK, (sub + 1) * SSM_CHUNK)
        x = xbc_ref[rows, :]
        w = convw_ref[...]
        acc = x * w[SSM_CONV - 1:SSM_CONV, :] + convb_ref[...]
        row8 = lax.broadcasted_iota(I32, (SUBLANES, CONV_CH), 0)
        for shift in range(1, SSM_CONV):
            xs = pltpu.roll(x, shift, axis=0)
            hs = pltpu.roll(halo, shift, axis=0)
            head = jnp.where(row8 < shift, hs, xs[0:SUBLANES])
            xs = jnp.concatenate([head, xs[SUBLANES:]], axis=0)
            acc = acc + xs * w[SSM_CONV - 1 - shift:SSM_CONV - shift, :]
        act = _silu(acc)
        x_s = act[:, :SSM_WIDTH]
        bc0 = SSM_WIDTH
        cc0 = SSM_WIDTH + SSM_GROUPS * SSM_STATE

        t = dtraw_ref[rows, :] + dtb_ref[...]
        dt = jnp.maximum(t, 0.0) + jnp.log(1.0 + jnp.exp(-jnp.abs(t)))
        a = dt * (-jnp.exp(alog_ref[...]))
        a_cs = _dot_exact_lhs(tril_ref[...], a)
        a_cs_t = a_cs.T
        a_last = a_cs[SSM_CHUNK - 1:SSM_CHUNK, :]
        expand = expand_ref[...]
        dt_e = _dot_exact_rhs(dt, expand)
        ea_e = _dot_exact_rhs(jnp.exp(a_cs), expand)
        dte_e = _dot_exact_rhs(jnp.exp(a_last - a_cs), expand)
        xdt = x_s * dt_e
        xw = (xdt * dte_e).astype(BF16)
        xdt_b = xdt.astype(BF16)

        li = lax.broadcasted_iota(I32, (SSM_CHUNK, SSM_CHUNK), 0)
        si = lax.broadcasted_iota(I32, (SSM_CHUNK, SSM_CHUNK), 1)
        causal = li >= si

        ys = []
        for g in range(SSM_GROUPS):
            gs = slice(g * GROUP_WIDTH, (g + 1) * GROUP_WIDTH)
            b_g = act[:, bc0 + g * SSM_STATE:bc0 + (g + 1) * SSM_STATE]
            c_g = act[:, cc0 + g * SSM_STATE:cc0 + (g + 1) * SSM_STATE].astype(BF16)
            cb = _dot_nt(c_g, b_g.astype(BF16))
            state = state_ref[g]
            y_off = _dot(c_g, state.astype(BF16)) * ea_e[:, gs]
            parts = []
            for j in range(HEADS_PER_GROUP):
                hh = g * HEADS_PER_GROUP + j
                seg = a_cs[:, hh:hh + 1] - a_cs_t[hh:hh + 1, :]
                decay = jnp.exp(jnp.where(causal, seg, NEG_BIG))
                m = (cb * decay).astype(BF16)
                parts.append(_dot(m, xdt_b[:, hh * SSM_HEAD_DIM:(hh + 1) * SSM_HEAD_DIM]))
            ys.append(jnp.concatenate(parts, axis=1) + y_off)
            state_ref[g] = state * ea_e[SSM_CHUNK - 1:SSM_CHUNK, gs] + _dot(b_g.T.astype(BF16), xw[:, gs])
        y = jnp.concatenate(ys, axis=1) + dskip_ref[...] * x_s
        y = y * _silu(z_ref[rows, :])
        gain = gain_ref[...]
        for g in range(SSM_GROUPS):
            gs = slice(g * GROUP_WIDTH, (g + 1) * GROUP_WIDTH)
            yg = y[:, gs]
            ms = jnp.mean(yg * yg, axis=-1, keepdims=True)
            y_ref[rows, gs] = yg * lax.rsqrt(ms + NORM_EPS) * gain[:, gs]
        halo = x[SSM_CHUNK - SUBLANES:, :]


def _ssd(xbc, z, dt_raw, conv_w, conv_b, dt_bias, a_log, d_skip, norm_gain, b, s):
    t = b * s
    step_rows = SSD_CHUNKS_PER_STEP * SSM_CHUNK
    nc = s // step_rows
    pad_heads = lambda v: jnp.zeros((1, LANES), F32).at[0, :SSM_HEADS].set(v)
    head_of_lane = np.arange(SSM_WIDTH) // SSM_HEAD_DIM
    expand = jnp.asarray((np.arange(LANES)[:, None] == head_of_lane[None, :]).astype(np.float32), BF16)
    tril = jnp.asarray(np.tril(np.ones((SSM_CHUNK, SSM_CHUNK), np.float32)), BF16)
    halo_blocks = step_rows // SUBLANES
    chunk = lambda n: pl.BlockSpec((step_rows, n), lambda bi, c: (bi * nc + c, 0))
    full = lambda shp: pl.BlockSpec(shp, lambda bi, c: (0,) * len(shp))
    halo = pl.BlockSpec((SUBLANES, CONV_CH), lambda bi, c: (jnp.maximum((bi * nc + c) * halo_blocks - 1, 0), 0))
    return pl.pallas_call(
        _ssd_kernel,
        grid=(b, nc),
        in_specs=[chunk(CONV_CH), halo, chunk(SSM_WIDTH), chunk(LANES),
                  full((SSM_CONV, CONV_CH)), full((1, CONV_CH)), full((1, LANES)), full((1, LANES)),
                  full((1, SSM_WIDTH)), full((1, SSM_WIDTH)), full((LANES, SSM_WIDTH)), full((SSM_CHUNK, SSM_CHUNK))],
        out_specs=chunk(SSM_WIDTH),
        out_shape=jax.ShapeDtypeStruct((t, SSM_WIDTH), F32),
        scratch_shapes=[pltpu.VMEM((SSM_GROUPS, SSM_STATE, GROUP_WIDTH), F32)],
        compiler_params=_params(("arbitrary", "arbitrary")),
        name="ssd",
    )(xbc, xbc, z, dt_raw, conv_w, conv_b.reshape(1, CONV_CH), pad_heads(dt_bias), pad_heads(a_log),
      jnp.repeat(d_skip, SSM_HEAD_DIM).reshape(1, SSM_WIDTH), norm_gain.reshape(1, SSM_WIDTH), expand, tril)


WORD = jnp.int32
TOKEN_ROWS = D_MODEL // (2 * LANES)
HIGH_HALF = np.int32(-65536)


def _to_token_tiles(ref, x):
    n = x.shape[0]
    for c in range(TOKEN_ROWS):
        lo = lax.bitcast_convert_type(x[:, c * LANES:(c + 1) * LANES].astype(BF16).astype(F32), WORD)
        hi = lax.bitcast_convert_type(x[:, (c + TOKEN_ROWS) * LANES:(c + TOKEN_ROWS + 1) * LANES]
                                      .astype(BF16).astype(F32), WORD)
        ref[pl.ds(c, n, stride=TOKEN_ROWS), :] = jnp.bitwise_or(lax.shift_right_logical(lo, 16),
                                                                 jnp.bitwise_and(hi, HIGH_HALF))


def _from_token_tiles(ref, n, token0=0):
    lows, highs = [], []
    for c in range(TOKEN_ROWS):
        word = ref[pl.ds(token0 * TOKEN_ROWS + c, n, stride=TOKEN_ROWS), :]
        lows.append(lax.bitcast_convert_type(jnp.left_shift(word, 16), F32))
        highs.append(lax.bitcast_convert_type(jnp.bitwise_and(word, HIGH_HALF), F32))
    return jnp.concatenate(lows + highs, axis=1)


def _outproj_kernel(attn_ref, ssm_ref, x_ref, gate_ref, shift_ref, scale_ref, g_ref, wa_ref, ws_ref, x1_ref, h2_ref,
                    h2t_ref):
    mixed = _dot(attn_ref[...].astype(BF16), wa_ref[...]) + _dot(ssm_ref[...].astype(BF16), ws_ref[...])
    x1 = x_ref[...] + gate_ref[...] * mixed
    x1_ref[...] = x1
    ms = jnp.mean(x1 * x1, axis=-1, keepdims=True)
    h = x1 * lax.rsqrt(ms + NORM_EPS) * g_ref[...]
    h2 = h * (1.0 + scale_ref[...]) + shift_ref[...]
    h2_ref[...] = h2
    for c in range(TOKEN_ROWS):
        lo = lax.bitcast_convert_type(h2[:, c * LANES:(c + 1) * LANES].astype(BF16).astype(F32), WORD)
        hi = lax.bitcast_convert_type(h2[:, (c + TOKEN_ROWS) * LANES:(c + TOKEN_ROWS + 1) * LANES]
                                      .astype(BF16).astype(F32), WORD)
        h2t_ref[c] = jnp.bitwise_or(lax.shift_right_logical(lo, 16), jnp.bitwise_and(hi, HIGH_HALF))


def _out_proj(attn, ssm, x, gate, shift, scale, gain, w_out, b, s, tm=256):
    t = b * s
    d = D_MODEL
    tiles_per_seq = s // tm
    w = w_out.astype(BF16)
    row = lambda n: pl.BlockSpec((tm, n), lambda i: (i, 0))
    full = lambda shp: pl.BlockSpec(shp, lambda i: (0,) * len(shp))
    per_batch = pl.BlockSpec((None, 1, d), lambda i: (i // tiles_per_seq, 0, 0))
    return pl.pallas_call(
        _outproj_kernel,
        grid=(t // tm,),
        in_specs=[row(ATTN_WIDTH), row(SSM_WIDTH), row(d), per_batch, per_batch, per_batch, full((1, d)),
                  full((ATTN_WIDTH, d)), full((SSM_WIDTH, d))],
        out_specs=[row(d), row(d), pl.BlockSpec((TOKEN_ROWS, tm, LANES), lambda i: (0, i, 0))],
        out_shape=[jax.ShapeDtypeStruct((t, d), F32)] * 2 + [jax.ShapeDtypeStruct((TOKEN_ROWS, t, LANES), WORD)],
        compiler_params=_params(("arbitrary",)),
        name="out_proj",
    )(attn.reshape(t, ATTN_WIDTH), ssm, x.reshape(t, d),
      gate.reshape(b, 1, d), shift.reshape(b, 1, d), scale.reshape(b, 1, d), gain.reshape(1, d),
      w[:ATTN_WIDTH], w[ATTN_WIDTH:])


def _mixer_sublayer(x, mod, norm_mix_gain, w_in, q_norm_gain, k_norm_gain, rel_bias_table, conv_w, conv_b, dt_bias,
                    a_log, d_skip, ssm_norm_gain, w_out, norm_ffn_gain):
    b, s, d = x.shape
    shift_m, scale_m, gate_m, shift_f, scale_f, _ = jnp.split(mod, 6, axis=-1)
    q, k, v, z, xbc, dt_raw = _in_proj(x, shift_m, scale_m, norm_mix_gain, w_in, q_norm_gain, k_norm_gain)
    bias = jnp.stack([_window_bias(rel_bias_table, dilation) for _, dilation in PATTERNS])
    attn = _attention(q.reshape(b, s, ATTN_WIDTH), k.reshape(b, s, ATTN_WIDTH), v.reshape(b, s, ATTN_WIDTH), bias)
    ssm = _ssd(xbc, z, dt_raw, conv_w, conv_b, dt_bias, a_log, d_skip, ssm_norm_gain, b, s)
    return _out_proj(attn, ssm, x, gate_m, shift_f, scale_f, norm_ffn_gain, w_out, b, s)


def _first_argmax(v, iota, limit):
    m = jnp.max(v, axis=0, keepdims=True)
    idx = jnp.min(jnp.where(v == m, iota, limit), axis=0, keepdims=True)
    return m, idx


def _router_kernel(h_ref, wt_ref, bias_ref, upper_ref, eidx_ref, rank_ref, gate_ref, counts_ref, carry_ref):
    @pl.when(pl.program_id(0) == 0)
    def _():
        carry_ref[...] = jnp.zeros_like(carry_ref)

    tm = h_ref.shape[0]
    h = h_ref[...]
    wt = wt_ref[...]
    h_hi = h.astype(BF16)
    h_lo = (h - h_hi.astype(F32)).astype(BF16)
    w_hi = wt.astype(BF16)
    w_lo = (wt - w_hi.astype(F32)).astype(BF16)
    logits = _dot_nt(w_hi, h_hi) + _dot_nt(w_hi, h_lo) + _dot_nt(w_lo, h_hi)
    scores = _sigmoid(logits)
    choice = scores + bias_ref[...]
    neg_inf = -jnp.inf

    iota_g = lax.broadcasted_iota(I32, (EXPERTS_PER_GROUP, tm), 0).astype(F32)
    group_rows = []
    for g in range(N_EXPERT_GROUPS):
        v = choice[g * EXPERTS_PER_GROUP:(g + 1) * EXPERTS_PER_GROUP]
        m1, i1 = _first_argmax(v, iota_g, float(EXPERTS_PER_GROUP))
        m2 = jnp.max(jnp.where(iota_g == i1, neg_inf, v), axis=0, keepdims=True)
        group_rows.append(m1 + m2)
    group_scores = jnp.concatenate(group_rows, axis=0)

    iota_n = lax.broadcasted_iota(I32, (N_EXPERT_GROUPS, tm), 0).astype(F32)
    chosen = jnp.zeros((N_EXPERT_GROUPS, tm), F32)
    for _ in range(TOPK_GROUPS):
        _, gi = _first_argmax(group_scores, iota_n, float(N_EXPERT_GROUPS))
        hit = iota_n == gi
        chosen = jnp.where(hit, 1.0, chosen)
        group_scores = jnp.where(hit, neg_inf, group_scores)

    masked = jnp.concatenate(
        [jnp.where(chosen[g:g + 1] > 0.0, choice[g * EXPERTS_PER_GROUP:(g + 1) * EXPERTS_PER_GROUP], neg_inf)
         for g in range(N_EXPERT_GROUPS)], axis=0)

    iota_e = lax.broadcasted_iota(I32, (N_EXPERTS, tm), 0).astype(F32)
    picked, gates = [], []
    onehot = jnp.zeros((N_EXPERTS, tm), F32)
    for _ in range(TOP_K):
        _, ei = _first_argmax(masked, iota_e, float(N_EXPERTS))
        hit = iota_e == ei
        gates.append(jnp.sum(jnp.where(hit, scores, 0.0), axis=0, keepdims=True))
        masked = jnp.where(hit, neg_inf, masked)
        onehot = jnp.where(hit, 1.0, onehot)
        picked.append(ei)
    gate_sum = gates[0]
    for gk in gates[1:]:
        gate_sum = gate_sum + gk

    base = _dot(onehot.astype(BF16), upper_ref[...]) + carry_ref[...]
    ranks = [jnp.sum(jnp.where(iota_e == ei, base, 0.0), axis=0, keepdims=True) for ei in picked]
    carry_ref[...] = carry_ref[...] + jnp.sum(onehot, axis=1, keepdims=True)

    eidx_ref[...] = jnp.concatenate(picked, axis=0).astype(I32)
    rank_ref[...] = jnp.concatenate(ranks, axis=0).astype(I32)
    gate_ref[...] = jnp.concatenate([gk / gate_sum * ROUTED_SCALE for gk in gates], axis=0)
    counts_ref[...] = carry_ref[...].astype(I32)


def _router(h2, w_router, router_bias, tm=256):
    t, d = h2.shape
    upper = jnp.asarray(np.triu(np.ones((tm, tm), np.float32), 1), BF16)
    tok = pl.BlockSpec((TOP_K, tm), lambda i: (0, i))
    full = lambda shp: pl.BlockSpec(shp, lambda i: (0,) * len(shp))
    return pl.pallas_call(
        _router_kernel,
        grid=(t // tm,),
        in_specs=[pl.BlockSpec((tm, d), lambda i: (i, 0)), full((N_EXPERTS, d)), full((N_EXPERTS, 1)), full((tm, tm))],
        out_specs=[tok, tok, tok, full((N_EXPERTS, 1))],
        out_shape=[jax.ShapeDtypeStruct((TOP_K, t), I32), jax.ShapeDtypeStruct((TOP_K, t), I32),
                   jax.ShapeDtypeStruct((TOP_K, t), F32), jax.ShapeDtypeStruct((N_EXPERTS, 1), I32)],
        scratch_shapes=[pltpu.VMEM((N_EXPERTS, 1), F32)],
        compiler_params=_params(("arbitrary",)),
        name="router",
    )(h2, w_router.T, router_bias.reshape(N_EXPERTS, 1), upper)


def _positions_kernel(counts_ref, lower_ref, eidx_ref, rank_ref, pos_ref):
    tm = eidx_ref.shape[1]
    counts = jnp.broadcast_to(counts_ref[...].astype(F32), (N_EXPERTS, LANES))
    offsets = _dot_exact_lhs(lower_ref[...], counts)[:, 0:1]
    iota_e = lax.broadcasted_iota(I32, (N_EXPERTS, tm), 0).astype(F32)
    e = eidx_ref[...].astype(F32)
    rows = [jnp.sum(jnp.where(iota_e == e[k:k + 1], offsets, 0.0), axis=0, keepdims=True) for k in range(TOP_K)]
    pos_ref[...] = jnp.concatenate(rows, axis=0).astype(I32) + rank_ref[...]


def _positions(counts, eidx, rank, tm):
    t = eidx.shape[1]
    lower = jnp.asarray(np.tril(np.ones((N_EXPERTS, N_EXPERTS), np.float32), -1), BF16)
    tok = pl.BlockSpec((TOP_K, tm), lambda i: (0, i))
    return pl.pallas_call(
        _positions_kernel,
        grid=(t // tm,),
        in_specs=[pl.BlockSpec((N_EXPERTS, 1), lambda i: (0, 0)), pl.BlockSpec((N_EXPERTS, N_EXPERTS), lambda i: (0, 0)),
                  tok, tok],
        out_specs=tok,
        out_shape=jax.ShapeDtypeStruct((TOP_K, t), I32),
        compiler_params=_params(("arbitrary",)),
        name="positions",
    )(counts, lower, eidx, rank)


EXPERT_BLOCK = 512
TAIL_UNIT = 128
TAIL_PIECES = tuple(1 << i for i in reversed(range(EXPERT_BLOCK.bit_length() - 1)))
W_SLOTS = 3
X_SLOTS = 8
Y_SLOTS = 4


def _experts_kernel(start_ref, count_ref, nxt_ref, nxt2_ref, slot_ref, first_ref, blk0_ref, full0_ref, ptail_ref,
                    ltail_ref, blktok_ref, nblocks_ref, xs_hbm, wg_hbm, wu_hbm, wd_hbm, ys_hbm,
                    wg_buf, wu_buf, wd_buf, wg_bf, wu_bf, wd_bf, xbuf, ybuf, ytail, wsem, xsem, ysem, tsem):
    e = pl.program_id(0)
    last_step = e == pl.num_programs(0) - 1
    start, count = start_ref[e], count_ref[e]
    n_full = jnp.right_shift(count, EXPERT_BLOCK.bit_length() - 1)
    tail = jnp.bitwise_and(count, EXPERT_BLOCK - 1)
    n_blk = n_full + (tail > 0).astype(I32)
    blk0, full0 = blk0_ref[e], full0_ref[e]
    slot, nxt, nxt2 = slot_ref[e], nxt_ref[e], nxt2_ref[e]
    slot1 = jnp.where(slot + 1 >= W_SLOTS, slot + 1 - W_SLOTS, slot + 1)
    slot2 = jnp.where(slot + 2 >= W_SLOTS, slot + 2 - W_SLOTS, slot + 2)

    def token_rows(token, n):
        return pl.ds(pl.multiple_of(token * TOKEN_ROWS, TOKEN_ROWS), n * TOKEN_ROWS)

    def fetch(ex, s):
        return (pltpu.make_async_copy(wg_hbm.at[ex], wg_buf.at[s], wsem.at[s, 0]),
                pltpu.make_async_copy(wu_hbm.at[ex], wu_buf.at[s], wsem.at[s, 1]),
                pltpu.make_async_copy(wd_hbm.at[ex], wd_buf.at[s], wsem.at[s, 2]))

    def x_copy(token, s):
        return pltpu.make_async_copy(xs_hbm.at[token_rows(token, EXPERT_BLOCK)], xbuf.at[s], xsem.at[s])

    def y_copy(token, s):
        return pltpu.make_async_copy(ybuf.at[s], ys_hbm.at[token_rows(token, EXPERT_BLOCK)], ysem.at[s])

    def tail_copies(token, length):
        out = []
        for piece in TAIL_PIECES:
            bigger = (EXPERT_BLOCK - 1) & ~(2 * piece - 1)
            done = jnp.bitwise_and(length, bigger)
            cp = pltpu.make_async_copy(ytail.at[token_rows(done, piece)], ys_hbm.at[token_rows(token + done, piece)],
                                       tsem)
            out.append((jnp.bitwise_and(length, piece) != 0, cp))
        return out

    def block(s, rows=EXPERT_BLOCK):
        x = _from_token_tiles(xbuf.at[s], rows).astype(BF16)
        g = _dot(x, wg_bf[...])
        u = _dot(x, wu_bf[...])
        return _dot((_silu(g) * u).astype(BF16), wd_bf[...])

    @pl.when(count > 0)
    def _():
        @pl.when(first_ref[e] == 1)
        def _():
            for g in range(X_SLOTS - 1):
                @pl.when(g < nblocks_ref[0])
                def _(g=g):
                    x_copy(blktok_ref[g], g).start()
            for cp in fetch(e, slot):
                cp.start()

            @pl.when(nxt >= 0)
            def _():
                for cp in fetch(nxt, slot1):
                    cp.start()

        for cp in fetch(e, slot):
            cp.wait()

        @pl.when(nxt2 >= 0)
        def _():
            for cp in fetch(nxt2, slot2):
                cp.start()

        wg_bf[...] = wg_buf[slot].astype(BF16)
        wu_bf[...] = wu_buf[slot].astype(BF16)
        wd_bf[...] = wd_buf[slot].astype(BF16)

        def take_x(i):
            g = blk0 + i
            xs_slot = jnp.bitwise_and(g, X_SLOTS - 1)
            x_copy(start, xs_slot).wait()
            ahead = g + (X_SLOTS - 1)

            @pl.when(ahead < nblocks_ref[0])
            def _():
                x_copy(blktok_ref[ahead], jnp.bitwise_and(ahead, X_SLOTS - 1)).start()

            return xs_slot

        def full_block(i, carry):
            y = block(take_x(i))
            j = full0 + i
            ys_slot = jnp.bitwise_and(j, Y_SLOTS - 1)

            @pl.when(j >= Y_SLOTS)
            def _():
                y_copy(start, ys_slot).wait()

            _to_token_tiles(ybuf.at[ys_slot], y)
            y_copy(start + i * EXPERT_BLOCK, ys_slot).start()
            return carry

        lax.fori_loop(0, n_full, full_block, 0)

        @pl.when(tail > 0)
        def _():
            xs_slot = take_x(n_full)
            for pred, cp in tail_copies(start, ptail_ref[e]):
                @pl.when(pred)
                def _(cp=cp):
                    cp.wait()
            units = jnp.right_shift(tail + (TAIL_UNIT - 1), TAIL_UNIT.bit_length() - 1)
            for u in range(1, EXPERT_BLOCK // TAIL_UNIT + 1):
                @pl.when(units == u)
                def _(u=u):
                    rows = u * TAIL_UNIT
                    _to_token_tiles(ytail.at[pl.ds(0, rows * TOKEN_ROWS)], block(xs_slot, rows))
            for pred, cp in tail_copies(start + n_full * EXPERT_BLOCK, tail):
                @pl.when(pred)
                def _(cp=cp):
                    cp.start()

    @pl.when(last_step)
    def _():
        total_full = full0 + n_full
        for back in range(1, Y_SLOTS + 1):
            @pl.when(total_full >= back)
            def _(back=back):
                y_copy(0, jnp.bitwise_and(total_full - back, Y_SLOTS - 1)).wait()
        for pred, cp in tail_copies(0, ltail_ref[0]):
            @pl.when(pred)
            def _(cp=cp):
                cp.wait()


def _max_expert_blocks(n_rows):
    return n_rows // EXPERT_BLOCK + N_EXPERTS


def _expert_metadata(counts, n_rows):
    ids = jnp.arange(N_EXPERTS, dtype=I32)
    used = counts > 0
    starts = jnp.cumsum(counts) - counts
    n_blk = (counts + EXPERT_BLOCK - 1) // EXPERT_BLOCK
    n_full = counts // EXPERT_BLOCK
    tail = counts % EXPERT_BLOCK
    blk0 = jnp.cumsum(n_blk) - n_blk
    full0 = jnp.cumsum(n_full) - n_full
    next_used = lax.cummin(jnp.where(used, ids, N_EXPERTS), reverse=True)
    next_after = jnp.concatenate([next_used[1:], jnp.full((1,), N_EXPERTS, I32)])
    nxt = jnp.where(next_after < N_EXPERTS, next_after, -1)
    ordinal = jnp.cumsum(used.astype(I32)) - 1
    slot = ordinal % W_SLOTS
    first = jnp.logical_and(used, ordinal == 0)
    latest = lax.cummax(jnp.where(tail > 0, ids, -1))
    before = jnp.concatenate([jnp.full((1,), -1, I32), latest[:-1]])
    pick = lambda index, values: jnp.sum(jnp.where(index[:, None] == ids[None, :], values[None, :], 0), axis=1)
    nxt2 = jnp.where(nxt >= 0, pick(nxt, nxt + 1), 0) - 1
    ptail = pick(before, tail)
    ltail = pick(latest[-1:], tail)
    block_ends = jnp.cumsum(n_blk)
    g = jnp.arange(_max_expert_blocks(n_rows), dtype=I32)
    eg = jnp.sum((g[:, None] >= block_ends[None, :]).astype(I32), axis=1)
    blktok = g * EXPERT_BLOCK + pick(eg, starts - blk0 * EXPERT_BLOCK)
    return tuple(v.astype(I32) for v in (starts, counts, nxt, nxt2, slot, first, blk0, full0, ptail, ltail, blktok,
                                         block_ends[-1:]))


def _experts(xs, counts, w_gate, w_up, w_down):
    d = D_MODEL
    meta = _expert_metadata(counts, xs.shape[0] // TOKEN_ROWS - EXPERT_BLOCK)
    hbm = pl.BlockSpec(memory_space=pl.ANY)
    blk = (EXPERT_BLOCK * TOKEN_ROWS, LANES)
    grid_spec = pltpu.PrefetchScalarGridSpec(
        num_scalar_prefetch=len(meta),
        grid=(N_EXPERTS,),
        in_specs=[hbm, hbm, hbm, hbm],
        out_specs=hbm,
        scratch_shapes=[pltpu.VMEM((W_SLOTS, d, EXPERT_FF), F32), pltpu.VMEM((W_SLOTS, d, EXPERT_FF), F32),
                        pltpu.VMEM((W_SLOTS, EXPERT_FF, d), F32),
                        pltpu.VMEM((d, EXPERT_FF), BF16), pltpu.VMEM((d, EXPERT_FF), BF16),
                        pltpu.VMEM((EXPERT_FF, d), BF16),
                        pltpu.VMEM((X_SLOTS,) + blk, WORD), pltpu.VMEM((Y_SLOTS,) + blk, WORD), pltpu.VMEM(blk, WORD),
                        pltpu.SemaphoreType.DMA((W_SLOTS, 3)), pltpu.SemaphoreType.DMA((X_SLOTS,)),
                        pltpu.SemaphoreType.DMA((Y_SLOTS,)), pltpu.SemaphoreType.DMA(())],
    )
    return pl.pallas_call(
        _experts_kernel,
        grid_spec=grid_spec,
        out_shape=jax.ShapeDtypeStruct(xs.shape, WORD),
        compiler_params=_params(("arbitrary",)),
        name="experts",
    )(*meta, xs, w_gate, w_up, w_down)


SC_CORES = 2
SC_SUBCORES = 16
SC_WORKERS = SC_CORES * SC_SUBCORES
SC_CHUNK = 128
SC_RING = 4


def _sc_scatter_rows(planes, idx, n_out_rows):
    n_planes, t, _ = planes.shape
    ranges = SC_WORKERS // n_planes
    n_chunks = t // ranges // SC_CHUNK
    idx = idx.reshape(TOP_K, n_planes, ranges, n_chunks, SC_CHUNK)
    mesh = plsc.VectorSubcoreMesh(core_axis_name="c", subcore_axis_name="s", num_cores=SC_CORES,
                                  num_subcores=SC_SUBCORES)

    def body(planes_hbm, idx_hbm, out_hbm, idx_v, rows_v, lsem, ssem):
        wid = lax.axis_index("s") * SC_CORES + lax.axis_index("c")
        plane = wid % n_planes
        token0 = (wid // n_planes) * (n_chunks * SC_CHUNK)
        for k in range(TOP_K):
            pltpu.sync_copy(idx_hbm.at[k, plane, wid // n_planes], idx_v.at[k])

        def load(slot, c):
            return pltpu.make_async_copy(planes_hbm.at[plane, pl.ds(token0 + c * SC_CHUNK, SC_CHUNK)], rows_v.at[slot],
                                         lsem.at[slot])

        def scatter(slot, c, k):
            return pltpu.make_async_copy(rows_v.at[slot], out_hbm.at[idx_v.at[k, c]], ssem.at[slot])

        @pl.loop(0, n_chunks, step=SC_RING)
        def _(g):
            for slot in range(SC_RING):
                @pl.when(g > 0)
                def _(slot=slot):
                    for k in range(TOP_K):
                        scatter(slot, 0, k).wait()
                load(slot, g + slot).start()
            for slot in range(SC_RING):
                load(slot, g + slot).wait()
                for k in range(TOP_K):
                    scatter(slot, g + slot, k).start()

        for slot in range(SC_RING):
            for k in range(TOP_K):
                scatter(slot, 0, k).wait()

    return pl.kernel(
        body, mesh=mesh,
        out_type=jax.ShapeDtypeStruct((n_out_rows, LANES), planes.dtype),
        scratch_types=[pltpu.VMEM((TOP_K, n_chunks, SC_CHUNK), I32), pltpu.VMEM((SC_RING, SC_CHUNK, LANES), planes.dtype),
                       pltpu.SemaphoreType.DMA((SC_RING,)), pltpu.SemaphoreType.DMA((SC_RING,))],
        name="sc_scatter",
    )(planes, idx)


def _sc_gather_rows(table, idx):
    n_workers, n_chunks, chunk = idx.shape
    rows_per_worker = n_chunks * chunk
    mesh = plsc.VectorSubcoreMesh(core_axis_name="c", subcore_axis_name="s", num_cores=SC_CORES,
                                  num_subcores=SC_SUBCORES)

    def body(table_hbm, idx_hbm, out_hbm, idx_v, rows_v, gsem, wsem):
        wid = lax.axis_index("s") * SC_CORES + lax.axis_index("c")
        base = wid * rows_per_worker
        pltpu.sync_copy(idx_hbm.at[wid], idx_v)

        def write(slot, c):
            return pltpu.make_async_copy(rows_v.at[slot], out_hbm.at[pl.ds(base + c * chunk, chunk)], wsem.at[slot])

        def gather(slot, c):
            return pltpu.make_async_copy(table_hbm.at[idx_v.at[c]], rows_v.at[slot], gsem.at[slot])

        @pl.loop(0, n_chunks, step=SC_RING)
        def _(g):
            for slot in range(SC_RING):
                @pl.when(g > 0)
                def _(slot=slot):
                    write(slot, 0).wait()
                gather(slot, g + slot).start()
            for slot in range(SC_RING):
                gather(slot, g + slot).wait()
                write(slot, g + slot).start()

        for slot in range(SC_RING):
            write(slot, 0).wait()

    return pl.kernel(
        body, mesh=mesh,
        out_type=jax.ShapeDtypeStruct((n_workers * rows_per_worker, LANES), table.dtype),
        scratch_types=[pltpu.VMEM((n_chunks, chunk), I32), pltpu.VMEM((SC_RING, chunk, LANES), table.dtype),
                       pltpu.SemaphoreType.DMA((SC_RING,)), pltpu.SemaphoreType.DMA((SC_RING,))],
        name="sc_gather",
    )(table, idx)


def _combine_kernel(rows_ref, gates_ref, h_ref, x1_ref, gatef_ref, wg_ref, wu_ref, wd_ref, out_ref):
    hb = h_ref[...].astype(BF16)
    shared = _dot((_silu(_dot(hb, wg_ref[...])) * _dot(hb, wu_ref[...])).astype(BF16), wd_ref[...])
    gates = gates_ref[...]

    def expert_rows(k):
        words = [rows_ref[k * TOKEN_ROWS + c] for c in range(TOKEN_ROWS)]
        lows = [lax.bitcast_convert_type(jnp.left_shift(w, 16), F32) for w in words]
        highs = [lax.bitcast_convert_type(jnp.bitwise_and(w, HIGH_HALF), F32) for w in words]
        return jnp.concatenate(lows + highs, axis=1)

    routed = expert_rows(0) * gates[:, 0:1]
    for k in range(1, TOP_K):
        routed = routed + expert_rows(k) * gates[:, k:k + 1]
    out_ref[...] = x1_ref[...] + gatef_ref[...] * (shared + routed)


def _combine(gathered, gates_t, h2, x1, gate_f, w_gate_s, w_up_s, w_down_s, b, s, tm):
    t, d = h2.shape
    tiles_per_seq = s // tm
    row = lambda n: pl.BlockSpec((tm, n), lambda i: (i, 0))
    full = lambda shp: pl.BlockSpec(shp, lambda i: (0,) * len(shp))
    return pl.pallas_call(
        _combine_kernel,
        grid=(t // tm,),
        in_specs=[pl.BlockSpec((TOP_K * TOKEN_ROWS, tm, LANES), lambda i: (0, i, 0)),
                  row(TOP_K), row(d), row(d),
                  pl.BlockSpec((None, 1, d), lambda i: (i // tiles_per_seq, 0, 0)),
                  full((d, EXPERT_FF)), full((d, EXPERT_FF)), full((EXPERT_FF, d))],
        out_specs=row(d),
        out_shape=jax.ShapeDtypeStruct((t, d), F32),
        compiler_params=_params(("arbitrary",)),
        name="combine",
    )(gathered, gates_t, h2, x1, gate_f.reshape(b, 1, d),
      w_gate_s.astype(BF16), w_up_s.astype(BF16), w_down_s.astype(BF16))


def _moe_sublayer(x1, h2, h2t, gate_f, w_router, router_bias, w_gate, w_up, w_down, w_gate_s, w_up_s, w_down_s, b, s,
                  tm=256):
    t = b * s
    eidx, rank, gates, counts = _router(h2, w_router, router_bias)
    pos = _positions(counts, eidx, rank, tm)
    idx = pos[:, None, :] * TOKEN_ROWS + jnp.arange(TOKEN_ROWS, dtype=I32)[None, :, None]
    xs = _sc_scatter_rows(h2t, idx, (t * TOP_K + EXPERT_BLOCK) * TOKEN_ROWS)
    ys = _experts(xs, counts[:, 0], w_gate, w_up, w_down)
    gathered = _sc_gather_rows(ys, idx.reshape(SC_WORKERS, -1, SC_CHUNK)).reshape(TOP_K * TOKEN_ROWS, t, LANES)
    return _combine(gathered, gates.T, h2, x1, gate_f, w_gate_s, w_up_s, w_down_s, b, s, tm)


def kernel(x, c, w_ada, b_ada, norm_mix_gain, w_in, q_norm_gain, k_norm_gain, rel_bias_table, conv_w, conv_b, dt_bias,
           a_log, d_skip, ssm_norm_gain, w_out, norm_ffn_gain, w_router, router_bias, w_gate_experts, w_up_experts,
           w_down_experts, w_gate_shared, w_up_shared, w_down_shared):
    b, s, d = x.shape
    for layer in range(w_ada.shape[0]):
        mod = _adaln(c, w_ada[layer], b_ada[layer])
        x1, h2, h2t = _mixer_sublayer(x, mod, norm_mix_gain[layer], w_in[layer], q_norm_gain[layer], k_norm_gain[layer],
                                 rel_bias_table, conv_w[layer], conv_b[layer], dt_bias[layer], a_log[layer],
                                 d_skip[layer], ssm_norm_gain[layer], w_out[layer], norm_ffn_gain[layer])
        gate_f = mod[:, 5 * d:]
        out = _moe_sublayer(x1, h2, h2t, gate_f, w_router[layer], router_bias[layer], w_gate_experts[layer],
                            w_up_experts[layer], w_down_experts[layer], w_gate_shared[layer], w_up_shared[layer],
                            w_down_shared[layer], b, s)
        x = out.reshape(b, s, d)
    return x
```

```python
import functools
import math

import numpy as np
import jax
import jax.numpy as jnp
from jax import lax
from jax.experimental import pallas as pl
from jax.experimental.pallas import tpu as pltpu
from jax.experimental.pallas import tpu_sc as plsc

F32 = jnp.float32
BF16 = jnp.bfloat16
I32 = jnp.int32

D_MODEL = 1024
ATTN_HEADS = 8
HEAD_DIM = 64
ATTN_WIDTH = ATTN_HEADS * HEAD_DIM
PATTERNS = ((128, 1), (512, 4), (2048, 16))
WIN_STEPS = 128
REL_BUCKETS = 32
REL_MAX_DISTANCE = 2048
SSM_HEADS = 24
SSM_HEAD_DIM = 64
SSM_WIDTH = SSM_HEADS * SSM_HEAD_DIM
SSM_GROUPS = 4
HEADS_PER_GROUP = SSM_HEADS // SSM_GROUPS
GROUP_WIDTH = SSM_WIDTH // SSM_GROUPS
SSM_STATE = 128
SSM_CONV = 4
SSM_CHUNK = 128
CONV_CH = SSM_WIDTH + 2 * SSM_GROUPS * SSM_STATE
N_EXPERTS = 256
TOP_K = 8
N_EXPERT_GROUPS = 8
EXPERTS_PER_GROUP = N_EXPERTS // N_EXPERT_GROUPS
TOPK_GROUPS = 4
EXPERT_FF = 256
ROUTED_SCALE = 2.5
NORM_EPS = 1e-6

LANES = 128
SUBLANES = 8
NEG_BIG = -1e30
VMEM_LIMIT = 56 * 1024 * 1024


def _params(sem, vmem=VMEM_LIMIT):
    return pltpu.CompilerParams(dimension_semantics=sem, vmem_limit_bytes=vmem)


def _sigmoid(x):
    return 1.0 / (1.0 + jnp.exp(-x))


def _silu(x):
    return x * _sigmoid(x)


def _split3(x):
    hi = x.astype(BF16)
    r = x - hi.astype(F32)
    mid = r.astype(BF16)
    lo = (r - mid.astype(F32)).astype(BF16)
    return hi, mid, lo


def _dot(a, b):
    return jnp.dot(a, b, preferred_element_type=F32)


def _dot_nt(a, b):
    return lax.dot_general(a, b, (((1,), (1,)), ((), ())), preferred_element_type=F32)


def _dot_exact_rhs(a, b_exact):
    hi, mid, lo = _split3(a)
    return _dot(hi, b_exact) + _dot(mid, b_exact) + _dot(lo, b_exact)


def _dot_exact_lhs(a_exact, b):
    hi, mid, lo = _split3(b)
    return _dot(a_exact, hi) + _dot(a_exact, mid) + _dot(a_exact, lo)


def _adaln_kernel(c_ref, w_ref, b_ref, o_ref):
    s = _silu(c_ref[...]).astype(BF16)
    o_ref[...] = _dot(s, w_ref[...].astype(BF16)) + b_ref[...]


def _adaln(c, w_ada, b_ada):
    b, d = c.shape
    n = w_ada.shape[1]
    rows = SUBLANES
    c_pad = jnp.zeros((rows, d), F32).at[:b].set(c)
    tn = 1024
    out = pl.pallas_call(
        _adaln_kernel,
        grid=(n // tn,),
        in_specs=[pl.BlockSpec((rows, d), lambda j: (0, 0)),
                  pl.BlockSpec((d, tn), lambda j: (0, j)),
                  pl.BlockSpec((1, tn), lambda j: (0, j))],
        out_specs=pl.BlockSpec((rows, tn), lambda j: (0, j)),
        out_shape=jax.ShapeDtypeStruct((rows, n), F32),
        compiler_params=_params(("arbitrary",)),
        name="adaln",
    )(c_pad, w_ada, b_ada.reshape(1, n))
    return out[:b]


def _inproj_kernel(x_ref, shift_ref, scale_ref, g_ref, wqkv_ref, wz_ref, wxbc_ref, wdt_ref,
                   qg_ref, kg_ref, hmean_ref, q_ref, k_ref, v_ref, z_ref, xbc_ref, dt_ref):
    x = x_ref[...]
    ms = jnp.mean(x * x, axis=-1, keepdims=True)
    h = x * lax.rsqrt(ms + NORM_EPS) * g_ref[...]
    h = h * (1.0 + scale_ref[...]) + shift_ref[...]
    hb = h.astype(BF16)

    hmean = hmean_ref[...]

    def head_norm(t, gain):
        ss = _dot_exact_rhs(t * t, hmean)
        return t * lax.rsqrt(ss + NORM_EPS) * gain

    q = _dot(hb, wqkv_ref[:, 0:ATTN_WIDTH])
    q_ref[...] = head_norm(q, qg_ref[...]) * (HEAD_DIM ** -0.5)
    k = _dot(hb, wqkv_ref[:, ATTN_WIDTH:2 * ATTN_WIDTH])
    k_ref[...] = head_norm(k, kg_ref[...])
    v_ref[...] = _dot(hb, wqkv_ref[:, 2 * ATTN_WIDTH:3 * ATTN_WIDTH])
    for c0 in range(0, SSM_WIDTH, 512):
        z_ref[:, c0:c0 + 512] = _dot(hb, wz_ref[:, c0:c0 + 512])
    for c0 in range(0, CONV_CH, 512):
        xbc_ref[:, c0:c0 + 512] = _dot(hb, wxbc_ref[:, c0:c0 + 512])
    dt_ref[...] = _dot(hb, wdt_ref[...])


def _in_proj(x, shift, scale, gain, w_in, q_gain, k_gain, tm=256):
    b, s, d = x.shape
    t = b * s
    tiles_per_seq = s // tm
    w = w_in.astype(BF16)
    o_z = 3 * ATTN_WIDTH
    o_x = o_z + SSM_WIDTH
    o_dt = o_x + CONV_CH
    w_qkv, w_z, w_xbc = w[:, :o_z], w[:, o_z:o_x], w[:, o_x:o_dt]
    w_dt = jnp.zeros((d, LANES), BF16).at[:, :SSM_HEADS].set(w[:, o_dt:])
    head_of = np.arange(ATTN_WIDTH) // HEAD_DIM
    hmean = jnp.asarray((head_of[:, None] == head_of[None, :]).astype(np.float32) / HEAD_DIM, BF16)
    full = lambda shp: pl.BlockSpec(shp, lambda i: (0,) * len(shp))
    row = lambda n: pl.BlockSpec((tm, n), lambda i: (i, 0))
    per_batch = pl.BlockSpec((None, 1, d), lambda i: (i // tiles_per_seq, 0, 0))
    outs = pl.pallas_call(
        _inproj_kernel,
        grid=(t // tm,),
        in_specs=[row(d), per_batch, per_batch, full((1, d)),
                  full((d, o_z)), full((d, SSM_WIDTH)), full((d, CONV_CH)), full((d, LANES)),
                  full((1, ATTN_WIDTH)), full((1, ATTN_WIDTH)), full((ATTN_WIDTH, ATTN_WIDTH))],
        out_specs=[row(ATTN_WIDTH), row(ATTN_WIDTH), row(ATTN_WIDTH), row(SSM_WIDTH), row(CONV_CH), row(LANES)],
        out_shape=[jax.ShapeDtypeStruct((t, n), F32)
                   for n in (ATTN_WIDTH, ATTN_WIDTH, ATTN_WIDTH, SSM_WIDTH, CONV_CH, LANES)],
        compiler_params=_params(("arbitrary",)),
        name="in_proj",
    )(x.reshape(t, d), shift.reshape(b, 1, d), scale.reshape(b, 1, d), gain.reshape(1, d),
      w_qkv, w_z, w_xbc, w_dt,
      jnp.tile(q_gain, ATTN_HEADS).reshape(1, ATTN_WIDTH), jnp.tile(k_gain, ATTN_HEADS).reshape(1, ATTN_WIDTH), hmean)
    return outs


def _t5_causal_buckets(distance):
    n = np.maximum(distance, 0)
    max_exact = REL_BUCKETS // 2
    large = max_exact + (np.log(np.maximum(n, 1) / max_exact) / math.log(REL_MAX_DISTANCE / max_exact)
                         * (REL_BUCKETS - max_exact)).astype(np.int64)
    large = np.minimum(large, REL_BUCKETS - 1)
    return np.where(n < max_exact, n, large).astype(np.int32)


def _window_bias(rel_bias_table, dilation):
    qi = np.arange(WIN_STEPS)[:, None]
    kj = np.arange(2 * WIN_STEPS)[None, :]
    dist = qi + WIN_STEPS - kj
    band = (dist >= 0) & (dist <= WIN_STEPS)
    onehot = (_t5_causal_buckets(dist * dilation).reshape(-1, 1) == np.arange(REL_BUCKETS)[None, :]).astype(np.float32)
    bias = jnp.dot(rel_bias_table.astype(F32).T, jnp.asarray(onehot).T, precision=lax.Precision.HIGHEST)
    bias = bias.reshape(ATTN_HEADS, WIN_STEPS, 2 * WIN_STEPS)
    return jnp.where(jnp.asarray(band)[None], bias, NEG_BIG)


ATTN_TOKENS = max(w for w, _ in PATTERNS)
ATTN_UNROLL = 16


def _attn_kernel(q_ref, kp_ref, kc_ref, vp_ref, vc_ref, bias_ref, out_ref, kw, vw, o_acc, l_acc):
    tb = ATTN_TOKENS
    first = pl.program_id(2) == 0
    kw[0:tb] = kp_ref[...]
    kw[tb:2 * tb] = kc_ref[...]
    vw[0:tb] = vp_ref[...]
    vw[tb:2 * tb] = vc_ref[...]
    lane = lax.broadcasted_iota(I32, (WIN_STEPS, LANES), 1)
    head0 = lane < HEAD_DIM
    col = lax.broadcasted_iota(I32, (WIN_STEPS, 2 * WIN_STEPS), 1)
    in_prev = col < WIN_STEPS

    for p, (_, d) in enumerate(PATTERNS):
        shift = d.bit_length() - 1
        n_blocks = tb // WIN_STEPS

        def rows(start, n, d=d):
            return pl.ds(start, n, stride=d) if d > 1 else pl.ds(start, n)

        def body(it, carry, p=p, d=d, shift=shift, rows=rows):
            for u in range(ATTN_UNROLL):
                idx = it * ATTN_UNROLL + u
                r = jnp.bitwise_and(idx, d - 1)
                j = jnp.right_shift(idx, shift)
                qs = j * (WIN_STEPS * d) + r
                q = q_ref[rows(qs, WIN_STEPS), :]
                k = kw[rows(tb + qs - WIN_STEPS * d, 2 * WIN_STEPS), :].astype(BF16)
                v = vw[rows(tb + qs - WIN_STEPS * d, 2 * WIN_STEPS), :].astype(BF16)
                no_prev = jnp.logical_and(in_prev, jnp.logical_and(first, j == 0))
                o_h, lse_h = [], []
                for h in range(2):
                    qh = jnp.where(head0 if h == 0 else jnp.logical_not(head0), q, 0.0).astype(BF16)
                    s = _dot_nt(qh, k) + bias_ref[p, h]
                    s = jnp.where(no_prev, NEG_BIG, s)
                    m = jnp.max(s, axis=-1, keepdims=True)
                    e = jnp.exp(s - m)
                    denom = jnp.sum(e, axis=-1, keepdims=True)
                    o_h.append(_dot(e.astype(BF16), v) / denom)
                    lse_h.append(m + jnp.log(denom))
                o_acc[p, rows(qs, WIN_STEPS), :] = jnp.where(head0, o_h[0], o_h[1])
                l_acc[p, rows(qs, WIN_STEPS), :] = jnp.where(head0, lse_h[0], lse_h[1])
            return carry

        lax.fori_loop(0, n_blocks // ATTN_UNROLL, body, 0)

    chunk = 256
    for c0 in range(0, tb, chunk):
        l1, l2, l3 = (l_acc[p, c0:c0 + chunk, :] for p in range(3))
        m = jnp.maximum(jnp.maximum(l1, l2), l3)
        e1, e2, e3 = jnp.exp(l1 - m), jnp.exp(l2 - m), jnp.exp(l3 - m)
        num = e1 * o_acc[0, c0:c0 + chunk, :] + e2 * o_acc[1, c0:c0 + chunk, :] + e3 * o_acc[2, c0:c0 + chunk, :]
        out_ref[c0:c0 + chunk, :] = num / (e1 + e2 + e3)


def _attention(q, k, v, bias):
    b, s, w = q.shape
    tb = ATTN_TOKENS
    pairs = ATTN_HEADS // 2
    cur = pl.BlockSpec((None, tb, LANES), lambda bi, hp, i: (bi, i, hp))
    prev = pl.BlockSpec((None, tb, LANES), lambda bi, hp, i: (bi, jnp.maximum(i - 1, 0), hp))
    return pl.pallas_call(
        _attn_kernel,
        grid=(b, pairs, s // tb),
        in_specs=[cur, prev, cur, prev, cur,
                  pl.BlockSpec((len(PATTERNS), 2, WIN_STEPS, 2 * WIN_STEPS), lambda bi, hp, i: (0, hp, 0, 0))],
        out_specs=cur,
        out_shape=jax.ShapeDtypeStruct((b, s, w), F32),
        scratch_shapes=[pltpu.VMEM((2 * tb, LANES), F32), pltpu.VMEM((2 * tb, LANES), F32),
                        pltpu.VMEM((len(PATTERNS), tb, LANES), F32), pltpu.VMEM((len(PATTERNS), tb, LANES), F32)],
        compiler_params=_params(("arbitrary",) * 3),
        name="attention",
    )(q, k, k, v, v, bias)


SSD_CHUNKS_PER_STEP = 2


def _ssd_kernel(xbc_ref, halo_ref, z_ref, dtraw_ref, convw_ref, convb_ref, dtb_ref, alog_ref, dskip_ref, gain_ref,
                expand_ref, tril_ref, y_ref, state_ref):
    c = pl.program_id(1)

    @pl.when(c == 0)
    def _():
        state_ref[...] = jnp.zeros_like(state_ref)

    halo = jnp.where(c == 0, 0.0, halo_ref[...])
    for sub in range(SSD_CHUNKS_PER_STEP):
        rows = slice(sub * SSM_CHUNK, (sub + 1) * SSM_CHUNK)
        x = xbc_ref[rows, :]
        w = convw_ref[...]
        acc = x * w[SSM_CONV - 1:SSM_CONV, :] + convb_ref[...]
        row8 = lax.broadcasted_iota(I32, (SUBLANES, CONV_CH), 0)
        for shift in range(1, SSM_CONV):
            xs = pltpu.roll(x, shift, axis=0)
            hs = pltpu.roll(halo, shift, axis=0)
            head = jnp.where(row8 < shift, hs, xs[0:SUBLANES])
            xs = jnp.concatenate([head, xs[SUBLANES:]], axis=0)
            acc = acc + xs * w[SSM_CONV - 1 - shift:SSM_CONV - shift, :]
        act = _silu(acc)
        x_s = act[:, :SSM_WIDTH]
        bc0 = SSM_WIDTH
        cc0 = SSM_WIDTH + SSM_GROUPS * SSM_STATE

        t = dtraw_ref[rows, :] + dtb_ref[...]
        dt = jnp.maximum(t, 0.0) + jnp.log(1.0 + jnp.exp(-jnp.abs(t)))
        a = dt * (-jnp.exp(alog_ref[...]))
        a_cs = _dot_exact_lhs(tril_ref[...], a)
        a_cs_t = a_cs.T
        a_last = a_cs[SSM_CHUNK - 1:SSM_CHUNK, :]
        expand = expand_ref[...]
        dt_e = _dot_exact_rhs(dt, expand)
        ea_e = _dot_exact_rhs(jnp.exp(a_cs), expand)
        dte_e = _dot_exact_rhs(jnp.exp(a_last - a_cs), expand)
        xdt = x_s * dt_e
        xw = (xdt * dte_e).astype(BF16)
        xdt_b = xdt.astype(BF16)

        li = lax.broadcasted_iota(I32, (SSM_CHUNK, SSM_CHUNK), 0)
        si = lax.broadcasted_iota(I32, (SSM_CHUNK, SSM_CHUNK), 1)
        causal = li >= si

        ys = []
        for g in range(SSM_GROUPS):
            gs = slice(g * GROUP_WIDTH, (g + 1) * GROUP_WIDTH)
            b_g = act[:, bc0 + g * SSM_STATE:bc0 + (g + 1) * SSM_STATE]
            c_g = act[:, cc0 + g * SSM_STATE:cc0 + (g + 1) * SSM_STATE].astype(BF16)
            cb = _dot_nt(c_g, b_g.astype(BF16))
            state = state_ref[g]
            y_off = _dot(c_g, state.astype(BF16)) * ea_e[:, gs]
            parts = []
            for j in range(HEADS_PER_GROUP):
                hh = g * HEADS_PER_GROUP + j
                seg = a_cs[:, hh:hh + 1] - a_cs_t[hh:hh + 1, :]
                decay = jnp.exp(jnp.where(causal, seg, NEG_BIG))
                m = (cb * decay).astype(BF16)
                parts.append(_dot(m, xdt_b[:, hh * SSM_HEAD_DIM:(hh + 1) * SSM_HEAD_DIM]))
            ys.append(jnp.concatenate(parts, axis=1) + y_off)
            state_ref[g] = state * ea_e[SSM_CHUNK - 1:SSM_CHUNK, gs] + _dot(b_g.T.astype(BF16), xw[:, gs])
        y = jnp.concatenate(ys, axis=1) + dskip_ref[...] * x_s
        y = y * _silu(z_ref[rows, :])
        gain = gain_ref[...]
        for g in range(SSM_GROUPS):
            gs = slice(g * GROUP_WIDTH, (g + 1) * GROUP_WIDTH)
            yg = y[:, gs]
            ms = jnp.mean(yg * yg, axis=-1, keepdims=True)
            y_ref[rows, gs] = yg * lax.rsqrt(ms + NORM_EPS) * gain[:, gs]
        halo = x[SSM_CHUNK - SUBLANES:, :]


def _ssd(xbc, z, dt_raw, conv_w, conv_b, dt_bias, a_log, d_skip, norm_gain, b, s):
    t = b * s
    step_rows = SSD_CHUNKS_PER_STEP * SSM_CHUNK
    nc = s // step_rows
    pad_heads = lambda v: jnp.zeros((1, LANES), F32).at[0, :SSM_HEADS].set(v)
    head_of_lane = np.arange(SSM_WIDTH) // SSM_HEAD_DIM
    expand = jnp.asarray((np.arange(LANES)[:, None] == head_of_lane[None, :]).astype(np.float32), BF16)
    tril = jnp.asarray(np.tril(np.ones((SSM_CHUNK, SSM_CHUNK), np.float32)), BF16)
    halo_blocks = step_rows // SUBLANES
    chunk = lambda n: pl.BlockSpec((step_rows, n), lambda bi, c: (bi * nc + c, 0))
    full = lambda shp: pl.BlockSpec(shp, lambda bi, c: (0,) * len(shp))
    halo = pl.BlockSpec((SUBLANES, CONV_CH), lambda bi, c: (jnp.maximum((bi * nc + c) * halo_blocks - 1, 0), 0))
    return pl.pallas_call(
        _ssd_kernel,
        grid=(b, nc),
        in_specs=[chunk(CONV_CH), halo, chunk(SSM_WIDTH), chunk(LANES),
                  full((SSM_CONV, CONV_CH)), full((1, CONV_CH)), full((1, LANES)), full((1, LANES)),
                  full((1, SSM_WIDTH)), full((1, SSM_WIDTH)), full((LANES, SSM_WIDTH)), full((SSM_CHUNK, SSM_CHUNK))],
        out_specs=chunk(SSM_WIDTH),
        out_shape=jax.ShapeDtypeStruct((t, SSM_WIDTH), F32),
        scratch_shapes=[pltpu.VMEM((SSM_GROUPS, SSM_STATE, GROUP_WIDTH), F32)],
        compiler_params=_params(("arbitrary", "arbitrary")),
        name="ssd",
    )(xbc, xbc, z, dt_raw, conv_w, conv_b.reshape(1, CONV_CH), pad_heads(dt_bias), pad_heads(a_log),
      jnp.repeat(d_skip, SSM_HEAD_DIM).reshape(1, SSM_WIDTH), norm_gain.reshape(1, SSM_WIDTH), expand, tril)


WORD = jnp.int32
TOKEN_ROWS = D_MODEL // (2 * LANES)
HIGH_HALF = np.int32(-65536)


def _to_token_tiles(ref, x):
    n = x.shape[0]
    for c in range(TOKEN_ROWS):
        lo = lax.bitcast_convert_type(x[:, c * LANES:(c + 1) * LANES].astype(BF16).astype(F32), WORD)
        hi = lax.bitcast_convert_type(x[:, (c + TOKEN_ROWS) * LANES:(c + TOKEN_ROWS + 1) * LANES]
                                      .astype(BF16).astype(F32), WORD)
        ref[pl.ds(c, n, stride=TOKEN_ROWS), :] = jnp.bitwise_or(lax.shift_right_logical(lo, 16),
                                                                 jnp.bitwise_and(hi, HIGH_HALF))


def _from_token_tiles(ref, n, token0=0):
    lows, highs = [], []
    for c in range(TOKEN_ROWS):
        word = ref[pl.ds(token0 * TOKEN_ROWS + c, n, stride=TOKEN_ROWS), :]
        lows.append(lax.bitcast_convert_type(jnp.left_shift(word, 16), F32))
        highs.append(lax.bitcast_convert_type(jnp.bitwise_and(word, HIGH_HALF), F32))
    return jnp.concatenate(lows + highs, axis=1)


def _outproj_kernel(attn_ref, ssm_ref, x_ref, gate_ref, shift_ref, scale_ref, g_ref, wa_ref, ws_ref, x1_ref, h2_ref,
                    h2t_ref):
    mixed = _dot(attn_ref[...].astype(BF16), wa_ref[...]) + _dot(ssm_ref[...].astype(BF16), ws_ref[...])
    x1 = x_ref[...] + gate_ref[...] * mixed
    x1_ref[...] = x1
    ms = jnp.mean(x1 * x1, axis=-1, keepdims=True)
    h = x1 * lax.rsqrt(ms + NORM_EPS) * g_ref[...]
    h2 = h * (1.0 + scale_ref[...]) + shift_ref[...]
    h2_ref[...] = h2
    for c in range(TOKEN_ROWS):
        lo = lax.bitcast_convert_type(h2[:, c * LANES:(c + 1) * LANES].astype(BF16).astype(F32), WORD)
        hi = lax.bitcast_convert_type(h2[:, (c + TOKEN_ROWS) * LANES:(c + TOKEN_ROWS + 1) * LANES]
                                      .astype(BF16).astype(F32), WORD)
        h2t_ref[c] = jnp.bitwise_or(lax.shift_right_logical(lo, 16), jnp.bitwise_and(hi, HIGH_HALF))


def _out_proj(attn, ssm, x, gate, shift, scale, gain, w_out, b, s, tm=256):
    t = b * s
    d = D_MODEL
    tiles_per_seq = s // tm
    w = w_out.astype(BF16)
    row = lambda n: pl.BlockSpec((tm, n), lambda i: (i, 0))
    full = lambda shp: pl.BlockSpec(shp, lambda i: (0,) * len(shp))
    per_batch = pl.BlockSpec((None, 1, d), lambda i: (i // tiles_per_seq, 0, 0))
    return pl.pallas_call(
        _outproj_kernel,
        grid=(t // tm,),
        in_specs=[row(ATTN_WIDTH), row(SSM_WIDTH), row(d), per_batch, per_batch, per_batch, full((1, d)),
                  full((ATTN_WIDTH, d)), full((SSM_WIDTH, d))],
        out_specs=[row(d), row(d), pl.BlockSpec((TOKEN_ROWS, tm, LANES), lambda i: (0, i, 0))],
        out_shape=[jax.ShapeDtypeStruct((t, d), F32)] * 2 + [jax.ShapeDtypeStruct((TOKEN_ROWS, t, LANES), WORD)],
        compiler_params=_params(("arbitrary",)),
        name="out_proj",
    )(attn.reshape(t, ATTN_WIDTH), ssm, x.reshape(t, d),
      gate.reshape(b, 1, d), shift.reshape(b, 1, d), scale.reshape(b, 1, d), gain.reshape(1, d),
      w[:ATTN_WIDTH], w[ATTN_WIDTH:])


def _mixer_sublayer(x, mod, norm_mix_gain, w_in, q_norm_gain, k_norm_gain, rel_bias_table, conv_w, conv_b, dt_bias,
                    a_log, d_skip, ssm_norm_gain, w_out, norm_ffn_gain):
    b, s, d = x.shape
    shift_m, scale_m, gate_m, shift_f, scale_f, _ = jnp.split(mod, 6, axis=-1)
    q, k, v, z, xbc, dt_raw = _in_proj(x, shift_m, scale_m, norm_mix_gain, w_in, q_norm_gain, k_norm_gain)
    bias = jnp.stack([_window_bias(rel_bias_table, dilation) for _, dilation in PATTERNS])
    attn = _attention(q.reshape(b, s, ATTN_WIDTH), k.reshape(b, s, ATTN_WIDTH), v.reshape(b, s, ATTN_WIDTH), bias)
    ssm = _ssd(xbc, z, dt_raw, conv_w, conv_b, dt_bias, a_log, d_skip, ssm_norm_gain, b, s)
    return _out_proj(attn, ssm, x, gate_m, shift_f, scale_f, norm_ffn_gain, w_out, b, s)


def _first_argmax(v, iota, limit):
    m = jnp.max(v, axis=0, keepdims=True)
    idx = jnp.min(jnp.where(v == m, iota, limit), axis=0, keepdims=True)
    return m, idx


def _router_kernel(h_ref, wt_ref, bias_ref, upper_ref, eidx_ref, rank_ref, gate_ref, counts_ref, carry_ref):
    @pl.when(pl.program_id(0) == 0)
    def _():
        carry_ref[...] = jnp.zeros_like(carry_ref)

    tm = h_ref.shape[0]
    h = h_ref[...]
    wt = wt_ref[...]
    h_hi = h.astype(BF16)
    h_lo = (h - h_hi.astype(F32)).astype(BF16)
    w_hi = wt.astype(BF16)
    w_lo = (wt - w_hi.astype(F32)).astype(BF16)
    logits = _dot_nt(w_hi, h_hi) + _dot_nt(w_hi, h_lo) + _dot_nt(w_lo, h_hi)
    scores = _sigmoid(logits)
    choice = scores + bias_ref[...]
    neg_inf = -jnp.inf

    iota_g = lax.broadcasted_iota(I32, (EXPERTS_PER_GROUP, tm), 0).astype(F32)
    group_rows = []
    for g in range(N_EXPERT_GROUPS):
        v = choice[g * EXPERTS_PER_GROUP:(g + 1) * EXPERTS_PER_GROUP]
        m1, i1 = _first_argmax(v, iota_g, float(EXPERTS_PER_GROUP))
        m2 = jnp.max(jnp.where(iota_g == i1, neg_inf, v), axis=0, keepdims=True)
        group_rows.append(m1 + m2)
    group_scores = jnp.concatenate(group_rows, axis=0)

    iota_n = lax.broadcasted_iota(I32, (N_EXPERT_GROUPS, tm), 0).astype(F32)
    chosen = jnp.zeros((N_EXPERT_GROUPS, tm), F32)
    for _ in range(TOPK_GROUPS):
        _, gi = _first_argmax(group_scores, iota_n, float(N_EXPERT_GROUPS))
        hit = iota_n == gi
        chosen = jnp.where(hit, 1.0, chosen)
        group_scores = jnp.where(hit, neg_inf, group_scores)

    masked = jnp.concatenate(
        [jnp.where(chosen[g:g + 1] > 0.0, choice[g * EXPERTS_PER_GROUP:(g + 1) * EXPERTS_PER_GROUP], neg_inf)
         for g in range(N_EXPERT_GROUPS)], axis=0)

    iota_e = lax.broadcasted_iota(I32, (N_EXPERTS, tm), 0).astype(F32)
    picked, gates = [], []
    onehot = jnp.zeros((N_EXPERTS, tm), F32)
    for _ in range(TOP_K):
        _, ei = _first_argmax(masked, iota_e, float(N_EXPERTS))
        hit = iota_e == ei
        gates.append(jnp.sum(jnp.where(hit, scores, 0.0), axis=0, keepdims=True))
        masked = jnp.where(hit, neg_inf, masked)
        onehot = jnp.where(hit, 1.0, onehot)
        picked.append(ei)
    gate_sum = gates[0]
    for gk in gates[1:]:
        gate_sum = gate_sum + gk

    base = _dot(onehot.astype(BF16), upper_ref[...]) + carry_ref[...]
    ranks = [jnp.sum(jnp.where(iota_e == ei, base, 0.0), axis=0, keepdims=True) for ei in picked]
    carry_ref[...] = carry_ref[...] + jnp.sum(onehot, axis=1, keepdims=True)

    eidx_ref[...] = jnp.concatenate(picked, axis=0).astype(I32)
    rank_ref[...] = jnp.concatenate(ranks, axis=0).astype(I32)
    gate_ref[...] = jnp.concatenate([gk / gate_sum * ROUTED_SCALE for gk in gates], axis=0)
    counts_ref[...] = carry_ref[...].astype(I32)


def _router(h2, w_router, router_bias, tm=256):
    t, d = h2.shape
    upper = jnp.asarray(np.triu(np.ones((tm, tm), np.float32), 1), BF16)
    tok = pl.BlockSpec((TOP_K, tm), lambda i: (0, i))
    full = lambda shp: pl.BlockSpec(shp, lambda i: (0,) * len(shp))
    return pl.pallas_call(
        _router_kernel,
        grid=(t // tm,),
        in_specs=[pl.BlockSpec((tm, d), lambda i: (i, 0)), full((N_EXPERTS, d)), full((N_EXPERTS, 1)), full((tm, tm))],
        out_specs=[tok, tok, tok, full((N_EXPERTS, 1))],
        out_shape=[jax.ShapeDtypeStruct((TOP_K, t), I32), jax.ShapeDtypeStruct((TOP_K, t), I32),
                   jax.ShapeDtypeStruct((TOP_K, t), F32), jax.ShapeDtypeStruct((N_EXPERTS, 1), I32)],
        scratch_shapes=[pltpu.VMEM((N_EXPERTS, 1), F32)],
        compiler_params=_params(("arbitrary",)),
        name="router",
    )(h2, w_router.T, router_bias.reshape(N_EXPERTS, 1), upper)


def _positions_kernel(counts_ref, lower_ref, eidx_ref, rank_ref, pos_ref):
    tm = eidx_ref.shape[1]
    counts = jnp.broadcast_to(counts_ref[...].astype(F32), (N_EXPERTS, LANES))
    offsets = _dot_exact_lhs(lower_ref[...], counts)[:, 0:1]
    iota_e = lax.broadcasted_iota(I32, (N_EXPERTS, tm), 0).astype(F32)
    e = eidx_ref[...].astype(F32)
    rows = [jnp.sum(jnp.where(iota_e == e[k:k + 1], offsets, 0.0), axis=0, keepdims=True) for k in range(TOP_K)]
    pos_ref[...] = jnp.concatenate(rows, axis=0).astype(I32) + rank_ref[...]


def _positions(counts, eidx, rank, tm):
    t = eidx.shape[1]
    lower = jnp.asarray(np.tril(np.ones((N_EXPERTS, N_EXPERTS), np.float32), -1), BF16)
    tok = pl.BlockSpec((TOP_K, tm), lambda i: (0, i))
    return pl.pallas_call(
        _positions_kernel,
        grid=(t // tm,),
        in_specs=[pl.BlockSpec((N_EXPERTS, 1), lambda i: (0, 0)), pl.BlockSpec((N_EXPERTS, N_EXPERTS), lambda i: (0, 0)),
                  tok, tok],
        out_specs=tok,
        out_shape=jax.ShapeDtypeStruct((TOP_K, t), I32),
        compiler_params=_params(("arbitrary",)),
        name="positions",
    )(counts, lower, eidx, rank)


EXPERT_BLOCK = 512
TAIL_UNIT = 128
TAIL_PIECES = tuple(1 << i for i in reversed(range(EXPERT_BLOCK.bit_length() - 1)))
W_AHEAD = 3
W_SLOTS = W_AHEAD + 1
X_SLOTS = 4
Y_SLOTS = 4


def _experts_kernel(start_ref, count_ref, ahead_ref, slot_ref, first_ref, blk0_ref, full0_ref, ptail_ref,
                    ltail_ref, blktok_ref, nblocks_ref, xs_hbm, wg_hbm, wu_hbm, wd_hbm, ys_hbm,
                    wg_buf, wu_buf, wd_buf, wg_bf, wu_bf, wd_bf, xbuf, ybuf, ytail, wsem, xsem, ysem, tsem):
    e = pl.program_id(0)
    last_step = e == pl.num_programs(0) - 1
    start, count = start_ref[e], count_ref[e]
    n_full = jnp.right_shift(count, EXPERT_BLOCK.bit_length() - 1)
    tail = jnp.bitwise_and(count, EXPERT_BLOCK - 1)
    n_blk = n_full + (tail > 0).astype(I32)
    blk0, full0 = blk0_ref[e], full0_ref[e]
    slot = slot_ref[e]

    def slot_ahead(k):
        return jnp.where(slot + k >= W_SLOTS, slot + k - W_SLOTS, slot + k)

    def token_rows(token, n):
        return pl.ds(pl.multiple_of(token * TOKEN_ROWS, TOKEN_ROWS), n * TOKEN_ROWS)

    def fetch(ex, s):
        return (pltpu.make_async_copy(wg_hbm.at[ex], wg_buf.at[s], wsem.at[s, 0]),
                pltpu.make_async_copy(wu_hbm.at[ex], wu_buf.at[s], wsem.at[s, 1]),
                pltpu.make_async_copy(wd_hbm.at[ex], wd_buf.at[s], wsem.at[s, 2]))

    def x_copy(token, s):
        return pltpu.make_async_copy(xs_hbm.at[token_rows(token, EXPERT_BLOCK)], xbuf.at[s], xsem.at[s])

    def y_copy(token, s):
        return pltpu.make_async_copy(ybuf.at[s], ys_hbm.at[token_rows(token, EXPERT_BLOCK)], ysem.at[s])

    def tail_copies(token, length):
        out = []
        for piece in TAIL_PIECES:
            bigger = (EXPERT_BLOCK - 1) & ~(2 * piece - 1)
            done = jnp.bitwise_and(length, bigger)
            cp = pltpu.make_async_copy(ytail.at[token_rows(done, piece)], ys_hbm.at[token_rows(token + done, piece)],
                                       tsem)
            out.append((jnp.bitwise_and(length, piece) != 0, cp))
        return out

    def block(s, rows=EXPERT_BLOCK):
        x = _from_token_tiles(xbuf.at[s], rows).astype(BF16)
        g = _dot(x, wg_bf[...])
        u = _dot(x, wu_bf[...])
        return _dot((_silu(g) * u).astype(BF16), wd_bf[...])

    @pl.when(count > 0)
    def _():
        @pl.when(first_ref[e] == 1)
        def _():
            for g in range(X_SLOTS - 1):
                @pl.when(g < nblocks_ref[0])
                def _(g=g):
                    x_copy(blktok_ref[g], g).start()
            for cp in fetch(e, slot):
                cp.start()

            for k in range(1, W_AHEAD):
                @pl.when(ahead_ref[k - 1, e] >= 0)
                def _(k=k):
                    for cp in fetch(ahead_ref[k - 1, e], slot_ahead(k)):
                        cp.start()

        for cp in fetch(e, slot):
            cp.wait()

        @pl.when(ahead_ref[W_AHEAD - 1, e] >= 0)
        def _():
            for cp in fetch(ahead_ref[W_AHEAD - 1, e], slot_ahead(W_AHEAD)):
                cp.start()

        wg_bf[...] = wg_buf[slot].astype(BF16)
        wu_bf[...] = wu_buf[slot].astype(BF16)
        wd_bf[...] = wd_buf[slot].astype(BF16)

        def take_x(i):
            g = blk0 + i
            xs_slot = jnp.bitwise_and(g, X_SLOTS - 1)
            x_copy(start, xs_slot).wait()
            ahead = g + (X_SLOTS - 1)

            @pl.when(ahead < nblocks_ref[0])
            def _():
                x_copy(blktok_ref[ahead], jnp.bitwise_and(ahead, X_SLOTS - 1)).start()

            return xs_slot

        def full_block(i, carry):
            y = block(take_x(i))
            j = full0 + i
            ys_slot = jnp.bitwise_and(j, Y_SLOTS - 1)

            @pl.when(j >= Y_SLOTS)
            def _():
                y_copy(start, ys_slot).wait()

            _to_token_tiles(ybuf.at[ys_slot], y)
            y_copy(start + i * EXPERT_BLOCK, ys_slot).start()
            return carry

        lax.fori_loop(0, n_full, full_block, 0)

        @pl.when(tail > 0)
        def _():
            xs_slot = take_x(n_full)
            for pred, cp in tail_copies(start, ptail_ref[e]):
                @pl.when(pred)
                def _(cp=cp):
                    cp.wait()
            units = jnp.right_shift(tail + (TAIL_UNIT - 1), TAIL_UNIT.bit_length() - 1)
            for u in range(1, EXPERT_BLOCK // TAIL_UNIT + 1):
                @pl.when(units == u)
                def _(u=u):
                    rows = u * TAIL_UNIT
                    _to_token_tiles(ytail.at[pl.ds(0, rows * TOKEN_ROWS)], block(xs_slot, rows))
            for pred, cp in tail_copies(start + n_full * EXPERT_BLOCK, tail):
                @pl.when(pred)
                def _(cp=cp):
                    cp.start()

    @pl.when(last_step)
    def _():
        total_full = full0 + n_full
        for back in range(1, Y_SLOTS + 1):
            @pl.when(total_full >= back)
            def _(back=back):
                y_copy(0, jnp.bitwise_and(total_full - back, Y_SLOTS - 1)).wait()
        for pred, cp in tail_copies(0, ltail_ref[0]):
            @pl.when(pred)
            def _(cp=cp):
                cp.wait()


def _max_expert_blocks(n_rows):
    return n_rows // EXPERT_BLOCK + N_EXPERTS


def _expert_metadata(counts, n_rows):
    ids = jnp.arange(N_EXPERTS, dtype=I32)
    used = counts > 0
    starts = jnp.cumsum(counts) - counts
    n_blk = (counts + EXPERT_BLOCK - 1) // EXPERT_BLOCK
    n_full = counts // EXPERT_BLOCK
    tail = counts % EXPERT_BLOCK
    blk0 = jnp.cumsum(n_blk) - n_blk
    full0 = jnp.cumsum(n_full) - n_full
    next_used = lax.cummin(jnp.where(used, ids, N_EXPERTS), reverse=True)
    next_after = jnp.concatenate([next_used[1:], jnp.full((1,), N_EXPERTS, I32)])
    nxt = jnp.where(next_after < N_EXPERTS, next_after, -1)
    ordinal = jnp.cumsum(used.astype(I32)) - 1
    slot = ordinal % W_SLOTS
    first = jnp.logical_and(used, ordinal == 0)
    latest = lax.cummax(jnp.where(tail > 0, ids, -1))
    before = jnp.concatenate([jnp.full((1,), -1, I32), latest[:-1]])
    pick = lambda index, values: jnp.sum(jnp.where(index[:, None] == ids[None, :], values[None, :], 0), axis=1)
    ahead = [nxt]
    for _ in range(1, W_AHEAD):
        ahead.append(jnp.where(ahead[-1] >= 0, pick(ahead[-1], nxt + 1), 0) - 1)
    ptail = pick(before, tail)
    ltail = pick(latest[-1:], tail)
    block_ends = jnp.cumsum(n_blk)
    g = jnp.arange(_max_expert_blocks(n_rows), dtype=I32)
    eg = jnp.sum((g[:, None] >= block_ends[None, :]).astype(I32), axis=1)
    blktok = g * EXPERT_BLOCK + pick(eg, starts - blk0 * EXPERT_BLOCK)
    return tuple(v.astype(I32) for v in (starts, counts, jnp.stack(ahead), slot, first, blk0, full0, ptail, ltail, blktok,
                                         block_ends[-1:]))


def _experts(xs, counts, w_gate, w_up, w_down):
    d = D_MODEL
    meta = _expert_metadata(counts, xs.shape[0] // TOKEN_ROWS - EXPERT_BLOCK)
    hbm = pl.BlockSpec(memory_space=pl.ANY)
    blk = (EXPERT_BLOCK * TOKEN_ROWS, LANES)
    grid_spec = pltpu.PrefetchScalarGridSpec(
        num_scalar_prefetch=len(meta),
        grid=(N_EXPERTS,),
        in_specs=[hbm, hbm, hbm, hbm],
        out_specs=hbm,
        scratch_shapes=[pltpu.VMEM((W_SLOTS, d, EXPERT_FF), F32), pltpu.VMEM((W_SLOTS, d, EXPERT_FF), F32),
                        pltpu.VMEM((W_SLOTS, EXPERT_FF, d), F32),
                        pltpu.VMEM((d, EXPERT_FF), BF16), pltpu.VMEM((d, EXPERT_FF), BF16),
                        pltpu.VMEM((EXPERT_FF, d), BF16),
                        pltpu.VMEM((X_SLOTS,) + blk, WORD), pltpu.VMEM((Y_SLOTS,) + blk, WORD), pltpu.VMEM(blk, WORD),
                        pltpu.SemaphoreType.DMA((W_SLOTS, 3)), pltpu.SemaphoreType.DMA((X_SLOTS,)),
                        pltpu.SemaphoreType.DMA((Y_SLOTS,)), pltpu.SemaphoreType.DMA(())],
    )
    return pl.pallas_call(
        _experts_kernel,
        grid_spec=grid_spec,
        out_shape=jax.ShapeDtypeStruct(xs.shape, WORD),
        compiler_params=_params(("arbitrary",)),
        name="experts",
    )(*meta, xs, w_gate, w_up, w_down)


SC_CORES = 2
SC_SUBCORES = 16
SC_WORKERS = SC_CORES * SC_SUBCORES
SC_CHUNK = 128
SC_RING = 4


def _sc_scatter_rows(planes, idx, n_out_rows):
    n_planes, t, _ = planes.shape
    ranges = SC_WORKERS // n_planes
    n_chunks = t // ranges // SC_CHUNK
    idx = idx.reshape(TOP_K, n_planes, ranges, n_chunks, SC_CHUNK)
    mesh = plsc.VectorSubcoreMesh(core_axis_name="c", subcore_axis_name="s", num_cores=SC_CORES,
                                  num_subcores=SC_SUBCORES)

    def body(planes_hbm, idx_hbm, out_hbm, idx_v, rows_v, lsem, ssem):
        wid = lax.axis_index("s") * SC_CORES + lax.axis_index("c")
        plane = wid % n_planes
        token0 = (wid // n_planes) * (n_chunks * SC_CHUNK)
        for k in range(TOP_K):
            pltpu.sync_copy(idx_hbm.at[k, plane, wid // n_planes], idx_v.at[k])

        def load(slot, c):
            return pltpu.make_async_copy(planes_hbm.at[plane, pl.ds(token0 + c * SC_CHUNK, SC_CHUNK)], rows_v.at[slot],
                                         lsem.at[slot])

        def scatter(slot, c, k):
            return pltpu.make_async_copy(rows_v.at[slot], out_hbm.at[idx_v.at[k, c]], ssem.at[slot])

        @pl.loop(0, n_chunks, step=SC_RING)
        def _(g):
            for slot in range(SC_RING):
                @pl.when(g > 0)
                def _(slot=slot):
                    for k in range(TOP_K):
                        scatter(slot, 0, k).wait()
                load(slot, g + slot).start()
            for slot in range(SC_RING):
                load(slot, g + slot).wait()
                for k in range(TOP_K):
                    scatter(slot, g + slot, k).start()

        for slot in range(SC_RING):
            for k in range(TOP_K):
                scatter(slot, 0, k).wait()

    return pl.kernel(
        body, mesh=mesh,
        out_type=jax.ShapeDtypeStruct((n_out_rows, LANES), planes.dtype),
        scratch_types=[pltpu.VMEM((TOP_K, n_chunks, SC_CHUNK), I32), pltpu.VMEM((SC_RING, SC_CHUNK, LANES), planes.dtype),
                       pltpu.SemaphoreType.DMA((SC_RING,)), pltpu.SemaphoreType.DMA((SC_RING,))],
        name="sc_scatter",
    )(planes, idx)


def _sc_gather_rows(table, idx):
    n_workers, n_chunks, chunk = idx.shape
    rows_per_worker = n_chunks * chunk
    mesh = plsc.VectorSubcoreMesh(core_axis_name="c", subcore_axis_name="s", num_cores=SC_CORES,
                                  num_subcores=SC_SUBCORES)

    def body(table_hbm, idx_hbm, out_hbm, idx_v, rows_v, gsem, wsem):
        wid = lax.axis_index("s") * SC_CORES + lax.axis_index("c")
        base = wid * rows_per_worker
        pltpu.sync_copy(idx_hbm.at[wid], idx_v)

        def write(slot, c):
            return pltpu.make_async_copy(rows_v.at[slot], out_hbm.at[pl.ds(base + c * chunk, chunk)], wsem.at[slot])

        def gather(slot, c):
            return pltpu.make_async_copy(table_hbm.at[idx_v.at[c]], rows_v.at[slot], gsem.at[slot])

        @pl.loop(0, n_chunks, step=SC_RING)
        def _(g):
            for slot in range(SC_RING):
                @pl.when(g > 0)
                def _(slot=slot):
                    write(slot, 0).wait()
                gather(slot, g + slot).start()
            for slot in range(SC_RING):
                gather(slot, g + slot).wait()
                write(slot, g + slot).start()

        for slot in range(SC_RING):
            write(slot, 0).wait()

    return pl.kernel(
        body, mesh=mesh,
        out_type=jax.ShapeDtypeStruct((n_workers * rows_per_worker, LANES), table.dtype),
        scratch_types=[pltpu.VMEM((n_chunks, chunk), I32), pltpu.VMEM((SC_RING, chunk, LANES), table.dtype),
                       pltpu.SemaphoreType.DMA((SC_RING,)), pltpu.SemaphoreType.DMA((SC_RING,))],
        name="sc_gather",
    )(table, idx)


def _combine_kernel(rows_ref, gates_ref, h_ref, x1_ref, gatef_ref, wg_ref, wu_ref, wd_ref, out_ref):
    hb = h_ref[...].astype(BF16)
    shared = _dot((_silu(_dot(hb, wg_ref[...])) * _dot(hb, wu_ref[...])).astype(BF16), wd_ref[...])
    gates = gates_ref[...]

    def expert_rows(k):
        words = [rows_ref[k * TOKEN_ROWS + c] for c in range(TOKEN_ROWS)]
        lows = [lax.bitcast_convert_type(jnp.left_shift(w, 16), F32) for w in words]
        highs = [lax.bitcast_convert_type(jnp.bitwise_and(w, HIGH_HALF), F32) for w in words]
        return jnp.concatenate(lows + highs, axis=1)

    routed = expert_rows(0) * gates[:, 0:1]
    for k in range(1, TOP_K):
        routed = routed + expert_rows(k) * gates[:, k:k + 1]
    out_ref[...] = x1_ref[...] + gatef_ref[...] * (shared + routed)


def _combine(gathered, gates_t, h2, x1, gate_f, w_gate_s, w_up_s, w_down_s, b, s, tm):
    t, d = h2.shape
    tiles_per_seq = s // tm
    row = lambda n: pl.BlockSpec((tm, n), lambda i: (i, 0))
    full = lambda shp: pl.BlockSpec(shp, lambda i: (0,) * len(shp))
    return pl.pallas_call(
        _combine_kernel,
        grid=(t // tm,),
        in_specs=[pl.BlockSpec((TOP_K * TOKEN_ROWS, tm, LANES), lambda i: (0, i, 0)),
                  row(TOP_K), row(d), row(d),
                  pl.BlockSpec((None, 1, d), lambda i: (i // tiles_per_seq, 0, 0)),
                  full((d, EXPERT_FF)), full((d, EXPERT_FF)), full((EXPERT_FF, d))],
        out_specs=row(d),
        out_shape=jax.ShapeDtypeStruct((t, d), F32),
        compiler_params=_params(("arbitrary",)),
        name="combine",
    )(gathered, gates_t, h2, x1, gate_f.reshape(b, 1, d),
      w_gate_s.astype(BF16), w_up_s.astype(BF16), w_down_s.astype(BF16))


def _moe_sublayer(x1, h2, h2t, gate_f, w_router, router_bias, w_gate, w_up, w_down, w_gate_s, w_up_s, w_down_s, b, s,
                  tm=256):
    t = b * s
    eidx, rank, gates, counts = _router(h2, w_router, router_bias)
    pos = _positions(counts, eidx, rank, tm)
    idx = pos[:, None, :] * TOKEN_ROWS + jnp.arange(TOKEN_ROWS, dtype=I32)[None, :, None]
    xs = _sc_scatter_rows(h2t, idx, (t * TOP_K + EXPERT_BLOCK) * TOKEN_ROWS)
    ys = _experts(xs, counts[:, 0], w_gate, w_up, w_down)
    gathered = _sc_gather_rows(ys, idx.reshape(SC_WORKERS, -1, SC_CHUNK)).reshape(TOP_K * TOKEN_ROWS, t, LANES)
    return _combine(gathered, gates.T, h2, x1, gate_f, w_gate_s, w_up_s, w_down_s, b, s, tm)


def kernel(x, c, w_ada, b_ada, norm_mix_gain, w_in, q_norm_gain, k_norm_gain, rel_bias_table, conv_w, conv_b, dt_bias,
           a_log, d_skip, ssm_norm_gain, w_out, norm_ffn_gain, w_router, router_bias, w_gate_experts, w_up_experts,
           w_down_experts, w_gate_shared, w_up_shared, w_down_shared):
    b, s, d = x.shape
    for layer in range(w_ada.shape[0]):
        mod = _adaln(c, w_ada[layer], b_ada[layer])
        x1, h2, h2t = _mixer_sublayer(x, mod, norm_mix_gain[layer], w_in[layer], q_norm_gain[layer], k_norm_gain[layer],
                                 rel_bias_table, conv_w[layer], conv_b[layer], dt_bias[layer], a_log[layer],
                                 d_skip[layer], ssm_norm_gain[layer], w_out[layer], norm_ffn_gain[layer])
        gate_f = mod[:, 5 * d:]
        out = _moe_sublayer(x1, h2, h2t, gate_f, w_router[layer], router_bias[layer], w_gate_experts[layer],
                            w_up_experts[layer], w_down_experts[layer], w_gate_shared[layer], w_up_shared[layer],
                            w_down_shared[layer], b, s)
        x = out.reshape(b, s, d)
    return x
```

```python
import functools
import math

import numpy as np
import jax
import jax.numpy as jnp
from jax import lax
from jax.experimental import pallas as pl
from jax.experimental.pallas import tpu as pltpu
from jax.experimental.pallas import tpu_sc as plsc

F32 = jnp.float32
BF16 = jnp.bfloat16
I32 = jnp.int32

D_MODEL = 1024
ATTN_HEADS = 8
HEAD_DIM = 64
ATTN_WIDTH = ATTN_HEADS * HEAD_DIM
PATTERNS = ((128, 1), (512, 4), (2048, 16))
WIN_STEPS = 128
REL_BUCKETS = 32
REL_MAX_DISTANCE = 2048
SSM_HEADS = 24
SSM_HEAD_DIM = 64
SSM_WIDTH = SSM_HEADS * SSM_HEAD_DIM
SSM_GROUPS = 4
HEADS_PER_GROUP = SSM_HEADS // SSM_GROUPS
GROUP_WIDTH = SSM_WIDTH // SSM_GROUPS
SSM_STATE = 128
SSM_CONV = 4
SSM_CHUNK = 128
CONV_CH = SSM_WIDTH + 2 * SSM_GROUPS * SSM_STATE
N_EXPERTS = 256
TOP_K = 8
N_EXPERT_GROUPS = 8
EXPERTS_PER_GROUP = N_EXPERTS // N_EXPERT_GROUPS
TOPK_GROUPS = 4
EXPERT_FF = 256
ROUTED_SCALE = 2.5
NORM_EPS = 1e-6

LANES = 128
SUBLANES = 8
NEG_BIG = -1e30
VMEM_LIMIT = 56 * 1024 * 1024


def _params(sem, vmem=VMEM_LIMIT):
    return pltpu.CompilerParams(dimension_semantics=sem, vmem_limit_bytes=vmem)


def _sigmoid(x):
    return 1.0 / (1.0 + jnp.exp(-x))


def _silu(x):
    return x * _sigmoid(x)


def _split3(x):
    hi = x.astype(BF16)
    r = x - hi.astype(F32)
    mid = r.astype(BF16)
    lo = (r - mid.astype(F32)).astype(BF16)
    return hi, mid, lo


def _dot(a, b):
    return jnp.dot(a, b, preferred_element_type=F32)


def _dot_nt(a, b):
    return lax.dot_general(a, b, (((1,), (1,)), ((), ())), preferred_element_type=F32)


def _dot_exact_rhs(a, b_exact):
    hi, mid, lo = _split3(a)
    return _dot(hi, b_exact) + _dot(mid, b_exact) + _dot(lo, b_exact)


def _dot_exact_lhs(a_exact, b):
    hi, mid, lo = _split3(b)
    return _dot(a_exact, hi) + _dot(a_exact, mid) + _dot(a_exact, lo)


def _adaln_kernel(c_ref, w_ref, b_ref, o_ref):
    s = _silu(c_ref[...]).astype(BF16)
    o_ref[...] = _dot(s, w_ref[...].astype(BF16)) + b_ref[...]


def _adaln(c, w_ada, b_ada):
    b, d = c.shape
    n = w_ada.shape[1]
    rows = SUBLANES
    c_pad = jnp.zeros((rows, d), F32).at[:b].set(c)
    tn = 1024
    out = pl.pallas_call(
        _adaln_kernel,
        grid=(n // tn,),
        in_specs=[pl.BlockSpec((rows, d), lambda j: (0, 0)),
                  pl.BlockSpec((d, tn), lambda j: (0, j)),
                  pl.BlockSpec((1, tn), lambda j: (0, j))],
        out_specs=pl.BlockSpec((rows, tn), lambda j: (0, j)),
        out_shape=jax.ShapeDtypeStruct((rows, n), F32),
        compiler_params=_params(("arbitrary",)),
        name="adaln",
    )(c_pad, w_ada, b_ada.reshape(1, n))
    return out[:b]


def _inproj_kernel(x_ref, shift_ref, scale_ref, g_ref, wqkv_ref, wz_ref, wxbc_ref, wdt_ref,
                   qg_ref, kg_ref, hmean_ref, q_ref, k_ref, v_ref, z_ref, xbc_ref, dt_ref):
    x = x_ref[...]
    ms = jnp.mean(x * x, axis=-1, keepdims=True)
    h = x * lax.rsqrt(ms + NORM_EPS) * g_ref[...]
    h = h * (1.0 + scale_ref[...]) + shift_ref[...]
    hb = h.astype(BF16)

    hmean = hmean_ref[...]

    def head_norm(t, gain):
        ss = _dot_exact_rhs(t * t, hmean)
        return t * lax.rsqrt(ss + NORM_EPS) * gain

    q = _dot(hb, wqkv_ref[:, 0:ATTN_WIDTH])
    q_ref[...] = head_norm(q, qg_ref[...]) * (HEAD_DIM ** -0.5)
    k = _dot(hb, wqkv_ref[:, ATTN_WIDTH:2 * ATTN_WIDTH])
    k_ref[...] = head_norm(k, kg_ref[...])
    v_ref[...] = _dot(hb, wqkv_ref[:, 2 * ATTN_WIDTH:3 * ATTN_WIDTH])
    for c0 in range(0, SSM_WIDTH, 512):
        z_ref[:, c0:c0 + 512] = _dot(hb, wz_ref[:, c0:c0 + 512])
    for c0 in range(0, CONV_CH, 512):
        xbc_ref[:, c0:c0 + 512] = _dot(hb, wxbc_ref[:, c0:c0 + 512])
    dt_ref[...] = _dot(hb, wdt_ref[...])


def _in_proj(x, shift, scale, gain, w_in, q_gain, k_gain, tm=256):
    b, s, d = x.shape
    t = b * s
    tiles_per_seq = s // tm
    w = w_in.astype(BF16)
    o_z = 3 * ATTN_WIDTH
    o_x = o_z + SSM_WIDTH
    o_dt = o_x + CONV_CH
    w_qkv, w_z, w_xbc = w[:, :o_z], w[:, o_z:o_x], w[:, o_x:o_dt]
    w_dt = jnp.zeros((d, LANES), BF16).at[:, :SSM_HEADS].set(w[:, o_dt:])
    head_of = np.arange(ATTN_WIDTH) // HEAD_DIM
    hmean = jnp.asarray((head_of[:, None] == head_of[None, :]).astype(np.float32) / HEAD_DIM, BF16)
    full = lambda shp: pl.BlockSpec(shp, lambda i: (0,) * len(shp))
    row = lambda n: pl.BlockSpec((tm, n), lambda i: (i, 0))
    per_batch = pl.BlockSpec((None, 1, d), lambda i: (i // tiles_per_seq, 0, 0))
    outs = pl.pallas_call(
        _inproj_kernel,
        grid=(t // tm,),
        in_specs=[row(d), per_batch, per_batch, full((1, d)),
                  full((d, o_z)), full((d, SSM_WIDTH)), full((d, CONV_CH)), full((d, LANES)),
                  full((1, ATTN_WIDTH)), full((1, ATTN_WIDTH)), full((ATTN_WIDTH, ATTN_WIDTH))],
        out_specs=[row(ATTN_WIDTH), row(ATTN_WIDTH), row(ATTN_WIDTH), row(SSM_WIDTH), row(CONV_CH), row(LANES)],
        out_shape=[jax.ShapeDtypeStruct((t, n), F32)
                   for n in (ATTN_WIDTH, ATTN_WIDTH, ATTN_WIDTH, SSM_WIDTH, CONV_CH, LANES)],
        compiler_params=_params(("arbitrary",)),
        name="in_proj",
    )(x.reshape(t, d), shift.reshape(b, 1, d), scale.reshape(b, 1, d), gain.reshape(1, d),
      w_qkv, w_z, w_xbc, w_dt,
      jnp.tile(q_gain, ATTN_HEADS).reshape(1, ATTN_WIDTH), jnp.tile(k_gain, ATTN_HEADS).reshape(1, ATTN_WIDTH), hmean)
    return outs


def _t5_causal_buckets(distance):
    n = np.maximum(distance, 0)
    max_exact = REL_BUCKETS // 2
    large = max_exact + (np.log(np.maximum(n, 1) / max_exact) / math.log(REL_MAX_DISTANCE / max_exact)
                         * (REL_BUCKETS - max_exact)).astype(np.int64)
    large = np.minimum(large, REL_BUCKETS - 1)
    return np.where(n < max_exact, n, large).astype(np.int32)


def _window_bias(rel_bias_table, dilation):
    qi = np.arange(WIN_STEPS)[:, None]
    kj = np.arange(2 * WIN_STEPS)[None, :]
    dist = qi + WIN_STEPS - kj
    band = (dist >= 0) & (dist <= WIN_STEPS)
    onehot = (_t5_causal_buckets(dist * dilation).reshape(-1, 1) == np.arange(REL_BUCKETS)[None, :]).astype(np.float32)
    bias = jnp.dot(rel_bias_table.astype(F32).T, jnp.asarray(onehot).T, precision=lax.Precision.HIGHEST)
    bias = bias.reshape(ATTN_HEADS, WIN_STEPS, 2 * WIN_STEPS)
    return jnp.where(jnp.asarray(band)[None], bias, NEG_BIG)


ATTN_TOKENS = max(w for w, _ in PATTERNS)
ATTN_UNROLL = 16


def _attn_kernel(q_ref, kp_ref, kc_ref, vp_ref, vc_ref, bias_ref, out_ref, kw, vw, o_acc, l_acc):
    tb = ATTN_TOKENS
    first = pl.program_id(2) == 0
    kw[0:tb] = kp_ref[...]
    kw[tb:2 * tb] = kc_ref[...]
    vw[0:tb] = vp_ref[...]
    vw[tb:2 * tb] = vc_ref[...]
    lane = lax.broadcasted_iota(I32, (WIN_STEPS, LANES), 1)
    head0 = lane < HEAD_DIM
    col = lax.broadcasted_iota(I32, (WIN_STEPS, 2 * WIN_STEPS), 1)
    in_prev = col < WIN_STEPS

    for p, (_, d) in enumerate(PATTERNS):
        shift = d.bit_length() - 1
        n_blocks = tb // WIN_STEPS

        def rows(start, n, d=d):
            return pl.ds(start, n, stride=d) if d > 1 else pl.ds(start, n)

        def body(it, carry, p=p, d=d, shift=shift, rows=rows):
            for u in range(ATTN_UNROLL):
                idx = it * ATTN_UNROLL + u
                r = jnp.bitwise_and(idx, d - 1)
                j = jnp.right_shift(idx, shift)
                qs = j * (WIN_STEPS * d) + r
                q = q_ref[rows(qs, WIN_STEPS), :]
                k = kw[rows(tb + qs - WIN_STEPS * d, 2 * WIN_STEPS), :].astype(BF16)
                v = vw[rows(tb + qs - WIN_STEPS * d, 2 * WIN_STEPS), :].astype(BF16)
                no_prev = jnp.logical_and(in_prev, jnp.logical_and(first, j == 0))
                o_h, lse_h = [], []
                for h in range(2):
                    qh = jnp.where(head0 if h == 0 else jnp.logical_not(head0), q, 0.0).astype(BF16)
                    s = _dot_nt(qh, k) + bias_ref[p, h]
                    s = jnp.where(no_prev, NEG_BIG, s)
                    m = jnp.max(s, axis=-1, keepdims=True)
                    e = jnp.exp(s - m)
                    denom = jnp.sum(e, axis=-1, keepdims=True)
                    o_h.append(_dot(e.astype(BF16), v) / denom)
                    lse_h.append(m + jnp.log(denom))
                o_acc[p, rows(qs, WIN_STEPS), :] = jnp.where(head0, o_h[0], o_h[1])
                l_acc[p, rows(qs, WIN_STEPS), :] = jnp.where(head0, lse_h[0], lse_h[1])
            return carry

        lax.fori_loop(0, n_blocks // ATTN_UNROLL, body, 0)

    chunk = 256
    for c0 in range(0, tb, chunk):
        l1, l2, l3 = (l_acc[p, c0:c0 + chunk, :] for p in range(3))
        m = jnp.maximum(jnp.maximum(l1, l2), l3)
        e1, e2, e3 = jnp.exp(l1 - m), jnp.exp(l2 - m), jnp.exp(l3 - m)
        num = e1 * o_acc[0, c0:c0 + chunk, :] + e2 * o_acc[1, c0:c0 + chunk, :] + e3 * o_acc[2, c0:c0 + chunk, :]
        out_ref[c0:c0 + chunk, :] = num / (e1 + e2 + e3)


def _attention(q, k, v, bias):
    b, s, w = q.shape
    tb = ATTN_TOKENS
    pairs = ATTN_HEADS // 2
    cur = pl.BlockSpec((None, tb, LANES), lambda bi, hp, i: (bi, i, hp))
    prev = pl.BlockSpec((None, tb, LANES), lambda bi, hp, i: (bi, jnp.maximum(i - 1, 0), hp))
    return pl.pallas_call(
        _attn_kernel,
        grid=(b, pairs, s // tb),
        in_specs=[cur, prev, cur, prev, cur,
                  pl.BlockSpec((len(PATTERNS), 2, WIN_STEPS, 2 * WIN_STEPS), lambda bi, hp, i: (0, hp, 0, 0))],
        out_specs=cur,
        out_shape=jax.ShapeDtypeStruct((b, s, w), F32),
        scratch_shapes=[pltpu.VMEM((2 * tb, LANES), F32), pltpu.VMEM((2 * tb, LANES), F32),
                        pltpu.VMEM((len(PATTERNS), tb, LANES), F32), pltpu.VMEM((len(PATTERNS), tb, LANES), F32)],
        compiler_params=_params(("arbitrary",) * 3),
        name="attention",
    )(q, k, k, v, v, bias)


SSD_CHUNKS_PER_STEP = 2


def _ssd_kernel(xbc_ref, halo_ref, z_ref, dtraw_ref, convw_ref, convb_ref, dtb_ref, alog_ref, dskip_ref, gain_ref,
                expand_ref, tril_ref, y_ref, state_ref):
    c = pl.program_id(1)

    @pl.when(c == 0)
    def _():
        state_ref[...] = jnp.zeros_like(state_ref)

    halo = jnp.where(c == 0, 0.0, halo_ref[...])
    for sub in range(SSD_CHUNKS_PER_STEP):
        rows = slice(sub * SSM_CHUNK, (sub + 1) * SSM_CHUNK)
        x = xbc_ref[rows, :]
        w = convw_ref[...]
        acc = x * w[SSM_CONV - 1:SSM_CONV, :] + convb_ref[...]
        row8 = lax.broadcasted_iota(I32, (SUBLANES, CONV_CH), 0)
        for shift in range(1, SSM_CONV):
            xs = pltpu.roll(x, shift, axis=0)
            hs = pltpu.roll(halo, shift, axis=0)
            head = jnp.where(row8 < shift, hs, xs[0:SUBLANES])
            xs = jnp.concatenate([head, xs[SUBLANES:]], axis=0)
            acc = acc + xs * w[SSM_CONV - 1 - shift:SSM_CONV - shift, :]
        act = _silu(acc)
        x_s = act[:, :SSM_WIDTH]
        bc0 = SSM_WIDTH
        cc0 = SSM_WIDTH + SSM_GROUPS * SSM_STATE

        t = dtraw_ref[rows, :] + dtb_ref[...]
        dt = jnp.maximum(t, 0.0) + jnp.log(1.0 + jnp.exp(-jnp.abs(t)))
        a = dt * (-jnp.exp(alog_ref[...]))
        a_cs = _dot_exact_lhs(tril_ref[...], a)
        a_cs_t = a_cs.T
        a_last = a_cs[SSM_CHUNK - 1:SSM_CHUNK, :]
        expand = expand_ref[...]
        dt_e = _dot_exact_rhs(dt, expand)
        ea_e = _dot_exact_rhs(jnp.exp(a_cs), expand)
        dte_e = _dot_exact_rhs(jnp.exp(a_last - a_cs), expand)
        xdt = x_s * dt_e
        xw = (xdt * dte_e).astype(BF16)
        xdt_b = xdt.astype(BF16)

        li = lax.broadcasted_iota(I32, (SSM_CHUNK, SSM_CHUNK), 0)
        si = lax.broadcasted_iota(I32, (SSM_CHUNK, SSM_CHUNK), 1)
        causal = li >= si

        ys = []
        for g in range(SSM_GROUPS):
            gs = slice(g * GROUP_WIDTH, (g + 1) * GROUP_WIDTH)
            b_g = act[:, bc0 + g * SSM_STATE:bc0 + (g + 1) * SSM_STATE]
            c_g = act[:, cc0 + g * SSM_STATE:cc0 + (g + 1) * SSM_STATE].astype(BF16)
            cb = _dot_nt(c_g, b_g.astype(BF16))
            state = state_ref[g]
            y_off = _dot(c_g, state.astype(BF16)) * ea_e[:, gs]
            parts = []
            for j in range(HEADS_PER_GROUP):
                hh = g * HEADS_PER_GROUP + j
                seg = a_cs[:, hh:hh + 1] - a_cs_t[hh:hh + 1, :]
                decay = jnp.exp(jnp.where(causal, seg, NEG_BIG))
                m = (cb * decay).astype(BF16)
                parts.append(_dot(m, xdt_b[:, hh * SSM_HEAD_DIM:(hh + 1) * SSM_HEAD_DIM]))
            ys.append(jnp.concatenate(parts, axis=1) + y_off)
            state_ref[g] = state * ea_e[SSM_CHUNK - 1:SSM_CHUNK, gs] + _dot(b_g.T.astype(BF16), xw[:, gs])
        y = jnp.concatenate(ys, axis=1) + dskip_ref[...] * x_s
        y = y * _silu(z_ref[rows, :])
        gain = gain_ref[...]
        for g in range(SSM_GROUPS):
            gs = slice(g * GROUP_WIDTH, (g + 1) * GROUP_WIDTH)
            yg = y[:, gs]
            ms = jnp.mean(yg * yg, axis=-1, keepdims=True)
            y_ref[rows, gs] = yg * lax.rsqrt(ms + NORM_EPS) * gain[:, gs]
        halo = x[SSM_CHUNK - SUBLANES:, :]


def _ssd(xbc, z, dt_raw, conv_w, conv_b, dt_bias, a_log, d_skip, norm_gain, b, s):
    t = b * s
    step_rows = SSD_CHUNKS_PER_STEP * SSM_CHUNK
    nc = s // step_rows
    pad_heads = lambda v: jnp.zeros((1, LANES), F32).at[0, :SSM_HEADS].set(v)
    head_of_lane = np.arange(SSM_WIDTH) // SSM_HEAD_DIM
    expand = jnp.asarray((np.arange(LANES)[:, None] == head_of_lane[None, :]).astype(np.float32), BF16)
    tril = jnp.asarray(np.tril(np.ones((SSM_CHUNK, SSM_CHUNK), np.float32)), BF16)
    halo_blocks = step_rows // SUBLANES
    chunk = lambda n: pl.BlockSpec((step_rows, n), lambda bi, c: (bi * nc + c, 0))
    full = lambda shp: pl.BlockSpec(shp, lambda bi, c: (0,) * len(shp))
    halo = pl.BlockSpec((SUBLANES, CONV_CH), lambda bi, c: (jnp.maximum((bi * nc + c) * halo_blocks - 1, 0), 0))
    return pl.pallas_call(
        _ssd_kernel,
        grid=(b, nc),
        in_specs=[chunk(CONV_CH), halo, chunk(SSM_WIDTH), chunk(LANES),
                  full((SSM_CONV, CONV_CH)), full((1, CONV_CH)), full((1, LANES)), full((1, LANES)),
                  full((1, SSM_WIDTH)), full((1, SSM_WIDTH)), full((LANES, SSM_WIDTH)), full((SSM_CHUNK, SSM_CHUNK))],
        out_specs=chunk(SSM_WIDTH),
        out_shape=jax.ShapeDtypeStruct((t, SSM_WIDTH), F32),
        scratch_shapes=[pltpu.VMEM((SSM_GROUPS, SSM_STATE, GROUP_WIDTH), F32)],
        compiler_params=_params(("arbitrary", "arbitrary")),
        name="ssd",
    )(xbc, xbc, z, dt_raw, conv_w, conv_b.reshape(1, CONV_CH), pad_heads(dt_bias), pad_heads(a_log),
      jnp.repeat(d_skip, SSM_HEAD_DIM).reshape(1, SSM_WIDTH), norm_gain.reshape(1, SSM_WIDTH), expand, tril)


WORD = jnp.int32
TOKEN_ROWS = D_MODEL // (2 * LANES)
HIGH_HALF = np.int32(-65536)


def _to_token_tiles(ref, x):
    n = x.shape[0]
    for c in range(TOKEN_ROWS):
        lo = lax.bitcast_convert_type(x[:, c * LANES:(c + 1) * LANES].astype(BF16).astype(F32), WORD)
        hi = lax.bitcast_convert_type(x[:, (c + TOKEN_ROWS) * LANES:(c + TOKEN_ROWS + 1) * LANES]
                                      .astype(BF16).astype(F32), WORD)
        ref[pl.ds(c, n, stride=TOKEN_ROWS), :] = jnp.bitwise_or(lax.shift_right_logical(lo, 16),
                                                                 jnp.bitwise_and(hi, HIGH_HALF))


def _from_token_tiles(ref, n, token0=0):
    lows, highs = [], []
    for c in range(TOKEN_ROWS):
        word = ref[pl.ds(token0 * TOKEN_ROWS + c, n, stride=TOKEN_ROWS), :]
        lows.append(lax.bitcast_convert_type(jnp.left_shift(word, 16), F32))
        highs.append(lax.bitcast_convert_type(jnp.bitwise_and(word, HIGH_HALF), F32))
    return jnp.concatenate(lows + highs, axis=1)


def _outproj_kernel(attn_ref, ssm_ref, x_ref, gate_ref, shift_ref, scale_ref, g_ref, wa_ref, ws_ref, x1_ref, h2_ref,
                    h2t_ref):
    mixed = _dot(attn_ref[...].astype(BF16), wa_ref[...]) + _dot(ssm_ref[...].astype(BF16), ws_ref[...])
    x1 = x_ref[...] + gate_ref[...] * mixed
    x1_ref[...] = x1
    ms = jnp.mean(x1 * x1, axis=-1, keepdims=True)
    h = x1 * lax.rsqrt(ms + NORM_EPS) * g_ref[...]
    h2 = h * (1.0 + scale_ref[...]) + shift_ref[...]
    h2_ref[...] = h2
    for c in range(TOKEN_ROWS):
        lo = lax.bitcast_convert_type(h2[:, c * LANES:(c + 1) * LANES].astype(BF16).astype(F32), WORD)
        hi = lax.bitcast_convert_type(h2[:, (c + TOKEN_ROWS) * LANES:(c + TOKEN_ROWS + 1) * LANES]
                                      .astype(BF16).astype(F32), WORD)
        h2t_ref[c] = jnp.bitwise_or(lax.shift_right_logical(lo, 16), jnp.bitwise_and(hi, HIGH_HALF))


def _out_proj(attn, ssm, x, gate, shift, scale, gain, w_out, b, s, tm=256):
    t = b * s
    d = D_MODEL
    tiles_per_seq = s // tm
    w = w_out.astype(BF16)
    row = lambda n: pl.BlockSpec((tm, n), lambda i: (i, 0))
    full = lambda shp: pl.BlockSpec(shp, lambda i: (0,) * len(shp))
    per_batch = pl.BlockSpec((None, 1, d), lambda i: (i // tiles_per_seq, 0, 0))
    return pl.pallas_call(
        _outproj_kernel,
        grid=(t // tm,),
        in_specs=[row(ATTN_WIDTH), row(SSM_WIDTH), row(d), per_batch, per_batch, per_batch, full((1, d)),
                  full((ATTN_WIDTH, d)), full((SSM_WIDTH, d))],
        out_specs=[row(d), row(d), pl.BlockSpec((TOKEN_ROWS, tm, LANES), lambda i: (0, i, 0))],
        out_shape=[jax.ShapeDtypeStruct((t, d), F32)] * 2 + [jax.ShapeDtypeStruct((TOKEN_ROWS, t, LANES), WORD)],
        compiler_params=_params(("arbitrary",)),
        name="out_proj",
    )(attn.reshape(t, ATTN_WIDTH), ssm, x.reshape(t, d),
      gate.reshape(b, 1, d), shift.reshape(b, 1, d), scale.reshape(b, 1, d), gain.reshape(1, d),
      w[:ATTN_WIDTH], w[ATTN_WIDTH:])


def _mixer_sublayer(x, mod, norm_mix_gain, w_in, q_norm_gain, k_norm_gain, rel_bias_table, conv_w, conv_b, dt_bias,
                    a_log, d_skip, ssm_norm_gain, w_out, norm_ffn_gain):
    b, s, d = x.shape
    shift_m, scale_m, gate_m, shift_f, scale_f, _ = jnp.split(mod, 6, axis=-1)
    q, k, v, z, xbc, dt_raw = _in_proj(x, shift_m, scale_m, norm_mix_gain, w_in, q_norm_gain, k_norm_gain)
    bias = jnp.stack([_window_bias(rel_bias_table, dilation) for _, dilation in PATTERNS])
    attn = _attention(q.reshape(b, s, ATTN_WIDTH), k.reshape(b, s, ATTN_WIDTH), v.reshape(b, s, ATTN_WIDTH), bias)
    ssm = _ssd(xbc, z, dt_raw, conv_w, conv_b, dt_bias, a_log, d_skip, ssm_norm_gain, b, s)
    return _out_proj(attn, ssm, x, gate_m, shift_f, scale_f, norm_ffn_gain, w_out, b, s)


def _first_argmax(v, iota, limit):
    m = jnp.max(v, axis=0, keepdims=True)
    idx = jnp.min(jnp.where(v == m, iota, limit), axis=0, keepdims=True)
    return m, idx


def _router_kernel(h_ref, wt_ref, bias_ref, upper_ref, eidx_ref, rank_ref, gate_ref, counts_ref, carry_ref):
    @pl.when(pl.program_id(0) == 0)
    def _():
        carry_ref[...] = jnp.zeros_like(carry_ref)

    tm = h_ref.shape[0]
    h = h_ref[...]
    wt = wt_ref[...]
    h_hi = h.astype(BF16)
    h_lo = (h - h_hi.astype(F32)).astype(BF16)
    w_hi = wt.astype(BF16)
    w_lo = (wt - w_hi.astype(F32)).astype(BF16)
    logits = _dot_nt(w_hi, h_hi) + _dot_nt(w_hi, h_lo) + _dot_nt(w_lo, h_hi)
    scores = _sigmoid(logits)
    choice = scores + bias_ref[...]
    neg_inf = -jnp.inf

    iota_g = lax.broadcasted_iota(I32, (EXPERTS_PER_GROUP, tm), 0).astype(F32)
    group_rows = []
    for g in range(N_EXPERT_GROUPS):
        v = choice[g * EXPERTS_PER_GROUP:(g + 1) * EXPERTS_PER_GROUP]
        m1, i1 = _first_argmax(v, iota_g, float(EXPERTS_PER_GROUP))
        m2 = jnp.max(jnp.where(iota_g == i1, neg_inf, v), axis=0, keepdims=True)
        group_rows.append(m1 + m2)
    group_scores = jnp.concatenate(group_rows, axis=0)

    iota_n = lax.broadcasted_iota(I32, (N_EXPERT_GROUPS, tm), 0).astype(F32)
    chosen = jnp.zeros((N_EXPERT_GROUPS, tm), F32)
    for _ in range(TOPK_GROUPS):
        _, gi = _first_argmax(group_scores, iota_n, float(N_EXPERT_GROUPS))
        hit = iota_n == gi
        chosen = jnp.where(hit, 1.0, chosen)
        group_scores = jnp.where(hit, neg_inf, group_scores)

    masked = jnp.concatenate(
        [jnp.where(chosen[g:g + 1] > 0.0, choice[g * EXPERTS_PER_GROUP:(g + 1) * EXPERTS_PER_GROUP], neg_inf)
         for g in range(N_EXPERT_GROUPS)], axis=0)

    iota_e = lax.broadcasted_iota(I32, (N_EXPERTS, tm), 0).astype(F32)
    picked, gates = [], []
    onehot = jnp.zeros((N_EXPERTS, tm), F32)
    for _ in range(TOP_K):
        _, ei = _first_argmax(masked, iota_e, float(N_EXPERTS))
        hit = iota_e == ei
        gates.append(jnp.sum(jnp.where(hit, scores, 0.0), axis=0, keepdims=True))
        masked = jnp.where(hit, neg_inf, masked)
        onehot = jnp.where(hit, 1.0, onehot)
        picked.append(ei)
    gate_sum = gates[0]
    for gk in gates[1:]:
        gate_sum = gate_sum + gk

    base = _dot(onehot.astype(BF16), upper_ref[...]) + carry_ref[...]
    ranks = [jnp.sum(jnp.where(iota_e == ei, base, 0.0), axis=0, keepdims=True) for ei in picked]
    carry_ref[...] = carry_ref[...] + jnp.sum(onehot, axis=1, keepdims=True)

    eidx_ref[...] = jnp.concatenate(picked, axis=0).astype(I32)
    rank_ref[...] = jnp.concatenate(ranks, axis=0).astype(I32)
    gate_ref[...] = jnp.concatenate([gk / gate_sum * ROUTED_SCALE for gk in gates], axis=0)
    counts_ref[...] = carry_ref[...].astype(I32)


def _router(h2, w_router, router_bias, tm=256):
    t, d = h2.shape
    upper = jnp.asarray(np.triu(np.ones((tm, tm), np.float32), 1), BF16)
    tok = pl.BlockSpec((TOP_K, tm), lambda i: (0, i))
    full = lambda shp: pl.BlockSpec(shp, lambda i: (0,) * len(shp))
    return pl.pallas_call(
        _router_kernel,
        grid=(t // tm,),
        in_specs=[pl.BlockSpec((tm, d), lambda i: (i, 0)), full((N_EXPERTS, d)), full((N_EXPERTS, 1)), full((tm, tm))],
        out_specs=[tok, tok, tok, full((N_EXPERTS, 1))],
        out_shape=[jax.ShapeDtypeStruct((TOP_K, t), I32), jax.ShapeDtypeStruct((TOP_K, t), I32),
                   jax.ShapeDtypeStruct((TOP_K, t), F32), jax.ShapeDtypeStruct((N_EXPERTS, 1), I32)],
        scratch_shapes=[pltpu.VMEM((N_EXPERTS, 1), F32)],
        compiler_params=_params(("arbitrary",)),
        name="router",
    )(h2, w_router.T, router_bias.reshape(N_EXPERTS, 1), upper)


def _positions_kernel(counts_ref, lower_ref, eidx_ref, rank_ref, pos_ref):
    tm = eidx_ref.shape[1]
    counts = jnp.broadcast_to(counts_ref[...].astype(F32), (N_EXPERTS, LANES))
    offsets = _dot_exact_lhs(lower_ref[...], counts)[:, 0:1]
    iota_e = lax.broadcasted_iota(I32, (N_EXPERTS, tm), 0).astype(F32)
    e = eidx_ref[...].astype(F32)
    rows = [jnp.sum(jnp.where(iota_e == e[k:k + 1], offsets, 0.0), axis=0, keepdims=True) for k in range(TOP_K)]
    pos_ref[...] = jnp.concatenate(rows, axis=0).astype(I32) + rank_ref[...]


def _positions(counts, eidx, rank, tm):
    t = eidx.shape[1]
    lower = jnp.asarray(np.tril(np.ones((N_EXPERTS, N_EXPERTS), np.float32), -1), BF16)
    tok = pl.BlockSpec((TOP_K, tm), lambda i: (0, i))
    return pl.pallas_call(
        _positions_kernel,
        grid=(t // tm,),
        in_specs=[pl.BlockSpec((N_EXPERTS, 1), lambda i: (0, 0)), pl.BlockSpec((N_EXPERTS, N_EXPERTS), lambda i: (0, 0)),
                  tok, tok],
        out_specs=tok,
        out_shape=jax.ShapeDtypeStruct((TOP_K, t), I32),
        compiler_params=_params(("arbitrary",)),
        name="positions",
    )(counts, lower, eidx, rank)


EXPERT_BLOCK = 512
TAIL_UNIT = 128
TAIL_PIECES = tuple(1 << i for i in reversed(range(EXPERT_BLOCK.bit_length() - 1)))
W_SLOTS = 3
X_SLOTS = 4
Y_SLOTS = 4


def _experts_kernel(start_ref, count_ref, nxt_ref, nxt2_ref, slot_ref, first_ref, blk0_ref, full0_ref, ptail_ref,
                    ltail_ref, blktok_ref, nblocks_ref, xs_hbm, wg_hbm, wu_hbm, wd_hbm, ys_hbm,
                    wg_buf, wu_buf, wd_buf, wg_bf, wu_bf, wd_bf, xbuf, ybuf, ytail, wsem, xsem, ysem, tsem):
    e = pl.program_id(0)
    last_step = e == pl.num_programs(0) - 1
    start, count = start_ref[e], count_ref[e]
    n_full = jnp.right_shift(count, EXPERT_BLOCK.bit_length() - 1)
    tail = jnp.bitwise_and(count, EXPERT_BLOCK - 1)
    n_blk = n_full + (tail > 0).astype(I32)
    blk0, full0 = blk0_ref[e], full0_ref[e]
    slot, nxt, nxt2 = slot_ref[e], nxt_ref[e], nxt2_ref[e]
    slot1 = jnp.where(slot + 1 >= W_SLOTS, slot + 1 - W_SLOTS, slot + 1)
    slot2 = jnp.where(slot + 2 >= W_SLOTS, slot + 2 - W_SLOTS, slot + 2)

    def token_rows(token, n):
        return pl.ds(pl.multiple_of(token * TOKEN_ROWS, TOKEN_ROWS), n * TOKEN_ROWS)

    def fetch(ex, s):
        return (pltpu.make_async_copy(wg_hbm.at[ex], wg_buf.at[s], wsem.at[s, 0]),
                pltpu.make_async_copy(wu_hbm.at[ex], wu_buf.at[s], wsem.at[s, 1]),
                pltpu.make_async_copy(wd_hbm.at[ex], wd_buf.at[s], wsem.at[s, 2]))

    def x_copy(token, s):
        return pltpu.make_async_copy(xs_hbm.at[token_rows(token, EXPERT_BLOCK)], xbuf.at[s], xsem.at[s])

    def y_copy(token, s):
        return pltpu.make_async_copy(ybuf.at[s], ys_hbm.at[token_rows(token, EXPERT_BLOCK)], ysem.at[s])

    def tail_copies(token, length):
        out = []
        for piece in TAIL_PIECES:
            bigger = (EXPERT_BLOCK - 1) & ~(2 * piece - 1)
            done = jnp.bitwise_and(length, bigger)
            cp = pltpu.make_async_copy(ytail.at[token_rows(done, piece)], ys_hbm.at[token_rows(token + done, piece)],
                                       tsem)
            out.append((jnp.bitwise_and(length, piece) != 0, cp))
        return out

    def block(s, rows=EXPERT_BLOCK):
        x = _from_token_tiles(xbuf.at[s], rows).astype(BF16)
        g = _dot(x, wg_bf[...])
        u = _dot(x, wu_bf[...])
        return _dot((_silu(g) * u).astype(BF16), wd_bf[...])

    @pl.when(count > 0)
    def _():
        @pl.when(first_ref[e] == 1)
        def _():
            for g in range(X_SLOTS - 1):
                @pl.when(g < nblocks_ref[0])
                def _(g=g):
                    x_copy(blktok_ref[g], g).start()
            for cp in fetch(e, slot):
                cp.start()

            @pl.when(nxt >= 0)
            def _():
                for cp in fetch(nxt, slot1):
                    cp.start()

        for cp in fetch(e, slot):
            cp.wait()

        @pl.when(nxt2 >= 0)
        def _():
            for cp in fetch(nxt2, slot2):
                cp.start()

        wg_bf[...] = wg_buf[slot].astype(BF16)
        wu_bf[...] = wu_buf[slot].astype(BF16)
        wd_bf[...] = wd_buf[slot].astype(BF16)

        def take_x(i):
            g = blk0 + i
            xs_slot = jnp.bitwise_and(g, X_SLOTS - 1)
            x_copy(start, xs_slot).wait()
            ahead = g + (X_SLOTS - 1)

            @pl.when(ahead < nblocks_ref[0])
            def _():
                x_copy(blktok_ref[ahead], jnp.bitwise_and(ahead, X_SLOTS - 1)).start()

            return xs_slot

        def full_block(i, carry):
            y = block(take_x(i))
            j = full0 + i
            ys_slot = jnp.bitwise_and(j, Y_SLOTS - 1)

            @pl.when(j >= Y_SLOTS)
            def _():
                y_copy(start, ys_slot).wait()

            _to_token_tiles(ybuf.at[ys_slot], y)
            y_copy(start + i * EXPERT_BLOCK, ys_slot).start()
            return carry

        lax.fori_loop(0, n_full, full_block, 0)

        @pl.when(tail > 0)
        def _():
            xs_slot = take_x(n_full)
            for pred, cp in tail_copies(start, ptail_ref[e]):
                @pl.when(pred)
                def _(cp=cp):
                    cp.wait()
            units = jnp.right_shift(tail + (TAIL_UNIT - 1), TAIL_UNIT.bit_length() - 1)
            for u in range(1, EXPERT_BLOCK // TAIL_UNIT + 1):
                @pl.when(units == u)
                def _(u=u):
                    rows = u * TAIL_UNIT
                    _to_token_tiles(ytail.at[pl.ds(0, rows * TOKEN_ROWS)], block(xs_slot, rows))
            for pred, cp in tail_copies(start + n_full * EXPERT_BLOCK, tail):
                @pl.when(pred)
                def _(cp=cp):
                    cp.start()

    @pl.when(last_step)
    def _():
        total_full = full0 + n_full
        for back in range(1, Y_SLOTS + 1):
            @pl.when(total_full >= back)
            def _(back=back):
                y_copy(0, jnp.bitwise_and(total_full - back, Y_SLOTS - 1)).wait()
        for pred, cp in tail_copies(0, ltail_ref[0]):
            @pl.when(pred)
            def _(cp=cp):
                cp.wait()


def _max_expert_blocks(n_rows):
    return n_rows // EXPERT_BLOCK + N_EXPERTS


def _expert_metadata(counts, n_rows):
    ids = jnp.arange(N_EXPERTS, dtype=I32)
    used = counts > 0
    starts = jnp.cumsum(counts) - counts
    n_blk = (counts + EXPERT_BLOCK - 1) // EXPERT_BLOCK
    n_full = counts // EXPERT_BLOCK
    tail = counts % EXPERT_BLOCK
    blk0 = jnp.cumsum(n_blk) - n_blk
    full0 = jnp.cumsum(n_full) - n_full
    next_used = lax.cummin(jnp.where(used, ids, N_EXPERTS), reverse=True)
    next_after = jnp.concatenate([next_used[1:], jnp.full((1,), N_EXPERTS, I32)])
    nxt = jnp.where(next_after < N_EXPERTS, next_after, -1)
    ordinal = jnp.cumsum(used.astype(I32)) - 1
    slot = ordinal % W_SLOTS
    first = jnp.logical_and(used, ordinal == 0)
    latest = lax.cummax(jnp.where(tail > 0, ids, -1))
    before = jnp.concatenate([jnp.full((1,), -1, I32), latest[:-1]])
    pick = lambda index, values: jnp.sum(jnp.where(index[:, None] == ids[None, :], values[None, :], 0), axis=1)
    nxt2 = jnp.where(nxt >= 0, pick(nxt, nxt + 1), 0) - 1
    ptail = pick(before, tail)
    ltail = pick(latest[-1:], tail)
    block_ends = jnp.cumsum(n_blk)
    g = jnp.arange(_max_expert_blocks(n_rows), dtype=I32)
    eg = jnp.sum((g[:, None] >= block_ends[None, :]).astype(I32), axis=1)
    blktok = g * EXPERT_BLOCK + pick(eg, starts - blk0 * EXPERT_BLOCK)
    return tuple(v.astype(I32) for v in (starts, counts, nxt, nxt2, slot, first, blk0, full0, ptail, ltail, blktok,
                                         block_ends[-1:]))


def _experts(xs, counts, w_gate, w_up, w_down):
    d = D_MODEL
    meta = _expert_metadata(counts, xs.shape[0] // TOKEN_ROWS - EXPERT_BLOCK)
    hbm = pl.BlockSpec(memory_space=pl.ANY)
    blk = (EXPERT_BLOCK * TOKEN_ROWS, LANES)
    grid_spec = pltpu.PrefetchScalarGridSpec(
        num_scalar_prefetch=len(meta),
        grid=(N_EXPERTS,),
        in_specs=[hbm, hbm, hbm, hbm],
        out_specs=hbm,
        scratch_shapes=[pltpu.VMEM((W_SLOTS, d, EXPERT_FF), F32), pltpu.VMEM((W_SLOTS, d, EXPERT_FF), F32),
                        pltpu.VMEM((W_SLOTS, EXPERT_FF, d), F32),
                        pltpu.VMEM((d, EXPERT_FF), BF16), pltpu.VMEM((d, EXPERT_FF), BF16),
                        pltpu.VMEM((EXPERT_FF, d), BF16),
                        pltpu.VMEM((X_SLOTS,) + blk, WORD), pltpu.VMEM((Y_SLOTS,) + blk, WORD), pltpu.VMEM(blk, WORD),
                        pltpu.SemaphoreType.DMA((W_SLOTS, 3)), pltpu.SemaphoreType.DMA((X_SLOTS,)),
                        pltpu.SemaphoreType.DMA((Y_SLOTS,)), pltpu.SemaphoreType.DMA(())],
    )
    return pl.pallas_call(
        _experts_kernel,
        grid_spec=grid_spec,
        out_shape=jax.ShapeDtypeStruct(xs.shape, WORD),
        compiler_params=_params(("arbitrary",)),
        name="experts",
    )(*meta, xs, w_gate, w_up, w_down)


SC_CORES = 2
SC_SUBCORES = 16
SC_WORKERS = SC_CORES * SC_SUBCORES
SC_CHUNK = 128
SC_RING = 4


def _sc_scatter_rows(planes, idx, n_out_rows):
    n_planes, t, _ = planes.shape
    ranges = SC_WORKERS // n_planes
    n_chunks = t // ranges // SC_CHUNK
    idx = idx.reshape(TOP_K, n_planes, ranges, n_chunks, SC_CHUNK)
    mesh = plsc.VectorSubcoreMesh(core_axis_name="c", subcore_axis_name="s", num_cores=SC_CORES,
                                  num_subcores=SC_SUBCORES)

    def body(planes_hbm, idx_hbm, out_hbm, idx_v, rows_v, lsem, ssem):
        wid = lax.axis_index("s") * SC_CORES + lax.axis_index("c")
        plane = wid % n_planes
        token0 = (wid // n_planes) * (n_chunks * SC_CHUNK)
        for k in range(TOP_K):
            pltpu.sync_copy(idx_hbm.at[k, plane, wid // n_planes], idx_v.at[k])

        def load(slot, c):
            return pltpu.make_async_copy(planes_hbm.at[plane, pl.ds(token0 + c * SC_CHUNK, SC_CHUNK)], rows_v.at[slot],
                                         lsem.at[slot])

        def scatter(slot, c, k):
            return pltpu.make_async_copy(rows_v.at[slot], out_hbm.at[idx_v.at[k, c]], ssem.at[slot])

        @pl.loop(0, n_chunks, step=SC_RING)
        def _(g):
            for slot in range(SC_RING):
                @pl.when(g > 0)
                def _(slot=slot):
                    for k in range(TOP_K):
                        scatter(slot, 0, k).wait()
                load(slot, g + slot).start()
            for slot in range(SC_RING):
                load(slot, g + slot).wait()
                for k in range(TOP_K):
                    scatter(slot, g + slot, k).start()

        for slot in range(SC_RING):
            for k in range(TOP_K):
                scatter(slot, 0, k).wait()

    return pl.kernel(
        body, mesh=mesh,
        out_type=jax.ShapeDtypeStruct((n_out_rows, LANES), planes.dtype),
        scratch_types=[pltpu.VMEM((TOP_K, n_chunks, SC_CHUNK), I32), pltpu.VMEM((SC_RING, SC_CHUNK, LANES), planes.dtype),
                       pltpu.SemaphoreType.DMA((SC_RING,)), pltpu.SemaphoreType.DMA((SC_RING,))],
        name="sc_scatter",
    )(planes, idx)


def _sc_gather_rows(table, idx):
    n_workers, n_chunks, chunk = idx.shape
    rows_per_worker = n_chunks * chunk
    mesh = plsc.VectorSubcoreMesh(core_axis_name="c", subcore_axis_name="s", num_cores=SC_CORES,
                                  num_subcores=SC_SUBCORES)

    def body(table_hbm, idx_hbm, out_hbm, idx_v, rows_v, gsem, wsem):
        wid = lax.axis_index("s") * SC_CORES + lax.axis_index("c")
        base = wid * rows_per_worker
        pltpu.sync_copy(idx_hbm.at[wid], idx_v)

        def write(slot, c):
            return pltpu.make_async_copy(rows_v.at[slot], out_hbm.at[pl.ds(base + c * chunk, chunk)], wsem.at[slot])

        def gather(slot, c):
            return pltpu.make_async_copy(table_hbm.at[idx_v.at[c]], rows_v.at[slot], gsem.at[slot])

        @pl.loop(0, n_chunks, step=SC_RING)
        def _(g):
            for slot in range(SC_RING):
                @pl.when(g > 0)
                def _(slot=slot):
                    write(slot, 0).wait()
                gather(slot, g + slot).start()
            for slot in range(SC_RING):
                gather(slot, g + slot).wait()
                write(slot, g + slot).start()

        for slot in range(SC_RING):
            write(slot, 0).wait()

    return pl.kernel(
        body, mesh=mesh,
        out_type=jax.ShapeDtypeStruct((n_workers * rows_per_worker, LANES), table.dtype),
        scratch_types=[pltpu.VMEM((n_chunks, chunk), I32), pltpu.VMEM((SC_RING, chunk, LANES), table.dtype),
                       pltpu.SemaphoreType.DMA((SC_RING,)), pltpu.SemaphoreType.DMA((SC_RING,))],
        name="sc_gather",
    )(table, idx)


def _combine_kernel(rows_ref, gates_ref, h_ref, x1_ref, gatef_ref, wg_ref, wu_ref, wd_ref, out_ref):
    hb = h_ref[...].astype(BF16)
    shared = _dot((_silu(_dot(hb, wg_ref[...])) * _dot(hb, wu_ref[...])).astype(BF16), wd_ref[...])
    gates = gates_ref[...]

    def expert_rows(k):
        words = [rows_ref[k * TOKEN_ROWS + c] for c in range(TOKEN_ROWS)]
        lows = [lax.bitcast_convert_type(jnp.left_shift(w, 16), F32) for w in words]
        highs = [lax.bitcast_convert_type(jnp.bitwise_and(w, HIGH_HALF), F32) for w in words]
        return jnp.concatenate(lows + highs, axis=1)

    routed = expert_rows(0) * gates[:, 0:1]
    for k in range(1, TOP_K):
        routed = routed + expert_rows(k) * gates[:, k:k + 1]
    out_ref[...] = x1_ref[...] + gatef_ref[...] * (shared + routed)


def _combine(gathered, gates_t, h2, x1, gate_f, w_gate_s, w_up_s, w_down_s, b, s, tm):
    t, d = h2.shape
    tiles_per_seq = s // tm
    row = lambda n: pl.BlockSpec((tm, n), lambda i: (i, 0))
    full = lambda shp: pl.BlockSpec(shp, lambda i: (0,) * len(shp))
    return pl.pallas_call(
        _combine_kernel,
        grid=(t // tm,),
        in_specs=[pl.BlockSpec((TOP_K * TOKEN_ROWS, tm, LANES), lambda i: (0, i, 0)),
                  row(TOP_K), row(d), row(d),
                  pl.BlockSpec((None, 1, d), lambda i: (i // tiles_per_seq, 0, 0)),
                  full((d, EXPERT_FF)), full((d, EXPERT_FF)), full((EXPERT_FF, d))],
        out_specs=row(d),
        out_shape=jax.ShapeDtypeStruct((t, d), F32),
        compiler_params=_params(("arbitrary",)),
        name="combine",
    )(gathered, gates_t, h2, x1, gate_f.reshape(b, 1, d),
      w_gate_s.astype(BF16), w_up_s.astype(BF16), w_down_s.astype(BF16))


def _moe_sublayer(x1, h2, h2t, gate_f, w_router, router_bias, w_gate, w_up, w_down, w_gate_s, w_up_s, w_down_s, b, s,
                  tm=256):
    t = b * s
    eidx, rank, gates, counts = _router(h2, w_router, router_bias)
    pos = _positions(counts, eidx, rank, 4 * tm)
    idx = pos[:, None, :] * TOKEN_ROWS + jnp.arange(TOKEN_ROWS, dtype=I32)[None, :, None]
    xs = _sc_scatter_rows(h2t, idx, (t * TOP_K + EXPERT_BLOCK) * TOKEN_ROWS)
    ys = _experts(xs, counts[:, 0], w_gate, w_up, w_down)
    gathered = _sc_gather_rows(ys, idx.reshape(SC_WORKERS, -1, SC_CHUNK)).reshape(TOP_K * TOKEN_ROWS, t, LANES)
    return _combine(gathered, gates.T, h2, x1, gate_f, w_gate_s, w_up_s, w_down_s, b, s, tm)


def kernel(x, c, w_ada, b_ada, norm_mix_gain, w_in, q_norm_gain, k_norm_gain, rel_bias_table, conv_w, conv_b, dt_bias,
           a_log, d_skip, ssm_norm_gain, w_out, norm_ffn_gain, w_router, router_bias, w_gate_experts, w_up_experts,
           w_down_experts, w_gate_shared, w_up_shared, w_down_shared):
    b, s, d = x.shape
    for layer in range(w_ada.shape[0]):
        mod = _adaln(c, w_ada[layer], b_ada[layer])
        x1, h2, h2t = _mixer_sublayer(x, mod, norm_mix_gain[layer], w_in[layer], q_norm_gain[layer], k_norm_gain[layer],
                                 rel_bias_table, conv_w[layer], conv_b[layer], dt_bias[layer], a_log[layer],
                                 d_skip[layer], ssm_norm_gain[layer], w_out[layer], norm_ffn_gain[layer])
        gate_f = mod[:, 5 * d:]
        out = _moe_sublayer(x1, h2, h2t, gate_f, w_router[layer], router_bias[layer], w_gate_experts[layer],
                            w_up_experts[layer], w_down_experts[layer], w_gate_shared[layer], w_up_shared[layer],
                            w_down_shared[layer], b, s)
        x = out.reshape(b, s, d)
    return x
```

```python
import functools
import math

import numpy as np
import jax
import jax.numpy as jnp
from jax import lax
from jax.experimental import pallas as pl
from jax.experimental.pallas import tpu as pltpu
from jax.experimental.pallas import tpu_sc as plsc

F32 = jnp.float32
BF16 = jnp.bfloat16
I32 = jnp.int32

D_MODEL = 1024
ATTN_HEADS = 8
HEAD_DIM = 64
ATTN_WIDTH = ATTN_HEADS * HEAD_DIM
PATTERNS = ((128, 1), (512, 4), (2048, 16))
WIN_STEPS = 128
REL_BUCKETS = 32
REL_MAX_DISTANCE = 2048
SSM_HEADS = 24
SSM_HEAD_DIM = 64
SSM_WIDTH = SSM_HEADS * SSM_HEAD_DIM
SSM_GROUPS = 4
HEADS_PER_GROUP = SSM_HEADS // SSM_GROUPS
GROUP_WIDTH = SSM_WIDTH // SSM_GROUPS
SSM_STATE = 128
SSM_CONV = 4
SSM_CHUNK = 128
CONV_CH = SSM_WIDTH + 2 * SSM_GROUPS * SSM_STATE
N_EXPERTS = 256
TOP_K = 8
N_EXPERT_GROUPS = 8
EXPERTS_PER_GROUP = N_EXPERTS // N_EXPERT_GROUPS
TOPK_GROUPS = 4
EXPERT_FF = 256
ROUTED_SCALE = 2.5
NORM_EPS = 1e-6

LANES = 128
SUBLANES = 8
NEG_BIG = -1e30
VMEM_LIMIT = 56 * 1024 * 1024


def _params(sem, vmem=VMEM_LIMIT):
    return pltpu.CompilerParams(dimension_semantics=sem, vmem_limit_bytes=vmem)


def _sigmoid(x):
    return 1.0 / (1.0 + jnp.exp(-x))


def _silu(x):
    return x * _sigmoid(x)


def _split3(x):
    hi = x.astype(BF16)
    r = x - hi.astype(F32)
    mid = r.astype(BF16)
    lo = (r - mid.astype(F32)).astype(BF16)
    return hi, mid, lo


def _dot(a, b):
    return jnp.dot(a, b, preferred_element_type=F32)


def _dot_nt(a, b):
    return lax.dot_general(a, b, (((1,), (1,)), ((), ())), preferred_element_type=F32)


def _dot_exact_rhs(a, b_exact):
    hi, mid, lo = _split3(a)
    return _dot(hi, b_exact) + _dot(mid, b_exact) + _dot(lo, b_exact)


def _dot_exact_lhs(a_exact, b):
    hi, mid, lo = _split3(b)
    return _dot(a_exact, hi) + _dot(a_exact, mid) + _dot(a_exact, lo)


def _adaln_kernel(c_ref, w_ref, b_ref, o_ref):
    s = _silu(c_ref[...]).astype(BF16)
    o_ref[...] = _dot(s, w_ref[...].astype(BF16)) + b_ref[...]


def _adaln(c, w_ada, b_ada):
    b, d = c.shape
    n = w_ada.shape[1]
    rows = SUBLANES
    c_pad = jnp.zeros((rows, d), F32).at[:b].set(c)
    tn = 1024
    out = pl.pallas_call(
        _adaln_kernel,
        grid=(n // tn,),
        in_specs=[pl.BlockSpec((rows, d), lambda j: (0, 0)),
                  pl.BlockSpec((d, tn), lambda j: (0, j)),
                  pl.BlockSpec((1, tn), lambda j: (0, j))],
        out_specs=pl.BlockSpec((rows, tn), lambda j: (0, j)),
        out_shape=jax.ShapeDtypeStruct((rows, n), F32),
        compiler_params=_params(("arbitrary",)),
        name="adaln",
    )(c_pad, w_ada, b_ada.reshape(1, n))
    return out[:b]


def _inproj_kernel(x_ref, shift_ref, scale_ref, g_ref, wqkv_ref, wz_ref, wxbc_ref, wdt_ref,
                   qg_ref, kg_ref, hmean_ref, q_ref, k_ref, v_ref, z_ref, xbc_ref, dt_ref):
    x = x_ref[...]
    ms = jnp.mean(x * x, axis=-1, keepdims=True)
    h = x * lax.rsqrt(ms + NORM_EPS) * g_ref[...]
    h = h * (1.0 + scale_ref[...]) + shift_ref[...]
    hb = h.astype(BF16)

    hmean = hmean_ref[...]

    def head_norm(t, gain):
        ss = _dot_exact_rhs(t * t, hmean)
        return t * lax.rsqrt(ss + NORM_EPS) * gain

    q = _dot(hb, wqkv_ref[:, 0:ATTN_WIDTH])
    q_ref[...] = head_norm(q, qg_ref[...]) * (HEAD_DIM ** -0.5)
    k = _dot(hb, wqkv_ref[:, ATTN_WIDTH:2 * ATTN_WIDTH])
    k_ref[...] = head_norm(k, kg_ref[...])
    v_ref[...] = _dot(hb, wqkv_ref[:, 2 * ATTN_WIDTH:3 * ATTN_WIDTH])
    for c0 in range(0, SSM_WIDTH, 512):
        z_ref[:, c0:c0 + 512] = _dot(hb, wz_ref[:, c0:c0 + 512])
    for c0 in range(0, CONV_CH, 512):
        xbc_ref[:, c0:c0 + 512] = _dot(hb, wxbc_ref[:, c0:c0 + 512])
    dt_ref[...] = _dot(hb, wdt_ref[...])


def _in_proj(x, shift, scale, gain, w_in, q_gain, k_gain, tm=256):
    b, s, d = x.shape
    t = b * s
    tiles_per_seq = s // tm
    w = w_in.astype(BF16)
    o_z = 3 * ATTN_WIDTH
    o_x = o_z + SSM_WIDTH
    o_dt = o_x + CONV_CH
    w_qkv, w_z, w_xbc = w[:, :o_z], w[:, o_z:o_x], w[:, o_x:o_dt]
    w_dt = jnp.zeros((d, LANES), BF16).at[:, :SSM_HEADS].set(w[:, o_dt:])
    head_of = np.arange(ATTN_WIDTH) // HEAD_DIM
    hmean = jnp.asarray((head_of[:, None] == head_of[None, :]).astype(np.float32) / HEAD_DIM, BF16)
    full = lambda shp: pl.BlockSpec(shp, lambda i: (0,) * len(shp))
    row = lambda n: pl.BlockSpec((tm, n), lambda i: (i, 0))
    per_batch = pl.BlockSpec((None, 1, d), lambda i: (i // tiles_per_seq, 0, 0))
    outs = pl.pallas_call(
        _inproj_kernel,
        grid=(t // tm,),
        in_specs=[row(d), per_batch, per_batch, full((1, d)),
                  full((d, o_z)), full((d, SSM_WIDTH)), full((d, CONV_CH)), full((d, LANES)),
                  full((1, ATTN_WIDTH)), full((1, ATTN_WIDTH)), full((ATTN_WIDTH, ATTN_WIDTH))],
        out_specs=[row(ATTN_WIDTH), row(ATTN_WIDTH), row(ATTN_WIDTH), row(SSM_WIDTH), row(CONV_CH), row(LANES)],
        out_shape=[jax.ShapeDtypeStruct((t, n), F32)
                   for n in (ATTN_WIDTH, ATTN_WIDTH, ATTN_WIDTH, SSM_WIDTH, CONV_CH, LANES)],
        compiler_params=_params(("arbitrary",)),
        name="in_proj",
    )(x.reshape(t, d), shift.reshape(b, 1, d), scale.reshape(b, 1, d), gain.reshape(1, d),
      w_qkv, w_z, w_xbc, w_dt,
      jnp.tile(q_gain, ATTN_HEADS).reshape(1, ATTN_WIDTH), jnp.tile(k_gain, ATTN_HEADS).reshape(1, ATTN_WIDTH), hmean)
    return outs


def _t5_causal_buckets(distance):
    n = np.maximum(distance, 0)
    max_exact = REL_BUCKETS // 2
    large = max_exact + (np.log(np.maximum(n, 1) / max_exact) / math.log(REL_MAX_DISTANCE / max_exact)
                         * (REL_BUCKETS - max_exact)).astype(np.int64)
    large = np.minimum(large, REL_BUCKETS - 1)
    return np.where(n < max_exact, n, large).astype(np.int32)


def _window_bias(rel_bias_table, dilation):
    qi = np.arange(WIN_STEPS)[:, None]
    kj = np.arange(2 * WIN_STEPS)[None, :]
    dist = qi + WIN_STEPS - kj
    band = (dist >= 0) & (dist <= WIN_STEPS)
    onehot = (_t5_causal_buckets(dist * dilation).reshape(-1, 1) == np.arange(REL_BUCKETS)[None, :]).astype(np.float32)
    bias = jnp.dot(rel_bias_table.astype(F32).T, jnp.asarray(onehot).T, precision=lax.Precision.HIGHEST)
    bias = bias.reshape(ATTN_HEADS, WIN_STEPS, 2 * WIN_STEPS)
    return jnp.where(jnp.asarray(band)[None], bias, NEG_BIG)


ATTN_TOKENS = max(w for w, _ in PATTERNS)
ATTN_UNROLL = 16


def _attn_kernel(q_ref, kp_ref, kc_ref, vp_ref, vc_ref, bias_ref, out_ref, kw, vw, o_acc, l_acc):
    tb = ATTN_TOKENS
    first = pl.program_id(2) == 0
    kw[0:tb] = kp_ref[...]
    kw[tb:2 * tb] = kc_ref[...]
    vw[0:tb] = vp_ref[...]
    vw[tb:2 * tb] = vc_ref[...]
    lane = lax.broadcasted_iota(I32, (WIN_STEPS, LANES), 1)
    head0 = lane < HEAD_DIM
    col = lax.broadcasted_iota(I32, (WIN_STEPS, 2 * WIN_STEPS), 1)
    in_prev = col < WIN_STEPS

    for p, (_, d) in enumerate(PATTERNS):
        shift = d.bit_length() - 1
        n_blocks = tb // WIN_STEPS

        def rows(start, n, d=d):
            return pl.ds(start, n, stride=d) if d > 1 else pl.ds(start, n)

        def body(it, carry, p=p, d=d, shift=shift, rows=rows):
            for u in range(ATTN_UNROLL):
                idx = it * ATTN_UNROLL + u
                r = jnp.bitwise_and(idx, d - 1)
                j = jnp.right_shift(idx, shift)
                qs = j * (WIN_STEPS * d) + r
                q = q_ref[rows(qs, WIN_STEPS), :]
                k = kw[rows(tb + qs - WIN_STEPS * d, 2 * WIN_STEPS), :].astype(BF16)
                v = vw[rows(tb + qs - WIN_STEPS * d, 2 * WIN_STEPS), :].astype(BF16)
                no_prev = jnp.logical_and(in_prev, jnp.logical_and(first, j == 0))
                o_h, lse_h = [], []
                for h in range(2):
                    qh = jnp.where(head0 if h == 0 else jnp.logical_not(head0), q, 0.0).astype(BF16)
                    s = _dot_nt(qh, k) + bias_ref[p, h]
                    s = jnp.where(no_prev, NEG_BIG, s)
                    m = jnp.max(s, axis=-1, keepdims=True)
                    e = jnp.exp(s - m)
                    denom = jnp.sum(e, axis=-1, keepdims=True)
                    o_h.append(_dot(e.astype(BF16), v) / denom)
                    lse_h.append(m + jnp.log(denom))
                o_acc[p, rows(qs, WIN_STEPS), :] = jnp.where(head0, o_h[0], o_h[1])
                l_acc[p, rows(qs, WIN_STEPS), :] = jnp.where(head0, lse_h[0], lse_h[1])
            return carry

        lax.fori_loop(0, n_blocks // ATTN_UNROLL, body, 0)

    chunk = 256
    for c0 in range(0, tb, chunk):
        l1, l2, l3 = (l_acc[p, c0:c0 + chunk, :] for p in range(3))
        m = jnp.maximum(jnp.maximum(l1, l2), l3)
        e1, e2, e3 = jnp.exp(l1 - m), jnp.exp(l2 - m), jnp.exp(l3 - m)
        num = e1 * o_acc[0, c0:c0 + chunk, :] + e2 * o_acc[1, c0:c0 + chunk, :] + e3 * o_acc[2, c0:c0 + chunk, :]
        out_ref[c0:c0 + chunk, :] = num / (e1 + e2 + e3)


def _attention(q, k, v, bias):
    b, s, w = q.shape
    tb = ATTN_TOKENS
    pairs = ATTN_HEADS // 2
    cur = pl.BlockSpec((None, tb, LANES), lambda bi, hp, i: (bi, i, hp))
    prev = pl.BlockSpec((None, tb, LANES), lambda bi, hp, i: (bi, jnp.maximum(i - 1, 0), hp))
    return pl.pallas_call(
        _attn_kernel,
        grid=(b, pairs, s // tb),
        in_specs=[cur, prev, cur, prev, cur,
                  pl.BlockSpec((len(PATTERNS), 2, WIN_STEPS, 2 * WIN_STEPS), lambda bi, hp, i: (0, hp, 0, 0))],
        out_specs=cur,
        out_shape=jax.ShapeDtypeStruct((b, s, w), F32),
        scratch_shapes=[pltpu.VMEM((2 * tb, LANES), F32), pltpu.VMEM((2 * tb, LANES), F32),
                        pltpu.VMEM((len(PATTERNS), tb, LANES), F32), pltpu.VMEM((len(PATTERNS), tb, LANES), F32)],
        compiler_params=_params(("arbitrary",) * 3),
        name="attention",
    )(q, k, k, v, v, bias)


SSD_CHUNKS_PER_STEP = 2


def _ssd_kernel(xbc_ref, halo_ref, z_ref, dtraw_ref, convw_ref, convb_ref, dtb_ref, alog_ref, dskip_ref, gain_ref,
                expand_ref, tril_ref, y_ref, state_ref):
    c = pl.program_id(1)

    @pl.when(c == 0)
    def _():
        state_ref[...] = jnp.zeros_like(state_ref)

    halo = jnp.where(c == 0, 0.0, halo_ref[...])
    for sub in range(SSD_CHUNKS_PER_STEP):
        rows = slice(sub * SSM_CHUNK, (sub + 1) * SSM_CHUNK)
        x = xbc_ref[rows, :]
        w = convw_ref[...]
        acc = x * w[SSM_CONV - 1:SSM_CONV, :] + convb_ref[...]
        row8 = lax.broadcasted_iota(I32, (SUBLANES, CONV_CH), 0)
        for shift in range(1, SSM_CONV):
            xs = pltpu.roll(x, shift, axis=0)
            hs = pltpu.roll(halo, shift, axis=0)
            head = jnp.where(row8 < shift, hs, xs[0:SUBLANES])
            xs = jnp.concatenate([head, xs[SUBLANES:]], axis=0)
            acc = acc + xs * w[SSM_CONV - 1 - shift:SSM_CONV - shift, :]
        act = _silu(acc)
        x_s = act[:, :SSM_WIDTH]
        bc0 = SSM_WIDTH
        cc0 = SSM_WIDTH + SSM_GROUPS * SSM_STATE

        t = dtraw_ref[rows, :] + dtb_ref[...]
        dt = jnp.maximum(t, 0.0) + jnp.log(1.0 + jnp.exp(-jnp.abs(t)))
        a = dt * (-jnp.exp(alog_ref[...]))
        a_cs = _dot_exact_lhs(tril_ref[...], a)
        a_cs_t = a_cs.T
        a_last = a_cs[SSM_CHUNK - 1:SSM_CHUNK, :]
        expand = expand_ref[...]
        dt_e = _dot_exact_rhs(dt, expand)
        ea_e = _dot_exact_rhs(jnp.exp(a_cs), expand)
        dte_e = _dot_exact_rhs(jnp.exp(a_last - a_cs), expand)
        xdt = x_s * dt_e
        xw = (xdt * dte_e).astype(BF16)
        xdt_b = xdt.astype(BF16)

        li = lax.broadcasted_iota(I32, (SSM_CHUNK, SSM_CHUNK), 0)
        si = lax.broadcasted_iota(I32, (SSM_CHUNK, SSM_CHUNK), 1)
        causal = li >= si

        ys = []
        for g in range(SSM_GROUPS):
            gs = slice(g * GROUP_WIDTH, (g + 1) * GROUP_WIDTH)
            b_g = act[:, bc0 + g * SSM_STATE:bc0 + (g + 1) * SSM_STATE]
            c_g = act[:, cc0 + g * SSM_STATE:cc0 + (g + 1) * SSM_STATE].astype(BF16)
            cb = _dot_nt(c_g, b_g.astype(BF16))
            state = state_ref[g]
            y_off = _dot(c_g, state.astype(BF16)) * ea_e[:, gs]
            parts = []
            for j in range(HEADS_PER_GROUP):
                hh = g * HEADS_PER_GROUP + j
                seg = a_cs[:, hh:hh + 1] - a_cs_t[hh:hh + 1, :]
                decay = jnp.exp(jnp.where(causal, seg, NEG_BIG))
                m = (cb * decay).astype(BF16)
                parts.append(_dot(m, xdt_b[:, hh * SSM_HEAD_DIM:(hh + 1) * SSM_HEAD_DIM]))
            ys.append(jnp.concatenate(parts, axis=1) + y_off)
            state_ref[g] = state * ea_e[SSM_CHUNK - 1:SSM_CHUNK, gs] + _dot(b_g.T.astype(BF16), xw[:, gs])
        y = jnp.concatenate(ys, axis=1) + dskip_ref[...] * x_s
        y = y * _silu(z_ref[rows, :])
        gain = gain_ref[...]
        for g in range(SSM_GROUPS):
            gs = slice(g * GROUP_WIDTH, (g + 1) * GROUP_WIDTH)
            yg = y[:, gs]
            ms = jnp.mean(yg * yg, axis=-1, keepdims=True)
            y_ref[rows, gs] = yg * lax.rsqrt(ms + NORM_EPS) * gain[:, gs]
        halo = x[SSM_CHUNK - SUBLANES:, :]


def _ssd(xbc, z, dt_raw, conv_w, conv_b, dt_bias, a_log, d_skip, norm_gain, b, s):
    t = b * s
    step_rows = SSD_CHUNKS_PER_STEP * SSM_CHUNK
    nc = s // step_rows
    pad_heads = lambda v: jnp.zeros((1, LANES), F32).at[0, :SSM_HEADS].set(v)
    head_of_lane = np.arange(SSM_WIDTH) // SSM_HEAD_DIM
    expand = jnp.asarray((np.arange(LANES)[:, None] == head_of_lane[None, :]).astype(np.float32), BF16)
    tril = jnp.asarray(np.tril(np.ones((SSM_CHUNK, SSM_CHUNK), np.float32)), BF16)
    halo_blocks = step_rows // SUBLANES
    chunk = lambda n: pl.BlockSpec((step_rows, n), lambda bi, c: (bi * nc + c, 0))
    full = lambda shp: pl.BlockSpec(shp, lambda bi, c: (0,) * len(shp))
    halo = pl.BlockSpec((SUBLANES, CONV_CH), lambda bi, c: (jnp.maximum((bi * nc + c) * halo_blocks - 1, 0), 0))
    return pl.pallas_call(
        _ssd_kernel,
        grid=(b, nc),
        in_specs=[chunk(CONV_CH), halo, chunk(SSM_WIDTH), chunk(LANES),
                  full((SSM_CONV, CONV_CH)), full((1, CONV_CH)), full((1, LANES)), full((1, LANES)),
                  full((1, SSM_WIDTH)), full((1, SSM_WIDTH)), full((LANES, SSM_WIDTH)), full((SSM_CHUNK, SSM_CHUNK))],
        out_specs=chunk(SSM_WIDTH),
        out_shape=jax.ShapeDtypeStruct((t, SSM_WIDTH), F32),
        scratch_shapes=[pltpu.VMEM((SSM_GROUPS, SSM_STATE, GROUP_WIDTH), F32)],
        compiler_params=_params(("arbitrary", "arbitrary")),
        name="ssd",
    )(xbc, xbc, z, dt_raw, conv_w, conv_b.reshape(1, CONV_CH), pad_heads(dt_bias), pad_heads(a_log),
      jnp.repeat(d_skip, SSM_HEAD_DIM).reshape(1, SSM_WIDTH), norm_gain.reshape(1, SSM_WIDTH), expand, tril)


WORD = jnp.int32
TOKEN_ROWS = D_MODEL // (2 * LANES)
HIGH_HALF = np.int32(-65536)


def _to_token_tiles(ref, x):
    n = x.shape[0]
    for c in range(TOKEN_ROWS):
        lo = lax.bitcast_convert_type(x[:, c * LANES:(c + 1) * LANES].astype(BF16).astype(F32), WORD)
        hi = lax.bitcast_convert_type(x[:, (c + TOKEN_ROWS) * LANES:(c + TOKEN_ROWS + 1) * LANES]
                                      .astype(BF16).astype(F32), WORD)
        ref[pl.ds(c, n, stride=TOKEN_ROWS), :] = jnp.bitwise_or(lax.shift_right_logical(lo, 16),
                                                                 jnp.bitwise_and(hi, HIGH_HALF))


def _from_token_tiles(ref, n, token0=0):
    lows, highs = [], []
    for c in range(TOKEN_ROWS):
        word = ref[pl.ds(token0 * TOKEN_ROWS + c, n, stride=TOKEN_ROWS), :]
        lows.append(lax.bitcast_convert_type(jnp.left_shift(word, 16), F32))
        highs.append(lax.bitcast_convert_type(jnp.bitwise_and(word, HIGH_HALF), F32))
    return jnp.concatenate(lows + highs, axis=1)


def _outproj_kernel(attn_ref, ssm_ref, x_ref, gate_ref, shift_ref, scale_ref, g_ref, wa_ref, ws_ref, x1_ref, h2_ref,
                    h2t_ref):
    mixed = _dot(attn_ref[...].astype(BF16), wa_ref[...]) + _dot(ssm_ref[...].astype(BF16), ws_ref[...])
    x1 = x_ref[...] + gate_ref[...] * mixed
    x1_ref[...] = x1
    ms = jnp.mean(x1 * x1, axis=-1, keepdims=True)
    h = x1 * lax.rsqrt(ms + NORM_EPS) * g_ref[...]
    h2 = h * (1.0 + scale_ref[...]) + shift_ref[...]
    h2_ref[...] = h2
    for c in range(TOKEN_ROWS):
        lo = lax.bitcast_convert_type(h2[:, c * LANES:(c + 1) * LANES].astype(BF16).astype(F32), WORD)
        hi = lax.bitcast_convert_type(h2[:, (c + TOKEN_ROWS) * LANES:(c + TOKEN_ROWS + 1) * LANES]
                                      .astype(BF16).astype(F32), WORD)
        h2t_ref[c] = jnp.bitwise_or(lax.shift_right_logical(lo, 16), jnp.bitwise_and(hi, HIGH_HALF))


def _out_proj(attn, ssm, x, gate, shift, scale, gain, w_out, b, s, tm=512):
    t = b * s
    d = D_MODEL
    tiles_per_seq = s // tm
    w = w_out.astype(BF16)
    row = lambda n: pl.BlockSpec((tm, n), lambda i: (i, 0))
    full = lambda shp: pl.BlockSpec(shp, lambda i: (0,) * len(shp))
    per_batch = pl.BlockSpec((None, 1, d), lambda i: (i // tiles_per_seq, 0, 0))
    return pl.pallas_call(
        _outproj_kernel,
        grid=(t // tm,),
        in_specs=[row(ATTN_WIDTH), row(SSM_WIDTH), row(d), per_batch, per_batch, per_batch, full((1, d)),
                  full((ATTN_WIDTH, d)), full((SSM_WIDTH, d))],
        out_specs=[row(d), row(d), pl.BlockSpec((TOKEN_ROWS, tm, LANES), lambda i: (0, i, 0))],
        out_shape=[jax.ShapeDtypeStruct((t, d), F32)] * 2 + [jax.ShapeDtypeStruct((TOKEN_ROWS, t, LANES), WORD)],
        compiler_params=_params(("arbitrary",)),
        name="out_proj",
    )(attn.reshape(t, ATTN_WIDTH), ssm, x.reshape(t, d),
      gate.reshape(b, 1, d), shift.reshape(b, 1, d), scale.reshape(b, 1, d), gain.reshape(1, d),
      w[:ATTN_WIDTH], w[ATTN_WIDTH:])


def _mixer_sublayer(x, mod, norm_mix_gain, w_in, q_norm_gain, k_norm_gain, rel_bias_table, conv_w, conv_b, dt_bias,
                    a_log, d_skip, ssm_norm_gain, w_out, norm_ffn_gain):
    b, s, d = x.shape
    shift_m, scale_m, gate_m, shift_f, scale_f, _ = jnp.split(mod, 6, axis=-1)
    q, k, v, z, xbc, dt_raw = _in_proj(x, shift_m, scale_m, norm_mix_gain, w_in, q_norm_gain, k_norm_gain)
    bias = jnp.stack([_window_bias(rel_bias_table, dilation) for _, dilation in PATTERNS])
    attn = _attention(q.reshape(b, s, ATTN_WIDTH), k.reshape(b, s, ATTN_WIDTH), v.reshape(b, s, ATTN_WIDTH), bias)
    ssm = _ssd(xbc, z, dt_raw, conv_w, conv_b, dt_bias, a_log, d_skip, ssm_norm_gain, b, s)
    return _out_proj(attn, ssm, x, gate_m, shift_f, scale_f, norm_ffn_gain, w_out, b, s)


def _first_argmax(v, iota, limit):
    m = jnp.max(v, axis=0, keepdims=True)
    idx = jnp.min(jnp.where(v == m, iota, limit), axis=0, keepdims=True)
    return m, idx


def _router_kernel(h_ref, wt_ref, bias_ref, upper_ref, eidx_ref, rank_ref, gate_ref, counts_ref, carry_ref):
    @pl.when(pl.program_id(0) == 0)
    def _():
        carry_ref[...] = jnp.zeros_like(carry_ref)

    tm = h_ref.shape[0]
    h = h_ref[...]
    wt = wt_ref[...]
    h_hi = h.astype(BF16)
    h_lo = (h - h_hi.astype(F32)).astype(BF16)
    w_hi = wt.astype(BF16)
    w_lo = (wt - w_hi.astype(F32)).astype(BF16)
    logits = _dot_nt(w_hi, h_hi) + _dot_nt(w_hi, h_lo) + _dot_nt(w_lo, h_hi)
    scores = _sigmoid(logits)
    choice = scores + bias_ref[...]
    neg_inf = -jnp.inf

    iota_g = lax.broadcasted_iota(I32, (EXPERTS_PER_GROUP, tm), 0).astype(F32)
    group_rows = []
    for g in range(N_EXPERT_GROUPS):
        v = choice[g * EXPERTS_PER_GROUP:(g + 1) * EXPERTS_PER_GROUP]
        m1, i1 = _first_argmax(v, iota_g, float(EXPERTS_PER_GROUP))
        m2 = jnp.max(jnp.where(iota_g == i1, neg_inf, v), axis=0, keepdims=True)
        group_rows.append(m1 + m2)
    group_scores = jnp.concatenate(group_rows, axis=0)

    iota_n = lax.broadcasted_iota(I32, (N_EXPERT_GROUPS, tm), 0).astype(F32)
    chosen = jnp.zeros((N_EXPERT_GROUPS, tm), F32)
    for _ in range(TOPK_GROUPS):
        _, gi = _first_argmax(group_scores, iota_n, float(N_EXPERT_GROUPS))
        hit = iota_n == gi
        chosen = jnp.where(hit, 1.0, chosen)
        group_scores = jnp.where(hit, neg_inf, group_scores)

    masked = jnp.concatenate(
        [jnp.where(chosen[g:g + 1] > 0.0, choice[g * EXPERTS_PER_GROUP:(g + 1) * EXPERTS_PER_GROUP], neg_inf)
         for g in range(N_EXPERT_GROUPS)], axis=0)

    iota_e = lax.broadcasted_iota(I32, (N_EXPERTS, tm), 0).astype(F32)
    picked, gates = [], []
    onehot = jnp.zeros((N_EXPERTS, tm), F32)
    for _ in range(TOP_K):
        _, ei = _first_argmax(masked, iota_e, float(N_EXPERTS))
        hit = iota_e == ei
        gates.append(jnp.sum(jnp.where(hit, scores, 0.0), axis=0, keepdims=True))
        masked = jnp.where(hit, neg_inf, masked)
        onehot = jnp.where(hit, 1.0, onehot)
        picked.append(ei)
    gate_sum = gates[0]
    for gk in gates[1:]:
        gate_sum = gate_sum + gk

    base = _dot(onehot.astype(BF16), upper_ref[...]) + carry_ref[...]
    ranks = [jnp.sum(jnp.where(iota_e == ei, base, 0.0), axis=0, keepdims=True) for ei in picked]
    carry_ref[...] = carry_ref[...] + jnp.sum(onehot, axis=1, keepdims=True)

    eidx_ref[...] = jnp.concatenate(picked, axis=0).astype(I32)
    rank_ref[...] = jnp.concatenate(ranks, axis=0).astype(I32)
    gate_ref[...] = jnp.concatenate([gk / gate_sum * ROUTED_SCALE for gk in gates], axis=0)
    counts_ref[...] = carry_ref[...].astype(I32)


def _router(h2, w_router, router_bias, tm=256):
    t, d = h2.shape
    upper = jnp.asarray(np.triu(np.ones((tm, tm), np.float32), 1), BF16)
    tok = pl.BlockSpec((TOP_K, tm), lambda i: (0, i))
    full = lambda shp: pl.BlockSpec(shp, lambda i: (0,) * len(shp))
    return pl.pallas_call(
        _router_kernel,
        grid=(t // tm,),
        in_specs=[pl.BlockSpec((tm, d), lambda i: (i, 0)), full((N_EXPERTS, d)), full((N_EXPERTS, 1)), full((tm, tm))],
        out_specs=[tok, tok, tok, full((N_EXPERTS, 1))],
        out_shape=[jax.ShapeDtypeStruct((TOP_K, t), I32), jax.ShapeDtypeStruct((TOP_K, t), I32),
                   jax.ShapeDtypeStruct((TOP_K, t), F32), jax.ShapeDtypeStruct((N_EXPERTS, 1), I32)],
        scratch_shapes=[pltpu.VMEM((N_EXPERTS, 1), F32)],
        compiler_params=_params(("arbitrary",)),
        name="router",
    )(h2, w_router.T, router_bias.reshape(N_EXPERTS, 1), upper)


def _positions_kernel(counts_ref, lower_ref, eidx_ref, rank_ref, pos_ref):
    tm = eidx_ref.shape[1]
    counts = jnp.broadcast_to(counts_ref[...].astype(F32), (N_EXPERTS, LANES))
    offsets = _dot_exact_lhs(lower_ref[...], counts)[:, 0:1]
    iota_e = lax.broadcasted_iota(I32, (N_EXPERTS, tm), 0).astype(F32)
    e = eidx_ref[...].astype(F32)
    rows = [jnp.sum(jnp.where(iota_e == e[k:k + 1], offsets, 0.0), axis=0, keepdims=True) for k in range(TOP_K)]
    pos_ref[...] = jnp.concatenate(rows, axis=0).astype(I32) + rank_ref[...]


def _positions(counts, eidx, rank, tm):
    t = eidx.shape[1]
    lower = jnp.asarray(np.tril(np.ones((N_EXPERTS, N_EXPERTS), np.float32), -1), BF16)
    tok = pl.BlockSpec((TOP_K, tm), lambda i: (0, i))
    return pl.pallas_call(
        _positions_kernel,
        grid=(t // tm,),
        in_specs=[pl.BlockSpec((N_EXPERTS, 1), lambda i: (0, 0)), pl.BlockSpec((N_EXPERTS, N_EXPERTS), lambda i: (0, 0)),
                  tok, tok],
        out_specs=tok,
        out_shape=jax.ShapeDtypeStruct((TOP_K, t), I32),
        compiler_params=_params(("arbitrary",)),
        name="positions",
    )(counts, lower, eidx, rank)


EXPERT_BLOCK = 512
TAIL_UNIT = 128
TAIL_PIECES = tuple(1 << i for i in reversed(range(EXPERT_BLOCK.bit_length() - 1)))
W_SLOTS = 3
X_SLOTS = 4
Y_SLOTS = 4


def _experts_kernel(start_ref, count_ref, nxt_ref, nxt2_ref, slot_ref, first_ref, blk0_ref, full0_ref, ptail_ref,
                    ltail_ref, blktok_ref, nblocks_ref, xs_hbm, wg_hbm, wu_hbm, wd_hbm, ys_hbm,
                    wg_buf, wu_buf, wd_buf, wg_bf, wu_bf, wd_bf, xbuf, ybuf, ytail, wsem, xsem, ysem, tsem):
    e = pl.program_id(0)
    last_step = e == pl.num_programs(0) - 1
    start, count = start_ref[e], count_ref[e]
    n_full = jnp.right_shift(count, EXPERT_BLOCK.bit_length() - 1)
    tail = jnp.bitwise_and(count, EXPERT_BLOCK - 1)
    n_blk = n_full + (tail > 0).astype(I32)
    blk0, full0 = blk0_ref[e], full0_ref[e]
    slot, nxt, nxt2 = slot_ref[e], nxt_ref[e], nxt2_ref[e]
    slot1 = jnp.where(slot + 1 >= W_SLOTS, slot + 1 - W_SLOTS, slot + 1)
    slot2 = jnp.where(slot + 2 >= W_SLOTS, slot + 2 - W_SLOTS, slot + 2)

    def token_rows(token, n):
        return pl.ds(pl.multiple_of(token * TOKEN_ROWS, TOKEN_ROWS), n * TOKEN_ROWS)

    def fetch(ex, s):
        return (pltpu.make_async_copy(wg_hbm.at[ex], wg_buf.at[s], wsem.at[s, 0]),
                pltpu.make_async_copy(wu_hbm.at[ex], wu_buf.at[s], wsem.at[s, 1]),
                pltpu.make_async_copy(wd_hbm.at[ex], wd_buf.at[s], wsem.at[s, 2]))

    def x_copy(token, s):
        return pltpu.make_async_copy(xs_hbm.at[token_rows(token, EXPERT_BLOCK)], xbuf.at[s], xsem.at[s])

    def y_copy(token, s):
        return pltpu.make_async_copy(ybuf.at[s], ys_hbm.at[token_rows(token, EXPERT_BLOCK)], ysem.at[s])

    def tail_copies(token, length):
        out = []
        for piece in TAIL_PIECES:
            bigger = (EXPERT_BLOCK - 1) & ~(2 * piece - 1)
            done = jnp.bitwise_and(length, bigger)
            cp = pltpu.make_async_copy(ytail.at[token_rows(done, piece)], ys_hbm.at[token_rows(token + done, piece)],
                                       tsem)
            out.append((jnp.bitwise_and(length, piece) != 0, cp))
        return out

    def block(s, rows=EXPERT_BLOCK):
        x = _from_token_tiles(xbuf.at[s], rows).astype(BF16)
        g = _dot(x, wg_bf[...])
        u = _dot(x, wu_bf[...])
        return _dot((_silu(g) * u).astype(BF16), wd_bf[...])

    @pl.when(count > 0)
    def _():
        @pl.when(first_ref[e] == 1)
        def _():
            for g in range(X_SLOTS - 1):
                @pl.when(g < nblocks_ref[0])
                def _(g=g):
                    x_copy(blktok_ref[g], g).start()
            for cp in fetch(e, slot):
                cp.start()

            @pl.when(nxt >= 0)
            def _():
                for cp in fetch(nxt, slot1):
                    cp.start()

        for cp in fetch(e, slot):
            cp.wait()

        @pl.when(nxt2 >= 0)
        def _():
            for cp in fetch(nxt2, slot2):
                cp.start()

        wg_bf[...] = wg_buf[slot].astype(BF16)
        wu_bf[...] = wu_buf[slot].astype(BF16)
        wd_bf[...] = wd_buf[slot].astype(BF16)

        def take_x(i):
            g = blk0 + i
            xs_slot = jnp.bitwise_and(g, X_SLOTS - 1)
            x_copy(start, xs_slot).wait()
            ahead = g + (X_SLOTS - 1)

            @pl.when(ahead < nblocks_ref[0])
            def _():
                x_copy(blktok_ref[ahead], jnp.bitwise_and(ahead, X_SLOTS - 1)).start()

            return xs_slot

        def full_block(i, carry):
            y = block(take_x(i))
            j = full0 + i
            ys_slot = jnp.bitwise_and(j, Y_SLOTS - 1)

            @pl.when(j >= Y_SLOTS)
            def _():
                y_copy(start, ys_slot).wait()

            _to_token_tiles(ybuf.at[ys_slot], y)
            y_copy(start + i * EXPERT_BLOCK, ys_slot).start()
            return carry

        lax.fori_loop(0, n_full, full_block, 0)

        @pl.when(tail > 0)
        def _():
            xs_slot = take_x(n_full)
            for pred, cp in tail_copies(start, ptail_ref[e]):
                @pl.when(pred)
                def _(cp=cp):
                    cp.wait()
            units = jnp.right_shift(tail + (TAIL_UNIT - 1), TAIL_UNIT.bit_length() - 1)
            for u in range(1, EXPERT_BLOCK // TAIL_UNIT + 1):
                @pl.when(units == u)
                def _(u=u):
                    rows = u * TAIL_UNIT
                    _to_token_tiles(ytail.at[pl.ds(0, rows * TOKEN_ROWS)], block(xs_slot, rows))
            for pred, cp in tail_copies(start + n_full * EXPERT_BLOCK, tail):
                @pl.when(pred)
                def _(cp=cp):
                    cp.start()

    @pl.when(last_step)
    def _():
        total_full = full0 + n_full
        for back in range(1, Y_SLOTS + 1):
            @pl.when(total_full >= back)
            def _(back=back):
                y_copy(0, jnp.bitwise_and(total_full - back, Y_SLOTS - 1)).wait()
        for pred, cp in tail_copies(0, ltail_ref[0]):
            @pl.when(pred)
            def _(cp=cp):
                cp.wait()


def _max_expert_blocks(n_rows):
    return n_rows // EXPERT_BLOCK + N_EXPERTS


def _expert_metadata(counts, n_rows):
    ids = jnp.arange(N_EXPERTS, dtype=I32)
    used = counts > 0
    starts = jnp.cumsum(counts) - counts
    n_blk = (counts + EXPERT_BLOCK - 1) // EXPERT_BLOCK
    n_full = counts // EXPERT_BLOCK
    tail = counts % EXPERT_BLOCK
    blk0 = jnp.cumsum(n_blk) - n_blk
    full0 = jnp.cumsum(n_full) - n_full
    next_used = lax.cummin(jnp.where(used, ids, N_EXPERTS), reverse=True)
    next_after = jnp.concatenate([next_used[1:], jnp.full((1,), N_EXPERTS, I32)])
    nxt = jnp.where(next_after < N_EXPERTS, next_after, -1)
    ordinal = jnp.cumsum(used.astype(I32)) - 1
    slot = ordinal % W_SLOTS
    first = jnp.logical_and(used, ordinal == 0)
    latest = lax.cummax(jnp.where(tail > 0, ids, -1))
    before = jnp.concatenate([jnp.full((1,), -1, I32), latest[:-1]])
    pick = lambda index, values: jnp.sum(jnp.where(index[:, None] == ids[None, :], values[None, :], 0), axis=1)
    nxt2 = jnp.where(nxt >= 0, pick(nxt, nxt + 1), 0) - 1
    ptail = pick(before, tail)
    ltail = pick(latest[-1:], tail)
    block_ends = jnp.cumsum(n_blk)
    g = jnp.arange(_max_expert_blocks(n_rows), dtype=I32)
    eg = jnp.sum((g[:, None] >= block_ends[None, :]).astype(I32), axis=1)
    blktok = g * EXPERT_BLOCK + pick(eg, starts - blk0 * EXPERT_BLOCK)
    return tuple(v.astype(I32) for v in (starts, counts, nxt, nxt2, slot, first, blk0, full0, ptail, ltail, blktok,
                                         block_ends[-1:]))


def _experts(xs, counts, w_gate, w_up, w_down):
    d = D_MODEL
    meta = _expert_metadata(counts, xs.shape[0] // TOKEN_ROWS - EXPERT_BLOCK)
    hbm = pl.BlockSpec(memory_space=pl.ANY)
    blk = (EXPERT_BLOCK * TOKEN_ROWS, LANES)
    grid_spec = pltpu.PrefetchScalarGridSpec(
        num_scalar_prefetch=len(meta),
        grid=(N_EXPERTS,),
        in_specs=[hbm, hbm, hbm, hbm],
        out_specs=hbm,
        scratch_shapes=[pltpu.VMEM((W_SLOTS, d, EXPERT_FF), F32), pltpu.VMEM((W_SLOTS, d, EXPERT_FF), F32),
                        pltpu.VMEM((W_SLOTS, EXPERT_FF, d), F32),
                        pltpu.VMEM((d, EXPERT_FF), BF16), pltpu.VMEM((d, EXPERT_FF), BF16),
                        pltpu.VMEM((EXPERT_FF, d), BF16),
                        pltpu.VMEM((X_SLOTS,) + blk, WORD), pltpu.VMEM((Y_SLOTS,) + blk, WORD), pltpu.VMEM(blk, WORD),
                        pltpu.SemaphoreType.DMA((W_SLOTS, 3)), pltpu.SemaphoreType.DMA((X_SLOTS,)),
                        pltpu.SemaphoreType.DMA((Y_SLOTS,)), pltpu.SemaphoreType.DMA(())],
    )
    return pl.pallas_call(
        _experts_kernel,
        grid_spec=grid_spec,
        out_shape=jax.ShapeDtypeStruct(xs.shape, WORD),
        compiler_params=_params(("arbitrary",)),
        name="experts",
    )(*meta, xs, w_gate, w_up, w_down)


SC_CORES = 2
SC_SUBCORES = 16
SC_WORKERS = SC_CORES * SC_SUBCORES
SC_CHUNK = 128
SC_RING = 4


def _sc_scatter_rows(planes, idx, n_out_rows):
    n_planes, t, _ = planes.shape
    ranges = SC_WORKERS // n_planes
    n_chunks = t // ranges // SC_CHUNK
    idx = idx.reshape(TOP_K, n_planes, ranges, n_chunks, SC_CHUNK)
    mesh = plsc.VectorSubcoreMesh(core_axis_name="c", subcore_axis_name="s", num_cores=SC_CORES,
                                  num_subcores=SC_SUBCORES)

    def body(planes_hbm, idx_hbm, out_hbm, idx_v, rows_v, lsem, ssem):
        wid = lax.axis_index("s") * SC_CORES + lax.axis_index("c")
        plane = wid % n_planes
        token0 = (wid // n_planes) * (n_chunks * SC_CHUNK)
        for k in range(TOP_K):
            pltpu.sync_copy(idx_hbm.at[k, plane, wid // n_planes], idx_v.at[k])

        def load(slot, c):
            return pltpu.make_async_copy(planes_hbm.at[plane, pl.ds(token0 + c * SC_CHUNK, SC_CHUNK)], rows_v.at[slot],
                                         lsem.at[slot])

        def scatter(slot, c, k):
            return pltpu.make_async_copy(rows_v.at[slot], out_hbm.at[idx_v.at[k, c]], ssem.at[slot])

        @pl.loop(0, n_chunks, step=SC_RING)
        def _(g):
            for slot in range(SC_RING):
                @pl.when(g > 0)
                def _(slot=slot):
                    for k in range(TOP_K):
                        scatter(slot, 0, k).wait()
                load(slot, g + slot).start()
            for slot in range(SC_RING):
                load(slot, g + slot).wait()
                for k in range(TOP_K):
                    scatter(slot, g + slot, k).start()

        for slot in range(SC_RING):
            for k in range(TOP_K):
                scatter(slot, 0, k).wait()

    return pl.kernel(
        body, mesh=mesh,
        out_type=jax.ShapeDtypeStruct((n_out_rows, LANES), planes.dtype),
        scratch_types=[pltpu.VMEM((TOP_K, n_chunks, SC_CHUNK), I32), pltpu.VMEM((SC_RING, SC_CHUNK, LANES), planes.dtype),
                       pltpu.SemaphoreType.DMA((SC_RING,)), pltpu.SemaphoreType.DMA((SC_RING,))],
        name="sc_scatter",
    )(planes, idx)


def _sc_gather_rows(table, idx):
    n_workers, n_chunks, chunk = idx.shape
    rows_per_worker = n_chunks * chunk
    mesh = plsc.VectorSubcoreMesh(core_axis_name="c", subcore_axis_name="s", num_cores=SC_CORES,
                                  num_subcores=SC_SUBCORES)

    def body(table_hbm, idx_hbm, out_hbm, idx_v, rows_v, gsem, wsem):
        wid = lax.axis_index("s") * SC_CORES + lax.axis_index("c")
        base = wid * rows_per_worker
        pltpu.sync_copy(idx_hbm.at[wid], idx_v)

        def write(slot, c):
            return pltpu.make_async_copy(rows_v.at[slot], out_hbm.at[pl.ds(base + c * chunk, chunk)], wsem.at[slot])

        def gather(slot, c):
            return pltpu.make_async_copy(table_hbm.at[idx_v.at[c]], rows_v.at[slot], gsem.at[slot])

        @pl.loop(0, n_chunks, step=SC_RING)
        def _(g):
            for slot in range(SC_RING):
                @pl.when(g > 0)
                def _(slot=slot):
                    write(slot, 0).wait()
                gather(slot, g + slot).start()
            for slot in range(SC_RING):
                gather(slot, g + slot).wait()
                write(slot, g + slot).start()

        for slot in range(SC_RING):
            write(slot, 0).wait()

    return pl.kernel(
        body, mesh=mesh,
        out_type=jax.ShapeDtypeStruct((n_workers * rows_per_worker, LANES), table.dtype),
        scratch_types=[pltpu.VMEM((n_chunks, chunk), I32), pltpu.VMEM((SC_RING, chunk, LANES), table.dtype),
                       pltpu.SemaphoreType.DMA((SC_RING,)), pltpu.SemaphoreType.DMA((SC_RING,))],
        name="sc_gather",
    )(table, idx)


def _combine_kernel(rows_ref, gates_ref, h_ref, x1_ref, gatef_ref, wg_ref, wu_ref, wd_ref, out_ref):
    hb = h_ref[...].astype(BF16)
    shared = _dot((_silu(_dot(hb, wg_ref[...])) * _dot(hb, wu_ref[...])).astype(BF16), wd_ref[...])
    gates = gates_ref[...]

    def expert_rows(k):
        words = [rows_ref[k * TOKEN_ROWS + c] for c in range(TOKEN_ROWS)]
        lows = [lax.bitcast_convert_type(jnp.left_shift(w, 16), F32) for w in words]
        highs = [lax.bitcast_convert_type(jnp.bitwise_and(w, HIGH_HALF), F32) for w in words]
        return jnp.concatenate(lows + highs, axis=1)

    routed = expert_rows(0) * gates[:, 0:1]
    for k in range(1, TOP_K):
        routed = routed + expert_rows(k) * gates[:, k:k + 1]
    out_ref[...] = x1_ref[...] + gatef_ref[...] * (shared + routed)


def _combine(gathered, gates_t, h2, x1, gate_f, w_gate_s, w_up_s, w_down_s, b, s, tm):
    t, d = h2.shape
    tiles_per_seq = s // tm
    row = lambda n: pl.BlockSpec((tm, n), lambda i: (i, 0))
    full = lambda shp: pl.BlockSpec(shp, lambda i: (0,) * len(shp))
    return pl.pallas_call(
        _combine_kernel,
        grid=(t // tm,),
        in_specs=[pl.BlockSpec((TOP_K * TOKEN_ROWS, tm, LANES), lambda i: (0, i, 0)),
                  row(TOP_K), row(d), row(d),
                  pl.BlockSpec((None, 1, d), lambda i: (i // tiles_per_seq, 0, 0)),
                  full((d, EXPERT_FF)), full((d, EXPERT_FF)), full((EXPERT_FF, d))],
        out_specs=row(d),
        out_shape=jax.ShapeDtypeStruct((t, d), F32),
        compiler_params=_params(("arbitrary",)),
        name="combine",
    )(gathered, gates_t, h2, x1, gate_f.reshape(b, 1, d),
      w_gate_s.astype(BF16), w_up_s.astype(BF16), w_down_s.astype(BF16))


def _moe_sublayer(x1, h2, h2t, gate_f, w_router, router_bias, w_gate, w_up, w_down, w_gate_s, w_up_s, w_down_s, b, s,
                  tm=256):
    t = b * s
    eidx, rank, gates, counts = _router(h2, w_router, router_bias)
    pos = _positions(counts, eidx, rank, 4 * tm)
    idx = pos[:, None, :] * TOKEN_ROWS + jnp.arange(TOKEN_ROWS, dtype=I32)[None, :, None]
    xs = _sc_scatter_rows(h2t, idx, (t * TOP_K + EXPERT_BLOCK) * TOKEN_ROWS)
    ys = _experts(xs, counts[:, 0], w_gate, w_up, w_down)
    gathered = _sc_gather_rows(ys, idx.reshape(SC_WORKERS, -1, SC_CHUNK)).reshape(TOP_K * TOKEN_ROWS, t, LANES)
    return _combine(gathered, gates.T, h2, x1, gate_f, w_gate_s, w_up_s, w_down_s, b, s, 2 * tm)


def kernel(x, c, w_ada, b_ada, norm_mix_gain, w_in, q_norm_gain, k_norm_gain, rel_bias_table, conv_w, conv_b, dt_bias,
           a_log, d_skip, ssm_norm_gain, w_out, norm_ffn_gain, w_router, router_bias, w_gate_experts, w_up_experts,
           w_down_experts, w_gate_shared, w_up_shared, w_down_shared):
    b, s, d = x.shape
    for layer in range(w_ada.shape[0]):
        mod = _adaln(c, w_ada[layer], b_ada[layer])
        x1, h2, h2t = _mixer_sublayer(x, mod, norm_mix_gain[layer], w_in[layer], q_norm_gain[layer], k_norm_gain[layer],
                                 rel_bias_table, conv_w[layer], conv_b[layer], dt_bias[layer], a_log[layer],
                                 d_skip[layer], ssm_norm_gain[layer], w_out[layer], norm_ffn_gain[layer])
        gate_f = mod[:, 5 * d:]
        out = _moe_sublayer(x1, h2, h2t, gate_f, w_router[layer], router_bias[layer], w_gate_experts[layer],
                            w_up_experts[layer], w_down_experts[layer], w_gate_shared[layer], w_up_shared[layer],
                            w_down_shared[layer], b, s)
        x = out.reshape(b, s, d)
    return x
```

```python
import functools
import math

import numpy as np
import jax
import jax.numpy as jnp
from jax import lax
from jax.experimental import pallas as pl
from jax.experimental.pallas import tpu as pltpu
from jax.experimental.pallas import tpu_sc as plsc

F32 = jnp.float32
BF16 = jnp.bfloat16
I32 = jnp.int32

D_MODEL = 1024
ATTN_HEADS = 8
HEAD_DIM = 64
ATTN_WIDTH = ATTN_HEADS * HEAD_DIM
PATTERNS = ((128, 1), (512, 4), (2048, 16))
WIN_STEPS = 128
REL_BUCKETS = 32
REL_MAX_DISTANCE = 2048
SSM_HEADS = 24
SSM_HEAD_DIM = 64
SSM_WIDTH = SSM_HEADS * SSM_HEAD_DIM
SSM_GROUPS = 4
HEADS_PER_GROUP = SSM_HEADS // SSM_GROUPS
GROUP_WIDTH = SSM_WIDTH // SSM_GROUPS
SSM_STATE = 128
SSM_CONV = 4
SSM_CHUNK = 128
CONV_CH = SSM_WIDTH + 2 * SSM_GROUPS * SSM_STATE
N_EXPERTS = 256
TOP_K = 8
N_EXPERT_GROUPS = 8
EXPERTS_PER_GROUP = N_EXPERTS // N_EXPERT_GROUPS
TOPK_GROUPS = 4
EXPERT_FF = 256
ROUTED_SCALE = 2.5
NORM_EPS = 1e-6

LANES = 128
SUBLANES = 8
NEG_BIG = -1e30
VMEM_LIMIT = 56 * 1024 * 1024


def _params(sem, vmem=VMEM_LIMIT):
    return pltpu.CompilerParams(dimension_semantics=sem, vmem_limit_bytes=vmem)


def _sigmoid(x):
    return 1.0 / (1.0 + jnp.exp(-x))


def _silu(x):
    return x * _sigmoid(x)


def _split3(x):
    hi = x.astype(BF16)
    r = x - hi.astype(F32)
    mid = r.astype(BF16)
    lo = (r - mid.astype(F32)).astype(BF16)
    return hi, mid, lo


def _dot(a, b):
    return jnp.dot(a, b, preferred_element_type=F32)


def _dot_nt(a, b):
    return lax.dot_general(a, b, (((1,), (1,)), ((), ())), preferred_element_type=F32)


def _dot_exact_rhs(a, b_exact):
    hi, mid, lo = _split3(a)
    return _dot(hi, b_exact) + _dot(mid, b_exact) + _dot(lo, b_exact)


def _dot_exact_lhs(a_exact, b):
    hi, mid, lo = _split3(b)
    return _dot(a_exact, hi) + _dot(a_exact, mid) + _dot(a_exact, lo)


def _adaln_kernel(c_ref, w_ref, b_ref, o_ref):
    s = _silu(c_ref[...]).astype(BF16)
    o_ref[...] = _dot(s, w_ref[...].astype(BF16)) + b_ref[...]


def _adaln(c, w_ada, b_ada):
    b, d = c.shape
    n = w_ada.shape[1]
    rows = SUBLANES
    c_pad = jnp.zeros((rows, d), F32).at[:b].set(c)
    tn = 1024
    out = pl.pallas_call(
        _adaln_kernel,
        grid=(n // tn,),
        in_specs=[pl.BlockSpec((rows, d), lambda j: (0, 0)),
                  pl.BlockSpec((d, tn), lambda j: (0, j)),
                  pl.BlockSpec((1, tn), lambda j: (0, j))],
        out_specs=pl.BlockSpec((rows, tn), lambda j: (0, j)),
        out_shape=jax.ShapeDtypeStruct((rows, n), F32),
        compiler_params=_params(("arbitrary",)),
        name="adaln",
    )(c_pad, w_ada, b_ada.reshape(1, n))
    return out[:b]


def _inproj_kernel(x_ref, shift_ref, scale_ref, g_ref, wqkv_ref, wz_ref, wxbc_ref, wdt_ref,
                   qg_ref, kg_ref, hmean_ref, q_ref, k_ref, v_ref, z_ref, xbc_ref, dt_ref):
    x = x_ref[...]
    ms = jnp.mean(x * x, axis=-1, keepdims=True)
    h = x * lax.rsqrt(ms + NORM_EPS) * g_ref[...]
    h = h * (1.0 + scale_ref[...]) + shift_ref[...]
    hb = h.astype(BF16)

    hmean = hmean_ref[...]

    def head_norm(t, gain):
        sq = t * t
        hi = sq.astype(BF16)
        mid = (sq - hi.astype(F32)).astype(BF16)
        ss = _dot(hi, hmean) + _dot(mid, hmean)
        return t * lax.rsqrt(ss + NORM_EPS) * gain

    q = _dot(hb, wqkv_ref[:, 0:ATTN_WIDTH])
    q_ref[...] = head_norm(q, qg_ref[...]) * (HEAD_DIM ** -0.5)
    k = _dot(hb, wqkv_ref[:, ATTN_WIDTH:2 * ATTN_WIDTH])
    k_ref[...] = head_norm(k, kg_ref[...])
    v_ref[...] = _dot(hb, wqkv_ref[:, 2 * ATTN_WIDTH:3 * ATTN_WIDTH])
    for c0 in range(0, SSM_WIDTH, 512):
        z_ref[:, c0:c0 + 512] = _dot(hb, wz_ref[:, c0:c0 + 512])
    for c0 in range(0, CONV_CH, 512):
        xbc_ref[:, c0:c0 + 512] = _dot(hb, wxbc_ref[:, c0:c0 + 512])
    dt_ref[...] = _dot(hb, wdt_ref[...])


def _in_proj(x, shift, scale, gain, w_in, q_gain, k_gain, tm=256):
    b, s, d = x.shape
    t = b * s
    tiles_per_seq = s // tm
    w = w_in.astype(BF16)
    o_z = 3 * ATTN_WIDTH
    o_x = o_z + SSM_WIDTH
    o_dt = o_x + CONV_CH
    w_qkv, w_z, w_xbc = w[:, :o_z], w[:, o_z:o_x], w[:, o_x:o_dt]
    w_dt = jnp.zeros((d, LANES), BF16).at[:, :SSM_HEADS].set(w[:, o_dt:])
    head_of = np.arange(ATTN_WIDTH) // HEAD_DIM
    hmean = jnp.asarray((head_of[:, None] == head_of[None, :]).astype(np.float32) / HEAD_DIM, BF16)
    full = lambda shp: pl.BlockSpec(shp, lambda i: (0,) * len(shp))
    row = lambda n: pl.BlockSpec((tm, n), lambda i: (i, 0))
    per_batch = pl.BlockSpec((None, 1, d), lambda i: (i // tiles_per_seq, 0, 0))
    outs = pl.pallas_call(
        _inproj_kernel,
        grid=(t // tm,),
        in_specs=[row(d), per_batch, per_batch, full((1, d)),
                  full((d, o_z)), full((d, SSM_WIDTH)), full((d, CONV_CH)), full((d, LANES)),
                  full((1, ATTN_WIDTH)), full((1, ATTN_WIDTH)), full((ATTN_WIDTH, ATTN_WIDTH))],
        out_specs=[row(ATTN_WIDTH), row(ATTN_WIDTH), row(ATTN_WIDTH), row(SSM_WIDTH), row(CONV_CH), row(LANES)],
        out_shape=[jax.ShapeDtypeStruct((t, n), F32)
                   for n in (ATTN_WIDTH, ATTN_WIDTH, ATTN_WIDTH, SSM_WIDTH, CONV_CH, LANES)],
        compiler_params=_params(("arbitrary",)),
        name="in_proj",
    )(x.reshape(t, d), shift.reshape(b, 1, d), scale.reshape(b, 1, d), gain.reshape(1, d),
      w_qkv, w_z, w_xbc, w_dt,
      jnp.tile(q_gain, ATTN_HEADS).reshape(1, ATTN_WIDTH), jnp.tile(k_gain, ATTN_HEADS).reshape(1, ATTN_WIDTH), hmean)
    return outs


def _t5_causal_buckets(distance):
    n = np.maximum(distance, 0)
    max_exact = REL_BUCKETS // 2
    large = max_exact + (np.log(np.maximum(n, 1) / max_exact) / math.log(REL_MAX_DISTANCE / max_exact)
                         * (REL_BUCKETS - max_exact)).astype(np.int64)
    large = np.minimum(large, REL_BUCKETS - 1)
    return np.where(n < max_exact, n, large).astype(np.int32)


def _window_bias(rel_bias_table, dilation):
    qi = np.arange(WIN_STEPS)[:, None]
    kj = np.arange(2 * WIN_STEPS)[None, :]
    dist = qi + WIN_STEPS - kj
    band = (dist >= 0) & (dist <= WIN_STEPS)
    onehot = (_t5_causal_buckets(dist * dilation).reshape(-1, 1) == np.arange(REL_BUCKETS)[None, :]).astype(np.float32)
    bias = jnp.dot(rel_bias_table.astype(F32).T, jnp.asarray(onehot).T, precision=lax.Precision.HIGHEST)
    bias = bias.reshape(ATTN_HEADS, WIN_STEPS, 2 * WIN_STEPS)
    return jnp.where(jnp.asarray(band)[None], bias, NEG_BIG)


ATTN_TOKENS = max(w for w, _ in PATTERNS)
ATTN_UNROLL = 16


def _attn_kernel(q_ref, kp_ref, kc_ref, vp_ref, vc_ref, bias_ref, out_ref, kw, vw, o_acc, l_acc):
    tb = ATTN_TOKENS
    first = pl.program_id(2) == 0
    kw[0:tb] = kp_ref[...]
    kw[tb:2 * tb] = kc_ref[...]
    vw[0:tb] = vp_ref[...]
    vw[tb:2 * tb] = vc_ref[...]
    lane = lax.broadcasted_iota(I32, (WIN_STEPS, LANES), 1)
    head0 = lane < HEAD_DIM
    col = lax.broadcasted_iota(I32, (WIN_STEPS, 2 * WIN_STEPS), 1)
    in_prev = col < WIN_STEPS

    for p, (_, d) in enumerate(PATTERNS):
        shift = d.bit_length() - 1
        n_blocks = tb // WIN_STEPS

        def rows(start, n, d=d):
            return pl.ds(start, n, stride=d) if d > 1 else pl.ds(start, n)

        def body(it, carry, p=p, d=d, shift=shift, rows=rows):
            for u in range(ATTN_UNROLL):
                idx = it * ATTN_UNROLL + u
                r = jnp.bitwise_and(idx, d - 1)
                j = jnp.right_shift(idx, shift)
                qs = j * (WIN_STEPS * d) + r
                q = q_ref[rows(qs, WIN_STEPS), :]
                k = kw[rows(tb + qs - WIN_STEPS * d, 2 * WIN_STEPS), :].astype(BF16)
                v = vw[rows(tb + qs - WIN_STEPS * d, 2 * WIN_STEPS), :].astype(BF16)
                no_prev = jnp.logical_and(in_prev, jnp.logical_and(first, j == 0))
                o_h, lse_h = [], []
                for h in range(2):
                    qh = jnp.where(head0 if h == 0 else jnp.logical_not(head0), q, 0.0).astype(BF16)
                    s = _dot_nt(qh, k) + bias_ref[p, h]
                    s = jnp.where(no_prev, NEG_BIG, s)
                    m = jnp.max(s, axis=-1, keepdims=True)
                    e = jnp.exp(s - m)
                    denom = jnp.sum(e, axis=-1, keepdims=True)
                    o_h.append(_dot(e.astype(BF16), v) / denom)
                    lse_h.append(m + jnp.log(denom))
                o_acc[p, rows(qs, WIN_STEPS), :] = jnp.where(head0, o_h[0], o_h[1])
                l_acc[p, rows(qs, WIN_STEPS), :] = jnp.where(head0, lse_h[0], lse_h[1])
            return carry

        lax.fori_loop(0, n_blocks // ATTN_UNROLL, body, 0)

    chunk = 256
    for c0 in range(0, tb, chunk):
        l1, l2, l3 = (l_acc[p, c0:c0 + chunk, :] for p in range(3))
        m = jnp.maximum(jnp.maximum(l1, l2), l3)
        e1, e2, e3 = jnp.exp(l1 - m), jnp.exp(l2 - m), jnp.exp(l3 - m)
        num = e1 * o_acc[0, c0:c0 + chunk, :] + e2 * o_acc[1, c0:c0 + chunk, :] + e3 * o_acc[2, c0:c0 + chunk, :]
        out_ref[c0:c0 + chunk, :] = num / (e1 + e2 + e3)


def _attention(q, k, v, bias):
    b, s, w = q.shape
    tb = ATTN_TOKENS
    pairs = ATTN_HEADS // 2
    cur = pl.BlockSpec((None, tb, LANES), lambda bi, hp, i: (bi, i, hp))
    prev = pl.BlockSpec((None, tb, LANES), lambda bi, hp, i: (bi, jnp.maximum(i - 1, 0), hp))
    return pl.pallas_call(
        _attn_kernel,
        grid=(b, pairs, s // tb),
        in_specs=[cur, prev, cur, prev, cur,
                  pl.BlockSpec((len(PATTERNS), 2, WIN_STEPS, 2 * WIN_STEPS), lambda bi, hp, i: (0, hp, 0, 0))],
        out_specs=cur,
        out_shape=jax.ShapeDtypeStruct((b, s, w), F32),
        scratch_shapes=[pltpu.VMEM((2 * tb, LANES), F32), pltpu.VMEM((2 * tb, LANES), F32),
                        pltpu.VMEM((len(PATTERNS), tb, LANES), F32), pltpu.VMEM((len(PATTERNS), tb, LANES), F32)],
        compiler_params=_params(("arbitrary",) * 3),
        name="attention",
    )(q, k, k, v, v, bias)


SSD_CHUNKS_PER_STEP = 2


def _ssd_kernel(xbc_ref, halo_ref, z_ref, dtraw_ref, convw_ref, convb_ref, dtb_ref, alog_ref, dskip_ref, gain_ref,
                expand_ref, tril_ref, y_ref, state_ref):
    c = pl.program_id(1)

    @pl.when(c == 0)
    def _():
        state_ref[...] = jnp.zeros_like(state_ref)

    halo = jnp.where(c == 0, 0.0, halo_ref[...])
    for sub in range(SSD_CHUNKS_PER_STEP):
        rows = slice(sub * SSM_CHUNK, (sub + 1) * SSM_CHUNK)
        x = xbc_ref[rows, :]
        w = convw_ref[...]
        acc = x * w[SSM_CONV - 1:SSM_CONV, :] + convb_ref[...]
        row8 = lax.broadcasted_iota(I32, (SUBLANES, CONV_CH), 0)
        for shift in range(1, SSM_CONV):
            xs = pltpu.roll(x, shift, axis=0)
            hs = pltpu.roll(halo, shift, axis=0)
            head = jnp.where(row8 < shift, hs, xs[0:SUBLANES])
            xs = jnp.concatenate([head, xs[SUBLANES:]], axis=0)
            acc = acc + xs * w[SSM_CONV - 1 - shift:SSM_CONV - shift, :]
        act = _silu(acc)
        x_s = act[:, :SSM_WIDTH]
        bc0 = SSM_WIDTH
        cc0 = SSM_WIDTH + SSM_GROUPS * SSM_STATE

        t = dtraw_ref[rows, :] + dtb_ref[...]
        dt = jnp.maximum(t, 0.0) + jnp.log(1.0 + jnp.exp(-jnp.abs(t)))
        a = dt * (-jnp.exp(alog_ref[...]))
        a_cs = _dot_exact_lhs(tril_ref[...], a)
        a_cs_t = a_cs.T
        a_last = a_cs[SSM_CHUNK - 1:SSM_CHUNK, :]
        expand = expand_ref[...]
        dt_e = _dot_exact_rhs(dt, expand)
        ea_e = _dot_exact_rhs(jnp.exp(a_cs), expand)
        dte_e = _dot_exact_rhs(jnp.exp(a_last - a_cs), expand)
        xdt = x_s * dt_e
        xw = (xdt * dte_e).astype(BF16)
        xdt_b = xdt.astype(BF16)

        li = lax.broadcasted_iota(I32, (SSM_CHUNK, SSM_CHUNK), 0)
        si = lax.broadcasted_iota(I32, (SSM_CHUNK, SSM_CHUNK), 1)
        causal = li >= si

        ys = []
        for g in range(SSM_GROUPS):
            gs = slice(g * GROUP_WIDTH, (g + 1) * GROUP_WIDTH)
            b_g = act[:, bc0 + g * SSM_STATE:bc0 + (g + 1) * SSM_STATE]
            c_g = act[:, cc0 + g * SSM_STATE:cc0 + (g + 1) * SSM_STATE].astype(BF16)
            cb = _dot_nt(c_g, b_g.astype(BF16))
            state = state_ref[g]
            y_off = _dot(c_g, state.astype(BF16)) * ea_e[:, gs]
            parts = []
            for j in range(HEADS_PER_GROUP):
                hh = g * HEADS_PER_GROUP + j
                seg = a_cs[:, hh:hh + 1] - a_cs_t[hh:hh + 1, :]
                decay = jnp.exp(jnp.where(causal, seg, NEG_BIG))
                m = (cb * decay).astype(BF16)
                parts.append(_dot(m, xdt_b[:, hh * SSM_HEAD_DIM:(hh + 1) * SSM_HEAD_DIM]))
            ys.append(jnp.concatenate(parts, axis=1) + y_off)
            state_ref[g] = state * ea_e[SSM_CHUNK - 1:SSM_CHUNK, gs] + _dot(b_g.T.astype(BF16), xw[:, gs])
        y = jnp.concatenate(ys, axis=1) + dskip_ref[...] * x_s
        y = y * _silu(z_ref[rows, :])
        gain = gain_ref[...]
        for g in range(SSM_GROUPS):
            gs = slice(g * GROUP_WIDTH, (g + 1) * GROUP_WIDTH)
            yg = y[:, gs]
            ms = jnp.mean(yg * yg, axis=-1, keepdims=True)
            y_ref[rows, gs] = yg * lax.rsqrt(ms + NORM_EPS) * gain[:, gs]
        halo = x[SSM_CHUNK - SUBLANES:, :]


def _ssd(xbc, z, dt_raw, conv_w, conv_b, dt_bias, a_log, d_skip, norm_gain, b, s):
    t = b * s
    step_rows = SSD_CHUNKS_PER_STEP * SSM_CHUNK
    nc = s // step_rows
    pad_heads = lambda v: jnp.zeros((1, LANES), F32).at[0, :SSM_HEADS].set(v)
    head_of_lane = np.arange(SSM_WIDTH) // SSM_HEAD_DIM
    expand = jnp.asarray((np.arange(LANES)[:, None] == head_of_lane[None, :]).astype(np.float32), BF16)
    tril = jnp.asarray(np.tril(np.ones((SSM_CHUNK, SSM_CHUNK), np.float32)), BF16)
    halo_blocks = step_rows // SUBLANES
    chunk = lambda n: pl.BlockSpec((step_rows, n), lambda bi, c: (bi * nc + c, 0))
    full = lambda shp: pl.BlockSpec(shp, lambda bi, c: (0,) * len(shp))
    halo = pl.BlockSpec((SUBLANES, CONV_CH), lambda bi, c: (jnp.maximum((bi * nc + c) * halo_blocks - 1, 0), 0))
    return pl.pallas_call(
        _ssd_kernel,
        grid=(b, nc),
        in_specs=[chunk(CONV_CH), halo, chunk(SSM_WIDTH), chunk(LANES),
                  full((SSM_CONV, CONV_CH)), full((1, CONV_CH)), full((1, LANES)), full((1, LANES)),
                  full((1, SSM_WIDTH)), full((1, SSM_WIDTH)), full((LANES, SSM_WIDTH)), full((SSM_CHUNK, SSM_CHUNK))],
        out_specs=chunk(SSM_WIDTH),
        out_shape=jax.ShapeDtypeStruct((t, SSM_WIDTH), F32),
        scratch_shapes=[pltpu.VMEM((SSM_GROUPS, SSM_STATE, GROUP_WIDTH), F32)],
        compiler_params=_params(("arbitrary", "arbitrary")),
        name="ssd",
    )(xbc, xbc, z, dt_raw, conv_w, conv_b.reshape(1, CONV_CH), pad_heads(dt_bias), pad_heads(a_log),
      jnp.repeat(d_skip, SSM_HEAD_DIM).reshape(1, SSM_WIDTH), norm_gain.reshape(1, SSM_WIDTH), expand, tril)


WORD = jnp.int32
TOKEN_ROWS = D_MODEL // (2 * LANES)
HIGH_HALF = np.int32(-65536)


def _to_token_tiles(ref, x):
    n = x.shape[0]
    for c in range(TOKEN_ROWS):
        lo = lax.bitcast_convert_type(x[:, c * LANES:(c + 1) * LANES].astype(BF16).astype(F32), WORD)
        hi = lax.bitcast_convert_type(x[:, (c + TOKEN_ROWS) * LANES:(c + TOKEN_ROWS + 1) * LANES]
                                      .astype(BF16).astype(F32), WORD)
        ref[pl.ds(c, n, stride=TOKEN_ROWS), :] = jnp.bitwise_or(lax.shift_right_logical(lo, 16),
                                                                 jnp.bitwise_and(hi, HIGH_HALF))


def _from_token_tiles(ref, n, token0=0):
    lows, highs = [], []
    for c in range(TOKEN_ROWS):
        word = ref[pl.ds(token0 * TOKEN_ROWS + c, n, stride=TOKEN_ROWS), :]
        lows.append(lax.bitcast_convert_type(jnp.left_shift(word, 16), F32))
        highs.append(lax.bitcast_convert_type(jnp.bitwise_and(word, HIGH_HALF), F32))
    return jnp.concatenate(lows + highs, axis=1)


def _outproj_kernel(attn_ref, ssm_ref, x_ref, gate_ref, shift_ref, scale_ref, g_ref, wa_ref, ws_ref, x1_ref, h2_ref,
                    h2t_ref):
    mixed = _dot(attn_ref[...].astype(BF16), wa_ref[...]) + _dot(ssm_ref[...].astype(BF16), ws_ref[...])
    x1 = x_ref[...] + gate_ref[...] * mixed
    x1_ref[...] = x1
    ms = jnp.mean(x1 * x1, axis=-1, keepdims=True)
    h = x1 * lax.rsqrt(ms + NORM_EPS) * g_ref[...]
    h2 = h * (1.0 + scale_ref[...]) + shift_ref[...]
    h2_ref[...] = h2
    for c in range(TOKEN_ROWS):
        lo = lax.bitcast_convert_type(h2[:, c * LANES:(c + 1) * LANES].astype(BF16).astype(F32), WORD)
        hi = lax.bitcast_convert_type(h2[:, (c + TOKEN_ROWS) * LANES:(c + TOKEN_ROWS + 1) * LANES]
                                      .astype(BF16).astype(F32), WORD)
        h2t_ref[c] = jnp.bitwise_or(lax.shift_right_logical(lo, 16), jnp.bitwise_and(hi, HIGH_HALF))


def _out_proj(attn, ssm, x, gate, shift, scale, gain, w_out, b, s, tm=512):
    t = b * s
    d = D_MODEL
    tiles_per_seq = s // tm
    w = w_out.astype(BF16)
    row = lambda n: pl.BlockSpec((tm, n), lambda i: (i, 0))
    full = lambda shp: pl.BlockSpec(shp, lambda i: (0,) * len(shp))
    per_batch = pl.BlockSpec((None, 1, d), lambda i: (i // tiles_per_seq, 0, 0))
    return pl.pallas_call(
        _outproj_kernel,
        grid=(t // tm,),
        in_specs=[row(ATTN_WIDTH), row(SSM_WIDTH), row(d), per_batch, per_batch, per_batch, full((1, d)),
                  full((ATTN_WIDTH, d)), full((SSM_WIDTH, d))],
        out_specs=[row(d), row(d), pl.BlockSpec((TOKEN_ROWS, tm, LANES), lambda i: (0, i, 0))],
        out_shape=[jax.ShapeDtypeStruct((t, d), F32)] * 2 + [jax.ShapeDtypeStruct((TOKEN_ROWS, t, LANES), WORD)],
        compiler_params=_params(("arbitrary",)),
        name="out_proj",
    )(attn.reshape(t, ATTN_WIDTH), ssm, x.reshape(t, d),
      gate.reshape(b, 1, d), shift.reshape(b, 1, d), scale.reshape(b, 1, d), gain.reshape(1, d),
      w[:ATTN_WIDTH], w[ATTN_WIDTH:])


def _mixer_sublayer(x, mod, norm_mix_gain, w_in, q_norm_gain, k_norm_gain, rel_bias_table, conv_w, conv_b, dt_bias,
                    a_log, d_skip, ssm_norm_gain, w_out, norm_ffn_gain):
    b, s, d = x.shape
    shift_m, scale_m, gate_m, shift_f, scale_f, _ = jnp.split(mod, 6, axis=-1)
    q, k, v, z, xbc, dt_raw = _in_proj(x, shift_m, scale_m, norm_mix_gain, w_in, q_norm_gain, k_norm_gain)
    bias = jnp.stack([_window_bias(rel_bias_table, dilation) for _, dilation in PATTERNS])
    attn = _attention(q.reshape(b, s, ATTN_WIDTH), k.reshape(b, s, ATTN_WIDTH), v.reshape(b, s, ATTN_WIDTH), bias)
    ssm = _ssd(xbc, z, dt_raw, conv_w, conv_b, dt_bias, a_log, d_skip, ssm_norm_gain, b, s)
    return _out_proj(attn, ssm, x, gate_m, shift_f, scale_f, norm_ffn_gain, w_out, b, s)


def _first_argmax(v, iota, limit):
    m = jnp.max(v, axis=0, keepdims=True)
    idx = jnp.min(jnp.where(v == m, iota, limit), axis=0, keepdims=True)
    return m, idx


def _router_kernel(h_ref, wt_ref, bias_ref, upper_ref, eidx_ref, rank_ref, gate_ref, counts_ref, carry_ref):
    @pl.when(pl.program_id(0) == 0)
    def _():
        carry_ref[...] = jnp.zeros_like(carry_ref)

    tm = h_ref.shape[0]
    h = h_ref[...]
    wt = wt_ref[...]
    h_hi = h.astype(BF16)
    h_lo = (h - h_hi.astype(F32)).astype(BF16)
    w_hi = wt.astype(BF16)
    w_lo = (wt - w_hi.astype(F32)).astype(BF16)
    logits = _dot_nt(w_hi, h_hi) + _dot_nt(w_hi, h_lo) + _dot_nt(w_lo, h_hi)
    scores = _sigmoid(logits)
    choice = scores + bias_ref[...]
    neg_inf = -jnp.inf

    iota_g = lax.broadcasted_iota(I32, (EXPERTS_PER_GROUP, tm), 0).astype(F32)
    group_rows = []
    for g in range(N_EXPERT_GROUPS):
        v = choice[g * EXPERTS_PER_GROUP:(g + 1) * EXPERTS_PER_GROUP]
        m1, i1 = _first_argmax(v, iota_g, float(EXPERTS_PER_GROUP))
        m2 = jnp.max(jnp.where(iota_g == i1, neg_inf, v), axis=0, keepdims=True)
        group_rows.append(m1 + m2)
    group_scores = jnp.concatenate(group_rows, axis=0)

    iota_n = lax.broadcasted_iota(I32, (N_EXPERT_GROUPS, tm), 0).astype(F32)
    chosen = jnp.zeros((N_EXPERT_GROUPS, tm), F32)
    for _ in range(TOPK_GROUPS):
        _, gi = _first_argmax(group_scores, iota_n, float(N_EXPERT_GROUPS))
        hit = iota_n == gi
        chosen = jnp.where(hit, 1.0, chosen)
        group_scores = jnp.where(hit, neg_inf, group_scores)

    masked = jnp.concatenate(
        [jnp.where(chosen[g:g + 1] > 0.0, choice[g * EXPERTS_PER_GROUP:(g + 1) * EXPERTS_PER_GROUP], neg_inf)
         for g in range(N_EXPERT_GROUPS)], axis=0)

    iota_e = lax.broadcasted_iota(I32, (N_EXPERTS, tm), 0).astype(F32)
    picked, gates = [], []
    onehot = jnp.zeros((N_EXPERTS, tm), F32)
    for _ in range(TOP_K):
        _, ei = _first_argmax(masked, iota_e, float(N_EXPERTS))
        hit = iota_e == ei
        gates.append(jnp.sum(jnp.where(hit, scores, 0.0), axis=0, keepdims=True))
        masked = jnp.where(hit, neg_inf, masked)
        onehot = jnp.where(hit, 1.0, onehot)
        picked.append(ei)
    gate_sum = gates[0]
    for gk in gates[1:]:
        gate_sum = gate_sum + gk

    base = _dot(onehot.astype(BF16), upper_ref[...]) + carry_ref[...]
    ranks = [jnp.sum(jnp.where(iota_e == ei, base, 0.0), axis=0, keepdims=True) for ei in picked]
    carry_ref[...] = carry_ref[...] + jnp.sum(onehot, axis=1, keepdims=True)

    eidx_ref[...] = jnp.concatenate(picked, axis=0).astype(I32)
    rank_ref[...] = jnp.concatenate(ranks, axis=0).astype(I32)
    gate_ref[...] = jnp.concatenate([gk / gate_sum * ROUTED_SCALE for gk in gates], axis=0)
    counts_ref[...] = carry_ref[...].astype(I32)


def _router(h2, w_router, router_bias, tm=256):
    t, d = h2.shape
    upper = jnp.asarray(np.triu(np.ones((tm, tm), np.float32), 1), BF16)
    tok = pl.BlockSpec((TOP_K, tm), lambda i: (0, i))
    full = lambda shp: pl.BlockSpec(shp, lambda i: (0,) * len(shp))
    return pl.pallas_call(
        _router_kernel,
        grid=(t // tm,),
        in_specs=[pl.BlockSpec((tm, d), lambda i: (i, 0)), full((N_EXPERTS, d)), full((N_EXPERTS, 1)), full((tm, tm))],
        out_specs=[tok, tok, tok, full((N_EXPERTS, 1))],
        out_shape=[jax.ShapeDtypeStruct((TOP_K, t), I32), jax.ShapeDtypeStruct((TOP_K, t), I32),
                   jax.ShapeDtypeStruct((TOP_K, t), F32), jax.ShapeDtypeStruct((N_EXPERTS, 1), I32)],
        scratch_shapes=[pltpu.VMEM((N_EXPERTS, 1), F32)],
        compiler_params=_params(("arbitrary",)),
        name="router",
    )(h2, w_router.T, router_bias.reshape(N_EXPERTS, 1), upper)


def _positions_kernel(counts_ref, lower_ref, eidx_ref, rank_ref, pos_ref):
    tm = eidx_ref.shape[1]
    counts = jnp.broadcast_to(counts_ref[...].astype(F32), (N_EXPERTS, LANES))
    offsets = _dot_exact_lhs(lower_ref[...], counts)[:, 0:1]
    iota_e = lax.broadcasted_iota(I32, (N_EXPERTS, tm), 0).astype(F32)
    e = eidx_ref[...].astype(F32)
    rows = [jnp.sum(jnp.where(iota_e == e[k:k + 1], offsets, 0.0), axis=0, keepdims=True) for k in range(TOP_K)]
    pos_ref[...] = jnp.concatenate(rows, axis=0).astype(I32) + rank_ref[...]


def _positions(counts, eidx, rank, tm):
    t = eidx.shape[1]
    lower = jnp.asarray(np.tril(np.ones((N_EXPERTS, N_EXPERTS), np.float32), -1), BF16)
    tok = pl.BlockSpec((TOP_K, tm), lambda i: (0, i))
    return pl.pallas_call(
        _positions_kernel,
        grid=(t // tm,),
        in_specs=[pl.BlockSpec((N_EXPERTS, 1), lambda i: (0, 0)), pl.BlockSpec((N_EXPERTS, N_EXPERTS), lambda i: (0, 0)),
                  tok, tok],
        out_specs=tok,
        out_shape=jax.ShapeDtypeStruct((TOP_K, t), I32),
        compiler_params=_params(("arbitrary",)),
        name="positions",
    )(counts, lower, eidx, rank)


EXPERT_BLOCK = 512
TAIL_UNIT = 128
TAIL_PIECES = tuple(1 << i for i in reversed(range(EXPERT_BLOCK.bit_length() - 1)))
W_SLOTS = 3
X_SLOTS = 4
Y_SLOTS = 4


def _experts_kernel(start_ref, count_ref, nxt_ref, nxt2_ref, slot_ref, first_ref, blk0_ref, full0_ref, ptail_ref,
                    ltail_ref, blktok_ref, nblocks_ref, xs_hbm, wg_hbm, wu_hbm, wd_hbm, ys_hbm,
                    wg_buf, wu_buf, wd_buf, wg_bf, wu_bf, wd_bf, xbuf, ybuf, ytail, wsem, xsem, ysem, tsem):
    e = pl.program_id(0)
    last_step = e == pl.num_programs(0) - 1
    start, count = start_ref[e], count_ref[e]
    n_full = jnp.right_shift(count, EXPERT_BLOCK.bit_length() - 1)
    tail = jnp.bitwise_and(count, EXPERT_BLOCK - 1)
    n_blk = n_full + (tail > 0).astype(I32)
    blk0, full0 = blk0_ref[e], full0_ref[e]
    slot, nxt, nxt2 = slot_ref[e], nxt_ref[e], nxt2_ref[e]
    slot1 = jnp.where(slot + 1 >= W_SLOTS, slot + 1 - W_SLOTS, slot + 1)
    slot2 = jnp.where(slot + 2 >= W_SLOTS, slot + 2 - W_SLOTS, slot + 2)

    def token_rows(token, n):
        return pl.ds(pl.multiple_of(token * TOKEN_ROWS, TOKEN_ROWS), n * TOKEN_ROWS)

    def fetch(ex, s):
        return (pltpu.make_async_copy(wg_hbm.at[ex], wg_buf.at[s], wsem.at[s, 0]),
                pltpu.make_async_copy(wu_hbm.at[ex], wu_buf.at[s], wsem.at[s, 1]),
                pltpu.make_async_copy(wd_hbm.at[ex], wd_buf.at[s], wsem.at[s, 2]))

    def x_copy(token, s):
        return pltpu.make_async_copy(xs_hbm.at[token_rows(token, EXPERT_BLOCK)], xbuf.at[s], xsem.at[s])

    def y_copy(token, s):
        return pltpu.make_async_copy(ybuf.at[s], ys_hbm.at[token_rows(token, EXPERT_BLOCK)], ysem.at[s])

    def tail_copies(token, length):
        out = []
        for piece in TAIL_PIECES:
            bigger = (EXPERT_BLOCK - 1) & ~(2 * piece - 1)
            done = jnp.bitwise_and(length, bigger)
            cp = pltpu.make_async_copy(ytail.at[token_rows(done, piece)], ys_hbm.at[token_rows(token + done, piece)],
                                       tsem)
            out.append((jnp.bitwise_and(length, piece) != 0, cp))
        return out

    def block(s, rows=EXPERT_BLOCK):
        x = _from_token_tiles(xbuf.at[s], rows).astype(BF16)
        g = _dot(x, wg_bf[...])
        u = _dot(x, wu_bf[...])
        return _dot((_silu(g) * u).astype(BF16), wd_bf[...])

    @pl.when(count > 0)
    def _():
        @pl.when(first_ref[e] == 1)
        def _():
            for g in range(X_SLOTS - 1):
                @pl.when(g < nblocks_ref[0])
                def _(g=g):
                    x_copy(blktok_ref[g], g).start()
            for cp in fetch(e, slot):
                cp.start()

            @pl.when(nxt >= 0)
            def _():
                for cp in fetch(nxt, slot1):
                    cp.start()

        for cp in fetch(e, slot):
            cp.wait()

        @pl.when(nxt2 >= 0)
        def _():
            for cp in fetch(nxt2, slot2):
                cp.start()

        wg_bf[...] = wg_buf[slot].astype(BF16)
        wu_bf[...] = wu_buf[slot].astype(BF16)
        wd_bf[...] = wd_buf[slot].astype(BF16)

        def take_x(i):
            g = blk0 + i
            xs_slot = jnp.bitwise_and(g, X_SLOTS - 1)
            x_copy(start, xs_slot).wait()
            ahead = g + (X_SLOTS - 1)

            @pl.when(ahead < nblocks_ref[0])
            def _():
                x_copy(blktok_ref[ahead], jnp.bitwise_and(ahead, X_SLOTS - 1)).start()

            return xs_slot

        def full_block(i, carry):
            y = block(take_x(i))
            j = full0 + i
            ys_slot = jnp.bitwise_and(j, Y_SLOTS - 1)

            @pl.when(j >= Y_SLOTS)
            def _():
                y_copy(start, ys_slot).wait()

            _to_token_tiles(ybuf.at[ys_slot], y)
            y_copy(start + i * EXPERT_BLOCK, ys_slot).start()
            return carry

        lax.fori_loop(0, n_full, full_block, 0)

        @pl.when(tail > 0)
        def _():
            xs_slot = take_x(n_full)
            for pred, cp in tail_copies(start, ptail_ref[e]):
                @pl.when(pred)
                def _(cp=cp):
                    cp.wait()
            units = jnp.right_shift(tail + (TAIL_UNIT - 1), TAIL_UNIT.bit_length() - 1)
            for u in range(1, EXPERT_BLOCK // TAIL_UNIT + 1):
                @pl.when(units == u)
                def _(u=u):
                    rows = u * TAIL_UNIT
                    _to_token_tiles(ytail.at[pl.ds(0, rows * TOKEN_ROWS)], block(xs_slot, rows))
            for pred, cp in tail_copies(start + n_full * EXPERT_BLOCK, tail):
                @pl.when(pred)
                def _(cp=cp):
                    cp.start()

    @pl.when(last_step)
    def _():
        total_full = full0 + n_full
        for back in range(1, Y_SLOTS + 1):
            @pl.when(total_full >= back)
            def _(back=back):
                y_copy(0, jnp.bitwise_and(total_full - back, Y_SLOTS - 1)).wait()
        for pred, cp in tail_copies(0, ltail_ref[0]):
            @pl.when(pred)
            def _(cp=cp):
                cp.wait()


def _max_expert_blocks(n_rows):
    return n_rows // EXPERT_BLOCK + N_EXPERTS


def _expert_metadata(counts, n_rows):
    ids = jnp.arange(N_EXPERTS, dtype=I32)
    used = counts > 0
    starts = jnp.cumsum(counts) - counts
    n_blk = (counts + EXPERT_BLOCK - 1) // EXPERT_BLOCK
    n_full = counts // EXPERT_BLOCK
    tail = counts % EXPERT_BLOCK
    blk0 = jnp.cumsum(n_blk) - n_blk
    full0 = jnp.cumsum(n_full) - n_full
    next_used = lax.cummin(jnp.where(used, ids, N_EXPERTS), reverse=True)
    next_after = jnp.concatenate([next_used[1:], jnp.full((1,), N_EXPERTS, I32)])
    nxt = jnp.where(next_after < N_EXPERTS, next_after, -1)
    ordinal = jnp.cumsum(used.astype(I32)) - 1
    slot = ordinal % W_SLOTS
    first = jnp.logical_and(used, ordinal == 0)
    latest = lax.cummax(jnp.where(tail > 0, ids, -1))
    before = jnp.concatenate([jnp.full((1,), -1, I32), latest[:-1]])
    pick = lambda index, values: jnp.sum(jnp.where(index[:, None] == ids[None, :], values[None, :], 0), axis=1)
    nxt2 = jnp.where(nxt >= 0, pick(nxt, nxt + 1), 0) - 1
    ptail = pick(before, tail)
    ltail = pick(latest[-1:], tail)
    block_ends = jnp.cumsum(n_blk)
    g = jnp.arange(_max_expert_blocks(n_rows), dtype=I32)
    eg = jnp.sum((g[:, None] >= block_ends[None, :]).astype(I32), axis=1)
    blktok = g * EXPERT_BLOCK + pick(eg, starts - blk0 * EXPERT_BLOCK)
    return tuple(v.astype(I32) for v in (starts, counts, nxt, nxt2, slot, first, blk0, full0, ptail, ltail, blktok,
                                         block_ends[-1:]))


def _experts(xs, counts, w_gate, w_up, w_down):
    d = D_MODEL
    meta = _expert_metadata(counts, xs.shape[0] // TOKEN_ROWS - EXPERT_BLOCK)
    hbm = pl.BlockSpec(memory_space=pl.ANY)
    blk = (EXPERT_BLOCK * TOKEN_ROWS, LANES)
    grid_spec = pltpu.PrefetchScalarGridSpec(
        num_scalar_prefetch=len(meta),
        grid=(N_EXPERTS,),
        in_specs=[hbm, hbm, hbm, hbm],
        out_specs=hbm,
        scratch_shapes=[pltpu.VMEM((W_SLOTS, d, EXPERT_FF), F32), pltpu.VMEM((W_SLOTS, d, EXPERT_FF), F32),
                        pltpu.VMEM((W_SLOTS, EXPERT_FF, d), F32),
                        pltpu.VMEM((d, EXPERT_FF), BF16), pltpu.VMEM((d, EXPERT_FF), BF16),
                        pltpu.VMEM((EXPERT_FF, d), BF16),
                        pltpu.VMEM((X_SLOTS,) + blk, WORD), pltpu.VMEM((Y_SLOTS,) + blk, WORD), pltpu.VMEM(blk, WORD),
                        pltpu.SemaphoreType.DMA((W_SLOTS, 3)), pltpu.SemaphoreType.DMA((X_SLOTS,)),
                        pltpu.SemaphoreType.DMA((Y_SLOTS,)), pltpu.SemaphoreType.DMA(())],
    )
    return pl.pallas_call(
        _experts_kernel,
        grid_spec=grid_spec,
        out_shape=jax.ShapeDtypeStruct(xs.shape, WORD),
        compiler_params=_params(("arbitrary",)),
        name="experts",
    )(*meta, xs, w_gate, w_up, w_down)


SC_CORES = 2
SC_SUBCORES = 16
SC_WORKERS = SC_CORES * SC_SUBCORES
SC_CHUNK = 128
SC_RING = 4


def _sc_scatter_rows(planes, idx, n_out_rows):
    n_planes, t, _ = planes.shape
    ranges = SC_WORKERS // n_planes
    n_chunks = t // ranges // SC_CHUNK
    idx = idx.reshape(TOP_K, n_planes, ranges, n_chunks, SC_CHUNK)
    mesh = plsc.VectorSubcoreMesh(core_axis_name="c", subcore_axis_name="s", num_cores=SC_CORES,
                                  num_subcores=SC_SUBCORES)

    def body(planes_hbm, idx_hbm, out_hbm, idx_v, rows_v, lsem, ssem):
        wid = lax.axis_index("s") * SC_CORES + lax.axis_index("c")
        plane = wid % n_planes
        token0 = (wid // n_planes) * (n_chunks * SC_CHUNK)
        for k in range(TOP_K):
            pltpu.sync_copy(idx_hbm.at[k, plane, wid // n_planes], idx_v.at[k])

        def load(slot, c):
            return pltpu.make_async_copy(planes_hbm.at[plane, pl.ds(token0 + c * SC_CHUNK, SC_CHUNK)], rows_v.at[slot],
                                         lsem.at[slot])

        def scatter(slot, c, k):
            return pltpu.make_async_copy(rows_v.at[slot], out_hbm.at[idx_v.at[k, c]], ssem.at[slot])

        @pl.loop(0, n_chunks, step=SC_RING)
        def _(g):
            for slot in range(SC_RING):
                @pl.when(g > 0)
                def _(slot=slot):
                    for k in range(TOP_K):
                        scatter(slot, 0, k).wait()
                load(slot, g + slot).start()
            for slot in range(SC_RING):
                load(slot, g + slot).wait()
                for k in range(TOP_K):
                    scatter(slot, g + slot, k).start()

        for slot in range(SC_RING):
            for k in range(TOP_K):
                scatter(slot, 0, k).wait()

    return pl.kernel(
        body, mesh=mesh,
        out_type=jax.ShapeDtypeStruct((n_out_rows, LANES), planes.dtype),
        scratch_types=[pltpu.VMEM((TOP_K, n_chunks, SC_CHUNK), I32), pltpu.VMEM((SC_RING, SC_CHUNK, LANES), planes.dtype),
                       pltpu.SemaphoreType.DMA((SC_RING,)), pltpu.SemaphoreType.DMA((SC_RING,))],
        name="sc_scatter",
    )(planes, idx)


def _sc_gather_rows(table, idx):
    n_workers, n_chunks, chunk = idx.shape
    rows_per_worker = n_chunks * chunk
    mesh = plsc.VectorSubcoreMesh(core_axis_name="c", subcore_axis_name="s", num_cores=SC_CORES,
                                  num_subcores=SC_SUBCORES)

    def body(table_hbm, idx_hbm, out_hbm, idx_v, rows_v, gsem, wsem):
        wid = lax.axis_index("s") * SC_CORES + lax.axis_index("c")
        base = wid * rows_per_worker
        pltpu.sync_copy(idx_hbm.at[wid], idx_v)

        def write(slot, c):
            return pltpu.make_async_copy(rows_v.at[slot], out_hbm.at[pl.ds(base + c * chunk, chunk)], wsem.at[slot])

        def gather(slot, c):
            return pltpu.make_async_copy(table_hbm.at[idx_v.at[c]], rows_v.at[slot], gsem.at[slot])

        @pl.loop(0, n_chunks, step=SC_RING)
        def _(g):
            for slot in range(SC_RING):
                @pl.when(g > 0)
                def _(slot=slot):
                    write(slot, 0).wait()
                gather(slot, g + slot).start()
            for slot in range(SC_RING):
                gather(slot, g + slot).wait()
                write(slot, g + slot).start()

        for slot in range(SC_RING):
            write(slot, 0).wait()

    return pl.kernel(
        body, mesh=mesh,
        out_type=jax.ShapeDtypeStruct((n_workers * rows_per_worker, LANES), table.dtype),
        scratch_types=[pltpu.VMEM((n_chunks, chunk), I32), pltpu.VMEM((SC_RING, chunk, LANES), table.dtype),
                       pltpu.SemaphoreType.DMA((SC_RING,)), pltpu.SemaphoreType.DMA((SC_RING,))],
        name="sc_gather",
    )(table, idx)


def _combine_kernel(rows_ref, gates_ref, h_ref, x1_ref, gatef_ref, wg_ref, wu_ref, wd_ref, out_ref):
    hb = h_ref[...].astype(BF16)
    shared = _dot((_silu(_dot(hb, wg_ref[...])) * _dot(hb, wu_ref[...])).astype(BF16), wd_ref[...])
    gates = gates_ref[...]

    def expert_rows(k):
        words = [rows_ref[k * TOKEN_ROWS + c] for c in range(TOKEN_ROWS)]
        lows = [lax.bitcast_convert_type(jnp.left_shift(w, 16), F32) for w in words]
        highs = [lax.bitcast_convert_type(jnp.bitwise_and(w, HIGH_HALF), F32) for w in words]
        return jnp.concatenate(lows + highs, axis=1)

    routed = expert_rows(0) * gates[:, 0:1]
    for k in range(1, TOP_K):
        routed = routed + expert_rows(k) * gates[:, k:k + 1]
    out_ref[...] = x1_ref[...] + gatef_ref[...] * (shared + routed)


def _combine(gathered, gates_t, h2, x1, gate_f, w_gate_s, w_up_s, w_down_s, b, s, tm):
    t, d = h2.shape
    tiles_per_seq = s // tm
    row = lambda n: pl.BlockSpec((tm, n), lambda i: (i, 0))
    full = lambda shp: pl.BlockSpec(shp, lambda i: (0,) * len(shp))
    return pl.pallas_call(
        _combine_kernel,
        grid=(t // tm,),
        in_specs=[pl.BlockSpec((TOP_K * TOKEN_ROWS, tm, LANES), lambda i: (0, i, 0)),
                  row(TOP_K), row(d), row(d),
                  pl.BlockSpec((None, 1, d), lambda i: (i // tiles_per_seq, 0, 0)),
                  full((d, EXPERT_FF)), full((d, EXPERT_FF)), full((EXPERT_FF, d))],
        out_specs=row(d),
        out_shape=jax.ShapeDtypeStruct((t, d), F32),
        compiler_params=_params(("arbitrary",)),
        name="combine",
    )(gathered, gates_t, h2, x1, gate_f.reshape(b, 1, d),
      w_gate_s.astype(BF16), w_up_s.astype(BF16), w_down_s.astype(BF16))


def _moe_sublayer(x1, h2, h2t, gate_f, w_router, router_bias, w_gate, w_up, w_down, w_gate_s, w_up_s, w_down_s, b, s,
                  tm=256):
    t = b * s
    eidx, rank, gates, counts = _router(h2, w_router, router_bias)
    pos = _positions(counts, eidx, rank, 4 * tm)
    idx = pos[:, None, :] * TOKEN_ROWS + jnp.arange(TOKEN_ROWS, dtype=I32)[None, :, None]
    xs = _sc_scatter_rows(h2t, idx, (t * TOP_K + EXPERT_BLOCK) * TOKEN_ROWS)
    ys = _experts(xs, counts[:, 0], w_gate, w_up, w_down)
    gathered = _sc_gather_rows(ys, idx.reshape(SC_WORKERS, -1, SC_CHUNK)).reshape(TOP_K * TOKEN_ROWS, t, LANES)
    return _combine(gathered, gates.T, h2, x1, gate_f, w_gate_s, w_up_s, w_down_s, b, s, 2 * tm)


def kernel(x, c, w_ada, b_ada, norm_mix_gain, w_in, q_norm_gain, k_norm_gain, rel_bias_table, conv_w, conv_b, dt_bias,
           a_log, d_skip, ssm_norm_gain, w_out, norm_ffn_gain, w_router, router_bias, w_gate_experts, w_up_experts,
           w_down_experts, w_gate_shared, w_up_shared, w_down_shared):
    b, s, d = x.shape
    for layer in range(w_ada.shape[0]):
        mod = _adaln(c, w_ada[layer], b_ada[layer])
        x1, h2, h2t = _mixer_sublayer(x, mod, norm_mix_gain[layer], w_in[layer], q_norm_gain[layer], k_norm_gain[layer],
                                 rel_bias_table, conv_w[layer], conv_b[layer], dt_bias[layer], a_log[layer],
                                 d_skip[layer], ssm_norm_gain[layer], w_out[layer], norm_ffn_gain[layer])
        gate_f = mod[:, 5 * d:]
        out = _moe_sublayer(x1, h2, h2t, gate_f, w_router[layer], router_bias[layer], w_gate_experts[layer],
                            w_up_experts[layer], w_down_experts[layer], w_gate_shared[layer], w_up_shared[layer],
                            w_down_shared[layer], b, s)
        x = out.reshape(b, s, d)
    return x
```

```python
import math

import numpy as np
import jax
import jax.numpy as jnp
from jax import lax
from jax.experimental import pallas as pl
from jax.experimental.pallas import tpu as pltpu
from jax.experimental.pallas import tpu_sc as plsc

F32 = jnp.float32
BF16 = jnp.bfloat16
I32 = jnp.int32

D_MODEL = 1024
ATTN_HEADS = 8
HEAD_DIM = 64
ATTN_WIDTH = ATTN_HEADS * HEAD_DIM
PATTERNS = ((128, 1), (512, 4), (2048, 16))
WIN_STEPS = 128
REL_BUCKETS = 32
REL_MAX_DISTANCE = 2048
SSM_HEADS = 24
SSM_HEAD_DIM = 64
SSM_WIDTH = SSM_HEADS * SSM_HEAD_DIM
SSM_GROUPS = 4
HEADS_PER_GROUP = SSM_HEADS // SSM_GROUPS
GROUP_WIDTH = SSM_WIDTH // SSM_GROUPS
SSM_STATE = 128
SSM_CONV = 4
SSM_CHUNK = 128
CONV_CH = SSM_WIDTH + 2 * SSM_GROUPS * SSM_STATE
N_EXPERTS = 256
TOP_K = 8
N_EXPERT_GROUPS = 8
EXPERTS_PER_GROUP = N_EXPERTS // N_EXPERT_GROUPS
TOPK_GROUPS = 4
EXPERT_FF = 256
ROUTED_SCALE = 2.5
NORM_EPS = 1e-6

LANES = 128
SUBLANES = 8
NEG_BIG = -1e30
VMEM_LIMIT = 56 * 1024 * 1024


def _params(sem, vmem=VMEM_LIMIT):
    return pltpu.CompilerParams(dimension_semantics=sem, vmem_limit_bytes=vmem)


def _sigmoid(x):
    return 1.0 / (1.0 + jnp.exp(-x))


def _silu(x):
    return x * _sigmoid(x)


def _split3(x):
    hi = x.astype(BF16)
    r = x - hi.astype(F32)
    mid = r.astype(BF16)
    lo = (r - mid.astype(F32)).astype(BF16)
    return hi, mid, lo


def _dot(a, b):
    return jnp.dot(a, b, preferred_element_type=F32)


def _dot_nt(a, b):
    return lax.dot_general(a, b, (((1,), (1,)), ((), ())), preferred_element_type=F32)


def _dot_exact_rhs(a, b_exact):
    hi, mid, lo = _split3(a)
    return _dot(hi, b_exact) + _dot(mid, b_exact) + _dot(lo, b_exact)


def _dot_exact_lhs(a_exact, b):
    hi, mid, lo = _split3(b)
    return _dot(a_exact, hi) + _dot(a_exact, mid) + _dot(a_exact, lo)


def _adaln_kernel(c_ref, w_ref, b_ref, o_ref):
    s = _silu(c_ref[...]).astype(BF16)
    o_ref[...] = _dot(s, w_ref[...].astype(BF16)) + b_ref[...]


def _adaln(c, w_ada, b_ada):
    b, d = c.shape
    n = w_ada.shape[1]
    rows = SUBLANES
    c_pad = jnp.zeros((rows, d), F32).at[:b].set(c)
    tn = 1024
    out = pl.pallas_call(
        _adaln_kernel,
        grid=(n // tn,),
        in_specs=[pl.BlockSpec((rows, d), lambda j: (0, 0)),
                  pl.BlockSpec((d, tn), lambda j: (0, j)),
                  pl.BlockSpec((1, tn), lambda j: (0, j))],
        out_specs=pl.BlockSpec((rows, tn), lambda j: (0, j)),
        out_shape=jax.ShapeDtypeStruct((rows, n), F32),
        compiler_params=_params(("arbitrary",)),
        name="adaln",
    )(c_pad, w_ada, b_ada.reshape(1, n))
    return out[:b]


def _inproj_kernel(x_ref, shift_ref, scale_ref, g_ref, wqkv_ref, wz_ref, wxbc_ref, wdt_ref,
                   qg_ref, kg_ref, hmean_ref, q_ref, k_ref, v_ref, z_ref, xbc_ref, dt_ref):
    x = x_ref[...]
    ms = jnp.mean(x * x, axis=-1, keepdims=True)
    h = x * lax.rsqrt(ms + NORM_EPS) * g_ref[...]
    h = h * (1.0 + scale_ref[...]) + shift_ref[...]
    hb = h.astype(BF16)

    hmean = hmean_ref[...]

    def head_norm(t, gain):
        sq = t * t
        hi = sq.astype(BF16)
        mid = (sq - hi.astype(F32)).astype(BF16)
        ss = _dot(hi, hmean) + _dot(mid, hmean)
        return t * lax.rsqrt(ss + NORM_EPS) * gain

    q = _dot(hb, wqkv_ref[:, 0:ATTN_WIDTH])
    q_ref[...] = head_norm(q, qg_ref[...]) * (HEAD_DIM ** -0.5)
    k = _dot(hb, wqkv_ref[:, ATTN_WIDTH:2 * ATTN_WIDTH])
    k_ref[...] = head_norm(k, kg_ref[...])
    v_ref[...] = _dot(hb, wqkv_ref[:, 2 * ATTN_WIDTH:3 * ATTN_WIDTH])
    for c0 in range(0, SSM_WIDTH, 512):
        z_ref[:, c0:c0 + 512] = _dot(hb, wz_ref[:, c0:c0 + 512])
    for c0 in range(0, CONV_CH, 512):
        xbc_ref[:, c0:c0 + 512] = _dot(hb, wxbc_ref[:, c0:c0 + 512])
    dt_ref[...] = _dot(hb, wdt_ref[...])


def _in_proj(x, shift, scale, gain, w_in, q_gain, k_gain, tm=256):
    b, s, d = x.shape
    t = b * s
    tiles_per_seq = s // tm
    w = w_in.astype(BF16)
    o_z = 3 * ATTN_WIDTH
    o_x = o_z + SSM_WIDTH
    o_dt = o_x + CONV_CH
    w_qkv, w_z, w_xbc = w[:, :o_z], w[:, o_z:o_x], w[:, o_x:o_dt]
    w_dt = jnp.zeros((d, LANES), BF16).at[:, :SSM_HEADS].set(w[:, o_dt:])
    head_of = np.arange(ATTN_WIDTH) // HEAD_DIM
    hmean = jnp.asarray((head_of[:, None] == head_of[None, :]).astype(np.float32) / HEAD_DIM, BF16)
    full = lambda shp: pl.BlockSpec(shp, lambda i: (0,) * len(shp))
    row = lambda n: pl.BlockSpec((tm, n), lambda i: (i, 0))
    per_batch = pl.BlockSpec((None, 1, d), lambda i: (i // tiles_per_seq, 0, 0))
    outs = pl.pallas_call(
        _inproj_kernel,
        grid=(t // tm,),
        in_specs=[row(d), per_batch, per_batch, full((1, d)),
                  full((d, o_z)), full((d, SSM_WIDTH)), full((d, CONV_CH)), full((d, LANES)),
                  full((1, ATTN_WIDTH)), full((1, ATTN_WIDTH)), full((ATTN_WIDTH, ATTN_WIDTH))],
        out_specs=[row(ATTN_WIDTH), row(ATTN_WIDTH), row(ATTN_WIDTH), row(SSM_WIDTH), row(CONV_CH), row(LANES)],
        out_shape=[jax.ShapeDtypeStruct((t, n), F32)
                   for n in (ATTN_WIDTH, ATTN_WIDTH, ATTN_WIDTH, SSM_WIDTH, CONV_CH, LANES)],
        compiler_params=_params(("arbitrary",)),
        name="in_proj",
    )(x.reshape(t, d), shift.reshape(b, 1, d), scale.reshape(b, 1, d), gain.reshape(1, d),
      w_qkv, w_z, w_xbc, w_dt,
      jnp.tile(q_gain, ATTN_HEADS).reshape(1, ATTN_WIDTH), jnp.tile(k_gain, ATTN_HEADS).reshape(1, ATTN_WIDTH), hmean)
    return outs


def _t5_causal_buckets(distance):
    n = np.maximum(distance, 0)
    max_exact = REL_BUCKETS // 2
    large = max_exact + (np.log(np.maximum(n, 1) / max_exact) / math.log(REL_MAX_DISTANCE / max_exact)
                         * (REL_BUCKETS - max_exact)).astype(np.int64)
    large = np.minimum(large, REL_BUCKETS - 1)
    return np.where(n < max_exact, n, large).astype(np.int32)


def _window_bias(rel_bias_table, dilation):
    qi = np.arange(WIN_STEPS)[:, None]
    kj = np.arange(2 * WIN_STEPS)[None, :]
    dist = qi + WIN_STEPS - kj
    band = (dist >= 0) & (dist <= WIN_STEPS)
    onehot = (_t5_causal_buckets(dist * dilation).reshape(-1, 1) == np.arange(REL_BUCKETS)[None, :]).astype(np.float32)
    bias = jnp.dot(rel_bias_table.astype(F32).T, jnp.asarray(onehot).T, precision=lax.Precision.HIGHEST)
    bias = bias.reshape(ATTN_HEADS, WIN_STEPS, 2 * WIN_STEPS)
    return jnp.where(jnp.asarray(band)[None], bias, NEG_BIG)


ATTN_TOKENS = max(w for w, _ in PATTERNS)
ATTN_UNROLL = 16


def _attn_kernel(q_ref, kp_ref, kc_ref, vp_ref, vc_ref, bias_ref, out_ref, kw, vw, o_acc, l_acc):
    tb = ATTN_TOKENS
    first = pl.program_id(2) == 0
    kw[0:tb] = kp_ref[...]
    kw[tb:2 * tb] = kc_ref[...]
    vw[0:tb] = vp_ref[...]
    vw[tb:2 * tb] = vc_ref[...]
    lane = lax.broadcasted_iota(I32, (WIN_STEPS, LANES), 1)
    head0 = lane < HEAD_DIM
    col = lax.broadcasted_iota(I32, (WIN_STEPS, 2 * WIN_STEPS), 1)
    in_prev = col < WIN_STEPS

    for p, (_, d) in enumerate(PATTERNS):
        shift = d.bit_length() - 1
        n_blocks = tb // WIN_STEPS

        def rows(start, n, d=d):
            return pl.ds(start, n, stride=d) if d > 1 else pl.ds(start, n)

        def body(it, carry, p=p, d=d, shift=shift, rows=rows):
            for u in range(ATTN_UNROLL):
                idx = it * ATTN_UNROLL + u
                r = jnp.bitwise_and(idx, d - 1)
                j = jnp.right_shift(idx, shift)
                qs = j * (WIN_STEPS * d) + r
                q = q_ref[rows(qs, WIN_STEPS), :]
                k = kw[rows(tb + qs - WIN_STEPS * d, 2 * WIN_STEPS), :].astype(BF16)
                v = vw[rows(tb + qs - WIN_STEPS * d, 2 * WIN_STEPS), :].astype(BF16)
                no_prev = jnp.logical_and(in_prev, jnp.logical_and(first, j == 0))
                o_h, lse_h = [], []
                for h in range(2):
                    qh = jnp.where(head0 if h == 0 else jnp.logical_not(head0), q, 0.0).astype(BF16)
                    s = _dot_nt(qh, k) + bias_ref[p, h]
                    s = jnp.where(no_prev, NEG_BIG, s)
                    m = jnp.max(s, axis=-1, keepdims=True)
                    e = jnp.exp(s - m)
                    denom = jnp.sum(e, axis=-1, keepdims=True)
                    o_h.append(_dot(e.astype(BF16), v) / denom)
                    lse_h.append(m + jnp.log(denom))
                o_acc[p, rows(qs, WIN_STEPS), :] = jnp.where(head0, o_h[0], o_h[1])
                l_acc[p, rows(qs, WIN_STEPS), :] = jnp.where(head0, lse_h[0], lse_h[1])
            return carry

        lax.fori_loop(0, n_blocks // ATTN_UNROLL, body, 0)

    chunk = 256
    for c0 in range(0, tb, chunk):
        l1, l2, l3 = (l_acc[p, c0:c0 + chunk, :] for p in range(3))
        m = jnp.maximum(jnp.maximum(l1, l2), l3)
        e1, e2, e3 = jnp.exp(l1 - m), jnp.exp(l2 - m), jnp.exp(l3 - m)
        num = e1 * o_acc[0, c0:c0 + chunk, :] + e2 * o_acc[1, c0:c0 + chunk, :] + e3 * o_acc[2, c0:c0 + chunk, :]
        out_ref[c0:c0 + chunk, :] = num / (e1 + e2 + e3)


def _attention(q, k, v, bias):
    b, s, w = q.shape
    tb = ATTN_TOKENS
    pairs = ATTN_HEADS // 2
    cur = pl.BlockSpec((None, tb, LANES), lambda bi, hp, i: (bi, i, hp))
    prev = pl.BlockSpec((None, tb, LANES), lambda bi, hp, i: (bi, jnp.maximum(i - 1, 0), hp))
    return pl.pallas_call(
        _attn_kernel,
        grid=(b, pairs, s // tb),
        in_specs=[cur, prev, cur, prev, cur,
                  pl.BlockSpec((len(PATTERNS), 2, WIN_STEPS, 2 * WIN_STEPS), lambda bi, hp, i: (0, hp, 0, 0))],
        out_specs=cur,
        out_shape=jax.ShapeDtypeStruct((b, s, w), F32),
        scratch_shapes=[pltpu.VMEM((2 * tb, LANES), F32), pltpu.VMEM((2 * tb, LANES), F32),
                        pltpu.VMEM((len(PATTERNS), tb, LANES), F32), pltpu.VMEM((len(PATTERNS), tb, LANES), F32)],
        compiler_params=_params(("arbitrary",) * 3),
        name="attention",
    )(q, k, k, v, v, bias)


SSD_CHUNKS_PER_STEP = 2


def _ssd_kernel(xbc_ref, halo_ref, z_ref, dtraw_ref, convw_ref, convb_ref, dtb_ref, alog_ref, dskip_ref, gain_ref,
                expand_ref, tril_ref, y_ref, state_ref):
    c = pl.program_id(1)

    @pl.when(c == 0)
    def _():
        state_ref[...] = jnp.zeros_like(state_ref)

    halo = jnp.where(c == 0, 0.0, halo_ref[...])
    for sub in range(SSD_CHUNKS_PER_STEP):
        rows = slice(sub * SSM_CHUNK, (sub + 1) * SSM_CHUNK)
        x = xbc_ref[rows, :]
        w = convw_ref[...]
        acc = x * w[SSM_CONV - 1:SSM_CONV, :] + convb_ref[...]
        row8 = lax.broadcasted_iota(I32, (SUBLANES, CONV_CH), 0)
        for shift in range(1, SSM_CONV):
            xs = pltpu.roll(x, shift, axis=0)
            hs = pltpu.roll(halo, shift, axis=0)
            head = jnp.where(row8 < shift, hs, xs[0:SUBLANES])
            xs = jnp.concatenate([head, xs[SUBLANES:]], axis=0)
            acc = acc + xs * w[SSM_CONV - 1 - shift:SSM_CONV - shift, :]
        act = _silu(acc)
        x_s = act[:, :SSM_WIDTH]
        bc0 = SSM_WIDTH
        cc0 = SSM_WIDTH + SSM_GROUPS * SSM_STATE

        t = dtraw_ref[rows, :] + dtb_ref[...]
        dt = jnp.maximum(t, 0.0) + jnp.log(1.0 + jnp.exp(-jnp.abs(t)))
        a = dt * (-jnp.exp(alog_ref[...]))
        a_cs = _dot_exact_lhs(tril_ref[...], a)
        a_cs_t = a_cs.T
        a_last = a_cs[SSM_CHUNK - 1:SSM_CHUNK, :]
        expand = expand_ref[...]
        dt_e = _dot_exact_rhs(dt, expand)
        ea_e = _dot_exact_rhs(jnp.exp(a_cs), expand)
        dte_e = _dot_exact_rhs(jnp.exp(a_last - a_cs), expand)
        xdt = x_s * dt_e
        xw = (xdt * dte_e).astype(BF16)
        xdt_b = xdt.astype(BF16)

        li = lax.broadcasted_iota(I32, (SSM_CHUNK, SSM_CHUNK), 0)
        si = lax.broadcasted_iota(I32, (SSM_CHUNK, SSM_CHUNK), 1)
        causal = li >= si

        ys = []
        for g in range(SSM_GROUPS):
            gs = slice(g * GROUP_WIDTH, (g + 1) * GROUP_WIDTH)
            b_g = act[:, bc0 + g * SSM_STATE:bc0 + (g + 1) * SSM_STATE]
            c_g = act[:, cc0 + g * SSM_STATE:cc0 + (g + 1) * SSM_STATE].astype(BF16)
            cb = _dot_nt(c_g, b_g.astype(BF16))
            state = state_ref[g]
            y_off = _dot(c_g, state.astype(BF16)) * ea_e[:, gs]
            parts = []
            for j in range(HEADS_PER_GROUP):
                hh = g * HEADS_PER_GROUP + j
                seg = a_cs[:, hh:hh + 1] - a_cs_t[hh:hh + 1, :]
                decay = jnp.exp(jnp.where(causal, seg, NEG_BIG))
                m = (cb * decay).astype(BF16)
                parts.append(_dot(m, xdt_b[:, hh * SSM_HEAD_DIM:(hh + 1) * SSM_HEAD_DIM]))
            ys.append(jnp.concatenate(parts, axis=1) + y_off)
            state_ref[g] = state * ea_e[SSM_CHUNK - 1:SSM_CHUNK, gs] + _dot(b_g.T.astype(BF16), xw[:, gs])
        y = jnp.concatenate(ys, axis=1) + dskip_ref[...] * x_s
        y = y * _silu(z_ref[rows, :])
        gain = gain_ref[...]
        for g in range(SSM_GROUPS):
            gs = slice(g * GROUP_WIDTH, (g + 1) * GROUP_WIDTH)
            yg = y[:, gs]
            ms = jnp.mean(yg * yg, axis=-1, keepdims=True)
            y_ref[rows, gs] = yg * lax.rsqrt(ms + NORM_EPS) * gain[:, gs]
        halo = x[SSM_CHUNK - SUBLANES:, :]


def _ssd(xbc, z, dt_raw, conv_w, conv_b, dt_bias, a_log, d_skip, norm_gain, b, s):
    t = b * s
    step_rows = SSD_CHUNKS_PER_STEP * SSM_CHUNK
    nc = s // step_rows
    pad_heads = lambda v: jnp.zeros((1, LANES), F32).at[0, :SSM_HEADS].set(v)
    head_of_lane = np.arange(SSM_WIDTH) // SSM_HEAD_DIM
    expand = jnp.asarray((np.arange(LANES)[:, None] == head_of_lane[None, :]).astype(np.float32), BF16)
    tril = jnp.asarray(np.tril(np.ones((SSM_CHUNK, SSM_CHUNK), np.float32)), BF16)
    halo_blocks = step_rows // SUBLANES
    chunk = lambda n: pl.BlockSpec((step_rows, n), lambda bi, c: (bi * nc + c, 0))
    full = lambda shp: pl.BlockSpec(shp, lambda bi, c: (0,) * len(shp))
    halo = pl.BlockSpec((SUBLANES, CONV_CH), lambda bi, c: (jnp.maximum((bi * nc + c) * halo_blocks - 1, 0), 0))
    return pl.pallas_call(
        _ssd_kernel,
        grid=(b, nc),
        in_specs=[chunk(CONV_CH), halo, chunk(SSM_WIDTH), chunk(LANES),
                  full((SSM_CONV, CONV_CH)), full((1, CONV_CH)), full((1, LANES)), full((1, LANES)),
                  full((1, SSM_WIDTH)), full((1, SSM_WIDTH)), full((LANES, SSM_WIDTH)), full((SSM_CHUNK, SSM_CHUNK))],
        out_specs=chunk(SSM_WIDTH),
        out_shape=jax.ShapeDtypeStruct((t, SSM_WIDTH), F32),
        scratch_shapes=[pltpu.VMEM((SSM_GROUPS, SSM_STATE, GROUP_WIDTH), F32)],
        compiler_params=_params(("arbitrary", "arbitrary")),
        name="ssd",
    )(xbc, xbc, z, dt_raw, conv_w, conv_b.reshape(1, CONV_CH), pad_heads(dt_bias), pad_heads(a_log),
      jnp.repeat(d_skip, SSM_HEAD_DIM).reshape(1, SSM_WIDTH), norm_gain.reshape(1, SSM_WIDTH), expand, tril)


WORD = jnp.int32
TOKEN_ROWS = D_MODEL // (2 * LANES)
HIGH_HALF = np.int32(-65536)


def _to_token_tiles(ref, x):
    n = x.shape[0]
    for c in range(TOKEN_ROWS):
        lo = lax.bitcast_convert_type(x[:, c * LANES:(c + 1) * LANES].astype(BF16).astype(F32), WORD)
        hi = lax.bitcast_convert_type(x[:, (c + TOKEN_ROWS) * LANES:(c + TOKEN_ROWS + 1) * LANES]
                                      .astype(BF16).astype(F32), WORD)
        ref[pl.ds(c, n, stride=TOKEN_ROWS), :] = jnp.bitwise_or(lax.shift_right_logical(lo, 16),
                                                                 jnp.bitwise_and(hi, HIGH_HALF))


def _from_token_tiles(ref, n, token0=0):
    lows, highs = [], []
    for c in range(TOKEN_ROWS):
        word = ref[pl.ds(token0 * TOKEN_ROWS + c, n, stride=TOKEN_ROWS), :]
        lows.append(lax.bitcast_convert_type(jnp.left_shift(word, 16), F32))
        highs.append(lax.bitcast_convert_type(jnp.bitwise_and(word, HIGH_HALF), F32))
    return jnp.concatenate(lows + highs, axis=1)


def _outproj_kernel(attn_ref, ssm_ref, x_ref, gate_ref, shift_ref, scale_ref, g_ref, wa_ref, ws_ref, x1_ref, h2_ref,
                    h2t_ref):
    mixed = _dot(attn_ref[...].astype(BF16), wa_ref[...]) + _dot(ssm_ref[...].astype(BF16), ws_ref[...])
    x1 = x_ref[...] + gate_ref[...] * mixed
    x1_ref[...] = x1
    ms = jnp.mean(x1 * x1, axis=-1, keepdims=True)
    h = x1 * lax.rsqrt(ms + NORM_EPS) * g_ref[...]
    h2 = h * (1.0 + scale_ref[...]) + shift_ref[...]
    h2_ref[...] = h2
    for c in range(TOKEN_ROWS):
        lo = lax.bitcast_convert_type(h2[:, c * LANES:(c + 1) * LANES].astype(BF16).astype(F32), WORD)
        hi = lax.bitcast_convert_type(h2[:, (c + TOKEN_ROWS) * LANES:(c + TOKEN_ROWS + 1) * LANES]
                                      .astype(BF16).astype(F32), WORD)
        h2t_ref[c] = jnp.bitwise_or(lax.shift_right_logical(lo, 16), jnp.bitwise_and(hi, HIGH_HALF))


def _out_proj(attn, ssm, x, gate, shift, scale, gain, w_out, b, s, tm=512):
    t = b * s
    d = D_MODEL
    tiles_per_seq = s // tm
    w = w_out.astype(BF16)
    row = lambda n: pl.BlockSpec((tm, n), lambda i: (i, 0))
    full = lambda shp: pl.BlockSpec(shp, lambda i: (0,) * len(shp))
    per_batch = pl.BlockSpec((None, 1, d), lambda i: (i // tiles_per_seq, 0, 0))
    return pl.pallas_call(
        _outproj_kernel,
        grid=(t // tm,),
        in_specs=[row(ATTN_WIDTH), row(SSM_WIDTH), row(d), per_batch, per_batch, per_batch, full((1, d)),
                  full((ATTN_WIDTH, d)), full((SSM_WIDTH, d))],
        out_specs=[row(d), row(d), pl.BlockSpec((TOKEN_ROWS, tm, LANES), lambda i: (0, i, 0))],
        out_shape=[jax.ShapeDtypeStruct((t, d), F32)] * 2 + [jax.ShapeDtypeStruct((TOKEN_ROWS, t, LANES), WORD)],
        compiler_params=_params(("arbitrary",)),
        name="out_proj",
    )(attn.reshape(t, ATTN_WIDTH), ssm, x.reshape(t, d),
      gate.reshape(b, 1, d), shift.reshape(b, 1, d), scale.reshape(b, 1, d), gain.reshape(1, d),
      w[:ATTN_WIDTH], w[ATTN_WIDTH:])


def _mixer_sublayer(x, mod, norm_mix_gain, w_in, q_norm_gain, k_norm_gain, rel_bias_table, conv_w, conv_b, dt_bias,
                    a_log, d_skip, ssm_norm_gain, w_out, norm_ffn_gain):
    b, s, d = x.shape
    shift_m, scale_m, gate_m, shift_f, scale_f, _ = jnp.split(mod, 6, axis=-1)
    q, k, v, z, xbc, dt_raw = _in_proj(x, shift_m, scale_m, norm_mix_gain, w_in, q_norm_gain, k_norm_gain)
    bias = jnp.stack([_window_bias(rel_bias_table, dilation) for _, dilation in PATTERNS])
    attn = _attention(q.reshape(b, s, ATTN_WIDTH), k.reshape(b, s, ATTN_WIDTH), v.reshape(b, s, ATTN_WIDTH), bias)
    ssm = _ssd(xbc, z, dt_raw, conv_w, conv_b, dt_bias, a_log, d_skip, ssm_norm_gain, b, s)
    return _out_proj(attn, ssm, x, gate_m, shift_f, scale_f, norm_ffn_gain, w_out, b, s)


def _first_argmax(v, iota, limit):
    m = jnp.max(v, axis=0, keepdims=True)
    idx = jnp.min(jnp.where(v == m, iota, limit), axis=0, keepdims=True)
    return m, idx


def _router_kernel(h_ref, wt_ref, bias_ref, upper_ref, eidx_ref, rank_ref, gate_ref, counts_ref, carry_ref):
    @pl.when(pl.program_id(0) == 0)
    def _():
        carry_ref[...] = jnp.zeros_like(carry_ref)

    tm = h_ref.shape[0]
    h = h_ref[...]
    wt = wt_ref[...]
    h_hi = h.astype(BF16)
    h_lo = (h - h_hi.astype(F32)).astype(BF16)
    w_hi = wt.astype(BF16)
    w_lo = (wt - w_hi.astype(F32)).astype(BF16)
    logits = _dot_nt(w_hi, h_hi) + _dot_nt(w_hi, h_lo) + _dot_nt(w_lo, h_hi)
    scores = _sigmoid(logits)
    choice = scores + bias_ref[...]
    neg_inf = -jnp.inf

    iota_g = lax.broadcasted_iota(I32, (EXPERTS_PER_GROUP, tm), 0).astype(F32)
    group_rows = []
    for g in range(N_EXPERT_GROUPS):
        v = choice[g * EXPERTS_PER_GROUP:(g + 1) * EXPERTS_PER_GROUP]
        m1, i1 = _first_argmax(v, iota_g, float(EXPERTS_PER_GROUP))
        m2 = jnp.max(jnp.where(iota_g == i1, neg_inf, v), axis=0, keepdims=True)
        group_rows.append(m1 + m2)
    group_scores = jnp.concatenate(group_rows, axis=0)

    iota_n = lax.broadcasted_iota(I32, (N_EXPERT_GROUPS, tm), 0).astype(F32)
    chosen = jnp.zeros((N_EXPERT_GROUPS, tm), F32)
    for _ in range(TOPK_GROUPS):
        _, gi = _first_argmax(group_scores, iota_n, float(N_EXPERT_GROUPS))
        hit = iota_n == gi
        chosen = jnp.where(hit, 1.0, chosen)
        group_scores = jnp.where(hit, neg_inf, group_scores)

    masked = jnp.concatenate(
        [jnp.where(chosen[g:g + 1] > 0.0, choice[g * EXPERTS_PER_GROUP:(g + 1) * EXPERTS_PER_GROUP], neg_inf)
         for g in range(N_EXPERT_GROUPS)], axis=0)

    iota_e = lax.broadcasted_iota(I32, (N_EXPERTS, tm), 0).astype(F32)
    picked, gates = [], []
    onehot = jnp.zeros((N_EXPERTS, tm), F32)
    for _ in range(TOP_K):
        _, ei = _first_argmax(masked, iota_e, float(N_EXPERTS))
        hit = iota_e == ei
        gates.append(jnp.sum(jnp.where(hit, scores, 0.0), axis=0, keepdims=True))
        masked = jnp.where(hit, neg_inf, masked)
        onehot = jnp.where(hit, 1.0, onehot)
        picked.append(ei)
    gate_sum = gates[0]
    for gk in gates[1:]:
        gate_sum = gate_sum + gk

    base = _dot(onehot.astype(BF16), upper_ref[...]) + carry_ref[...]
    ranks = [jnp.sum(jnp.where(iota_e == ei, base, 0.0), axis=0, keepdims=True) for ei in picked]
    carry_ref[...] = carry_ref[...] + jnp.sum(onehot, axis=1, keepdims=True)

    eidx_ref[...] = jnp.concatenate(picked, axis=0).astype(I32)
    rank_ref[...] = jnp.concatenate(ranks, axis=0).astype(I32)
    gate_ref[...] = jnp.concatenate([gk / gate_sum * ROUTED_SCALE for gk in gates], axis=0)
    counts_ref[...] = carry_ref[...].astype(I32)


def _router(h2, w_router, router_bias, tm=256):
    t, d = h2.shape
    upper = jnp.asarray(np.triu(np.ones((tm, tm), np.float32), 1), BF16)
    tok = pl.BlockSpec((TOP_K, tm), lambda i: (0, i))
    full = lambda shp: pl.BlockSpec(shp, lambda i: (0,) * len(shp))
    return pl.pallas_call(
        _router_kernel,
        grid=(t // tm,),
        in_specs=[pl.BlockSpec((tm, d), lambda i: (i, 0)), full((N_EXPERTS, d)), full((N_EXPERTS, 1)), full((tm, tm))],
        out_specs=[tok, tok, tok, full((N_EXPERTS, 1))],
        out_shape=[jax.ShapeDtypeStruct((TOP_K, t), I32), jax.ShapeDtypeStruct((TOP_K, t), I32),
                   jax.ShapeDtypeStruct((TOP_K, t), F32), jax.ShapeDtypeStruct((N_EXPERTS, 1), I32)],
        scratch_shapes=[pltpu.VMEM((N_EXPERTS, 1), F32)],
        compiler_params=_params(("arbitrary",)),
        name="router",
    )(h2, w_router.T, router_bias.reshape(N_EXPERTS, 1), upper)


def _positions_kernel(counts_ref, lower_ref, eidx_ref, rank_ref, pos_ref):
    tm = eidx_ref.shape[1]
    counts = jnp.broadcast_to(counts_ref[...].astype(F32), (N_EXPERTS, LANES))
    offsets = _dot_exact_lhs(lower_ref[...], counts)[:, 0:1]
    iota_e = lax.broadcasted_iota(I32, (N_EXPERTS, tm), 0).astype(F32)
    e = eidx_ref[...].astype(F32)
    rows = [jnp.sum(jnp.where(iota_e == e[k:k + 1], offsets, 0.0), axis=0, keepdims=True) for k in range(TOP_K)]
    pos_ref[...] = jnp.concatenate(rows, axis=0).astype(I32) + rank_ref[...]


def _positions(counts, eidx, rank, tm):
    t = eidx.shape[1]
    lower = jnp.asarray(np.tril(np.ones((N_EXPERTS, N_EXPERTS), np.float32), -1), BF16)
    tok = pl.BlockSpec((TOP_K, tm), lambda i: (0, i))
    return pl.pallas_call(
        _positions_kernel,
        grid=(t // tm,),
        in_specs=[pl.BlockSpec((N_EXPERTS, 1), lambda i: (0, 0)), pl.BlockSpec((N_EXPERTS, N_EXPERTS), lambda i: (0, 0)),
                  tok, tok],
        out_specs=tok,
        out_shape=jax.ShapeDtypeStruct((TOP_K, t), I32),
        compiler_params=_params(("arbitrary",)),
        name="positions",
    )(counts, lower, eidx, rank)


EXPERT_BLOCK = 512
TAIL_UNIT = 128
TAIL_PIECES = tuple(1 << i for i in reversed(range(EXPERT_BLOCK.bit_length() - 1)))
W_SLOTS = 3
X_SLOTS = 4
Y_SLOTS = 4


def _experts_kernel(start_ref, count_ref, nxt_ref, nxt2_ref, slot_ref, first_ref, blk0_ref, full0_ref, ptail_ref,
                    ltail_ref, blktok_ref, nblocks_ref, xs_hbm, wg_hbm, wu_hbm, wd_hbm, ys_hbm,
                    wg_buf, wu_buf, wd_buf, wg_bf, wu_bf, wd_bf, xbuf, ybuf, ytail, wsem, xsem, ysem, tsem):
    e = pl.program_id(0)
    last_step = e == pl.num_programs(0) - 1
    start, count = start_ref[e], count_ref[e]
    n_full = jnp.right_shift(count, EXPERT_BLOCK.bit_length() - 1)
    tail = jnp.bitwise_and(count, EXPERT_BLOCK - 1)
    n_blk = n_full + (tail > 0).astype(I32)
    blk0, full0 = blk0_ref[e], full0_ref[e]
    slot, nxt, nxt2 = slot_ref[e], nxt_ref[e], nxt2_ref[e]
    slot1 = jnp.where(slot + 1 >= W_SLOTS, slot + 1 - W_SLOTS, slot + 1)
    slot2 = jnp.where(slot + 2 >= W_SLOTS, slot + 2 - W_SLOTS, slot + 2)

    def token_rows(token, n):
        return pl.ds(pl.multiple_of(token * TOKEN_ROWS, TOKEN_ROWS), n * TOKEN_ROWS)

    def fetch(ex, s):
        return (pltpu.make_async_copy(wg_hbm.at[ex], wg_buf.at[s], wsem.at[s, 0]),
                pltpu.make_async_copy(wu_hbm.at[ex], wu_buf.at[s], wsem.at[s, 1]),
                pltpu.make_async_copy(wd_hbm.at[ex], wd_buf.at[s], wsem.at[s, 2]))

    def x_copy(token, s):
        return pltpu.make_async_copy(xs_hbm.at[token_rows(token, EXPERT_BLOCK)], xbuf.at[s], xsem.at[s])

    def y_copy(token, s):
        return pltpu.make_async_copy(ybuf.at[s], ys_hbm.at[token_rows(token, EXPERT_BLOCK)], ysem.at[s])

    def tail_copies(token, length):
        out = []
        for piece in TAIL_PIECES:
            bigger = (EXPERT_BLOCK - 1) & ~(2 * piece - 1)
            done = jnp.bitwise_and(length, bigger)
            cp = pltpu.make_async_copy(ytail.at[token_rows(done, piece)], ys_hbm.at[token_rows(token + done, piece)],
                                       tsem)
            out.append((jnp.bitwise_and(length, piece) != 0, cp))
        return out

    def block(s, rows=EXPERT_BLOCK):
        x = _from_token_tiles(xbuf.at[s], rows).astype(BF16)
        g = _dot(x, wg_bf[...])
        u = _dot(x, wu_bf[...])
        return _dot((_silu(g) * u).astype(BF16), wd_bf[...])

    @pl.when(count > 0)
    def _():
        @pl.when(first_ref[e] == 1)
        def _():
            for g in range(X_SLOTS - 1):
                @pl.when(g < nblocks_ref[0])
                def _(g=g):
                    x_copy(blktok_ref[g], g).start()
            for cp in fetch(e, slot):
                cp.start()

            @pl.when(nxt >= 0)
            def _():
                for cp in fetch(nxt, slot1):
                    cp.start()

        for cp in fetch(e, slot):
            cp.wait()

        @pl.when(nxt2 >= 0)
        def _():
            for cp in fetch(nxt2, slot2):
                cp.start()

        wg_bf[...] = wg_buf[slot].astype(BF16)
        wu_bf[...] = wu_buf[slot].astype(BF16)
        wd_bf[...] = wd_buf[slot].astype(BF16)

        def take_x(i):
            g = blk0 + i
            xs_slot = jnp.bitwise_and(g, X_SLOTS - 1)
            x_copy(start, xs_slot).wait()
            ahead = g + (X_SLOTS - 1)

            @pl.when(ahead < nblocks_ref[0])
            def _():
                x_copy(blktok_ref[ahead], jnp.bitwise_and(ahead, X_SLOTS - 1)).start()

            return xs_slot

        def full_block(i, carry):
            y = block(take_x(i))
            j = full0 + i
            ys_slot = jnp.bitwise_and(j, Y_SLOTS - 1)

            @pl.when(j >= Y_SLOTS)
            def _():
                y_copy(start, ys_slot).wait()

            _to_token_tiles(ybuf.at[ys_slot], y)
            y_copy(start + i * EXPERT_BLOCK, ys_slot).start()
            return carry

        lax.fori_loop(0, n_full, full_block, 0)

        @pl.when(tail > 0)
        def _():
            xs_slot = take_x(n_full)
            for pred, cp in tail_copies(start, ptail_ref[e]):
                @pl.when(pred)
                def _(cp=cp):
                    cp.wait()
            units = jnp.right_shift(tail + (TAIL_UNIT - 1), TAIL_UNIT.bit_length() - 1)
            for u in range(1, EXPERT_BLOCK // TAIL_UNIT + 1):
                @pl.when(units == u)
                def _(u=u):
                    rows = u * TAIL_UNIT
                    _to_token_tiles(ytail.at[pl.ds(0, rows * TOKEN_ROWS)], block(xs_slot, rows))
            for pred, cp in tail_copies(start + n_full * EXPERT_BLOCK, tail):
                @pl.when(pred)
                def _(cp=cp):
                    cp.start()

    @pl.when(last_step)
    def _():
        total_full = full0 + n_full
        for back in range(1, Y_SLOTS + 1):
            @pl.when(total_full >= back)
            def _(back=back):
                y_copy(0, jnp.bitwise_and(total_full - back, Y_SLOTS - 1)).wait()
        for pred, cp in tail_copies(0, ltail_ref[0]):
            @pl.when(pred)
            def _(cp=cp):
                cp.wait()


def _max_expert_blocks(n_rows):
    return n_rows // EXPERT_BLOCK + N_EXPERTS


def _expert_metadata(counts, n_rows):
    ids = jnp.arange(N_EXPERTS, dtype=I32)
    used = counts > 0
    starts = jnp.cumsum(counts) - counts
    n_blk = (counts + EXPERT_BLOCK - 1) // EXPERT_BLOCK
    n_full = counts // EXPERT_BLOCK
    tail = counts % EXPERT_BLOCK
    blk0 = jnp.cumsum(n_blk) - n_blk
    full0 = jnp.cumsum(n_full) - n_full
    next_used = lax.cummin(jnp.where(used, ids, N_EXPERTS), reverse=True)
    next_after = jnp.concatenate([next_used[1:], jnp.full((1,), N_EXPERTS, I32)])
    nxt = jnp.where(next_after < N_EXPERTS, next_after, -1)
    ordinal = jnp.cumsum(used.astype(I32)) - 1
    slot = ordinal % W_SLOTS
    first = jnp.logical_and(used, ordinal == 0)
    latest = lax.cummax(jnp.where(tail > 0, ids, -1))
    before = jnp.concatenate([jnp.full((1,), -1, I32), latest[:-1]])
    pick = lambda index, values: jnp.sum(jnp.where(index[:, None] == ids[None, :], values[None, :], 0), axis=1)
    nxt2 = jnp.where(nxt >= 0, pick(nxt, nxt + 1), 0) - 1
    ptail = pick(before, tail)
    ltail = pick(latest[-1:], tail)
    block_ends = jnp.cumsum(n_blk)
    g = jnp.arange(_max_expert_blocks(n_rows), dtype=I32)
    eg = jnp.sum((g[:, None] >= block_ends[None, :]).astype(I32), axis=1)
    blktok = g * EXPERT_BLOCK + pick(eg, starts - blk0 * EXPERT_BLOCK)
    return tuple(v.astype(I32) for v in (starts, counts, nxt, nxt2, slot, first, blk0, full0, ptail, ltail, blktok,
                                         block_ends[-1:]))


def _experts(xs, counts, w_gate, w_up, w_down):
    d = D_MODEL
    meta = _expert_metadata(counts, xs.shape[0] // TOKEN_ROWS - EXPERT_BLOCK)
    hbm = pl.BlockSpec(memory_space=pl.ANY)
    blk = (EXPERT_BLOCK * TOKEN_ROWS, LANES)
    grid_spec = pltpu.PrefetchScalarGridSpec(
        num_scalar_prefetch=len(meta),
        grid=(N_EXPERTS,),
        in_specs=[hbm, hbm, hbm, hbm],
        out_specs=hbm,
        scratch_shapes=[pltpu.VMEM((W_SLOTS, d, EXPERT_FF), F32), pltpu.VMEM((W_SLOTS, d, EXPERT_FF), F32),
                        pltpu.VMEM((W_SLOTS, EXPERT_FF, d), F32),
                        pltpu.VMEM((d, EXPERT_FF), BF16), pltpu.VMEM((d, EXPERT_FF), BF16),
                        pltpu.VMEM((EXPERT_FF, d), BF16),
                        pltpu.VMEM((X_SLOTS,) + blk, WORD), pltpu.VMEM((Y_SLOTS,) + blk, WORD), pltpu.VMEM(blk, WORD),
                        pltpu.SemaphoreType.DMA((W_SLOTS, 3)), pltpu.SemaphoreType.DMA((X_SLOTS,)),
                        pltpu.SemaphoreType.DMA((Y_SLOTS,)), pltpu.SemaphoreType.DMA(())],
    )
    return pl.pallas_call(
        _experts_kernel,
        grid_spec=grid_spec,
        out_shape=jax.ShapeDtypeStruct(xs.shape, WORD),
        compiler_params=_params(("arbitrary",)),
        name="experts",
    )(*meta, xs, w_gate, w_up, w_down)


SC_CORES = 2
SC_SUBCORES = 16
SC_WORKERS = SC_CORES * SC_SUBCORES
SC_CHUNK = 128
SC_RING = 4


def _sc_scatter_rows(planes, idx, n_out_rows):
    n_planes, t, _ = planes.shape
    ranges = SC_WORKERS // n_planes
    n_chunks = t // ranges // SC_CHUNK
    idx = idx.reshape(TOP_K, n_planes, ranges, n_chunks, SC_CHUNK)
    mesh = plsc.VectorSubcoreMesh(core_axis_name="c", subcore_axis_name="s", num_cores=SC_CORES,
                                  num_subcores=SC_SUBCORES)

    def body(planes_hbm, idx_hbm, out_hbm, idx_v, rows_v, lsem, ssem):
        wid = lax.axis_index("s") * SC_CORES + lax.axis_index("c")
        plane = wid % n_planes
        token0 = (wid // n_planes) * (n_chunks * SC_CHUNK)
        for k in range(TOP_K):
            pltpu.sync_copy(idx_hbm.at[k, plane, wid // n_planes], idx_v.at[k])

        def load(slot, c):
            return pltpu.make_async_copy(planes_hbm.at[plane, pl.ds(token0 + c * SC_CHUNK, SC_CHUNK)], rows_v.at[slot],
                                         lsem.at[slot])

        def scatter(slot, c, k):
            return pltpu.make_async_copy(rows_v.at[slot], out_hbm.at[idx_v.at[k, c]], ssem.at[slot])

        @pl.loop(0, n_chunks, step=SC_RING)
        def _(g):
            for slot in range(SC_RING):
                @pl.when(g > 0)
                def _(slot=slot):
                    for k in range(TOP_K):
                        scatter(slot, 0, k).wait()
                load(slot, g + slot).start()
            for slot in range(SC_RING):
                load(slot, g + slot).wait()
                for k in range(TOP_K):
                    scatter(slot, g + slot, k).start()

        for slot in range(SC_RING):
            for k in range(TOP_K):
                scatter(slot, 0, k).wait()

    return pl.kernel(
        body, mesh=mesh,
        out_type=jax.ShapeDtypeStruct((n_out_rows, LANES), planes.dtype),
        scratch_types=[pltpu.VMEM((TOP_K, n_chunks, SC_CHUNK), I32), pltpu.VMEM((SC_RING, SC_CHUNK, LANES), planes.dtype),
                       pltpu.SemaphoreType.DMA((SC_RING,)), pltpu.SemaphoreType.DMA((SC_RING,))],
        name="sc_scatter",
    )(planes, idx)


def _sc_gather_rows(table, idx):
    n_workers, n_chunks, chunk = idx.shape
    rows_per_worker = n_chunks * chunk
    mesh = plsc.VectorSubcoreMesh(core_axis_name="c", subcore_axis_name="s", num_cores=SC_CORES,
                                  num_subcores=SC_SUBCORES)

    def body(table_hbm, idx_hbm, out_hbm, idx_v, rows_v, gsem, wsem):
        wid = lax.axis_index("s") * SC_CORES + lax.axis_index("c")
        base = wid * rows_per_worker
        pltpu.sync_copy(idx_hbm.at[wid], idx_v)

        def write(slot, c):
            return pltpu.make_async_copy(rows_v.at[slot], out_hbm.at[pl.ds(base + c * chunk, chunk)], wsem.at[slot])

        def gather(slot, c):
            return pltpu.make_async_copy(table_hbm.at[idx_v.at[c]], rows_v.at[slot], gsem.at[slot])

        @pl.loop(0, n_chunks, step=SC_RING)
        def _(g):
            for slot in range(SC_RING):
                @pl.when(g > 0)
                def _(slot=slot):
                    write(slot, 0).wait()
                gather(slot, g + slot).start()
            for slot in range(SC_RING):
                gather(slot, g + slot).wait()
                write(slot, g + slot).start()

        for slot in range(SC_RING):
            write(slot, 0).wait()

    return pl.kernel(
        body, mesh=mesh,
        out_type=jax.ShapeDtypeStruct((n_workers * rows_per_worker, LANES), table.dtype),
        scratch_types=[pltpu.VMEM((n_chunks, chunk), I32), pltpu.VMEM((SC_RING, chunk, LANES), table.dtype),
                       pltpu.SemaphoreType.DMA((SC_RING,)), pltpu.SemaphoreType.DMA((SC_RING,))],
        name="sc_gather",
    )(table, idx)


def _combine_kernel(rows_ref, gates_ref, h_ref, x1_ref, gatef_ref, wg_ref, wu_ref, wd_ref, out_ref):
    hb = h_ref[...].astype(BF16)
    shared = _dot((_silu(_dot(hb, wg_ref[...])) * _dot(hb, wu_ref[...])).astype(BF16), wd_ref[...])
    gates = gates_ref[...]

    def expert_rows(k):
        words = [rows_ref[k * TOKEN_ROWS + c] for c in range(TOKEN_ROWS)]
        lows = [lax.bitcast_convert_type(jnp.left_shift(w, 16), F32) for w in words]
        highs = [lax.bitcast_convert_type(jnp.bitwise_and(w, HIGH_HALF), F32) for w in words]
        return jnp.concatenate(lows + highs, axis=1)

    routed = expert_rows(0) * gates[:, 0:1]
    for k in range(1, TOP_K):
        routed = routed + expert_rows(k) * gates[:, k:k + 1]
    out_ref[...] = x1_ref[...] + gatef_ref[...] * (shared + routed)


def _combine(gathered, gates_t, h2, x1, gate_f, w_gate_s, w_up_s, w_down_s, b, s, tm):
    t, d = h2.shape
    tiles_per_seq = s // tm
    row = lambda n: pl.BlockSpec((tm, n), lambda i: (i, 0))
    full = lambda shp: pl.BlockSpec(shp, lambda i: (0,) * len(shp))
    return pl.pallas_call(
        _combine_kernel,
        grid=(t // tm,),
        in_specs=[pl.BlockSpec((TOP_K * TOKEN_ROWS, tm, LANES), lambda i: (0, i, 0)),
                  row(TOP_K), row(d), row(d),
                  pl.BlockSpec((None, 1, d), lambda i: (i // tiles_per_seq, 0, 0)),
                  full((d, EXPERT_FF)), full((d, EXPERT_FF)), full((EXPERT_FF, d))],
        out_specs=row(d),
        out_shape=jax.ShapeDtypeStruct((t, d), F32),
        compiler_params=_params(("arbitrary",)),
        name="combine",
    )(gathered, gates_t, h2, x1, gate_f.reshape(b, 1, d),
      w_gate_s.astype(BF16), w_up_s.astype(BF16), w_down_s.astype(BF16))


def _moe_sublayer(x1, h2, h2t, gate_f, w_router, router_bias, w_gate, w_up, w_down, w_gate_s, w_up_s, w_down_s, b, s,
                  tm=256):
    t = b * s
    eidx, rank, gates, counts = _router(h2, w_router, router_bias)
    pos = _positions(counts, eidx, rank, 4 * tm)
    idx = pos[:, None, :] * TOKEN_ROWS + jnp.arange(TOKEN_ROWS, dtype=I32)[None, :, None]
    xs = _sc_scatter_rows(h2t, idx, (t * TOP_K + EXPERT_BLOCK) * TOKEN_ROWS)
    ys = _experts(xs, counts[:, 0], w_gate, w_up, w_down)
    gathered = _sc_gather_rows(ys, idx.reshape(SC_WORKERS, -1, SC_CHUNK)).reshape(TOP_K * TOKEN_ROWS, t, LANES)
    return _combine(gathered, gates.T, h2, x1, gate_f, w_gate_s, w_up_s, w_down_s, b, s, 2 * tm)


def kernel(x, c, w_ada, b_ada, norm_mix_gain, w_in, q_norm_gain, k_norm_gain, rel_bias_table, conv_w, conv_b, dt_bias,
           a_log, d_skip, ssm_norm_gain, w_out, norm_ffn_gain, w_router, router_bias, w_gate_experts, w_up_experts,
           w_down_experts, w_gate_shared, w_up_shared, w_down_shared):
    b, s, d = x.shape
    for layer in range(w_ada.shape[0]):
        mod = _adaln(c, w_ada[layer], b_ada[layer])
        x1, h2, h2t = _mixer_sublayer(x, mod, norm_mix_gain[layer], w_in[layer], q_norm_gain[layer], k_norm_gain[layer],
                                 rel_bias_table, conv_w[layer], conv_b[layer], dt_bias[layer], a_log[layer],
                                 d_skip[layer], ssm_norm_gain[layer], w_out[layer], norm_ffn_gain[layer])
        gate_f = mod[:, 5 * d:]
        out = _moe_sublayer(x1, h2, h2t, gate_f, w_router[layer], router_bias[layer], w_gate_experts[layer],
                            w_up_experts[layer], w_down_experts[layer], w_gate_shared[layer], w_up_shared[layer],
                            w_down_shared[layer], b, s)
        x = out.reshape(b, s, d)
    return x
```

```python
import math

import numpy as np
import jax
import jax.numpy as jnp
from jax import lax
from jax.experimental import pallas as pl
from jax.experimental.pallas import tpu as pltpu
from jax.experimental.pallas import tpu_sc as plsc

F32 = jnp.float32
BF16 = jnp.bfloat16
I32 = jnp.int32

D_MODEL = 1024
ATTN_HEADS = 8
HEAD_DIM = 64
ATTN_WIDTH = ATTN_HEADS * HEAD_DIM
PATTERNS = ((128, 1), (512, 4), (2048, 16))
WIN_STEPS = 128
REL_BUCKETS = 32
REL_MAX_DISTANCE = 2048
SSM_HEADS = 24
SSM_HEAD_DIM = 64
SSM_WIDTH = SSM_HEADS * SSM_HEAD_DIM
SSM_GROUPS = 4
HEADS_PER_GROUP = SSM_HEADS // SSM_GROUPS
GROUP_WIDTH = SSM_WIDTH // SSM_GROUPS
SSM_STATE = 128
SSM_CONV = 4
SSM_CHUNK = 128
CONV_CH = SSM_WIDTH + 2 * SSM_GROUPS * SSM_STATE
N_EXPERTS = 256
TOP_K = 8
N_EXPERT_GROUPS = 8
EXPERTS_PER_GROUP = N_EXPERTS // N_EXPERT_GROUPS
TOPK_GROUPS = 4
EXPERT_FF = 256
ROUTED_SCALE = 2.5
NORM_EPS = 1e-6

LANES = 128
SUBLANES = 8
NEG_BIG = -1e30
VMEM_LIMIT = 56 * 1024 * 1024


def _params(sem, vmem=VMEM_LIMIT):
    return pltpu.CompilerParams(dimension_semantics=sem, vmem_limit_bytes=vmem)


def _sigmoid(x):
    return 1.0 / (1.0 + jnp.exp(-x))


def _silu(x):
    return x * _sigmoid(x)


def _split3(x):
    hi = x.astype(BF16)
    r = x - hi.astype(F32)
    mid = r.astype(BF16)
    lo = (r - mid.astype(F32)).astype(BF16)
    return hi, mid, lo


def _dot(a, b):
    return jnp.dot(a, b, preferred_element_type=F32)


def _dot_nt(a, b):
    return lax.dot_general(a, b, (((1,), (1,)), ((), ())), preferred_element_type=F32)


def _dot_exact_rhs(a, b_exact):
    hi, mid, lo = _split3(a)
    return _dot(hi, b_exact) + _dot(mid, b_exact) + _dot(lo, b_exact)


def _dot_exact_lhs(a_exact, b):
    hi, mid, lo = _split3(b)
    return _dot(a_exact, hi) + _dot(a_exact, mid) + _dot(a_exact, lo)


def _adaln_kernel(c_ref, w_ref, b_ref, o_ref):
    s = _silu(c_ref[...]).astype(BF16)
    o_ref[...] = _dot(s, w_ref[...].astype(BF16)) + b_ref[...]


def _adaln(c, w_ada, b_ada):
    b, d = c.shape
    n = w_ada.shape[1]
    rows = SUBLANES
    c_pad = jnp.zeros((rows, d), F32).at[:b].set(c)
    tn = 1024
    out = pl.pallas_call(
        _adaln_kernel,
        grid=(n // tn,),
        in_specs=[pl.BlockSpec((rows, d), lambda j: (0, 0)),
                  pl.BlockSpec((d, tn), lambda j: (0, j)),
                  pl.BlockSpec((1, tn), lambda j: (0, j))],
        out_specs=pl.BlockSpec((rows, tn), lambda j: (0, j)),
        out_shape=jax.ShapeDtypeStruct((rows, n), F32),
        compiler_params=_params(("arbitrary",)),
        name="adaln",
    )(c_pad, w_ada, b_ada.reshape(1, n))
    return out[:b]


def _inproj_kernel(x_ref, shift_ref, scale_ref, g_ref, wqkv_ref, wz_ref, wxbc_ref, wdt_ref,
                   qg_ref, kg_ref, hmean_ref, q_ref, k_ref, v_ref, z_ref, xbc_ref, dt_ref):
    x = x_ref[...]
    ms = jnp.mean(x * x, axis=-1, keepdims=True)
    h = x * lax.rsqrt(ms + NORM_EPS) * g_ref[...]
    h = h * (1.0 + scale_ref[...]) + shift_ref[...]
    hb = h.astype(BF16)

    hmean = hmean_ref[...]

    def head_norm(t, gain):
        sq = t * t
        hi = sq.astype(BF16)
        mid = (sq - hi.astype(F32)).astype(BF16)
        ss = _dot(hi, hmean) + _dot(mid, hmean)
        return t * lax.rsqrt(ss + NORM_EPS) * gain

    q = _dot(hb, wqkv_ref[:, 0:ATTN_WIDTH])
    q_ref[...] = head_norm(q, qg_ref[...]) * (HEAD_DIM ** -0.5)
    k = _dot(hb, wqkv_ref[:, ATTN_WIDTH:2 * ATTN_WIDTH])
    k_ref[...] = head_norm(k, kg_ref[...])
    v_ref[...] = _dot(hb, wqkv_ref[:, 2 * ATTN_WIDTH:3 * ATTN_WIDTH])
    for c0 in range(0, SSM_WIDTH, 512):
        z_ref[:, c0:c0 + 512] = _dot(hb, wz_ref[:, c0:c0 + 512])
    for c0 in range(0, CONV_CH, 512):
        xbc_ref[:, c0:c0 + 512] = _dot(hb, wxbc_ref[:, c0:c0 + 512])
    dt_ref[...] = _dot(hb, wdt_ref[...])


def _in_proj(x, shift, scale, gain, w_in, q_gain, k_gain, tm=256):
    b, s, d = x.shape
    t = b * s
    tiles_per_seq = s // tm
    w = w_in.astype(BF16)
    o_z = 3 * ATTN_WIDTH
    o_x = o_z + SSM_WIDTH
    o_dt = o_x + CONV_CH
    w_qkv, w_z, w_xbc = w[:, :o_z], w[:, o_z:o_x], w[:, o_x:o_dt]
    w_dt = jnp.zeros((d, LANES), BF16).at[:, :SSM_HEADS].set(w[:, o_dt:])
    head_of = np.arange(ATTN_WIDTH) // HEAD_DIM
    hmean = jnp.asarray((head_of[:, None] == head_of[None, :]).astype(np.float32) / HEAD_DIM, BF16)
    full = lambda shp: pl.BlockSpec(shp, lambda i: (0,) * len(shp))
    row = lambda n: pl.BlockSpec((tm, n), lambda i: (i, 0))
    per_batch = pl.BlockSpec((None, 1, d), lambda i: (i // tiles_per_seq, 0, 0))
    outs = pl.pallas_call(
        _inproj_kernel,
        grid=(t // tm,),
        in_specs=[row(d), per_batch, per_batch, full((1, d)),
                  full((d, o_z)), full((d, SSM_WIDTH)), full((d, CONV_CH)), full((d, LANES)),
                  full((1, ATTN_WIDTH)), full((1, ATTN_WIDTH)), full((ATTN_WIDTH, ATTN_WIDTH))],
        out_specs=[row(ATTN_WIDTH), row(ATTN_WIDTH), row(ATTN_WIDTH), row(SSM_WIDTH), row(CONV_CH), row(LANES)],
        out_shape=[jax.ShapeDtypeStruct((t, n), F32)
                   for n in (ATTN_WIDTH, ATTN_WIDTH, ATTN_WIDTH, SSM_WIDTH, CONV_CH, LANES)],
        compiler_params=_params(("arbitrary",)),
        name="in_proj",
    )(x.reshape(t, d), shift.reshape(b, 1, d), scale.reshape(b, 1, d), gain.reshape(1, d),
      w_qkv, w_z, w_xbc, w_dt,
      jnp.tile(q_gain, ATTN_HEADS).reshape(1, ATTN_WIDTH), jnp.tile(k_gain, ATTN_HEADS).reshape(1, ATTN_WIDTH), hmean)
    return outs


def _t5_causal_buckets(distance):
    n = np.maximum(distance, 0)
    max_exact = REL_BUCKETS // 2
    large = max_exact + (np.log(np.maximum(n, 1) / max_exact) / math.log(REL_MAX_DISTANCE / max_exact)
                         * (REL_BUCKETS - max_exact)).astype(np.int64)
    large = np.minimum(large, REL_BUCKETS - 1)
    return np.where(n < max_exact, n, large).astype(np.int32)


def _window_bias(rel_bias_table, dilation):
    qi = np.arange(WIN_STEPS)[:, None]
    kj = np.arange(2 * WIN_STEPS)[None, :]
    dist = qi + WIN_STEPS - kj
    band = (dist >= 0) & (dist <= WIN_STEPS)
    onehot = (_t5_causal_buckets(dist * dilation).reshape(-1, 1) == np.arange(REL_BUCKETS)[None, :]).astype(np.float32)
    bias = jnp.dot(rel_bias_table.astype(F32).T, jnp.asarray(onehot).T, precision=lax.Precision.HIGHEST)
    bias = bias.reshape(ATTN_HEADS, WIN_STEPS, 2 * WIN_STEPS)
    return jnp.where(jnp.asarray(band)[None], bias, NEG_BIG)


ATTN_TOKENS = max(w for w, _ in PATTERNS)
ATTN_UNROLL = 16


def _attn_kernel(q_ref, kp_ref, kc_ref, vp_ref, vc_ref, bias_ref, out_ref, kw, vw, o_acc, l_acc):
    tb = ATTN_TOKENS
    first = pl.program_id(2) == 0
    kw[0:tb] = kp_ref[...]
    kw[tb:2 * tb] = kc_ref[...]
    vw[0:tb] = vp_ref[...]
    vw[tb:2 * tb] = vc_ref[...]
    lane = lax.broadcasted_iota(I32, (WIN_STEPS, LANES), 1)
    head0 = lane < HEAD_DIM
    col = lax.broadcasted_iota(I32, (WIN_STEPS, 2 * WIN_STEPS), 1)
    in_prev = col < WIN_STEPS

    for p, (_, d) in enumerate(PATTERNS):
        shift = d.bit_length() - 1
        n_blocks = tb // WIN_STEPS

        def rows(start, n, d=d):
            return pl.ds(start, n, stride=d) if d > 1 else pl.ds(start, n)

        def body(it, carry, p=p, d=d, shift=shift, rows=rows):
            for u in range(ATTN_UNROLL):
                idx = it * ATTN_UNROLL + u
                r = jnp.bitwise_and(idx, d - 1)
                j = jnp.right_shift(idx, shift)
                qs = j * (WIN_STEPS * d) + r
                q = q_ref[rows(qs, WIN_STEPS), :]
                k = kw[rows(tb + qs - WIN_STEPS * d, 2 * WIN_STEPS), :].astype(BF16)
                v = vw[rows(tb + qs - WIN_STEPS * d, 2 * WIN_STEPS), :].astype(BF16)
                no_prev = jnp.logical_and(in_prev, jnp.logical_and(first, j == 0))
                o_h, lse_h = [], []
                for h in range(2):
                    qh = jnp.where(head0 if h == 0 else jnp.logical_not(head0), q, 0.0).astype(BF16)
                    s = _dot_nt(qh, k) + bias_ref[p, h]
                    s = jnp.where(no_prev, NEG_BIG, s)
                    m = jnp.max(s, axis=-1, keepdims=True)
                    e = jnp.exp(s - m)
                    denom = jnp.sum(e, axis=-1, keepdims=True)
                    o_h.append(_dot(e.astype(BF16), v) / denom)
                    lse_h.append(m + jnp.log(denom))
                o_acc[p, rows(qs, WIN_STEPS), :] = jnp.where(head0, o_h[0], o_h[1])
                l_acc[p, rows(qs, WIN_STEPS), :] = jnp.where(head0, lse_h[0], lse_h[1])
            return carry

        lax.fori_loop(0, n_blocks // ATTN_UNROLL, body, 0)

    chunk = 256
    for c0 in range(0, tb, chunk):
        l1, l2, l3 = (l_acc[p, c0:c0 + chunk, :] for p in range(3))
        m = jnp.maximum(jnp.maximum(l1, l2), l3)
        e1, e2, e3 = jnp.exp(l1 - m), jnp.exp(l2 - m), jnp.exp(l3 - m)
        num = e1 * o_acc[0, c0:c0 + chunk, :] + e2 * o_acc[1, c0:c0 + chunk, :] + e3 * o_acc[2, c0:c0 + chunk, :]
        out_ref[c0:c0 + chunk, :] = num / (e1 + e2 + e3)


def _attention(q, k, v, bias):
    b, s, w = q.shape
    tb = ATTN_TOKENS
    pairs = ATTN_HEADS // 2
    cur = pl.BlockSpec((None, tb, LANES), lambda bi, hp, i: (bi, i, hp))
    prev = pl.BlockSpec((None, tb, LANES), lambda bi, hp, i: (bi, jnp.maximum(i - 1, 0), hp))
    return pl.pallas_call(
        _attn_kernel,
        grid=(b, pairs, s // tb),
        in_specs=[cur, prev, cur, prev, cur,
                  pl.BlockSpec((len(PATTERNS), 2, WIN_STEPS, 2 * WIN_STEPS), lambda bi, hp, i: (0, hp, 0, 0))],
        out_specs=cur,
        out_shape=jax.ShapeDtypeStruct((b, s, w), F32),
        scratch_shapes=[pltpu.VMEM((2 * tb, LANES), F32), pltpu.VMEM((2 * tb, LANES), F32),
                        pltpu.VMEM((len(PATTERNS), tb, LANES), F32), pltpu.VMEM((len(PATTERNS), tb, LANES), F32)],
        compiler_params=_params(("arbitrary",) * 3),
        name="attention",
    )(q, k, k, v, v, bias)


SSD_CHUNKS_PER_STEP = 2


def _ssd_kernel(xbc_ref, halo_ref, z_ref, dtraw_ref, convw_ref, convb_ref, dtb_ref, alog_ref, dskip_ref, gain_ref,
                expand_ref, tril_ref, y_ref, state_ref):
    c = pl.program_id(1)

    @pl.when(c == 0)
    def _():
        state_ref[...] = jnp.zeros_like(state_ref)

    halo = jnp.where(c == 0, 0.0, halo_ref[...])
    for sub in range(SSD_CHUNKS_PER_STEP):
        rows = slice(sub * SSM_CHUNK, (sub + 1) * SSM_CHUNK)
        x = xbc_ref[rows, :]
        w = convw_ref[...]
        acc = x * w[SSM_CONV - 1:SSM_CONV, :] + convb_ref[...]
        row8 = lax.broadcasted_iota(I32, (SUBLANES, CONV_CH), 0)
        for shift in range(1, SSM_CONV):
            xs = pltpu.roll(x, shift, axis=0)
            hs = pltpu.roll(halo, shift, axis=0)
            head = jnp.where(row8 < shift, hs, xs[0:SUBLANES])
            xs = jnp.concatenate([head, xs[SUBLANES:]], axis=0)
            acc = acc + xs * w[SSM_CONV - 1 - shift:SSM_CONV - shift, :]
        act = _silu(acc)
        x_s = act[:, :SSM_WIDTH]
        bc0 = SSM_WIDTH
        cc0 = SSM_WIDTH + SSM_GROUPS * SSM_STATE

        t = dtraw_ref[rows, :] + dtb_ref[...]
        dt = jnp.maximum(t, 0.0) + jnp.log(1.0 + jnp.exp(-jnp.abs(t)))
        a = dt * (-jnp.exp(alog_ref[...]))
        a_cs = _dot_exact_lhs(tril_ref[...], a)
        a_cs_t = a_cs.T
        a_last = a_cs[SSM_CHUNK - 1:SSM_CHUNK, :]
        expand = expand_ref[...]
        dt_e = _dot_exact_rhs(dt, expand)
        ea_e = _dot_exact_rhs(jnp.exp(a_cs), expand)
        dte_e = _dot_exact_rhs(jnp.exp(a_last - a_cs), expand)
        xdt = x_s * dt_e
        xw = (xdt * dte_e).astype(BF16)
        xdt_b = xdt.astype(BF16)

        li = lax.broadcasted_iota(I32, (SSM_CHUNK, SSM_CHUNK), 0)
        si = lax.broadcasted_iota(I32, (SSM_CHUNK, SSM_CHUNK), 1)
        causal = li >= si

        ys = []
        for g in range(SSM_GROUPS):
            gs = slice(g * GROUP_WIDTH, (g + 1) * GROUP_WIDTH)
            b_g = act[:, bc0 + g * SSM_STATE:bc0 + (g + 1) * SSM_STATE]
            c_g = act[:, cc0 + g * SSM_STATE:cc0 + (g + 1) * SSM_STATE].astype(BF16)
            cb = _dot_nt(c_g, b_g.astype(BF16))
            state = state_ref[g]
            y_off = _dot(c_g, state.astype(BF16)) * ea_e[:, gs]
            parts = []
            for j in range(HEADS_PER_GROUP):
                hh = g * HEADS_PER_GROUP + j
                seg = a_cs[:, hh:hh + 1] - a_cs_t[hh:hh + 1, :]
                decay = jnp.exp(jnp.where(causal, seg, NEG_BIG))
                m = (cb * decay).astype(BF16)
                parts.append(_dot(m, xdt_b[:, hh * SSM_HEAD_DIM:(hh + 1) * SSM_HEAD_DIM]))
            ys.append(jnp.concatenate(parts, axis=1) + y_off)
            state_ref[g] = state * ea_e[SSM_CHUNK - 1:SSM_CHUNK, gs] + _dot(b_g.T.astype(BF16), xw[:, gs])
        y = jnp.concatenate(ys, axis=1) + dskip_ref[...] * x_s
        y = y * _silu(z_ref[rows, :])
        gain = gain_ref[...]
        for g in range(SSM_GROUPS):
            gs = slice(g * GROUP_WIDTH, (g + 1) * GROUP_WIDTH)
            yg = y[:, gs]
            ms = jnp.mean(yg * yg, axis=-1, keepdims=True)
            y_ref[rows, gs] = yg * lax.rsqrt(ms + NORM_EPS) * gain[:, gs]
        halo = x[SSM_CHUNK - SUBLANES:, :]


def _ssd(xbc, z, dt_raw, conv_w, conv_b, dt_bias, a_log, d_skip, norm_gain, b, s):
    t = b * s
    step_rows = SSD_CHUNKS_PER_STEP * SSM_CHUNK
    nc = s // step_rows
    pad_heads = lambda v: jnp.zeros((1, LANES), F32).at[0, :SSM_HEADS].set(v)
    head_of_lane = np.arange(SSM_WIDTH) // SSM_HEAD_DIM
    expand = jnp.asarray((np.arange(LANES)[:, None] == head_of_lane[None, :]).astype(np.float32), BF16)
    tril = jnp.asarray(np.tril(np.ones((SSM_CHUNK, SSM_CHUNK), np.float32)), BF16)
    halo_blocks = step_rows // SUBLANES
    chunk = lambda n: pl.BlockSpec((step_rows, n), lambda bi, c: (bi * nc + c, 0))
    full = lambda shp: pl.BlockSpec(shp, lambda bi, c: (0,) * len(shp))
    halo = pl.BlockSpec((SUBLANES, CONV_CH), lambda bi, c: (jnp.maximum((bi * nc + c) * halo_blocks - 1, 0), 0))
    return pl.pallas_call(
        _ssd_kernel,
        grid=(b, nc),
        in_specs=[chunk(CONV_CH), halo, chunk(SSM_WIDTH), chunk(LANES),
                  full((SSM_CONV, CONV_CH)), full((1, CONV_CH)), full((1, LANES)), full((1, LANES)),
                  full((1, SSM_WIDTH)), full((1, SSM_WIDTH)), full((LANES, SSM_WIDTH)), full((SSM_CHUNK, SSM_CHUNK))],
        out_specs=chunk(SSM_WIDTH),
        out_shape=jax.ShapeDtypeStruct((t, SSM_WIDTH), F32),
        scratch_shapes=[pltpu.VMEM((SSM_GROUPS, SSM_STATE, GROUP_WIDTH), F32)],
        compiler_params=_params(("arbitrary", "arbitrary")),
        name="ssd",
    )(xbc, xbc, z, dt_raw, conv_w, conv_b.reshape(1, CONV_CH), pad_heads(dt_bias), pad_heads(a_log),
      jnp.repeat(d_skip, SSM_HEAD_DIM).reshape(1, SSM_WIDTH), norm_gain.reshape(1, SSM_WIDTH), expand, tril)


WORD = jnp.int32
TOKEN_ROWS = D_MODEL // (2 * LANES)
HIGH_HALF = np.int32(-65536)


def _to_token_tiles(ref, x):
    n = x.shape[0]
    for c in range(TOKEN_ROWS):
        lo = lax.bitcast_convert_type(x[:, c * LANES:(c + 1) * LANES].astype(BF16).astype(F32), WORD)
        hi = lax.bitcast_convert_type(x[:, (c + TOKEN_ROWS) * LANES:(c + TOKEN_ROWS + 1) * LANES]
                                      .astype(BF16).astype(F32), WORD)
        ref[pl.ds(c, n, stride=TOKEN_ROWS), :] = jnp.bitwise_or(lax.shift_right_logical(lo, 16),
                                                                 jnp.bitwise_and(hi, HIGH_HALF))


def _from_token_tiles(ref, n, token0=0):
    lows, highs = [], []
    for c in range(TOKEN_ROWS):
        word = ref[pl.ds(token0 * TOKEN_ROWS + c, n, stride=TOKEN_ROWS), :]
        lows.append(lax.bitcast_convert_type(jnp.left_shift(word, 16), F32))
        highs.append(lax.bitcast_convert_type(jnp.bitwise_and(word, HIGH_HALF), F32))
    return jnp.concatenate(lows + highs, axis=1)


def _outproj_kernel(attn_ref, ssm_ref, x_ref, gate_ref, shift_ref, scale_ref, g_ref, wa_ref, ws_ref, x1_ref, h2_ref,
                    h2t_ref):
    mixed = _dot(attn_ref[...].astype(BF16), wa_ref[...]) + _dot(ssm_ref[...].astype(BF16), ws_ref[...])
    x1 = x_ref[...] + gate_ref[...] * mixed
    x1_ref[...] = x1
    ms = jnp.mean(x1 * x1, axis=-1, keepdims=True)
    h = x1 * lax.rsqrt(ms + NORM_EPS) * g_ref[...]
    h2 = h * (1.0 + scale_ref[...]) + shift_ref[...]
    h2_ref[...] = h2
    for c in range(TOKEN_ROWS):
        lo = lax.bitcast_convert_type(h2[:, c * LANES:(c + 1) * LANES].astype(BF16).astype(F32), WORD)
        hi = lax.bitcast_convert_type(h2[:, (c + TOKEN_ROWS) * LANES:(c + TOKEN_ROWS + 1) * LANES]
                                      .astype(BF16).astype(F32), WORD)
        h2t_ref[c] = jnp.bitwise_or(lax.shift_right_logical(lo, 16), jnp.bitwise_and(hi, HIGH_HALF))


def _out_proj(attn, ssm, x, gate, shift, scale, gain, w_out, b, s, tm=512):
    t = b * s
    d = D_MODEL
    tiles_per_seq = s // tm
    w = w_out.astype(BF16)
    row = lambda n: pl.BlockSpec((tm, n), lambda i: (i, 0))
    full = lambda shp: pl.BlockSpec(shp, lambda i: (0,) * len(shp))
    per_batch = pl.BlockSpec((None, 1, d), lambda i: (i // tiles_per_seq, 0, 0))
    return pl.pallas_call(
        _outproj_kernel,
        grid=(t // tm,),
        in_specs=[row(ATTN_WIDTH), row(SSM_WIDTH), row(d), per_batch, per_batch, per_batch, full((1, d)),
                  full((ATTN_WIDTH, d)), full((SSM_WIDTH, d))],
        out_specs=[row(d), row(d), pl.BlockSpec((TOKEN_ROWS, tm, LANES), lambda i: (0, i, 0))],
        out_shape=[jax.ShapeDtypeStruct((t, d), F32)] * 2 + [jax.ShapeDtypeStruct((TOKEN_ROWS, t, LANES), WORD)],
        compiler_params=_params(("arbitrary",)),
        name="out_proj",
    )(attn.reshape(t, ATTN_WIDTH), ssm, x.reshape(t, d),
      gate.reshape(b, 1, d), shift.reshape(b, 1, d), scale.reshape(b, 1, d), gain.reshape(1, d),
      w[:ATTN_WIDTH], w[ATTN_WIDTH:])


def _mixer_sublayer(x, mod, norm_mix_gain, w_in, q_norm_gain, k_norm_gain, rel_bias_table, conv_w, conv_b, dt_bias,
                    a_log, d_skip, ssm_norm_gain, w_out, norm_ffn_gain):
    b, s, d = x.shape
    shift_m, scale_m, gate_m, shift_f, scale_f, _ = jnp.split(mod, 6, axis=-1)
    q, k, v, z, xbc, dt_raw = _in_proj(x, shift_m, scale_m, norm_mix_gain, w_in, q_norm_gain, k_norm_gain)
    bias = jnp.stack([_window_bias(rel_bias_table, dilation) for _, dilation in PATTERNS])
    attn = _attention(q.reshape(b, s, ATTN_WIDTH), k.reshape(b, s, ATTN_WIDTH), v.reshape(b, s, ATTN_WIDTH), bias)
    ssm = _ssd(xbc, z, dt_raw, conv_w, conv_b, dt_bias, a_log, d_skip, ssm_norm_gain, b, s)
    return _out_proj(attn, ssm, x, gate_m, shift_f, scale_f, norm_ffn_gain, w_out, b, s)


def _first_argmax(v, iota, limit):
    m = jnp.max(v, axis=0, keepdims=True)
    idx = jnp.min(jnp.where(v == m, iota, limit), axis=0, keepdims=True)
    return m, idx


def _router_kernel(h_ref, wt_ref, bias_ref, upper_ref, eidx_ref, rank_ref, gate_ref, counts_ref, carry_ref):
    @pl.when(pl.program_id(0) == 0)
    def _():
        carry_ref[...] = jnp.zeros_like(carry_ref)

    tm = h_ref.shape[0]
    h = h_ref[...]
    wt = wt_ref[...]
    h_hi = h.astype(BF16)
    h_lo = (h - h_hi.astype(F32)).astype(BF16)
    w_hi = wt.astype(BF16)
    w_lo = (wt - w_hi.astype(F32)).astype(BF16)
    logits = _dot_nt(w_hi, h_hi) + _dot_nt(w_hi, h_lo) + _dot_nt(w_lo, h_hi)
    scores = _sigmoid(logits)
    choice = scores + bias_ref[...]
    neg_inf = -jnp.inf

    iota_g = lax.broadcasted_iota(I32, (EXPERTS_PER_GROUP, tm), 0).astype(F32)
    group_rows = []
    for g in range(N_EXPERT_GROUPS):
        v = choice[g * EXPERTS_PER_GROUP:(g + 1) * EXPERTS_PER_GROUP]
        m1, i1 = _first_argmax(v, iota_g, float(EXPERTS_PER_GROUP))
        m2 = jnp.max(jnp.where(iota_g == i1, neg_inf, v), axis=0, keepdims=True)
        group_rows.append(m1 + m2)
    group_scores = jnp.concatenate(group_rows, axis=0)

    iota_n = lax.broadcasted_iota(I32, (N_EXPERT_GROUPS, tm), 0).astype(F32)
    chosen = jnp.zeros((N_EXPERT_GROUPS, tm), F32)
    for _ in range(TOPK_GROUPS):
        _, gi = _first_argmax(group_scores, iota_n, float(N_EXPERT_GROUPS))
        hit = iota_n == gi
        chosen = jnp.where(hit, 1.0, chosen)
        group_scores = jnp.where(hit, neg_inf, group_scores)

    masked = jnp.concatenate(
        [jnp.where(chosen[g:g + 1] > 0.0, choice[g * EXPERTS_PER_GROUP:(g + 1) * EXPERTS_PER_GROUP], neg_inf)
         for g in range(N_EXPERT_GROUPS)], axis=0)

    iota_e = lax.broadcasted_iota(I32, (N_EXPERTS, tm), 0).astype(F32)
    picked, gates = [], []
    onehot = jnp.zeros((N_EXPERTS, tm), F32)
    for _ in range(TOP_K):
        _, ei = _first_argmax(masked, iota_e, float(N_EXPERTS))
        hit = iota_e == ei
        gates.append(jnp.sum(jnp.where(hit, scores, 0.0), axis=0, keepdims=True))
        masked = jnp.where(hit, neg_inf, masked)
        onehot = jnp.where(hit, 1.0, onehot)
        picked.append(ei)
    gate_sum = gates[0]
    for gk in gates[1:]:
        gate_sum = gate_sum + gk

    base = _dot(onehot.astype(BF16), upper_ref[...]) + carry_ref[...]
    ranks = [jnp.sum(jnp.where(iota_e == ei, base, 0.0), axis=0, keepdims=True) for ei in picked]
    carry_ref[...] = carry_ref[...] + jnp.sum(onehot, axis=1, keepdims=True)

    eidx_ref[...] = jnp.concatenate(picked, axis=0).astype(I32)
    rank_ref[...] = jnp.concatenate(ranks, axis=0).astype(I32)
    gate_ref[...] = jnp.concatenate([gk / gate_sum * ROUTED_SCALE for gk in gates], axis=0)
    counts_ref[...] = carry_ref[...].astype(I32)


def _router(h2, w_router, router_bias, tm=256):
    t, d = h2.shape
    upper = jnp.asarray(np.triu(np.ones((tm, tm), np.float32), 1), BF16)
    tok = pl.BlockSpec((TOP_K, tm), lambda i: (0, i))
    full = lambda shp: pl.BlockSpec(shp, lambda i: (0,) * len(shp))
    return pl.pallas_call(
        _router_kernel,
        grid=(t // tm,),
        in_specs=[pl.BlockSpec((tm, d), lambda i: (i, 0)), full((N_EXPERTS, d)), full((N_EXPERTS, 1)), full((tm, tm))],
        out_specs=[tok, tok, tok, full((N_EXPERTS, 1))],
        out_shape=[jax.ShapeDtypeStruct((TOP_K, t), I32), jax.ShapeDtypeStruct((TOP_K, t), I32),
                   jax.ShapeDtypeStruct((TOP_K, t), F32), jax.ShapeDtypeStruct((N_EXPERTS, 1), I32)],
        scratch_shapes=[pltpu.VMEM((N_EXPERTS, 1), F32)],
        compiler_params=_params(("arbitrary",)),
        name="router",
    )(h2, w_router.T, router_bias.reshape(N_EXPERTS, 1), upper)


def _positions_kernel(counts_ref, lower_ref, eidx_ref, rank_ref, pos_ref):
    tm = eidx_ref.shape[1]
    counts = jnp.broadcast_to(counts_ref[...].astype(F32), (N_EXPERTS, LANES))
    offsets = _dot_exact_lhs(lower_ref[...], counts)[:, 0:1]
    iota_e = lax.broadcasted_iota(I32, (N_EXPERTS, tm), 0).astype(F32)
    e = eidx_ref[...].astype(F32)
    rows = [jnp.sum(jnp.where(iota_e == e[k:k + 1], offsets, 0.0), axis=0, keepdims=True) for k in range(TOP_K)]
    pos_ref[...] = jnp.concatenate(rows, axis=0).astype(I32) + rank_ref[...]


def _positions(counts, eidx, rank, tm):
    t = eidx.shape[1]
    lower = jnp.asarray(np.tril(np.ones((N_EXPERTS, N_EXPERTS), np.float32), -1), BF16)
    tok = pl.BlockSpec((TOP_K, tm), lambda i: (0, i))
    return pl.pallas_call(
        _positions_kernel,
        grid=(t // tm,),
        in_specs=[pl.BlockSpec((N_EXPERTS, 1), lambda i: (0, 0)), pl.BlockSpec((N_EXPERTS, N_EXPERTS), lambda i: (0, 0)),
                  tok, tok],
        out_specs=tok,
        out_shape=jax.ShapeDtypeStruct((TOP_K, t), I32),
        compiler_params=_params(("arbitrary",)),
        name="positions",
    )(counts, lower, eidx, rank)


EXPERT_BLOCK = 512
TAIL_UNIT = 128
TAIL_PIECES = tuple(1 << i for i in reversed(range(EXPERT_BLOCK.bit_length() - 1)))
W_SLOTS = 3
X_SLOTS = 4
Y_SLOTS = 4


def _experts_kernel(start_ref, count_ref, nxt_ref, nxt2_ref, slot_ref, first_ref, blk0_ref, full0_ref, ptail_ref,
                    ltail_ref, blktok_ref, nblocks_ref, xs_hbm, wg_hbm, wu_hbm, wd_hbm, ys_hbm,
                    wg_buf, wu_buf, wd_buf, wg_bf, wu_bf, wd_bf, xbuf, ybuf, ytail, wsem, xsem, ysem, tsem):
    e = pl.program_id(0)
    last_step = e == pl.num_programs(0) - 1
    start, count = start_ref[e], count_ref[e]
    n_full = jnp.right_shift(count, EXPERT_BLOCK.bit_length() - 1)
    tail = jnp.bitwise_and(count, EXPERT_BLOCK - 1)
    n_blk = n_full + (tail > 0).astype(I32)
    blk0, full0 = blk0_ref[e], full0_ref[e]
    slot, nxt, nxt2 = slot_ref[e], nxt_ref[e], nxt2_ref[e]
    slot1 = jnp.where(slot + 1 >= W_SLOTS, slot + 1 - W_SLOTS, slot + 1)
    slot2 = jnp.where(slot + 2 >= W_SLOTS, slot + 2 - W_SLOTS, slot + 2)

    def token_rows(token, n):
        return pl.ds(pl.multiple_of(token * TOKEN_ROWS, TOKEN_ROWS), n * TOKEN_ROWS)

    def fetch(ex, s):
        return (pltpu.make_async_copy(wg_hbm.at[ex], wg_buf.at[s], wsem.at[s, 0]),
                pltpu.make_async_copy(wu_hbm.at[ex], wu_buf.at[s], wsem.at[s, 1]),
                pltpu.make_async_copy(wd_hbm.at[ex], wd_buf.at[s], wsem.at[s, 2]))

    def x_copy(token, s):
        return pltpu.make_async_copy(xs_hbm.at[token_rows(token, EXPERT_BLOCK)], xbuf.at[s], xsem.at[s])

    def y_copy(token, s):
        return pltpu.make_async_copy(ybuf.at[s], ys_hbm.at[token_rows(token, EXPERT_BLOCK)], ysem.at[s])

    def tail_copies(token, length):
        out = []
        for piece in TAIL_PIECES:
            bigger = (EXPERT_BLOCK - 1) & ~(2 * piece - 1)
            done = jnp.bitwise_and(length, bigger)
            cp = pltpu.make_async_copy(ytail.at[token_rows(done, piece)], ys_hbm.at[token_rows(token + done, piece)],
                                       tsem)
            out.append((jnp.bitwise_and(length, piece) != 0, cp))
        return out

    def block(s, rows=EXPERT_BLOCK):
        x = _from_token_tiles(xbuf.at[s], rows).astype(BF16)
        g = _dot(x, wg_bf[...])
        u = _dot(x, wu_bf[...])
        return _dot((_silu(g) * u).astype(BF16), wd_bf[...])

    @pl.when(count > 0)
    def _():
        @pl.when(first_ref[e] == 1)
        def _():
            for g in range(X_SLOTS - 1):
                @pl.when(g < nblocks_ref[0])
                def _(g=g):
                    x_copy(blktok_ref[g], g).start()
            for cp in fetch(e, slot):
                cp.start()

            @pl.when(nxt >= 0)
            def _():
                for cp in fetch(nxt, slot1):
                    cp.start()

        for cp in fetch(e, slot):
            cp.wait()

        @pl.when(nxt2 >= 0)
        def _():
            for cp in fetch(nxt2, slot2):
                cp.start()

        wg_bf[...] = wg_buf[slot].astype(BF16)
        wu_bf[...] = wu_buf[slot].astype(BF16)
        wd_bf[...] = wd_buf[slot].astype(BF16)

        def take_x(i):
            g = blk0 + i
            xs_slot = jnp.bitwise_and(g, X_SLOTS - 1)
            x_copy(start, xs_slot).wait()
            ahead = g + (X_SLOTS - 1)

            @pl.when(ahead < nblocks_ref[0])
            def _():
                x_copy(blktok_ref[ahead], jnp.bitwise_and(ahead, X_SLOTS - 1)).start()

            return xs_slot

        def full_block(i, carry):
            y = block(take_x(i))
            j = full0 + i
            ys_slot = jnp.bitwise_and(j, Y_SLOTS - 1)

            @pl.when(j >= Y_SLOTS)
            def _():
                y_copy(start, ys_slot).wait()

            _to_token_tiles(ybuf.at[ys_slot], y)
            y_copy(start + i * EXPERT_BLOCK, ys_slot).start()
            return carry

        lax.fori_loop(0, n_full, full_block, 0)

        @pl.when(tail > 0)
        def _():
            xs_slot = take_x(n_full)
            for pred, cp in tail_copies(start, ptail_ref[e]):
                @pl.when(pred)
                def _(cp=cp):
                    cp.wait()
            units = jnp.right_shift(tail + (TAIL_UNIT - 1), TAIL_UNIT.bit_length() - 1)
            for u in range(1, EXPERT_BLOCK // TAIL_UNIT + 1):
                @pl.when(units == u)
                def _(u=u):
                    rows = u * TAIL_UNIT
                    _to_token_tiles(ytail.at[pl.ds(0, rows * TOKEN_ROWS)], block(xs_slot, rows))
            for pred, cp in tail_copies(start + n_full * EXPERT_BLOCK, tail):
                @pl.when(pred)
                def _(cp=cp):
                    cp.start()

    @pl.when(last_step)
    def _():
        total_full = full0 + n_full
        for back in range(1, Y_SLOTS + 1):
            @pl.when(total_full >= back)
            def _(back=back):
                y_copy(0, jnp.bitwise_and(total_full - back, Y_SLOTS - 1)).wait()
        for pred, cp in tail_copies(0, ltail_ref[0]):
            @pl.when(pred)
            def _(cp=cp):
                cp.wait()


def _max_expert_blocks(n_rows):
    return n_rows // EXPERT_BLOCK + N_EXPERTS


def _expert_metadata(counts, n_rows):
    ids = jnp.arange(N_EXPERTS, dtype=I32)
    used = counts > 0
    starts = jnp.cumsum(counts) - counts
    n_blk = (counts + EXPERT_BLOCK - 1) // EXPERT_BLOCK
    n_full = counts // EXPERT_BLOCK
    tail = counts % EXPERT_BLOCK
    blk0 = jnp.cumsum(n_blk) - n_blk
    full0 = jnp.cumsum(n_full) - n_full
    next_used = lax.cummin(jnp.where(used, ids, N_EXPERTS), reverse=True)
    next_after = jnp.concatenate([next_used[1:], jnp.full((1,), N_EXPERTS, I32)])
    nxt = jnp.where(next_after < N_EXPERTS, next_after, -1)
    ordinal = jnp.cumsum(used.astype(I32)) - 1
    slot = ordinal % W_SLOTS
    first = jnp.logical_and(used, ordinal == 0)
    latest = lax.cummax(jnp.where(tail > 0, ids, -1))
    before = jnp.concatenate([jnp.full((1,), -1, I32), latest[:-1]])
    pick = lambda index, values: jnp.sum(jnp.where(index[:, None] == ids[None, :], values[None, :], 0), axis=1)
    nxt2 = jnp.where(nxt >= 0, pick(nxt, nxt + 1), 0) - 1
    ptail = pick(before, tail)
    ltail = pick(latest[-1:], tail)
    block_ends = jnp.cumsum(n_blk)
    g = jnp.arange(_max_expert_blocks(n_rows), dtype=I32)
    eg = jnp.sum((g[:, None] >= block_ends[None, :]).astype(I32), axis=1)
    blktok = g * EXPERT_BLOCK + pick(eg, starts - blk0 * EXPERT_BLOCK)
    return tuple(v.astype(I32) for v in (starts, counts, nxt, nxt2, slot, first, blk0, full0, ptail, ltail, blktok,
                                         block_ends[-1:]))


def _experts(xs, counts, w_gate, w_up, w_down):
    d = D_MODEL
    meta = _expert_metadata(counts, xs.shape[0] // TOKEN_ROWS - EXPERT_BLOCK)
    hbm = pl.BlockSpec(memory_space=pl.ANY)
    blk = (EXPERT_BLOCK * TOKEN_ROWS, LANES)
    grid_spec = pltpu.PrefetchScalarGridSpec(
        num_scalar_prefetch=len(meta),
        grid=(N_EXPERTS,),
        in_specs=[hbm, hbm, hbm, hbm],
        out_specs=hbm,
        scratch_shapes=[pltpu.VMEM((W_SLOTS, d, EXPERT_FF), F32), pltpu.VMEM((W_SLOTS, d, EXPERT_FF), F32),
                        pltpu.VMEM((W_SLOTS, EXPERT_FF, d), F32),
                        pltpu.VMEM((d, EXPERT_FF), BF16), pltpu.VMEM((d, EXPERT_FF), BF16),
                        pltpu.VMEM((EXPERT_FF, d), BF16),
                        pltpu.VMEM((X_SLOTS,) + blk, WORD), pltpu.VMEM((Y_SLOTS,) + blk, WORD), pltpu.VMEM(blk, WORD),
                        pltpu.SemaphoreType.DMA((W_SLOTS, 3)), pltpu.SemaphoreType.DMA((X_SLOTS,)),
                        pltpu.SemaphoreType.DMA((Y_SLOTS,)), pltpu.SemaphoreType.DMA(())],
    )
    return pl.pallas_call(
        _experts_kernel,
        grid_spec=grid_spec,
        out_shape=jax.ShapeDtypeStruct(xs.shape, WORD),
        compiler_params=_params(("arbitrary",)),
        name="experts",
    )(*meta, xs, w_gate, w_up, w_down)


SC_CORES = 2
SC_SUBCORES = 16
SC_WORKERS = SC_CORES * SC_SUBCORES
SC_CHUNK = 128
SC_RING = 4


def _sc_scatter_rows(planes, idx, n_out_rows):
    n_planes, t, _ = planes.shape
    ranges = SC_WORKERS // n_planes
    n_chunks = t // ranges // SC_CHUNK
    idx = idx.reshape(TOP_K, n_planes, ranges, n_chunks, SC_CHUNK)
    mesh = plsc.VectorSubcoreMesh(core_axis_name="c", subcore_axis_name="s", num_cores=SC_CORES,
                                  num_subcores=SC_SUBCORES)

    def body(planes_hbm, idx_hbm, out_hbm, idx_v, rows_v, lsem, ssem):
        wid = lax.axis_index("s") * SC_CORES + lax.axis_index("c")
        plane = wid % n_planes
        token0 = (wid // n_planes) * (n_chunks * SC_CHUNK)
        for k in range(TOP_K):
            pltpu.sync_copy(idx_hbm.at[k, plane, wid // n_planes], idx_v.at[k])

        def load(slot, c):
            return pltpu.make_async_copy(planes_hbm.at[plane, pl.ds(token0 + c * SC_CHUNK, SC_CHUNK)], rows_v.at[slot],
                                         lsem.at[slot])

        def scatter(slot, c, k):
            return pltpu.make_async_copy(rows_v.at[slot], out_hbm.at[idx_v.at[k, c]], ssem.at[slot])

        @pl.loop(0, n_chunks, step=SC_RING)
        def _(g):
            for slot in range(SC_RING):
                @pl.when(g > 0)
                def _(slot=slot):
                    for k in range(TOP_K):
                        scatter(slot, 0, k).wait()
                load(slot, g + slot).start()
            for slot in range(SC_RING):
                load(slot, g + slot).wait()
                for k in range(TOP_K):
                    scatter(slot, g + slot, k).start()

        for slot in range(SC_RING):
            for k in range(TOP_K):
                scatter(slot, 0, k).wait()

    return pl.kernel(
        body, mesh=mesh,
        out_type=jax.ShapeDtypeStruct((n_out_rows, LANES), planes.dtype),
        scratch_types=[pltpu.VMEM((TOP_K, n_chunks, SC_CHUNK), I32), pltpu.VMEM((SC_RING, SC_CHUNK, LANES), planes.dtype),
                       pltpu.SemaphoreType.DMA((SC_RING,)), pltpu.SemaphoreType.DMA((SC_RING,))],
        name="sc_scatter",
    )(planes, idx)


def _sc_gather_rows(table, idx):
    n_workers, n_chunks, chunk = idx.shape
    rows_per_worker = n_chunks * chunk
    mesh = plsc.VectorSubcoreMesh(core_axis_name="c", subcore_axis_name="s", num_cores=SC_CORES,
                                  num_subcores=SC_SUBCORES)

    def body(table_hbm, idx_hbm, out_hbm, idx_v, rows_v, gsem, wsem):
        wid = lax.axis_index("s") * SC_CORES + lax.axis_index("c")
        base = wid * rows_per_worker
        pltpu.sync_copy(idx_hbm.at[wid], idx_v)

        def write(slot, c):
            return pltpu.make_async_copy(rows_v.at[slot], out_hbm.at[pl.ds(base + c * chunk, chunk)], wsem.at[slot])

        def gather(slot, c):
            return pltpu.make_async_copy(table_hbm.at[idx_v.at[c]], rows_v.at[slot], gsem.at[slot])

        @pl.loop(0, n_chunks, step=SC_RING)
        def _(g):
            for slot in range(SC_RING):
                @pl.when(g > 0)
                def _(slot=slot):
                    write(slot, 0).wait()
                gather(slot, g + slot).start()
            for slot in range(SC_RING):
                gather(slot, g + slot).wait()
                write(slot, g + slot).start()

        for slot in range(SC_RING):
            write(slot, 0).wait()

    return pl.kernel(
        body, mesh=mesh,
        out_type=jax.ShapeDtypeStruct((n_workers * rows_per_worker, LANES), table.dtype),
        scratch_types=[pltpu.VMEM((n_chunks, chunk), I32), pltpu.VMEM((SC_RING, chunk, LANES), table.dtype),
                       pltpu.SemaphoreType.DMA((SC_RING,)), pltpu.SemaphoreType.DMA((SC_RING,))],
        name="sc_gather",
    )(table, idx)


def _combine_kernel(rows_ref, gates_ref, hp_ref, x1_ref, gatef_ref, wg_ref, wu_ref, wd_ref, out_ref):
    def unpack(ref, first):
        words = [ref[first + c] for c in range(TOKEN_ROWS)]
        lows = [lax.bitcast_convert_type(jnp.left_shift(w, 16), F32) for w in words]
        highs = [lax.bitcast_convert_type(jnp.bitwise_and(w, HIGH_HALF), F32) for w in words]
        return jnp.concatenate(lows + highs, axis=1)

    hb = unpack(hp_ref, 0).astype(BF16)
    shared = _dot((_silu(_dot(hb, wg_ref[...])) * _dot(hb, wu_ref[...])).astype(BF16), wd_ref[...])
    gates = gates_ref[...]
    routed = unpack(rows_ref, 0) * gates[:, 0:1]
    for k in range(1, TOP_K):
        routed = routed + unpack(rows_ref, k * TOKEN_ROWS) * gates[:, k:k + 1]
    out_ref[...] = x1_ref[...] + gatef_ref[...] * (shared + routed)


def _combine(gathered, gates_t, h2p, x1, gate_f, w_gate_s, w_up_s, w_down_s, b, s, tm):
    t, d = x1.shape
    tiles_per_seq = s // tm
    row = lambda n: pl.BlockSpec((tm, n), lambda i: (i, 0))
    full = lambda shp: pl.BlockSpec(shp, lambda i: (0,) * len(shp))
    return pl.pallas_call(
        _combine_kernel,
        grid=(t // tm,),
        in_specs=[pl.BlockSpec((TOP_K * TOKEN_ROWS, tm, LANES), lambda i: (0, i, 0)),
                  row(TOP_K), pl.BlockSpec((TOKEN_ROWS, tm, LANES), lambda i: (0, i, 0)), row(d),
                  pl.BlockSpec((None, 1, d), lambda i: (i // tiles_per_seq, 0, 0)),
                  full((d, EXPERT_FF)), full((d, EXPERT_FF)), full((EXPERT_FF, d))],
        out_specs=row(d),
        out_shape=jax.ShapeDtypeStruct((t, d), F32),
        compiler_params=_params(("arbitrary",)),
        name="combine",
    )(gathered, gates_t, h2p, x1, gate_f.reshape(b, 1, d),
      w_gate_s.astype(BF16), w_up_s.astype(BF16), w_down_s.astype(BF16))


def _moe_sublayer(x1, h2, h2t, gate_f, w_router, router_bias, w_gate, w_up, w_down, w_gate_s, w_up_s, w_down_s, b, s,
                  tm=256):
    t = b * s
    eidx, rank, gates, counts = _router(h2, w_router, router_bias)
    pos = _positions(counts, eidx, rank, 4 * tm)
    idx = pos[:, None, :] * TOKEN_ROWS + jnp.arange(TOKEN_ROWS, dtype=I32)[None, :, None]
    xs = _sc_scatter_rows(h2t, idx, (t * TOP_K + EXPERT_BLOCK) * TOKEN_ROWS)
    ys = _experts(xs, counts[:, 0], w_gate, w_up, w_down)
    gathered = _sc_gather_rows(ys, idx.reshape(SC_WORKERS, -1, SC_CHUNK)).reshape(TOP_K * TOKEN_ROWS, t, LANES)
    return _combine(gathered, gates.T, h2t, x1, gate_f, w_gate_s, w_up_s, w_down_s, b, s, 2 * tm)


def kernel(x, c, w_ada, b_ada, norm_mix_gain, w_in, q_norm_gain, k_norm_gain, rel_bias_table, conv_w, conv_b, dt_bias,
           a_log, d_skip, ssm_norm_gain, w_out, norm_ffn_gain, w_router, router_bias, w_gate_experts, w_up_experts,
           w_down_experts, w_gate_shared, w_up_shared, w_down_shared):
    b, s, d = x.shape
    for layer in range(w_ada.shape[0]):
        mod = _adaln(c, w_ada[layer], b_ada[layer])
        x1, h2, h2t = _mixer_sublayer(x, mod, norm_mix_gain[layer], w_in[layer], q_norm_gain[layer], k_norm_gain[layer],
                                 rel_bias_table, conv_w[layer], conv_b[layer], dt_bias[layer], a_log[layer],
                                 d_skip[layer], ssm_norm_gain[layer], w_out[layer], norm_ffn_gain[layer])
        gate_f = mod[:, 5 * d:]
        out = _moe_sublayer(x1, h2, h2t, gate_f, w_router[layer], router_bias[layer], w_gate_experts[layer],
                            w_up_experts[layer], w_down_experts[layer], w_gate_shared[layer], w_up_shared[layer],
                            w_down_shared[layer], b, s)
        x = out.reshape(b, s, d)
    return x
```

```python
import math

import numpy as np
import jax
import jax.numpy as jnp
from jax import lax
from jax.experimental import pallas as pl
from jax.experimental.pallas import tpu as pltpu
from jax.experimental.pallas import tpu_sc as plsc

F32 = jnp.float32
BF16 = jnp.bfloat16
I32 = jnp.int32

D_MODEL = 1024
ATTN_HEADS = 8
HEAD_DIM = 64
ATTN_WIDTH = ATTN_HEADS * HEAD_DIM
PATTERNS = ((128, 1), (512, 4), (2048, 16))
WIN_STEPS = 128
REL_BUCKETS = 32
REL_MAX_DISTANCE = 2048
SSM_HEADS = 24
SSM_HEAD_DIM = 64
SSM_WIDTH = SSM_HEADS * SSM_HEAD_DIM
SSM_GROUPS = 4
HEADS_PER_GROUP = SSM_HEADS // SSM_GROUPS
GROUP_WIDTH = SSM_WIDTH // SSM_GROUPS
SSM_STATE = 128
SSM_CONV = 4
SSM_CHUNK = 128
CONV_CH = SSM_WIDTH + 2 * SSM_GROUPS * SSM_STATE
N_EXPERTS = 256
TOP_K = 8
N_EXPERT_GROUPS = 8
EXPERTS_PER_GROUP = N_EXPERTS // N_EXPERT_GROUPS
TOPK_GROUPS = 4
EXPERT_FF = 256
ROUTED_SCALE = 2.5
NORM_EPS = 1e-6

LANES = 128
SUBLANES = 8
NEG_BIG = -1e30
VMEM_LIMIT = 56 * 1024 * 1024


def _params(sem, vmem=VMEM_LIMIT):
    return pltpu.CompilerParams(dimension_semantics=sem, vmem_limit_bytes=vmem)


def _sigmoid(x):
    return 1.0 / (1.0 + jnp.exp(-x))


def _silu(x):
    return x * _sigmoid(x)


def _split3(x):
    hi = x.astype(BF16)
    r = x - hi.astype(F32)
    mid = r.astype(BF16)
    lo = (r - mid.astype(F32)).astype(BF16)
    return hi, mid, lo


def _dot(a, b):
    return jnp.dot(a, b, preferred_element_type=F32)


def _dot_nt(a, b):
    return lax.dot_general(a, b, (((1,), (1,)), ((), ())), preferred_element_type=F32)


def _dot_exact_rhs(a, b_exact):
    hi, mid, lo = _split3(a)
    return _dot(hi, b_exact) + _dot(mid, b_exact) + _dot(lo, b_exact)


def _dot_exact_lhs(a_exact, b):
    hi, mid, lo = _split3(b)
    return _dot(a_exact, hi) + _dot(a_exact, mid) + _dot(a_exact, lo)


def _adaln_kernel(c_ref, w_ref, b_ref, o_ref):
    s = _silu(c_ref[...]).astype(BF16)
    o_ref[...] = _dot(s, w_ref[...].astype(BF16)) + b_ref[...]


def _adaln(c, w_ada, b_ada):
    b, d = c.shape
    n = w_ada.shape[1]
    rows = SUBLANES
    c_pad = jnp.zeros((rows, d), F32).at[:b].set(c)
    tn = 1024
    out = pl.pallas_call(
        _adaln_kernel,
        grid=(n // tn,),
        in_specs=[pl.BlockSpec((rows, d), lambda j: (0, 0)),
                  pl.BlockSpec((d, tn), lambda j: (0, j)),
                  pl.BlockSpec((1, tn), lambda j: (0, j))],
        out_specs=pl.BlockSpec((rows, tn), lambda j: (0, j)),
        out_shape=jax.ShapeDtypeStruct((rows, n), F32),
        compiler_params=_params(("arbitrary",)),
        name="adaln",
    )(c_pad, w_ada, b_ada.reshape(1, n))
    return out[:b]


def _inproj_kernel(x_ref, shift_ref, scale_ref, g_ref, wqkv_ref, wz_ref, wxbc_ref, wdt_ref,
                   qg_ref, kg_ref, hmean_ref, q_ref, k_ref, v_ref, z_ref, xbc_ref, dt_ref):
    x = x_ref[...]
    ms = jnp.mean(x * x, axis=-1, keepdims=True)
    h = x * lax.rsqrt(ms + NORM_EPS) * g_ref[...]
    h = h * (1.0 + scale_ref[...]) + shift_ref[...]
    hb = h.astype(BF16)

    hmean = hmean_ref[...]

    def head_norm(t, gain):
        sq = t * t
        hi = sq.astype(BF16)
        mid = (sq - hi.astype(F32)).astype(BF16)
        ss = _dot(hi, hmean) + _dot(mid, hmean)
        return t * lax.rsqrt(ss + NORM_EPS) * gain

    q = _dot(hb, wqkv_ref[:, 0:ATTN_WIDTH])
    q_ref[...] = head_norm(q, qg_ref[...]) * (HEAD_DIM ** -0.5)
    k = _dot(hb, wqkv_ref[:, ATTN_WIDTH:2 * ATTN_WIDTH])
    k_ref[...] = head_norm(k, kg_ref[...])
    v_ref[...] = _dot(hb, wqkv_ref[:, 2 * ATTN_WIDTH:3 * ATTN_WIDTH])
    for c0 in range(0, SSM_WIDTH, 512):
        z_ref[:, c0:c0 + 512] = _dot(hb, wz_ref[:, c0:c0 + 512])
    for c0 in range(0, CONV_CH, 512):
        xbc_ref[:, c0:c0 + 512] = _dot(hb, wxbc_ref[:, c0:c0 + 512])
    dt_ref[...] = _dot(hb, wdt_ref[...])


def _in_proj(x, shift, scale, gain, w_in, q_gain, k_gain, tm=256):
    b, s, d = x.shape
    t = b * s
    tiles_per_seq = s // tm
    w = w_in.astype(BF16)
    o_z = 3 * ATTN_WIDTH
    o_x = o_z + SSM_WIDTH
    o_dt = o_x + CONV_CH
    w_qkv, w_z, w_xbc = w[:, :o_z], w[:, o_z:o_x], w[:, o_x:o_dt]
    w_dt = jnp.zeros((d, LANES), BF16).at[:, :SSM_HEADS].set(w[:, o_dt:])
    head_of = np.arange(ATTN_WIDTH) // HEAD_DIM
    hmean = jnp.asarray((head_of[:, None] == head_of[None, :]).astype(np.float32) / HEAD_DIM, BF16)
    full = lambda shp: pl.BlockSpec(shp, lambda i: (0,) * len(shp))
    row = lambda n: pl.BlockSpec((tm, n), lambda i: (i, 0))
    per_batch = pl.BlockSpec((None, 1, d), lambda i: (i // tiles_per_seq, 0, 0))
    outs = pl.pallas_call(
        _inproj_kernel,
        grid=(t // tm,),
        in_specs=[row(d), per_batch, per_batch, full((1, d)),
                  full((d, o_z)), full((d, SSM_WIDTH)), full((d, CONV_CH)), full((d, LANES)),
                  full((1, ATTN_WIDTH)), full((1, ATTN_WIDTH)), full((ATTN_WIDTH, ATTN_WIDTH))],
        out_specs=[row(ATTN_WIDTH), row(ATTN_WIDTH), row(ATTN_WIDTH), row(SSM_WIDTH), row(CONV_CH), row(LANES)],
        out_shape=[jax.ShapeDtypeStruct((t, n), F32)
                   for n in (ATTN_WIDTH, ATTN_WIDTH, ATTN_WIDTH, SSM_WIDTH, CONV_CH, LANES)],
        compiler_params=_params(("arbitrary",)),
        name="in_proj",
    )(x.reshape(t, d), shift.reshape(b, 1, d), scale.reshape(b, 1, d), gain.reshape(1, d),
      w_qkv, w_z, w_xbc, w_dt,
      jnp.tile(q_gain, ATTN_HEADS).reshape(1, ATTN_WIDTH), jnp.tile(k_gain, ATTN_HEADS).reshape(1, ATTN_WIDTH), hmean)
    return outs


def _t5_causal_buckets(distance):
    n = np.maximum(distance, 0)
    max_exact = REL_BUCKETS // 2
    large = max_exact + (np.log(np.maximum(n, 1) / max_exact) / math.log(REL_MAX_DISTANCE / max_exact)
                         * (REL_BUCKETS - max_exact)).astype(np.int64)
    large = np.minimum(large, REL_BUCKETS - 1)
    return np.where(n < max_exact, n, large).astype(np.int32)


def _window_bias(rel_bias_table, dilation):
    qi = np.arange(WIN_STEPS)[:, None]
    kj = np.arange(2 * WIN_STEPS)[None, :]
    dist = qi + WIN_STEPS - kj
    band = (dist >= 0) & (dist <= WIN_STEPS)
    onehot = (_t5_causal_buckets(dist * dilation).reshape(-1, 1) == np.arange(REL_BUCKETS)[None, :]).astype(np.float32)
    bias = jnp.dot(rel_bias_table.astype(F32).T, jnp.asarray(onehot).T, precision=lax.Precision.HIGHEST)
    bias = bias.reshape(ATTN_HEADS, WIN_STEPS, 2 * WIN_STEPS)
    return jnp.where(jnp.asarray(band)[None], bias, NEG_BIG)


ATTN_TOKENS = max(w for w, _ in PATTERNS)
ATTN_UNROLL = 16


def _attn_kernel(q_ref, kp_ref, kc_ref, vp_ref, vc_ref, bias_ref, out_ref, kw, vw, o_acc, l_acc):
    tb = ATTN_TOKENS
    first = pl.program_id(2) == 0
    kw[0:tb] = kp_ref[...]
    kw[tb:2 * tb] = kc_ref[...]
    vw[0:tb] = vp_ref[...]
    vw[tb:2 * tb] = vc_ref[...]
    lane = lax.broadcasted_iota(I32, (WIN_STEPS, LANES), 1)
    head0 = lane < HEAD_DIM
    col = lax.broadcasted_iota(I32, (WIN_STEPS, 2 * WIN_STEPS), 1)
    in_prev = col < WIN_STEPS

    for p, (_, d) in enumerate(PATTERNS):
        shift = d.bit_length() - 1
        n_blocks = tb // WIN_STEPS

        def rows(start, n, d=d):
            return pl.ds(start, n, stride=d) if d > 1 else pl.ds(start, n)

        def body(it, carry, p=p, d=d, shift=shift, rows=rows):
            for u in range(ATTN_UNROLL):
                idx = it * ATTN_UNROLL + u
                r = jnp.bitwise_and(idx, d - 1)
                j = jnp.right_shift(idx, shift)
                qs = j * (WIN_STEPS * d) + r
                q = q_ref[rows(qs, WIN_STEPS), :]
                k = kw[rows(tb + qs - WIN_STEPS * d, 2 * WIN_STEPS), :].astype(BF16)
                v = vw[rows(tb + qs - WIN_STEPS * d, 2 * WIN_STEPS), :].astype(BF16)
                no_prev = jnp.logical_and(in_prev, jnp.logical_and(first, j == 0))
                o_h, lse_h = [], []
                for h in range(2):
                    qh = jnp.where(head0 if h == 0 else jnp.logical_not(head0), q, 0.0).astype(BF16)
                    s = _dot_nt(qh, k) + bias_ref[p, h]
                    s = jnp.where(no_prev, NEG_BIG, s)
                    m = jnp.max(s, axis=-1, keepdims=True)
                    e = jnp.exp(s - m)
                    denom = jnp.sum(e, axis=-1, keepdims=True)
                    o_h.append(_dot(e.astype(BF16), v) / denom)
                    lse_h.append(m + jnp.log(denom))
                o_acc[p, rows(qs, WIN_STEPS), :] = jnp.where(head0, o_h[0], o_h[1])
                l_acc[p, rows(qs, WIN_STEPS), :] = jnp.where(head0, lse_h[0], lse_h[1])
            return carry

        lax.fori_loop(0, n_blocks // ATTN_UNROLL, body, 0)

    chunk = 256
    for c0 in range(0, tb, chunk):
        l1, l2, l3 = (l_acc[p, c0:c0 + chunk, :] for p in range(3))
        m = jnp.maximum(jnp.maximum(l1, l2), l3)
        e1, e2, e3 = jnp.exp(l1 - m), jnp.exp(l2 - m), jnp.exp(l3 - m)
        num = e1 * o_acc[0, c0:c0 + chunk, :] + e2 * o_acc[1, c0:c0 + chunk, :] + e3 * o_acc[2, c0:c0 + chunk, :]
        out_ref[c0:c0 + chunk, :] = num / (e1 + e2 + e3)


def _attention(q, k, v, bias):
    b, s, w = q.shape
    tb = ATTN_TOKENS
    pairs = ATTN_HEADS // 2
    cur = pl.BlockSpec((None, tb, LANES), lambda bi, hp, i: (bi, i, hp))
    prev = pl.BlockSpec((None, tb, LANES), lambda bi, hp, i: (bi, jnp.maximum(i - 1, 0), hp))
    return pl.pallas_call(
        _attn_kernel,
        grid=(b, pairs, s // tb),
        in_specs=[cur, prev, cur, prev, cur,
                  pl.BlockSpec((len(PATTERNS), 2, WIN_STEPS, 2 * WIN_STEPS), lambda bi, hp, i: (0, hp, 0, 0))],
        out_specs=cur,
        out_shape=jax.ShapeDtypeStruct((b, s, w), F32),
        scratch_shapes=[pltpu.VMEM((2 * tb, LANES), F32), pltpu.VMEM((2 * tb, LANES), F32),
                        pltpu.VMEM((len(PATTERNS), tb, LANES), F32), pltpu.VMEM((len(PATTERNS), tb, LANES), F32)],
        compiler_params=_params(("arbitrary",) * 3),
        name="attention",
    )(q, k, k, v, v, bias)


SSD_CHUNKS_PER_STEP = 2


def _ssd_kernel(xbc_ref, halo_ref, z_ref, dtraw_ref, convw_ref, convb_ref, dtb_ref, alog_ref, dskip_ref, gain_ref,
                expand_ref, tril_ref, y_ref, state_ref):
    c = pl.program_id(1)

    @pl.when(c == 0)
    def _():
        state_ref[...] = jnp.zeros_like(state_ref)

    halo = jnp.where(c == 0, 0.0, halo_ref[...])
    for sub in range(SSD_CHUNKS_PER_STEP):
        rows = slice(sub * SSM_CHUNK, (sub + 1) * SSM_CHUNK)
        x = xbc_ref[rows, :]
        w = convw_ref[...]
        acc = x * w[SSM_CONV - 1:SSM_CONV, :] + convb_ref[...]
        row8 = lax.broadcasted_iota(I32, (SUBLANES, CONV_CH), 0)
        for shift in range(1, SSM_CONV):
            xs = pltpu.roll(x, shift, axis=0)
            hs = pltpu.roll(halo, shift, axis=0)
            head = jnp.where(row8 < shift, hs, xs[0:SUBLANES])
            xs = jnp.concatenate([head, xs[SUBLANES:]], axis=0)
            acc = acc + xs * w[SSM_CONV - 1 - shift:SSM_CONV - shift, :]
        act = _silu(acc)
        x_s = act[:, :SSM_WIDTH]
        bc0 = SSM_WIDTH
        cc0 = SSM_WIDTH + SSM_GROUPS * SSM_STATE

        t = dtraw_ref[rows, :] + dtb_ref[...]
        dt = jnp.maximum(t, 0.0) + jnp.log(1.0 + jnp.exp(-jnp.abs(t)))
        a = dt * (-jnp.exp(alog_ref[...]))
        a_cs = _dot_exact_lhs(tril_ref[...], a)
        a_cs_t = a_cs.T
        a_last = a_cs[SSM_CHUNK - 1:SSM_CHUNK, :]
        expand = expand_ref[...]
        dt_e = _dot_exact_rhs(dt, expand)
        ea_e = _dot_exact_rhs(jnp.exp(a_cs), expand)
        dte_e = _dot_exact_rhs(jnp.exp(a_last - a_cs), expand)
        xdt = x_s * dt_e
        xw = (xdt * dte_e).astype(BF16)
        xdt_b = xdt.astype(BF16)

        li = lax.broadcasted_iota(I32, (SSM_CHUNK, SSM_CHUNK), 0)
        si = lax.broadcasted_iota(I32, (SSM_CHUNK, SSM_CHUNK), 1)
        causal = li >= si

        ys = []
        for g in range(SSM_GROUPS):
            gs = slice(g * GROUP_WIDTH, (g + 1) * GROUP_WIDTH)
            b_g = act[:, bc0 + g * SSM_STATE:bc0 + (g + 1) * SSM_STATE]
            c_g = act[:, cc0 + g * SSM_STATE:cc0 + (g + 1) * SSM_STATE].astype(BF16)
            cb = _dot_nt(c_g, b_g.astype(BF16))
            state = state_ref[g]
            y_off = _dot(c_g, state.astype(BF16)) * ea_e[:, gs]
            parts = []
            for j in range(HEADS_PER_GROUP):
                hh = g * HEADS_PER_GROUP + j
                seg = a_cs[:, hh:hh + 1] - a_cs_t[hh:hh + 1, :]
                decay = jnp.exp(jnp.where(causal, seg, NEG_BIG))
                m = (cb * decay).astype(BF16)
                parts.append(_dot(m, xdt_b[:, hh * SSM_HEAD_DIM:(hh + 1) * SSM_HEAD_DIM]))
            ys.append(jnp.concatenate(parts, axis=1) + y_off)
            state_ref[g] = state * ea_e[SSM_CHUNK - 1:SSM_CHUNK, gs] + _dot(b_g.T.astype(BF16), xw[:, gs])
        y = jnp.concatenate(ys, axis=1) + dskip_ref[...] * x_s
        y = y * _silu(z_ref[rows, :])
        gain = gain_ref[...]
        for g in range(SSM_GROUPS):
            gs = slice(g * GROUP_WIDTH, (g + 1) * GROUP_WIDTH)
            yg = y[:, gs]
            ms = jnp.mean(yg * yg, axis=-1, keepdims=True)
            y_ref[rows, gs] = yg * lax.rsqrt(ms + NORM_EPS) * gain[:, gs]
        halo = x[SSM_CHUNK - SUBLANES:, :]


def _ssd(xbc, z, dt_raw, conv_w, conv_b, dt_bias, a_log, d_skip, norm_gain, b, s):
    t = b * s
    step_rows = SSD_CHUNKS_PER_STEP * SSM_CHUNK
    nc = s // step_rows
    pad_heads = lambda v: jnp.zeros((1, LANES), F32).at[0, :SSM_HEADS].set(v)
    head_of_lane = np.arange(SSM_WIDTH) // SSM_HEAD_DIM
    expand = jnp.asarray((np.arange(LANES)[:, None] == head_of_lane[None, :]).astype(np.float32), BF16)
    tril = jnp.asarray(np.tril(np.ones((SSM_CHUNK, SSM_CHUNK), np.float32)), BF16)
    halo_blocks = step_rows // SUBLANES
    chunk = lambda n: pl.BlockSpec((step_rows, n), lambda bi, c: (bi * nc + c, 0))
    full = lambda shp: pl.BlockSpec(shp, lambda bi, c: (0,) * len(shp))
    halo = pl.BlockSpec((SUBLANES, CONV_CH), lambda bi, c: (jnp.maximum((bi * nc + c) * halo_blocks - 1, 0), 0))
    return pl.pallas_call(
        _ssd_kernel,
        grid=(b, nc),
        in_specs=[chunk(CONV_CH), halo, chunk(SSM_WIDTH), chunk(LANES),
                  full((SSM_CONV, CONV_CH)), full((1, CONV_CH)), full((1, LANES)), full((1, LANES)),
                  full((1, SSM_WIDTH)), full((1, SSM_WIDTH)), full((LANES, SSM_WIDTH)), full((SSM_CHUNK, SSM_CHUNK))],
        out_specs=chunk(SSM_WIDTH),
        out_shape=jax.ShapeDtypeStruct((t, SSM_WIDTH), F32),
        scratch_shapes=[pltpu.VMEM((SSM_GROUPS, SSM_STATE, GROUP_WIDTH), F32)],
        compiler_params=_params(("arbitrary", "arbitrary")),
        name="ssd",
    )(xbc, xbc, z, dt_raw, conv_w, conv_b.reshape(1, CONV_CH), pad_heads(dt_bias), pad_heads(a_log),
      jnp.repeat(d_skip, SSM_HEAD_DIM).reshape(1, SSM_WIDTH), norm_gain.reshape(1, SSM_WIDTH), expand, tril)


WORD = jnp.int32
TOKEN_ROWS = D_MODEL // (2 * LANES)
HIGH_HALF = np.int32(-65536)


def _to_token_tiles(ref, x):
    n = x.shape[0]
    for c in range(TOKEN_ROWS):
        lo = lax.bitcast_convert_type(x[:, c * LANES:(c + 1) * LANES].astype(BF16).astype(F32), WORD)
        hi = lax.bitcast_convert_type(x[:, (c + TOKEN_ROWS) * LANES:(c + TOKEN_ROWS + 1) * LANES]
                                      .astype(BF16).astype(F32), WORD)
        ref[pl.ds(c, n, stride=TOKEN_ROWS), :] = jnp.bitwise_or(lax.shift_right_logical(lo, 16),
                                                                 jnp.bitwise_and(hi, HIGH_HALF))


def _from_token_tiles(ref, n, token0=0):
    lows, highs = [], []
    for c in range(TOKEN_ROWS):
        word = ref[pl.ds(token0 * TOKEN_ROWS + c, n, stride=TOKEN_ROWS), :]
        lows.append(lax.bitcast_convert_type(jnp.left_shift(word, 16), F32))
        highs.append(lax.bitcast_convert_type(jnp.bitwise_and(word, HIGH_HALF), F32))
    return jnp.concatenate(lows + highs, axis=1)


def _outproj_kernel(attn_ref, ssm_ref, x_ref, gate_ref, shift_ref, scale_ref, g_ref, wa_ref, ws_ref, x1_ref, h2_ref,
                    h2t_ref):
    mixed = _dot(attn_ref[...].astype(BF16), wa_ref[...]) + _dot(ssm_ref[...].astype(BF16), ws_ref[...])
    x1 = x_ref[...] + gate_ref[...] * mixed
    x1_ref[...] = x1
    ms = jnp.mean(x1 * x1, axis=-1, keepdims=True)
    h = x1 * lax.rsqrt(ms + NORM_EPS) * g_ref[...]
    h2 = h * (1.0 + scale_ref[...]) + shift_ref[...]
    h2_ref[...] = h2
    for c in range(TOKEN_ROWS):
        lo = lax.bitcast_convert_type(h2[:, c * LANES:(c + 1) * LANES].astype(BF16).astype(F32), WORD)
        hi = lax.bitcast_convert_type(h2[:, (c + TOKEN_ROWS) * LANES:(c + TOKEN_ROWS + 1) * LANES]
                                      .astype(BF16).astype(F32), WORD)
        h2t_ref[c] = jnp.bitwise_or(lax.shift_right_logical(lo, 16), jnp.bitwise_and(hi, HIGH_HALF))


def _out_proj(attn, ssm, x, gate, shift, scale, gain, w_out, b, s, tm=512):
    t = b * s
    d = D_MODEL
    tiles_per_seq = s // tm
    w = w_out.astype(BF16)
    row = lambda n: pl.BlockSpec((tm, n), lambda i: (i, 0))
    full = lambda shp: pl.BlockSpec(shp, lambda i: (0,) * len(shp))
    per_batch = pl.BlockSpec((None, 1, d), lambda i: (i // tiles_per_seq, 0, 0))
    return pl.pallas_call(
        _outproj_kernel,
        grid=(t // tm,),
        in_specs=[row(ATTN_WIDTH), row(SSM_WIDTH), row(d), per_batch, per_batch, per_batch, full((1, d)),
                  full((ATTN_WIDTH, d)), full((SSM_WIDTH, d))],
        out_specs=[row(d), row(d), pl.BlockSpec((TOKEN_ROWS, tm, LANES), lambda i: (0, i, 0))],
        out_shape=[jax.ShapeDtypeStruct((t, d), F32)] * 2 + [jax.ShapeDtypeStruct((TOKEN_ROWS, t, LANES), WORD)],
        compiler_params=_params(("arbitrary",)),
        name="out_proj",
    )(attn.reshape(t, ATTN_WIDTH), ssm, x.reshape(t, d),
      gate.reshape(b, 1, d), shift.reshape(b, 1, d), scale.reshape(b, 1, d), gain.reshape(1, d),
      w[:ATTN_WIDTH], w[ATTN_WIDTH:])


def _mixer_sublayer(x, mod, norm_mix_gain, w_in, q_norm_gain, k_norm_gain, rel_bias_table, conv_w, conv_b, dt_bias,
                    a_log, d_skip, ssm_norm_gain, w_out, norm_ffn_gain):
    b, s, d = x.shape
    shift_m, scale_m, gate_m, shift_f, scale_f, _ = jnp.split(mod, 6, axis=-1)
    q, k, v, z, xbc, dt_raw = _in_proj(x, shift_m, scale_m, norm_mix_gain, w_in, q_norm_gain, k_norm_gain)
    bias = jnp.stack([_window_bias(rel_bias_table, dilation) for _, dilation in PATTERNS])
    attn = _attention(q.reshape(b, s, ATTN_WIDTH), k.reshape(b, s, ATTN_WIDTH), v.reshape(b, s, ATTN_WIDTH), bias)
    ssm = _ssd(xbc, z, dt_raw, conv_w, conv_b, dt_bias, a_log, d_skip, ssm_norm_gain, b, s)
    return _out_proj(attn, ssm, x, gate_m, shift_f, scale_f, norm_ffn_gain, w_out, b, s)


def _first_argmax(v, iota, limit):
    m = jnp.max(v, axis=0, keepdims=True)
    idx = jnp.min(jnp.where(v == m, iota, limit), axis=0, keepdims=True)
    return m, idx


def _router_kernel(h_ref, wt_ref, bias_ref, upper_ref, eidx_ref, rank_ref, gate_ref, counts_ref, carry_ref):
    @pl.when(pl.program_id(0) == 0)
    def _():
        carry_ref[...] = jnp.zeros_like(carry_ref)

    tm = h_ref.shape[0]
    h = h_ref[...]
    wt = wt_ref[...]
    h_hi = h.astype(BF16)
    h_lo = (h - h_hi.astype(F32)).astype(BF16)
    w_hi = wt.astype(BF16)
    w_lo = (wt - w_hi.astype(F32)).astype(BF16)
    logits = _dot_nt(w_hi, h_hi) + _dot_nt(w_hi, h_lo) + _dot_nt(w_lo, h_hi)
    scores = _sigmoid(logits)
    choice = scores + bias_ref[...]
    neg_inf = -jnp.inf

    iota_g = lax.broadcasted_iota(I32, (EXPERTS_PER_GROUP, tm), 0).astype(F32)
    group_rows = []
    for g in range(N_EXPERT_GROUPS):
        v = choice[g * EXPERTS_PER_GROUP:(g + 1) * EXPERTS_PER_GROUP]
        m1, i1 = _first_argmax(v, iota_g, float(EXPERTS_PER_GROUP))
        m2 = jnp.max(jnp.where(iota_g == i1, neg_inf, v), axis=0, keepdims=True)
        group_rows.append(m1 + m2)
    group_scores = jnp.concatenate(group_rows, axis=0)

    iota_n = lax.broadcasted_iota(I32, (N_EXPERT_GROUPS, tm), 0).astype(F32)
    chosen = jnp.zeros((N_EXPERT_GROUPS, tm), F32)
    for _ in range(TOPK_GROUPS):
        _, gi = _first_argmax(group_scores, iota_n, float(N_EXPERT_GROUPS))
        hit = iota_n == gi
        chosen = jnp.where(hit, 1.0, chosen)
        group_scores = jnp.where(hit, neg_inf, group_scores)

    masked = jnp.concatenate(
        [jnp.where(chosen[g:g + 1] > 0.0, choice[g * EXPERTS_PER_GROUP:(g + 1) * EXPERTS_PER_GROUP], neg_inf)
         for g in range(N_EXPERT_GROUPS)], axis=0)

    iota_e = lax.broadcasted_iota(I32, (N_EXPERTS, tm), 0).astype(F32)
    picked, gates = [], []
    onehot = jnp.zeros((N_EXPERTS, tm), F32)
    for _ in range(TOP_K):
        _, ei = _first_argmax(masked, iota_e, float(N_EXPERTS))
        hit = iota_e == ei
        gates.append(jnp.sum(jnp.where(hit, scores, 0.0), axis=0, keepdims=True))
        masked = jnp.where(hit, neg_inf, masked)
        onehot = jnp.where(hit, 1.0, onehot)
        picked.append(ei)
    gate_sum = gates[0]
    for gk in gates[1:]:
        gate_sum = gate_sum + gk

    base = _dot(onehot.astype(BF16), upper_ref[...]) + carry_ref[...]
    ranks = [jnp.sum(jnp.where(iota_e == ei, base, 0.0), axis=0, keepdims=True) for ei in picked]
    carry_ref[...] = carry_ref[...] + jnp.sum(onehot, axis=1, keepdims=True)

    eidx_ref[...] = jnp.concatenate(picked, axis=0).astype(I32)
    rank_ref[...] = jnp.concatenate(ranks, axis=0).astype(I32)
    gate_ref[...] = jnp.concatenate([gk / gate_sum * ROUTED_SCALE for gk in gates], axis=0)
    counts_ref[...] = carry_ref[...].astype(I32)


def _router(h2, w_router, router_bias, tm=512):
    t, d = h2.shape
    upper = jnp.asarray(np.triu(np.ones((tm, tm), np.float32), 1), BF16)
    tok = pl.BlockSpec((TOP_K, tm), lambda i: (0, i))
    full = lambda shp: pl.BlockSpec(shp, lambda i: (0,) * len(shp))
    return pl.pallas_call(
        _router_kernel,
        grid=(t // tm,),
        in_specs=[pl.BlockSpec((tm, d), lambda i: (i, 0)), full((N_EXPERTS, d)), full((N_EXPERTS, 1)), full((tm, tm))],
        out_specs=[tok, tok, tok, full((N_EXPERTS, 1))],
        out_shape=[jax.ShapeDtypeStruct((TOP_K, t), I32), jax.ShapeDtypeStruct((TOP_K, t), I32),
                   jax.ShapeDtypeStruct((TOP_K, t), F32), jax.ShapeDtypeStruct((N_EXPERTS, 1), I32)],
        scratch_shapes=[pltpu.VMEM((N_EXPERTS, 1), F32)],
        compiler_params=_params(("arbitrary",)),
        name="router",
    )(h2, w_router.T, router_bias.reshape(N_EXPERTS, 1), upper)


def _positions_kernel(counts_ref, lower_ref, eidx_ref, rank_ref, pos_ref):
    tm = eidx_ref.shape[1]
    counts = jnp.broadcast_to(counts_ref[...].astype(F32), (N_EXPERTS, LANES))
    offsets = _dot_exact_lhs(lower_ref[...], counts)[:, 0:1]
    iota_e = lax.broadcasted_iota(I32, (N_EXPERTS, tm), 0).astype(F32)
    e = eidx_ref[...].astype(F32)
    rows = [jnp.sum(jnp.where(iota_e == e[k:k + 1], offsets, 0.0), axis=0, keepdims=True) for k in range(TOP_K)]
    pos_ref[...] = jnp.concatenate(rows, axis=0).astype(I32) + rank_ref[...]


def _positions(counts, eidx, rank, tm):
    t = eidx.shape[1]
    lower = jnp.asarray(np.tril(np.ones((N_EXPERTS, N_EXPERTS), np.float32), -1), BF16)
    tok = pl.BlockSpec((TOP_K, tm), lambda i: (0, i))
    return pl.pallas_call(
        _positions_kernel,
        grid=(t // tm,),
        in_specs=[pl.BlockSpec((N_EXPERTS, 1), lambda i: (0, 0)), pl.BlockSpec((N_EXPERTS, N_EXPERTS), lambda i: (0, 0)),
                  tok, tok],
        out_specs=tok,
        out_shape=jax.ShapeDtypeStruct((TOP_K, t), I32),
        compiler_params=_params(("arbitrary",)),
        name="positions",
    )(counts, lower, eidx, rank)


EXPERT_BLOCK = 512
TAIL_UNIT = 128
TAIL_PIECES = tuple(1 << i for i in reversed(range(EXPERT_BLOCK.bit_length() - 1)))
W_SLOTS = 3
X_SLOTS = 4
Y_SLOTS = 4


def _experts_kernel(start_ref, count_ref, nxt_ref, nxt2_ref, slot_ref, first_ref, blk0_ref, full0_ref, ptail_ref,
                    ltail_ref, blktok_ref, nblocks_ref, xs_hbm, wg_hbm, wu_hbm, wd_hbm, ys_hbm,
                    wg_buf, wu_buf, wd_buf, wg_bf, wu_bf, wd_bf, xbuf, ybuf, ytail, wsem, xsem, ysem, tsem):
    e = pl.program_id(0)
    last_step = e == pl.num_programs(0) - 1
    start, count = start_ref[e], count_ref[e]
    n_full = jnp.right_shift(count, EXPERT_BLOCK.bit_length() - 1)
    tail = jnp.bitwise_and(count, EXPERT_BLOCK - 1)
    n_blk = n_full + (tail > 0).astype(I32)
    blk0, full0 = blk0_ref[e], full0_ref[e]
    slot, nxt, nxt2 = slot_ref[e], nxt_ref[e], nxt2_ref[e]
    slot1 = jnp.where(slot + 1 >= W_SLOTS, slot + 1 - W_SLOTS, slot + 1)
    slot2 = jnp.where(slot + 2 >= W_SLOTS, slot + 2 - W_SLOTS, slot + 2)

    def token_rows(token, n):
        return pl.ds(pl.multiple_of(token * TOKEN_ROWS, TOKEN_ROWS), n * TOKEN_ROWS)

    def fetch(ex, s):
        return (pltpu.make_async_copy(wg_hbm.at[ex], wg_buf.at[s], wsem.at[s, 0]),
                pltpu.make_async_copy(wu_hbm.at[ex], wu_buf.at[s], wsem.at[s, 1]),
                pltpu.make_async_copy(wd_hbm.at[ex], wd_buf.at[s], wsem.at[s, 2]))

    def x_copy(token, s):
        return pltpu.make_async_copy(xs_hbm.at[token_rows(token, EXPERT_BLOCK)], xbuf.at[s], xsem.at[s])

    def y_copy(token, s):
        return pltpu.make_async_copy(ybuf.at[s], ys_hbm.at[token_rows(token, EXPERT_BLOCK)], ysem.at[s])

    def tail_copies(token, length):
        out = []
        for piece in TAIL_PIECES:
            bigger = (EXPERT_BLOCK - 1) & ~(2 * piece - 1)
            done = jnp.bitwise_and(length, bigger)
            cp = pltpu.make_async_copy(ytail.at[token_rows(done, piece)], ys_hbm.at[token_rows(token + done, piece)],
                                       tsem)
            out.append((jnp.bitwise_and(length, piece) != 0, cp))
        return out

    def block(s, rows=EXPERT_BLOCK):
        x = _from_token_tiles(xbuf.at[s], rows).astype(BF16)
        g = _dot(x, wg_bf[...])
        u = _dot(x, wu_bf[...])
        return _dot((_silu(g) * u).astype(BF16), wd_bf[...])

    @pl.when(count > 0)
    def _():
        @pl.when(first_ref[e] == 1)
        def _():
            for g in range(X_SLOTS - 1):
                @pl.when(g < nblocks_ref[0])
                def _(g=g):
                    x_copy(blktok_ref[g], g).start()
            for cp in fetch(e, slot):
                cp.start()

            @pl.when(nxt >= 0)
            def _():
                for cp in fetch(nxt, slot1):
                    cp.start()

        for cp in fetch(e, slot):
            cp.wait()

        @pl.when(nxt2 >= 0)
        def _():
            for cp in fetch(nxt2, slot2):
                cp.start()

        wg_bf[...] = wg_buf[slot].astype(BF16)
        wu_bf[...] = wu_buf[slot].astype(BF16)
        wd_bf[...] = wd_buf[slot].astype(BF16)

        def take_x(i):
            g = blk0 + i
            xs_slot = jnp.bitwise_and(g, X_SLOTS - 1)
            x_copy(start, xs_slot).wait()
            ahead = g + (X_SLOTS - 1)

            @pl.when(ahead < nblocks_ref[0])
            def _():
                x_copy(blktok_ref[ahead], jnp.bitwise_and(ahead, X_SLOTS - 1)).start()

            return xs_slot

        def full_block(i, carry):
            y = block(take_x(i))
            j = full0 + i
            ys_slot = jnp.bitwise_and(j, Y_SLOTS - 1)

            @pl.when(j >= Y_SLOTS)
            def _():
                y_copy(start, ys_slot).wait()

            _to_token_tiles(ybuf.at[ys_slot], y)
            y_copy(start + i * EXPERT_BLOCK, ys_slot).start()
            return carry

        lax.fori_loop(0, n_full, full_block, 0)

        @pl.when(tail > 0)
        def _():
            xs_slot = take_x(n_full)
            for pred, cp in tail_copies(start, ptail_ref[e]):
                @pl.when(pred)
                def _(cp=cp):
                    cp.wait()
            units = jnp.right_shift(tail + (TAIL_UNIT - 1), TAIL_UNIT.bit_length() - 1)
            for u in range(1, EXPERT_BLOCK // TAIL_UNIT + 1):
                @pl.when(units == u)
                def _(u=u):
                    rows = u * TAIL_UNIT
                    _to_token_tiles(ytail.at[pl.ds(0, rows * TOKEN_ROWS)], block(xs_slot, rows))
            for pred, cp in tail_copies(start + n_full * EXPERT_BLOCK, tail):
                @pl.when(pred)
                def _(cp=cp):
                    cp.start()

    @pl.when(last_step)
    def _():
        total_full = full0 + n_full
        for back in range(1, Y_SLOTS + 1):
            @pl.when(total_full >= back)
            def _(back=back):
                y_copy(0, jnp.bitwise_and(total_full - back, Y_SLOTS - 1)).wait()
        for pred, cp in tail_copies(0, ltail_ref[0]):
            @pl.when(pred)
            def _(cp=cp):
                cp.wait()


def _max_expert_blocks(n_rows):
    return n_rows // EXPERT_BLOCK + N_EXPERTS


def _expert_metadata(counts, n_rows):
    ids = jnp.arange(N_EXPERTS, dtype=I32)
    used = counts > 0
    starts = jnp.cumsum(counts) - counts
    n_blk = (counts + EXPERT_BLOCK - 1) // EXPERT_BLOCK
    n_full = counts // EXPERT_BLOCK
    tail = counts % EXPERT_BLOCK
    blk0 = jnp.cumsum(n_blk) - n_blk
    full0 = jnp.cumsum(n_full) - n_full
    next_used = lax.cummin(jnp.where(used, ids, N_EXPERTS), reverse=True)
    next_after = jnp.concatenate([next_used[1:], jnp.full((1,), N_EXPERTS, I32)])
    nxt = jnp.where(next_after < N_EXPERTS, next_after, -1)
    ordinal = jnp.cumsum(used.astype(I32)) - 1
    slot = ordinal % W_SLOTS
    first = jnp.logical_and(used, ordinal == 0)
    latest = lax.cummax(jnp.where(tail > 0, ids, -1))
    before = jnp.concatenate([jnp.full((1,), -1, I32), latest[:-1]])
    pick = lambda index, values: jnp.sum(jnp.where(index[:, None] == ids[None, :], values[None, :], 0), axis=1)
    nxt2 = jnp.where(nxt >= 0, pick(nxt, nxt + 1), 0) - 1
    ptail = pick(before, tail)
    ltail = pick(latest[-1:], tail)
    block_ends = jnp.cumsum(n_blk)
    g = jnp.arange(_max_expert_blocks(n_rows), dtype=I32)
    eg = jnp.sum((g[:, None] >= block_ends[None, :]).astype(I32), axis=1)
    blktok = g * EXPERT_BLOCK + pick(eg, starts - blk0 * EXPERT_BLOCK)
    return tuple(v.astype(I32) for v in (starts, counts, nxt, nxt2, slot, first, blk0, full0, ptail, ltail, blktok,
                                         block_ends[-1:]))


def _experts(xs, counts, w_gate, w_up, w_down):
    d = D_MODEL
    meta = _expert_metadata(counts, xs.shape[0] // TOKEN_ROWS - EXPERT_BLOCK)
    hbm = pl.BlockSpec(memory_space=pl.ANY)
    blk = (EXPERT_BLOCK * TOKEN_ROWS, LANES)
    grid_spec = pltpu.PrefetchScalarGridSpec(
        num_scalar_prefetch=len(meta),
        grid=(N_EXPERTS,),
        in_specs=[hbm, hbm, hbm, hbm],
        out_specs=hbm,
        scratch_shapes=[pltpu.VMEM((W_SLOTS, d, EXPERT_FF), F32), pltpu.VMEM((W_SLOTS, d, EXPERT_FF), F32),
                        pltpu.VMEM((W_SLOTS, EXPERT_FF, d), F32),
                        pltpu.VMEM((d, EXPERT_FF), BF16), pltpu.VMEM((d, EXPERT_FF), BF16),
                        pltpu.VMEM((EXPERT_FF, d), BF16),
                        pltpu.VMEM((X_SLOTS,) + blk, WORD), pltpu.VMEM((Y_SLOTS,) + blk, WORD), pltpu.VMEM(blk, WORD),
                        pltpu.SemaphoreType.DMA((W_SLOTS, 3)), pltpu.SemaphoreType.DMA((X_SLOTS,)),
                        pltpu.SemaphoreType.DMA((Y_SLOTS,)), pltpu.SemaphoreType.DMA(())],
    )
    return pl.pallas_call(
        _experts_kernel,
        grid_spec=grid_spec,
        out_shape=jax.ShapeDtypeStruct(xs.shape, WORD),
        compiler_params=_params(("arbitrary",)),
        name="experts",
    )(*meta, xs, w_gate, w_up, w_down)


SC_CORES = 2
SC_SUBCORES = 16
SC_WORKERS = SC_CORES * SC_SUBCORES
SC_CHUNK = 128
SC_RING = 4


def _sc_scatter_rows(planes, idx, n_out_rows):
    n_planes, t, _ = planes.shape
    ranges = SC_WORKERS // n_planes
    n_chunks = t // ranges // SC_CHUNK
    idx = idx.reshape(TOP_K, n_planes, ranges, n_chunks, SC_CHUNK)
    mesh = plsc.VectorSubcoreMesh(core_axis_name="c", subcore_axis_name="s", num_cores=SC_CORES,
                                  num_subcores=SC_SUBCORES)

    def body(planes_hbm, idx_hbm, out_hbm, idx_v, rows_v, lsem, ssem):
        wid = lax.axis_index("s") * SC_CORES + lax.axis_index("c")
        plane = wid % n_planes
        token0 = (wid // n_planes) * (n_chunks * SC_CHUNK)
        for k in range(TOP_K):
            pltpu.sync_copy(idx_hbm.at[k, plane, wid // n_planes], idx_v.at[k])

        def load(slot, c):
            return pltpu.make_async_copy(planes_hbm.at[plane, pl.ds(token0 + c * SC_CHUNK, SC_CHUNK)], rows_v.at[slot],
                                         lsem.at[slot])

        def scatter(slot, c, k):
            return pltpu.make_async_copy(rows_v.at[slot], out_hbm.at[idx_v.at[k, c]], ssem.at[slot])

        @pl.loop(0, n_chunks, step=SC_RING)
        def _(g):
            for slot in range(SC_RING):
                @pl.when(g > 0)
                def _(slot=slot):
                    for k in range(TOP_K):
                        scatter(slot, 0, k).wait()
                load(slot, g + slot).start()
            for slot in range(SC_RING):
                load(slot, g + slot).wait()
                for k in range(TOP_K):
                    scatter(slot, g + slot, k).start()

        for slot in range(SC_RING):
            for k in range(TOP_K):
                scatter(slot, 0, k).wait()

    return pl.kernel(
        body, mesh=mesh,
        out_type=jax.ShapeDtypeStruct((n_out_rows, LANES), planes.dtype),
        scratch_types=[pltpu.VMEM((TOP_K, n_chunks, SC_CHUNK), I32), pltpu.VMEM((SC_RING, SC_CHUNK, LANES), planes.dtype),
                       pltpu.SemaphoreType.DMA((SC_RING,)), pltpu.SemaphoreType.DMA((SC_RING,))],
        name="sc_scatter",
    )(planes, idx)


def _sc_gather_rows(table, idx):
    n_workers, n_chunks, chunk = idx.shape
    rows_per_worker = n_chunks * chunk
    mesh = plsc.VectorSubcoreMesh(core_axis_name="c", subcore_axis_name="s", num_cores=SC_CORES,
                                  num_subcores=SC_SUBCORES)

    def body(table_hbm, idx_hbm, out_hbm, idx_v, rows_v, gsem, wsem):
        wid = lax.axis_index("s") * SC_CORES + lax.axis_index("c")
        base = wid * rows_per_worker
        pltpu.sync_copy(idx_hbm.at[wid], idx_v)

        def write(slot, c):
            return pltpu.make_async_copy(rows_v.at[slot], out_hbm.at[pl.ds(base + c * chunk, chunk)], wsem.at[slot])

        def gather(slot, c):
            return pltpu.make_async_copy(table_hbm.at[idx_v.at[c]], rows_v.at[slot], gsem.at[slot])

        @pl.loop(0, n_chunks, step=SC_RING)
        def _(g):
            for slot in range(SC_RING):
                @pl.when(g > 0)
                def _(slot=slot):
                    write(slot, 0).wait()
                gather(slot, g + slot).start()
            for slot in range(SC_RING):
                gather(slot, g + slot).wait()
                write(slot, g + slot).start()

        for slot in range(SC_RING):
            write(slot, 0).wait()

    return pl.kernel(
        body, mesh=mesh,
        out_type=jax.ShapeDtypeStruct((n_workers * rows_per_worker, LANES), table.dtype),
        scratch_types=[pltpu.VMEM((n_chunks, chunk), I32), pltpu.VMEM((SC_RING, chunk, LANES), table.dtype),
                       pltpu.SemaphoreType.DMA((SC_RING,)), pltpu.SemaphoreType.DMA((SC_RING,))],
        name="sc_gather",
    )(table, idx)


def _combine_kernel(rows_ref, gates_ref, hp_ref, x1_ref, gatef_ref, wg_ref, wu_ref, wd_ref, out_ref):
    def unpack(ref, first):
        words = [ref[first + c] for c in range(TOKEN_ROWS)]
        lows = [lax.bitcast_convert_type(jnp.left_shift(w, 16), F32) for w in words]
        highs = [lax.bitcast_convert_type(jnp.bitwise_and(w, HIGH_HALF), F32) for w in words]
        return jnp.concatenate(lows + highs, axis=1)

    hb = unpack(hp_ref, 0).astype(BF16)
    shared = _dot((_silu(_dot(hb, wg_ref[...])) * _dot(hb, wu_ref[...])).astype(BF16), wd_ref[...])
    gates = gates_ref[...]
    routed = unpack(rows_ref, 0) * gates[:, 0:1]
    for k in range(1, TOP_K):
        routed = routed + unpack(rows_ref, k * TOKEN_ROWS) * gates[:, k:k + 1]
    out_ref[...] = x1_ref[...] + gatef_ref[...] * (shared + routed)


def _combine(gathered, gates_t, h2p, x1, gate_f, w_gate_s, w_up_s, w_down_s, b, s, tm):
    t, d = x1.shape
    tiles_per_seq = s // tm
    row = lambda n: pl.BlockSpec((tm, n), lambda i: (i, 0))
    full = lambda shp: pl.BlockSpec(shp, lambda i: (0,) * len(shp))
    return pl.pallas_call(
        _combine_kernel,
        grid=(t // tm,),
        in_specs=[pl.BlockSpec((TOP_K * TOKEN_ROWS, tm, LANES), lambda i: (0, i, 0)),
                  row(TOP_K), pl.BlockSpec((TOKEN_ROWS, tm, LANES), lambda i: (0, i, 0)), row(d),
                  pl.BlockSpec((None, 1, d), lambda i: (i // tiles_per_seq, 0, 0)),
                  full((d, EXPERT_FF)), full((d, EXPERT_FF)), full((EXPERT_FF, d))],
        out_specs=row(d),
        out_shape=jax.ShapeDtypeStruct((t, d), F32),
        compiler_params=_params(("arbitrary",)),
        name="combine",
    )(gathered, gates_t, h2p, x1, gate_f.reshape(b, 1, d),
      w_gate_s.astype(BF16), w_up_s.astype(BF16), w_down_s.astype(BF16))


def _moe_sublayer(x1, h2, h2t, gate_f, w_router, router_bias, w_gate, w_up, w_down, w_gate_s, w_up_s, w_down_s, b, s,
                  tm=256):
    t = b * s
    eidx, rank, gates, counts = _router(h2, w_router, router_bias)
    pos = _positions(counts, eidx, rank, 4 * tm)
    idx = pos[:, None, :] * TOKEN_ROWS + jnp.arange(TOKEN_ROWS, dtype=I32)[None, :, None]
    xs = _sc_scatter_rows(h2t, idx, (t * TOP_K + EXPERT_BLOCK) * TOKEN_ROWS)
    ys = _experts(xs, counts[:, 0], w_gate, w_up, w_down)
    gathered = _sc_gather_rows(ys, idx.reshape(SC_WORKERS, -1, SC_CHUNK)).reshape(TOP_K * TOKEN_ROWS, t, LANES)
    return _combine(gathered, gates.T, h2t, x1, gate_f, w_gate_s, w_up_s, w_down_s, b, s, 2 * tm)


def kernel(x, c, w_ada, b_ada, norm_mix_gain, w_in, q_norm_gain, k_norm_gain, rel_bias_table, conv_w, conv_b, dt_bias,
           a_log, d_skip, ssm_norm_gain, w_out, norm_ffn_gain, w_router, router_bias, w_gate_experts, w_up_experts,
           w_down_experts, w_gate_shared, w_up_shared, w_down_shared):
    b, s, d = x.shape
    for layer in range(w_ada.shape[0]):
        mod = _adaln(c, w_ada[layer], b_ada[layer])
        x1, h2, h2t = _mixer_sublayer(x, mod, norm_mix_gain[layer], w_in[layer], q_norm_gain[layer], k_norm_gain[layer],
                                 rel_bias_table, conv_w[layer], conv_b[layer], dt_bias[layer], a_log[layer],
                                 d_skip[layer], ssm_norm_gain[layer], w_out[layer], norm_ffn_gain[layer])
        gate_f = mod[:, 5 * d:]
        out = _moe_sublayer(x1, h2, h2t, gate_f, w_router[layer], router_bias[layer], w_gate_experts[layer],
                            w_up_experts[layer], w_down_experts[layer], w_gate_shared[layer], w_up_shared[layer],
                            w_down_shared[layer], b, s)
        x = out.reshape(b, s, d)
    return x
```

```python
import math

import numpy as np
import jax
import jax.numpy as jnp
from jax import lax
from jax.experimental import pallas as pl
from jax.experimental.pallas import tpu as pltpu
from jax.experimental.pallas import tpu_sc as plsc

F32 = jnp.float32
BF16 = jnp.bfloat16
I32 = jnp.int32

D_MODEL = 1024
ATTN_HEADS = 8
HEAD_DIM = 64
ATTN_WIDTH = ATTN_HEADS * HEAD_DIM
PATTERNS = ((128, 1), (512, 4), (2048, 16))
WIN_STEPS = 128
REL_BUCKETS = 32
REL_MAX_DISTANCE = 2048
SSM_HEADS = 24
SSM_HEAD_DIM = 64
SSM_WIDTH = SSM_HEADS * SSM_HEAD_DIM
SSM_GROUPS = 4
HEADS_PER_GROUP = SSM_HEADS // SSM_GROUPS
GROUP_WIDTH = SSM_WIDTH // SSM_GROUPS
SSM_STATE = 128
SSM_CONV = 4
SSM_CHUNK = 128
CONV_CH = SSM_WIDTH + 2 * SSM_GROUPS * SSM_STATE
N_EXPERTS = 256
TOP_K = 8
N_EXPERT_GROUPS = 8
EXPERTS_PER_GROUP = N_EXPERTS // N_EXPERT_GROUPS
TOPK_GROUPS = 4
EXPERT_FF = 256
ROUTED_SCALE = 2.5
NORM_EPS = 1e-6

LANES = 128
SUBLANES = 8
NEG_BIG = -1e30
VMEM_LIMIT = 56 * 1024 * 1024


def _params(sem, vmem=VMEM_LIMIT):
    return pltpu.CompilerParams(dimension_semantics=sem, vmem_limit_bytes=vmem)


def _sigmoid(x):
    return 1.0 / (1.0 + jnp.exp(-x))


def _silu(x):
    return x * _sigmoid(x)


def _split3(x):
    hi = x.astype(BF16)
    r = x - hi.astype(F32)
    mid = r.astype(BF16)
    lo = (r - mid.astype(F32)).astype(BF16)
    return hi, mid, lo


def _dot(a, b):
    return jnp.dot(a, b, preferred_element_type=F32)


def _dot_nt(a, b):
    return lax.dot_general(a, b, (((1,), (1,)), ((), ())), preferred_element_type=F32)


def _dot_exact_rhs(a, b_exact):
    hi, mid, lo = _split3(a)
    return _dot(hi, b_exact) + _dot(mid, b_exact) + _dot(lo, b_exact)


def _dot_exact_lhs(a_exact, b):
    hi, mid, lo = _split3(b)
    return _dot(a_exact, hi) + _dot(a_exact, mid) + _dot(a_exact, lo)


def _adaln_kernel(c_ref, w_ref, b_ref, o_ref):
    s = _silu(c_ref[...]).astype(BF16)
    o_ref[...] = _dot(s, w_ref[...].astype(BF16)) + b_ref[...]


def _adaln(c, w_ada, b_ada):
    b, d = c.shape
    n = w_ada.shape[1]
    rows = SUBLANES
    c_pad = jnp.zeros((rows, d), F32).at[:b].set(c)
    tn = 1024
    out = pl.pallas_call(
        _adaln_kernel,
        grid=(n // tn,),
        in_specs=[pl.BlockSpec((rows, d), lambda j: (0, 0)),
                  pl.BlockSpec((d, tn), lambda j: (0, j)),
                  pl.BlockSpec((1, tn), lambda j: (0, j))],
        out_specs=pl.BlockSpec((rows, tn), lambda j: (0, j)),
        out_shape=jax.ShapeDtypeStruct((rows, n), F32),
        compiler_params=_params(("arbitrary",)),
        name="adaln",
    )(c_pad, w_ada, b_ada.reshape(1, n))
    return out[:b]


def _inproj_kernel(x_ref, shift_ref, scale_ref, g_ref, wqkv_ref, wz_ref, wxbc_ref, wdt_ref,
                   qg_ref, kg_ref, hmean_ref, q_ref, k_ref, v_ref, z_ref, xbc_ref, dt_ref):
    x = x_ref[...]
    ms = jnp.mean(x * x, axis=-1, keepdims=True)
    h = x * lax.rsqrt(ms + NORM_EPS) * g_ref[...]
    h = h * (1.0 + scale_ref[...]) + shift_ref[...]
    hb = h.astype(BF16)

    hmean = hmean_ref[...]

    def head_norm(t, gain):
        sq = t * t
        hi = sq.astype(BF16)
        mid = (sq - hi.astype(F32)).astype(BF16)
        ss = _dot(hi, hmean) + _dot(mid, hmean)
        return t * lax.rsqrt(ss + NORM_EPS) * gain

    q = _dot(hb, wqkv_ref[:, 0:ATTN_WIDTH])
    q_ref[...] = head_norm(q, qg_ref[...]) * (HEAD_DIM ** -0.5)
    k = _dot(hb, wqkv_ref[:, ATTN_WIDTH:2 * ATTN_WIDTH])
    k_ref[...] = head_norm(k, kg_ref[...])
    v_ref[...] = _dot(hb, wqkv_ref[:, 2 * ATTN_WIDTH:3 * ATTN_WIDTH])
    for c0 in range(0, SSM_WIDTH, 512):
        z_ref[:, c0:c0 + 512] = _dot(hb, wz_ref[:, c0:c0 + 512])
    for c0 in range(0, CONV_CH, 512):
        xbc_ref[:, c0:c0 + 512] = _dot(hb, wxbc_ref[:, c0:c0 + 512])
    dt_ref[...] = _dot(hb, wdt_ref[...])


def _in_proj(x, shift, scale, gain, w_in, q_gain, k_gain, tm=256):
    b, s, d = x.shape
    t = b * s
    tiles_per_seq = s // tm
    w = w_in.astype(BF16)
    o_z = 3 * ATTN_WIDTH
    o_x = o_z + SSM_WIDTH
    o_dt = o_x + CONV_CH
    w_qkv, w_z, w_xbc = w[:, :o_z], w[:, o_z:o_x], w[:, o_x:o_dt]
    w_dt = jnp.zeros((d, LANES), BF16).at[:, :SSM_HEADS].set(w[:, o_dt:])
    head_of = np.arange(ATTN_WIDTH) // HEAD_DIM
    hmean = jnp.asarray((head_of[:, None] == head_of[None, :]).astype(np.float32) / HEAD_DIM, BF16)
    full = lambda shp: pl.BlockSpec(shp, lambda i: (0,) * len(shp))
    row = lambda n: pl.BlockSpec((tm, n), lambda i: (i, 0))
    per_batch = pl.BlockSpec((None, 1, d), lambda i: (i // tiles_per_seq, 0, 0))
    outs = pl.pallas_call(
        _inproj_kernel,
        grid=(t // tm,),
        in_specs=[row(d), per_batch, per_batch, full((1, d)),
                  full((d, o_z)), full((d, SSM_WIDTH)), full((d, CONV_CH)), full((d, LANES)),
                  full((1, ATTN_WIDTH)), full((1, ATTN_WIDTH)), full((ATTN_WIDTH, ATTN_WIDTH))],
        out_specs=[row(ATTN_WIDTH), row(ATTN_WIDTH), row(ATTN_WIDTH), row(SSM_WIDTH), row(CONV_CH), row(LANES)],
        out_shape=[jax.ShapeDtypeStruct((t, n), F32)
                   for n in (ATTN_WIDTH, ATTN_WIDTH, ATTN_WIDTH, SSM_WIDTH, CONV_CH, LANES)],
        compiler_params=_params(("arbitrary",)),
        name="in_proj",
    )(x.reshape(t, d), shift.reshape(b, 1, d), scale.reshape(b, 1, d), gain.reshape(1, d),
      w_qkv, w_z, w_xbc, w_dt,
      jnp.tile(q_gain, ATTN_HEADS).reshape(1, ATTN_WIDTH), jnp.tile(k_gain, ATTN_HEADS).reshape(1, ATTN_WIDTH), hmean)
    return outs


def _t5_causal_buckets(distance):
    n = np.maximum(distance, 0)
    max_exact = REL_BUCKETS // 2
    large = max_exact + (np.log(np.maximum(n, 1) / max_exact) / math.log(REL_MAX_DISTANCE / max_exact)
                         * (REL_BUCKETS - max_exact)).astype(np.int64)
    large = np.minimum(large, REL_BUCKETS - 1)
    return np.where(n < max_exact, n, large).astype(np.int32)


def _window_bias(rel_bias_table, dilation):
    qi = np.arange(WIN_STEPS)[:, None]
    kj = np.arange(2 * WIN_STEPS)[None, :]
    dist = qi + WIN_STEPS - kj
    band = (dist >= 0) & (dist <= WIN_STEPS)
    onehot = (_t5_causal_buckets(dist * dilation).reshape(-1, 1) == np.arange(REL_BUCKETS)[None, :]).astype(np.float32)
    bias = jnp.dot(rel_bias_table.astype(F32).T, jnp.asarray(onehot).T, precision=lax.Precision.HIGHEST)
    bias = bias.reshape(ATTN_HEADS, WIN_STEPS, 2 * WIN_STEPS)
    return jnp.where(jnp.asarray(band)[None], bias, NEG_BIG)


ATTN_TOKENS = max(w for w, _ in PATTERNS)
ATTN_UNROLL = 16


def _attn_kernel(q_ref, kp_ref, kc_ref, vp_ref, vc_ref, bias_ref, out_ref, kw, vw, o_acc, l_acc):
    tb = ATTN_TOKENS
    first = pl.program_id(2) == 0
    kw[0:tb] = kp_ref[...]
    kw[tb:2 * tb] = kc_ref[...]
    vw[0:tb] = vp_ref[...]
    vw[tb:2 * tb] = vc_ref[...]
    lane = lax.broadcasted_iota(I32, (WIN_STEPS, LANES), 1)
    head0 = lane < HEAD_DIM
    col = lax.broadcasted_iota(I32, (WIN_STEPS, 2 * WIN_STEPS), 1)
    in_prev = col < WIN_STEPS

    for p, (_, d) in enumerate(PATTERNS):
        shift = d.bit_length() - 1
        n_blocks = tb // WIN_STEPS

        def rows(start, n, d=d):
            return pl.ds(start, n, stride=d) if d > 1 else pl.ds(start, n)

        def body(it, carry, p=p, d=d, shift=shift, rows=rows):
            for u in range(ATTN_UNROLL):
                idx = it * ATTN_UNROLL + u
                r = jnp.bitwise_and(idx, d - 1)
                j = jnp.right_shift(idx, shift)
                qs = j * (WIN_STEPS * d) + r
                q = q_ref[rows(qs, WIN_STEPS), :]
                k = kw[rows(tb + qs - WIN_STEPS * d, 2 * WIN_STEPS), :].astype(BF16)
                v = vw[rows(tb + qs - WIN_STEPS * d, 2 * WIN_STEPS), :].astype(BF16)
                no_prev = jnp.logical_and(in_prev, jnp.logical_and(first, j == 0))
                o_h, lse_h = [], []
                for h in range(2):
                    qh = jnp.where(head0 if h == 0 else jnp.logical_not(head0), q, 0.0).astype(BF16)
                    s = _dot_nt(qh, k) + bias_ref[p, h]
                    s = jnp.where(no_prev, NEG_BIG, s)
                    m = jnp.max(s, axis=-1, keepdims=True)
                    e = jnp.exp(s - m)
                    denom = jnp.sum(e, axis=-1, keepdims=True)
                    o_h.append(_dot(e.astype(BF16), v) / denom)
                    lse_h.append(m + jnp.log(denom))
                o_acc[p, rows(qs, WIN_STEPS), :] = jnp.where(head0, o_h[0], o_h[1])
                l_acc[p, rows(qs, WIN_STEPS), :] = jnp.where(head0, lse_h[0], lse_h[1])
            return carry

        lax.fori_loop(0, n_blocks // ATTN_UNROLL, body, 0)

    chunk = 256
    for c0 in range(0, tb, chunk):
        l1, l2, l3 = (l_acc[p, c0:c0 + chunk, :] for p in range(3))
        m = jnp.maximum(jnp.maximum(l1, l2), l3)
        e1, e2, e3 = jnp.exp(l1 - m), jnp.exp(l2 - m), jnp.exp(l3 - m)
        num = e1 * o_acc[0, c0:c0 + chunk, :] + e2 * o_acc[1, c0:c0 + chunk, :] + e3 * o_acc[2, c0:c0 + chunk, :]
        out_ref[c0:c0 + chunk, :] = num / (e1 + e2 + e3)


def _attention(q, k, v, bias):
    b, s, w = q.shape
    tb = ATTN_TOKENS
    pairs = ATTN_HEADS // 2
    cur = pl.BlockSpec((None, tb, LANES), lambda bi, hp, i: (bi, i, hp))
    prev = pl.BlockSpec((None, tb, LANES), lambda bi, hp, i: (bi, jnp.maximum(i - 1, 0), hp))
    return pl.pallas_call(
        _attn_kernel,
        grid=(b, pairs, s // tb),
        in_specs=[cur, prev, cur, prev, cur,
                  pl.BlockSpec((len(PATTERNS), 2, WIN_STEPS, 2 * WIN_STEPS), lambda bi, hp, i: (0, hp, 0, 0))],
        out_specs=cur,
        out_shape=jax.ShapeDtypeStruct((b, s, w), F32),
        scratch_shapes=[pltpu.VMEM((2 * tb, LANES), F32), pltpu.VMEM((2 * tb, LANES), F32),
                        pltpu.VMEM((len(PATTERNS), tb, LANES), F32), pltpu.VMEM((len(PATTERNS), tb, LANES), F32)],
        compiler_params=_params(("arbitrary",) * 3),
        name="attention",
    )(q, k, k, v, v, bias)


SSD_CHUNKS_PER_STEP = 2


def _ssd_kernel(xbc_ref, halo_ref, z_ref, dtraw_ref, convw_ref, convb_ref, dtb_ref, alog_ref, dskip_ref, gain_ref,
                expand_ref, tril_ref, y_ref, state_ref):
    c = pl.program_id(1)

    @pl.when(c == 0)
    def _():
        state_ref[...] = jnp.zeros_like(state_ref)

    halo = jnp.where(c == 0, 0.0, halo_ref[...])
    for sub in range(SSD_CHUNKS_PER_STEP):
        rows = slice(sub * SSM_CHUNK, (sub + 1) * SSM_CHUNK)
        x = xbc_ref[rows, :]
        w = convw_ref[...]
        acc = x * w[SSM_CONV - 1:SSM_CONV, :] + convb_ref[...]
        row8 = lax.broadcasted_iota(I32, (SUBLANES, CONV_CH), 0)
        for shift in range(1, SSM_CONV):
            xs = pltpu.roll(x, shift, axis=0)
            hs = pltpu.roll(halo, shift, axis=0)
            head = jnp.where(row8 < shift, hs, xs[0:SUBLANES])
            xs = jnp.concatenate([head, xs[SUBLANES:]], axis=0)
            acc = acc + xs * w[SSM_CONV - 1 - shift:SSM_CONV - shift, :]
        act = _silu(acc)
        x_s = act[:, :SSM_WIDTH]
        bc0 = SSM_WIDTH
        cc0 = SSM_WIDTH + SSM_GROUPS * SSM_STATE

        t = dtraw_ref[rows, :] + dtb_ref[...]
        dt = jnp.maximum(t, 0.0) + jnp.log(1.0 + jnp.exp(-jnp.abs(t)))
        a = dt * (-jnp.exp(alog_ref[...]))
        a_cs = _dot_exact_lhs(tril_ref[...], a)
        a_cs_t = a_cs.T
        a_last = a_cs[SSM_CHUNK - 1:SSM_CHUNK, :]
        expand = expand_ref[...]
        dt_e = _dot_exact_rhs(dt, expand)
        ea_e = _dot_exact_rhs(jnp.exp(a_cs), expand)
        dte_e = _dot_exact_rhs(jnp.exp(a_last - a_cs), expand)
        xdt = x_s * dt_e
        xw = (xdt * dte_e).astype(BF16)
        xdt_b = xdt.astype(BF16)

        li = lax.broadcasted_iota(I32, (SSM_CHUNK, SSM_CHUNK), 0)
        si = lax.broadcasted_iota(I32, (SSM_CHUNK, SSM_CHUNK), 1)
        causal = li >= si

        ys = []
        for g in range(SSM_GROUPS):
            gs = slice(g * GROUP_WIDTH, (g + 1) * GROUP_WIDTH)
            b_g = act[:, bc0 + g * SSM_STATE:bc0 + (g + 1) * SSM_STATE]
            c_g = act[:, cc0 + g * SSM_STATE:cc0 + (g + 1) * SSM_STATE].astype(BF16)
            cb = _dot_nt(c_g, b_g.astype(BF16))
            state = state_ref[g]
            y_off = _dot(c_g, state.astype(BF16)) * ea_e[:, gs]
            parts = []
            for j in range(HEADS_PER_GROUP):
                hh = g * HEADS_PER_GROUP + j
                seg = a_cs[:, hh:hh + 1] - a_cs_t[hh:hh + 1, :]
                decay = jnp.exp(jnp.where(causal, seg, NEG_BIG))
                m = (cb * decay).astype(BF16)
                parts.append(_dot(m, xdt_b[:, hh * SSM_HEAD_DIM:(hh + 1) * SSM_HEAD_DIM]))
            ys.append(jnp.concatenate(parts, axis=1) + y_off)
            state_ref[g] = state * ea_e[SSM_CHUNK - 1:SSM_CHUNK, gs] + _dot(b_g.T.astype(BF16), xw[:, gs])
        y = jnp.concatenate(ys, axis=1) + dskip_ref[...] * x_s
        y = y * _silu(z_ref[rows, :])
        gain = gain_ref[...]
        for g in range(SSM_GROUPS):
            gs = slice(g * GROUP_WIDTH, (g + 1) * GROUP_WIDTH)
            yg = y[:, gs]
            ms = jnp.mean(yg * yg, axis=-1, keepdims=True)
            y_ref[rows, gs] = yg * lax.rsqrt(ms + NORM_EPS) * gain[:, gs]
        halo = x[SSM_CHUNK - SUBLANES:, :]


def _ssd(xbc, z, dt_raw, conv_w, conv_b, dt_bias, a_log, d_skip, norm_gain, b, s):
    t = b * s
    step_rows = SSD_CHUNKS_PER_STEP * SSM_CHUNK
    nc = s // step_rows
    pad_heads = lambda v: jnp.zeros((1, LANES), F32).at[0, :SSM_HEADS].set(v)
    head_of_lane = np.arange(SSM_WIDTH) // SSM_HEAD_DIM
    expand = jnp.asarray((np.arange(LANES)[:, None] == head_of_lane[None, :]).astype(np.float32), BF16)
    tril = jnp.asarray(np.tril(np.ones((SSM_CHUNK, SSM_CHUNK), np.float32)), BF16)
    halo_blocks = step_rows // SUBLANES
    chunk = lambda n: pl.BlockSpec((step_rows, n), lambda bi, c: (bi * nc + c, 0))
    full = lambda shp: pl.BlockSpec(shp, lambda bi, c: (0,) * len(shp))
    halo = pl.BlockSpec((SUBLANES, CONV_CH), lambda bi, c: (jnp.maximum((bi * nc + c) * halo_blocks - 1, 0), 0))
    return pl.pallas_call(
        _ssd_kernel,
        grid=(b, nc),
        in_specs=[chunk(CONV_CH), halo, chunk(SSM_WIDTH), chunk(LANES),
                  full((SSM_CONV, CONV_CH)), full((1, CONV_CH)), full((1, LANES)), full((1, LANES)),
                  full((1, SSM_WIDTH)), full((1, SSM_WIDTH)), full((LANES, SSM_WIDTH)), full((SSM_CHUNK, SSM_CHUNK))],
        out_specs=chunk(SSM_WIDTH),
        out_shape=jax.ShapeDtypeStruct((t, SSM_WIDTH), F32),
        scratch_shapes=[pltpu.VMEM((SSM_GROUPS, SSM_STATE, GROUP_WIDTH), F32)],
        compiler_params=_params(("arbitrary", "arbitrary")),
        name="ssd",
    )(xbc, xbc, z, dt_raw, conv_w, conv_b.reshape(1, CONV_CH), pad_heads(dt_bias), pad_heads(a_log),
      jnp.repeat(d_skip, SSM_HEAD_DIM).reshape(1, SSM_WIDTH), norm_gain.reshape(1, SSM_WIDTH), expand, tril)


WORD = jnp.int32
TOKEN_ROWS = D_MODEL // (2 * LANES)
HIGH_HALF = np.int32(-65536)


def _to_token_tiles(ref, x):
    n = x.shape[0]
    for c in range(TOKEN_ROWS):
        lo = lax.bitcast_convert_type(x[:, c * LANES:(c + 1) * LANES].astype(BF16).astype(F32), WORD)
        hi = lax.bitcast_convert_type(x[:, (c + TOKEN_ROWS) * LANES:(c + TOKEN_ROWS + 1) * LANES]
                                      .astype(BF16).astype(F32), WORD)
        ref[pl.ds(c, n, stride=TOKEN_ROWS), :] = jnp.bitwise_or(lax.shift_right_logical(lo, 16),
                                                                 jnp.bitwise_and(hi, HIGH_HALF))


def _from_token_tiles(ref, n, token0=0):
    lows, highs = [], []
    for c in range(TOKEN_ROWS):
        word = ref[pl.ds(token0 * TOKEN_ROWS + c, n, stride=TOKEN_ROWS), :]
        lows.append(lax.bitcast_convert_type(jnp.left_shift(word, 16), F32))
        highs.append(lax.bitcast_convert_type(jnp.bitwise_and(word, HIGH_HALF), F32))
    return jnp.concatenate(lows + highs, axis=1)


def _outproj_kernel(attn_ref, ssm_ref, x_ref, gate_ref, shift_ref, scale_ref, g_ref, wa_ref, ws_ref, x1_ref, h2_ref,
                    h2t_ref):
    mixed = _dot(attn_ref[...].astype(BF16), wa_ref[...]) + _dot(ssm_ref[...].astype(BF16), ws_ref[...])
    x1 = x_ref[...] + gate_ref[...] * mixed
    x1_ref[...] = x1
    ms = jnp.mean(x1 * x1, axis=-1, keepdims=True)
    h = x1 * lax.rsqrt(ms + NORM_EPS) * g_ref[...]
    h2 = h * (1.0 + scale_ref[...]) + shift_ref[...]
    h2_ref[...] = h2
    for c in range(TOKEN_ROWS):
        lo = lax.bitcast_convert_type(h2[:, c * LANES:(c + 1) * LANES].astype(BF16).astype(F32), WORD)
        hi = lax.bitcast_convert_type(h2[:, (c + TOKEN_ROWS) * LANES:(c + TOKEN_ROWS + 1) * LANES]
                                      .astype(BF16).astype(F32), WORD)
        h2t_ref[c] = jnp.bitwise_or(lax.shift_right_logical(lo, 16), jnp.bitwise_and(hi, HIGH_HALF))


def _out_proj(attn, ssm, x, gate, shift, scale, gain, w_out, b, s, tm=512):
    t = b * s
    d = D_MODEL
    tiles_per_seq = s // tm
    w = w_out.astype(BF16)
    row = lambda n: pl.BlockSpec((tm, n), lambda i: (i, 0))
    full = lambda shp: pl.BlockSpec(shp, lambda i: (0,) * len(shp))
    per_batch = pl.BlockSpec((None, 1, d), lambda i: (i // tiles_per_seq, 0, 0))
    return pl.pallas_call(
        _outproj_kernel,
        grid=(t // tm,),
        in_specs=[row(ATTN_WIDTH), row(SSM_WIDTH), row(d), per_batch, per_batch, per_batch, full((1, d)),
                  full((ATTN_WIDTH, d)), full((SSM_WIDTH, d))],
        out_specs=[row(d), row(d), pl.BlockSpec((TOKEN_ROWS, tm, LANES), lambda i: (0, i, 0))],
        out_shape=[jax.ShapeDtypeStruct((t, d), F32)] * 2 + [jax.ShapeDtypeStruct((TOKEN_ROWS, t, LANES), WORD)],
        compiler_params=_params(("arbitrary",)),
        name="out_proj",
    )(attn.reshape(t, ATTN_WIDTH), ssm, x.reshape(t, d),
      gate.reshape(b, 1, d), shift.reshape(b, 1, d), scale.reshape(b, 1, d), gain.reshape(1, d),
      w[:ATTN_WIDTH], w[ATTN_WIDTH:])


def _mixer_sublayer(x, mod, norm_mix_gain, w_in, q_norm_gain, k_norm_gain, rel_bias_table, conv_w, conv_b, dt_bias,
                    a_log, d_skip, ssm_norm_gain, w_out, norm_ffn_gain):
    b, s, d = x.shape
    shift_m, scale_m, gate_m, shift_f, scale_f, _ = jnp.split(mod, 6, axis=-1)
    q, k, v, z, xbc, dt_raw = _in_proj(x, shift_m, scale_m, norm_mix_gain, w_in, q_norm_gain, k_norm_gain)
    bias = jnp.stack([_window_bias(rel_bias_table, dilation) for _, dilation in PATTERNS])
    attn = _attention(q.reshape(b, s, ATTN_WIDTH), k.reshape(b, s, ATTN_WIDTH), v.reshape(b, s, ATTN_WIDTH), bias)
    ssm = _ssd(xbc, z, dt_raw, conv_w, conv_b, dt_bias, a_log, d_skip, ssm_norm_gain, b, s)
    return _out_proj(attn, ssm, x, gate_m, shift_f, scale_f, norm_ffn_gain, w_out, b, s)


def _first_argmax(v, iota, limit):
    m = jnp.max(v, axis=0, keepdims=True)
    idx = jnp.min(jnp.where(v == m, iota, limit), axis=0, keepdims=True)
    return m, idx


def _router_kernel(h_ref, wt_ref, bias_ref, upper_ref, eidx_ref, rank_ref, gate_ref, counts_ref, carry_ref):
    @pl.when(pl.program_id(0) == 0)
    def _():
        carry_ref[...] = jnp.zeros_like(carry_ref)

    tm = h_ref.shape[0]
    h = h_ref[...]
    wt = wt_ref[...]
    h_hi = h.astype(BF16)
    h_lo = (h - h_hi.astype(F32)).astype(BF16)
    w_hi = wt.astype(BF16)
    w_lo = (wt - w_hi.astype(F32)).astype(BF16)
    logits = _dot_nt(w_hi, h_hi) + _dot_nt(w_hi, h_lo) + _dot_nt(w_lo, h_hi)
    scores = _sigmoid(logits)
    choice = scores + bias_ref[...]
    neg_inf = -jnp.inf

    iota_g = lax.broadcasted_iota(I32, (EXPERTS_PER_GROUP, tm), 0).astype(F32)
    group_rows = []
    for g in range(N_EXPERT_GROUPS):
        v = choice[g * EXPERTS_PER_GROUP:(g + 1) * EXPERTS_PER_GROUP]
        m1, i1 = _first_argmax(v, iota_g, float(EXPERTS_PER_GROUP))
        m2 = jnp.max(jnp.where(iota_g == i1, neg_inf, v), axis=0, keepdims=True)
        group_rows.append(m1 + m2)
    group_scores = jnp.concatenate(group_rows, axis=0)

    iota_n = lax.broadcasted_iota(I32, (N_EXPERT_GROUPS, tm), 0).astype(F32)
    chosen = jnp.zeros((N_EXPERT_GROUPS, tm), F32)
    for _ in range(TOPK_GROUPS):
        _, gi = _first_argmax(group_scores, iota_n, float(N_EXPERT_GROUPS))
        hit = iota_n == gi
        chosen = jnp.where(hit, 1.0, chosen)
        group_scores = jnp.where(hit, neg_inf, group_scores)

    masked = jnp.concatenate(
        [jnp.where(chosen[g:g + 1] > 0.0, choice[g * EXPERTS_PER_GROUP:(g + 1) * EXPERTS_PER_GROUP], neg_inf)
         for g in range(N_EXPERT_GROUPS)], axis=0)

    iota_e = lax.broadcasted_iota(I32, (N_EXPERTS, tm), 0).astype(F32)
    picked, gates = [], []
    onehot = jnp.zeros((N_EXPERTS, tm), F32)
    for _ in range(TOP_K):
        _, ei = _first_argmax(masked, iota_e, float(N_EXPERTS))
        hit = iota_e == ei
        gates.append(jnp.sum(jnp.where(hit, scores, 0.0), axis=0, keepdims=True))
        masked = jnp.where(hit, neg_inf, masked)
        onehot = jnp.where(hit, 1.0, onehot)
        picked.append(ei)
    gate_sum = gates[0]
    for gk in gates[1:]:
        gate_sum = gate_sum + gk

    base = _dot(onehot.astype(BF16), upper_ref[...]) + carry_ref[...]
    ranks = [jnp.sum(jnp.where(iota_e == ei, base, 0.0), axis=0, keepdims=True) for ei in picked]
    carry_ref[...] = carry_ref[...] + jnp.sum(onehot, axis=1, keepdims=True)

    eidx_ref[...] = jnp.concatenate(picked, axis=0).astype(I32)
    rank_ref[...] = jnp.concatenate(ranks, axis=0).astype(I32)
    gate_ref[...] = jnp.concatenate([gk / gate_sum * ROUTED_SCALE for gk in gates], axis=0)
    counts_ref[...] = carry_ref[...].astype(I32)


def _router(h2, w_router, router_bias, tm=1024):
    t, d = h2.shape
    upper = jnp.asarray(np.triu(np.ones((tm, tm), np.float32), 1), BF16)
    tok = pl.BlockSpec((TOP_K, tm), lambda i: (0, i))
    full = lambda shp: pl.BlockSpec(shp, lambda i: (0,) * len(shp))
    return pl.pallas_call(
        _router_kernel,
        grid=(t // tm,),
        in_specs=[pl.BlockSpec((tm, d), lambda i: (i, 0)), full((N_EXPERTS, d)), full((N_EXPERTS, 1)), full((tm, tm))],
        out_specs=[tok, tok, tok, full((N_EXPERTS, 1))],
        out_shape=[jax.ShapeDtypeStruct((TOP_K, t), I32), jax.ShapeDtypeStruct((TOP_K, t), I32),
                   jax.ShapeDtypeStruct((TOP_K, t), F32), jax.ShapeDtypeStruct((N_EXPERTS, 1), I32)],
        scratch_shapes=[pltpu.VMEM((N_EXPERTS, 1), F32)],
        compiler_params=_params(("arbitrary",)),
        name="router",
    )(h2, w_router.T, router_bias.reshape(N_EXPERTS, 1), upper)


def _positions_kernel(counts_ref, lower_ref, eidx_ref, rank_ref, pos_ref):
    tm = eidx_ref.shape[1]
    counts = jnp.broadcast_to(counts_ref[...].astype(F32), (N_EXPERTS, LANES))
    offsets = _dot_exact_lhs(lower_ref[...], counts)[:, 0:1]
    iota_e = lax.broadcasted_iota(I32, (N_EXPERTS, tm), 0).astype(F32)
    e = eidx_ref[...].astype(F32)
    rows = [jnp.sum(jnp.where(iota_e == e[k:k + 1], offsets, 0.0), axis=0, keepdims=True) for k in range(TOP_K)]
    pos_ref[...] = jnp.concatenate(rows, axis=0).astype(I32) + rank_ref[...]


def _positions(counts, eidx, rank, tm):
    t = eidx.shape[1]
    lower = jnp.asarray(np.tril(np.ones((N_EXPERTS, N_EXPERTS), np.float32), -1), BF16)
    tok = pl.BlockSpec((TOP_K, tm), lambda i: (0, i))
    return pl.pallas_call(
        _positions_kernel,
        grid=(t // tm,),
        in_specs=[pl.BlockSpec((N_EXPERTS, 1), lambda i: (0, 0)), pl.BlockSpec((N_EXPERTS, N_EXPERTS), lambda i: (0, 0)),
                  tok, tok],
        out_specs=tok,
        out_shape=jax.ShapeDtypeStruct((TOP_K, t), I32),
        compiler_params=_params(("arbitrary",)),
        name="positions",
    )(counts, lower, eidx, rank)


EXPERT_BLOCK = 512
TAIL_UNIT = 128
TAIL_PIECES = tuple(1 << i for i in reversed(range(EXPERT_BLOCK.bit_length() - 1)))
W_SLOTS = 3
X_SLOTS = 4
Y_SLOTS = 4


def _experts_kernel(start_ref, count_ref, nxt_ref, nxt2_ref, slot_ref, first_ref, blk0_ref, full0_ref, ptail_ref,
                    ltail_ref, blktok_ref, nblocks_ref, xs_hbm, wg_hbm, wu_hbm, wd_hbm, ys_hbm,
                    wg_buf, wu_buf, wd_buf, wg_bf, wu_bf, wd_bf, xbuf, ybuf, ytail, wsem, xsem, ysem, tsem):
    e = pl.program_id(0)
    last_step = e == pl.num_programs(0) - 1
    start, count = start_ref[e], count_ref[e]
    n_full = jnp.right_shift(count, EXPERT_BLOCK.bit_length() - 1)
    tail = jnp.bitwise_and(count, EXPERT_BLOCK - 1)
    n_blk = n_full + (tail > 0).astype(I32)
    blk0, full0 = blk0_ref[e], full0_ref[e]
    slot, nxt, nxt2 = slot_ref[e], nxt_ref[e], nxt2_ref[e]
    slot1 = jnp.where(slot + 1 >= W_SLOTS, slot + 1 - W_SLOTS, slot + 1)
    slot2 = jnp.where(slot + 2 >= W_SLOTS, slot + 2 - W_SLOTS, slot + 2)

    def token_rows(token, n):
        return pl.ds(pl.multiple_of(token * TOKEN_ROWS, TOKEN_ROWS), n * TOKEN_ROWS)

    def fetch(ex, s):
        return (pltpu.make_async_copy(wg_hbm.at[ex], wg_buf.at[s], wsem.at[s, 0]),
                pltpu.make_async_copy(wu_hbm.at[ex], wu_buf.at[s], wsem.at[s, 1]),
                pltpu.make_async_copy(wd_hbm.at[ex], wd_buf.at[s], wsem.at[s, 2]))

    def x_copy(token, s):
        return pltpu.make_async_copy(xs_hbm.at[token_rows(token, EXPERT_BLOCK)], xbuf.at[s], xsem.at[s])

    def y_copy(token, s):
        return pltpu.make_async_copy(ybuf.at[s], ys_hbm.at[token_rows(token, EXPERT_BLOCK)], ysem.at[s])

    def tail_copies(token, length):
        out = []
        for piece in TAIL_PIECES:
            bigger = (EXPERT_BLOCK - 1) & ~(2 * piece - 1)
            done = jnp.bitwise_and(length, bigger)
            cp = pltpu.make_async_copy(ytail.at[token_rows(done, piece)], ys_hbm.at[token_rows(token + done, piece)],
                                       tsem)
            out.append((jnp.bitwise_and(length, piece) != 0, cp))
        return out

    def block(s, rows=EXPERT_BLOCK):
        x = _from_token_tiles(xbuf.at[s], rows).astype(BF16)
        g = _dot(x, wg_bf[...])
        u = _dot(x, wu_bf[...])
        return _dot((_silu(g) * u).astype(BF16), wd_bf[...])

    @pl.when(count > 0)
    def _():
        @pl.when(first_ref[e] == 1)
        def _():
            for g in range(X_SLOTS - 1):
                @pl.when(g < nblocks_ref[0])
                def _(g=g):
                    x_copy(blktok_ref[g], g).start()
            for cp in fetch(e, slot):
                cp.start()

            @pl.when(nxt >= 0)
            def _():
                for cp in fetch(nxt, slot1):
                    cp.start()

        for cp in fetch(e, slot):
            cp.wait()

        @pl.when(nxt2 >= 0)
        def _():
            for cp in fetch(nxt2, slot2):
                cp.start()

        wg_bf[...] = wg_buf[slot].astype(BF16)
        wu_bf[...] = wu_buf[slot].astype(BF16)
        wd_bf[...] = wd_buf[slot].astype(BF16)

        def take_x(i):
            g = blk0 + i
            xs_slot = jnp.bitwise_and(g, X_SLOTS - 1)
            x_copy(start, xs_slot).wait()
            ahead = g + (X_SLOTS - 1)

            @pl.when(ahead < nblocks_ref[0])
            def _():
                x_copy(blktok_ref[ahead], jnp.bitwise_and(ahead, X_SLOTS - 1)).start()

            return xs_slot

        def full_block(i, carry):
            y = block(take_x(i))
            j = full0 + i
            ys_slot = jnp.bitwise_and(j, Y_SLOTS - 1)

            @pl.when(j >= Y_SLOTS)
            def _():
                y_copy(start, ys_slot).wait()

            _to_token_tiles(ybuf.at[ys_slot], y)
            y_copy(start + i * EXPERT_BLOCK, ys_slot).start()
            return carry

        lax.fori_loop(0, n_full, full_block, 0)

        @pl.when(tail > 0)
        def _():
            xs_slot = take_x(n_full)
            for pred, cp in tail_copies(start, ptail_ref[e]):
                @pl.when(pred)
                def _(cp=cp):
                    cp.wait()
            units = jnp.right_shift(tail + (TAIL_UNIT - 1), TAIL_UNIT.bit_length() - 1)
            for u in range(1, EXPERT_BLOCK // TAIL_UNIT + 1):
                @pl.when(units == u)
                def _(u=u):
                    rows = u * TAIL_UNIT
                    _to_token_tiles(ytail.at[pl.ds(0, rows * TOKEN_ROWS)], block(xs_slot, rows))
            for pred, cp in tail_copies(start + n_full * EXPERT_BLOCK, tail):
                @pl.when(pred)
                def _(cp=cp):
                    cp.start()

    @pl.when(last_step)
    def _():
        total_full = full0 + n_full
        for back in range(1, Y_SLOTS + 1):
            @pl.when(total_full >= back)
            def _(back=back):
                y_copy(0, jnp.bitwise_and(total_full - back, Y_SLOTS - 1)).wait()
        for pred, cp in tail_copies(0, ltail_ref[0]):
            @pl.when(pred)
            def _(cp=cp):
                cp.wait()


def _max_expert_blocks(n_rows):
    return n_rows // EXPERT_BLOCK + N_EXPERTS


def _expert_metadata(counts, n_rows):
    ids = jnp.arange(N_EXPERTS, dtype=I32)
    used = counts > 0
    starts = jnp.cumsum(counts) - counts
    n_blk = (counts + EXPERT_BLOCK - 1) // EXPERT_BLOCK
    n_full = counts // EXPERT_BLOCK
    tail = counts % EXPERT_BLOCK
    blk0 = jnp.cumsum(n_blk) - n_blk
    full0 = jnp.cumsum(n_full) - n_full
    next_used = lax.cummin(jnp.where(used, ids, N_EXPERTS), reverse=True)
    next_after = jnp.concatenate([next_used[1:], jnp.full((1,), N_EXPERTS, I32)])
    nxt = jnp.where(next_after < N_EXPERTS, next_after, -1)
    ordinal = jnp.cumsum(used.astype(I32)) - 1
    slot = ordinal % W_SLOTS
    first = jnp.logical_and(used, ordinal == 0)
    latest = lax.cummax(jnp.where(tail > 0, ids, -1))
    before = jnp.concatenate([jnp.full((1,), -1, I32), latest[:-1]])
    pick = lambda index, values: jnp.sum(jnp.where(index[:, None] == ids[None, :], values[None, :], 0), axis=1)
    nxt2 = jnp.where(nxt >= 0, pick(nxt, nxt + 1), 0) - 1
    ptail = pick(before, tail)
    ltail = pick(latest[-1:], tail)
    block_ends = jnp.cumsum(n_blk)
    g = jnp.arange(_max_expert_blocks(n_rows), dtype=I32)
    eg = jnp.sum((g[:, None] >= block_ends[None, :]).astype(I32), axis=1)
    blktok = g * EXPERT_BLOCK + pick(eg, starts - blk0 * EXPERT_BLOCK)
    return tuple(v.astype(I32) for v in (starts, counts, nxt, nxt2, slot, first, blk0, full0, ptail, ltail, blktok,
                                         block_ends[-1:]))


def _experts(xs, counts, w_gate, w_up, w_down):
    d = D_MODEL
    meta = _expert_metadata(counts, xs.shape[0] // TOKEN_ROWS - EXPERT_BLOCK)
    hbm = pl.BlockSpec(memory_space=pl.ANY)
    blk = (EXPERT_BLOCK * TOKEN_ROWS, LANES)
    grid_spec = pltpu.PrefetchScalarGridSpec(
        num_scalar_prefetch=len(meta),
        grid=(N_EXPERTS,),
        in_specs=[hbm, hbm, hbm, hbm],
        out_specs=hbm,
        scratch_shapes=[pltpu.VMEM((W_SLOTS, d, EXPERT_FF), F32), pltpu.VMEM((W_SLOTS, d, EXPERT_FF), F32),
                        pltpu.VMEM((W_SLOTS, EXPERT_FF, d), F32),
                        pltpu.VMEM((d, EXPERT_FF), BF16), pltpu.VMEM((d, EXPERT_FF), BF16),
                        pltpu.VMEM((EXPERT_FF, d), BF16),
                        pltpu.VMEM((X_SLOTS,) + blk, WORD), pltpu.VMEM((Y_SLOTS,) + blk, WORD), pltpu.VMEM(blk, WORD),
                        pltpu.SemaphoreType.DMA((W_SLOTS, 3)), pltpu.SemaphoreType.DMA((X_SLOTS,)),
                        pltpu.SemaphoreType.DMA((Y_SLOTS,)), pltpu.SemaphoreType.DMA(())],
    )
    return pl.pallas_call(
        _experts_kernel,
        grid_spec=grid_spec,
        out_shape=jax.ShapeDtypeStruct(xs.shape, WORD),
        compiler_params=_params(("arbitrary",)),
        name="experts",
    )(*meta, xs, w_gate, w_up, w_down)


SC_CORES = 2
SC_SUBCORES = 16
SC_WORKERS = SC_CORES * SC_SUBCORES
SC_CHUNK = 128
SC_RING = 4


def _sc_scatter_rows(planes, idx, n_out_rows):
    n_planes, t, _ = planes.shape
    ranges = SC_WORKERS // n_planes
    n_chunks = t // ranges // SC_CHUNK
    idx = idx.reshape(TOP_K, n_planes, ranges, n_chunks, SC_CHUNK)
    mesh = plsc.VectorSubcoreMesh(core_axis_name="c", subcore_axis_name="s", num_cores=SC_CORES,
                                  num_subcores=SC_SUBCORES)

    def body(planes_hbm, idx_hbm, out_hbm, idx_v, rows_v, lsem, ssem):
        wid = lax.axis_index("s") * SC_CORES + lax.axis_index("c")
        plane = wid % n_planes
        token0 = (wid // n_planes) * (n_chunks * SC_CHUNK)
        for k in range(TOP_K):
            pltpu.sync_copy(idx_hbm.at[k, plane, wid // n_planes], idx_v.at[k])

        def load(slot, c):
            return pltpu.make_async_copy(planes_hbm.at[plane, pl.ds(token0 + c * SC_CHUNK, SC_CHUNK)], rows_v.at[slot],
                                         lsem.at[slot])

        def scatter(slot, c, k):
            return pltpu.make_async_copy(rows_v.at[slot], out_hbm.at[idx_v.at[k, c]], ssem.at[slot])

        @pl.loop(0, n_chunks, step=SC_RING)
        def _(g):
            for slot in range(SC_RING):
                @pl.when(g > 0)
                def _(slot=slot):
                    for k in range(TOP_K):
                        scatter(slot, 0, k).wait()
                load(slot, g + slot).start()
            for slot in range(SC_RING):
                load(slot, g + slot).wait()
                for k in range(TOP_K):
                    scatter(slot, g + slot, k).start()

        for slot in range(SC_RING):
            for k in range(TOP_K):
                scatter(slot, 0, k).wait()

    return pl.kernel(
        body, mesh=mesh,
        out_type=jax.ShapeDtypeStruct((n_out_rows, LANES), planes.dtype),
        scratch_types=[pltpu.VMEM((TOP_K, n_chunks, SC_CHUNK), I32), pltpu.VMEM((SC_RING, SC_CHUNK, LANES), planes.dtype),
                       pltpu.SemaphoreType.DMA((SC_RING,)), pltpu.SemaphoreType.DMA((SC_RING,))],
        name="sc_scatter",
    )(planes, idx)


def _sc_gather_rows(table, idx):
    n_workers, n_chunks, chunk = idx.shape
    rows_per_worker = n_chunks * chunk
    mesh = plsc.VectorSubcoreMesh(core_axis_name="c", subcore_axis_name="s", num_cores=SC_CORES,
                                  num_subcores=SC_SUBCORES)

    def body(table_hbm, idx_hbm, out_hbm, idx_v, rows_v, gsem, wsem):
        wid = lax.axis_index("s") * SC_CORES + lax.axis_index("c")
        base = wid * rows_per_worker
        pltpu.sync_copy(idx_hbm.at[wid], idx_v)

        def write(slot, c):
            return pltpu.make_async_copy(rows_v.at[slot], out_hbm.at[pl.ds(base + c * chunk, chunk)], wsem.at[slot])

        def gather(slot, c):
            return pltpu.make_async_copy(table_hbm.at[idx_v.at[c]], rows_v.at[slot], gsem.at[slot])

        @pl.loop(0, n_chunks, step=SC_RING)
        def _(g):
            for slot in range(SC_RING):
                @pl.when(g > 0)
                def _(slot=slot):
                    write(slot, 0).wait()
                gather(slot, g + slot).start()
            for slot in range(SC_RING):
                gather(slot, g + slot).wait()
                write(slot, g + slot).start()

        for slot in range(SC_RING):
            write(slot, 0).wait()

    return pl.kernel(
        body, mesh=mesh,
        out_type=jax.ShapeDtypeStruct((n_workers * rows_per_worker, LANES), table.dtype),
        scratch_types=[pltpu.VMEM((n_chunks, chunk), I32), pltpu.VMEM((SC_RING, chunk, LANES), table.dtype),
                       pltpu.SemaphoreType.DMA((SC_RING,)), pltpu.SemaphoreType.DMA((SC_RING,))],
        name="sc_gather",
    )(table, idx)


def _combine_kernel(rows_ref, gates_ref, hp_ref, x1_ref, gatef_ref, wg_ref, wu_ref, wd_ref, out_ref):
    def unpack(ref, first):
        words = [ref[first + c] for c in range(TOKEN_ROWS)]
        lows = [lax.bitcast_convert_type(jnp.left_shift(w, 16), F32) for w in words]
        highs = [lax.bitcast_convert_type(jnp.bitwise_and(w, HIGH_HALF), F32) for w in words]
        return jnp.concatenate(lows + highs, axis=1)

    hb = unpack(hp_ref, 0).astype(BF16)
    shared = _dot((_silu(_dot(hb, wg_ref[...])) * _dot(hb, wu_ref[...])).astype(BF16), wd_ref[...])
    gates = gates_ref[...]
    routed = unpack(rows_ref, 0) * gates[:, 0:1]
    for k in range(1, TOP_K):
        routed = routed + unpack(rows_ref, k * TOKEN_ROWS) * gates[:, k:k + 1]
    out_ref[...] = x1_ref[...] + gatef_ref[...] * (shared + routed)


def _combine(gathered, gates_t, h2p, x1, gate_f, w_gate_s, w_up_s, w_down_s, b, s, tm):
    t, d = x1.shape
    tiles_per_seq = s // tm
    row = lambda n: pl.BlockSpec((tm, n), lambda i: (i, 0))
    full = lambda shp: pl.BlockSpec(shp, lambda i: (0,) * len(shp))
    return pl.pallas_call(
        _combine_kernel,
        grid=(t // tm,),
        in_specs=[pl.BlockSpec((TOP_K * TOKEN_ROWS, tm, LANES), lambda i: (0, i, 0)),
                  row(TOP_K), pl.BlockSpec((TOKEN_ROWS, tm, LANES), lambda i: (0, i, 0)), row(d),
                  pl.BlockSpec((None, 1, d), lambda i: (i // tiles_per_seq, 0, 0)),
                  full((d, EXPERT_FF)), full((d, EXPERT_FF)), full((EXPERT_FF, d))],
        out_specs=row(d),
        out_shape=jax.ShapeDtypeStruct((t, d), F32),
        compiler_params=_params(("arbitrary",)),
        name="combine",
    )(gathered, gates_t, h2p, x1, gate_f.reshape(b, 1, d),
      w_gate_s.astype(BF16), w_up_s.astype(BF16), w_down_s.astype(BF16))


def _moe_sublayer(x1, h2, h2t, gate_f, w_router, router_bias, w_gate, w_up, w_down, w_gate_s, w_up_s, w_down_s, b, s,
                  tm=256):
    t = b * s
    eidx, rank, gates, counts = _router(h2, w_router, router_bias)
    pos = _positions(counts, eidx, rank, 4 * tm)
    idx = pos[:, None, :] * TOKEN_ROWS + jnp.arange(TOKEN_ROWS, dtype=I32)[None, :, None]
    xs = _sc_scatter_rows(h2t, idx, (t * TOP_K + EXPERT_BLOCK) * TOKEN_ROWS)
    ys = _experts(xs, counts[:, 0], w_gate, w_up, w_down)
    gathered = _sc_gather_rows(ys, idx.reshape(SC_WORKERS, -1, SC_CHUNK)).reshape(TOP_K * TOKEN_ROWS, t, LANES)
    return _combine(gathered, gates.T, h2t, x1, gate_f, w_gate_s, w_up_s, w_down_s, b, s, 2 * tm)


def kernel(x, c, w_ada, b_ada, norm_mix_gain, w_in, q_norm_gain, k_norm_gain, rel_bias_table, conv_w, conv_b, dt_bias,
           a_log, d_skip, ssm_norm_gain, w_out, norm_ffn_gain, w_router, router_bias, w_gate_experts, w_up_experts,
           w_down_experts, w_gate_shared, w_up_shared, w_down_shared):
    b, s, d = x.shape
    for layer in range(w_ada.shape[0]):
        mod = _adaln(c, w_ada[layer], b_ada[layer])
        x1, h2, h2t = _mixer_sublayer(x, mod, norm_mix_gain[layer], w_in[layer], q_norm_gain[layer], k_norm_gain[layer],
                                 rel_bias_table, conv_w[layer], conv_b[layer], dt_bias[layer], a_log[layer],
                                 d_skip[layer], ssm_norm_gain[layer], w_out[layer], norm_ffn_gain[layer])
        gate_f = mod[:, 5 * d:]
        out = _moe_sublayer(x1, h2, h2t, gate_f, w_router[layer], router_bias[layer], w_gate_experts[layer],
                            w_up_experts[layer], w_down_experts[layer], w_gate_shared[layer], w_up_shared[layer],
                            w_down_shared[layer], b, s)
        x = out.reshape(b, s, d)
    return x
```

```python
import math

import numpy as np
import jax
import jax.numpy as jnp
from jax import lax
from jax.experimental import pallas as pl
from jax.experimental.pallas import tpu as pltpu
from jax.experimental.pallas import tpu_sc as plsc

F32 = jnp.float32
BF16 = jnp.bfloat16
I32 = jnp.int32

D_MODEL = 1024
ATTN_HEADS = 8
HEAD_DIM = 64
ATTN_WIDTH = ATTN_HEADS * HEAD_DIM
PATTERNS = ((128, 1), (512, 4), (2048, 16))
WIN_STEPS = 128
REL_BUCKETS = 32
REL_MAX_DISTANCE = 2048
SSM_HEADS = 24
SSM_HEAD_DIM = 64
SSM_WIDTH = SSM_HEADS * SSM_HEAD_DIM
SSM_GROUPS = 4
HEADS_PER_GROUP = SSM_HEADS // SSM_GROUPS
GROUP_WIDTH = SSM_WIDTH // SSM_GROUPS
SSM_STATE = 128
SSM_CONV = 4
SSM_CHUNK = 128
CONV_CH = SSM_WIDTH + 2 * SSM_GROUPS * SSM_STATE
N_EXPERTS = 256
TOP_K = 8
N_EXPERT_GROUPS = 8
EXPERTS_PER_GROUP = N_EXPERTS // N_EXPERT_GROUPS
TOPK_GROUPS = 4
EXPERT_FF = 256
ROUTED_SCALE = 2.5
NORM_EPS = 1e-6

LANES = 128
SUBLANES = 8
NEG_BIG = -1e30
VMEM_LIMIT = 56 * 1024 * 1024


def _params(sem, vmem=VMEM_LIMIT):
    return pltpu.CompilerParams(dimension_semantics=sem, vmem_limit_bytes=vmem)


def _sigmoid(x):
    return 1.0 / (1.0 + jnp.exp(-x))


def _silu(x):
    return x * _sigmoid(x)


def _split3(x):
    hi = x.astype(BF16)
    r = x - hi.astype(F32)
    mid = r.astype(BF16)
    lo = (r - mid.astype(F32)).astype(BF16)
    return hi, mid, lo


def _dot(a, b):
    return jnp.dot(a, b, preferred_element_type=F32)


def _dot_nt(a, b):
    return lax.dot_general(a, b, (((1,), (1,)), ((), ())), preferred_element_type=F32)


def _dot_exact_rhs(a, b_exact):
    hi, mid, lo = _split3(a)
    return _dot(hi, b_exact) + _dot(mid, b_exact) + _dot(lo, b_exact)


def _dot_exact_lhs(a_exact, b):
    hi, mid, lo = _split3(b)
    return _dot(a_exact, hi) + _dot(a_exact, mid) + _dot(a_exact, lo)


def _adaln_kernel(c_ref, w_ref, b_ref, o_ref):
    s = _silu(c_ref[...]).astype(BF16)
    o_ref[...] = _dot(s, w_ref[...].astype(BF16)) + b_ref[...]


def _adaln(c, w_ada, b_ada):
    b, d = c.shape
    n = w_ada.shape[1]
    rows = SUBLANES
    c_pad = jnp.zeros((rows, d), F32).at[:b].set(c)
    tn = 1024
    out = pl.pallas_call(
        _adaln_kernel,
        grid=(n // tn,),
        in_specs=[pl.BlockSpec((rows, d), lambda j: (0, 0)),
                  pl.BlockSpec((d, tn), lambda j: (0, j)),
                  pl.BlockSpec((1, tn), lambda j: (0, j))],
        out_specs=pl.BlockSpec((rows, tn), lambda j: (0, j)),
        out_shape=jax.ShapeDtypeStruct((rows, n), F32),
        compiler_params=_params(("arbitrary",)),
        name="adaln",
    )(c_pad, w_ada, b_ada.reshape(1, n))
    return out[:b]


def _inproj_kernel(x_ref, shift_ref, scale_ref, g_ref, wqkv_ref, wz_ref, wxbc_ref, wdt_ref,
                   qg_ref, kg_ref, hmean_ref, q_ref, k_ref, v_ref, z_ref, xbc_ref, dt_ref):
    x = x_ref[...]
    ms = jnp.mean(x * x, axis=-1, keepdims=True)
    h = x * lax.rsqrt(ms + NORM_EPS) * g_ref[...]
    h = h * (1.0 + scale_ref[...]) + shift_ref[...]
    hb = h.astype(BF16)

    hmean = hmean_ref[...]

    def head_norm(t, gain):
        sq = t * t
        hi = sq.astype(BF16)
        mid = (sq - hi.astype(F32)).astype(BF16)
        ss = _dot(hi, hmean) + _dot(mid, hmean)
        return t * lax.rsqrt(ss + NORM_EPS) * gain

    q = _dot(hb, wqkv_ref[:, 0:ATTN_WIDTH])
    q_ref[...] = head_norm(q, qg_ref[...]) * (HEAD_DIM ** -0.5)
    k = _dot(hb, wqkv_ref[:, ATTN_WIDTH:2 * ATTN_WIDTH])
    k_ref[...] = head_norm(k, kg_ref[...])
    v_ref[...] = _dot(hb, wqkv_ref[:, 2 * ATTN_WIDTH:3 * ATTN_WIDTH])
    for c0 in range(0, SSM_WIDTH, 512):
        z_ref[:, c0:c0 + 512] = _dot(hb, wz_ref[:, c0:c0 + 512])
    for c0 in range(0, CONV_CH, 512):
        xbc_ref[:, c0:c0 + 512] = _dot(hb, wxbc_ref[:, c0:c0 + 512])
    dt_ref[...] = _dot(hb, wdt_ref[...])


def _in_proj(x, shift, scale, gain, w_in, q_gain, k_gain, tm=256):
    b, s, d = x.shape
    t = b * s
    tiles_per_seq = s // tm
    w = w_in.astype(BF16)
    o_z = 3 * ATTN_WIDTH
    o_x = o_z + SSM_WIDTH
    o_dt = o_x + CONV_CH
    w_qkv, w_z, w_xbc = w[:, :o_z], w[:, o_z:o_x], w[:, o_x:o_dt]
    w_dt = jnp.zeros((d, LANES), BF16).at[:, :SSM_HEADS].set(w[:, o_dt:])
    head_of = np.arange(ATTN_WIDTH) // HEAD_DIM
    hmean = jnp.asarray((head_of[:, None] == head_of[None, :]).astype(np.float32) / HEAD_DIM, BF16)
    full = lambda shp: pl.BlockSpec(shp, lambda i: (0,) * len(shp))
    row = lambda n: pl.BlockSpec((tm, n), lambda i: (i, 0))
    per_batch = pl.BlockSpec((None, 1, d), lambda i: (i // tiles_per_seq, 0, 0))
    outs = pl.pallas_call(
        _inproj_kernel,
        grid=(t // tm,),
        in_specs=[row(d), per_batch, per_batch, full((1, d)),
                  full((d, o_z)), full((d, SSM_WIDTH)), full((d, CONV_CH)), full((d, LANES)),
                  full((1, ATTN_WIDTH)), full((1, ATTN_WIDTH)), full((ATTN_WIDTH, ATTN_WIDTH))],
        out_specs=[row(ATTN_WIDTH), row(ATTN_WIDTH), row(ATTN_WIDTH), row(SSM_WIDTH), row(CONV_CH), row(LANES)],
        out_shape=[jax.ShapeDtypeStruct((t, n), F32)
                   for n in (ATTN_WIDTH, ATTN_WIDTH, ATTN_WIDTH, SSM_WIDTH, CONV_CH, LANES)],
        compiler_params=_params(("arbitrary",)),
        name="in_proj",
    )(x.reshape(t, d), shift.reshape(b, 1, d), scale.reshape(b, 1, d), gain.reshape(1, d),
      w_qkv, w_z, w_xbc, w_dt,
      jnp.tile(q_gain, ATTN_HEADS).reshape(1, ATTN_WIDTH), jnp.tile(k_gain, ATTN_HEADS).reshape(1, ATTN_WIDTH), hmean)
    return outs


def _t5_causal_buckets(distance):
    n = np.maximum(distance, 0)
    max_exact = REL_BUCKETS // 2
    large = max_exact + (np.log(np.maximum(n, 1) / max_exact) / math.log(REL_MAX_DISTANCE / max_exact)
                         * (REL_BUCKETS - max_exact)).astype(np.int64)
    large = np.minimum(large, REL_BUCKETS - 1)
    return np.where(n < max_exact, n, large).astype(np.int32)


def _window_bias(rel_bias_table, dilation):
    qi = np.arange(WIN_STEPS)[:, None]
    kj = np.arange(2 * WIN_STEPS)[None, :]
    dist = qi + WIN_STEPS - kj
    band = (dist >= 0) & (dist <= WIN_STEPS)
    onehot = (_t5_causal_buckets(dist * dilation).reshape(-1, 1) == np.arange(REL_BUCKETS)[None, :]).astype(np.float32)
    bias = jnp.dot(rel_bias_table.astype(F32).T, jnp.asarray(onehot).T, precision=lax.Precision.HIGHEST)
    bias = bias.reshape(ATTN_HEADS, WIN_STEPS, 2 * WIN_STEPS)
    return jnp.where(jnp.asarray(band)[None], bias, NEG_BIG)


ATTN_TOKENS = max(w for w, _ in PATTERNS)
ATTN_UNROLL = 16


def _attn_kernel(q_ref, kp_ref, kc_ref, vp_ref, vc_ref, bias_ref, out_ref, kw, vw, o_acc, l_acc):
    tb = ATTN_TOKENS
    first = pl.program_id(2) == 0
    kw[0:tb] = kp_ref[...]
    kw[tb:2 * tb] = kc_ref[...]
    vw[0:tb] = vp_ref[...]
    vw[tb:2 * tb] = vc_ref[...]
    lane = lax.broadcasted_iota(I32, (WIN_STEPS, LANES), 1)
    head0 = lane < HEAD_DIM
    col = lax.broadcasted_iota(I32, (WIN_STEPS, 2 * WIN_STEPS), 1)
    in_prev = col < WIN_STEPS

    for p, (_, d) in enumerate(PATTERNS):
        shift = d.bit_length() - 1
        n_blocks = tb // WIN_STEPS

        def rows(start, n, d=d):
            return pl.ds(start, n, stride=d) if d > 1 else pl.ds(start, n)

        def body(it, carry, p=p, d=d, shift=shift, rows=rows):
            for u in range(ATTN_UNROLL):
                idx = it * ATTN_UNROLL + u
                r = jnp.bitwise_and(idx, d - 1)
                j = jnp.right_shift(idx, shift)
                qs = j * (WIN_STEPS * d) + r
                q = q_ref[rows(qs, WIN_STEPS), :]
                k = kw[rows(tb + qs - WIN_STEPS * d, 2 * WIN_STEPS), :].astype(BF16)
                v = vw[rows(tb + qs - WIN_STEPS * d, 2 * WIN_STEPS), :].astype(BF16)
                no_prev = jnp.logical_and(in_prev, jnp.logical_and(first, j == 0))
                o_h, lse_h = [], []
                for h in range(2):
                    qh = jnp.where(head0 if h == 0 else jnp.logical_not(head0), q, 0.0).astype(BF16)
                    s = _dot_nt(qh, k) + bias_ref[p, h]
                    s = jnp.where(no_prev, NEG_BIG, s)
                    m = jnp.max(s, axis=-1, keepdims=True)
                    e = jnp.exp(s - m)
                    denom = jnp.sum(e, axis=-1, keepdims=True)
                    o_h.append(_dot(e.astype(BF16), v) / denom)
                    lse_h.append(m + jnp.log(denom))
                o_acc[p, rows(qs, WIN_STEPS), :] = jnp.where(head0, o_h[0], o_h[1])
                l_acc[p, rows(qs, WIN_STEPS), :] = jnp.where(head0, lse_h[0], lse_h[1])
            return carry

        lax.fori_loop(0, n_blocks // ATTN_UNROLL, body, 0)

    chunk = 256
    for c0 in range(0, tb, chunk):
        l1, l2, l3 = (l_acc[p, c0:c0 + chunk, :] for p in range(3))
        m = jnp.maximum(jnp.maximum(l1, l2), l3)
        e1, e2, e3 = jnp.exp(l1 - m), jnp.exp(l2 - m), jnp.exp(l3 - m)
        num = e1 * o_acc[0, c0:c0 + chunk, :] + e2 * o_acc[1, c0:c0 + chunk, :] + e3 * o_acc[2, c0:c0 + chunk, :]
        out_ref[c0:c0 + chunk, :] = num / (e1 + e2 + e3)


def _attention(q, k, v, bias):
    b, s, w = q.shape
    tb = ATTN_TOKENS
    pairs = ATTN_HEADS // 2
    cur = pl.BlockSpec((None, tb, LANES), lambda bi, hp, i: (bi, i, hp))
    prev = pl.BlockSpec((None, tb, LANES), lambda bi, hp, i: (bi, jnp.maximum(i - 1, 0), hp))
    return pl.pallas_call(
        _attn_kernel,
        grid=(b, pairs, s // tb),
        in_specs=[cur, prev, cur, prev, cur,
                  pl.BlockSpec((len(PATTERNS), 2, WIN_STEPS, 2 * WIN_STEPS), lambda bi, hp, i: (0, hp, 0, 0))],
        out_specs=cur,
        out_shape=jax.ShapeDtypeStruct((b, s, w), F32),
        scratch_shapes=[pltpu.VMEM((2 * tb, LANES), F32), pltpu.VMEM((2 * tb, LANES), F32),
                        pltpu.VMEM((len(PATTERNS), tb, LANES), F32), pltpu.VMEM((len(PATTERNS), tb, LANES), F32)],
        compiler_params=_params(("arbitrary",) * 3),
        name="attention",
    )(q, k, k, v, v, bias)


SSD_CHUNKS_PER_STEP = 2


def _ssd_kernel(xbc_ref, halo_ref, z_ref, dtraw_ref, convw_ref, convb_ref, dtb_ref, alog_ref, dskip_ref, gain_ref,
                expand_ref, tril_ref, y_ref, state_ref):
    c = pl.program_id(1)

    @pl.when(c == 0)
    def _():
        state_ref[...] = jnp.zeros_like(state_ref)

    halo = jnp.where(c == 0, 0.0, halo_ref[...])
    for sub in range(SSD_CHUNKS_PER_STEP):
        rows = slice(sub * SSM_CHUNK, (sub + 1) * SSM_CHUNK)
        x = xbc_ref[rows, :]
        w = convw_ref[...]
        acc = x * w[SSM_CONV - 1:SSM_CONV, :] + convb_ref[...]
        row8 = lax.broadcasted_iota(I32, (SUBLANES, CONV_CH), 0)
        for shift in range(1, SSM_CONV):
            xs = pltpu.roll(x, shift, axis=0)
            hs = pltpu.roll(halo, shift, axis=0)
            head = jnp.where(row8 < shift, hs, xs[0:SUBLANES])
            xs = jnp.concatenate([head, xs[SUBLANES:]], axis=0)
            acc = acc + xs * w[SSM_CONV - 1 - shift:SSM_CONV - shift, :]
        act = _silu(acc)
        x_s = act[:, :SSM_WIDTH]
        bc0 = SSM_WIDTH
        cc0 = SSM_WIDTH + SSM_GROUPS * SSM_STATE

        t = dtraw_ref[rows, :] + dtb_ref[...]
        dt = jnp.maximum(t, 0.0) + jnp.log(1.0 + jnp.exp(-jnp.abs(t)))
        a = dt * (-jnp.exp(alog_ref[...]))
        a_cs = _dot_exact_lhs(tril_ref[...], a)
        a_cs_t = a_cs.T
        a_last = a_cs[SSM_CHUNK - 1:SSM_CHUNK, :]
        expand = expand_ref[...]
        dt_e = _dot_exact_rhs(dt, expand)
        ea_e = _dot_exact_rhs(jnp.exp(a_cs), expand)
        dte_e = _dot_exact_rhs(jnp.exp(a_last - a_cs), expand)
        xdt = x_s * dt_e
        xw = (xdt * dte_e).astype(BF16)
        xdt_b = xdt.astype(BF16)

        li = lax.broadcasted_iota(I32, (SSM_CHUNK, SSM_CHUNK), 0)
        si = lax.broadcasted_iota(I32, (SSM_CHUNK, SSM_CHUNK), 1)
        causal = li >= si

        ys = []
        for g in range(SSM_GROUPS):
            gs = slice(g * GROUP_WIDTH, (g + 1) * GROUP_WIDTH)
            b_g = act[:, bc0 + g * SSM_STATE:bc0 + (g + 1) * SSM_STATE]
            c_g = act[:, cc0 + g * SSM_STATE:cc0 + (g + 1) * SSM_STATE].astype(BF16)
            cb = _dot_nt(c_g, b_g.astype(BF16))
            state = state_ref[g]
            y_off = _dot(c_g, state.astype(BF16)) * ea_e[:, gs]
            parts = []
            for j in range(HEADS_PER_GROUP):
                hh = g * HEADS_PER_GROUP + j
                seg = a_cs[:, hh:hh + 1] - a_cs_t[hh:hh + 1, :]
                decay = jnp.exp(jnp.where(causal, seg, NEG_BIG))
                m = (cb * decay).astype(BF16)
                parts.append(_dot(m, xdt_b[:, hh * SSM_HEAD_DIM:(hh + 1) * SSM_HEAD_DIM]))
            ys.append(jnp.concatenate(parts, axis=1) + y_off)
            state_ref[g] = state * ea_e[SSM_CHUNK - 1:SSM_CHUNK, gs] + _dot(b_g.T.astype(BF16), xw[:, gs])
        y = jnp.concatenate(ys, axis=1) + dskip_ref[...] * x_s
        y = y * _silu(z_ref[rows, :])
        gain = gain_ref[...]
        for g in range(SSM_GROUPS):
            gs = slice(g * GROUP_WIDTH, (g + 1) * GROUP_WIDTH)
            yg = y[:, gs]
            ms = jnp.mean(yg * yg, axis=-1, keepdims=True)
            y_ref[rows, gs] = yg * lax.rsqrt(ms + NORM_EPS) * gain[:, gs]
        halo = x[SSM_CHUNK - SUBLANES:, :]


def _ssd(xbc, z, dt_raw, conv_w, conv_b, dt_bias, a_log, d_skip, norm_gain, b, s):
    t = b * s
    step_rows = SSD_CHUNKS_PER_STEP * SSM_CHUNK
    nc = s // step_rows
    pad_heads = lambda v: jnp.zeros((1, LANES), F32).at[0, :SSM_HEADS].set(v)
    head_of_lane = np.arange(SSM_WIDTH) // SSM_HEAD_DIM
    expand = jnp.asarray((np.arange(LANES)[:, None] == head_of_lane[None, :]).astype(np.float32), BF16)
    tril = jnp.asarray(np.tril(np.ones((SSM_CHUNK, SSM_CHUNK), np.float32)), BF16)
    halo_blocks = step_rows // SUBLANES
    chunk = lambda n: pl.BlockSpec((step_rows, n), lambda bi, c: (bi * nc + c, 0))
    full = lambda shp: pl.BlockSpec(shp, lambda bi, c: (0,) * len(shp))
    halo = pl.BlockSpec((SUBLANES, CONV_CH), lambda bi, c: (jnp.maximum((bi * nc + c) * halo_blocks - 1, 0), 0))
    return pl.pallas_call(
        _ssd_kernel,
        grid=(b, nc),
        in_specs=[chunk(CONV_CH), halo, chunk(SSM_WIDTH), chunk(LANES),
                  full((SSM_CONV, CONV_CH)), full((1, CONV_CH)), full((1, LANES)), full((1, LANES)),
                  full((1, SSM_WIDTH)), full((1, SSM_WIDTH)), full((LANES, SSM_WIDTH)), full((SSM_CHUNK, SSM_CHUNK))],
        out_specs=chunk(SSM_WIDTH),
        out_shape=jax.ShapeDtypeStruct((t, SSM_WIDTH), F32),
        scratch_shapes=[pltpu.VMEM((SSM_GROUPS, SSM_STATE, GROUP_WIDTH), F32)],
        compiler_params=_params(("arbitrary", "arbitrary")),
        name="ssd",
    )(xbc, xbc, z, dt_raw, conv_w, conv_b.reshape(1, CONV_CH), pad_heads(dt_bias), pad_heads(a_log),
      jnp.repeat(d_skip, SSM_HEAD_DIM).reshape(1, SSM_WIDTH), norm_gain.reshape(1, SSM_WIDTH), expand, tril)


WORD = jnp.int32
TOKEN_ROWS = D_MODEL // (2 * LANES)
HIGH_HALF = np.int32(-65536)


def _to_token_tiles(ref, x):
    n = x.shape[0]
    for c in range(TOKEN_ROWS):
        lo = lax.bitcast_convert_type(x[:, c * LANES:(c + 1) * LANES].astype(BF16).astype(F32), WORD)
        hi = lax.bitcast_convert_type(x[:, (c + TOKEN_ROWS) * LANES:(c + TOKEN_ROWS + 1) * LANES]
                                      .astype(BF16).astype(F32), WORD)
        ref[pl.ds(c, n, stride=TOKEN_ROWS), :] = jnp.bitwise_or(lax.shift_right_logical(lo, 16),
                                                                 jnp.bitwise_and(hi, HIGH_HALF))


def _from_token_tiles(ref, n, token0=0):
    lows, highs = [], []
    for c in range(TOKEN_ROWS):
        word = ref[pl.ds(token0 * TOKEN_ROWS + c, n, stride=TOKEN_ROWS), :]
        lows.append(lax.bitcast_convert_type(jnp.left_shift(word, 16), F32))
        highs.append(lax.bitcast_convert_type(jnp.bitwise_and(word, HIGH_HALF), F32))
    return jnp.concatenate(lows + highs, axis=1)


def _outproj_kernel(attn_ref, ssm_ref, x_ref, gate_ref, shift_ref, scale_ref, g_ref, wa_ref, ws_ref, x1_ref, h2_ref,
                    h2t_ref):
    mixed = _dot(attn_ref[...].astype(BF16), wa_ref[...]) + _dot(ssm_ref[...].astype(BF16), ws_ref[...])
    x1 = x_ref[...] + gate_ref[...] * mixed
    x1_ref[...] = x1
    ms = jnp.mean(x1 * x1, axis=-1, keepdims=True)
    h = x1 * lax.rsqrt(ms + NORM_EPS) * g_ref[...]
    h2 = h * (1.0 + scale_ref[...]) + shift_ref[...]
    h2_ref[...] = h2
    for c in range(TOKEN_ROWS):
        lo = lax.bitcast_convert_type(h2[:, c * LANES:(c + 1) * LANES].astype(BF16).astype(F32), WORD)
        hi = lax.bitcast_convert_type(h2[:, (c + TOKEN_ROWS) * LANES:(c + TOKEN_ROWS + 1) * LANES]
                                      .astype(BF16).astype(F32), WORD)
        h2t_ref[c] = jnp.bitwise_or(lax.shift_right_logical(lo, 16), jnp.bitwise_and(hi, HIGH_HALF))


def _out_proj(attn, ssm, x, gate, shift, scale, gain, w_out, b, s, tm=512):
    t = b * s
    d = D_MODEL
    tiles_per_seq = s // tm
    w = w_out.astype(BF16)
    row = lambda n: pl.BlockSpec((tm, n), lambda i: (i, 0))
    full = lambda shp: pl.BlockSpec(shp, lambda i: (0,) * len(shp))
    per_batch = pl.BlockSpec((None, 1, d), lambda i: (i // tiles_per_seq, 0, 0))
    return pl.pallas_call(
        _outproj_kernel,
        grid=(t // tm,),
        in_specs=[row(ATTN_WIDTH), row(SSM_WIDTH), row(d), per_batch, per_batch, per_batch, full((1, d)),
                  full((ATTN_WIDTH, d)), full((SSM_WIDTH, d))],
        out_specs=[row(d), row(d), pl.BlockSpec((TOKEN_ROWS, tm, LANES), lambda i: (0, i, 0))],
        out_shape=[jax.ShapeDtypeStruct((t, d), F32)] * 2 + [jax.ShapeDtypeStruct((TOKEN_ROWS, t, LANES), WORD)],
        compiler_params=_params(("arbitrary",)),
        name="out_proj",
    )(attn.reshape(t, ATTN_WIDTH), ssm, x.reshape(t, d),
      gate.reshape(b, 1, d), shift.reshape(b, 1, d), scale.reshape(b, 1, d), gain.reshape(1, d),
      w[:ATTN_WIDTH], w[ATTN_WIDTH:])


def _mixer_sublayer(x, mod, norm_mix_gain, w_in, q_norm_gain, k_norm_gain, rel_bias_table, conv_w, conv_b, dt_bias,
                    a_log, d_skip, ssm_norm_gain, w_out, norm_ffn_gain):
    b, s, d = x.shape
    shift_m, scale_m, gate_m, shift_f, scale_f, _ = jnp.split(mod, 6, axis=-1)
    q, k, v, z, xbc, dt_raw = _in_proj(x, shift_m, scale_m, norm_mix_gain, w_in, q_norm_gain, k_norm_gain)
    bias = jnp.stack([_window_bias(rel_bias_table, dilation) for _, dilation in PATTERNS])
    attn = _attention(q.reshape(b, s, ATTN_WIDTH), k.reshape(b, s, ATTN_WIDTH), v.reshape(b, s, ATTN_WIDTH), bias)
    ssm = _ssd(xbc, z, dt_raw, conv_w, conv_b, dt_bias, a_log, d_skip, ssm_norm_gain, b, s)
    return _out_proj(attn, ssm, x, gate_m, shift_f, scale_f, norm_ffn_gain, w_out, b, s)


def _first_argmax(v, iota, limit):
    m = jnp.max(v, axis=0, keepdims=True)
    idx = jnp.min(jnp.where(v == m, iota, limit), axis=0, keepdims=True)
    return m, idx


def _router_kernel(h_ref, wt_ref, bias_ref, upper_ref, eidx_ref, rank_ref, gate_ref, counts_ref, carry_ref):
    @pl.when(pl.program_id(0) == 0)
    def _():
        carry_ref[...] = jnp.zeros_like(carry_ref)

    tm = h_ref.shape[0]
    h = h_ref[...]
    wt = wt_ref[...]
    h_hi = h.astype(BF16)
    h_lo = (h - h_hi.astype(F32)).astype(BF16)
    w_hi = wt.astype(BF16)
    w_lo = (wt - w_hi.astype(F32)).astype(BF16)
    logits = _dot_nt(w_hi, h_hi) + _dot_nt(w_hi, h_lo) + _dot_nt(w_lo, h_hi)
    scores = _sigmoid(logits)
    choice = scores + bias_ref[...]
    neg_inf = -jnp.inf

    iota_g = lax.broadcasted_iota(I32, (EXPERTS_PER_GROUP, tm), 0).astype(F32)
    group_rows = []
    for g in range(N_EXPERT_GROUPS):
        v = choice[g * EXPERTS_PER_GROUP:(g + 1) * EXPERTS_PER_GROUP]
        m1, i1 = _first_argmax(v, iota_g, float(EXPERTS_PER_GROUP))
        m2 = jnp.max(jnp.where(iota_g == i1, neg_inf, v), axis=0, keepdims=True)
        group_rows.append(m1 + m2)
    group_scores = jnp.concatenate(group_rows, axis=0)

    iota_n = lax.broadcasted_iota(I32, (N_EXPERT_GROUPS, tm), 0).astype(F32)
    chosen = jnp.zeros((N_EXPERT_GROUPS, tm), F32)
    for _ in range(TOPK_GROUPS):
        _, gi = _first_argmax(group_scores, iota_n, float(N_EXPERT_GROUPS))
        hit = iota_n == gi
        chosen = jnp.where(hit, 1.0, chosen)
        group_scores = jnp.where(hit, neg_inf, group_scores)

    masked = jnp.concatenate(
        [jnp.where(chosen[g:g + 1] > 0.0, choice[g * EXPERTS_PER_GROUP:(g + 1) * EXPERTS_PER_GROUP], neg_inf)
         for g in range(N_EXPERT_GROUPS)], axis=0)

    iota_e = lax.broadcasted_iota(I32, (N_EXPERTS, tm), 0).astype(F32)
    picked, gates = [], []
    onehot = jnp.zeros((N_EXPERTS, tm), F32)
    for _ in range(TOP_K):
        _, ei = _first_argmax(masked, iota_e, float(N_EXPERTS))
        hit = iota_e == ei
        gates.append(jnp.sum(jnp.where(hit, scores, 0.0), axis=0, keepdims=True))
        masked = jnp.where(hit, neg_inf, masked)
        onehot = jnp.where(hit, 1.0, onehot)
        picked.append(ei)
    gate_sum = gates[0]
    for gk in gates[1:]:
        gate_sum = gate_sum + gk

    base = _dot(onehot.astype(BF16), upper_ref[...]) + carry_ref[...]
    ranks = [jnp.sum(jnp.where(iota_e == ei, base, 0.0), axis=0, keepdims=True) for ei in picked]
    carry_ref[...] = carry_ref[...] + jnp.sum(onehot, axis=1, keepdims=True)

    eidx_ref[...] = jnp.concatenate(picked, axis=0).astype(I32)
    rank_ref[...] = jnp.concatenate(ranks, axis=0).astype(I32)
    gate_ref[...] = jnp.concatenate([gk / gate_sum * ROUTED_SCALE for gk in gates], axis=0)
    counts_ref[...] = carry_ref[...].astype(I32)


def _router(h2, w_router, router_bias, tm=1024):
    t, d = h2.shape
    upper = jnp.asarray(np.triu(np.ones((tm, tm), np.float32), 1), BF16)
    tok = pl.BlockSpec((TOP_K, tm), lambda i: (0, i))
    full = lambda shp: pl.BlockSpec(shp, lambda i: (0,) * len(shp))
    return pl.pallas_call(
        _router_kernel,
        grid=(t // tm,),
        in_specs=[pl.BlockSpec((tm, d), lambda i: (i, 0)), full((N_EXPERTS, d)), full((N_EXPERTS, 1)), full((tm, tm))],
        out_specs=[tok, tok, tok, full((N_EXPERTS, 1))],
        out_shape=[jax.ShapeDtypeStruct((TOP_K, t), I32), jax.ShapeDtypeStruct((TOP_K, t), I32),
                   jax.ShapeDtypeStruct((TOP_K, t), F32), jax.ShapeDtypeStruct((N_EXPERTS, 1), I32)],
        scratch_shapes=[pltpu.VMEM((N_EXPERTS, 1), F32)],
        compiler_params=_params(("arbitrary",)),
        name="router",
    )(h2, w_router.T, router_bias.reshape(N_EXPERTS, 1), upper)


def _positions_kernel(counts_ref, lower_ref, eidx_ref, rank_ref, pos_ref):
    tm = eidx_ref.shape[1]
    counts = jnp.broadcast_to(counts_ref[...].astype(F32), (N_EXPERTS, LANES))
    offsets = _dot_exact_lhs(lower_ref[...], counts)[:, 0:1]
    iota_e = lax.broadcasted_iota(I32, (N_EXPERTS, tm), 0).astype(F32)
    e = eidx_ref[...].astype(F32)
    rows = [jnp.sum(jnp.where(iota_e == e[k:k + 1], offsets, 0.0), axis=0, keepdims=True) for k in range(TOP_K)]
    pos_ref[...] = jnp.concatenate(rows, axis=0).astype(I32) + rank_ref[...]


def _positions(counts, eidx, rank, tm):
    t = eidx.shape[1]
    lower = jnp.asarray(np.tril(np.ones((N_EXPERTS, N_EXPERTS), np.float32), -1), BF16)
    tok = pl.BlockSpec((TOP_K, tm), lambda i: (0, i))
    return pl.pallas_call(
        _positions_kernel,
        grid=(t // tm,),
        in_specs=[pl.BlockSpec((N_EXPERTS, 1), lambda i: (0, 0)), pl.BlockSpec((N_EXPERTS, N_EXPERTS), lambda i: (0, 0)),
                  tok, tok],
        out_specs=tok,
        out_shape=jax.ShapeDtypeStruct((TOP_K, t), I32),
        compiler_params=_params(("arbitrary",)),
        name="positions",
    )(counts, lower, eidx, rank)


EXPERT_BLOCK = 512
TAIL_UNIT = 128
TAIL_PIECES = tuple(1 << i for i in reversed(range(EXPERT_BLOCK.bit_length() - 1)))
W_SLOTS = 3
X_SLOTS = 4
Y_SLOTS = 4


def _experts_kernel(start_ref, count_ref, nxt_ref, nxt2_ref, slot_ref, first_ref, blk0_ref, full0_ref, ptail_ref,
                    ltail_ref, blktok_ref, nblocks_ref, xs_hbm, wg_hbm, wu_hbm, wd_hbm, ys_hbm,
                    wg_buf, wu_buf, wd_buf, wg_bf, wu_bf, wd_bf, xbuf, ybuf, ytail, wsem, xsem, ysem, tsem):
    e = pl.program_id(0)
    last_step = e == pl.num_programs(0) - 1
    start, count = start_ref[e], count_ref[e]
    n_full = jnp.right_shift(count, EXPERT_BLOCK.bit_length() - 1)
    tail = jnp.bitwise_and(count, EXPERT_BLOCK - 1)
    n_blk = n_full + (tail > 0).astype(I32)
    blk0, full0 = blk0_ref[e], full0_ref[e]
    slot, nxt, nxt2 = slot_ref[e], nxt_ref[e], nxt2_ref[e]
    slot1 = jnp.where(slot + 1 >= W_SLOTS, slot + 1 - W_SLOTS, slot + 1)
    slot2 = jnp.where(slot + 2 >= W_SLOTS, slot + 2 - W_SLOTS, slot + 2)

    def token_rows(token, n):
        return pl.ds(pl.multiple_of(token * TOKEN_ROWS, TOKEN_ROWS), n * TOKEN_ROWS)

    def fetch(ex, s):
        return (pltpu.make_async_copy(wg_hbm.at[ex], wg_buf.at[s], wsem.at[s, 0]),
                pltpu.make_async_copy(wu_hbm.at[ex], wu_buf.at[s], wsem.at[s, 1]),
                pltpu.make_async_copy(wd_hbm.at[ex], wd_buf.at[s], wsem.at[s, 2]))

    def x_copy(token, s):
        return pltpu.make_async_copy(xs_hbm.at[token_rows(token, EXPERT_BLOCK)], xbuf.at[s], xsem.at[s])

    def y_copy(token, s):
        return pltpu.make_async_copy(ybuf.at[s], ys_hbm.at[token_rows(token, EXPERT_BLOCK)], ysem.at[s])

    def tail_copies(token, length):
        out = []
        for piece in TAIL_PIECES:
            bigger = (EXPERT_BLOCK - 1) & ~(2 * piece - 1)
            done = jnp.bitwise_and(length, bigger)
            cp = pltpu.make_async_copy(ytail.at[token_rows(done, piece)], ys_hbm.at[token_rows(token + done, piece)],
                                       tsem)
            out.append((jnp.bitwise_and(length, piece) != 0, cp))
        return out

    def block(s, rows=EXPERT_BLOCK):
        x = _from_token_tiles(xbuf.at[s], rows).astype(BF16)
        g = _dot(x, wg_bf[...])
        u = _dot(x, wu_bf[...])
        return _dot((_silu(g) * u).astype(BF16), wd_bf[...])

    @pl.when(count > 0)
    def _():
        @pl.when(first_ref[e] == 1)
        def _():
            for g in range(X_SLOTS - 1):
                @pl.when(g < nblocks_ref[0])
                def _(g=g):
                    x_copy(blktok_ref[g], g).start()
            for cp in fetch(e, slot):
                cp.start()

            @pl.when(nxt >= 0)
            def _():
                for cp in fetch(nxt, slot1):
                    cp.start()

        for cp in fetch(e, slot):
            cp.wait()

        @pl.when(nxt2 >= 0)
        def _():
            for cp in fetch(nxt2, slot2):
                cp.start()

        wg_bf[...] = wg_buf[slot].astype(BF16)
        wu_bf[...] = wu_buf[slot].astype(BF16)
        wd_bf[...] = wd_buf[slot].astype(BF16)

        def take_x(i):
            g = blk0 + i
            xs_slot = jnp.bitwise_and(g, X_SLOTS - 1)
            x_copy(start, xs_slot).wait()
            ahead = g + (X_SLOTS - 1)

            @pl.when(ahead < nblocks_ref[0])
            def _():
                x_copy(blktok_ref[ahead], jnp.bitwise_and(ahead, X_SLOTS - 1)).start()

            return xs_slot

        def full_block(i, carry):
            y = block(take_x(i))
            j = full0 + i
            ys_slot = jnp.bitwise_and(j, Y_SLOTS - 1)

            @pl.when(j >= Y_SLOTS)
            def _():
                y_copy(start, ys_slot).wait()

            _to_token_tiles(ybuf.at[ys_slot], y)
            y_copy(start + i * EXPERT_BLOCK, ys_slot).start()
            return carry

        lax.fori_loop(0, n_full, full_block, 0)

        @pl.when(tail > 0)
        def _():
            xs_slot = take_x(n_full)
            for pred, cp in tail_copies(start, ptail_ref[e]):
                @pl.when(pred)
                def _(cp=cp):
                    cp.wait()
            units = jnp.right_shift(tail + (TAIL_UNIT - 1), TAIL_UNIT.bit_length() - 1)
            for u in range(1, EXPERT_BLOCK // TAIL_UNIT + 1):
                @pl.when(units == u)
                def _(u=u):
                    rows = u * TAIL_UNIT
                    _to_token_tiles(ytail.at[pl.ds(0, rows * TOKEN_ROWS)], block(xs_slot, rows))
            for pred, cp in tail_copies(start + n_full * EXPERT_BLOCK, tail):
                @pl.when(pred)
                def _(cp=cp):
                    cp.start()

    @pl.when(last_step)
    def _():
        total_full = full0 + n_full
        for back in range(1, Y_SLOTS + 1):
            @pl.when(total_full >= back)
            def _(back=back):
                y_copy(0, jnp.bitwise_and(total_full - back, Y_SLOTS - 1)).wait()
        for pred, cp in tail_copies(0, ltail_ref[0]):
            @pl.when(pred)
            def _(cp=cp):
                cp.wait()


def _max_expert_blocks(n_rows):
    return n_rows // EXPERT_BLOCK + N_EXPERTS


def _expert_metadata(counts, n_rows):
    ids = jnp.arange(N_EXPERTS, dtype=I32)
    used = counts > 0
    starts = jnp.cumsum(counts) - counts
    n_blk = (counts + EXPERT_BLOCK - 1) // EXPERT_BLOCK
    n_full = counts // EXPERT_BLOCK
    tail = counts % EXPERT_BLOCK
    blk0 = jnp.cumsum(n_blk) - n_blk
    full0 = jnp.cumsum(n_full) - n_full
    next_used = lax.cummin(jnp.where(used, ids, N_EXPERTS), reverse=True)
    next_after = jnp.concatenate([next_used[1:], jnp.full((1,), N_EXPERTS, I32)])
    nxt = jnp.where(next_after < N_EXPERTS, next_after, -1)
    ordinal = jnp.cumsum(used.astype(I32)) - 1
    slot = ordinal % W_SLOTS
    first = jnp.logical_and(used, ordinal == 0)
    latest = lax.cummax(jnp.where(tail > 0, ids, -1))
    before = jnp.concatenate([jnp.full((1,), -1, I32), latest[:-1]])
    pick = lambda index, values: jnp.sum(jnp.where(index[:, None] == ids[None, :], values[None, :], 0), axis=1)
    nxt2 = jnp.where(nxt >= 0, pick(nxt, nxt + 1), 0) - 1
    ptail = pick(before, tail)
    ltail = pick(latest[-1:], tail)
    block_ends = jnp.cumsum(n_blk)
    g = jnp.arange(_max_expert_blocks(n_rows), dtype=I32)
    eg = jnp.sum((g[:, None] >= block_ends[None, :]).astype(I32), axis=1)
    blktok = g * EXPERT_BLOCK + pick(eg, starts - blk0 * EXPERT_BLOCK)
    return tuple(v.astype(I32) for v in (starts, counts, nxt, nxt2, slot, first, blk0, full0, ptail, ltail, blktok,
                                         block_ends[-1:]))


def _experts(xs, counts, w_gate, w_up, w_down):
    d = D_MODEL
    meta = _expert_metadata(counts, xs.shape[0] // TOKEN_ROWS - EXPERT_BLOCK)
    hbm = pl.BlockSpec(memory_space=pl.ANY)
    blk = (EXPERT_BLOCK * TOKEN_ROWS, LANES)
    grid_spec = pltpu.PrefetchScalarGridSpec(
        num_scalar_prefetch=len(meta),
        grid=(N_EXPERTS,),
        in_specs=[hbm, hbm, hbm, hbm],
        out_specs=hbm,
        scratch_shapes=[pltpu.VMEM((W_SLOTS, d, EXPERT_FF), F32), pltpu.VMEM((W_SLOTS, d, EXPERT_FF), F32),
                        pltpu.VMEM((W_SLOTS, EXPERT_FF, d), F32),
                        pltpu.VMEM((d, EXPERT_FF), BF16), pltpu.VMEM((d, EXPERT_FF), BF16),
                        pltpu.VMEM((EXPERT_FF, d), BF16),
                        pltpu.VMEM((X_SLOTS,) + blk, WORD), pltpu.VMEM((Y_SLOTS,) + blk, WORD), pltpu.VMEM(blk, WORD),
                        pltpu.SemaphoreType.DMA((W_SLOTS, 3)), pltpu.SemaphoreType.DMA((X_SLOTS,)),
                        pltpu.SemaphoreType.DMA((Y_SLOTS,)), pltpu.SemaphoreType.DMA(())],
    )
    return pl.pallas_call(
        _experts_kernel,
        grid_spec=grid_spec,
        out_shape=jax.ShapeDtypeStruct(xs.shape, WORD),
        compiler_params=_params(("arbitrary",)),
        name="experts",
    )(*meta, xs, w_gate, w_up, w_down)


SC_CORES = 2
SC_SUBCORES = 16
SC_WORKERS = SC_CORES * SC_SUBCORES
SC_CHUNK = 128
SC_RING = 4


def _sc_scatter_rows(planes, idx, n_out_rows):
    n_planes, t, _ = planes.shape
    ranges = SC_WORKERS // n_planes
    n_chunks = t // ranges // SC_CHUNK
    idx = idx.reshape(TOP_K, n_planes, ranges, n_chunks, SC_CHUNK)
    mesh = plsc.VectorSubcoreMesh(core_axis_name="c", subcore_axis_name="s", num_cores=SC_CORES,
                                  num_subcores=SC_SUBCORES)

    def body(planes_hbm, idx_hbm, out_hbm, idx_v, rows_v, lsem, ssem):
        wid = lax.axis_index("s") * SC_CORES + lax.axis_index("c")
        plane = wid % n_planes
        token0 = (wid // n_planes) * (n_chunks * SC_CHUNK)
        for k in range(TOP_K):
            pltpu.sync_copy(idx_hbm.at[k, plane, wid // n_planes], idx_v.at[k])

        def load(slot, c):
            return pltpu.make_async_copy(planes_hbm.at[plane, pl.ds(token0 + c * SC_CHUNK, SC_CHUNK)], rows_v.at[slot],
                                         lsem.at[slot])

        def scatter(slot, c, k):
            return pltpu.make_async_copy(rows_v.at[slot], out_hbm.at[idx_v.at[k, c]], ssem.at[slot])

        @pl.loop(0, n_chunks, step=SC_RING)
        def _(g):
            for slot in range(SC_RING):
                @pl.when(g > 0)
                def _(slot=slot):
                    for k in range(TOP_K):
                        scatter(slot, 0, k).wait()
                load(slot, g + slot).start()
            for slot in range(SC_RING):
                load(slot, g + slot).wait()
                for k in range(TOP_K):
                    scatter(slot, g + slot, k).start()

        for slot in range(SC_RING):
            for k in range(TOP_K):
                scatter(slot, 0, k).wait()

    return pl.kernel(
        body, mesh=mesh,
        out_type=jax.ShapeDtypeStruct((n_out_rows, LANES), planes.dtype),
        scratch_types=[pltpu.VMEM((TOP_K, n_chunks, SC_CHUNK), I32), pltpu.VMEM((SC_RING, SC_CHUNK, LANES), planes.dtype),
                       pltpu.SemaphoreType.DMA((SC_RING,)), pltpu.SemaphoreType.DMA((SC_RING,))],
        name="sc_scatter",
    )(planes, idx)


def _sc_gather_rows(table, idx):
    n_workers, n_chunks, chunk = idx.shape
    rows_per_worker = n_chunks * chunk
    mesh = plsc.VectorSubcoreMesh(core_axis_name="c", subcore_axis_name="s", num_cores=SC_CORES,
                                  num_subcores=SC_SUBCORES)

    def body(table_hbm, idx_hbm, out_hbm, idx_v, rows_v, gsem, wsem):
        wid = lax.axis_index("s") * SC_CORES + lax.axis_index("c")
        base = wid * rows_per_worker
        pltpu.sync_copy(idx_hbm.at[wid], idx_v)

        def write(slot, c):
            return pltpu.make_async_copy(rows_v.at[slot], out_hbm.at[pl.ds(base + c * chunk, chunk)], wsem.at[slot])

        def gather(slot, c):
            return pltpu.make_async_copy(table_hbm.at[idx_v.at[c]], rows_v.at[slot], gsem.at[slot])

        @pl.loop(0, n_chunks, step=SC_RING)
        def _(g):
            for slot in range(SC_RING):
                @pl.when(g > 0)
                def _(slot=slot):
                    write(slot, 0).wait()
                gather(slot, g + slot).start()
            for slot in range(SC_RING):
                gather(slot, g + slot).wait()
                write(slot, g + slot).start()

        for slot in range(SC_RING):
            write(slot, 0).wait()

    return pl.kernel(
        body, mesh=mesh,
        out_type=jax.ShapeDtypeStruct((n_workers * rows_per_worker, LANES), table.dtype),
        scratch_types=[pltpu.VMEM((n_chunks, chunk), I32), pltpu.VMEM((SC_RING, chunk, LANES), table.dtype),
                       pltpu.SemaphoreType.DMA((SC_RING,)), pltpu.SemaphoreType.DMA((SC_RING,))],
        name="sc_gather",
    )(table, idx)


def _unpack_planes(ref, first):
    words = [ref[first + c] for c in range(TOKEN_ROWS)]
    lows = [lax.bitcast_convert_type(jnp.left_shift(w, 16), F32) for w in words]
    highs = [lax.bitcast_convert_type(jnp.bitwise_and(w, HIGH_HALF), F32) for w in words]
    return jnp.concatenate(lows + highs, axis=1)


def _shared_kernel(hp_ref, wg_ref, wu_ref, wd_ref, out_ref):
    hb = _unpack_planes(hp_ref, 0).astype(BF16)
    out_ref[...] = _dot((_silu(_dot(hb, wg_ref[...])) * _dot(hb, wu_ref[...])).astype(BF16), wd_ref[...])


def _shared_expert(h2p, w_gate_s, w_up_s, w_down_s, tm=512):
    t = h2p.shape[1]
    d = D_MODEL
    full = lambda shp: pl.BlockSpec(shp, lambda i: (0,) * len(shp))
    return pl.pallas_call(
        _shared_kernel,
        grid=(t // tm,),
        in_specs=[pl.BlockSpec((TOKEN_ROWS, tm, LANES), lambda i: (0, i, 0)),
                  full((d, EXPERT_FF)), full((d, EXPERT_FF)), full((EXPERT_FF, d))],
        out_specs=pl.BlockSpec((tm, d), lambda i: (i, 0)),
        out_shape=jax.ShapeDtypeStruct((t, d), F32),
        compiler_params=_params(("arbitrary",)),
        name="shared_expert",
    )(h2p, w_gate_s.astype(BF16), w_up_s.astype(BF16), w_down_s.astype(BF16))


def _combine_kernel(rows_ref, gates_ref, shared_ref, x1_ref, gatef_ref, out_ref):
    gates = gates_ref[...]
    routed = _unpack_planes(rows_ref, 0) * gates[:, 0:1]
    for k in range(1, TOP_K):
        routed = routed + _unpack_planes(rows_ref, k * TOKEN_ROWS) * gates[:, k:k + 1]
    out_ref[...] = x1_ref[...] + gatef_ref[...] * (shared_ref[...] + routed)


def _combine(gathered, gates_t, shared, x1, gate_f, b, s, tm):
    t, d = x1.shape
    tiles_per_seq = s // tm
    row = lambda n: pl.BlockSpec((tm, n), lambda i: (i, 0))
    return pl.pallas_call(
        _combine_kernel,
        grid=(t // tm,),
        in_specs=[pl.BlockSpec((TOP_K * TOKEN_ROWS, tm, LANES), lambda i: (0, i, 0)),
                  row(TOP_K), row(d), row(d),
                  pl.BlockSpec((None, 1, d), lambda i: (i // tiles_per_seq, 0, 0))],
        out_specs=row(d),
        out_shape=jax.ShapeDtypeStruct((t, d), F32),
        compiler_params=_params(("arbitrary",)),
        name="combine",
    )(gathered, gates_t, shared, x1, gate_f.reshape(b, 1, d))


def _moe_sublayer(x1, h2, h2t, gate_f, w_router, router_bias, w_gate, w_up, w_down, w_gate_s, w_up_s, w_down_s, b, s,
                  tm=256):
    t = b * s
    eidx, rank, gates, counts = _router(h2, w_router, router_bias)
    pos = _positions(counts, eidx, rank, 4 * tm)
    idx = pos[:, None, :] * TOKEN_ROWS + jnp.arange(TOKEN_ROWS, dtype=I32)[None, :, None]
    xs = _sc_scatter_rows(h2t, idx, (t * TOP_K + EXPERT_BLOCK) * TOKEN_ROWS)
    ys = _experts(xs, counts[:, 0], w_gate, w_up, w_down)
    gathered = _sc_gather_rows(ys, idx.reshape(SC_WORKERS, -1, SC_CHUNK)).reshape(TOP_K * TOKEN_ROWS, t, LANES)
    shared = _shared_expert(h2t, w_gate_s, w_up_s, w_down_s)
    return _combine(gathered, gates.T, shared, x1, gate_f, b, s, 2 * tm)


def kernel(x, c, w_ada, b_ada, norm_mix_gain, w_in, q_norm_gain, k_norm_gain, rel_bias_table, conv_w, conv_b, dt_bias,
           a_log, d_skip, ssm_norm_gain, w_out, norm_ffn_gain, w_router, router_bias, w_gate_experts, w_up_experts,
           w_down_experts, w_gate_shared, w_up_shared, w_down_shared):
    b, s, d = x.shape
    for layer in range(w_ada.shape[0]):
        mod = _adaln(c, w_ada[layer], b_ada[layer])
        x1, h2, h2t = _mixer_sublayer(x, mod, norm_mix_gain[layer], w_in[layer], q_norm_gain[layer], k_norm_gain[layer],
                                 rel_bias_table, conv_w[layer], conv_b[layer], dt_bias[layer], a_log[layer],
                                 d_skip[layer], ssm_norm_gain[layer], w_out[layer], norm_ffn_gain[layer])
        gate_f = mod[:, 5 * d:]
        out = _moe_sublayer(x1, h2, h2t, gate_f, w_router[layer], router_bias[layer], w_gate_experts[layer],
                            w_up_experts[layer], w_down_experts[layer], w_gate_shared[layer], w_up_shared[layer],
                            w_down_shared[layer], b, s)
        x = out.reshape(b, s, d)
    return x
```
